```python
import math
import jax, jax.numpy as jnp
from jax import lax
import numpy as np

D_MODEL = 1024
BATCH = 4
SEQ = 4096
DEPTH = 1

DA_HEADS = 4
DA_HALF = 64
DA_VDIM = 2 * DA_HALF
DA_WIDTH = DA_HEADS * DA_VDIM
ROPE_THETA = 500000.0
ROPE_DIM = DA_HALF // 4
QBLK = 128
RET_HEADS = 4
RET_KDIM = 128
RET_VDIM = 128
RET_WIDTH = RET_HEADS * RET_VDIM
RET_THETA = 10000.0
RCHUNK = 128
SPLIT_SIZES = [DA_WIDTH, DA_WIDTH, DA_WIDTH,
               RET_HEADS * RET_KDIM, RET_HEADS * RET_KDIM, RET_WIDTH, RET_WIDTH,
               D_MODEL, D_MODEL]
IN_COLS = sum(SPLIT_SIZES)
N_GROUPS = 4
EXPERTS_PER_GROUP = 8
N_EXPERTS = N_GROUPS * EXPERTS_PER_GROUP
TOP_K = 2
EXPERT_FF = 512
MOE_BLOCK = 128
EPS = 1e-6

kernel_name = "hybrid_diffattn_retention_hiermoe"


def rms_norm(x, g):
    xf = x.astype(jnp.float32)
    y = xf * lax.rsqrt(jnp.mean(xf * xf, axis=-1, keepdims=True) + EPS)
    return (y * g.astype(jnp.float32)).astype(x.dtype)


def apply_rope(x, positions, rot_dim, theta):
    half = rot_dim // 2
    freqs = jnp.power(jnp.float32(theta), -2.0 * jnp.arange(half, dtype=jnp.float32) / rot_dim)
    ang = positions.astype(jnp.float32)[..., None] * freqs
    ang = ang.reshape(ang.shape[:2] + (1,) * (x.ndim - 3) + (half,))
    cos, sin = jnp.cos(ang), jnp.sin(ang)
    xf = x.astype(jnp.float32)
    x1, x2, rest = xf[..., :half], xf[..., half:rot_dim], xf[..., rot_dim:]
    out = jnp.concatenate([x1 * cos - x2 * sin, x2 * cos + x1 * sin, rest], axis=-1)
    return out.astype(x.dtype)


def diff_attention(q, k, v, lam):
    B, S, H, _, d = q.shape
    nb = S // QBLK
    scale = 1.0 / math.sqrt(d)
    qb = q.reshape(B, nb, QBLK, H, 2, d).transpose(1, 0, 2, 3, 4, 5)
    kpos = jnp.arange(S)

    def one_block(args):
        qblk, i = args
        s = jnp.einsum('bqhcd,bkhcd->bhcqk', qblk, k).astype(jnp.float32) * scale
        qpos = i * QBLK + jnp.arange(QBLK)
        mask = kpos[None, :] <= qpos[:, None]
        p = jax.nn.softmax(jnp.where(mask, s, -jnp.inf), axis=-1)
        a = p[:, :, 0] - lam * p[:, :, 1]
        return jnp.einsum('bhqk,bkhe->bqhe', a.astype(v.dtype), v)

    o = lax.map(one_block, (qb, jnp.arange(nb)))
    return o.transpose(1, 0, 2, 3, 4).reshape(B, S, H, v.shape[-1])


def retention(q, k, v):
    B, S, H, dk = q.shape
    dv = v.shape[-1]
    C = RCHUNK
    nc = S // C
    log_g = jnp.log1p(-jnp.exp2(-5.0 - jnp.arange(H, dtype=jnp.float32)))
    idx = jnp.arange(C, dtype=jnp.float32)
    rel = idx[:, None] - idx[None, :]
    dmask = jnp.where(rel[None] >= 0, jnp.exp(jnp.maximum(rel, 0.0)[None] * log_g[:, None, None]), 0.0)
    zeta = jnp.exp((C - 1 - idx)[None, :] * log_g[:, None])
    xi = jnp.exp((idx + 1.0)[None, :] * log_g[:, None])
    g_chunk = jnp.exp(C * log_g)
    qc = q.reshape(B, nc, C, H, dk)
    kc = k.reshape(B, nc, C, H, dk)
    vc = v.reshape(B, nc, C, H, dv)
    s = jnp.einsum('bnihd,bnjhd->bnhij', qc, kc) * dmask
    o_intra = jnp.einsum('bnhij,bnjhe->bnihe', s, vc)
    kv = jnp.einsum('bnjhd,hj,bnjhe->nbhde', kc, zeta, vc)

    def step(r, kv_n):
        return g_chunk[None, :, None, None] * r + kv_n, r

    _, r_prev = lax.scan(step, jnp.zeros((B, H, dk, dv), jnp.float32), kv)
    o_cross = jnp.einsum('bnihd,hi,nbhde->bnihe', qc, xi, r_prev)
    return (o_intra + o_cross).reshape(B, S, H, dv)


def hier_moe(h, w_gr, b_gr, w_er, b_er, w_gate, w_up, w_down):
    B, S, D = h.shape
    N = B * S
    hf = h.reshape(N, D)
    gl = (hf @ w_gr + b_gr).astype(jnp.float32)
    g_idx = jnp.argmax(gl, axis=-1)
    p_g = jnp.take_along_axis(jax.nn.softmax(gl, axis=-1), g_idx[:, None], axis=1)[:, 0]
    el = (hf @ w_er + b_er).astype(jnp.float32).reshape(N, N_GROUPS, EXPERTS_PER_GROUP)
    el_sel = jnp.take_along_axis(el, g_idx[:, None, None], axis=1)[:, 0]
    top_v, top_i = lax.top_k(el_sel, TOP_K)
    w = jax.nn.softmax(top_v, axis=-1) * p_g[:, None]
    e_flat = (g_idx[:, None] * EXPERTS_PER_GROUP + top_i).reshape(-1).astype(jnp.int32)
    w_flat = w.reshape(-1)
    NK = N * TOP_K
    order = jnp.argsort(e_flat)
    e_s = e_flat[order]
    t_s = (order // TOP_K).astype(jnp.int32)
    w_s = w_flat[order]
    counts = jnp.zeros((N_EXPERTS,), jnp.int32).at[e_flat].add(1)
    padded = ((counts + MOE_BLOCK - 1) // MOE_BLOCK) * MOE_BLOCK
    pad_end = jnp.cumsum(padded)
    pad_start = pad_end - padded
    start = jnp.cumsum(counts) - counts
    dest = pad_start[e_s] + jnp.arange(NK, dtype=jnp.int32) - start[e_s]
    P = NK + N_EXPERTS * MOE_BLOCK
    nblk = P // MOE_BLOCK
    row_tok = jnp.zeros((P,), jnp.int32).at[dest].set(t_s)
    row_w = jnp.zeros((P,), jnp.float32).at[dest].set(w_s)
    blk_expert = jnp.clip(jnp.searchsorted(pad_end, jnp.arange(nblk, dtype=jnp.int32) * MOE_BLOCK, side='right'),
                          0, N_EXPERTS - 1)
    xs = hf[row_tok].reshape(nblk, MOE_BLOCK, D)

    def expert_block(args):
        xb, eid = args
        a = xb @ w_gate[eid]
        u = xb @ w_up[eid]
        return (jax.nn.silu(a) * u) @ w_down[eid]

    ys = lax.map(expert_block, (xs, blk_expert)).reshape(P, D)
    out = jnp.zeros((N, D), h.dtype).at[row_tok].add(ys * row_w[:, None].astype(h.dtype))
    return out.reshape(B, S, D)


def setup_inputs(seed: int = 0) -> dict:
    key = jax.random.key(seed)
    ks = jax.random.split(key, 24)
    nrm = lambda k, shape, fan: jax.random.normal(k, shape, jnp.float32) * fan ** -0.5
    gain = lambda k, shape: 1.0 + 0.02 * jax.random.normal(k, shape, jnp.float32)
    L = DEPTH
    x = jax.random.normal(ks[0], (BATCH, SEQ, D_MODEL), jnp.float32)
    offsets = jax.random.randint(ks[1], (BATCH, 1), 0, 1024, jnp.int32)
    positions = (offsets + jnp.arange(SEQ, dtype=jnp.int32)[None, :]).astype(jnp.int32)
    return {
        "x": x,
        "positions": positions,
        "norm1_g": gain(ks[2], (L, D_MODEL)),
        "w_in": nrm(ks[3], (L, D_MODEL, IN_COLS), D_MODEL),
        "q_norm_g": gain(ks[4], (L, DA_HALF)),
        "k_norm_g": gain(ks[5], (L, DA_HALF)),
        "lambda_q1": 0.1 * jax.random.normal(ks[6], (L, DA_HALF), jnp.float32),
        "lambda_k1": 0.1 * jax.random.normal(ks[7], (L, DA_HALF), jnp.float32),
        "lambda_q2": 0.1 * jax.random.normal(ks[8], (L, DA_HALF), jnp.float32),
        "lambda_k2": 0.1 * jax.random.normal(ks[9], (L, DA_HALF), jnp.float32),
        "diff_subln_g": gain(ks[10], (L, DA_VDIM)),
        "ret_gn_g": gain(ks[11], (L, RET_WIDTH)),
        "ret_gn_b": 0.02 * jax.random.normal(ks[12], (L, RET_WIDTH), jnp.float32),
        "w_branch_a": nrm(ks[13], (L, DA_WIDTH, D_MODEL), DA_WIDTH),
        "w_branch_b": nrm(ks[14], (L, RET_WIDTH, D_MODEL), RET_WIDTH),
        "w_out": nrm(ks[15], (L, D_MODEL, D_MODEL), D_MODEL),
        "norm2_g": gain(ks[16], (L, D_MODEL)),
        "w_group_router": nrm(ks[17], (L, D_MODEL, N_GROUPS), D_MODEL),
        "b_group_router": 0.01 * jax.random.normal(ks[18], (L, N_GROUPS), jnp.float32),
        "w_expert_router": nrm(ks[19], (L, D_MODEL, N_EXPERTS), D_MODEL),
        "b_expert_router": 0.01 * jax.random.normal(ks[20], (L, N_EXPERTS), jnp.float32),
        "w_gate": nrm(ks[21], (L, N_EXPERTS, D_MODEL, EXPERT_FF), D_MODEL),
        "w_up": nrm(ks[22], (L, N_EXPERTS, D_MODEL, EXPERT_FF), D_MODEL),
        "w_down": nrm(ks[23], (L, N_EXPERTS, EXPERT_FF, D_MODEL), EXPERT_FF),
    }


def reference(x, positions, norm1_g, w_in, q_norm_g, k_norm_g, lambda_q1, lambda_k1, lambda_q2, lambda_k2,
              diff_subln_g, ret_gn_g, ret_gn_b, w_branch_a, w_branch_b, w_out, norm2_g,
              w_group_router, b_group_router, w_expert_router, b_expert_router, w_gate, w_up, w_down):
    B, S, D = x.shape
    split_idx = np.cumsum(SPLIT_SIZES)[:-1].tolist()
    for l in range(DEPTH):
        lam_init = 0.8 - 0.6 * math.exp(-0.3 * l)
        h = rms_norm(x, norm1_g[l])
        qa, ka, va, qb, kb, vb, gb, gate_a, gate_b = jnp.split(h @ w_in[l], split_idx, axis=-1)
        qa = apply_rope(rms_norm(qa.reshape(B, S, DA_HEADS, 2, DA_HALF), q_norm_g[l]), positions, ROPE_DIM, ROPE_THETA)
        ka = apply_rope(rms_norm(ka.reshape(B, S, DA_HEADS, 2, DA_HALF), k_norm_g[l]), positions, ROPE_DIM, ROPE_THETA)
        va = va.reshape(B, S, DA_HEADS, DA_VDIM)
        lam = (jnp.exp(jnp.sum(lambda_q1[l].astype(jnp.float32) * lambda_k1[l].astype(jnp.float32)))
               - jnp.exp(jnp.sum(lambda_q2[l].astype(jnp.float32) * lambda_k2[l].astype(jnp.float32)))
               + lam_init)
        oa = rms_norm(diff_attention(qa, ka, va, lam), diff_subln_g[l]) * (1.0 - lam_init)
        ya = oa.reshape(B, S, DA_WIDTH) @ w_branch_a[l]
        qr = apply_rope(qb.reshape(B, S, RET_HEADS, RET_KDIM), positions, RET_KDIM, RET_THETA).astype(jnp.float32)
        kr = apply_rope(kb.reshape(B, S, RET_HEADS, RET_KDIM), positions, RET_KDIM, RET_THETA).astype(jnp.float32) * RET_KDIM ** -0.5
        vr = vb.reshape(B, S, RET_HEADS, RET_VDIM).astype(jnp.float32)
        ob = retention(qr, kr, vr)
        mu = jnp.mean(ob, axis=-1, keepdims=True)
        var = jnp.mean(jnp.square(ob - mu), axis=-1, keepdims=True)
        ob = ((ob - mu) * lax.rsqrt(var + EPS)).reshape(B, S, RET_WIDTH)
        ob = (ob * ret_gn_g[l].astype(jnp.float32) + ret_gn_b[l].astype(jnp.float32)).astype(x.dtype)
        yb = (jax.nn.silu(gb) * ob) @ w_branch_b[l]
        merged = jax.nn.sigmoid(gate_a) * ya + jax.nn.sigmoid(gate_b) * yb
        x = x + merged @ w_out[l]
        h2 = rms_norm(x, norm2_g[l])
        x = x + hier_moe(h2, w_group_router[l], b_group_router[l], w_expert_router[l], b_expert_router[l],
                         w_gate[l], w_up[l], w_down[l])
    return x
```

```python
import functools
import math

import jax
import jax.numpy as jnp
from jax import lax
from jax.experimental import pallas as pl
from jax.experimental.pallas import tpu as pltpu

D_MODEL = 1024
DA_HEADS = 4
DA_HALF = 64
DA_VDIM = 2 * DA_HALF
DA_WIDTH = DA_HEADS * DA_VDIM
ROPE_THETA = 500000.0
ROPE_DIM = DA_HALF // 4
RET_HEADS = 4
RET_KDIM = 128
RET_VDIM = 128
RET_WIDTH = RET_HEADS * RET_VDIM
RET_THETA = 10000.0
N_GROUPS = 4
EXPERTS_PER_GROUP = 8
N_EXPERTS = N_GROUPS * EXPERTS_PER_GROUP
TOP_K = 2
EXPERT_FF = 512
EPS = 1e-6
LAMBDA_INIT = 0.8 - 0.6 * math.exp(-0.3 * 0)

LANES = 128
IN_COLS = 3 * DA_WIDTH + 4 * RET_WIDTH + 2 * D_MODEL
COL_QA, COL_KA, COL_VA = 0, DA_WIDTH, 2 * DA_WIDTH
COL_QR = 3 * DA_WIDTH
COL_KR = COL_QR + RET_WIDTH
COL_VR = COL_KR + RET_WIDTH
COL_GB = COL_VR + RET_WIDTH
COL_GATE_A = COL_GB + RET_WIDTH
COL_GATE_B = COL_GATE_A + D_MODEL

PROJ_ROWS = 512
PROJ_CHUNK = 256
ATT_TQ = 256
RET_CHUNK = 256
MOE_BLOCK = 256
PERM_ROWS = 1024
VMEM_LIMIT = 56 * 1024 * 1024


def _dot(a, b):
    return jnp.dot(a, b, preferred_element_type=jnp.float32)


def _dot_nt(a, b):
    return lax.dot_general(a, b, (((1,), (1,)), ((), ())), preferred_element_type=jnp.float32)


def _dot_tn(a, b):
    return lax.dot_general(a, b, (((0,), (0,)), ((), ())), preferred_element_type=jnp.float32)


def _sigmoid(x):
    return 1.0 / (1.0 + jnp.exp(-x))


def _in_proj_kernel(x_ref, pos_ref, g1_ref, w_ref, gsum_ref, gq_ref, gk_ref, fa_ref, fr_ref,
                    o_ref, h_scr):
    x = x_ref[...]
    h = x * lax.rsqrt(jnp.mean(x * x, axis=-1, keepdims=True) + EPS) * g1_ref[...]
    h_scr[...] = h.astype(jnp.bfloat16)
    rows = x.shape[0]
    pos = pos_ref[...].astype(jnp.float32)

    lane = lax.broadcasted_iota(jnp.int32, (rows, LANES), 1)
    ang_a = pos * fa_ref[...]
    cos_a, sin_a = jnp.cos(ang_a), jnp.sin(ang_a)
    lane64 = lane % DA_HALF
    half_a = ROPE_DIM // 2
    c_a = jnp.where(lane64 < ROPE_DIM, cos_a, 1.0)
    s_lo = jnp.where(lane64 < half_a, -sin_a, 0.0)
    s_hi = jnp.where((lane64 >= half_a) & (lane64 < ROPE_DIM), sin_a, 0.0)
    c_a2 = jnp.concatenate([c_a, c_a], axis=1)
    s_lo2 = jnp.concatenate([s_lo, s_lo], axis=1)
    s_hi2 = jnp.concatenate([s_hi, s_hi], axis=1)
    ang_r = pos * fr_ref[...]
    cos_r, sin_r = jnp.cos(ang_r), jnp.sin(ang_r)
    s_r = jnp.where(lane < RET_KDIM // 2, -sin_r, sin_r)
    c_r2 = jnp.concatenate([cos_r, cos_r], axis=1)
    s_r2 = jnp.concatenate([s_r, s_r], axis=1)

    def qk_norm_rope(y, g, scale):
        ss = y * y
        hi = ss.astype(jnp.bfloat16)
        lo = (ss - hi.astype(jnp.float32)).astype(jnp.bfloat16)
        gs = _dot(hi, gsum_ref[...]) + _dot(lo, gsum_ref[...])
        n = y * lax.rsqrt(gs * (1.0 / DA_HALF) + EPS) * g
        up = pltpu.roll(n, PROJ_CHUNK - half_a, axis=1)
        dn = pltpu.roll(n, half_a, axis=1)
        r = n * c_a2 + up * s_lo2 + dn * s_hi2
        return r * scale if scale != 1.0 else r

    def ret_rope(y, scale):
        halves = [pltpu.roll(y[:, i * LANES:(i + 1) * LANES], RET_KDIM // 2, axis=1)
                  for i in range(PROJ_CHUNK // LANES)]
        sw = jnp.concatenate(halves, axis=1)
        r = y * c_r2 + sw * s_r2
        return r * scale if scale != 1.0 else r

    for c in range(IN_COLS // PROJ_CHUNK):
        c0 = c * PROJ_CHUNK
        y = _dot(h_scr[...], w_ref[:, c0:c0 + PROJ_CHUNK])
        if c0 < COL_KA:
            y = qk_norm_rope(y, gq_ref[...], DA_HALF ** -0.5)
        elif c0 < COL_VA:
            y = qk_norm_rope(y, gk_ref[...], 1.0)
        elif c0 < COL_QR:
            pass
        elif c0 < COL_KR:
            y = ret_rope(y, 1.0)
        elif c0 < COL_VR:
            y = ret_rope(y, RET_KDIM ** -0.5)
        elif c0 < COL_GB:
            pass
        elif c0 < COL_GATE_A:
            y = y * _sigmoid(y)
        else:
            y = _sigmoid(y)
        o_ref[:, c0:c0 + PROJ_CHUNK] = y.astype(o_ref.dtype)


def _in_proj(x2, pos2, g1, w_in, gq, gk):
    n = x2.shape[0]
    tm = min(PROJ_ROWS, n)
    grp = jnp.arange(PROJ_CHUNK) // DA_HALF
    gsum = (grp[:, None] == grp[None, :]).astype(jnp.bfloat16)
    half_a = ROPE_DIM // 2
    freq_a = jnp.power(jnp.float32(ROPE_THETA), -2.0 * jnp.arange(half_a, dtype=jnp.float32) / ROPE_DIM)
    fa = freq_a[jnp.arange(LANES) % half_a][None, :]
    half_r = RET_KDIM // 2
    freq_r = jnp.power(jnp.float32(RET_THETA), -2.0 * jnp.arange(half_r, dtype=jnp.float32) / RET_KDIM)
    fr = freq_r[jnp.arange(LANES) % half_r][None, :]
    reps = PROJ_CHUNK // DA_HALF
    full = lambda shape: pl.BlockSpec(shape, lambda i: (0,) * len(shape))
    return pl.pallas_call(
        _in_proj_kernel,
        grid=(n // tm,),
        in_specs=[
            pl.BlockSpec((tm, D_MODEL), lambda i: (i, 0)),
            pl.BlockSpec((tm, 1), lambda i: (i, 0)),
            full((1, D_MODEL)),
            full((D_MODEL, IN_COLS)),
            full((PROJ_CHUNK, PROJ_CHUNK)),
            full((1, PROJ_CHUNK)),
            full((1, PROJ_CHUNK)),
            full((1, LANES)),
            full((1, LANES)),
        ],
        out_specs=pl.BlockSpec((tm, IN_COLS), lambda i: (i, 0)),
        out_shape=jax.ShapeDtypeStruct((n, IN_COLS), jnp.bfloat16),
        scratch_shapes=[pltpu.VMEM((tm, D_MODEL), jnp.bfloat16)],
        compiler_params=pltpu.CompilerParams(dimension_semantics=("arbitrary",),
                                             vmem_limit_bytes=VMEM_LIMIT),
        name="in_proj",
    )(x2, pos2, g1.reshape(1, D_MODEL), w_in.astype(jnp.bfloat16), gsum,
      jnp.tile(gq, reps)[None, :], jnp.tile(gk, reps)[None, :], fa, fr)


def _diff_attn_kernel(q_ref, k_ref, v_ref, lam_ref, gsub_ref, o_ref, m_scr, l_scr, acc_scr):
    i = pl.program_id(2)
    tq = q_ref.shape[0]
    q = q_ref[...]
    lane = lax.broadcasted_iota(jnp.int32, q.shape, 1)
    zero = jnp.zeros_like(q)
    qs = jnp.concatenate([jnp.where(lane < DA_HALF, q, zero), jnp.where(lane >= DA_HALF, q, zero)], axis=0)
    m_scr[...] = jnp.full(m_scr.shape, -jnp.inf, jnp.float32)
    l_scr[...] = jnp.zeros(l_scr.shape, jnp.float32)
    acc_scr[...] = jnp.zeros(acc_scr.shape, jnp.float32)

    def step(j, masked):
        start = pl.multiple_of(j * tq, tq)
        k = k_ref[pl.ds(start, tq), :]
        v = v_ref[pl.ds(start, tq), :]
        s = _dot_nt(qs, k)
        if masked:
            r = lax.broadcasted_iota(jnp.int32, s.shape, 0) % tq
            c = lax.broadcasted_iota(jnp.int32, s.shape, 1)
            s = jnp.where(c <= r, s, -jnp.inf)
        m_old = m_scr[...]
        m_new = jnp.maximum(m_old, jnp.max(s, axis=-1, keepdims=True))
        alpha = jnp.exp(m_old - m_new)
        p = jnp.exp(s - m_new)
        l_scr[...] = alpha * l_scr[...] + jnp.sum(p, axis=-1, keepdims=True)
        acc_scr[...] = alpha * acc_scr[...] + _dot(p.astype(jnp.bfloat16), v)
        m_scr[...] = m_new

    def body(j, carry):
        step(j, False)
        return carry

    lax.fori_loop(0, i, body, 0)
    step(i, True)

    lam4 = lam_ref[...]
    lam = (jnp.exp(jnp.sum(lam4[0:1] * lam4[1:2], axis=-1, keepdims=True))
           - jnp.exp(jnp.sum(lam4[2:3] * lam4[3:4], axis=-1, keepdims=True)) + LAMBDA_INIT)
    o_all = acc_scr[...] / l_scr[...]
    o = o_all[:tq] - lam * o_all[tq:]
    o = o * lax.rsqrt(jnp.mean(o * o, axis=-1, keepdims=True) + EPS) * gsub_ref[...] * (1.0 - LAMBDA_INIT)
    o_ref[...] = o.astype(o_ref.dtype)


def _diff_attn(proj, lam4, gsub, batch, seq):
    n = proj.shape[0]
    tq = min(ATT_TQ, seq)
    nq = seq // tq
    qb, kb, vb = COL_QA // LANES, COL_KA // LANES, COL_VA // LANES
    return pl.pallas_call(
        _diff_attn_kernel,
        grid=(batch, DA_HEADS, nq),
        in_specs=[
            pl.BlockSpec((tq, LANES), lambda b, h, i: (b * nq + i, qb + h)),
            pl.BlockSpec((seq, LANES), lambda b, h, i: (b, kb + h)),
            pl.BlockSpec((seq, LANES), lambda b, h, i: (b, vb + h)),
            pl.BlockSpec((4, LANES), lambda b, h, i: (0, 0)),
            pl.BlockSpec((1, LANES), lambda b, h, i: (0, 0)),
        ],
        out_specs=pl.BlockSpec((tq, LANES), lambda b, h, i: (b * nq + i, h)),
        out_shape=jax.ShapeDtypeStruct((n, DA_WIDTH), jnp.bfloat16),
        scratch_shapes=[pltpu.VMEM((2 * tq, 1), jnp.float32),
                        pltpu.VMEM((2 * tq, 1), jnp.float32),
                        pltpu.VMEM((2 * tq, DA_VDIM), jnp.float32)],
        compiler_params=pltpu.CompilerParams(dimension_semantics=("arbitrary",) * 3,
                                             vmem_limit_bytes=VMEM_LIMIT),
        name="diff_attn",
    )(proj, proj, proj, lam4, gsub)


def _retention_kernel(q_ref, k_ref, v_ref, g_ref, gng_ref, gnb_ref, o_ref, r_scr, *, chunk):
    hf = jnp.full((1, 1), pl.program_id(1), jnp.int32).astype(jnp.float32)
    log_g = jnp.log1p(-jnp.exp2(-5.0 - hf))
    ri = lax.broadcasted_iota(jnp.int32, (chunk, chunk), 0)
    ci = lax.broadcasted_iota(jnp.int32, (chunk, chunk), 1)
    rel = (ri - ci).astype(jnp.float32)
    dmask = jnp.where(rel >= 0, jnp.exp(jnp.maximum(rel, 0.0) * log_g), 0.0)
    idx = lax.broadcasted_iota(jnp.int32, (chunk, 1), 0).astype(jnp.float32)
    zeta = jnp.exp((chunk - 1 - idx) * log_g)
    xi = jnp.exp((idx + 1.0) * log_g)
    g_chunk = jnp.exp(chunk * log_g)
    r_scr[...] = jnp.zeros(r_scr.shape, jnp.float32)
    gng = gng_ref[...]
    gnb = gnb_ref[...]

    def body(n, carry):
        start = pl.multiple_of(n * chunk, chunk)
        q = q_ref[pl.ds(start, chunk), :]
        k = k_ref[pl.ds(start, chunk), :]
        v = v_ref[pl.ds(start, chunk), :]
        s = _dot_nt(q, k) * dmask
        r_old = r_scr[...]
        o = _dot(s.astype(jnp.bfloat16), v) + xi * _dot(q, r_old.astype(jnp.bfloat16))
        kz = (k.astype(jnp.float32) * zeta).astype(jnp.bfloat16)
        r_scr[...] = g_chunk * r_old + _dot_tn(kz, v)
        mu = jnp.mean(o, axis=-1, keepdims=True)
        d = o - mu
        var = jnp.mean(d * d, axis=-1, keepdims=True)
        y = d * lax.rsqrt(var + EPS) * gng + gnb
        y = y * g_ref[pl.ds(start, chunk), :].astype(jnp.float32)
        o_ref[pl.ds(start, chunk), :] = y.astype(o_ref.dtype)
        return carry

    lax.fori_loop(0, q_ref.shape[0] // chunk, body, 0)


def _retention(proj, gn_g, gn_b, batch, seq):
    n = proj.shape[0]
    chunk = min(RET_CHUNK, seq)
    col = lambda c0: (lambda b, h: (b, c0 // LANES + h))
    return pl.pallas_call(
        functools.partial(_retention_kernel, chunk=chunk),
        grid=(batch, RET_HEADS),
        in_specs=[
            pl.BlockSpec((seq, LANES), col(COL_QR)),
            pl.BlockSpec((seq, LANES), col(COL_KR)),
            pl.BlockSpec((seq, LANES), col(COL_VR)),
            pl.BlockSpec((seq, LANES), col(COL_GB)),
            pl.BlockSpec((1, LANES), lambda b, h: (0, h)),
            pl.BlockSpec((1, LANES), lambda b, h: (0, h)),
        ],
        out_specs=pl.BlockSpec((seq, LANES), lambda b, h: (b, h)),
        out_shape=jax.ShapeDtypeStruct((n, RET_WIDTH), jnp.bfloat16),
        scratch_shapes=[pltpu.VMEM((RET_KDIM, RET_VDIM), jnp.float32)],
        compiler_params=pltpu.CompilerParams(dimension_semantics=("arbitrary",) * 2,
                                             vmem_limit_bytes=VMEM_LIMIT),
        name="retention",
    )(proj, proj, proj, proj, gn_g.reshape(1, RET_WIDTH), gn_b.reshape(1, RET_WIDTH))


def _merge_kernel(x_ref, oa_ref, ob_ref, sa0_ref, sa1_ref, sb0_ref, sb1_ref, wa_ref, wb_ref, wo_ref,
                  g2_ref, wr_hi_ref, wr_lo_ref, br_ref, x1_ref, h2_ref, route_ref):
    ya = _dot(oa_ref[...], wa_ref[...])
    yb = _dot(ob_ref[...], wb_ref[...])
    sa = jnp.concatenate([sa0_ref[...], sa1_ref[...]], axis=1).astype(jnp.float32)
    sb = jnp.concatenate([sb0_ref[...], sb1_ref[...]], axis=1).astype(jnp.float32)
    merged = sa * ya + sb * yb
    x1 = x_ref[...] + _dot(merged.astype(jnp.bfloat16), wo_ref[...])
    x1_ref[...] = x1
    h2 = x1 * lax.rsqrt(jnp.mean(x1 * x1, axis=-1, keepdims=True) + EPS) * g2_ref[...]
    h2_ref[...] = h2

    hi = h2.astype(jnp.bfloat16)
    lo = (h2 - hi.astype(jnp.float32)).astype(jnp.bfloat16)
    logits = (_dot(hi, wr_hi_ref[...]) + _dot(lo, wr_hi_ref[...]) + _dot(hi, wr_lo_ref[...])
              + br_ref[...])
    lane = lax.broadcasted_iota(jnp.int32, logits.shape, 1)
    neg = -jnp.inf
    gl = jnp.where(lane < N_GROUPS, logits, neg)
    gmax = jnp.max(gl, axis=-1, keepdims=True)
    g_idx = jnp.min(jnp.where(gl == gmax, lane, LANES), axis=-1, keepdims=True)
    p_g = 1.0 / jnp.sum(jnp.exp(gl - gmax), axis=-1, keepdims=True)
    e_lo = N_GROUPS + EXPERTS_PER_GROUP * g_idx
    el = jnp.where((lane >= e_lo) & (lane < e_lo + EXPERTS_PER_GROUP), logits, neg)
    v1 = jnp.max(el, axis=-1, keepdims=True)
    i1 = jnp.min(jnp.where(el == v1, lane, LANES), axis=-1, keepdims=True)
    el2 = jnp.where(lane == i1, neg, el)
    v2 = jnp.max(el2, axis=-1, keepdims=True)
    i2 = jnp.min(jnp.where(el2 == v2, lane, LANES), axis=-1, keepdims=True)
    t = jnp.exp(v2 - v1)
    w1 = p_g / (1.0 + t)
    w2 = p_g * t / (1.0 + t)
    e1 = (i1 - N_GROUPS).astype(jnp.float32)
    e2 = (i2 - N_GROUPS).astype(jnp.float32)
    route_ref[...] = jnp.where(lane == 0, e1, jnp.where(lane == 1, e2, jnp.where(lane == 2, w1, w2)))


def _merge(x2, oa, ob, proj, wa, wb, wo, g2, w_gr, b_gr, w_er, b_er):
    n = x2.shape[0]
    tm = min(PROJ_ROWS, n)
    half = D_MODEL // 2
    wr = jnp.zeros((D_MODEL, LANES), jnp.float32)
    wr = wr.at[:, :N_GROUPS].set(w_gr).at[:, N_GROUPS:N_GROUPS + N_EXPERTS].set(w_er)
    wr_hi = wr.astype(jnp.bfloat16)
    wr_lo = (wr - wr_hi.astype(jnp.float32)).astype(jnp.bfloat16)
    br = jnp.zeros((1, LANES), jnp.float32)
    br = br.at[0, :N_GROUPS].set(b_gr).at[0, N_GROUPS:N_GROUPS + N_EXPERTS].set(b_er)
    full = lambda shape: pl.BlockSpec(shape, lambda i: (0,) * len(shape))
    gate = lambda c0: pl.BlockSpec((tm, half), lambda i: (i, c0 // half))
    return pl.pallas_call(
        _merge_kernel,
        grid=(n // tm,),
        in_specs=[
            pl.BlockSpec((tm, D_MODEL), lambda i: (i, 0)),
            pl.BlockSpec((tm, DA_WIDTH), lambda i: (i, 0)),
            pl.BlockSpec((tm, RET_WIDTH), lambda i: (i, 0)),
            gate(COL_GATE_A), gate(COL_GATE_A + half), gate(COL_GATE_B), gate(COL_GATE_B + half),
            full((DA_WIDTH, D_MODEL)), full((RET_WIDTH, D_MODEL)), full((D_MODEL, D_MODEL)),
            full((1, D_MODEL)), full((D_MODEL, LANES)), full((D_MODEL, LANES)), full((1, LANES)),
        ],
        out_specs=[
            pl.BlockSpec((tm, D_MODEL), lambda i: (i, 0)),
            pl.BlockSpec((tm, D_MODEL), lambda i: (i, 0)),
            pl.BlockSpec((tm, LANES), lambda i: (i, 0)),
        ],
        out_shape=[
            jax.ShapeDtypeStruct((n, D_MODEL), jnp.float32),
            jax.ShapeDtypeStruct((n, D_MODEL), jnp.float32),
            jax.ShapeDtypeStruct((n, LANES), jnp.float32),
        ],
        compiler_params=pltpu.CompilerParams(dimension_semantics=("arbitrary",),
                                             vmem_limit_bytes=VMEM_LIMIT),
        name="merge",
    )(x2, oa, ob, proj, proj, proj, proj, wa.astype(jnp.bfloat16), wb.astype(jnp.bfloat16),
      wo.astype(jnp.bfloat16), g2.reshape(1, D_MODEL), wr_hi, wr_lo, br)


def _permute_kernel(sidx_ref, didx_ref, src_ref, *rest, rows):
    dst_ref, sem = rest[-2:]
    base = pl.program_id(0) * rows

    def row_copy(a):
        return pltpu.make_async_copy(src_ref.at[pl.ds(sidx_ref[a], 1)],
                                     dst_ref.at[pl.ds(didx_ref[a], 1)], sem)

    def issue(a, carry):
        row_copy(base + a).start()
        return carry

    def drain(a, carry):
        row_copy(base + a).wait()
        return carry

    lax.fori_loop(0, rows, issue, 0, unroll=8)
    lax.fori_loop(0, rows, drain, 0, unroll=8)


def _permute_rows(src, sidx, didx, dst_rows, dst_init=None):
    count = sidx.shape[0]
    rows = min(PERM_ROWS, count)
    any_spec = pl.BlockSpec(memory_space=pl.ANY)
    extra = () if dst_init is None else (dst_init,)
    return pl.pallas_call(
        functools.partial(_permute_kernel, rows=rows),
        grid_spec=pltpu.PrefetchScalarGridSpec(
            num_scalar_prefetch=2,
            grid=(count // rows,),
            in_specs=[any_spec] * (1 + len(extra)),
            out_specs=any_spec,
            scratch_shapes=[pltpu.SemaphoreType.DMA(())],
        ),
        out_shape=jax.ShapeDtypeStruct((dst_rows, src.shape[1]), src.dtype),
        input_output_aliases={} if dst_init is None else {3: 0},
        compiler_params=pltpu.CompilerParams(dimension_semantics=("arbitrary",)),
        name="permute_rows",
    )(sidx, didx, src, *extra)


def _expert_kernel(blk_e_ref, n_used_ref, x_ref, wg_ref, wu_ref, wd_ref, o_ref):
    i = pl.program_id(0)

    @pl.when(i < n_used_ref[0])
    def _():
        x = x_ref[...].astype(jnp.bfloat16)
        a = _dot(x, wg_ref[0])
        u = _dot(x, wu_ref[0])
        hmid = (a * _sigmoid(a) * u).astype(jnp.bfloat16)
        o_ref[...] = _dot(hmid, wd_ref[0])

    @pl.when(i >= n_used_ref[0])
    def _():
        o_ref[...] = jnp.zeros(o_ref.shape, o_ref.dtype)


def _experts(xs, blk_expert, n_used, w_gate, w_up, w_down):
    p = xs.shape[0]
    nblk = p // MOE_BLOCK
    live = lambda i, be, nu: jnp.minimum(i, nu[0] - 1)
    return pl.pallas_call(
        _expert_kernel,
        grid_spec=pltpu.PrefetchScalarGridSpec(
            num_scalar_prefetch=2,
            grid=(nblk,),
            in_specs=[
                pl.BlockSpec((MOE_BLOCK, D_MODEL), lambda i, be, nu: (live(i, be, nu), 0)),
                pl.BlockSpec((1, D_MODEL, EXPERT_FF), lambda i, be, nu: (be[i], 0, 0)),
                pl.BlockSpec((1, D_MODEL, EXPERT_FF), lambda i, be, nu: (be[i], 0, 0)),
                pl.BlockSpec((1, EXPERT_FF, D_MODEL), lambda i, be, nu: (be[i], 0, 0)),
            ],
            out_specs=pl.BlockSpec((MOE_BLOCK, D_MODEL), lambda i, be, nu: (i, 0)),
        ),
        out_shape=jax.ShapeDtypeStruct((p, D_MODEL), jnp.float32),
        compiler_params=pltpu.CompilerParams(dimension_semantics=("arbitrary",),
                                             vmem_limit_bytes=VMEM_LIMIT),
        name="experts",
    )(blk_expert, n_used, xs, w_gate.astype(jnp.bfloat16), w_up.astype(jnp.bfloat16),
      w_down.astype(jnp.bfloat16))


def _combine_kernel(x1_ref, y0_ref, y1_ref, route_ref, o_ref):
    route = route_ref[...]
    o_ref[...] = x1_ref[...] + route[:, 2:3] * y0_ref[...] + route[:, 3:4] * y1_ref[...]


def _combine(x1, yg, route):
    n = x1.shape[0]
    tm = min(PROJ_ROWS, n)
    return pl.pallas_call(
        _combine_kernel,
        grid=(n // tm,),
        in_specs=[
            pl.BlockSpec((tm, D_MODEL), lambda i: (i, 0)),
            pl.BlockSpec((tm, D_MODEL), lambda i: (i, 0)),
            pl.BlockSpec((tm, D_MODEL), lambda i: (i, 1)),
            pl.BlockSpec((tm, LANES), lambda i: (i, 0)),
        ],
        out_specs=pl.BlockSpec((tm, D_MODEL), lambda i: (i, 0)),
        out_shape=jax.ShapeDtypeStruct((n, D_MODEL), jnp.float32),
        compiler_params=pltpu.CompilerParams(dimension_semantics=("arbitrary",),
                                             vmem_limit_bytes=VMEM_LIMIT),
        name="combine",
    )(x1, yg, yg, route)


def _dispatch_plan(route, n):
    e_flat = route[:, :TOP_K].astype(jnp.int32).reshape(-1)
    onehot = (e_flat[:, None] == jnp.arange(N_EXPERTS, dtype=jnp.int32)[None, :]).astype(jnp.int32)
    csum = jnp.cumsum(onehot, axis=0)
    counts = csum[-1]
    rank = jnp.sum((csum - onehot) * onehot, axis=1)
    padded = ((counts + MOE_BLOCK - 1) // MOE_BLOCK) * MOE_BLOCK
    pad_end = jnp.cumsum(padded)
    pad_start = pad_end - padded
    dest = (pad_start[e_flat] + rank).astype(jnp.int32)
    nblk = (n * TOP_K + N_EXPERTS * MOE_BLOCK) // MOE_BLOCK
    blk_start = jnp.arange(nblk, dtype=jnp.int32) * MOE_BLOCK
    blk_expert = jnp.clip(jnp.searchsorted(pad_end, blk_start, side='right'), 0, N_EXPERTS - 1)
    n_used = (pad_end[-1] // MOE_BLOCK).astype(jnp.int32).reshape(1)
    return dest, blk_expert.astype(jnp.int32), n_used


def _layer(x, positions, norm1_g, w_in, q_norm_g, k_norm_g, lam4, diff_subln_g, ret_gn_g, ret_gn_b,
           w_branch_a, w_branch_b, w_out, norm2_g, w_gr, b_gr, w_er, b_er, w_gate, w_up, w_down):
    batch, seq, _ = x.shape
    n = batch * seq
    x2 = x.reshape(n, D_MODEL)
    proj = _in_proj(x2, positions.reshape(n, 1), norm1_g, w_in, q_norm_g, k_norm_g)
    oa = _diff_attn(proj, lam4, diff_subln_g.reshape(1, DA_VDIM), batch, seq)
    ob = _retention(proj, ret_gn_g, ret_gn_b, batch, seq)
    x1, h2, route = _merge(x2, oa, ob, proj, w_branch_a, w_branch_b, w_out, norm2_g, w_gr, b_gr, w_er, b_er)
    dest, blk_expert, n_used = _dispatch_plan(route, n)
    tok = jnp.arange(n * TOP_K, dtype=jnp.int32) // TOP_K
    p = n * TOP_K + N_EXPERTS * MOE_BLOCK
    xs = _permute_rows(h2, tok, dest, p, dst_init=jnp.zeros((p, D_MODEL), jnp.float32))
    ys = _experts(xs, blk_expert, n_used, w_gate, w_up, w_down)
    slot = jnp.arange(n * TOP_K, dtype=jnp.int32)
    yg = _permute_rows(ys, dest, slot, n * TOP_K)
    out = _combine(x1, yg.reshape(n, TOP_K * D_MODEL), route)
    return out.reshape(batch, seq, D_MODEL)


def kernel(x, positions, norm1_g, w_in, q_norm_g, k_norm_g, lambda_q1, lambda_k1, lambda_q2, lambda_k2, diff_subln_g, ret_gn_g, ret_gn_b, w_branch_a, w_branch_b, w_out, norm2_g, w_group_router, b_group_router, w_expert_router, b_expert_router, w_gate, w_up, w_down):
    assert x.shape[-1] == D_MODEL and norm1_g.shape[0] == 1, "single-layer, D_MODEL-wide input expected"
    lam4 = jnp.zeros((4, LANES), jnp.float32)
    lam4 = lam4.at[:, :DA_HALF].set(jnp.stack([lambda_q1[0], lambda_k1[0], lambda_q2[0], lambda_k2[0]]))
    return _layer(x, positions, norm1_g[0], w_in[0], q_norm_g[0], k_norm_g[0], lam4, diff_subln_g[0],
                  ret_gn_g[0], ret_gn_b[0], w_branch_a[0], w_branch_b[0], w_out[0], norm2_g[0],
                  w_group_router[0], b_group_router[0], w_expert_router[0], b_expert_router[0],
                  w_gate[0], w_up[0], w_down[0])
```

```python
import functools
import math

import jax
import jax.numpy as jnp
from jax import lax
from jax.experimental import pallas as pl
from jax.experimental.pallas import tpu as pltpu

D_MODEL = 1024
DA_HEADS = 4
DA_HALF = 64
DA_VDIM = 2 * DA_HALF
DA_WIDTH = DA_HEADS * DA_VDIM
ROPE_THETA = 500000.0
ROPE_DIM = DA_HALF // 4
RET_HEADS = 4
RET_KDIM = 128
RET_VDIM = 128
RET_WIDTH = RET_HEADS * RET_VDIM
RET_THETA = 10000.0
N_GROUPS = 4
EXPERTS_PER_GROUP = 8
N_EXPERTS = N_GROUPS * EXPERTS_PER_GROUP
TOP_K = 2
EXPERT_FF = 512
EPS = 1e-6
LAMBDA_INIT = 0.8 - 0.6 * math.exp(-0.3 * 0)

LANES = 128
IN_COLS = 3 * DA_WIDTH + 4 * RET_WIDTH + 2 * D_MODEL
COL_QA, COL_KA, COL_VA = 0, DA_WIDTH, 2 * DA_WIDTH
COL_QR = 3 * DA_WIDTH
COL_KR = COL_QR + RET_WIDTH
COL_VR = COL_KR + RET_WIDTH
COL_GB = COL_VR + RET_WIDTH
COL_GATE_A = COL_GB + RET_WIDTH
COL_GATE_B = COL_GATE_A + D_MODEL

PROJ_ROWS = 512
PROJ_CHUNK = 256
ATT_TQ = 256
RET_CHUNK = 256
MOE_BLOCK = 256
PERM_ROWS = 1024
VMEM_LIMIT = 56 * 1024 * 1024


def _dot(a, b):
    return jnp.dot(a, b, preferred_element_type=jnp.float32)


def _dot_nt(a, b):
    return lax.dot_general(a, b, (((1,), (1,)), ((), ())), preferred_element_type=jnp.float32)


def _dot_tn(a, b):
    return lax.dot_general(a, b, (((0,), (0,)), ((), ())), preferred_element_type=jnp.float32)


def _sigmoid(x):
    return 1.0 / (1.0 + jnp.exp(-x))


ROW_TILES = D_MODEL // LANES


def _store_row_tiles(ref, val):
    for j in range(ROW_TILES):
        ref[:, j, :] = val[:, j * LANES:(j + 1) * LANES].astype(ref.dtype)


def _load_row_tiles(ref):
    return jnp.concatenate([ref[:, j, :] for j in range(ROW_TILES)], axis=1)


def _in_proj_kernel(x_ref, pos_ref, g1_ref, w_ref, gsum_ref, gq_ref, gk_ref, fa_ref, fr_ref,
                    o_ref, h_scr):
    x = x_ref[...]
    h = x * lax.rsqrt(jnp.mean(x * x, axis=-1, keepdims=True) + EPS) * g1_ref[...]
    h_scr[...] = h.astype(jnp.bfloat16)
    rows = x.shape[0]
    pos = pos_ref[...].astype(jnp.float32)

    lane = lax.broadcasted_iota(jnp.int32, (rows, LANES), 1)
    ang_a = pos * fa_ref[...]
    cos_a, sin_a = jnp.cos(ang_a), jnp.sin(ang_a)
    lane64 = lane % DA_HALF
    half_a = ROPE_DIM // 2
    c_a = jnp.where(lane64 < ROPE_DIM, cos_a, 1.0)
    s_lo = jnp.where(lane64 < half_a, -sin_a, 0.0)
    s_hi = jnp.where((lane64 >= half_a) & (lane64 < ROPE_DIM), sin_a, 0.0)
    c_a2 = jnp.concatenate([c_a, c_a], axis=1)
    s_lo2 = jnp.concatenate([s_lo, s_lo], axis=1)
    s_hi2 = jnp.concatenate([s_hi, s_hi], axis=1)
    ang_r = pos * fr_ref[...]
    cos_r, sin_r = jnp.cos(ang_r), jnp.sin(ang_r)
    s_r = jnp.where(lane < RET_KDIM // 2, -sin_r, sin_r)
    c_r2 = jnp.concatenate([cos_r, cos_r], axis=1)
    s_r2 = jnp.concatenate([s_r, s_r], axis=1)

    def qk_norm_rope(y, g, scale):
        ss = y * y
        hi = ss.astype(jnp.bfloat16)
        lo = (ss - hi.astype(jnp.float32)).astype(jnp.bfloat16)
        gs = _dot(hi, gsum_ref[...]) + _dot(lo, gsum_ref[...])
        n = y * lax.rsqrt(gs * (1.0 / DA_HALF) + EPS) * g
        up = pltpu.roll(n, PROJ_CHUNK - half_a, axis=1)
        dn = pltpu.roll(n, half_a, axis=1)
        r = n * c_a2 + up * s_lo2 + dn * s_hi2
        return r * scale if scale != 1.0 else r

    def ret_rope(y, scale):
        halves = [pltpu.roll(y[:, i * LANES:(i + 1) * LANES], RET_KDIM // 2, axis=1)
                  for i in range(PROJ_CHUNK // LANES)]
        sw = jnp.concatenate(halves, axis=1)
        r = y * c_r2 + sw * s_r2
        return r * scale if scale != 1.0 else r

    for c in range(IN_COLS // PROJ_CHUNK):
        c0 = c * PROJ_CHUNK
        y = _dot(h_scr[...], w_ref[:, c0:c0 + PROJ_CHUNK])
        if c0 < COL_KA:
            y = qk_norm_rope(y, gq_ref[...], DA_HALF ** -0.5)
        elif c0 < COL_VA:
            y = qk_norm_rope(y, gk_ref[...], 1.0)
        elif c0 < COL_QR:
            pass
        elif c0 < COL_KR:
            y = ret_rope(y, 1.0)
        elif c0 < COL_VR:
            y = ret_rope(y, RET_KDIM ** -0.5)
        elif c0 < COL_GB:
            pass
        elif c0 < COL_GATE_A:
            y = y * _sigmoid(y)
        else:
            y = _sigmoid(y)
        o_ref[:, c0:c0 + PROJ_CHUNK] = y.astype(o_ref.dtype)


def _in_proj(x2, pos2, g1, w_in, gq, gk):
    n = x2.shape[0]
    tm = min(PROJ_ROWS, n)
    grp = jnp.arange(PROJ_CHUNK) // DA_HALF
    gsum = (grp[:, None] == grp[None, :]).astype(jnp.bfloat16)
    half_a = ROPE_DIM // 2
    freq_a = jnp.power(jnp.float32(ROPE_THETA), -2.0 * jnp.arange(half_a, dtype=jnp.float32) / ROPE_DIM)
    fa = freq_a[jnp.arange(LANES) % half_a][None, :]
    half_r = RET_KDIM // 2
    freq_r = jnp.power(jnp.float32(RET_THETA), -2.0 * jnp.arange(half_r, dtype=jnp.float32) / RET_KDIM)
    fr = freq_r[jnp.arange(LANES) % half_r][None, :]
    reps = PROJ_CHUNK // DA_HALF
    full = lambda shape: pl.BlockSpec(shape, lambda i: (0,) * len(shape))
    return pl.pallas_call(
        _in_proj_kernel,
        grid=(n // tm,),
        in_specs=[
            pl.BlockSpec((tm, D_MODEL), lambda i: (i, 0)),
            pl.BlockSpec((tm, 1), lambda i: (i, 0)),
            full((1, D_MODEL)),
            full((D_MODEL, IN_COLS)),
            full((PROJ_CHUNK, PROJ_CHUNK)),
            full((1, PROJ_CHUNK)),
            full((1, PROJ_CHUNK)),
            full((1, LANES)),
            full((1, LANES)),
        ],
        out_specs=pl.BlockSpec((tm, IN_COLS), lambda i: (i, 0)),
        out_shape=jax.ShapeDtypeStruct((n, IN_COLS), jnp.bfloat16),
        scratch_shapes=[pltpu.VMEM((tm, D_MODEL), jnp.bfloat16)],
        compiler_params=pltpu.CompilerParams(dimension_semantics=("arbitrary",),
                                             vmem_limit_bytes=VMEM_LIMIT),
        name="in_proj",
    )(x2, pos2, g1.reshape(1, D_MODEL), w_in.astype(jnp.bfloat16), gsum,
      jnp.tile(gq, reps)[None, :], jnp.tile(gk, reps)[None, :], fa, fr)


def _diff_attn_kernel(q_ref, k_ref, v_ref, lam_ref, gsub_ref, o_ref, m_scr, l_scr, acc_scr):
    i = pl.program_id(2)
    tq = q_ref.shape[0]
    q = q_ref[...]
    lane = lax.broadcasted_iota(jnp.int32, q.shape, 1)
    zero = jnp.zeros_like(q)
    qs = jnp.concatenate([jnp.where(lane < DA_HALF, q, zero), jnp.where(lane >= DA_HALF, q, zero)], axis=0)
    m_scr[...] = jnp.full(m_scr.shape, -jnp.inf, jnp.float32)
    l_scr[...] = jnp.zeros(l_scr.shape, jnp.float32)
    acc_scr[...] = jnp.zeros(acc_scr.shape, jnp.float32)

    def step(j, masked):
        start = pl.multiple_of(j * tq, tq)
        k = k_ref[pl.ds(start, tq), :]
        v = v_ref[pl.ds(start, tq), :]
        s = _dot_nt(qs, k)
        if masked:
            r = lax.broadcasted_iota(jnp.int32, s.shape, 0) % tq
            c = lax.broadcasted_iota(jnp.int32, s.shape, 1)
            s = jnp.where(c <= r, s, -jnp.inf)
        m_old = m_scr[...]
        m_new = jnp.maximum(m_old, jnp.max(s, axis=-1, keepdims=True))
        alpha = jnp.exp(m_old - m_new)
        p = jnp.exp(s - m_new)
        l_scr[...] = alpha * l_scr[...] + jnp.sum(p, axis=-1, keepdims=True)
        acc_scr[...] = alpha * acc_scr[...] + _dot(p.astype(jnp.bfloat16), v)
        m_scr[...] = m_new

    def body(j, carry):
        step(j, False)
        return carry

    lax.fori_loop(0, i, body, 0)
    step(i, True)

    lam4 = lam_ref[...]
    lam = (jnp.exp(jnp.sum(lam4[0:1] * lam4[1:2], axis=-1, keepdims=True))
           - jnp.exp(jnp.sum(lam4[2:3] * lam4[3:4], axis=-1, keepdims=True)) + LAMBDA_INIT)
    o_all = acc_scr[...] / l_scr[...]
    o = o_all[:tq] - lam * o_all[tq:]
    o = o * lax.rsqrt(jnp.mean(o * o, axis=-1, keepdims=True) + EPS) * gsub_ref[...] * (1.0 - LAMBDA_INIT)
    o_ref[...] = o.astype(o_ref.dtype)


def _diff_attn(proj, lam4, gsub, batch, seq):
    n = proj.shape[0]
    tq = min(ATT_TQ, seq)
    nq = seq // tq
    qb, kb, vb = COL_QA // LANES, COL_KA // LANES, COL_VA // LANES
    return pl.pallas_call(
        _diff_attn_kernel,
        grid=(batch, DA_HEADS, nq),
        in_specs=[
            pl.BlockSpec((tq, LANES), lambda b, h, i: (b * nq + i, qb + h)),
            pl.BlockSpec((seq, LANES), lambda b, h, i: (b, kb + h)),
            pl.BlockSpec((seq, LANES), lambda b, h, i: (b, vb + h)),
            pl.BlockSpec((4, LANES), lambda b, h, i: (0, 0)),
            pl.BlockSpec((1, LANES), lambda b, h, i: (0, 0)),
        ],
        out_specs=pl.BlockSpec((tq, LANES), lambda b, h, i: (b * nq + i, h)),
        out_shape=jax.ShapeDtypeStruct((n, DA_WIDTH), jnp.bfloat16),
        scratch_shapes=[pltpu.VMEM((2 * tq, 1), jnp.float32),
                        pltpu.VMEM((2 * tq, 1), jnp.float32),
                        pltpu.VMEM((2 * tq, DA_VDIM), jnp.float32)],
        compiler_params=pltpu.CompilerParams(dimension_semantics=("arbitrary",) * 3,
                                             vmem_limit_bytes=VMEM_LIMIT),
        name="diff_attn",
    )(proj, proj, proj, lam4, gsub)


def _retention_kernel(q_ref, k_ref, v_ref, g_ref, gng_ref, gnb_ref, o_ref, r_scr, *, chunk):
    hf = jnp.full((1, 1), pl.program_id(1), jnp.int32).astype(jnp.float32)
    log_g = jnp.log1p(-jnp.exp2(-5.0 - hf))
    ri = lax.broadcasted_iota(jnp.int32, (chunk, chunk), 0)
    ci = lax.broadcasted_iota(jnp.int32, (chunk, chunk), 1)
    rel = (ri - ci).astype(jnp.float32)
    dmask = jnp.where(rel >= 0, jnp.exp(jnp.maximum(rel, 0.0) * log_g), 0.0)
    idx = lax.broadcasted_iota(jnp.int32, (chunk, 1), 0).astype(jnp.float32)
    zeta = jnp.exp((chunk - 1 - idx) * log_g)
    xi = jnp.exp((idx + 1.0) * log_g)
    g_chunk = jnp.exp(chunk * log_g)
    r_scr[...] = jnp.zeros(r_scr.shape, jnp.float32)
    gng = gng_ref[...]
    gnb = gnb_ref[...]

    def body(n, carry):
        start = pl.multiple_of(n * chunk, chunk)
        q = q_ref[pl.ds(start, chunk), :]
        k = k_ref[pl.ds(start, chunk), :]
        v = v_ref[pl.ds(start, chunk), :]
        s = _dot_nt(q, k) * dmask
        r_old = r_scr[...]
        o = _dot(s.astype(jnp.bfloat16), v) + xi * _dot(q, r_old.astype(jnp.bfloat16))
        kz = (k.astype(jnp.float32) * zeta).astype(jnp.bfloat16)
        r_scr[...] = g_chunk * r_old + _dot_tn(kz, v)
        mu = jnp.mean(o, axis=-1, keepdims=True)
        d = o - mu
        var = jnp.mean(d * d, axis=-1, keepdims=True)
        y = d * lax.rsqrt(var + EPS) * gng + gnb
        y = y * g_ref[pl.ds(start, chunk), :].astype(jnp.float32)
        o_ref[pl.ds(start, chunk), :] = y.astype(o_ref.dtype)
        return carry

    lax.fori_loop(0, q_ref.shape[0] // chunk, body, 0)


def _retention(proj, gn_g, gn_b, batch, seq):
    n = proj.shape[0]
    chunk = min(RET_CHUNK, seq)
    col = lambda c0: (lambda b, h: (b, c0 // LANES + h))
    return pl.pallas_call(
        functools.partial(_retention_kernel, chunk=chunk),
        grid=(batch, RET_HEADS),
        in_specs=[
            pl.BlockSpec((seq, LANES), col(COL_QR)),
            pl.BlockSpec((seq, LANES), col(COL_KR)),
            pl.BlockSpec((seq, LANES), col(COL_VR)),
            pl.BlockSpec((seq, LANES), col(COL_GB)),
            pl.BlockSpec((1, LANES), lambda b, h: (0, h)),
            pl.BlockSpec((1, LANES), lambda b, h: (0, h)),
        ],
        out_specs=pl.BlockSpec((seq, LANES), lambda b, h: (b, h)),
        out_shape=jax.ShapeDtypeStruct((n, RET_WIDTH), jnp.bfloat16),
        scratch_shapes=[pltpu.VMEM((RET_KDIM, RET_VDIM), jnp.float32)],
        compiler_params=pltpu.CompilerParams(dimension_semantics=("arbitrary",) * 2,
                                             vmem_limit_bytes=VMEM_LIMIT),
        name="retention",
    )(proj, proj, proj, proj, gn_g.reshape(1, RET_WIDTH), gn_b.reshape(1, RET_WIDTH))


def _merge_kernel(x_ref, oa_ref, ob_ref, sa0_ref, sa1_ref, sb0_ref, sb1_ref, wa_ref, wb_ref, wo_ref,
                  g2_ref, wr_hi_ref, wr_lo_ref, br_ref, x1_ref, h2_ref, route_ref):
    ya = _dot(oa_ref[...], wa_ref[...])
    yb = _dot(ob_ref[...], wb_ref[...])
    sa = jnp.concatenate([sa0_ref[...], sa1_ref[...]], axis=1).astype(jnp.float32)
    sb = jnp.concatenate([sb0_ref[...], sb1_ref[...]], axis=1).astype(jnp.float32)
    merged = sa * ya + sb * yb
    x1 = x_ref[...] + _dot(merged.astype(jnp.bfloat16), wo_ref[...])
    x1_ref[...] = x1
    h2 = x1 * lax.rsqrt(jnp.mean(x1 * x1, axis=-1, keepdims=True) + EPS) * g2_ref[...]
    _store_row_tiles(h2_ref, h2)

    hi = h2.astype(jnp.bfloat16)
    lo = (h2 - hi.astype(jnp.float32)).astype(jnp.bfloat16)
    logits = (_dot(hi, wr_hi_ref[...]) + _dot(lo, wr_hi_ref[...]) + _dot(hi, wr_lo_ref[...])
              + br_ref[...])
    lane = lax.broadcasted_iota(jnp.int32, logits.shape, 1)
    neg = -jnp.inf
    gl = jnp.where(lane < N_GROUPS, logits, neg)
    gmax = jnp.max(gl, axis=-1, keepdims=True)
    g_idx = jnp.min(jnp.where(gl == gmax, lane, LANES), axis=-1, keepdims=True)
    p_g = 1.0 / jnp.sum(jnp.exp(gl - gmax), axis=-1, keepdims=True)
    e_lo = N_GROUPS + EXPERTS_PER_GROUP * g_idx
    el = jnp.where((lane >= e_lo) & (lane < e_lo + EXPERTS_PER_GROUP), logits, neg)
    v1 = jnp.max(el, axis=-1, keepdims=True)
    i1 = jnp.min(jnp.where(el == v1, lane, LANES), axis=-1, keepdims=True)
    el2 = jnp.where(lane == i1, neg, el)
    v2 = jnp.max(el2, axis=-1, keepdims=True)
    i2 = jnp.min(jnp.where(el2 == v2, lane, LANES), axis=-1, keepdims=True)
    t = jnp.exp(v2 - v1)
    w1 = p_g / (1.0 + t)
    w2 = p_g * t / (1.0 + t)
    e1 = (i1 - N_GROUPS).astype(jnp.float32)
    e2 = (i2 - N_GROUPS).astype(jnp.float32)
    route_ref[...] = jnp.where(lane == 0, e1, jnp.where(lane == 1, e2, jnp.where(lane == 2, w1, w2)))


def _merge(x2, oa, ob, proj, wa, wb, wo, g2, w_gr, b_gr, w_er, b_er):
    n = x2.shape[0]
    tm = min(PROJ_ROWS, n)
    half = D_MODEL // 2
    wr = jnp.zeros((D_MODEL, LANES), jnp.float32)
    wr = wr.at[:, :N_GROUPS].set(w_gr).at[:, N_GROUPS:N_GROUPS + N_EXPERTS].set(w_er)
    wr_hi = wr.astype(jnp.bfloat16)
    wr_lo = (wr - wr_hi.astype(jnp.float32)).astype(jnp.bfloat16)
    br = jnp.zeros((1, LANES), jnp.float32)
    br = br.at[0, :N_GROUPS].set(b_gr).at[0, N_GROUPS:N_GROUPS + N_EXPERTS].set(b_er)
    full = lambda shape: pl.BlockSpec(shape, lambda i: (0,) * len(shape))
    gate = lambda c0: pl.BlockSpec((tm, half), lambda i: (i, c0 // half))
    return pl.pallas_call(
        _merge_kernel,
        grid=(n // tm,),
        in_specs=[
            pl.BlockSpec((tm, D_MODEL), lambda i: (i, 0)),
            pl.BlockSpec((tm, DA_WIDTH), lambda i: (i, 0)),
            pl.BlockSpec((tm, RET_WIDTH), lambda i: (i, 0)),
            gate(COL_GATE_A), gate(COL_GATE_A + half), gate(COL_GATE_B), gate(COL_GATE_B + half),
            full((DA_WIDTH, D_MODEL)), full((RET_WIDTH, D_MODEL)), full((D_MODEL, D_MODEL)),
            full((1, D_MODEL)), full((D_MODEL, LANES)), full((D_MODEL, LANES)), full((1, LANES)),
        ],
        out_specs=[
            pl.BlockSpec((tm, D_MODEL), lambda i: (i, 0)),
            pl.BlockSpec((tm, ROW_TILES, LANES), lambda i: (i, 0, 0)),
            pl.BlockSpec((tm, LANES), lambda i: (i, 0)),
        ],
        out_shape=[
            jax.ShapeDtypeStruct((n, D_MODEL), jnp.float32),
            jax.ShapeDtypeStruct((n, ROW_TILES, LANES), jnp.float32),
            jax.ShapeDtypeStruct((n, LANES), jnp.float32),
        ],
        compiler_params=pltpu.CompilerParams(dimension_semantics=("arbitrary",),
                                             vmem_limit_bytes=VMEM_LIMIT),
        name="merge",
    )(x2, oa, ob, proj, proj, proj, proj, wa.astype(jnp.bfloat16), wb.astype(jnp.bfloat16),
      wo.astype(jnp.bfloat16), g2.reshape(1, D_MODEL), wr_hi, wr_lo, br)


def _permute_kernel(sidx_ref, didx_ref, src_ref, *rest, rows):
    dst_ref, sem = rest[-2:]
    base = pl.program_id(0) * rows

    def row_copy(a):
        return pltpu.make_async_copy(src_ref.at[sidx_ref[a]], dst_ref.at[didx_ref[a]], sem)

    def issue(a, carry):
        row_copy(base + a).start()
        return carry

    def drain(a, carry):
        row_copy(base + a).wait()
        return carry

    lax.fori_loop(0, rows, issue, 0, unroll=8)
    lax.fori_loop(0, rows, drain, 0, unroll=8)


def _permute_rows(src, sidx, didx, dst_rows, dst_init=None):
    count = sidx.shape[0]
    rows = min(PERM_ROWS, count)
    any_spec = pl.BlockSpec(memory_space=pl.ANY)
    extra = () if dst_init is None else (dst_init,)
    return pl.pallas_call(
        functools.partial(_permute_kernel, rows=rows),
        grid_spec=pltpu.PrefetchScalarGridSpec(
            num_scalar_prefetch=2,
            grid=(count // rows,),
            in_specs=[any_spec] * (1 + len(extra)),
            out_specs=any_spec,
            scratch_shapes=[pltpu.SemaphoreType.DMA(())],
        ),
        out_shape=jax.ShapeDtypeStruct((dst_rows,) + src.shape[1:], src.dtype),
        input_output_aliases={} if dst_init is None else {3: 0},
        compiler_params=pltpu.CompilerParams(dimension_semantics=("arbitrary",)),
        name="permute_rows",
    )(sidx, didx, src, *extra)


def _expert_kernel(blk_e_ref, n_used_ref, x_ref, wg_ref, wu_ref, wd_ref, o_ref):
    i = pl.program_id(0)

    @pl.when(i < n_used_ref[0])
    def _():
        x = _load_row_tiles(x_ref).astype(jnp.bfloat16)
        a = _dot(x, wg_ref[0])
        u = _dot(x, wu_ref[0])
        hmid = (a * _sigmoid(a) * u).astype(jnp.bfloat16)
        _store_row_tiles(o_ref, _dot(hmid, wd_ref[0]))

    @pl.when(i >= n_used_ref[0])
    def _():
        o_ref[...] = jnp.zeros(o_ref.shape, o_ref.dtype)


def _experts(xs, blk_expert, n_used, w_gate, w_up, w_down):
    p = xs.shape[0]
    nblk = p // MOE_BLOCK
    live = lambda i, be, nu: jnp.minimum(i, nu[0] - 1)
    return pl.pallas_call(
        _expert_kernel,
        grid_spec=pltpu.PrefetchScalarGridSpec(
            num_scalar_prefetch=2,
            grid=(nblk,),
            in_specs=[
                pl.BlockSpec((MOE_BLOCK, ROW_TILES, LANES), lambda i, be, nu: (live(i, be, nu), 0, 0)),
                pl.BlockSpec((1, D_MODEL, EXPERT_FF), lambda i, be, nu: (be[i], 0, 0)),
                pl.BlockSpec((1, D_MODEL, EXPERT_FF), lambda i, be, nu: (be[i], 0, 0)),
                pl.BlockSpec((1, EXPERT_FF, D_MODEL), lambda i, be, nu: (be[i], 0, 0)),
            ],
            out_specs=pl.BlockSpec((MOE_BLOCK, ROW_TILES, LANES), lambda i, be, nu: (i, 0, 0)),
        ),
        out_shape=jax.ShapeDtypeStruct((p, ROW_TILES, LANES), jnp.float32),
        compiler_params=pltpu.CompilerParams(dimension_semantics=("arbitrary",),
                                             vmem_limit_bytes=VMEM_LIMIT),
        name="experts",
    )(blk_expert, n_used, xs, w_gate.astype(jnp.bfloat16), w_up.astype(jnp.bfloat16),
      w_down.astype(jnp.bfloat16))


def _combine_kernel(x1_ref, y0_ref, y1_ref, route_ref, o_ref):
    route = route_ref[...]
    o_ref[...] = (x1_ref[...] + route[:, 2:3] * _load_row_tiles(y0_ref)
                  + route[:, 3:4] * _load_row_tiles(y1_ref))


def _combine(x1, yg, route):
    n = x1.shape[0]
    tm = min(PROJ_ROWS, n)
    return pl.pallas_call(
        _combine_kernel,
        grid=(n // tm,),
        in_specs=[
            pl.BlockSpec((tm, D_MODEL), lambda i: (i, 0)),
            pl.BlockSpec((tm, ROW_TILES, LANES), lambda i: (i, 0, 0)),
            pl.BlockSpec((tm, ROW_TILES, LANES), lambda i: (i + n // tm, 0, 0)),
            pl.BlockSpec((tm, LANES), lambda i: (i, 0)),
        ],
        out_specs=pl.BlockSpec((tm, D_MODEL), lambda i: (i, 0)),
        out_shape=jax.ShapeDtypeStruct((n, D_MODEL), jnp.float32),
        compiler_params=pltpu.CompilerParams(dimension_semantics=("arbitrary",),
                                             vmem_limit_bytes=VMEM_LIMIT),
        name="combine",
    )(x1, yg, yg, route)


def _dispatch_plan(route, n):
    e_flat = route[:, :TOP_K].astype(jnp.int32).reshape(-1)
    onehot = (e_flat[:, None] == jnp.arange(N_EXPERTS, dtype=jnp.int32)[None, :]).astype(jnp.int32)
    csum = jnp.cumsum(onehot, axis=0)
    counts = csum[-1]
    rank = jnp.sum((csum - onehot) * onehot, axis=1)
    padded = ((counts + MOE_BLOCK - 1) // MOE_BLOCK) * MOE_BLOCK
    pad_end = jnp.cumsum(padded)
    pad_start = pad_end - padded
    dest = (pad_start[e_flat] + rank).astype(jnp.int32)
    nblk = (n * TOP_K + N_EXPERTS * MOE_BLOCK) // MOE_BLOCK
    blk_start = jnp.arange(nblk, dtype=jnp.int32) * MOE_BLOCK
    blk_expert = jnp.clip(jnp.searchsorted(pad_end, blk_start, side='right'), 0, N_EXPERTS - 1)
    n_used = (pad_end[-1] // MOE_BLOCK).astype(jnp.int32).reshape(1)
    return dest, blk_expert.astype(jnp.int32), n_used


def _layer(x, positions, norm1_g, w_in, q_norm_g, k_norm_g, lam4, diff_subln_g, ret_gn_g, ret_gn_b,
           w_branch_a, w_branch_b, w_out, norm2_g, w_gr, b_gr, w_er, b_er, w_gate, w_up, w_down):
    batch, seq, _ = x.shape
    n = batch * seq
    x2 = x.reshape(n, D_MODEL)
    proj = _in_proj(x2, positions.reshape(n, 1), norm1_g, w_in, q_norm_g, k_norm_g)
    oa = _diff_attn(proj, lam4, diff_subln_g.reshape(1, DA_VDIM), batch, seq)
    ob = _retention(proj, ret_gn_g, ret_gn_b, batch, seq)
    x1, h2, route = _merge(x2, oa, ob, proj, w_branch_a, w_branch_b, w_out, norm2_g, w_gr, b_gr, w_er, b_er)
    dest, blk_expert, n_used = _dispatch_plan(route, n)
    tok = jnp.arange(n * TOP_K, dtype=jnp.int32) // TOP_K
    p = n * TOP_K + N_EXPERTS * MOE_BLOCK
    xs = _permute_rows(h2, tok, dest, p, dst_init=jnp.zeros((p, ROW_TILES, LANES), jnp.float32))
    ys = _experts(xs, blk_expert, n_used, w_gate, w_up, w_down)
    a = jnp.arange(n * TOP_K, dtype=jnp.int32)
    slot = (a % TOP_K) * n + a // TOP_K
    yg = _permute_rows(ys, dest, slot, n * TOP_K)
    out = _combine(x1, yg, route)
    return out.reshape(batch, seq, D_MODEL)


def kernel(x, positions, norm1_g, w_in, q_norm_g, k_norm_g, lambda_q1, lambda_k1, lambda_q2, lambda_k2, diff_subln_g, ret_gn_g, ret_gn_b, w_branch_a, w_branch_b, w_out, norm2_g, w_group_router, b_group_router, w_expert_router, b_expert_router, w_gate, w_up, w_down):
    assert x.shape[-1] == D_MODEL and norm1_g.shape[0] == 1, "single-layer, D_MODEL-wide input expected"
    lam4 = jnp.zeros((4, LANES), jnp.float32)
    lam4 = lam4.at[:, :DA_HALF].set(jnp.stack([lambda_q1[0], lambda_k1[0], lambda_q2[0], lambda_k2[0]]))
    return _layer(x, positions, norm1_g[0], w_in[0], q_norm_g[0], k_norm_g[0], lam4, diff_subln_g[0],
                  ret_gn_g[0], ret_gn_b[0], w_branch_a[0], w_branch_b[0], w_out[0], norm2_g[0],
                  w_group_router[0], b_group_router[0], w_expert_router[0], b_expert_router[0],
                  w_gate[0], w_up[0], w_down[0])
```

```python
import functools
import math

import jax
import jax.numpy as jnp
from jax import lax
from jax.experimental import pallas as pl
from jax.experimental.pallas import tpu as pltpu

D_MODEL = 1024
DA_HEADS = 4
DA_HALF = 64
DA_VDIM = 2 * DA_HALF
DA_WIDTH = DA_HEADS * DA_VDIM
ROPE_THETA = 500000.0
ROPE_DIM = DA_HALF // 4
RET_HEADS = 4
RET_KDIM = 128
RET_VDIM = 128
RET_WIDTH = RET_HEADS * RET_VDIM
RET_THETA = 10000.0
N_GROUPS = 4
EXPERTS_PER_GROUP = 8
N_EXPERTS = N_GROUPS * EXPERTS_PER_GROUP
TOP_K = 2
EXPERT_FF = 512
EPS = 1e-6
LAMBDA_INIT = 0.8 - 0.6 * math.exp(-0.3 * 0)

LANES = 128
IN_COLS = 3 * DA_WIDTH + 4 * RET_WIDTH + 2 * D_MODEL
COL_QA, COL_KA, COL_VA = 0, DA_WIDTH, 2 * DA_WIDTH
COL_QR = 3 * DA_WIDTH
COL_KR = COL_QR + RET_WIDTH
COL_VR = COL_KR + RET_WIDTH
COL_GB = COL_VR + RET_WIDTH
COL_GATE_A = COL_GB + RET_WIDTH
COL_GATE_B = COL_GATE_A + D_MODEL

PROJ_ROWS = 512
PROJ_CHUNK = 256
ATT_TQ = 256
RET_CHUNK = 256
MOE_BLOCK = 256
COMBINE_ROWS = 256
VMEM_LIMIT = 56 * 1024 * 1024


def _dot(a, b):
    return jnp.dot(a, b, preferred_element_type=jnp.float32)


def _dot_nt(a, b):
    return lax.dot_general(a, b, (((1,), (1,)), ((), ())), preferred_element_type=jnp.float32)


def _dot_tn(a, b):
    return lax.dot_general(a, b, (((0,), (0,)), ((), ())), preferred_element_type=jnp.float32)


def _sigmoid(x):
    return 1.0 / (1.0 + jnp.exp(-x))


ROW_TILES = D_MODEL // LANES


def _store_row_tiles(ref, val):
    for j in range(ROW_TILES):
        ref[:, j, :] = val[:, j * LANES:(j + 1) * LANES].astype(ref.dtype)


def _load_row_tiles(ref):
    return jnp.concatenate([ref[:, j, :] for j in range(ROW_TILES)], axis=1)


def _in_proj_kernel(x_ref, pos_ref, g1_ref, w_ref, gsum_ref, gq_ref, gk_ref, fa_ref, fr_ref,
                    o_ref, h_scr):
    x = x_ref[...]
    h = x * lax.rsqrt(jnp.mean(x * x, axis=-1, keepdims=True) + EPS) * g1_ref[...]
    h_scr[...] = h.astype(jnp.bfloat16)
    rows = x.shape[0]
    pos = pos_ref[...].astype(jnp.float32)

    lane = lax.broadcasted_iota(jnp.int32, (rows, LANES), 1)
    ang_a = pos * fa_ref[...]
    cos_a, sin_a = jnp.cos(ang_a), jnp.sin(ang_a)
    lane64 = lane % DA_HALF
    half_a = ROPE_DIM // 2
    c_a = jnp.where(lane64 < ROPE_DIM, cos_a, 1.0)
    s_lo = jnp.where(lane64 < half_a, -sin_a, 0.0)
    s_hi = jnp.where((lane64 >= half_a) & (lane64 < ROPE_DIM), sin_a, 0.0)
    c_a2 = jnp.concatenate([c_a, c_a], axis=1)
    s_lo2 = jnp.concatenate([s_lo, s_lo], axis=1)
    s_hi2 = jnp.concatenate([s_hi, s_hi], axis=1)
    ang_r = pos * fr_ref[...]
    cos_r, sin_r = jnp.cos(ang_r), jnp.sin(ang_r)
    s_r = jnp.where(lane < RET_KDIM // 2, -sin_r, sin_r)
    c_r2 = jnp.concatenate([cos_r, cos_r], axis=1)
    s_r2 = jnp.concatenate([s_r, s_r], axis=1)

    def qk_norm_rope(y, g, scale):
        ss = y * y
        hi = ss.astype(jnp.bfloat16)
        lo = (ss - hi.astype(jnp.float32)).astype(jnp.bfloat16)
        gs = _dot(hi, gsum_ref[...]) + _dot(lo, gsum_ref[...])
        n = y * lax.rsqrt(gs * (1.0 / DA_HALF) + EPS) * g
        up = pltpu.roll(n, PROJ_CHUNK - half_a, axis=1)
        dn = pltpu.roll(n, half_a, axis=1)
        r = n * c_a2 + up * s_lo2 + dn * s_hi2
        return r * scale if scale != 1.0 else r

    def ret_rope(y, scale):
        halves = [pltpu.roll(y[:, i * LANES:(i + 1) * LANES], RET_KDIM // 2, axis=1)
                  for i in range(PROJ_CHUNK // LANES)]
        sw = jnp.concatenate(halves, axis=1)
        r = y * c_r2 + sw * s_r2
        return r * scale if scale != 1.0 else r

    for c in range(IN_COLS // PROJ_CHUNK):
        c0 = c * PROJ_CHUNK
        y = _dot(h_scr[...], w_ref[:, c0:c0 + PROJ_CHUNK])
        if c0 < COL_KA:
            y = qk_norm_rope(y, gq_ref[...], DA_HALF ** -0.5)
        elif c0 < COL_VA:
            y = qk_norm_rope(y, gk_ref[...], 1.0)
        elif c0 < COL_QR:
            pass
        elif c0 < COL_KR:
            y = ret_rope(y, 1.0)
        elif c0 < COL_VR:
            y = ret_rope(y, RET_KDIM ** -0.5)
        elif c0 < COL_GB:
            pass
        elif c0 < COL_GATE_A:
            y = y * _sigmoid(y)
        else:
            y = _sigmoid(y)
        o_ref[:, c0:c0 + PROJ_CHUNK] = y.astype(o_ref.dtype)


def _in_proj(x2, pos2, g1, w_in, gq, gk):
    n = x2.shape[0]
    tm = min(PROJ_ROWS, n)
    grp = jnp.arange(PROJ_CHUNK) // DA_HALF
    gsum = (grp[:, None] == grp[None, :]).astype(jnp.bfloat16)
    half_a = ROPE_DIM // 2
    freq_a = jnp.power(jnp.float32(ROPE_THETA), -2.0 * jnp.arange(half_a, dtype=jnp.float32) / ROPE_DIM)
    fa = freq_a[jnp.arange(LANES) % half_a][None, :]
    half_r = RET_KDIM // 2
    freq_r = jnp.power(jnp.float32(RET_THETA), -2.0 * jnp.arange(half_r, dtype=jnp.float32) / RET_KDIM)
    fr = freq_r[jnp.arange(LANES) % half_r][None, :]
    reps = PROJ_CHUNK // DA_HALF
    full = lambda shape: pl.BlockSpec(shape, lambda i: (0,) * len(shape))
    return pl.pallas_call(
        _in_proj_kernel,
        grid=(n // tm,),
        in_specs=[
            pl.BlockSpec((tm, D_MODEL), lambda i: (i, 0)),
            pl.BlockSpec((tm, 1), lambda i: (i, 0)),
            full((1, D_MODEL)),
            full((D_MODEL, IN_COLS)),
            full((PROJ_CHUNK, PROJ_CHUNK)),
            full((1, PROJ_CHUNK)),
            full((1, PROJ_CHUNK)),
            full((1, LANES)),
            full((1, LANES)),
        ],
        out_specs=pl.BlockSpec((tm, IN_COLS), lambda i: (i, 0)),
        out_shape=jax.ShapeDtypeStruct((n, IN_COLS), jnp.bfloat16),
        scratch_shapes=[pltpu.VMEM((tm, D_MODEL), jnp.bfloat16)],
        compiler_params=pltpu.CompilerParams(dimension_semantics=("arbitrary",),
                                             vmem_limit_bytes=VMEM_LIMIT),
        name="in_proj",
    )(x2, pos2, g1.reshape(1, D_MODEL), w_in.astype(jnp.bfloat16), gsum,
      jnp.tile(gq, reps)[None, :], jnp.tile(gk, reps)[None, :], fa, fr)


def _diff_attn_kernel(q_ref, k_ref, v_ref, lam_ref, gsub_ref, o_ref, m_scr, l_scr, acc_scr):
    i = pl.program_id(2)
    tq = q_ref.shape[0]
    q = q_ref[...]
    lane = lax.broadcasted_iota(jnp.int32, q.shape, 1)
    zero = jnp.zeros_like(q)
    qs = jnp.concatenate([jnp.where(lane < DA_HALF, q, zero), jnp.where(lane >= DA_HALF, q, zero)], axis=0)
    m_scr[...] = jnp.full(m_scr.shape, -jnp.inf, jnp.float32)
    l_scr[...] = jnp.zeros(l_scr.shape, jnp.float32)
    acc_scr[...] = jnp.zeros(acc_scr.shape, jnp.float32)

    def step(j, masked):
        start = pl.multiple_of(j * tq, tq)
        k = k_ref[pl.ds(start, tq), :]
        v = v_ref[pl.ds(start, tq), :]
        s = _dot_nt(qs, k)
        if masked:
            r = lax.broadcasted_iota(jnp.int32, s.shape, 0) % tq
            c = lax.broadcasted_iota(jnp.int32, s.shape, 1)
            s = jnp.where(c <= r, s, -jnp.inf)
        m_old = m_scr[...]
        m_new = jnp.maximum(m_old, jnp.max(s, axis=-1, keepdims=True))
        alpha = jnp.exp(m_old - m_new)
        p = jnp.exp(s - m_new)
        l_scr[...] = alpha * l_scr[...] + jnp.sum(p, axis=-1, keepdims=True)
        acc_scr[...] = alpha * acc_scr[...] + _dot(p.astype(jnp.bfloat16), v)
        m_scr[...] = m_new

    def body(j, carry):
        step(j, False)
        return carry

    lax.fori_loop(0, i, body, 0)
    step(i, True)

    lam4 = lam_ref[...]
    lam = (jnp.exp(jnp.sum(lam4[0:1] * lam4[1:2], axis=-1, keepdims=True))
           - jnp.exp(jnp.sum(lam4[2:3] * lam4[3:4], axis=-1, keepdims=True)) + LAMBDA_INIT)
    o_all = acc_scr[...] / l_scr[...]
    o = o_all[:tq] - lam * o_all[tq:]
    o = o * lax.rsqrt(jnp.mean(o * o, axis=-1, keepdims=True) + EPS) * gsub_ref[...] * (1.0 - LAMBDA_INIT)
    o_ref[...] = o.astype(o_ref.dtype)


def _diff_attn(proj, lam4, gsub, batch, seq):
    n = proj.shape[0]
    tq = min(ATT_TQ, seq)
    nq = seq // tq
    qb, kb, vb = COL_QA // LANES, COL_KA // LANES, COL_VA // LANES
    return pl.pallas_call(
        _diff_attn_kernel,
        grid=(batch, DA_HEADS, nq),
        in_specs=[
            pl.BlockSpec((tq, LANES), lambda b, h, i: (b * nq + i, qb + h)),
            pl.BlockSpec((seq, LANES), lambda b, h, i: (b, kb + h)),
            pl.BlockSpec((seq, LANES), lambda b, h, i: (b, vb + h)),
            pl.BlockSpec((4, LANES), lambda b, h, i: (0, 0)),
            pl.BlockSpec((1, LANES), lambda b, h, i: (0, 0)),
        ],
        out_specs=pl.BlockSpec((tq, LANES), lambda b, h, i: (b * nq + i, h)),
        out_shape=jax.ShapeDtypeStruct((n, DA_WIDTH), jnp.bfloat16),
        scratch_shapes=[pltpu.VMEM((2 * tq, 1), jnp.float32),
                        pltpu.VMEM((2 * tq, 1), jnp.float32),
                        pltpu.VMEM((2 * tq, DA_VDIM), jnp.float32)],
        compiler_params=pltpu.CompilerParams(dimension_semantics=("arbitrary",) * 3,
                                             vmem_limit_bytes=VMEM_LIMIT),
        name="diff_attn",
    )(proj, proj, proj, lam4, gsub)


def _retention_kernel(q_ref, k_ref, v_ref, g_ref, gng_ref, gnb_ref, o_ref, r_scr, *, chunk):
    hf = jnp.full((1, 1), pl.program_id(1), jnp.int32).astype(jnp.float32)
    log_g = jnp.log1p(-jnp.exp2(-5.0 - hf))
    ri = lax.broadcasted_iota(jnp.int32, (chunk, chunk), 0)
    ci = lax.broadcasted_iota(jnp.int32, (chunk, chunk), 1)
    rel = (ri - ci).astype(jnp.float32)
    dmask = jnp.where(rel >= 0, jnp.exp(jnp.maximum(rel, 0.0) * log_g), 0.0)
    idx = lax.broadcasted_iota(jnp.int32, (chunk, 1), 0).astype(jnp.float32)
    zeta = jnp.exp((chunk - 1 - idx) * log_g)
    xi = jnp.exp((idx + 1.0) * log_g)
    g_chunk = jnp.exp(chunk * log_g)
    r_scr[...] = jnp.zeros(r_scr.shape, jnp.float32)
    gng = gng_ref[...]
    gnb = gnb_ref[...]

    def body(n, carry):
        start = pl.multiple_of(n * chunk, chunk)
        q = q_ref[pl.ds(start, chunk), :]
        k = k_ref[pl.ds(start, chunk), :]
        v = v_ref[pl.ds(start, chunk), :]
        s = _dot_nt(q, k) * dmask
        r_old = r_scr[...]
        o = _dot(s.astype(jnp.bfloat16), v) + xi * _dot(q, r_old.astype(jnp.bfloat16))
        kz = (k.astype(jnp.float32) * zeta).astype(jnp.bfloat16)
        r_scr[...] = g_chunk * r_old + _dot_tn(kz, v)
        mu = jnp.mean(o, axis=-1, keepdims=True)
        d = o - mu
        var = jnp.mean(d * d, axis=-1, keepdims=True)
        y = d * lax.rsqrt(var + EPS) * gng + gnb
        y = y * g_ref[pl.ds(start, chunk), :].astype(jnp.float32)
        o_ref[pl.ds(start, chunk), :] = y.astype(o_ref.dtype)
        return carry

    lax.fori_loop(0, q_ref.shape[0] // chunk, body, 0)


def _retention(proj, gn_g, gn_b, batch, seq):
    n = proj.shape[0]
    chunk = min(RET_CHUNK, seq)
    col = lambda c0: (lambda b, h: (b, c0 // LANES + h))
    return pl.pallas_call(
        functools.partial(_retention_kernel, chunk=chunk),
        grid=(batch, RET_HEADS),
        in_specs=[
            pl.BlockSpec((seq, LANES), col(COL_QR)),
            pl.BlockSpec((seq, LANES), col(COL_KR)),
            pl.BlockSpec((seq, LANES), col(COL_VR)),
            pl.BlockSpec((seq, LANES), col(COL_GB)),
            pl.BlockSpec((1, LANES), lambda b, h: (0, h)),
            pl.BlockSpec((1, LANES), lambda b, h: (0, h)),
        ],
        out_specs=pl.BlockSpec((seq, LANES), lambda b, h: (b, h)),
        out_shape=jax.ShapeDtypeStruct((n, RET_WIDTH), jnp.bfloat16),
        scratch_shapes=[pltpu.VMEM((RET_KDIM, RET_VDIM), jnp.float32)],
        compiler_params=pltpu.CompilerParams(dimension_semantics=("arbitrary",) * 2,
                                             vmem_limit_bytes=VMEM_LIMIT),
        name="retention",
    )(proj, proj, proj, proj, gn_g.reshape(1, RET_WIDTH), gn_b.reshape(1, RET_WIDTH))


def _merge_kernel(x_ref, oa_ref, ob_ref, sa0_ref, sa1_ref, sb0_ref, sb1_ref, wa_ref, wb_ref, wo_ref,
                  g2_ref, wr_hi_ref, wr_lo_ref, br_ref, x1_ref, h2_ref, route_ref):
    ya = _dot(oa_ref[...], wa_ref[...])
    yb = _dot(ob_ref[...], wb_ref[...])
    sa = jnp.concatenate([sa0_ref[...], sa1_ref[...]], axis=1).astype(jnp.float32)
    sb = jnp.concatenate([sb0_ref[...], sb1_ref[...]], axis=1).astype(jnp.float32)
    merged = sa * ya + sb * yb
    x1 = x_ref[...] + _dot(merged.astype(jnp.bfloat16), wo_ref[...])
    x1_ref[...] = x1
    h2 = x1 * lax.rsqrt(jnp.mean(x1 * x1, axis=-1, keepdims=True) + EPS) * g2_ref[...]
    _store_row_tiles(h2_ref, h2)

    hi = h2.astype(jnp.bfloat16)
    lo = (h2 - hi.astype(jnp.float32)).astype(jnp.bfloat16)
    logits = (_dot(hi, wr_hi_ref[...]) + _dot(lo, wr_hi_ref[...]) + _dot(hi, wr_lo_ref[...])
              + br_ref[...])
    lane = lax.broadcasted_iota(jnp.int32, logits.shape, 1)
    neg = -jnp.inf
    gl = jnp.where(lane < N_GROUPS, logits, neg)
    gmax = jnp.max(gl, axis=-1, keepdims=True)
    g_idx = jnp.min(jnp.where(gl == gmax, lane, LANES), axis=-1, keepdims=True)
    p_g = 1.0 / jnp.sum(jnp.exp(gl - gmax), axis=-1, keepdims=True)
    e_lo = N_GROUPS + EXPERTS_PER_GROUP * g_idx
    el = jnp.where((lane >= e_lo) & (lane < e_lo + EXPERTS_PER_GROUP), logits, neg)
    v1 = jnp.max(el, axis=-1, keepdims=True)
    i1 = jnp.min(jnp.where(el == v1, lane, LANES), axis=-1, keepdims=True)
    el2 = jnp.where(lane == i1, neg, el)
    v2 = jnp.max(el2, axis=-1, keepdims=True)
    i2 = jnp.min(jnp.where(el2 == v2, lane, LANES), axis=-1, keepdims=True)
    t = jnp.exp(v2 - v1)
    w1 = p_g / (1.0 + t)
    w2 = p_g * t / (1.0 + t)
    e1 = (i1 - N_GROUPS).astype(jnp.float32)
    e2 = (i2 - N_GROUPS).astype(jnp.float32)
    route_ref[...] = jnp.where(lane == 0, e1, jnp.where(lane == 1, e2, jnp.where(lane == 2, w1, w2)))


def _merge(x2, oa, ob, proj, wa, wb, wo, g2, w_gr, b_gr, w_er, b_er):
    n = x2.shape[0]
    tm = min(PROJ_ROWS, n)
    half = D_MODEL // 2
    wr = jnp.zeros((D_MODEL, LANES), jnp.float32)
    wr = wr.at[:, :N_GROUPS].set(w_gr).at[:, N_GROUPS:N_GROUPS + N_EXPERTS].set(w_er)
    wr_hi = wr.astype(jnp.bfloat16)
    wr_lo = (wr - wr_hi.astype(jnp.float32)).astype(jnp.bfloat16)
    br = jnp.zeros((1, LANES), jnp.float32)
    br = br.at[0, :N_GROUPS].set(b_gr).at[0, N_GROUPS:N_GROUPS + N_EXPERTS].set(b_er)
    full = lambda shape: pl.BlockSpec(shape, lambda i: (0,) * len(shape))
    gate = lambda c0: pl.BlockSpec((tm, half), lambda i: (i, c0 // half))
    return pl.pallas_call(
        _merge_kernel,
        grid=(n // tm,),
        in_specs=[
            pl.BlockSpec((tm, D_MODEL), lambda i: (i, 0)),
            pl.BlockSpec((tm, DA_WIDTH), lambda i: (i, 0)),
            pl.BlockSpec((tm, RET_WIDTH), lambda i: (i, 0)),
            gate(COL_GATE_A), gate(COL_GATE_A + half), gate(COL_GATE_B), gate(COL_GATE_B + half),
            full((DA_WIDTH, D_MODEL)), full((RET_WIDTH, D_MODEL)), full((D_MODEL, D_MODEL)),
            full((1, D_MODEL)), full((D_MODEL, LANES)), full((D_MODEL, LANES)), full((1, LANES)),
        ],
        out_specs=[
            pl.BlockSpec((tm, D_MODEL), lambda i: (i, 0)),
            pl.BlockSpec((tm, ROW_TILES, LANES), lambda i: (i, 0, 0)),
            pl.BlockSpec((tm, LANES), lambda i: (i, 0)),
        ],
        out_shape=[
            jax.ShapeDtypeStruct((n, D_MODEL), jnp.float32),
            jax.ShapeDtypeStruct((n, ROW_TILES, LANES), jnp.float32),
            jax.ShapeDtypeStruct((n, LANES), jnp.float32),
        ],
        compiler_params=pltpu.CompilerParams(dimension_semantics=("arbitrary",),
                                             vmem_limit_bytes=VMEM_LIMIT),
        name="merge",
    )(x2, oa, ob, proj, proj, proj, proj, wa.astype(jnp.bfloat16), wb.astype(jnp.bfloat16),
      wo.astype(jnp.bfloat16), g2.reshape(1, D_MODEL), wr_hi, wr_lo, br)


def _start_row_gather(src_hbm, idx_ref, idx_base, dst_vmem, sem, count):
    def issue(r, carry):
        pltpu.make_async_copy(src_hbm.at[idx_ref[idx_base + r]], dst_vmem.at[r], sem).start()
        return carry
    lax.fori_loop(0, count, issue, 0, unroll=8)


def _wait_row_gather(src_hbm, dst_vmem, sem, count):
    def drain(r, carry):
        pltpu.make_async_copy(src_hbm.at[0], dst_vmem.at[r], sem).wait()
        return carry
    lax.fori_loop(0, count, drain, 0, unroll=8)


def _expert_kernel(blk_e_ref, n_used_ref, row_tok_ref, h2_hbm, wg_ref, wu_ref, wd_ref, o_ref, x_buf, sem):
    i = pl.program_id(0)
    n_used = n_used_ref[0]
    slot = i % 2

    def start(blk, s):
        _start_row_gather(h2_hbm, row_tok_ref, blk * MOE_BLOCK, x_buf.at[s], sem.at[s], MOE_BLOCK)

    @pl.when(i == 0)
    def _():
        start(0, 0)

    @pl.when(i + 1 < n_used)
    def _():
        start(i + 1, 1 - slot)

    @pl.when(i < n_used)
    def _():
        _wait_row_gather(h2_hbm, x_buf.at[slot], sem.at[slot], MOE_BLOCK)
        x = _load_row_tiles(x_buf.at[slot]).astype(jnp.bfloat16)
        a = _dot(x, wg_ref[0])
        u = _dot(x, wu_ref[0])
        hmid = (a * _sigmoid(a) * u).astype(jnp.bfloat16)
        _store_row_tiles(o_ref, _dot(hmid, wd_ref[0]))

    @pl.when(i >= n_used)
    def _():
        o_ref[...] = jnp.zeros(o_ref.shape, o_ref.dtype)


def _experts(h2, row_tok, blk_expert, n_used, w_gate, w_up, w_down):
    p = row_tok.shape[0]
    nblk = p // MOE_BLOCK
    return pl.pallas_call(
        _expert_kernel,
        grid_spec=pltpu.PrefetchScalarGridSpec(
            num_scalar_prefetch=3,
            grid=(nblk,),
            in_specs=[
                pl.BlockSpec(memory_space=pl.ANY),
                pl.BlockSpec((1, D_MODEL, EXPERT_FF), lambda i, be, nu, rt: (be[i], 0, 0)),
                pl.BlockSpec((1, D_MODEL, EXPERT_FF), lambda i, be, nu, rt: (be[i], 0, 0)),
                pl.BlockSpec((1, EXPERT_FF, D_MODEL), lambda i, be, nu, rt: (be[i], 0, 0)),
            ],
            out_specs=pl.BlockSpec((MOE_BLOCK, ROW_TILES, LANES), lambda i, be, nu, rt: (i, 0, 0)),
            scratch_shapes=[pltpu.VMEM((2, MOE_BLOCK, ROW_TILES, LANES), jnp.float32),
                            pltpu.SemaphoreType.DMA((2,))],
        ),
        out_shape=jax.ShapeDtypeStruct((p, ROW_TILES, LANES), jnp.float32),
        compiler_params=pltpu.CompilerParams(dimension_semantics=("arbitrary",),
                                             vmem_limit_bytes=VMEM_LIMIT),
        name="experts",
    )(blk_expert, n_used, row_tok, h2, w_gate.astype(jnp.bfloat16), w_up.astype(jnp.bfloat16),
      w_down.astype(jnp.bfloat16))


def _combine_kernel(dest_ref, x1_ref, route_ref, ys_hbm, o_ref, y_buf, sem, *, rows):
    i = pl.program_id(0)
    slot = i % 2

    def start(step, s):
        for k in range(TOP_K):
            _start_row_gather(ys_hbm, dest_ref, k * pl.num_programs(0) * rows + step * rows,
                              y_buf.at[s, k], sem.at[s], rows)

    @pl.when(i == 0)
    def _():
        start(0, 0)

    @pl.when(i + 1 < pl.num_programs(0))
    def _():
        start(i + 1, 1 - slot)

    for k in range(TOP_K):
        _wait_row_gather(ys_hbm, y_buf.at[slot, k], sem.at[slot], rows)
    route = route_ref[...]
    o_ref[...] = (x1_ref[...] + route[:, 2:3] * _load_row_tiles(y_buf.at[slot, 0])
                  + route[:, 3:4] * _load_row_tiles(y_buf.at[slot, 1]))


def _combine(x1, ys, route, dest_kmajor):
    n = x1.shape[0]
    tm = min(COMBINE_ROWS, n)
    return pl.pallas_call(
        functools.partial(_combine_kernel, rows=tm),
        grid_spec=pltpu.PrefetchScalarGridSpec(
            num_scalar_prefetch=1,
            grid=(n // tm,),
            in_specs=[
                pl.BlockSpec((tm, D_MODEL), lambda i, d: (i, 0)),
                pl.BlockSpec((tm, LANES), lambda i, d: (i, 0)),
                pl.BlockSpec(memory_space=pl.ANY),
            ],
            out_specs=pl.BlockSpec((tm, D_MODEL), lambda i, d: (i, 0)),
            scratch_shapes=[pltpu.VMEM((2, TOP_K, tm, ROW_TILES, LANES), jnp.float32),
                            pltpu.SemaphoreType.DMA((2,))],
        ),
        out_shape=jax.ShapeDtypeStruct((n, D_MODEL), jnp.float32),
        compiler_params=pltpu.CompilerParams(dimension_semantics=("arbitrary",),
                                             vmem_limit_bytes=VMEM_LIMIT),
        name="combine",
    )(dest_kmajor, x1, route, ys)


def _dispatch_plan(route, n):
    e_flat = route[:, :TOP_K].astype(jnp.int32).T.reshape(-1)
    onehot = (e_flat[:, None] == jnp.arange(N_EXPERTS, dtype=jnp.int32)[None, :]).astype(jnp.int32)
    csum = jnp.cumsum(onehot, axis=0)
    counts = csum[-1]
    rank = jnp.sum((csum - onehot) * onehot, axis=1)
    padded = ((counts + MOE_BLOCK - 1) // MOE_BLOCK) * MOE_BLOCK
    pad_end = jnp.cumsum(padded)
    pad_start = pad_end - padded
    dest = (pad_start[e_flat] + rank).astype(jnp.int32)
    p = n * TOP_K + N_EXPERTS * MOE_BLOCK
    tok = jnp.arange(n * TOP_K, dtype=jnp.int32) % n
    row_tok = jnp.zeros((p,), jnp.int32).at[dest].set(tok)
    blk_start = jnp.arange(p // MOE_BLOCK, dtype=jnp.int32) * MOE_BLOCK
    blk_expert = jnp.clip(jnp.searchsorted(pad_end, blk_start, side='right'), 0, N_EXPERTS - 1)
    n_used = (pad_end[-1] // MOE_BLOCK).astype(jnp.int32).reshape(1)
    return dest, row_tok, blk_expert.astype(jnp.int32), n_used


def _layer(x, positions, norm1_g, w_in, q_norm_g, k_norm_g, lam4, diff_subln_g, ret_gn_g, ret_gn_b,
           w_branch_a, w_branch_b, w_out, norm2_g, w_gr, b_gr, w_er, b_er, w_gate, w_up, w_down):
    batch, seq, _ = x.shape
    n = batch * seq
    x2 = x.reshape(n, D_MODEL)
    proj = _in_proj(x2, positions.reshape(n, 1), norm1_g, w_in, q_norm_g, k_norm_g)
    oa = _diff_attn(proj, lam4, diff_subln_g.reshape(1, DA_VDIM), batch, seq)
    ob = _retention(proj, ret_gn_g, ret_gn_b, batch, seq)
    x1, h2, route = _merge(x2, oa, ob, proj, w_branch_a, w_branch_b, w_out, norm2_g, w_gr, b_gr, w_er, b_er)
    dest, row_tok, blk_expert, n_used = _dispatch_plan(route, n)
    ys = _experts(h2, row_tok, blk_expert, n_used, w_gate, w_up, w_down)
    out = _combine(x1, ys, route, dest)
    return out.reshape(batch, seq, D_MODEL)


def kernel(x, positions, norm1_g, w_in, q_norm_g, k_norm_g, lambda_q1, lambda_k1, lambda_q2, lambda_k2, diff_subln_g, ret_gn_g, ret_gn_b, w_branch_a, w_branch_b, w_out, norm2_g, w_group_router, b_group_router, w_expert_router, b_expert_router, w_gate, w_up, w_down):
    assert x.shape[-1] == D_MODEL and norm1_g.shape[0] == 1, "single-layer, D_MODEL-wide input expected"
    lam4 = jnp.zeros((4, LANES), jnp.float32)
    lam4 = lam4.at[:, :DA_HALF].set(jnp.stack([lambda_q1[0], lambda_k1[0], lambda_q2[0], lambda_k2[0]]))
    return _layer(x, positions, norm1_g[0], w_in[0], q_norm_g[0], k_norm_g[0], lam4, diff_subln_g[0],
                  ret_gn_g[0], ret_gn_b[0], w_branch_a[0], w_branch_b[0], w_out[0], norm2_g[0],
                  w_group_router[0], b_group_router[0], w_expert_router[0], b_expert_router[0],
                  w_gate[0], w_up[0], w_down[0])
```

```python
import functools
import math

import jax
import jax.numpy as jnp
from jax import lax
from jax.experimental import pallas as pl
from jax.experimental.pallas import tpu as pltpu

D_MODEL = 1024
DA_HEADS = 4
DA_HALF = 64
DA_VDIM = 2 * DA_HALF
DA_WIDTH = DA_HEADS * DA_VDIM
ROPE_THETA = 500000.0
ROPE_DIM = DA_HALF // 4
RET_HEADS = 4
RET_KDIM = 128
RET_VDIM = 128
RET_WIDTH = RET_HEADS * RET_VDIM
RET_THETA = 10000.0
N_GROUPS = 4
EXPERTS_PER_GROUP = 8
N_EXPERTS = N_GROUPS * EXPERTS_PER_GROUP
TOP_K = 2
EXPERT_FF = 512
EPS = 1e-6
LAMBDA_INIT = 0.8 - 0.6 * math.exp(-0.3 * 0)

LANES = 128
IN_COLS = 3 * DA_WIDTH + 4 * RET_WIDTH + 2 * D_MODEL
COL_QA, COL_KA, COL_VA = 0, DA_WIDTH, 2 * DA_WIDTH
COL_QR = 3 * DA_WIDTH
COL_KR = COL_QR + RET_WIDTH
COL_VR = COL_KR + RET_WIDTH
COL_GB = COL_VR + RET_WIDTH
COL_GATE_A = COL_GB + RET_WIDTH
COL_GATE_B = COL_GATE_A + D_MODEL

PROJ_ROWS = 512
PROJ_CHUNK = 256
ATT_TILE = 512
ATT_ROWS = 32
RET_CHUNK = 256
MOE_BLOCK = 256
COMBINE_ROWS = 256
VMEM_LIMIT = 56 * 1024 * 1024


def _dot(a, b):
    return jnp.dot(a, b, preferred_element_type=jnp.float32)


def _dot_nt(a, b):
    return lax.dot_general(a, b, (((1,), (1,)), ((), ())), preferred_element_type=jnp.float32)


def _dot_tn(a, b):
    return lax.dot_general(a, b, (((0,), (0,)), ((), ())), preferred_element_type=jnp.float32)


def _sigmoid(x):
    return 1.0 / (1.0 + jnp.exp(-x))


ROW_TILES = D_MODEL // LANES


def _store_row_tiles(ref, val):
    for j in range(ROW_TILES):
        ref[:, j, :] = val[:, j * LANES:(j + 1) * LANES].astype(ref.dtype)


def _load_row_tiles(ref):
    return jnp.concatenate([ref[:, j, :] for j in range(ROW_TILES)], axis=1)


def _in_proj_kernel(x_ref, pos_ref, g1_ref, w_ref, gsum_ref, gq_ref, gk_ref, fa_ref, fr_ref,
                    o_ref, h_scr):
    x = x_ref[...]
    h = x * lax.rsqrt(jnp.mean(x * x, axis=-1, keepdims=True) + EPS) * g1_ref[...]
    h_scr[...] = h.astype(jnp.bfloat16)
    rows = x.shape[0]
    pos = pos_ref[...].astype(jnp.float32)

    lane = lax.broadcasted_iota(jnp.int32, (rows, LANES), 1)
    ang_a = pos * fa_ref[...]
    cos_a, sin_a = jnp.cos(ang_a), jnp.sin(ang_a)
    lane64 = lane % DA_HALF
    half_a = ROPE_DIM // 2
    c_a = jnp.where(lane64 < ROPE_DIM, cos_a, 1.0)
    s_lo = jnp.where(lane64 < half_a, -sin_a, 0.0)
    s_hi = jnp.where((lane64 >= half_a) & (lane64 < ROPE_DIM), sin_a, 0.0)
    c_a2 = jnp.concatenate([c_a, c_a], axis=1)
    s_lo2 = jnp.concatenate([s_lo, s_lo], axis=1)
    s_hi2 = jnp.concatenate([s_hi, s_hi], axis=1)
    ang_r = pos * fr_ref[...]
    cos_r, sin_r = jnp.cos(ang_r), jnp.sin(ang_r)
    s_r = jnp.where(lane < RET_KDIM // 2, -sin_r, sin_r)
    c_r2 = jnp.concatenate([cos_r, cos_r], axis=1)
    s_r2 = jnp.concatenate([s_r, s_r], axis=1)

    def qk_norm_rope(y, g, scale):
        ss = y * y
        hi = ss.astype(jnp.bfloat16)
        lo = (ss - hi.astype(jnp.float32)).astype(jnp.bfloat16)
        gs = _dot(hi, gsum_ref[...]) + _dot(lo, gsum_ref[...])
        n = y * lax.rsqrt(gs * (1.0 / DA_HALF) + EPS) * g
        up = pltpu.roll(n, PROJ_CHUNK - half_a, axis=1)
        dn = pltpu.roll(n, half_a, axis=1)
        r = n * c_a2 + up * s_lo2 + dn * s_hi2
        return r * scale if scale != 1.0 else r

    def ret_rope(y, scale):
        halves = [pltpu.roll(y[:, i * LANES:(i + 1) * LANES], RET_KDIM // 2, axis=1)
                  for i in range(PROJ_CHUNK // LANES)]
        sw = jnp.concatenate(halves, axis=1)
        r = y * c_r2 + sw * s_r2
        return r * scale if scale != 1.0 else r

    for c in range(IN_COLS // PROJ_CHUNK):
        c0 = c * PROJ_CHUNK
        y = _dot(h_scr[...], w_ref[:, c0:c0 + PROJ_CHUNK])
        if c0 < COL_KA:
            y = qk_norm_rope(y, gq_ref[...], DA_HALF ** -0.5)
        elif c0 < COL_VA:
            y = qk_norm_rope(y, gk_ref[...], 1.0)
        elif c0 < COL_QR:
            pass
        elif c0 < COL_KR:
            y = ret_rope(y, 1.0)
        elif c0 < COL_VR:
            y = ret_rope(y, RET_KDIM ** -0.5)
        elif c0 < COL_GB:
            pass
        elif c0 < COL_GATE_A:
            y = y * _sigmoid(y)
        else:
            y = _sigmoid(y)
        o_ref[:, c0:c0 + PROJ_CHUNK] = y.astype(o_ref.dtype)


def _in_proj(x2, pos2, g1, w_in, gq, gk):
    n = x2.shape[0]
    tm = min(PROJ_ROWS, n)
    grp = jnp.arange(PROJ_CHUNK) // DA_HALF
    gsum = (grp[:, None] == grp[None, :]).astype(jnp.bfloat16)
    half_a = ROPE_DIM // 2
    freq_a = jnp.power(jnp.float32(ROPE_THETA), -2.0 * jnp.arange(half_a, dtype=jnp.float32) / ROPE_DIM)
    fa = freq_a[jnp.arange(LANES) % half_a][None, :]
    half_r = RET_KDIM // 2
    freq_r = jnp.power(jnp.float32(RET_THETA), -2.0 * jnp.arange(half_r, dtype=jnp.float32) / RET_KDIM)
    fr = freq_r[jnp.arange(LANES) % half_r][None, :]
    reps = PROJ_CHUNK // DA_HALF
    full = lambda shape: pl.BlockSpec(shape, lambda i: (0,) * len(shape))
    return pl.pallas_call(
        _in_proj_kernel,
        grid=(n // tm,),
        in_specs=[
            pl.BlockSpec((tm, D_MODEL), lambda i: (i, 0)),
            pl.BlockSpec((tm, 1), lambda i: (i, 0)),
            full((1, D_MODEL)),
            full((D_MODEL, IN_COLS)),
            full((PROJ_CHUNK, PROJ_CHUNK)),
            full((1, PROJ_CHUNK)),
            full((1, PROJ_CHUNK)),
            full((1, LANES)),
            full((1, LANES)),
        ],
        out_specs=pl.BlockSpec((tm, IN_COLS), lambda i: (i, 0)),
        out_shape=jax.ShapeDtypeStruct((n, IN_COLS), jnp.bfloat16),
        scratch_shapes=[pltpu.VMEM((tm, D_MODEL), jnp.bfloat16)],
        compiler_params=pltpu.CompilerParams(dimension_semantics=("arbitrary",),
                                             vmem_limit_bytes=VMEM_LIMIT),
        name="in_proj",
    )(x2, pos2, g1.reshape(1, D_MODEL), w_in.astype(jnp.bfloat16), gsum,
      jnp.tile(gq, reps)[None, :], jnp.tile(gk, reps)[None, :], fa, fr)


def _diff_attn_kernel(q_ref, k_ref, v_ref, lam_ref, gsub_ref, o_ref,
                      qs_scr, vx_scr, s0_scr, s1_scr, p_scr, m_scr, alpha_scr, acc_scr):
    i = pl.program_id(2)
    t = q_ref.shape[0]

    @pl.when(i == 0)
    def _():
        vx_scr[:, :DA_VDIM] = v_ref[...]
        vx_scr[:, DA_VDIM:] = jnp.ones((vx_scr.shape[0], LANES), vx_scr.dtype)

    q = q_ref[...]
    lane = lax.broadcasted_iota(jnp.int32, q.shape, 1)
    zero = jnp.zeros_like(q)
    qs_scr[:t] = jnp.where(lane < DA_HALF, q, zero)
    qs_scr[t:] = jnp.where(lane >= DA_HALF, q, zero)
    m_scr[...] = jnp.full(m_scr.shape, -jnp.inf, jnp.float32)
    acc_scr[...] = jnp.zeros(acc_scr.shape, jnp.float32)

    def scores(j, s_ref):
        start = pl.multiple_of(j * t, t)
        s_ref[...] = _dot_nt(qs_scr[...], k_ref[pl.ds(start, t), :])

    def softmax_pv(j, s_ref, masked):
        for c in range(2 * t // ATT_ROWS):
            rows = pl.ds(c * ATT_ROWS, ATT_ROWS)
            s = s_ref[rows, :]
            if masked:
                r = lax.broadcasted_iota(jnp.int32, s.shape, 0) + (c * ATT_ROWS) % t
                col = lax.broadcasted_iota(jnp.int32, s.shape, 1)
                s = jnp.where(col <= r, s, -jnp.inf)
            m_prev = m_scr[rows, :]
            m_new = jnp.maximum(m_prev, jnp.max(s, axis=-1, keepdims=True))
            alpha_scr[rows, :] = jnp.exp(m_prev - m_new)
            m_scr[rows, :] = m_new
            p = jnp.exp(s - pltpu.repeat(m_new, t // LANES, axis=1))
            p_scr[rows, :] = p.astype(p_scr.dtype)
        start = pl.multiple_of(j * t, t)
        pv = _dot(p_scr[...], vx_scr[pl.ds(start, t), :])
        alpha = alpha_scr[...]
        for half in range(2):
            cols = pl.ds(half * LANES, LANES)
            acc_scr[:, cols] = alpha * acc_scr[:, cols] + pv[:, half * LANES:(half + 1) * LANES]

    scores(0, s0_scr)

    def pair(jj, carry):
        j = 2 * jj
        scores(j + 1, s1_scr)
        softmax_pv(j, s0_scr, False)
        scores(j + 2, s0_scr)
        softmax_pv(j + 1, s1_scr, False)
        return carry

    lax.fori_loop(0, i // 2, pair, 0)

    @pl.when(i % 2 == 1)
    def _():
        scores(i, s1_scr)
        softmax_pv(i - 1, s0_scr, False)
        softmax_pv(i, s1_scr, True)

    @pl.when(i % 2 == 0)
    def _():
        softmax_pv(i, s0_scr, True)

    lam4 = lam_ref[...]
    lam = (jnp.exp(jnp.sum(lam4[0:1] * lam4[1:2], axis=-1, keepdims=True))
           - jnp.exp(jnp.sum(lam4[2:3] * lam4[3:4], axis=-1, keepdims=True)) + LAMBDA_INIT)
    o_all = acc_scr[:, :DA_VDIM] / acc_scr[:, DA_VDIM:]
    o = o_all[:t] - lam * o_all[t:]
    o = o * lax.rsqrt(jnp.mean(o * o, axis=-1, keepdims=True) + EPS) * gsub_ref[...] * (1.0 - LAMBDA_INIT)
    o_ref[...] = o.astype(o_ref.dtype)


def _diff_attn(proj, lam4, gsub, batch, seq):
    n = proj.shape[0]
    t = min(ATT_TILE, seq)
    nq = seq // t
    qb, kb, vb = COL_QA // LANES, COL_KA // LANES, COL_VA // LANES
    return pl.pallas_call(
        _diff_attn_kernel,
        grid=(batch, DA_HEADS, nq),
        in_specs=[
            pl.BlockSpec((t, LANES), lambda b, h, i: (b * nq + i, qb + h)),
            pl.BlockSpec((seq, LANES), lambda b, h, i: (b, kb + h)),
            pl.BlockSpec((seq, LANES), lambda b, h, i: (b, vb + h)),
            pl.BlockSpec((4, LANES), lambda b, h, i: (0, 0)),
            pl.BlockSpec((1, LANES), lambda b, h, i: (0, 0)),
        ],
        out_specs=pl.BlockSpec((t, LANES), lambda b, h, i: (b * nq + i, h)),
        out_shape=jax.ShapeDtypeStruct((n, DA_WIDTH), jnp.bfloat16),
        scratch_shapes=[pltpu.VMEM((2 * t, LANES), jnp.bfloat16),
                        pltpu.VMEM((seq, DA_VDIM + LANES), jnp.bfloat16),
                        pltpu.VMEM((2 * t, t), jnp.float32),
                        pltpu.VMEM((2 * t, t), jnp.float32),
                        pltpu.VMEM((2 * t, t), jnp.bfloat16),
                        pltpu.VMEM((2 * t, LANES), jnp.float32),
                        pltpu.VMEM((2 * t, LANES), jnp.float32),
                        pltpu.VMEM((2 * t, DA_VDIM + LANES), jnp.float32)],
        compiler_params=pltpu.CompilerParams(dimension_semantics=("arbitrary",) * 3,
                                             vmem_limit_bytes=VMEM_LIMIT),
        name="diff_attn",
    )(proj, proj, proj, lam4, gsub)


def _retention_kernel(q_ref, k_ref, v_ref, g_ref, gng_ref, gnb_ref, o_ref, r_scr, *, chunk):
    hf = jnp.full((1, 1), pl.program_id(1), jnp.int32).astype(jnp.float32)
    log_g = jnp.log1p(-jnp.exp2(-5.0 - hf))
    ri = lax.broadcasted_iota(jnp.int32, (chunk, chunk), 0)
    ci = lax.broadcasted_iota(jnp.int32, (chunk, chunk), 1)
    rel = (ri - ci).astype(jnp.float32)
    dmask = jnp.where(rel >= 0, jnp.exp(jnp.maximum(rel, 0.0) * log_g), 0.0)
    idx = lax.broadcasted_iota(jnp.int32, (chunk, 1), 0).astype(jnp.float32)
    zeta = jnp.exp((chunk - 1 - idx) * log_g)
    xi = jnp.exp((idx + 1.0) * log_g)
    g_chunk = jnp.exp(chunk * log_g)
    r_scr[...] = jnp.zeros(r_scr.shape, jnp.float32)
    gng = gng_ref[...]
    gnb = gnb_ref[...]

    def body(n, carry):
        start = pl.multiple_of(n * chunk, chunk)
        q = q_ref[pl.ds(start, chunk), :]
        k = k_ref[pl.ds(start, chunk), :]
        v = v_ref[pl.ds(start, chunk), :]
        s = _dot_nt(q, k) * dmask
        r_old = r_scr[...]
        o = _dot(s.astype(jnp.bfloat16), v) + xi * _dot(q, r_old.astype(jnp.bfloat16))
        kz = (k.astype(jnp.float32) * zeta).astype(jnp.bfloat16)
        r_scr[...] = g_chunk * r_old + _dot_tn(kz, v)
        mu = jnp.mean(o, axis=-1, keepdims=True)
        d = o - mu
        var = jnp.mean(d * d, axis=-1, keepdims=True)
        y = d * lax.rsqrt(var + EPS) * gng + gnb
        y = y * g_ref[pl.ds(start, chunk), :].astype(jnp.float32)
        o_ref[pl.ds(start, chunk), :] = y.astype(o_ref.dtype)
        return carry

    lax.fori_loop(0, q_ref.shape[0] // chunk, body, 0)


def _retention(proj, gn_g, gn_b, batch, seq):
    n = proj.shape[0]
    chunk = min(RET_CHUNK, seq)
    col = lambda c0: (lambda b, h: (b, c0 // LANES + h))
    return pl.pallas_call(
        functools.partial(_retention_kernel, chunk=chunk),
        grid=(batch, RET_HEADS),
        in_specs=[
            pl.BlockSpec((seq, LANES), col(COL_QR)),
            pl.BlockSpec((seq, LANES), col(COL_KR)),
            pl.BlockSpec((seq, LANES), col(COL_VR)),
            pl.BlockSpec((seq, LANES), col(COL_GB)),
            pl.BlockSpec((1, LANES), lambda b, h: (0, h)),
            pl.BlockSpec((1, LANES), lambda b, h: (0, h)),
        ],
        out_specs=pl.BlockSpec((seq, LANES), lambda b, h: (b, h)),
        out_shape=jax.ShapeDtypeStruct((n, RET_WIDTH), jnp.bfloat16),
        scratch_shapes=[pltpu.VMEM((RET_KDIM, RET_VDIM), jnp.float32)],
        compiler_params=pltpu.CompilerParams(dimension_semantics=("arbitrary",) * 2,
                                             vmem_limit_bytes=VMEM_LIMIT),
        name="retention",
    )(proj, proj, proj, proj, gn_g.reshape(1, RET_WIDTH), gn_b.reshape(1, RET_WIDTH))


def _merge_kernel(x_ref, oa_ref, ob_ref, sa0_ref, sa1_ref, sb0_ref, sb1_ref, wa_ref, wb_ref, wo_ref,
                  g2_ref, wr_hi_ref, wr_lo_ref, br_ref, x1_ref, h2_ref, route_ref):
    ya = _dot(oa_ref[...], wa_ref[...])
    yb = _dot(ob_ref[...], wb_ref[...])
    sa = jnp.concatenate([sa0_ref[...], sa1_ref[...]], axis=1).astype(jnp.float32)
    sb = jnp.concatenate([sb0_ref[...], sb1_ref[...]], axis=1).astype(jnp.float32)
    merged = sa * ya + sb * yb
    x1 = x_ref[...] + _dot(merged.astype(jnp.bfloat16), wo_ref[...])
    x1_ref[...] = x1
    h2 = x1 * lax.rsqrt(jnp.mean(x1 * x1, axis=-1, keepdims=True) + EPS) * g2_ref[...]
    _store_row_tiles(h2_ref, h2)

    hi = h2.astype(jnp.bfloat16)
    lo = (h2 - hi.astype(jnp.float32)).astype(jnp.bfloat16)
    logits = (_dot(hi, wr_hi_ref[...]) + _dot(lo, wr_hi_ref[...]) + _dot(hi, wr_lo_ref[...])
              + br_ref[...])
    lane = lax.broadcasted_iota(jnp.int32, logits.shape, 1)
    neg = -jnp.inf
    gl = jnp.where(lane < N_GROUPS, logits, neg)
    gmax = jnp.max(gl, axis=-1, keepdims=True)
    g_idx = jnp.min(jnp.where(gl == gmax, lane, LANES), axis=-1, keepdims=True)
    p_g = 1.0 / jnp.sum(jnp.exp(gl - gmax), axis=-1, keepdims=True)
    e_lo = N_GROUPS + EXPERTS_PER_GROUP * g_idx
    el = jnp.where((lane >= e_lo) & (lane < e_lo + EXPERTS_PER_GROUP), logits, neg)
    v1 = jnp.max(el, axis=-1, keepdims=True)
    i1 = jnp.min(jnp.where(el == v1, lane, LANES), axis=-1, keepdims=True)
    el2 = jnp.where(lane == i1, neg, el)
    v2 = jnp.max(el2, axis=-1, keepdims=True)
    i2 = jnp.min(jnp.where(el2 == v2, lane, LANES), axis=-1, keepdims=True)
    t = jnp.exp(v2 - v1)
    w1 = p_g / (1.0 + t)
    w2 = p_g * t / (1.0 + t)
    e1 = (i1 - N_GROUPS).astype(jnp.float32)
    e2 = (i2 - N_GROUPS).astype(jnp.float32)
    route_ref[...] = jnp.where(lane == 0, e1, jnp.where(lane == 1, e2, jnp.where(lane == 2, w1, w2)))


def _merge(x2, oa, ob, proj, wa, wb, wo, g2, w_gr, b_gr, w_er, b_er):
    n = x2.shape[0]
    tm = min(PROJ_ROWS, n)
    half = D_MODEL // 2
    wr = jnp.zeros((D_MODEL, LANES), jnp.float32)
    wr = wr.at[:, :N_GROUPS].set(w_gr).at[:, N_GROUPS:N_GROUPS + N_EXPERTS].set(w_er)
    wr_hi = wr.astype(jnp.bfloat16)
    wr_lo = (wr - wr_hi.astype(jnp.float32)).astype(jnp.bfloat16)
    br = jnp.zeros((1, LANES), jnp.float32)
    br = br.at[0, :N_GROUPS].set(b_gr).at[0, N_GROUPS:N_GROUPS + N_EXPERTS].set(b_er)
    full = lambda shape: pl.BlockSpec(shape, lambda i: (0,) * len(shape))
    gate = lambda c0: pl.BlockSpec((tm, half), lambda i: (i, c0 // half))
    return pl.pallas_call(
        _merge_kernel,
        grid=(n // tm,),
        in_specs=[
            pl.BlockSpec((tm, D_MODEL), lambda i: (i, 0)),
            pl.BlockSpec((tm, DA_WIDTH), lambda i: (i, 0)),
            pl.BlockSpec((tm, RET_WIDTH), lambda i: (i, 0)),
            gate(COL_GATE_A), gate(COL_GATE_A + half), gate(COL_GATE_B), gate(COL_GATE_B + half),
            full((DA_WIDTH, D_MODEL)), full((RET_WIDTH, D_MODEL)), full((D_MODEL, D_MODEL)),
            full((1, D_MODEL)), full((D_MODEL, LANES)), full((D_MODEL, LANES)), full((1, LANES)),
        ],
        out_specs=[
            pl.BlockSpec((tm, D_MODEL), lambda i: (i, 0)),
            pl.BlockSpec((tm, ROW_TILES, LANES), lambda i: (i, 0, 0)),
            pl.BlockSpec((tm, LANES), lambda i: (i, 0)),
        ],
        out_shape=[
            jax.ShapeDtypeStruct((n, D_MODEL), jnp.float32),
            jax.ShapeDtypeStruct((n, ROW_TILES, LANES), jnp.float32),
            jax.ShapeDtypeStruct((n, LANES), jnp.float32),
        ],
        compiler_params=pltpu.CompilerParams(dimension_semantics=("arbitrary",),
                                             vmem_limit_bytes=VMEM_LIMIT),
        name="merge",
    )(x2, oa, ob, proj, proj, proj, proj, wa.astype(jnp.bfloat16), wb.astype(jnp.bfloat16),
      wo.astype(jnp.bfloat16), g2.reshape(1, D_MODEL), wr_hi, wr_lo, br)


def _start_row_gather(src_hbm, idx_ref, idx_base, dst_vmem, sem, count):
    def issue(r, carry):
        pltpu.make_async_copy(src_hbm.at[idx_ref[idx_base + r]], dst_vmem.at[r], sem).start()
        return carry
    lax.fori_loop(0, count, issue, 0, unroll=8)


def _wait_row_gather(src_hbm, dst_vmem, sem, count):
    def drain(r, carry):
        pltpu.make_async_copy(src_hbm.at[0], dst_vmem.at[r], sem).wait()
        return carry
    lax.fori_loop(0, count, drain, 0, unroll=8)


def _expert_kernel(blk_e_ref, n_used_ref, row_tok_ref, h2_hbm, wg_ref, wu_ref, wd_ref, o_ref, x_buf, sem):
    i = pl.program_id(0)
    n_used = n_used_ref[0]
    slot = i % 2

    def start(blk, s):
        _start_row_gather(h2_hbm, row_tok_ref, blk * MOE_BLOCK, x_buf.at[s], sem.at[s], MOE_BLOCK)

    @pl.when(i == 0)
    def _():
        start(0, 0)

    @pl.when(i + 1 < n_used)
    def _():
        start(i + 1, 1 - slot)

    @pl.when(i < n_used)
    def _():
        _wait_row_gather(h2_hbm, x_buf.at[slot], sem.at[slot], MOE_BLOCK)
        x = _load_row_tiles(x_buf.at[slot]).astype(jnp.bfloat16)
        a = _dot(x, wg_ref[0])
        u = _dot(x, wu_ref[0])
        hmid = (a * _sigmoid(a) * u).astype(jnp.bfloat16)
        _store_row_tiles(o_ref, _dot(hmid, wd_ref[0]))

    @pl.when(i >= n_used)
    def _():
        o_ref[...] = jnp.zeros(o_ref.shape, o_ref.dtype)


def _experts(h2, row_tok, blk_expert, n_used, w_gate, w_up, w_down):
    p = row_tok.shape[0]
    nblk = p // MOE_BLOCK
    return pl.pallas_call(
        _expert_kernel,
        grid_spec=pltpu.PrefetchScalarGridSpec(
            num_scalar_prefetch=3,
            grid=(nblk,),
            in_specs=[
                pl.BlockSpec(memory_space=pl.ANY),
                pl.BlockSpec((1, D_MODEL, EXPERT_FF), lambda i, be, nu, rt: (be[i], 0, 0)),
                pl.BlockSpec((1, D_MODEL, EXPERT_FF), lambda i, be, nu, rt: (be[i], 0, 0)),
                pl.BlockSpec((1, EXPERT_FF, D_MODEL), lambda i, be, nu, rt: (be[i], 0, 0)),
            ],
            out_specs=pl.BlockSpec((MOE_BLOCK, ROW_TILES, LANES), lambda i, be, nu, rt: (i, 0, 0)),
            scratch_shapes=[pltpu.VMEM((2, MOE_BLOCK, ROW_TILES, LANES), jnp.float32),
                            pltpu.SemaphoreType.DMA((2,))],
        ),
        out_shape=jax.ShapeDtypeStruct((p, ROW_TILES, LANES), jnp.float32),
        compiler_params=pltpu.CompilerParams(dimension_semantics=("arbitrary",),
                                             vmem_limit_bytes=VMEM_LIMIT),
        name="experts",
    )(blk_expert, n_used, row_tok, h2, w_gate.astype(jnp.bfloat16), w_up.astype(jnp.bfloat16),
      w_down.astype(jnp.bfloat16))


def _combine_kernel(dest_ref, x1_ref, route_ref, ys_hbm, o_ref, y_buf, sem, *, rows):
    i = pl.program_id(0)
    slot = i % 2

    def start(step, s):
        for k in range(TOP_K):
            _start_row_gather(ys_hbm, dest_ref, k * pl.num_programs(0) * rows + step * rows,
                              y_buf.at[s, k], sem.at[s], rows)

    @pl.when(i == 0)
    def _():
        start(0, 0)

    @pl.when(i + 1 < pl.num_programs(0))
    def _():
        start(i + 1, 1 - slot)

    for k in range(TOP_K):
        _wait_row_gather(ys_hbm, y_buf.at[slot, k], sem.at[slot], rows)
    route = route_ref[...]
    o_ref[...] = (x1_ref[...] + route[:, 2:3] * _load_row_tiles(y_buf.at[slot, 0])
                  + route[:, 3:4] * _load_row_tiles(y_buf.at[slot, 1]))


def _combine(x1, ys, route, dest_kmajor):
    n = x1.shape[0]
    tm = min(COMBINE_ROWS, n)
    return pl.pallas_call(
        functools.partial(_combine_kernel, rows=tm),
        grid_spec=pltpu.PrefetchScalarGridSpec(
            num_scalar_prefetch=1,
            grid=(n // tm,),
            in_specs=[
                pl.BlockSpec((tm, D_MODEL), lambda i, d: (i, 0)),
                pl.BlockSpec((tm, LANES), lambda i, d: (i, 0)),
                pl.BlockSpec(memory_space=pl.ANY),
            ],
            out_specs=pl.BlockSpec((tm, D_MODEL), lambda i, d: (i, 0)),
            scratch_shapes=[pltpu.VMEM((2, TOP_K, tm, ROW_TILES, LANES), jnp.float32),
                            pltpu.SemaphoreType.DMA((2,))],
        ),
        out_shape=jax.ShapeDtypeStruct((n, D_MODEL), jnp.float32),
        compiler_params=pltpu.CompilerParams(dimension_semantics=("arbitrary",),
                                             vmem_limit_bytes=VMEM_LIMIT),
        name="combine",
    )(dest_kmajor, x1, route, ys)


def _dispatch_plan(route, n):
    e_flat = route[:, :TOP_K].astype(jnp.int32).T.reshape(-1)
    onehot = (e_flat[:, None] == jnp.arange(N_EXPERTS, dtype=jnp.int32)[None, :]).astype(jnp.int32)
    csum = jnp.cumsum(onehot, axis=0)
    counts = csum[-1]
    rank = jnp.sum((csum - onehot) * onehot, axis=1)
    padded = ((counts + MOE_BLOCK - 1) // MOE_BLOCK) * MOE_BLOCK
    pad_end = jnp.cumsum(padded)
    pad_start = pad_end - padded
    dest = (pad_start[e_flat] + rank).astype(jnp.int32)
    p = n * TOP_K + N_EXPERTS * MOE_BLOCK
    tok = jnp.arange(n * TOP_K, dtype=jnp.int32) % n
    row_tok = jnp.zeros((p,), jnp.int32).at[dest].set(tok)
    blk_start = jnp.arange(p // MOE_BLOCK, dtype=jnp.int32) * MOE_BLOCK
    blk_expert = jnp.clip(jnp.searchsorted(pad_end, blk_start, side='right'), 0, N_EXPERTS - 1)
    n_used = (pad_end[-1] // MOE_BLOCK).astype(jnp.int32).reshape(1)
    return dest, row_tok, blk_expert.astype(jnp.int32), n_used


def _layer(x, positions, norm1_g, w_in, q_norm_g, k_norm_g, lam4, diff_subln_g, ret_gn_g, ret_gn_b,
           w_branch_a, w_branch_b, w_out, norm2_g, w_gr, b_gr, w_er, b_er, w_gate, w_up, w_down):
    batch, seq, _ = x.shape
    n = batch * seq
    x2 = x.reshape(n, D_MODEL)
    proj = _in_proj(x2, positions.reshape(n, 1), norm1_g, w_in, q_norm_g, k_norm_g)
    oa = _diff_attn(proj, lam4, diff_subln_g.reshape(1, DA_VDIM), batch, seq)
    ob = _retention(proj, ret_gn_g, ret_gn_b, batch, seq)
    x1, h2, route = _merge(x2, oa, ob, proj, w_branch_a, w_branch_b, w_out, norm2_g, w_gr, b_gr, w_er, b_er)
    dest, row_tok, blk_expert, n_used = _dispatch_plan(route, n)
    ys = _experts(h2, row_tok, blk_expert, n_used, w_gate, w_up, w_down)
    out = _combine(x1, ys, route, dest)
    return out.reshape(batch, seq, D_MODEL)


def kernel(x, positions, norm1_g, w_in, q_norm_g, k_norm_g, lambda_q1, lambda_k1, lambda_q2, lambda_k2, diff_subln_g, ret_gn_g, ret_gn_b, w_branch_a, w_branch_b, w_out, norm2_g, w_group_router, b_group_router, w_expert_router, b_expert_router, w_gate, w_up, w_down):
    assert x.shape[-1] == D_MODEL and norm1_g.shape[0] == 1, "single-layer, D_MODEL-wide input expected"
    lam4 = jnp.zeros((4, LANES), jnp.float32)
    lam4 = lam4.at[:, :DA_HALF].set(jnp.stack([lambda_q1[0], lambda_k1[0], lambda_q2[0], lambda_k2[0]]))
    return _layer(x, positions, norm1_g[0], w_in[0], q_norm_g[0], k_norm_g[0], lam4, diff_subln_g[0],
                  ret_gn_g[0], ret_gn_b[0], w_branch_a[0], w_branch_b[0], w_out[0], norm2_g[0],
                  w_group_router[0], b_group_router[0], w_expert_router[0], b_expert_router[0],
                  w_gate[0], w_up[0], w_down[0])
```

```python
import functools
import math

import jax
import jax.numpy as jnp
from jax import lax
from jax.experimental import pallas as pl
from jax.experimental.pallas import tpu as pltpu

D_MODEL = 1024
DA_HEADS = 4
DA_HALF = 64
DA_VDIM = 2 * DA_HALF
DA_WIDTH = DA_HEADS * DA_VDIM
ROPE_THETA = 500000.0
ROPE_DIM = DA_HALF // 4
RET_HEADS = 4
RET_KDIM = 128
RET_VDIM = 128
RET_WIDTH = RET_HEADS * RET_VDIM
RET_THETA = 10000.0
N_GROUPS = 4
EXPERTS_PER_GROUP = 8
N_EXPERTS = N_GROUPS * EXPERTS_PER_GROUP
TOP_K = 2
EXPERT_FF = 512
EPS = 1e-6
LAMBDA_INIT = 0.8 - 0.6 * math.exp(-0.3 * 0)

LANES = 128
IN_COLS = 3 * DA_WIDTH + 4 * RET_WIDTH + 2 * D_MODEL
COL_QA, COL_KA, COL_VA = 0, DA_WIDTH, 2 * DA_WIDTH
COL_QR = 3 * DA_WIDTH
COL_KR = COL_QR + RET_WIDTH
COL_VR = COL_KR + RET_WIDTH
COL_GB = COL_VR + RET_WIDTH
COL_GATE_A = COL_GB + RET_WIDTH
COL_GATE_B = COL_GATE_A + D_MODEL

PROJ_ROWS = 512
PROJ_CHUNK = 256
ATT_TILE = 512
ATT_ROWS = 32
RET_CHUNK = 256
MOE_BLOCK = 256
COMBINE_ROWS = 256
VMEM_LIMIT = 56 * 1024 * 1024


def _dot(a, b):
    return jnp.dot(a, b, preferred_element_type=jnp.float32)


def _dot_nt(a, b):
    return lax.dot_general(a, b, (((1,), (1,)), ((), ())), preferred_element_type=jnp.float32)


def _dot_tn(a, b):
    return lax.dot_general(a, b, (((0,), (0,)), ((), ())), preferred_element_type=jnp.float32)


def _sigmoid(x):
    return 1.0 / (1.0 + jnp.exp(-x))


def _in_proj_kernel(x_ref, pos_ref, g1_ref, w_ref, gsum_ref, gq_ref, gk_ref, fa_ref, fr_ref,
                    o_ref, h_scr):
    x = x_ref[...]
    h = x * lax.rsqrt(jnp.mean(x * x, axis=-1, keepdims=True) + EPS) * g1_ref[...]
    h_scr[...] = h.astype(jnp.bfloat16)
    rows = x.shape[0]
    pos = pos_ref[...].astype(jnp.float32)

    lane = lax.broadcasted_iota(jnp.int32, (rows, LANES), 1)
    ang_a = pos * fa_ref[...]
    cos_a, sin_a = jnp.cos(ang_a), jnp.sin(ang_a)
    lane64 = lane % DA_HALF
    half_a = ROPE_DIM // 2
    c_a = jnp.where(lane64 < ROPE_DIM, cos_a, 1.0)
    s_lo = jnp.where(lane64 < half_a, -sin_a, 0.0)
    s_hi = jnp.where((lane64 >= half_a) & (lane64 < ROPE_DIM), sin_a, 0.0)
    c_a2 = jnp.concatenate([c_a, c_a], axis=1)
    s_lo2 = jnp.concatenate([s_lo, s_lo], axis=1)
    s_hi2 = jnp.concatenate([s_hi, s_hi], axis=1)
    ang_r = pos * fr_ref[...]
    cos_r, sin_r = jnp.cos(ang_r), jnp.sin(ang_r)
    s_r = jnp.where(lane < RET_KDIM // 2, -sin_r, sin_r)
    c_r2 = jnp.concatenate([cos_r, cos_r], axis=1)
    s_r2 = jnp.concatenate([s_r, s_r], axis=1)

    def qk_norm_rope(y, g, scale):
        ss = y * y
        hi = ss.astype(jnp.bfloat16)
        lo = (ss - hi.astype(jnp.float32)).astype(jnp.bfloat16)
        gs = _dot(hi, gsum_ref[...]) + _dot(lo, gsum_ref[...])
        n = y * lax.rsqrt(gs * (1.0 / DA_HALF) + EPS) * g
        up = pltpu.roll(n, PROJ_CHUNK - half_a, axis=1)
        dn = pltpu.roll(n, half_a, axis=1)
        r = n * c_a2 + up * s_lo2 + dn * s_hi2
        return r * scale if scale != 1.0 else r

    def ret_rope(y, scale):
        halves = [pltpu.roll(y[:, i * LANES:(i + 1) * LANES], RET_KDIM // 2, axis=1)
                  for i in range(PROJ_CHUNK // LANES)]
        sw = jnp.concatenate(halves, axis=1)
        r = y * c_r2 + sw * s_r2
        return r * scale if scale != 1.0 else r

    for c in range(IN_COLS // PROJ_CHUNK):
        c0 = c * PROJ_CHUNK
        y = _dot(h_scr[...], w_ref[:, c0:c0 + PROJ_CHUNK])
        if c0 < COL_KA:
            y = qk_norm_rope(y, gq_ref[...], DA_HALF ** -0.5)
        elif c0 < COL_VA:
            y = qk_norm_rope(y, gk_ref[...], 1.0)
        elif c0 < COL_QR:
            pass
        elif c0 < COL_KR:
            y = ret_rope(y, 1.0)
        elif c0 < COL_VR:
            y = ret_rope(y, RET_KDIM ** -0.5)
        elif c0 < COL_GB:
            pass
        elif c0 < COL_GATE_A:
            y = y * _sigmoid(y)
        else:
            y = _sigmoid(y)
        o_ref[:, c0:c0 + PROJ_CHUNK] = y.astype(o_ref.dtype)


def _in_proj(x2, pos2, g1, w_in, gq, gk):
    n = x2.shape[0]
    tm = min(PROJ_ROWS, n)
    grp = jnp.arange(PROJ_CHUNK) // DA_HALF
    gsum = (grp[:, None] == grp[None, :]).astype(jnp.bfloat16)
    half_a = ROPE_DIM // 2
    freq_a = jnp.power(jnp.float32(ROPE_THETA), -2.0 * jnp.arange(half_a, dtype=jnp.float32) / ROPE_DIM)
    fa = freq_a[jnp.arange(LANES) % half_a][None, :]
    half_r = RET_KDIM // 2
    freq_r = jnp.power(jnp.float32(RET_THETA), -2.0 * jnp.arange(half_r, dtype=jnp.float32) / RET_KDIM)
    fr = freq_r[jnp.arange(LANES) % half_r][None, :]
    reps = PROJ_CHUNK // DA_HALF
    full = lambda shape: pl.BlockSpec(shape, lambda i: (0,) * len(shape))
    return pl.pallas_call(
        _in_proj_kernel,
        grid=(n // tm,),
        in_specs=[
            pl.BlockSpec((tm, D_MODEL), lambda i: (i, 0)),
            pl.BlockSpec((tm, 1), lambda i: (i, 0)),
            full((1, D_MODEL)),
            full((D_MODEL, IN_COLS)),
            full((PROJ_CHUNK, PROJ_CHUNK)),
            full((1, PROJ_CHUNK)),
            full((1, PROJ_CHUNK)),
            full((1, LANES)),
            full((1, LANES)),
        ],
        out_specs=pl.BlockSpec((tm, IN_COLS), lambda i: (i, 0)),
        out_shape=jax.ShapeDtypeStruct((n, IN_COLS), jnp.bfloat16),
        scratch_shapes=[pltpu.VMEM((tm, D_MODEL), jnp.bfloat16)],
        compiler_params=pltpu.CompilerParams(dimension_semantics=("arbitrary",),
                                             vmem_limit_bytes=VMEM_LIMIT),
        name="in_proj",
    )(x2, pos2, g1.reshape(1, D_MODEL), w_in.astype(jnp.bfloat16), gsum,
      jnp.tile(gq, reps)[None, :], jnp.tile(gk, reps)[None, :], fa, fr)


def _diff_attn_kernel(q_ref, k_ref, v_ref, lam_ref, gsub_ref, o_ref,
                      qs_scr, vx_scr, s0_scr, s1_scr, p_scr, m_scr, alpha_scr, acc_scr):
    i = pl.program_id(2)
    t = q_ref.shape[0]

    @pl.when(i == 0)
    def _():
        vx_scr[:, :DA_VDIM] = v_ref[...]
        vx_scr[:, DA_VDIM:] = jnp.ones((vx_scr.shape[0], LANES), vx_scr.dtype)

    q = q_ref[...]
    lane = lax.broadcasted_iota(jnp.int32, q.shape, 1)
    zero = jnp.zeros_like(q)
    qs_scr[:t] = jnp.where(lane < DA_HALF, q, zero)
    qs_scr[t:] = jnp.where(lane >= DA_HALF, q, zero)
    m_scr[...] = jnp.full(m_scr.shape, -jnp.inf, jnp.float32)
    acc_scr[...] = jnp.zeros(acc_scr.shape, jnp.float32)

    def scores(j, s_ref):
        start = pl.multiple_of(j * t, t)
        s_ref[...] = _dot_nt(qs_scr[...], k_ref[pl.ds(start, t), :])

    def softmax_pv(j, s_ref, masked):
        for c in range(2 * t // ATT_ROWS):
            rows = pl.ds(c * ATT_ROWS, ATT_ROWS)
            s = s_ref[rows, :]
            if masked:
                r = lax.broadcasted_iota(jnp.int32, s.shape, 0) + (c * ATT_ROWS) % t
                col = lax.broadcasted_iota(jnp.int32, s.shape, 1)
                s = jnp.where(col <= r, s, -jnp.inf)
            m_prev = m_scr[rows, :]
            m_new = jnp.maximum(m_prev, jnp.max(s, axis=-1, keepdims=True))
            alpha_scr[rows, :] = jnp.exp(m_prev - m_new)
            m_scr[rows, :] = m_new
            p = jnp.exp(s - jnp.concatenate([m_new] * (t // LANES), axis=1))
            p_scr[rows, :] = p.astype(p_scr.dtype)
        start = pl.multiple_of(j * t, t)
        pv = _dot(p_scr[...], vx_scr[pl.ds(start, t), :])
        alpha = alpha_scr[...]
        for half in range(2):
            cols = pl.ds(half * LANES, LANES)
            acc_scr[:, cols] = alpha * acc_scr[:, cols] + pv[:, half * LANES:(half + 1) * LANES]

    scores(0, s0_scr)

    def pair(jj, carry):
        j = 2 * jj
        scores(j + 1, s1_scr)
        softmax_pv(j, s0_scr, False)
        scores(j + 2, s0_scr)
        softmax_pv(j + 1, s1_scr, False)
        return carry

    lax.fori_loop(0, i // 2, pair, 0)

    @pl.when(i % 2 == 1)
    def _():
        scores(i, s1_scr)
        softmax_pv(i - 1, s0_scr, False)
        softmax_pv(i, s1_scr, True)

    @pl.when(i % 2 == 0)
    def _():
        softmax_pv(i, s0_scr, True)

    lam4 = lam_ref[...]
    lam = (jnp.exp(jnp.sum(lam4[0:1] * lam4[1:2], axis=-1, keepdims=True))
           - jnp.exp(jnp.sum(lam4[2:3] * lam4[3:4], axis=-1, keepdims=True)) + LAMBDA_INIT)
    o_all = acc_scr[:, :DA_VDIM] / acc_scr[:, DA_VDIM:]
    o = o_all[:t] - lam * o_all[t:]
    o = o * lax.rsqrt(jnp.mean(o * o, axis=-1, keepdims=True) + EPS) * gsub_ref[...] * (1.0 - LAMBDA_INIT)
    o_ref[...] = o.astype(o_ref.dtype)


def _diff_attn(proj, lam4, gsub, batch, seq):
    n = proj.shape[0]
    t = min(ATT_TILE, seq)
    nq = seq // t
    qb, kb, vb = COL_QA // LANES, COL_KA // LANES, COL_VA // LANES
    return pl.pallas_call(
        _diff_attn_kernel,
        grid=(batch, DA_HEADS, nq),
        in_specs=[
            pl.BlockSpec((t, LANES), lambda b, h, i: (b * nq + i, qb + h)),
            pl.BlockSpec((seq, LANES), lambda b, h, i: (b, kb + h)),
            pl.BlockSpec((seq, LANES), lambda b, h, i: (b, vb + h)),
            pl.BlockSpec((4, LANES), lambda b, h, i: (0, 0)),
            pl.BlockSpec((1, LANES), lambda b, h, i: (0, 0)),
        ],
        out_specs=pl.BlockSpec((t, LANES), lambda b, h, i: (b * nq + i, h)),
        out_shape=jax.ShapeDtypeStruct((n, DA_WIDTH), jnp.bfloat16),
        scratch_shapes=[pltpu.VMEM((2 * t, LANES), jnp.bfloat16),
                        pltpu.VMEM((seq, DA_VDIM + LANES), jnp.bfloat16),
                        pltpu.VMEM((2 * t, t), jnp.float32),
                        pltpu.VMEM((2 * t, t), jnp.float32),
                        pltpu.VMEM((2 * t, t), jnp.bfloat16),
                        pltpu.VMEM((2 * t, LANES), jnp.float32),
                        pltpu.VMEM((2 * t, LANES), jnp.float32),
                        pltpu.VMEM((2 * t, DA_VDIM + LANES), jnp.float32)],
        compiler_params=pltpu.CompilerParams(dimension_semantics=("arbitrary",) * 3,
                                             vmem_limit_bytes=VMEM_LIMIT),
        name="diff_attn",
    )(proj, proj, proj, lam4, gsub)


def _retention_kernel(q_ref, k_ref, v_ref, g_ref, gng_ref, gnb_ref, o_ref, r_scr, *, chunk):
    hf = jnp.full((1, 1), pl.program_id(1), jnp.int32).astype(jnp.float32)
    log_g = jnp.log1p(-jnp.exp2(-5.0 - hf))
    ri = lax.broadcasted_iota(jnp.int32, (chunk, chunk), 0)
    ci = lax.broadcasted_iota(jnp.int32, (chunk, chunk), 1)
    rel = (ri - ci).astype(jnp.float32)
    dmask = jnp.where(rel >= 0, jnp.exp(jnp.maximum(rel, 0.0) * log_g), 0.0)
    idx = lax.broadcasted_iota(jnp.int32, (chunk, 1), 0).astype(jnp.float32)
    zeta = jnp.exp((chunk - 1 - idx) * log_g)
    xi = jnp.exp((idx + 1.0) * log_g)
    g_chunk = jnp.exp(chunk * log_g)
    r_scr[...] = jnp.zeros(r_scr.shape, jnp.float32)
    gng = gng_ref[...]
    gnb = gnb_ref[...]

    def body(n, carry):
        start = pl.multiple_of(n * chunk, chunk)
        q = q_ref[pl.ds(start, chunk), :]
        k = k_ref[pl.ds(start, chunk), :]
        v = v_ref[pl.ds(start, chunk), :]
        s = _dot_nt(q, k) * dmask
        r_old = r_scr[...]
        o = _dot(s.astype(jnp.bfloat16), v) + xi * _dot(q, r_old.astype(jnp.bfloat16))
        kz = (k.astype(jnp.float32) * zeta).astype(jnp.bfloat16)
        r_scr[...] = g_chunk * r_old + _dot_tn(kz, v)
        mu = jnp.mean(o, axis=-1, keepdims=True)
        d = o - mu
        var = jnp.mean(d * d, axis=-1, keepdims=True)
        y = d * lax.rsqrt(var + EPS) * gng + gnb
        y = y * g_ref[pl.ds(start, chunk), :].astype(jnp.float32)
        o_ref[pl.ds(start, chunk), :] = y.astype(o_ref.dtype)
        return carry

    lax.fori_loop(0, q_ref.shape[0] // chunk, body, 0)


def _retention(proj, gn_g, gn_b, batch, seq):
    n = proj.shape[0]
    chunk = min(RET_CHUNK, seq)
    col = lambda c0: (lambda b, h: (b, c0 // LANES + h))
    return pl.pallas_call(
        functools.partial(_retention_kernel, chunk=chunk),
        grid=(batch, RET_HEADS),
        in_specs=[
            pl.BlockSpec((seq, LANES), col(COL_QR)),
            pl.BlockSpec((seq, LANES), col(COL_KR)),
            pl.BlockSpec((seq, LANES), col(COL_VR)),
            pl.BlockSpec((seq, LANES), col(COL_GB)),
            pl.BlockSpec((1, LANES), lambda b, h: (0, h)),
            pl.BlockSpec((1, LANES), lambda b, h: (0, h)),
        ],
        out_specs=pl.BlockSpec((seq, LANES), lambda b, h: (b, h)),
        out_shape=jax.ShapeDtypeStruct((n, RET_WIDTH), jnp.bfloat16),
        scratch_shapes=[pltpu.VMEM((RET_KDIM, RET_VDIM), jnp.float32)],
        compiler_params=pltpu.CompilerParams(dimension_semantics=("arbitrary",) * 2,
                                             vmem_limit_bytes=VMEM_LIMIT),
        name="retention",
    )(proj, proj, proj, proj, gn_g.reshape(1, RET_WIDTH), gn_b.reshape(1, RET_WIDTH))


def _merge_kernel(x_ref, oa_ref, ob_ref, sa0_ref, sa1_ref, sb0_ref, sb1_ref, wa_ref, wb_ref, wo_ref,
                  g2_ref, wr_hi_ref, wr_lo_ref, br_ref, x1_ref, h2_ref, route_ref):
    ya = _dot(oa_ref[...], wa_ref[...])
    yb = _dot(ob_ref[...], wb_ref[...])
    sa = jnp.concatenate([sa0_ref[...], sa1_ref[...]], axis=1).astype(jnp.float32)
    sb = jnp.concatenate([sb0_ref[...], sb1_ref[...]], axis=1).astype(jnp.float32)
    merged = sa * ya + sb * yb
    x1 = x_ref[...] + _dot(merged.astype(jnp.bfloat16), wo_ref[...])
    x1_ref[...] = x1
    h2 = x1 * lax.rsqrt(jnp.mean(x1 * x1, axis=-1, keepdims=True) + EPS) * g2_ref[...]
    h2_ref[...] = h2

    hi = h2.astype(jnp.bfloat16)
    lo = (h2 - hi.astype(jnp.float32)).astype(jnp.bfloat16)
    logits = (_dot(hi, wr_hi_ref[...]) + _dot(lo, wr_hi_ref[...]) + _dot(hi, wr_lo_ref[...])
              + br_ref[...])
    lane = lax.broadcasted_iota(jnp.int32, logits.shape, 1)
    neg = -jnp.inf
    gl = jnp.where(lane < N_GROUPS, logits, neg)
    gmax = jnp.max(gl, axis=-1, keepdims=True)
    g_idx = jnp.min(jnp.where(gl == gmax, lane, LANES), axis=-1, keepdims=True)
    p_g = 1.0 / jnp.sum(jnp.exp(gl - gmax), axis=-1, keepdims=True)
    e_lo = N_GROUPS + EXPERTS_PER_GROUP * g_idx
    el = jnp.where((lane >= e_lo) & (lane < e_lo + EXPERTS_PER_GROUP), logits, neg)
    v1 = jnp.max(el, axis=-1, keepdims=True)
    i1 = jnp.min(jnp.where(el == v1, lane, LANES), axis=-1, keepdims=True)
    el2 = jnp.where(lane == i1, neg, el)
    v2 = jnp.max(el2, axis=-1, keepdims=True)
    i2 = jnp.min(jnp.where(el2 == v2, lane, LANES), axis=-1, keepdims=True)
    t = jnp.exp(v2 - v1)
    w1 = p_g / (1.0 + t)
    w2 = p_g * t / (1.0 + t)
    e1 = (i1 - N_GROUPS).astype(jnp.float32)
    e2 = (i2 - N_GROUPS).astype(jnp.float32)
    route_ref[...] = jnp.where(lane == 0, e1, jnp.where(lane == 1, e2, jnp.where(lane == 2, w1, w2)))


def _merge(x2, oa, ob, proj, wa, wb, wo, g2, w_gr, b_gr, w_er, b_er):
    n = x2.shape[0]
    tm = min(PROJ_ROWS, n)
    half = D_MODEL // 2
    wr = jnp.zeros((D_MODEL, LANES), jnp.float32)
    wr = wr.at[:, :N_GROUPS].set(w_gr).at[:, N_GROUPS:N_GROUPS + N_EXPERTS].set(w_er)
    wr_hi = wr.astype(jnp.bfloat16)
    wr_lo = (wr - wr_hi.astype(jnp.float32)).astype(jnp.bfloat16)
    br = jnp.zeros((1, LANES), jnp.float32)
    br = br.at[0, :N_GROUPS].set(b_gr).at[0, N_GROUPS:N_GROUPS + N_EXPERTS].set(b_er)
    full = lambda shape: pl.BlockSpec(shape, lambda i: (0,) * len(shape))
    gate = lambda c0: pl.BlockSpec((tm, half), lambda i: (i, c0 // half))
    return pl.pallas_call(
        _merge_kernel,
        grid=(n // tm,),
        in_specs=[
            pl.BlockSpec((tm, D_MODEL), lambda i: (i, 0)),
            pl.BlockSpec((tm, DA_WIDTH), lambda i: (i, 0)),
            pl.BlockSpec((tm, RET_WIDTH), lambda i: (i, 0)),
            gate(COL_GATE_A), gate(COL_GATE_A + half), gate(COL_GATE_B), gate(COL_GATE_B + half),
            full((DA_WIDTH, D_MODEL)), full((RET_WIDTH, D_MODEL)), full((D_MODEL, D_MODEL)),
            full((1, D_MODEL)), full((D_MODEL, LANES)), full((D_MODEL, LANES)), full((1, LANES)),
        ],
        out_specs=[
            pl.BlockSpec((tm, D_MODEL), lambda i: (i, 0)),
            pl.BlockSpec((tm, D_MODEL), lambda i: (i, 0)),
            pl.BlockSpec((tm, LANES), lambda i: (i, 0)),
        ],
        out_shape=[
            jax.ShapeDtypeStruct((n, D_MODEL), jnp.float32),
            jax.ShapeDtypeStruct((n, D_MODEL), jnp.float32),
            jax.ShapeDtypeStruct((n, LANES), jnp.float32),
        ],
        compiler_params=pltpu.CompilerParams(dimension_semantics=("arbitrary",),
                                             vmem_limit_bytes=VMEM_LIMIT),
        name="merge",
    )(x2, oa, ob, proj, proj, proj, proj, wa.astype(jnp.bfloat16), wb.astype(jnp.bfloat16),
      wo.astype(jnp.bfloat16), g2.reshape(1, D_MODEL), wr_hi, wr_lo, br)


def _start_row_gather(src_hbm, idx_ref, idx_base, dst_vmem, sem, count):
    def issue(r, carry):
        pltpu.make_async_copy(src_hbm.at[pl.ds(idx_ref[idx_base + r], 1)], dst_vmem.at[pl.ds(r, 1)], sem).start()
        return carry
    lax.fori_loop(0, count, issue, 0, unroll=8)


def _wait_row_gather(src_hbm, dst_vmem, sem, count):
    def drain(r, carry):
        pltpu.make_async_copy(src_hbm.at[pl.ds(0, 1)], dst_vmem.at[pl.ds(r, 1)], sem).wait()
        return carry
    lax.fori_loop(0, count, drain, 0, unroll=8)


def _expert_kernel(blk_e_ref, n_used_ref, row_tok_ref, h2_hbm, wg_ref, wu_ref, wd_ref, o_ref, x_buf, sem):
    i = pl.program_id(0)
    n_used = n_used_ref[0]
    slot = i % 2

    def start(blk, s):
        _start_row_gather(h2_hbm, row_tok_ref, blk * MOE_BLOCK, x_buf.at[s], sem.at[s], MOE_BLOCK)

    @pl.when(i == 0)
    def _():
        start(0, 0)

    @pl.when(i + 1 < n_used)
    def _():
        start(i + 1, 1 - slot)

    @pl.when(i < n_used)
    def _():
        _wait_row_gather(h2_hbm, x_buf.at[slot], sem.at[slot], MOE_BLOCK)
        x = x_buf[slot].astype(jnp.bfloat16)
        a = _dot(x, wg_ref[0])
        u = _dot(x, wu_ref[0])
        hmid = (a * _sigmoid(a) * u).astype(jnp.bfloat16)
        o_ref[...] = _dot(hmid, wd_ref[0])

    @pl.when(i >= n_used)
    def _():
        o_ref[...] = jnp.zeros(o_ref.shape, o_ref.dtype)


def _experts(h2, row_tok, blk_expert, n_used, w_gate, w_up, w_down):
    p = row_tok.shape[0]
    nblk = p // MOE_BLOCK
    return pl.pallas_call(
        _expert_kernel,
        grid_spec=pltpu.PrefetchScalarGridSpec(
            num_scalar_prefetch=3,
            grid=(nblk,),
            in_specs=[
                pl.BlockSpec(memory_space=pl.ANY),
                pl.BlockSpec((1, D_MODEL, EXPERT_FF), lambda i, be, nu, rt: (be[i], 0, 0)),
                pl.BlockSpec((1, D_MODEL, EXPERT_FF), lambda i, be, nu, rt: (be[i], 0, 0)),
                pl.BlockSpec((1, EXPERT_FF, D_MODEL), lambda i, be, nu, rt: (be[i], 0, 0)),
            ],
            out_specs=pl.BlockSpec((MOE_BLOCK, D_MODEL), lambda i, be, nu, rt: (i, 0)),
            scratch_shapes=[pltpu.VMEM((2, MOE_BLOCK, D_MODEL), jnp.float32),
                            pltpu.SemaphoreType.DMA((2,))],
        ),
        out_shape=jax.ShapeDtypeStruct((p, D_MODEL), jnp.float32),
        compiler_params=pltpu.CompilerParams(dimension_semantics=("arbitrary",),
                                             vmem_limit_bytes=VMEM_LIMIT),
        name="experts",
    )(blk_expert, n_used, row_tok, h2, w_gate.astype(jnp.bfloat16), w_up.astype(jnp.bfloat16),
      w_down.astype(jnp.bfloat16))


def _combine_kernel(dest_ref, x1_ref, route_ref, ys_hbm, o_ref, y_buf, sem, *, rows):
    i = pl.program_id(0)
    slot = i % 2

    def start(step, s):
        for k in range(TOP_K):
            _start_row_gather(ys_hbm, dest_ref, k * pl.num_programs(0) * rows + step * rows,
                              y_buf.at[s, k], sem.at[s], rows)

    @pl.when(i == 0)
    def _():
        start(0, 0)

    @pl.when(i + 1 < pl.num_programs(0))
    def _():
        start(i + 1, 1 - slot)

    for k in range(TOP_K):
        _wait_row_gather(ys_hbm, y_buf.at[slot, k], sem.at[slot], rows)
    route = route_ref[...]
    o_ref[...] = x1_ref[...] + route[:, 2:3] * y_buf[slot, 0] + route[:, 3:4] * y_buf[slot, 1]


def _combine(x1, ys, route, dest_kmajor):
    n = x1.shape[0]
    tm = min(COMBINE_ROWS, n)
    return pl.pallas_call(
        functools.partial(_combine_kernel, rows=tm),
        grid_spec=pltpu.PrefetchScalarGridSpec(
            num_scalar_prefetch=1,
            grid=(n // tm,),
            in_specs=[
                pl.BlockSpec((tm, D_MODEL), lambda i, d: (i, 0)),
                pl.BlockSpec((tm, LANES), lambda i, d: (i, 0)),
                pl.BlockSpec(memory_space=pl.ANY),
            ],
            out_specs=pl.BlockSpec((tm, D_MODEL), lambda i, d: (i, 0)),
            scratch_shapes=[pltpu.VMEM((2, TOP_K, tm, D_MODEL), jnp.float32),
                            pltpu.SemaphoreType.DMA((2,))],
        ),
        out_shape=jax.ShapeDtypeStruct((n, D_MODEL), jnp.float32),
        compiler_params=pltpu.CompilerParams(dimension_semantics=("arbitrary",),
                                             vmem_limit_bytes=VMEM_LIMIT),
        name="combine",
    )(dest_kmajor, x1, route, ys)


def _dispatch_plan(route, n):
    e_flat = route[:, :TOP_K].astype(jnp.int32).T.reshape(-1)
    onehot = (e_flat[:, None] == jnp.arange(N_EXPERTS, dtype=jnp.int32)[None, :]).astype(jnp.int32)
    csum = jnp.cumsum(onehot, axis=0)
    counts = csum[-1]
    rank = jnp.sum((csum - onehot) * onehot, axis=1)
    padded = ((counts + MOE_BLOCK - 1) // MOE_BLOCK) * MOE_BLOCK
    pad_end = jnp.cumsum(padded)
    pad_start = pad_end - padded
    dest = (pad_start[e_flat] + rank).astype(jnp.int32)
    p = n * TOP_K + N_EXPERTS * MOE_BLOCK
    tok = jnp.arange(n * TOP_K, dtype=jnp.int32) % n
    row_tok = jnp.zeros((p,), jnp.int32).at[dest].set(tok)
    blk_start = jnp.arange(p // MOE_BLOCK, dtype=jnp.int32) * MOE_BLOCK
    blk_expert = jnp.clip(jnp.searchsorted(pad_end, blk_start, side='right'), 0, N_EXPERTS - 1)
    n_used = (pad_end[-1] // MOE_BLOCK).astype(jnp.int32).reshape(1)
    return dest, row_tok, blk_expert.astype(jnp.int32), n_used


def _layer(x, positions, norm1_g, w_in, q_norm_g, k_norm_g, lam4, diff_subln_g, ret_gn_g, ret_gn_b,
           w_branch_a, w_branch_b, w_out, norm2_g, w_gr, b_gr, w_er, b_er, w_gate, w_up, w_down):
    batch, seq, _ = x.shape
    n = batch * seq
    x2 = x.reshape(n, D_MODEL)
    proj = _in_proj(x2, positions.reshape(n, 1), norm1_g, w_in, q_norm_g, k_norm_g)
    oa = _diff_attn(proj, lam4, diff_subln_g.reshape(1, DA_VDIM), batch, seq)
    ob = _retention(proj, ret_gn_g, ret_gn_b, batch, seq)
    x1, h2, route = _merge(x2, oa, ob, proj, w_branch_a, w_branch_b, w_out, norm2_g, w_gr, b_gr, w_er, b_er)
    dest, row_tok, blk_expert, n_used = _dispatch_plan(route, n)
    ys = _experts(h2, row_tok, blk_expert, n_used, w_gate, w_up, w_down)
    out = _combine(x1, ys, route, dest)
    return out.reshape(batch, seq, D_MODEL)


def kernel(x, positions, norm1_g, w_in, q_norm_g, k_norm_g, lambda_q1, lambda_k1, lambda_q2, lambda_k2, diff_subln_g, ret_gn_g, ret_gn_b, w_branch_a, w_branch_b, w_out, norm2_g, w_group_router, b_group_router, w_expert_router, b_expert_router, w_gate, w_up, w_down):
    assert x.shape[-1] == D_MODEL and norm1_g.shape[0] == 1, "single-layer, D_MODEL-wide input expected"
    lam4 = jnp.zeros((4, LANES), jnp.float32)
    lam4 = lam4.at[:, :DA_HALF].set(jnp.stack([lambda_q1[0], lambda_k1[0], lambda_q2[0], lambda_k2[0]]))
    return _layer(x, positions, norm1_g[0], w_in[0], q_norm_g[0], k_norm_g[0], lam4, diff_subln_g[0],
                  ret_gn_g[0], ret_gn_b[0], w_branch_a[0], w_branch_b[0], w_out[0], norm2_g[0],
                  w_group_router[0], b_group_router[0], w_expert_router[0], b_expert_router[0],
                  w_gate[0], w_up[0], w_down[0])
```

```python
import functools
import math

import jax
import jax.numpy as jnp
from jax import lax
from jax.experimental import pallas as pl
from jax.experimental.pallas import tpu as pltpu

D_MODEL = 1024
DA_HEADS = 4
DA_HALF = 64
DA_VDIM = 2 * DA_HALF
DA_WIDTH = DA_HEADS * DA_VDIM
ROPE_THETA = 500000.0
ROPE_DIM = DA_HALF // 4
RET_HEADS = 4
RET_KDIM = 128
RET_VDIM = 128
RET_WIDTH = RET_HEADS * RET_VDIM
RET_THETA = 10000.0
N_GROUPS = 4
EXPERTS_PER_GROUP = 8
N_EXPERTS = N_GROUPS * EXPERTS_PER_GROUP
TOP_K = 2
EXPERT_FF = 512
EPS = 1e-6
LAMBDA_INIT = 0.8 - 0.6 * math.exp(-0.3 * 0)

LANES = 128
IN_COLS = 3 * DA_WIDTH + 4 * RET_WIDTH + 2 * D_MODEL
COL_QA, COL_KA, COL_VA = 0, DA_WIDTH, 2 * DA_WIDTH
COL_QR = 3 * DA_WIDTH
COL_KR = COL_QR + RET_WIDTH
COL_VR = COL_KR + RET_WIDTH
COL_GB = COL_VR + RET_WIDTH
COL_GATE_A = COL_GB + RET_WIDTH
COL_GATE_B = COL_GATE_A + D_MODEL

PROJ_ROWS = 512
PROJ_CHUNK = 256
ATT_TILE = 512
ATT_ROWS = 32
RET_CHUNK = 256
MOE_BLOCK = 256
COMBINE_ROWS = 256
GATHER_UNROLL = 8
VMEM_LIMIT = 56 * 1024 * 1024


def _dot(a, b):
    return jnp.dot(a, b, preferred_element_type=jnp.float32)


def _dot_nt(a, b):
    return lax.dot_general(a, b, (((1,), (1,)), ((), ())), preferred_element_type=jnp.float32)


def _dot_tn(a, b):
    return lax.dot_general(a, b, (((0,), (0,)), ((), ())), preferred_element_type=jnp.float32)


def _sigmoid(x):
    return 1.0 / (1.0 + jnp.exp(-x))


def _in_proj_kernel(x_ref, pos_ref, g1_ref, w_ref, gsum_ref, gq_ref, gk_ref, fa_ref, fr_ref,
                    o_ref, h_scr):
    x = x_ref[...]
    h = x * lax.rsqrt(jnp.mean(x * x, axis=-1, keepdims=True) + EPS) * g1_ref[...]
    h_scr[...] = h.astype(jnp.bfloat16)
    rows = x.shape[0]
    pos = pos_ref[...].astype(jnp.float32)

    lane = lax.broadcasted_iota(jnp.int32, (rows, LANES), 1)
    ang_a = pos * fa_ref[...]
    cos_a, sin_a = jnp.cos(ang_a), jnp.sin(ang_a)
    lane64 = lane % DA_HALF
    half_a = ROPE_DIM // 2
    c_a = jnp.where(lane64 < ROPE_DIM, cos_a, 1.0)
    s_lo = jnp.where(lane64 < half_a, -sin_a, 0.0)
    s_hi = jnp.where((lane64 >= half_a) & (lane64 < ROPE_DIM), sin_a, 0.0)
    c_a2 = jnp.concatenate([c_a, c_a], axis=1)
    s_lo2 = jnp.concatenate([s_lo, s_lo], axis=1)
    s_hi2 = jnp.concatenate([s_hi, s_hi], axis=1)
    ang_r = pos * fr_ref[...]
    cos_r, sin_r = jnp.cos(ang_r), jnp.sin(ang_r)
    s_r = jnp.where(lane < RET_KDIM // 2, -sin_r, sin_r)
    c_r2 = jnp.concatenate([cos_r, cos_r], axis=1)
    s_r2 = jnp.concatenate([s_r, s_r], axis=1)

    def qk_norm_rope(y, g, scale):
        ss = y * y
        hi = ss.astype(jnp.bfloat16)
        lo = (ss - hi.astype(jnp.float32)).astype(jnp.bfloat16)
        gs = _dot(hi, gsum_ref[...]) + _dot(lo, gsum_ref[...])
        n = y * lax.rsqrt(gs * (1.0 / DA_HALF) + EPS) * g
        up = pltpu.roll(n, PROJ_CHUNK - half_a, axis=1)
        dn = pltpu.roll(n, half_a, axis=1)
        r = n * c_a2 + up * s_lo2 + dn * s_hi2
        return r * scale if scale != 1.0 else r

    def ret_rope(y, scale):
        halves = [pltpu.roll(y[:, i * LANES:(i + 1) * LANES], RET_KDIM // 2, axis=1)
                  for i in range(PROJ_CHUNK // LANES)]
        sw = jnp.concatenate(halves, axis=1)
        r = y * c_r2 + sw * s_r2
        return r * scale if scale != 1.0 else r

    for c in range(IN_COLS // PROJ_CHUNK):
        c0 = c * PROJ_CHUNK
        y = _dot(h_scr[...], w_ref[:, c0:c0 + PROJ_CHUNK])
        if c0 < COL_KA:
            y = qk_norm_rope(y, gq_ref[...], DA_HALF ** -0.5)
        elif c0 < COL_VA:
            y = qk_norm_rope(y, gk_ref[...], 1.0)
        elif c0 < COL_QR:
            pass
        elif c0 < COL_KR:
            y = ret_rope(y, 1.0)
        elif c0 < COL_VR:
            y = ret_rope(y, RET_KDIM ** -0.5)
        elif c0 < COL_GB:
            pass
        elif c0 < COL_GATE_A:
            y = y * _sigmoid(y)
        else:
            y = _sigmoid(y)
        o_ref[:, c0:c0 + PROJ_CHUNK] = y.astype(o_ref.dtype)


def _in_proj(x2, pos2, g1, w_in, gq, gk):
    n = x2.shape[0]
    tm = min(PROJ_ROWS, n)
    grp = jnp.arange(PROJ_CHUNK) // DA_HALF
    gsum = (grp[:, None] == grp[None, :]).astype(jnp.bfloat16)
    half_a = ROPE_DIM // 2
    freq_a = jnp.power(jnp.float32(ROPE_THETA), -2.0 * jnp.arange(half_a, dtype=jnp.float32) / ROPE_DIM)
    fa = freq_a[jnp.arange(LANES) % half_a][None, :]
    half_r = RET_KDIM // 2
    freq_r = jnp.power(jnp.float32(RET_THETA), -2.0 * jnp.arange(half_r, dtype=jnp.float32) / RET_KDIM)
    fr = freq_r[jnp.arange(LANES) % half_r][None, :]
    reps = PROJ_CHUNK // DA_HALF
    full = lambda shape: pl.BlockSpec(shape, lambda i: (0,) * len(shape))
    return pl.pallas_call(
        _in_proj_kernel,
        grid=(n // tm,),
        in_specs=[
            pl.BlockSpec((tm, D_MODEL), lambda i: (i, 0)),
            pl.BlockSpec((tm, 1), lambda i: (i, 0)),
            full((1, D_MODEL)),
            full((D_MODEL, IN_COLS)),
            full((PROJ_CHUNK, PROJ_CHUNK)),
            full((1, PROJ_CHUNK)),
            full((1, PROJ_CHUNK)),
            full((1, LANES)),
            full((1, LANES)),
        ],
        out_specs=pl.BlockSpec((tm, IN_COLS), lambda i: (i, 0)),
        out_shape=jax.ShapeDtypeStruct((n, IN_COLS), jnp.bfloat16),
        scratch_shapes=[pltpu.VMEM((tm, D_MODEL), jnp.bfloat16)],
        compiler_params=pltpu.CompilerParams(dimension_semantics=("arbitrary",),
                                             vmem_limit_bytes=VMEM_LIMIT),
        name="in_proj",
    )(x2, pos2, g1.reshape(1, D_MODEL), w_in.astype(jnp.bfloat16), gsum,
      jnp.tile(gq, reps)[None, :], jnp.tile(gk, reps)[None, :], fa, fr)


def _diff_attn_kernel(q_ref, k_ref, v_ref, lam_ref, gsub_ref, o_ref,
                      qs_scr, vx_scr, s0_scr, s1_scr, p_scr, m_scr, alpha_scr, acc_scr):
    i = pl.program_id(2)
    t = q_ref.shape[0]

    @pl.when(i == 0)
    def _():
        vx_scr[:, :DA_VDIM] = v_ref[...]
        vx_scr[:, DA_VDIM:] = jnp.ones((vx_scr.shape[0], LANES), vx_scr.dtype)

    q = q_ref[...]
    lane = lax.broadcasted_iota(jnp.int32, q.shape, 1)
    zero = jnp.zeros_like(q)
    qs_scr[:t] = jnp.where(lane < DA_HALF, q, zero)
    qs_scr[t:] = jnp.where(lane >= DA_HALF, q, zero)
    m_scr[...] = jnp.full(m_scr.shape, -jnp.inf, jnp.float32)
    acc_scr[...] = jnp.zeros(acc_scr.shape, jnp.float32)

    def scores(j, s_ref):
        start = pl.multiple_of(j * t, t)
        s_ref[...] = _dot_nt(qs_scr[...], k_ref[pl.ds(start, t), :])

    def softmax_pv(j, s_ref, masked):
        for c in range(2 * t // ATT_ROWS):
            rows = pl.ds(c * ATT_ROWS, ATT_ROWS)
            s = s_ref[rows, :]
            if masked:
                r = lax.broadcasted_iota(jnp.int32, s.shape, 0) + (c * ATT_ROWS) % t
                col = lax.broadcasted_iota(jnp.int32, s.shape, 1)
                s = jnp.where(col <= r, s, -jnp.inf)
            m_prev = m_scr[rows, :]
            m_new = jnp.maximum(m_prev, jnp.max(s, axis=-1, keepdims=True))
            alpha_scr[rows, :] = jnp.exp(m_prev - m_new)
            m_scr[rows, :] = m_new
            p = jnp.exp(s - jnp.concatenate([m_new] * (t // LANES), axis=1))
            p_scr[rows, :] = p.astype(p_scr.dtype)
        start = pl.multiple_of(j * t, t)
        pv = _dot(p_scr[...], vx_scr[pl.ds(start, t), :])
        alpha = alpha_scr[...]
        for half in range(2):
            cols = pl.ds(half * LANES, LANES)
            acc_scr[:, cols] = alpha * acc_scr[:, cols] + pv[:, half * LANES:(half + 1) * LANES]

    scores(0, s0_scr)

    def pair(jj, carry):
        j = 2 * jj
        scores(j + 1, s1_scr)
        softmax_pv(j, s0_scr, False)
        scores(j + 2, s0_scr)
        softmax_pv(j + 1, s1_scr, False)
        return carry

    lax.fori_loop(0, i // 2, pair, 0)

    @pl.when(i % 2 == 1)
    def _():
        scores(i, s1_scr)
        softmax_pv(i - 1, s0_scr, False)
        softmax_pv(i, s1_scr, True)

    @pl.when(i % 2 == 0)
    def _():
        softmax_pv(i, s0_scr, True)

    lam4 = lam_ref[...]
    lam = (jnp.exp(jnp.sum(lam4[0:1] * lam4[1:2], axis=-1, keepdims=True))
           - jnp.exp(jnp.sum(lam4[2:3] * lam4[3:4], axis=-1, keepdims=True)) + LAMBDA_INIT)
    o_all = acc_scr[:, :DA_VDIM] / acc_scr[:, DA_VDIM:]
    o = o_all[:t] - lam * o_all[t:]
    o = o * lax.rsqrt(jnp.mean(o * o, axis=-1, keepdims=True) + EPS) * gsub_ref[...] * (1.0 - LAMBDA_INIT)
    o_ref[...] = o.astype(o_ref.dtype)


def _diff_attn(proj, lam4, gsub, batch, seq):
    n = proj.shape[0]
    t = min(ATT_TILE, seq)
    nq = seq // t
    qb, kb, vb = COL_QA // LANES, COL_KA // LANES, COL_VA // LANES
    return pl.pallas_call(
        _diff_attn_kernel,
        grid=(batch, DA_HEADS, nq),
        in_specs=[
            pl.BlockSpec((t, LANES), lambda b, h, i: (b * nq + i, qb + h)),
            pl.BlockSpec((seq, LANES), lambda b, h, i: (b, kb + h)),
            pl.BlockSpec((seq, LANES), lambda b, h, i: (b, vb + h)),
            pl.BlockSpec((4, LANES), lambda b, h, i: (0, 0)),
            pl.BlockSpec((1, LANES), lambda b, h, i: (0, 0)),
        ],
        out_specs=pl.BlockSpec((t, LANES), lambda b, h, i: (b * nq + i, h)),
        out_shape=jax.ShapeDtypeStruct((n, DA_WIDTH), jnp.bfloat16),
        scratch_shapes=[pltpu.VMEM((2 * t, LANES), jnp.bfloat16),
                        pltpu.VMEM((seq, DA_VDIM + LANES), jnp.bfloat16),
                        pltpu.VMEM((2 * t, t), jnp.float32),
                        pltpu.VMEM((2 * t, t), jnp.float32),
                        pltpu.VMEM((2 * t, t), jnp.bfloat16),
                        pltpu.VMEM((2 * t, LANES), jnp.float32),
                        pltpu.VMEM((2 * t, LANES), jnp.float32),
                        pltpu.VMEM((2 * t, DA_VDIM + LANES), jnp.float32)],
        compiler_params=pltpu.CompilerParams(dimension_semantics=("arbitrary",) * 3,
                                             vmem_limit_bytes=VMEM_LIMIT),
        name="diff_attn",
    )(proj, proj, proj, lam4, gsub)


def _retention_kernel(q_ref, k_ref, v_ref, g_ref, gng_ref, gnb_ref, o_ref, r_scr, *, chunk):
    hf = jnp.full((1, 1), pl.program_id(1), jnp.int32).astype(jnp.float32)
    log_g = jnp.log1p(-jnp.exp2(-5.0 - hf))
    ri = lax.broadcasted_iota(jnp.int32, (chunk, chunk), 0)
    ci = lax.broadcasted_iota(jnp.int32, (chunk, chunk), 1)
    rel = (ri - ci).astype(jnp.float32)
    dmask = jnp.where(rel >= 0, jnp.exp(jnp.maximum(rel, 0.0) * log_g), 0.0)
    idx = lax.broadcasted_iota(jnp.int32, (chunk, 1), 0).astype(jnp.float32)
    zeta = jnp.exp((chunk - 1 - idx) * log_g)
    xi = jnp.exp((idx + 1.0) * log_g)
    g_chunk = jnp.exp(chunk * log_g)
    r_scr[...] = jnp.zeros(r_scr.shape, jnp.float32)
    gng = gng_ref[...]
    gnb = gnb_ref[...]

    def body(n, carry):
        start = pl.multiple_of(n * chunk, chunk)
        q = q_ref[pl.ds(start, chunk), :]
        k = k_ref[pl.ds(start, chunk), :]
        v = v_ref[pl.ds(start, chunk), :]
        s = _dot_nt(q, k) * dmask
        r_old = r_scr[...]
        o = _dot(s.astype(jnp.bfloat16), v) + xi * _dot(q, r_old.astype(jnp.bfloat16))
        kz = (k.astype(jnp.float32) * zeta).astype(jnp.bfloat16)
        r_scr[...] = g_chunk * r_old + _dot_tn(kz, v)
        mu = jnp.mean(o, axis=-1, keepdims=True)
        d = o - mu
        var = jnp.mean(d * d, axis=-1, keepdims=True)
        y = d * lax.rsqrt(var + EPS) * gng + gnb
        y = y * g_ref[pl.ds(start, chunk), :].astype(jnp.float32)
        o_ref[pl.ds(start, chunk), :] = y.astype(o_ref.dtype)
        return carry

    lax.fori_loop(0, q_ref.shape[0] // chunk, body, 0)


def _retention(proj, gn_g, gn_b, batch, seq):
    n = proj.shape[0]
    chunk = min(RET_CHUNK, seq)
    col = lambda c0: (lambda b, h: (b, c0 // LANES + h))
    return pl.pallas_call(
        functools.partial(_retention_kernel, chunk=chunk),
        grid=(batch, RET_HEADS),
        in_specs=[
            pl.BlockSpec((seq, LANES), col(COL_QR)),
            pl.BlockSpec((seq, LANES), col(COL_KR)),
            pl.BlockSpec((seq, LANES), col(COL_VR)),
            pl.BlockSpec((seq, LANES), col(COL_GB)),
            pl.BlockSpec((1, LANES), lambda b, h: (0, h)),
            pl.BlockSpec((1, LANES), lambda b, h: (0, h)),
        ],
        out_specs=pl.BlockSpec((seq, LANES), lambda b, h: (b, h)),
        out_shape=jax.ShapeDtypeStruct((n, RET_WIDTH), jnp.bfloat16),
        scratch_shapes=[pltpu.VMEM((RET_KDIM, RET_VDIM), jnp.float32)],
        compiler_params=pltpu.CompilerParams(dimension_semantics=("arbitrary",) * 2,
                                             vmem_limit_bytes=VMEM_LIMIT),
        name="retention",
    )(proj, proj, proj, proj, gn_g.reshape(1, RET_WIDTH), gn_b.reshape(1, RET_WIDTH))


def _merge_kernel(x_ref, oa_ref, ob_ref, sa0_ref, sa1_ref, sb0_ref, sb1_ref, wa_ref, wb_ref, wo_ref,
                  g2_ref, wr_hi_ref, wr_lo_ref, br_ref, x1_ref, h2_ref, route_ref):
    ya = _dot(oa_ref[...], wa_ref[...])
    yb = _dot(ob_ref[...], wb_ref[...])
    sa = jnp.concatenate([sa0_ref[...], sa1_ref[...]], axis=1).astype(jnp.float32)
    sb = jnp.concatenate([sb0_ref[...], sb1_ref[...]], axis=1).astype(jnp.float32)
    merged = sa * ya + sb * yb
    x1 = x_ref[...] + _dot(merged.astype(jnp.bfloat16), wo_ref[...])
    x1_ref[...] = x1
    h2 = x1 * lax.rsqrt(jnp.mean(x1 * x1, axis=-1, keepdims=True) + EPS) * g2_ref[...]
    h2_ref[...] = h2

    hi = h2.astype(jnp.bfloat16)
    lo = (h2 - hi.astype(jnp.float32)).astype(jnp.bfloat16)
    logits = (_dot(hi, wr_hi_ref[...]) + _dot(lo, wr_hi_ref[...]) + _dot(hi, wr_lo_ref[...])
              + br_ref[...])
    lane = lax.broadcasted_iota(jnp.int32, logits.shape, 1)
    neg = -jnp.inf
    gl = jnp.where(lane < N_GROUPS, logits, neg)
    gmax = jnp.max(gl, axis=-1, keepdims=True)
    g_idx = jnp.min(jnp.where(gl == gmax, lane, LANES), axis=-1, keepdims=True)
    p_g = 1.0 / jnp.sum(jnp.exp(gl - gmax), axis=-1, keepdims=True)
    e_lo = N_GROUPS + EXPERTS_PER_GROUP * g_idx
    el = jnp.where((lane >= e_lo) & (lane < e_lo + EXPERTS_PER_GROUP), logits, neg)
    v1 = jnp.max(el, axis=-1, keepdims=True)
    i1 = jnp.min(jnp.where(el == v1, lane, LANES), axis=-1, keepdims=True)
    el2 = jnp.where(lane == i1, neg, el)
    v2 = jnp.max(el2, axis=-1, keepdims=True)
    i2 = jnp.min(jnp.where(el2 == v2, lane, LANES), axis=-1, keepdims=True)
    t = jnp.exp(v2 - v1)
    w1 = p_g / (1.0 + t)
    w2 = p_g * t / (1.0 + t)
    e1 = (i1 - N_GROUPS).astype(jnp.float32)
    e2 = (i2 - N_GROUPS).astype(jnp.float32)
    route_ref[...] = jnp.where(lane == 0, e1, jnp.where(lane == 1, e2, jnp.where(lane == 2, w1, w2)))


def _merge(x2, oa, ob, proj, wa, wb, wo, g2, w_gr, b_gr, w_er, b_er):
    n = x2.shape[0]
    tm = min(PROJ_ROWS, n)
    half = D_MODEL // 2
    wr = jnp.zeros((D_MODEL, LANES), jnp.float32)
    wr = wr.at[:, :N_GROUPS].set(w_gr).at[:, N_GROUPS:N_GROUPS + N_EXPERTS].set(w_er)
    wr_hi = wr.astype(jnp.bfloat16)
    wr_lo = (wr - wr_hi.astype(jnp.float32)).astype(jnp.bfloat16)
    br = jnp.zeros((1, LANES), jnp.float32)
    br = br.at[0, :N_GROUPS].set(b_gr).at[0, N_GROUPS:N_GROUPS + N_EXPERTS].set(b_er)
    full = lambda shape: pl.BlockSpec(shape, lambda i: (0,) * len(shape))
    gate = lambda c0: pl.BlockSpec((tm, half), lambda i: (i, c0 // half))
    return pl.pallas_call(
        _merge_kernel,
        grid=(n // tm,),
        in_specs=[
            pl.BlockSpec((tm, D_MODEL), lambda i: (i, 0)),
            pl.BlockSpec((tm, DA_WIDTH), lambda i: (i, 0)),
            pl.BlockSpec((tm, RET_WIDTH), lambda i: (i, 0)),
            gate(COL_GATE_A), gate(COL_GATE_A + half), gate(COL_GATE_B), gate(COL_GATE_B + half),
            full((DA_WIDTH, D_MODEL)), full((RET_WIDTH, D_MODEL)), full((D_MODEL, D_MODEL)),
            full((1, D_MODEL)), full((D_MODEL, LANES)), full((D_MODEL, LANES)), full((1, LANES)),
        ],
        out_specs=[
            pl.BlockSpec((tm, D_MODEL), lambda i: (i, 0)),
            pl.BlockSpec((tm, D_MODEL), lambda i: (i, 0)),
            pl.BlockSpec((tm, LANES), lambda i: (i, 0)),
        ],
        out_shape=[
            jax.ShapeDtypeStruct((n, D_MODEL), jnp.float32),
            jax.ShapeDtypeStruct((n, D_MODEL), jnp.float32),
            jax.ShapeDtypeStruct((n, LANES), jnp.float32),
        ],
        compiler_params=pltpu.CompilerParams(dimension_semantics=("arbitrary",),
                                             vmem_limit_bytes=VMEM_LIMIT),
        name="merge",
    )(x2, oa, ob, proj, proj, proj, proj, wa.astype(jnp.bfloat16), wb.astype(jnp.bfloat16),
      wo.astype(jnp.bfloat16), g2.reshape(1, D_MODEL), wr_hi, wr_lo, br)


def _start_row_gather(src_hbm, idx_ref, idx_base, dst_vmem, sem, count):
    def issue(g, carry):
        for u in range(GATHER_UNROLL):
            r = g * GATHER_UNROLL + u
            pltpu.make_async_copy(src_hbm.at[pl.ds(idx_ref[idx_base + r], 1)], dst_vmem.at[pl.ds(r, 1)],
                                  sem).start(priority=u % 2)
        return carry
    lax.fori_loop(0, count // GATHER_UNROLL, issue, 0)


def _wait_row_gather(src_hbm, dst_vmem, sem, count):
    def drain(r, carry):
        pltpu.make_async_copy(src_hbm.at[pl.ds(0, 1)], dst_vmem.at[pl.ds(r, 1)], sem).wait()
        return carry
    lax.fori_loop(0, count, drain, 0, unroll=8)


def _expert_kernel(blk_e_ref, n_used_ref, row_tok_ref, h2_hbm, wg_ref, wu_ref, wd_ref, o_ref, x_buf, sem):
    i = pl.program_id(0)
    n_used = n_used_ref[0]
    slot = i % 2

    def start(blk, s):
        _start_row_gather(h2_hbm, row_tok_ref, blk * MOE_BLOCK, x_buf.at[s], sem.at[s], MOE_BLOCK)

    @pl.when(i == 0)
    def _():
        start(0, 0)

    @pl.when(i + 1 < n_used)
    def _():
        start(i + 1, 1 - slot)

    @pl.when(i < n_used)
    def _():
        _wait_row_gather(h2_hbm, x_buf.at[slot], sem.at[slot], MOE_BLOCK)
        x = x_buf[slot].astype(jnp.bfloat16)
        a = _dot(x, wg_ref[0])
        u = _dot(x, wu_ref[0])
        hmid = (a * _sigmoid(a) * u).astype(jnp.bfloat16)
        o_ref[...] = _dot(hmid, wd_ref[0])

    @pl.when(i >= n_used)
    def _():
        o_ref[...] = jnp.zeros(o_ref.shape, o_ref.dtype)


def _experts(h2, row_tok, blk_expert, n_used, w_gate, w_up, w_down):
    p = row_tok.shape[0]
    nblk = p // MOE_BLOCK
    return pl.pallas_call(
        _expert_kernel,
        grid_spec=pltpu.PrefetchScalarGridSpec(
            num_scalar_prefetch=3,
            grid=(nblk,),
            in_specs=[
                pl.BlockSpec(memory_space=pl.ANY),
                pl.BlockSpec((1, D_MODEL, EXPERT_FF), lambda i, be, nu, rt: (be[i], 0, 0)),
                pl.BlockSpec((1, D_MODEL, EXPERT_FF), lambda i, be, nu, rt: (be[i], 0, 0)),
                pl.BlockSpec((1, EXPERT_FF, D_MODEL), lambda i, be, nu, rt: (be[i], 0, 0)),
            ],
            out_specs=pl.BlockSpec((MOE_BLOCK, D_MODEL), lambda i, be, nu, rt: (i, 0)),
            scratch_shapes=[pltpu.VMEM((2, MOE_BLOCK, D_MODEL), jnp.float32),
                            pltpu.SemaphoreType.DMA((2,))],
        ),
        out_shape=jax.ShapeDtypeStruct((p, D_MODEL), jnp.float32),
        compiler_params=pltpu.CompilerParams(dimension_semantics=("arbitrary",),
                                             vmem_limit_bytes=VMEM_LIMIT),
        name="experts",
    )(blk_expert, n_used, row_tok, h2, w_gate.astype(jnp.bfloat16), w_up.astype(jnp.bfloat16),
      w_down.astype(jnp.bfloat16))


def _combine_kernel(dest_ref, x1_ref, route_ref, ys_hbm, o_ref, y_buf, sem, *, rows):
    i = pl.program_id(0)
    slot = i % 2

    def start(step, s):
        for k in range(TOP_K):
            _start_row_gather(ys_hbm, dest_ref, k * pl.num_programs(0) * rows + step * rows,
                              y_buf.at[s, k], sem.at[s], rows)

    @pl.when(i == 0)
    def _():
        start(0, 0)

    @pl.when(i + 1 < pl.num_programs(0))
    def _():
        start(i + 1, 1 - slot)

    for k in range(TOP_K):
        _wait_row_gather(ys_hbm, y_buf.at[slot, k], sem.at[slot], rows)
    route = route_ref[...]
    o_ref[...] = x1_ref[...] + route[:, 2:3] * y_buf[slot, 0] + route[:, 3:4] * y_buf[slot, 1]


def _combine(x1, ys, route, dest_kmajor):
    n = x1.shape[0]
    tm = min(COMBINE_ROWS, n)
    return pl.pallas_call(
        functools.partial(_combine_kernel, rows=tm),
        grid_spec=pltpu.PrefetchScalarGridSpec(
            num_scalar_prefetch=1,
            grid=(n // tm,),
            in_specs=[
                pl.BlockSpec((tm, D_MODEL), lambda i, d: (i, 0)),
                pl.BlockSpec((tm, LANES), lambda i, d: (i, 0)),
                pl.BlockSpec(memory_space=pl.ANY),
            ],
            out_specs=pl.BlockSpec((tm, D_MODEL), lambda i, d: (i, 0)),
            scratch_shapes=[pltpu.VMEM((2, TOP_K, tm, D_MODEL), jnp.float32),
                            pltpu.SemaphoreType.DMA((2,))],
        ),
        out_shape=jax.ShapeDtypeStruct((n, D_MODEL), jnp.float32),
        compiler_params=pltpu.CompilerParams(dimension_semantics=("arbitrary",),
                                             vmem_limit_bytes=VMEM_LIMIT),
        name="combine",
    )(dest_kmajor, x1, route, ys)


def _dispatch_plan(route, n):
    e_flat = route[:, :TOP_K].astype(jnp.int32).T.reshape(-1)
    onehot = (e_flat[:, None] == jnp.arange(N_EXPERTS, dtype=jnp.int32)[None, :]).astype(jnp.int32)
    csum = jnp.cumsum(onehot, axis=0)
    counts = csum[-1]
    rank = jnp.sum((csum - onehot) * onehot, axis=1)
    padded = ((counts + MOE_BLOCK - 1) // MOE_BLOCK) * MOE_BLOCK
    pad_end = jnp.cumsum(padded)
    pad_start = pad_end - padded
    dest = (pad_start[e_flat] + rank).astype(jnp.int32)
    p = n * TOP_K + N_EXPERTS * MOE_BLOCK
    tok = jnp.arange(n * TOP_K, dtype=jnp.int32) % n
    row_tok = jnp.zeros((p,), jnp.int32).at[dest].set(tok)
    blk_start = jnp.arange(p // MOE_BLOCK, dtype=jnp.int32) * MOE_BLOCK
    blk_expert = jnp.clip(jnp.searchsorted(pad_end, blk_start, side='right'), 0, N_EXPERTS - 1)
    n_used = (pad_end[-1] // MOE_BLOCK).astype(jnp.int32).reshape(1)
    return dest, row_tok, blk_expert.astype(jnp.int32), n_used


def _layer(x, positions, norm1_g, w_in, q_norm_g, k_norm_g, lam4, diff_subln_g, ret_gn_g, ret_gn_b,
           w_branch_a, w_branch_b, w_out, norm2_g, w_gr, b_gr, w_er, b_er, w_gate, w_up, w_down):
    batch, seq, _ = x.shape
    n = batch * seq
    x2 = x.reshape(n, D_MODEL)
    proj = _in_proj(x2, positions.reshape(n, 1), norm1_g, w_in, q_norm_g, k_norm_g)
    oa = _diff_attn(proj, lam4, diff_subln_g.reshape(1, DA_VDIM), batch, seq)
    ob = _retention(proj, ret_gn_g, ret_gn_b, batch, seq)
    x1, h2, route = _merge(x2, oa, ob, proj, w_branch_a, w_branch_b, w_out, norm2_g, w_gr, b_gr, w_er, b_er)
    dest, row_tok, blk_expert, n_used = _dispatch_plan(route, n)
    ys = _experts(h2, row_tok, blk_expert, n_used, w_gate, w_up, w_down)
    out = _combine(x1, ys, route, dest)
    return out.reshape(batch, seq, D_MODEL)


def kernel(x, positions, norm1_g, w_in, q_norm_g, k_norm_g, lambda_q1, lambda_k1, lambda_q2, lambda_k2, diff_subln_g, ret_gn_g, ret_gn_b, w_branch_a, w_branch_b, w_out, norm2_g, w_group_router, b_group_router, w_expert_router, b_expert_router, w_gate, w_up, w_down):
    assert x.shape[-1] == D_MODEL and norm1_g.shape[0] == 1, "single-layer, D_MODEL-wide input expected"
    lam4 = jnp.zeros((4, LANES), jnp.float32)
    lam4 = lam4.at[:, :DA_HALF].set(jnp.stack([lambda_q1[0], lambda_k1[0], lambda_q2[0], lambda_k2[0]]))
    return _layer(x, positions, norm1_g[0], w_in[0], q_norm_g[0], k_norm_g[0], lam4, diff_subln_g[0],
                  ret_gn_g[0], ret_gn_b[0], w_branch_a[0], w_branch_b[0], w_out[0], norm2_g[0],
                  w_group_router[0], b_group_router[0], w_expert_router[0], b_expert_router[0],
                  w_gate[0], w_up[0], w_down[0])
```

```python
import functools
import math

import jax
import jax.numpy as jnp
from jax import lax
from jax.experimental import pallas as pl
from jax.experimental.pallas import tpu as pltpu

D_MODEL = 1024
DA_HEADS = 4
DA_HALF = 64
DA_VDIM = 2 * DA_HALF
DA_WIDTH = DA_HEADS * DA_VDIM
ROPE_THETA = 500000.0
ROPE_DIM = DA_HALF // 4
RET_HEADS = 4
RET_KDIM = 128
RET_VDIM = 128
RET_WIDTH = RET_HEADS * RET_VDIM
RET_THETA = 10000.0
N_GROUPS = 4
EXPERTS_PER_GROUP = 8
N_EXPERTS = N_GROUPS * EXPERTS_PER_GROUP
TOP_K = 2
EXPERT_FF = 512
EPS = 1e-6
LAMBDA_INIT = 0.8 - 0.6 * math.exp(-0.3 * 0)

LANES = 128
IN_COLS = 3 * DA_WIDTH + 4 * RET_WIDTH + 2 * D_MODEL
COL_QA, COL_KA, COL_VA = 0, DA_WIDTH, 2 * DA_WIDTH
COL_QR = 3 * DA_WIDTH
COL_KR = COL_QR + RET_WIDTH
COL_VR = COL_KR + RET_WIDTH
COL_GB = COL_VR + RET_WIDTH
COL_GATE_A = COL_GB + RET_WIDTH
COL_GATE_B = COL_GATE_A + D_MODEL

PROJ_ROWS = 512
PROJ_CHUNK = 256
ATT_TILE = 512
ATT_ROWS = 32
RET_CHUNK = 256
MOE_BLOCK = 256
COMBINE_ROWS = 256
GATHER_UNROLL = 8
DISPATCH_ROWS = 512
VMEM_LIMIT = 56 * 1024 * 1024


def _dot(a, b):
    return jnp.dot(a, b, preferred_element_type=jnp.float32)


def _dot_nt(a, b):
    return lax.dot_general(a, b, (((1,), (1,)), ((), ())), preferred_element_type=jnp.float32)


def _dot_tn(a, b):
    return lax.dot_general(a, b, (((0,), (0,)), ((), ())), preferred_element_type=jnp.float32)


def _sigmoid(x):
    return 1.0 / (1.0 + jnp.exp(-x))


def _in_proj_kernel(x_ref, pos_ref, g1_ref, w_ref, gsum_ref, gq_ref, gk_ref, fa_ref, fr_ref,
                    o_ref, h_scr):
    x = x_ref[...]
    h = x * lax.rsqrt(jnp.mean(x * x, axis=-1, keepdims=True) + EPS) * g1_ref[...]
    h_scr[...] = h.astype(jnp.bfloat16)
    rows = x.shape[0]
    pos = pos_ref[...].astype(jnp.float32)

    lane = lax.broadcasted_iota(jnp.int32, (rows, LANES), 1)
    ang_a = pos * fa_ref[...]
    cos_a, sin_a = jnp.cos(ang_a), jnp.sin(ang_a)
    lane64 = lane % DA_HALF
    half_a = ROPE_DIM // 2
    c_a = jnp.where(lane64 < ROPE_DIM, cos_a, 1.0)
    s_lo = jnp.where(lane64 < half_a, -sin_a, 0.0)
    s_hi = jnp.where((lane64 >= half_a) & (lane64 < ROPE_DIM), sin_a, 0.0)
    c_a2 = jnp.concatenate([c_a, c_a], axis=1)
    s_lo2 = jnp.concatenate([s_lo, s_lo], axis=1)
    s_hi2 = jnp.concatenate([s_hi, s_hi], axis=1)
    ang_r = pos * fr_ref[...]
    cos_r, sin_r = jnp.cos(ang_r), jnp.sin(ang_r)
    s_r = jnp.where(lane < RET_KDIM // 2, -sin_r, sin_r)
    c_r2 = jnp.concatenate([cos_r, cos_r], axis=1)
    s_r2 = jnp.concatenate([s_r, s_r], axis=1)

    def qk_norm_rope(y, g, scale):
        ss = y * y
        hi = ss.astype(jnp.bfloat16)
        lo = (ss - hi.astype(jnp.float32)).astype(jnp.bfloat16)
        gs = _dot(hi, gsum_ref[...]) + _dot(lo, gsum_ref[...])
        n = y * lax.rsqrt(gs * (1.0 / DA_HALF) + EPS) * g
        up = pltpu.roll(n, PROJ_CHUNK - half_a, axis=1)
        dn = pltpu.roll(n, half_a, axis=1)
        r = n * c_a2 + up * s_lo2 + dn * s_hi2
        return r * scale if scale != 1.0 else r

    def ret_rope(y, scale):
        halves = [pltpu.roll(y[:, i * LANES:(i + 1) * LANES], RET_KDIM // 2, axis=1)
                  for i in range(PROJ_CHUNK // LANES)]
        sw = jnp.concatenate(halves, axis=1)
        r = y * c_r2 + sw * s_r2
        return r * scale if scale != 1.0 else r

    for c in range(IN_COLS // PROJ_CHUNK):
        c0 = c * PROJ_CHUNK
        y = _dot(h_scr[...], w_ref[:, c0:c0 + PROJ_CHUNK])
        if c0 < COL_KA:
            y = qk_norm_rope(y, gq_ref[...], DA_HALF ** -0.5)
        elif c0 < COL_VA:
            y = qk_norm_rope(y, gk_ref[...], 1.0)
        elif c0 < COL_QR:
            pass
        elif c0 < COL_KR:
            y = ret_rope(y, 1.0)
        elif c0 < COL_VR:
            y = ret_rope(y, RET_KDIM ** -0.5)
        elif c0 < COL_GB:
            pass
        elif c0 < COL_GATE_A:
            y = y * _sigmoid(y)
        else:
            y = _sigmoid(y)
        o_ref[:, c0:c0 + PROJ_CHUNK] = y.astype(o_ref.dtype)


def _in_proj(x2, pos2, g1, w_in, gq, gk):
    n = x2.shape[0]
    tm = min(PROJ_ROWS, n)
    grp = jnp.arange(PROJ_CHUNK) // DA_HALF
    gsum = (grp[:, None] == grp[None, :]).astype(jnp.bfloat16)
    half_a = ROPE_DIM // 2
    freq_a = jnp.power(jnp.float32(ROPE_THETA), -2.0 * jnp.arange(half_a, dtype=jnp.float32) / ROPE_DIM)
    fa = freq_a[jnp.arange(LANES) % half_a][None, :]
    half_r = RET_KDIM // 2
    freq_r = jnp.power(jnp.float32(RET_THETA), -2.0 * jnp.arange(half_r, dtype=jnp.float32) / RET_KDIM)
    fr = freq_r[jnp.arange(LANES) % half_r][None, :]
    reps = PROJ_CHUNK // DA_HALF
    full = lambda shape: pl.BlockSpec(shape, lambda i: (0,) * len(shape))
    return pl.pallas_call(
        _in_proj_kernel,
        grid=(n // tm,),
        in_specs=[
            pl.BlockSpec((tm, D_MODEL), lambda i: (i, 0)),
            pl.BlockSpec((tm, 1), lambda i: (i, 0)),
            full((1, D_MODEL)),
            full((D_MODEL, IN_COLS)),
            full((PROJ_CHUNK, PROJ_CHUNK)),
            full((1, PROJ_CHUNK)),
            full((1, PROJ_CHUNK)),
            full((1, LANES)),
            full((1, LANES)),
        ],
        out_specs=pl.BlockSpec((tm, IN_COLS), lambda i: (i, 0)),
        out_shape=jax.ShapeDtypeStruct((n, IN_COLS), jnp.bfloat16),
        scratch_shapes=[pltpu.VMEM((tm, D_MODEL), jnp.bfloat16)],
        compiler_params=pltpu.CompilerParams(dimension_semantics=("arbitrary",),
                                             vmem_limit_bytes=VMEM_LIMIT),
        name="in_proj",
    )(x2, pos2, g1.reshape(1, D_MODEL), w_in.astype(jnp.bfloat16), gsum,
      jnp.tile(gq, reps)[None, :], jnp.tile(gk, reps)[None, :], fa, fr)


def _diff_attn_kernel(q_ref, k_ref, v_ref, lam_ref, gsub_ref, o_ref,
                      qs_scr, vx_scr, s0_scr, s1_scr, p_scr, m_scr, alpha_scr, acc_scr):
    i = pl.program_id(2)
    t = q_ref.shape[0]

    @pl.when(i == 0)
    def _():
        vx_scr[:, :DA_VDIM] = v_ref[...]
        vx_scr[:, DA_VDIM:] = jnp.ones((vx_scr.shape[0], LANES), vx_scr.dtype)

    q = q_ref[...]
    lane = lax.broadcasted_iota(jnp.int32, q.shape, 1)
    zero = jnp.zeros_like(q)
    qs_scr[:t] = jnp.where(lane < DA_HALF, q, zero)
    qs_scr[t:] = jnp.where(lane >= DA_HALF, q, zero)
    m_scr[...] = jnp.full(m_scr.shape, -jnp.inf, jnp.float32)
    acc_scr[...] = jnp.zeros(acc_scr.shape, jnp.float32)

    def scores(j, s_ref):
        start = pl.multiple_of(j * t, t)
        s_ref[...] = _dot_nt(qs_scr[...], k_ref[pl.ds(start, t), :])

    def softmax_pv(j, s_ref, masked):
        for c in range(2 * t // ATT_ROWS):
            rows = pl.ds(c * ATT_ROWS, ATT_ROWS)
            s = s_ref[rows, :]
            if masked:
                r = lax.broadcasted_iota(jnp.int32, s.shape, 0) + (c * ATT_ROWS) % t
                col = lax.broadcasted_iota(jnp.int32, s.shape, 1)
                s = jnp.where(col <= r, s, -jnp.inf)
            m_prev = m_scr[rows, :]
            m_new = jnp.maximum(m_prev, jnp.max(s, axis=-1, keepdims=True))
            alpha_scr[rows, :] = jnp.exp(m_prev - m_new)
            m_scr[rows, :] = m_new
            p = jnp.exp(s - jnp.concatenate([m_new] * (t // LANES), axis=1))
            p_scr[rows, :] = p.astype(p_scr.dtype)
        start = pl.multiple_of(j * t, t)
        pv = _dot(p_scr[...], vx_scr[pl.ds(start, t), :])
        alpha = alpha_scr[...]
        for half in range(2):
            cols = pl.ds(half * LANES, LANES)
            acc_scr[:, cols] = alpha * acc_scr[:, cols] + pv[:, half * LANES:(half + 1) * LANES]

    scores(0, s0_scr)

    def pair(jj, carry):
        j = 2 * jj
        scores(j + 1, s1_scr)
        softmax_pv(j, s0_scr, False)
        scores(j + 2, s0_scr)
        softmax_pv(j + 1, s1_scr, False)
        return carry

    lax.fori_loop(0, i // 2, pair, 0)

    @pl.when(i % 2 == 1)
    def _():
        scores(i, s1_scr)
        softmax_pv(i - 1, s0_scr, False)
        softmax_pv(i, s1_scr, True)

    @pl.when(i % 2 == 0)
    def _():
        softmax_pv(i, s0_scr, True)

    lam4 = lam_ref[...]
    lam = (jnp.exp(jnp.sum(lam4[0:1] * lam4[1:2], axis=-1, keepdims=True))
           - jnp.exp(jnp.sum(lam4[2:3] * lam4[3:4], axis=-1, keepdims=True)) + LAMBDA_INIT)
    o_all = acc_scr[:, :DA_VDIM] / acc_scr[:, DA_VDIM:]
    o = o_all[:t] - lam * o_all[t:]
    o = o * lax.rsqrt(jnp.mean(o * o, axis=-1, keepdims=True) + EPS) * gsub_ref[...] * (1.0 - LAMBDA_INIT)
    o_ref[...] = o.astype(o_ref.dtype)


def _diff_attn(proj, lam4, gsub, batch, seq):
    n = proj.shape[0]
    t = min(ATT_TILE, seq)
    nq = seq // t
    qb, kb, vb = COL_QA // LANES, COL_KA // LANES, COL_VA // LANES
    return pl.pallas_call(
        _diff_attn_kernel,
        grid=(batch, DA_HEADS, nq),
        in_specs=[
            pl.BlockSpec((t, LANES), lambda b, h, i: (b * nq + i, qb + h)),
            pl.BlockSpec((seq, LANES), lambda b, h, i: (b, kb + h)),
            pl.BlockSpec((seq, LANES), lambda b, h, i: (b, vb + h)),
            pl.BlockSpec((4, LANES), lambda b, h, i: (0, 0)),
            pl.BlockSpec((1, LANES), lambda b, h, i: (0, 0)),
        ],
        out_specs=pl.BlockSpec((t, LANES), lambda b, h, i: (b * nq + i, h)),
        out_shape=jax.ShapeDtypeStruct((n, DA_WIDTH), jnp.bfloat16),
        scratch_shapes=[pltpu.VMEM((2 * t, LANES), jnp.bfloat16),
                        pltpu.VMEM((seq, DA_VDIM + LANES), jnp.bfloat16),
                        pltpu.VMEM((2 * t, t), jnp.float32),
                        pltpu.VMEM((2 * t, t), jnp.float32),
                        pltpu.VMEM((2 * t, t), jnp.bfloat16),
                        pltpu.VMEM((2 * t, LANES), jnp.float32),
                        pltpu.VMEM((2 * t, LANES), jnp.float32),
                        pltpu.VMEM((2 * t, DA_VDIM + LANES), jnp.float32)],
        compiler_params=pltpu.CompilerParams(dimension_semantics=("arbitrary",) * 3,
                                             vmem_limit_bytes=VMEM_LIMIT),
        name="diff_attn",
    )(proj, proj, proj, lam4, gsub)


def _retention_kernel(q_ref, k_ref, v_ref, g_ref, gng_ref, gnb_ref, o_ref, r_scr, *, chunk):
    hf = jnp.full((1, 1), pl.program_id(1), jnp.int32).astype(jnp.float32)
    log_g = jnp.log1p(-jnp.exp2(-5.0 - hf))
    ri = lax.broadcasted_iota(jnp.int32, (chunk, chunk), 0)
    ci = lax.broadcasted_iota(jnp.int32, (chunk, chunk), 1)
    rel = (ri - ci).astype(jnp.float32)
    dmask = jnp.where(rel >= 0, jnp.exp(jnp.maximum(rel, 0.0) * log_g), 0.0)
    idx = lax.broadcasted_iota(jnp.int32, (chunk, 1), 0).astype(jnp.float32)
    zeta = jnp.exp((chunk - 1 - idx) * log_g)
    xi = jnp.exp((idx + 1.0) * log_g)
    g_chunk = jnp.exp(chunk * log_g)
    r_scr[...] = jnp.zeros(r_scr.shape, jnp.float32)
    gng = gng_ref[...]
    gnb = gnb_ref[...]

    def body(n, carry):
        start = pl.multiple_of(n * chunk, chunk)
        q = q_ref[pl.ds(start, chunk), :]
        k = k_ref[pl.ds(start, chunk), :]
        v = v_ref[pl.ds(start, chunk), :]
        s = _dot_nt(q, k) * dmask
        r_old = r_scr[...]
        o = _dot(s.astype(jnp.bfloat16), v) + xi * _dot(q, r_old.astype(jnp.bfloat16))
        kz = (k.astype(jnp.float32) * zeta).astype(jnp.bfloat16)
        r_scr[...] = g_chunk * r_old + _dot_tn(kz, v)
        mu = jnp.mean(o, axis=-1, keepdims=True)
        d = o - mu
        var = jnp.mean(d * d, axis=-1, keepdims=True)
        y = d * lax.rsqrt(var + EPS) * gng + gnb
        y = y * g_ref[pl.ds(start, chunk), :].astype(jnp.float32)
        o_ref[pl.ds(start, chunk), :] = y.astype(o_ref.dtype)
        return carry

    lax.fori_loop(0, q_ref.shape[0] // chunk, body, 0)


def _retention(proj, gn_g, gn_b, batch, seq):
    n = proj.shape[0]
    chunk = min(RET_CHUNK, seq)
    col = lambda c0: (lambda b, h: (b, c0 // LANES + h))
    return pl.pallas_call(
        functools.partial(_retention_kernel, chunk=chunk),
        grid=(batch, RET_HEADS),
        in_specs=[
            pl.BlockSpec((seq, LANES), col(COL_QR)),
            pl.BlockSpec((seq, LANES), col(COL_KR)),
            pl.BlockSpec((seq, LANES), col(COL_VR)),
            pl.BlockSpec((seq, LANES), col(COL_GB)),
            pl.BlockSpec((1, LANES), lambda b, h: (0, h)),
            pl.BlockSpec((1, LANES), lambda b, h: (0, h)),
        ],
        out_specs=pl.BlockSpec((seq, LANES), lambda b, h: (b, h)),
        out_shape=jax.ShapeDtypeStruct((n, RET_WIDTH), jnp.bfloat16),
        scratch_shapes=[pltpu.VMEM((RET_KDIM, RET_VDIM), jnp.float32)],
        compiler_params=pltpu.CompilerParams(dimension_semantics=("arbitrary",) * 2,
                                             vmem_limit_bytes=VMEM_LIMIT),
        name="retention",
    )(proj, proj, proj, proj, gn_g.reshape(1, RET_WIDTH), gn_b.reshape(1, RET_WIDTH))


def _merge_kernel(x_ref, oa_ref, ob_ref, sa0_ref, sa1_ref, sb0_ref, sb1_ref, wa_ref, wb_ref, wo_ref,
                  g2_ref, wr_hi_ref, wr_lo_ref, br_ref, tri_ref, x1_ref, h2_ref, route_ref, counts_ref,
                  base_scr):
    ya = _dot(oa_ref[...], wa_ref[...])
    yb = _dot(ob_ref[...], wb_ref[...])
    sa = jnp.concatenate([sa0_ref[...], sa1_ref[...]], axis=1).astype(jnp.float32)
    sb = jnp.concatenate([sb0_ref[...], sb1_ref[...]], axis=1).astype(jnp.float32)
    merged = sa * ya + sb * yb
    x1 = x_ref[...] + _dot(merged.astype(jnp.bfloat16), wo_ref[...])
    x1_ref[...] = x1
    h2 = x1 * lax.rsqrt(jnp.mean(x1 * x1, axis=-1, keepdims=True) + EPS) * g2_ref[...]
    h2_ref[...] = h2

    hi = h2.astype(jnp.bfloat16)
    lo = (h2 - hi.astype(jnp.float32)).astype(jnp.bfloat16)
    logits = (_dot(hi, wr_hi_ref[...]) + _dot(lo, wr_hi_ref[...]) + _dot(hi, wr_lo_ref[...])
              + br_ref[...])
    lane = lax.broadcasted_iota(jnp.int32, logits.shape, 1)
    neg = -jnp.inf
    gl = jnp.where(lane < N_GROUPS, logits, neg)
    gmax = jnp.max(gl, axis=-1, keepdims=True)
    g_idx = jnp.min(jnp.where(gl == gmax, lane, LANES), axis=-1, keepdims=True)
    p_g = 1.0 / jnp.sum(jnp.exp(gl - gmax), axis=-1, keepdims=True)
    e_lo = N_GROUPS + EXPERTS_PER_GROUP * g_idx
    el = jnp.where((lane >= e_lo) & (lane < e_lo + EXPERTS_PER_GROUP), logits, neg)
    v1 = jnp.max(el, axis=-1, keepdims=True)
    i1 = jnp.min(jnp.where(el == v1, lane, LANES), axis=-1, keepdims=True)
    el2 = jnp.where(lane == i1, neg, el)
    v2 = jnp.max(el2, axis=-1, keepdims=True)
    i2 = jnp.min(jnp.where(el2 == v2, lane, LANES), axis=-1, keepdims=True)
    t = jnp.exp(v2 - v1)
    w1 = p_g / (1.0 + t)
    w2 = p_g * t / (1.0 + t)
    e1 = i1 - N_GROUPS
    e2 = i2 - N_GROUPS

    @pl.when(pl.program_id(0) == 0)
    def _():
        base_scr[...] = jnp.zeros(base_scr.shape, jnp.float32)

    oh1 = lane == e1
    oh2 = lane == e2
    picked = jnp.where(oh1 | oh2, 1.0, 0.0)
    before = _dot(tri_ref[...], picked.astype(jnp.bfloat16)) + base_scr[0:1, :]
    rank1 = jnp.sum(jnp.where(oh1, before, 0.0), axis=-1, keepdims=True)
    rank2 = jnp.sum(jnp.where(oh2, before, 0.0), axis=-1, keepdims=True)
    base_scr[...] = base_scr[...] + jnp.sum(picked, axis=0, keepdims=True)
    counts_ref[...] = base_scr[...]

    cols = [e1.astype(jnp.float32), e2.astype(jnp.float32), w1, w2, rank1, rank2]
    route = jnp.zeros(logits.shape, jnp.float32)
    for c, val in enumerate(cols):
        route = jnp.where(lane == c, val, route)
    route_ref[...] = route


def _merge(x2, oa, ob, proj, wa, wb, wo, g2, w_gr, b_gr, w_er, b_er):
    n = x2.shape[0]
    tm = min(PROJ_ROWS, n)
    half = D_MODEL // 2
    wr = jnp.zeros((D_MODEL, LANES), jnp.float32)
    wr = wr.at[:, :N_GROUPS].set(w_gr).at[:, N_GROUPS:N_GROUPS + N_EXPERTS].set(w_er)
    wr_hi = wr.astype(jnp.bfloat16)
    wr_lo = (wr - wr_hi.astype(jnp.float32)).astype(jnp.bfloat16)
    br = jnp.zeros((1, LANES), jnp.float32)
    br = br.at[0, :N_GROUPS].set(b_gr).at[0, N_GROUPS:N_GROUPS + N_EXPERTS].set(b_er)
    tri = (jnp.arange(tm)[:, None] > jnp.arange(tm)[None, :]).astype(jnp.bfloat16)
    full = lambda shape: pl.BlockSpec(shape, lambda i: (0,) * len(shape))
    gate = lambda c0: pl.BlockSpec((tm, half), lambda i: (i, c0 // half))
    return pl.pallas_call(
        _merge_kernel,
        grid=(n // tm,),
        in_specs=[
            pl.BlockSpec((tm, D_MODEL), lambda i: (i, 0)),
            pl.BlockSpec((tm, DA_WIDTH), lambda i: (i, 0)),
            pl.BlockSpec((tm, RET_WIDTH), lambda i: (i, 0)),
            gate(COL_GATE_A), gate(COL_GATE_A + half), gate(COL_GATE_B), gate(COL_GATE_B + half),
            full((DA_WIDTH, D_MODEL)), full((RET_WIDTH, D_MODEL)), full((D_MODEL, D_MODEL)),
            full((1, D_MODEL)), full((D_MODEL, LANES)), full((D_MODEL, LANES)), full((1, LANES)),
            full((tm, tm)),
        ],
        out_specs=[
            pl.BlockSpec((tm, D_MODEL), lambda i: (i, 0)),
            pl.BlockSpec((tm, D_MODEL), lambda i: (i, 0)),
            pl.BlockSpec((tm, LANES), lambda i: (i, 0)),
            pl.BlockSpec((8, LANES), lambda i: (0, 0)),
        ],
        out_shape=[
            jax.ShapeDtypeStruct((n, D_MODEL), jnp.float32),
            jax.ShapeDtypeStruct((n, D_MODEL), jnp.float32),
            jax.ShapeDtypeStruct((n, LANES), jnp.float32),
            jax.ShapeDtypeStruct((8, LANES), jnp.float32),
        ],
        scratch_shapes=[pltpu.VMEM((8, LANES), jnp.float32)],
        compiler_params=pltpu.CompilerParams(dimension_semantics=("arbitrary",),
                                             vmem_limit_bytes=VMEM_LIMIT),
        name="merge",
    )(x2, oa, ob, proj, proj, proj, proj, wa.astype(jnp.bfloat16), wb.astype(jnp.bfloat16),
      wo.astype(jnp.bfloat16), g2.reshape(1, D_MODEL), wr_hi, wr_lo, br, tri)


def _row_dma_loop(count, make_copy, start):
    def trip(g, carry):
        for u in range(GATHER_UNROLL):
            copy = make_copy(g * GATHER_UNROLL + u)
            if start:
                copy.start()
            else:
                copy.wait()
        return carry
    lax.fori_loop(0, count // GATHER_UNROLL, trip, 0)


def _dispatch_kernel(dest_ref, seg_ref, h2_ref, xs_hbm, zero_buf, sem, zsem, *, rows):
    i = pl.program_id(0)
    n = pl.num_programs(0) * rows

    def zero_copy(e):
        start = pl.multiple_of(seg_ref[e] - MOE_BLOCK, MOE_BLOCK)
        return pltpu.make_async_copy(zero_buf, xs_hbm.at[pl.ds(start, MOE_BLOCK)], zsem)

    @pl.when(i == 0)
    def _():
        zero_buf[...] = jnp.zeros(zero_buf.shape, zero_buf.dtype)
        for e in range(N_EXPERTS):
            prev = seg_ref[e - 1] if e else 0
            @pl.when(seg_ref[e] > prev)
            def _():
                zero_copy(e).start()
        for e in range(N_EXPERTS):
            prev = seg_ref[e - 1] if e else 0
            @pl.when(seg_ref[e] > prev)
            def _():
                zero_copy(e).wait()

    def row_copy(k):
        def make(r):
            return pltpu.make_async_copy(h2_ref.at[pl.ds(r, 1)],
                                         xs_hbm.at[pl.ds(dest_ref[k * n + i * rows + r], 1)], sem)
        return make

    for k in range(TOP_K):
        _row_dma_loop(rows, row_copy(k), True)
    for k in range(TOP_K):
        _row_dma_loop(rows, row_copy(k), False)


def _dispatch(h2, dest_kmajor, seg_end, p):
    n = h2.shape[0]
    rows = min(DISPATCH_ROWS, n)
    return pl.pallas_call(
        functools.partial(_dispatch_kernel, rows=rows),
        grid_spec=pltpu.PrefetchScalarGridSpec(
            num_scalar_prefetch=2,
            grid=(n // rows,),
            in_specs=[pl.BlockSpec((rows, D_MODEL), lambda i, d, sg: (i, 0))],
            out_specs=pl.BlockSpec(memory_space=pl.ANY),
            scratch_shapes=[pltpu.VMEM((MOE_BLOCK, D_MODEL), jnp.float32),
                            pltpu.SemaphoreType.DMA(()), pltpu.SemaphoreType.DMA(())],
        ),
        out_shape=jax.ShapeDtypeStruct((p, D_MODEL), jnp.float32),
        compiler_params=pltpu.CompilerParams(dimension_semantics=("arbitrary",),
                                             vmem_limit_bytes=VMEM_LIMIT),
        name="dispatch",
    )(dest_kmajor, seg_end, h2)


def _expert_kernel(blk_e_ref, n_used_ref, x_ref, wg_ref, wu_ref, wd_ref, o_ref):
    i = pl.program_id(0)

    @pl.when(i < n_used_ref[0])
    def _():
        x = x_ref[...].astype(jnp.bfloat16)
        a = _dot(x, wg_ref[0])
        u = _dot(x, wu_ref[0])
        hmid = (a * _sigmoid(a) * u).astype(jnp.bfloat16)
        o_ref[...] = _dot(hmid, wd_ref[0])

    @pl.when(i >= n_used_ref[0])
    def _():
        o_ref[...] = jnp.zeros(o_ref.shape, o_ref.dtype)


def _experts(xs, blk_expert, n_used, w_gate, w_up, w_down):
    p = xs.shape[0]
    live = lambda i, be, nu: jnp.minimum(i, nu[0] - 1)
    return pl.pallas_call(
        _expert_kernel,
        grid_spec=pltpu.PrefetchScalarGridSpec(
            num_scalar_prefetch=2,
            grid=(p // MOE_BLOCK,),
            in_specs=[
                pl.BlockSpec((MOE_BLOCK, D_MODEL), lambda i, be, nu: (live(i, be, nu), 0)),
                pl.BlockSpec((1, D_MODEL, EXPERT_FF), lambda i, be, nu: (be[i], 0, 0)),
                pl.BlockSpec((1, D_MODEL, EXPERT_FF), lambda i, be, nu: (be[i], 0, 0)),
                pl.BlockSpec((1, EXPERT_FF, D_MODEL), lambda i, be, nu: (be[i], 0, 0)),
            ],
            out_specs=pl.BlockSpec((MOE_BLOCK, D_MODEL), lambda i, be, nu: (i, 0)),
        ),
        out_shape=jax.ShapeDtypeStruct((p, D_MODEL), jnp.float32),
        compiler_params=pltpu.CompilerParams(dimension_semantics=("arbitrary",),
                                             vmem_limit_bytes=VMEM_LIMIT),
        name="experts",
    )(blk_expert, n_used, xs, w_gate.astype(jnp.bfloat16), w_up.astype(jnp.bfloat16),
      w_down.astype(jnp.bfloat16))


def _combine_kernel(dest_ref, x1_ref, route_ref, ys_hbm, o_ref, y_buf, sem, *, rows):
    i = pl.program_id(0)
    slot = i % 2
    n = pl.num_programs(0) * rows

    def row_copy(step, s, k):
        def make(r):
            return pltpu.make_async_copy(ys_hbm.at[pl.ds(dest_ref[k * n + step * rows + r], 1)],
                                         y_buf.at[s, k, pl.ds(r, 1)], sem.at[s])
        return make

    def start(step, s):
        for k in range(TOP_K):
            _row_dma_loop(rows, row_copy(step, s, k), True)

    @pl.when(i == 0)
    def _():
        start(0, 0)

    @pl.when(i + 1 < pl.num_programs(0))
    def _():
        start(i + 1, 1 - slot)

    for k in range(TOP_K):
        _row_dma_loop(rows, row_copy(i, slot, k), False)
    route = route_ref[...]
    o_ref[...] = x1_ref[...] + route[:, 2:3] * y_buf[slot, 0] + route[:, 3:4] * y_buf[slot, 1]


def _combine(x1, ys, route, dest_kmajor):
    n = x1.shape[0]
    tm = min(COMBINE_ROWS, n)
    return pl.pallas_call(
        functools.partial(_combine_kernel, rows=tm),
        grid_spec=pltpu.PrefetchScalarGridSpec(
            num_scalar_prefetch=1,
            grid=(n // tm,),
            in_specs=[
                pl.BlockSpec((tm, D_MODEL), lambda i, d: (i, 0)),
                pl.BlockSpec((tm, LANES), lambda i, d: (i, 0)),
                pl.BlockSpec(memory_space=pl.ANY),
            ],
            out_specs=pl.BlockSpec((tm, D_MODEL), lambda i, d: (i, 0)),
            scratch_shapes=[pltpu.VMEM((2, TOP_K, tm, D_MODEL), jnp.float32),
                            pltpu.SemaphoreType.DMA((2,))],
        ),
        out_shape=jax.ShapeDtypeStruct((n, D_MODEL), jnp.float32),
        compiler_params=pltpu.CompilerParams(dimension_semantics=("arbitrary",),
                                             vmem_limit_bytes=VMEM_LIMIT),
        name="combine",
    )(dest_kmajor, x1, route, ys)


def _dispatch_plan(route, counts, n):
    counts = counts[0, :N_EXPERTS].astype(jnp.int32)
    padded = ((counts + MOE_BLOCK - 1) // MOE_BLOCK) * MOE_BLOCK
    seg_end = jnp.cumsum(padded).astype(jnp.int32)
    seg_start = seg_end - padded
    e = route[:, 0:TOP_K].astype(jnp.int32)
    rank = route[:, 4:4 + TOP_K].astype(jnp.int32)
    dest = (seg_start[e] + rank).T.reshape(-1)
    p = n * TOP_K + N_EXPERTS * MOE_BLOCK
    blk_start = jnp.arange(p // MOE_BLOCK, dtype=jnp.int32) * MOE_BLOCK
    blk_expert = jnp.clip(jnp.searchsorted(seg_end, blk_start, side='right'), 0, N_EXPERTS - 1)
    n_used = (seg_end[-1] // MOE_BLOCK).reshape(1)
    return dest, seg_end, blk_expert.astype(jnp.int32), n_used, p


def _layer(x, positions, norm1_g, w_in, q_norm_g, k_norm_g, lam4, diff_subln_g, ret_gn_g, ret_gn_b,
           w_branch_a, w_branch_b, w_out, norm2_g, w_gr, b_gr, w_er, b_er, w_gate, w_up, w_down):
    batch, seq, _ = x.shape
    n = batch * seq
    x2 = x.reshape(n, D_MODEL)
    proj = _in_proj(x2, positions.reshape(n, 1), norm1_g, w_in, q_norm_g, k_norm_g)
    oa = _diff_attn(proj, lam4, diff_subln_g.reshape(1, DA_VDIM), batch, seq)
    ob = _retention(proj, ret_gn_g, ret_gn_b, batch, seq)
    x1, h2, route, counts = _merge(x2, oa, ob, proj, w_branch_a, w_branch_b, w_out, norm2_g,
                                   w_gr, b_gr, w_er, b_er)
    dest, seg_end, blk_expert, n_used, p = _dispatch_plan(route, counts, n)
    xs = _dispatch(h2, dest, seg_end, p)
    ys = _experts(xs, blk_expert, n_used, w_gate, w_up, w_down)
    out = _combine(x1, ys, route, dest)
    return out.reshape(batch, seq, D_MODEL)


def kernel(x, positions, norm1_g, w_in, q_norm_g, k_norm_g, lambda_q1, lambda_k1, lambda_q2, lambda_k2, diff_subln_g, ret_gn_g, ret_gn_b, w_branch_a, w_branch_b, w_out, norm2_g, w_group_router, b_group_router, w_expert_router, b_expert_router, w_gate, w_up, w_down):
    assert x.shape[-1] == D_MODEL and norm1_g.shape[0] == 1, "single-layer, D_MODEL-wide input expected"
    lam4 = jnp.zeros((4, LANES), jnp.float32)
    lam4 = lam4.at[:, :DA_HALF].set(jnp.stack([lambda_q1[0], lambda_k1[0], lambda_q2[0], lambda_k2[0]]))
    return _layer(x, positions, norm1_g[0], w_in[0], q_norm_g[0], k_norm_g[0], lam4, diff_subln_g[0],
                  ret_gn_g[0], ret_gn_b[0], w_branch_a[0], w_branch_b[0], w_out[0], norm2_g[0],
                  w_group_router[0], b_group_router[0], w_expert_router[0], b_expert_router[0],
                  w_gate[0], w_up[0], w_down[0])
```

```python
import functools
import math

import jax
import jax.numpy as jnp
from jax import lax
from jax.experimental import pallas as pl
from jax.experimental.pallas import tpu as pltpu

D_MODEL = 1024
DA_HEADS = 4
DA_HALF = 64
DA_VDIM = 2 * DA_HALF
DA_WIDTH = DA_HEADS * DA_VDIM
ROPE_THETA = 500000.0
ROPE_DIM = DA_HALF // 4
RET_HEADS = 4
RET_KDIM = 128
RET_VDIM = 128
RET_WIDTH = RET_HEADS * RET_VDIM
RET_THETA = 10000.0
N_GROUPS = 4
EXPERTS_PER_GROUP = 8
N_EXPERTS = N_GROUPS * EXPERTS_PER_GROUP
TOP_K = 2
EXPERT_FF = 512
EPS = 1e-6
LAMBDA_INIT = 0.8 - 0.6 * math.exp(-0.3 * 0)

LANES = 128
IN_COLS = 3 * DA_WIDTH + 4 * RET_WIDTH + 2 * D_MODEL
COL_QA, COL_KA, COL_VA = 0, DA_WIDTH, 2 * DA_WIDTH
COL_QR = 3 * DA_WIDTH
COL_KR = COL_QR + RET_WIDTH
COL_VR = COL_KR + RET_WIDTH
COL_GB = COL_VR + RET_WIDTH
COL_GATE_A = COL_GB + RET_WIDTH
COL_GATE_B = COL_GATE_A + D_MODEL

PROJ_ROWS = 512
PROJ_CHUNK = 256
ATT_TILE = 512
ATT_ROWS = 32
RET_CHUNK = 256
MOE_BLOCK = 256
COMBINE_ROWS = 256
GATHER_UNROLL = 8
DISPATCH_ROWS = 512
VMEM_LIMIT = 56 * 1024 * 1024


def _dot(a, b):
    return jnp.dot(a, b, preferred_element_type=jnp.float32)


def _dot_nt(a, b):
    return lax.dot_general(a, b, (((1,), (1,)), ((), ())), preferred_element_type=jnp.float32)


def _dot_tn(a, b):
    return lax.dot_general(a, b, (((0,), (0,)), ((), ())), preferred_element_type=jnp.float32)


def _sigmoid(x):
    return 1.0 / (1.0 + jnp.exp(-x))


def _in_proj_kernel(x_ref, pos_ref, g1_ref, w_ref, gsum_ref, gq_ref, gk_ref, fa_ref, fr_ref,
                    o_ref, h_scr):
    x = x_ref[...]
    h = x * lax.rsqrt(jnp.mean(x * x, axis=-1, keepdims=True) + EPS) * g1_ref[...]
    h_scr[...] = h.astype(jnp.bfloat16)
    rows = x.shape[0]
    pos = pos_ref[...].astype(jnp.float32)

    lane = lax.broadcasted_iota(jnp.int32, (rows, LANES), 1)
    ang_a = pos * fa_ref[...]
    cos_a, sin_a = jnp.cos(ang_a), jnp.sin(ang_a)
    lane64 = lane % DA_HALF
    half_a = ROPE_DIM // 2
    c_a = jnp.where(lane64 < ROPE_DIM, cos_a, 1.0)
    s_lo = jnp.where(lane64 < half_a, -sin_a, 0.0)
    s_hi = jnp.where((lane64 >= half_a) & (lane64 < ROPE_DIM), sin_a, 0.0)
    c_a2 = jnp.concatenate([c_a, c_a], axis=1)
    s_lo2 = jnp.concatenate([s_lo, s_lo], axis=1)
    s_hi2 = jnp.concatenate([s_hi, s_hi], axis=1)
    ang_r = pos * fr_ref[...]
    cos_r, sin_r = jnp.cos(ang_r), jnp.sin(ang_r)
    s_r = jnp.where(lane < RET_KDIM // 2, -sin_r, sin_r)
    c_r2 = jnp.concatenate([cos_r, cos_r], axis=1)
    s_r2 = jnp.concatenate([s_r, s_r], axis=1)

    def qk_norm_rope(y, g, scale):
        ss = y * y
        hi = ss.astype(jnp.bfloat16)
        lo = (ss - hi.astype(jnp.float32)).astype(jnp.bfloat16)
        gs = _dot(hi, gsum_ref[...]) + _dot(lo, gsum_ref[...])
        n = y * lax.rsqrt(gs * (1.0 / DA_HALF) + EPS) * g
        up = pltpu.roll(n, PROJ_CHUNK - half_a, axis=1)
        dn = pltpu.roll(n, half_a, axis=1)
        r = n * c_a2 + up * s_lo2 + dn * s_hi2
        return r * scale if scale != 1.0 else r

    def ret_rope(y, scale):
        halves = [pltpu.roll(y[:, i * LANES:(i + 1) * LANES], RET_KDIM // 2, axis=1)
                  for i in range(PROJ_CHUNK // LANES)]
        sw = jnp.concatenate(halves, axis=1)
        r = y * c_r2 + sw * s_r2
        return r * scale if scale != 1.0 else r

    for c in range(IN_COLS // PROJ_CHUNK):
        c0 = c * PROJ_CHUNK
        y = _dot(h_scr[...], w_ref[:, c0:c0 + PROJ_CHUNK])
        if c0 < COL_KA:
            y = qk_norm_rope(y, gq_ref[...], DA_HALF ** -0.5)
        elif c0 < COL_VA:
            y = qk_norm_rope(y, gk_ref[...], 1.0)
        elif c0 < COL_QR:
            pass
        elif c0 < COL_KR:
            y = ret_rope(y, 1.0)
        elif c0 < COL_VR:
            y = ret_rope(y, RET_KDIM ** -0.5)
        elif c0 < COL_GB:
            pass
        elif c0 < COL_GATE_A:
            y = y * _sigmoid(y)
        else:
            y = _sigmoid(y)
        o_ref[:, c0:c0 + PROJ_CHUNK] = y.astype(o_ref.dtype)


def _in_proj(x2, pos2, g1, w_in, gq, gk):
    n = x2.shape[0]
    tm = min(PROJ_ROWS, n)
    grp = jnp.arange(PROJ_CHUNK) // DA_HALF
    gsum = (grp[:, None] == grp[None, :]).astype(jnp.bfloat16)
    half_a = ROPE_DIM // 2
    freq_a = jnp.power(jnp.float32(ROPE_THETA), -2.0 * jnp.arange(half_a, dtype=jnp.float32) / ROPE_DIM)
    fa = freq_a[jnp.arange(LANES) % half_a][None, :]
    half_r = RET_KDIM // 2
    freq_r = jnp.power(jnp.float32(RET_THETA), -2.0 * jnp.arange(half_r, dtype=jnp.float32) / RET_KDIM)
    fr = freq_r[jnp.arange(LANES) % half_r][None, :]
    reps = PROJ_CHUNK // DA_HALF
    full = lambda shape: pl.BlockSpec(shape, lambda i: (0,) * len(shape))
    return pl.pallas_call(
        _in_proj_kernel,
        grid=(n // tm,),
        in_specs=[
            pl.BlockSpec((tm, D_MODEL), lambda i: (i, 0)),
            pl.BlockSpec((tm, 1), lambda i: (i, 0)),
            full((1, D_MODEL)),
            full((D_MODEL, IN_COLS)),
            full((PROJ_CHUNK, PROJ_CHUNK)),
            full((1, PROJ_CHUNK)),
            full((1, PROJ_CHUNK)),
            full((1, LANES)),
            full((1, LANES)),
        ],
        out_specs=pl.BlockSpec((tm, IN_COLS), lambda i: (i, 0)),
        out_shape=jax.ShapeDtypeStruct((n, IN_COLS), jnp.bfloat16),
        scratch_shapes=[pltpu.VMEM((tm, D_MODEL), jnp.bfloat16)],
        compiler_params=pltpu.CompilerParams(dimension_semantics=("arbitrary",),
                                             vmem_limit_bytes=VMEM_LIMIT),
        name="in_proj",
    )(x2, pos2, g1.reshape(1, D_MODEL), w_in.astype(jnp.bfloat16), gsum,
      jnp.tile(gq, reps)[None, :], jnp.tile(gk, reps)[None, :], fa, fr)


def _diff_attn_kernel(q_ref, k_ref, v_ref, lam_ref, gsub_ref, o_ref,
                      qs_scr, vx_scr, s0_scr, s1_scr, p_scr, m_scr, alpha_scr, acc_scr):
    i = pl.program_id(2)
    t = q_ref.shape[0]

    @pl.when(i == 0)
    def _():
        vx_scr[:, :DA_VDIM] = v_ref[...]
        vx_scr[:, DA_VDIM:] = jnp.ones((vx_scr.shape[0], LANES), vx_scr.dtype)

    q = q_ref[...]
    lane = lax.broadcasted_iota(jnp.int32, q.shape, 1)
    zero = jnp.zeros_like(q)
    qs_scr[:t] = jnp.where(lane < DA_HALF, q, zero)
    qs_scr[t:] = jnp.where(lane >= DA_HALF, q, zero)
    m_scr[...] = jnp.full(m_scr.shape, -jnp.inf, jnp.float32)
    acc_scr[...] = jnp.zeros(acc_scr.shape, jnp.float32)

    def scores(j, s_ref):
        start = pl.multiple_of(j * t, t)
        s_ref[...] = _dot_nt(qs_scr[...], k_ref[pl.ds(start, t), :])

    def softmax_pv(j, s_ref, masked):
        for c in range(2 * t // ATT_ROWS):
            rows = pl.ds(c * ATT_ROWS, ATT_ROWS)
            s = s_ref[rows, :]
            if masked:
                r = lax.broadcasted_iota(jnp.int32, s.shape, 0) + (c * ATT_ROWS) % t
                col = lax.broadcasted_iota(jnp.int32, s.shape, 1)
                s = jnp.where(col <= r, s, -jnp.inf)
            m_prev = m_scr[rows, :]
            m_new = jnp.maximum(m_prev, jnp.max(s, axis=-1, keepdims=True))
            alpha_scr[rows, :] = jnp.exp(m_prev - m_new)
            m_scr[rows, :] = m_new
            p = jnp.exp(s - jnp.concatenate([m_new] * (t // LANES), axis=1))
            p_scr[rows, :] = p.astype(p_scr.dtype)
        start = pl.multiple_of(j * t, t)
        pv = _dot(p_scr[...], vx_scr[pl.ds(start, t), :])
        alpha = alpha_scr[...]
        for half in range(2):
            cols = pl.ds(half * LANES, LANES)
            acc_scr[:, cols] = alpha * acc_scr[:, cols] + pv[:, half * LANES:(half + 1) * LANES]

    scores(0, s0_scr)

    def pair(jj, carry):
        j = 2 * jj
        scores(j + 1, s1_scr)
        softmax_pv(j, s0_scr, False)
        scores(j + 2, s0_scr)
        softmax_pv(j + 1, s1_scr, False)
        return carry

    lax.fori_loop(0, i // 2, pair, 0)

    @pl.when(i % 2 == 1)
    def _():
        scores(i, s1_scr)
        softmax_pv(i - 1, s0_scr, False)
        softmax_pv(i, s1_scr, True)

    @pl.when(i % 2 == 0)
    def _():
        softmax_pv(i, s0_scr, True)

    lam4 = lam_ref[...]
    lam = (jnp.exp(jnp.sum(lam4[0:1] * lam4[1:2], axis=-1, keepdims=True))
           - jnp.exp(jnp.sum(lam4[2:3] * lam4[3:4], axis=-1, keepdims=True)) + LAMBDA_INIT)
    o_all = acc_scr[:, :DA_VDIM] / acc_scr[:, DA_VDIM:]
    o = o_all[:t] - lam * o_all[t:]
    o = o * lax.rsqrt(jnp.mean(o * o, axis=-1, keepdims=True) + EPS) * gsub_ref[...] * (1.0 - LAMBDA_INIT)
    o_ref[...] = o.astype(o_ref.dtype)


def _diff_attn(proj, lam4, gsub, batch, seq):
    n = proj.shape[0]
    t = min(ATT_TILE, seq)
    nq = seq // t
    qb, kb, vb = COL_QA // LANES, COL_KA // LANES, COL_VA // LANES
    return pl.pallas_call(
        _diff_attn_kernel,
        grid=(batch, DA_HEADS, nq),
        in_specs=[
            pl.BlockSpec((t, LANES), lambda b, h, i: (b * nq + i, qb + h)),
            pl.BlockSpec((seq, LANES), lambda b, h, i: (b, kb + h)),
            pl.BlockSpec((seq, LANES), lambda b, h, i: (b, vb + h)),
            pl.BlockSpec((4, LANES), lambda b, h, i: (0, 0)),
            pl.BlockSpec((1, LANES), lambda b, h, i: (0, 0)),
        ],
        out_specs=pl.BlockSpec((t, LANES), lambda b, h, i: (b * nq + i, h)),
        out_shape=jax.ShapeDtypeStruct((n, DA_WIDTH), jnp.bfloat16),
        scratch_shapes=[pltpu.VMEM((2 * t, LANES), jnp.bfloat16),
                        pltpu.VMEM((seq, DA_VDIM + LANES), jnp.bfloat16),
                        pltpu.VMEM((2 * t, t), jnp.float32),
                        pltpu.VMEM((2 * t, t), jnp.float32),
                        pltpu.VMEM((2 * t, t), jnp.bfloat16),
                        pltpu.VMEM((2 * t, LANES), jnp.float32),
                        pltpu.VMEM((2 * t, LANES), jnp.float32),
                        pltpu.VMEM((2 * t, DA_VDIM + LANES), jnp.float32)],
        compiler_params=pltpu.CompilerParams(dimension_semantics=("arbitrary",) * 3,
                                             vmem_limit_bytes=VMEM_LIMIT),
        name="diff_attn",
    )(proj, proj, proj, lam4, gsub)


def _retention_kernel(q_ref, k_ref, v_ref, g_ref, gng_ref, gnb_ref, o_ref, r_scr, *, chunk):
    hf = jnp.full((1, 1), pl.program_id(1), jnp.int32).astype(jnp.float32)
    log_g = jnp.log1p(-jnp.exp2(-5.0 - hf))
    ri = lax.broadcasted_iota(jnp.int32, (chunk, chunk), 0)
    ci = lax.broadcasted_iota(jnp.int32, (chunk, chunk), 1)
    rel = (ri - ci).astype(jnp.float32)
    dmask = jnp.where(rel >= 0, jnp.exp(jnp.maximum(rel, 0.0) * log_g), 0.0)
    idx = lax.broadcasted_iota(jnp.int32, (chunk, 1), 0).astype(jnp.float32)
    zeta = jnp.exp((chunk - 1 - idx) * log_g)
    xi = jnp.exp((idx + 1.0) * log_g)
    g_chunk = jnp.exp(chunk * log_g)
    r_scr[...] = jnp.zeros(r_scr.shape, jnp.float32)
    gng = gng_ref[...]
    gnb = gnb_ref[...]

    def body(n, carry):
        start = pl.multiple_of(n * chunk, chunk)
        q = q_ref[pl.ds(start, chunk), :]
        k = k_ref[pl.ds(start, chunk), :]
        v = v_ref[pl.ds(start, chunk), :]
        s = _dot_nt(q, k) * dmask
        r_old = r_scr[...]
        o = _dot(s.astype(jnp.bfloat16), v) + xi * _dot(q, r_old.astype(jnp.bfloat16))
        kz = (k.astype(jnp.float32) * zeta).astype(jnp.bfloat16)
        r_scr[...] = g_chunk * r_old + _dot_tn(kz, v)
        mu = jnp.mean(o, axis=-1, keepdims=True)
        d = o - mu
        var = jnp.mean(d * d, axis=-1, keepdims=True)
        y = d * lax.rsqrt(var + EPS) * gng + gnb
        y = y * g_ref[pl.ds(start, chunk), :].astype(jnp.float32)
        o_ref[pl.ds(start, chunk), :] = y.astype(o_ref.dtype)
        return carry

    lax.fori_loop(0, q_ref.shape[0] // chunk, body, 0)


def _retention(proj, gn_g, gn_b, batch, seq):
    n = proj.shape[0]
    chunk = min(RET_CHUNK, seq)
    col = lambda c0: (lambda b, h: (b, c0 // LANES + h))
    return pl.pallas_call(
        functools.partial(_retention_kernel, chunk=chunk),
        grid=(batch, RET_HEADS),
        in_specs=[
            pl.BlockSpec((seq, LANES), col(COL_QR)),
            pl.BlockSpec((seq, LANES), col(COL_KR)),
            pl.BlockSpec((seq, LANES), col(COL_VR)),
            pl.BlockSpec((seq, LANES), col(COL_GB)),
            pl.BlockSpec((1, LANES), lambda b, h: (0, h)),
            pl.BlockSpec((1, LANES), lambda b, h: (0, h)),
        ],
        out_specs=pl.BlockSpec((seq, LANES), lambda b, h: (b, h)),
        out_shape=jax.ShapeDtypeStruct((n, RET_WIDTH), jnp.bfloat16),
        scratch_shapes=[pltpu.VMEM((RET_KDIM, RET_VDIM), jnp.float32)],
        compiler_params=pltpu.CompilerParams(dimension_semantics=("arbitrary",) * 2,
                                             vmem_limit_bytes=VMEM_LIMIT),
        name="retention",
    )(proj, proj, proj, proj, gn_g.reshape(1, RET_WIDTH), gn_b.reshape(1, RET_WIDTH))


def _merge_kernel(x_ref, oa_ref, ob_ref, sa0_ref, sa1_ref, sb0_ref, sb1_ref, wa_ref, wb_ref, wo_ref,
                  g2_ref, wr_hi_ref, wr_lo_ref, br_ref, tri_ref, x1_ref, h2_ref, route_ref, counts_ref,
                  base_scr):
    ya = _dot(oa_ref[...], wa_ref[...])
    yb = _dot(ob_ref[...], wb_ref[...])
    sa = jnp.concatenate([sa0_ref[...], sa1_ref[...]], axis=1).astype(jnp.float32)
    sb = jnp.concatenate([sb0_ref[...], sb1_ref[...]], axis=1).astype(jnp.float32)
    merged = sa * ya + sb * yb
    x1 = x_ref[...] + _dot(merged.astype(jnp.bfloat16), wo_ref[...])
    x1_ref[...] = x1
    h2 = x1 * lax.rsqrt(jnp.mean(x1 * x1, axis=-1, keepdims=True) + EPS) * g2_ref[...]
    h2_ref[...] = h2

    hi = h2.astype(jnp.bfloat16)
    lo = (h2 - hi.astype(jnp.float32)).astype(jnp.bfloat16)
    logits = (_dot(hi, wr_hi_ref[...]) + _dot(lo, wr_hi_ref[...]) + _dot(hi, wr_lo_ref[...])
              + br_ref[...])
    lane = lax.broadcasted_iota(jnp.int32, logits.shape, 1)
    neg = -jnp.inf
    gl = jnp.where(lane < N_GROUPS, logits, neg)
    gmax = jnp.max(gl, axis=-1, keepdims=True)
    g_idx = jnp.min(jnp.where(gl == gmax, lane, LANES), axis=-1, keepdims=True)
    p_g = 1.0 / jnp.sum(jnp.exp(gl - gmax), axis=-1, keepdims=True)
    e_lo = N_GROUPS + EXPERTS_PER_GROUP * g_idx
    el = jnp.where((lane >= e_lo) & (lane < e_lo + EXPERTS_PER_GROUP), logits, neg)
    v1 = jnp.max(el, axis=-1, keepdims=True)
    i1 = jnp.min(jnp.where(el == v1, lane, LANES), axis=-1, keepdims=True)
    el2 = jnp.where(lane == i1, neg, el)
    v2 = jnp.max(el2, axis=-1, keepdims=True)
    i2 = jnp.min(jnp.where(el2 == v2, lane, LANES), axis=-1, keepdims=True)
    t = jnp.exp(v2 - v1)
    w1 = p_g / (1.0 + t)
    w2 = p_g * t / (1.0 + t)
    e1 = i1 - N_GROUPS
    e2 = i2 - N_GROUPS

    @pl.when(pl.program_id(0) == 0)
    def _():
        base_scr[...] = jnp.zeros(base_scr.shape, jnp.float32)

    oh1 = lane == e1
    oh2 = lane == e2
    picked = jnp.where(oh1 | oh2, 1.0, 0.0)
    before = _dot(tri_ref[...], picked.astype(jnp.bfloat16)) + base_scr[0:1, :]
    rank1 = jnp.sum(jnp.where(oh1, before, 0.0), axis=-1, keepdims=True)
    rank2 = jnp.sum(jnp.where(oh2, before, 0.0), axis=-1, keepdims=True)
    base_scr[...] = base_scr[...] + jnp.sum(picked, axis=0, keepdims=True)
    counts_ref[...] = base_scr[...]

    cols = [e1.astype(jnp.float32), e2.astype(jnp.float32), w1, w2, rank1, rank2]
    route = jnp.zeros(logits.shape, jnp.float32)
    for c, val in enumerate(cols):
        route = jnp.where(lane == c, val, route)
    route_ref[...] = route


def _merge(x2, oa, ob, proj, wa, wb, wo, g2, w_gr, b_gr, w_er, b_er):
    n = x2.shape[0]
    tm = min(PROJ_ROWS, n)
    half = D_MODEL // 2
    wr = jnp.zeros((D_MODEL, LANES), jnp.float32)
    wr = wr.at[:, :N_GROUPS].set(w_gr).at[:, N_GROUPS:N_GROUPS + N_EXPERTS].set(w_er)
    wr_hi = wr.astype(jnp.bfloat16)
    wr_lo = (wr - wr_hi.astype(jnp.float32)).astype(jnp.bfloat16)
    br = jnp.zeros((1, LANES), jnp.float32)
    br = br.at[0, :N_GROUPS].set(b_gr).at[0, N_GROUPS:N_GROUPS + N_EXPERTS].set(b_er)
    tri = (jnp.arange(tm)[:, None] > jnp.arange(tm)[None, :]).astype(jnp.bfloat16)
    full = lambda shape: pl.BlockSpec(shape, lambda i: (0,) * len(shape))
    gate = lambda c0: pl.BlockSpec((tm, half), lambda i: (i, c0 // half))
    return pl.pallas_call(
        _merge_kernel,
        grid=(n // tm,),
        in_specs=[
            pl.BlockSpec((tm, D_MODEL), lambda i: (i, 0)),
            pl.BlockSpec((tm, DA_WIDTH), lambda i: (i, 0)),
            pl.BlockSpec((tm, RET_WIDTH), lambda i: (i, 0)),
            gate(COL_GATE_A), gate(COL_GATE_A + half), gate(COL_GATE_B), gate(COL_GATE_B + half),
            full((DA_WIDTH, D_MODEL)), full((RET_WIDTH, D_MODEL)), full((D_MODEL, D_MODEL)),
            full((1, D_MODEL)), full((D_MODEL, LANES)), full((D_MODEL, LANES)), full((1, LANES)),
            full((tm, tm)),
        ],
        out_specs=[
            pl.BlockSpec((tm, D_MODEL), lambda i: (i, 0)),
            pl.BlockSpec((tm, D_MODEL), lambda i: (i, 0)),
            pl.BlockSpec((tm, LANES), lambda i: (i, 0)),
            pl.BlockSpec((8, LANES), lambda i: (0, 0)),
        ],
        out_shape=[
            jax.ShapeDtypeStruct((n, D_MODEL), jnp.float32),
            jax.ShapeDtypeStruct((n, D_MODEL), jnp.float32),
            jax.ShapeDtypeStruct((n, LANES), jnp.float32),
            jax.ShapeDtypeStruct((8, LANES), jnp.float32),
        ],
        scratch_shapes=[pltpu.VMEM((8, LANES), jnp.float32)],
        compiler_params=pltpu.CompilerParams(dimension_semantics=("arbitrary",),
                                             vmem_limit_bytes=VMEM_LIMIT),
        name="merge",
    )(x2, oa, ob, proj, proj, proj, proj, wa.astype(jnp.bfloat16), wb.astype(jnp.bfloat16),
      wo.astype(jnp.bfloat16), g2.reshape(1, D_MODEL), wr_hi, wr_lo, br, tri)


def _row_dma_loop(count, make_copy, start):
    def trip(g, carry):
        for u in range(GATHER_UNROLL):
            copy = make_copy(g * GATHER_UNROLL + u)
            if start:
                copy.start()
            else:
                copy.wait()
        return carry
    lax.fori_loop(0, count // GATHER_UNROLL, trip, 0)


def _dispatch_kernel(dest_ref, seg_ref, h2_ref, xs_hbm, zero_buf, sem, zsem, *, rows):
    i = pl.program_id(0)
    n = pl.num_programs(0) * rows

    def zero_copy(e):
        start = pl.multiple_of(seg_ref[e] - MOE_BLOCK, MOE_BLOCK)
        return pltpu.make_async_copy(zero_buf, xs_hbm.at[pl.ds(start, MOE_BLOCK)], zsem)

    @pl.when(i == 0)
    def _():
        zero_buf[...] = jnp.zeros(zero_buf.shape, zero_buf.dtype)
        for e in range(N_EXPERTS):
            prev = seg_ref[e - 1] if e else 0
            @pl.when(seg_ref[e] > prev)
            def _():
                zero_copy(e).start()
        for e in range(N_EXPERTS):
            prev = seg_ref[e - 1] if e else 0
            @pl.when(seg_ref[e] > prev)
            def _():
                zero_copy(e).wait()

        def tail_copy(b):
            start = pl.multiple_of(b * MOE_BLOCK, MOE_BLOCK)
            return pltpu.make_async_copy(zero_buf, xs_hbm.at[pl.ds(start, MOE_BLOCK)], zsem)

        first_tail = seg_ref[N_EXPERTS - 1] // MOE_BLOCK
        n_blocks = xs_hbm.shape[0] // MOE_BLOCK
        lax.fori_loop(first_tail, n_blocks, lambda b, c: (tail_copy(b).start(), c)[1], 0)
        lax.fori_loop(first_tail, n_blocks, lambda b, c: (tail_copy(b).wait(), c)[1], 0)

    def row_copy(k):
        def make(r):
            return pltpu.make_async_copy(h2_ref.at[pl.ds(r, 1)],
                                         xs_hbm.at[pl.ds(dest_ref[k * n + i * rows + r], 1)], sem)
        return make

    for k in range(TOP_K):
        _row_dma_loop(rows, row_copy(k), True)
    for k in range(TOP_K):
        _row_dma_loop(rows, row_copy(k), False)


def _dispatch(h2, dest_kmajor, seg_end, p):
    n = h2.shape[0]
    rows = min(DISPATCH_ROWS, n)
    return pl.pallas_call(
        functools.partial(_dispatch_kernel, rows=rows),
        grid_spec=pltpu.PrefetchScalarGridSpec(
            num_scalar_prefetch=2,
            grid=(n // rows,),
            in_specs=[pl.BlockSpec((rows, D_MODEL), lambda i, d, sg: (i, 0))],
            out_specs=pl.BlockSpec(memory_space=pl.ANY),
            scratch_shapes=[pltpu.VMEM((MOE_BLOCK, D_MODEL), jnp.float32),
                            pltpu.SemaphoreType.DMA(()), pltpu.SemaphoreType.DMA(())],
        ),
        out_shape=jax.ShapeDtypeStruct((p, D_MODEL), jnp.float32),
        compiler_params=pltpu.CompilerParams(dimension_semantics=("arbitrary",),
                                             vmem_limit_bytes=VMEM_LIMIT),
        name="dispatch",
    )(dest_kmajor, seg_end, h2)


def _expert_kernel(blk_e_ref, n_used_ref, x_ref, wg_ref, wu_ref, wd_ref, o_ref):
    i = pl.program_id(0)

    @pl.when(i < n_used_ref[0])
    def _():
        x = x_ref[...].astype(jnp.bfloat16)
        a = _dot(x, wg_ref[0])
        u = _dot(x, wu_ref[0])
        hmid = (a * _sigmoid(a) * u).astype(jnp.bfloat16)
        o_ref[...] = _dot(hmid, wd_ref[0])

    @pl.when(i >= n_used_ref[0])
    def _():
        o_ref[...] = jnp.zeros(o_ref.shape, o_ref.dtype)


def _experts(xs, blk_expert, n_used, w_gate, w_up, w_down):
    p = xs.shape[0]
    live = lambda i, be, nu: jnp.minimum(i, nu[0] - 1)
    return pl.pallas_call(
        _expert_kernel,
        grid_spec=pltpu.PrefetchScalarGridSpec(
            num_scalar_prefetch=2,
            grid=(p // MOE_BLOCK,),
            in_specs=[
                pl.BlockSpec((MOE_BLOCK, D_MODEL), lambda i, be, nu: (live(i, be, nu), 0)),
                pl.BlockSpec((1, D_MODEL, EXPERT_FF), lambda i, be, nu: (be[i], 0, 0)),
                pl.BlockSpec((1, D_MODEL, EXPERT_FF), lambda i, be, nu: (be[i], 0, 0)),
                pl.BlockSpec((1, EXPERT_FF, D_MODEL), lambda i, be, nu: (be[i], 0, 0)),
            ],
            out_specs=pl.BlockSpec((MOE_BLOCK, D_MODEL), lambda i, be, nu: (i, 0)),
        ),
        out_shape=jax.ShapeDtypeStruct((p, D_MODEL), jnp.float32),
        compiler_params=pltpu.CompilerParams(dimension_semantics=("arbitrary",),
                                             vmem_limit_bytes=VMEM_LIMIT),
        name="experts",
    )(blk_expert, n_used, xs, w_gate.astype(jnp.bfloat16), w_up.astype(jnp.bfloat16),
      w_down.astype(jnp.bfloat16))


def _combine_kernel(dest_ref, x1_ref, route_ref, ys_hbm, o_ref, y_buf, sem, *, rows):
    i = pl.program_id(0)
    slot = i % 2
    n = pl.num_programs(0) * rows

    def row_copy(step, s, k):
        def make(r):
            return pltpu.make_async_copy(ys_hbm.at[pl.ds(dest_ref[k * n + step * rows + r], 1)],
                                         y_buf.at[s, k, pl.ds(r, 1)], sem.at[s])
        return make

    def start(step, s):
        for k in range(TOP_K):
            _row_dma_loop(rows, row_copy(step, s, k), True)

    @pl.when(i == 0)
    def _():
        start(0, 0)

    @pl.when(i + 1 < pl.num_programs(0))
    def _():
        start(i + 1, 1 - slot)

    for k in range(TOP_K):
        _row_dma_loop(rows, row_copy(i, slot, k), False)
    route = route_ref[...]
    o_ref[...] = x1_ref[...] + route[:, 2:3] * y_buf[slot, 0] + route[:, 3:4] * y_buf[slot, 1]


def _combine(x1, ys, route, dest_kmajor):
    n = x1.shape[0]
    tm = min(COMBINE_ROWS, n)
    return pl.pallas_call(
        functools.partial(_combine_kernel, rows=tm),
        grid_spec=pltpu.PrefetchScalarGridSpec(
            num_scalar_prefetch=1,
            grid=(n // tm,),
            in_specs=[
                pl.BlockSpec((tm, D_MODEL), lambda i, d: (i, 0)),
                pl.BlockSpec((tm, LANES), lambda i, d: (i, 0)),
                pl.BlockSpec(memory_space=pl.ANY),
            ],
            out_specs=pl.BlockSpec((tm, D_MODEL), lambda i, d: (i, 0)),
            scratch_shapes=[pltpu.VMEM((2, TOP_K, tm, D_MODEL), jnp.float32),
                            pltpu.SemaphoreType.DMA((2,))],
        ),
        out_shape=jax.ShapeDtypeStruct((n, D_MODEL), jnp.float32),
        compiler_params=pltpu.CompilerParams(dimension_semantics=("arbitrary",),
                                             vmem_limit_bytes=VMEM_LIMIT),
        name="combine",
    )(dest_kmajor, x1, route, ys)


def _dispatch_plan(route, counts, n):
    counts = counts[0, :N_EXPERTS].astype(jnp.int32)
    padded = ((counts + MOE_BLOCK - 1) // MOE_BLOCK) * MOE_BLOCK
    seg_end = jnp.cumsum(padded).astype(jnp.int32)
    seg_start = seg_end - padded
    e = route[:, 0:TOP_K].astype(jnp.int32)
    rank = route[:, 4:4 + TOP_K].astype(jnp.int32)
    dest = (seg_start[e] + rank).T.reshape(-1)
    p = n * TOP_K + N_EXPERTS * MOE_BLOCK
    blk_start = jnp.arange(p // MOE_BLOCK, dtype=jnp.int32) * MOE_BLOCK
    blk_expert = jnp.sum((seg_end[None, :] <= blk_start[:, None]).astype(jnp.int32), axis=1)
    blk_expert = jnp.minimum(blk_expert, N_EXPERTS - 1)
    n_used = (seg_end[-1] // MOE_BLOCK).reshape(1)
    return dest, seg_end, blk_expert, n_used, p


def _layer(x, positions, norm1_g, w_in, q_norm_g, k_norm_g, lam4, diff_subln_g, ret_gn_g, ret_gn_b,
           w_branch_a, w_branch_b, w_out, norm2_g, w_gr, b_gr, w_er, b_er, w_gate, w_up, w_down):
    batch, seq, _ = x.shape
    n = batch * seq
    x2 = x.reshape(n, D_MODEL)
    proj = _in_proj(x2, positions.reshape(n, 1), norm1_g, w_in, q_norm_g, k_norm_g)
    oa = _diff_attn(proj, lam4, diff_subln_g.reshape(1, DA_VDIM), batch, seq)
    ob = _retention(proj, ret_gn_g, ret_gn_b, batch, seq)
    x1, h2, route, counts = _merge(x2, oa, ob, proj, w_branch_a, w_branch_b, w_out, norm2_g,
                                   w_gr, b_gr, w_er, b_er)
    dest, seg_end, blk_expert, n_used, p = _dispatch_plan(route, counts, n)
    xs = _dispatch(h2, dest, seg_end, p)
    ys = _experts(xs, blk_expert, n_used, w_gate, w_up, w_down)
    out = _combine(x1, ys, route, dest)
    return out.reshape(batch, seq, D_MODEL)


def kernel(x, positions, norm1_g, w_in, q_norm_g, k_norm_g, lambda_q1, lambda_k1, lambda_q2, lambda_k2, diff_subln_g, ret_gn_g, ret_gn_b, w_branch_a, w_branch_b, w_out, norm2_g, w_group_router, b_group_router, w_expert_router, b_expert_router, w_gate, w_up, w_down):
    assert x.shape[-1] == D_MODEL and norm1_g.shape[0] == 1, "single-layer, D_MODEL-wide input expected"
    lam4 = jnp.zeros((4, LANES), jnp.float32)
    lam4 = lam4.at[:, :DA_HALF].set(jnp.stack([lambda_q1[0], lambda_k1[0], lambda_q2[0], lambda_k2[0]]))
    return _layer(x, positions, norm1_g[0], w_in[0], q_norm_g[0], k_norm_g[0], lam4, diff_subln_g[0],
                  ret_gn_g[0], ret_gn_b[0], w_branch_a[0], w_branch_b[0], w_out[0], norm2_g[0],
                  w_group_router[0], b_group_router[0], w_expert_router[0], b_expert_router[0],
                  w_gate[0], w_up[0], w_down[0])
```

```python
import functools
import math

import jax
import jax.numpy as jnp
from jax import lax
from jax.experimental import pallas as pl
from jax.experimental.pallas import tpu as pltpu

D_MODEL = 1024
DA_HEADS = 4
DA_HALF = 64
DA_VDIM = 2 * DA_HALF
DA_WIDTH = DA_HEADS * DA_VDIM
ROPE_THETA = 500000.0
ROPE_DIM = DA_HALF // 4
RET_HEADS = 4
RET_KDIM = 128
RET_VDIM = 128
RET_WIDTH = RET_HEADS * RET_VDIM
RET_THETA = 10000.0
N_GROUPS = 4
EXPERTS_PER_GROUP = 8
N_EXPERTS = N_GROUPS * EXPERTS_PER_GROUP
TOP_K = 2
EXPERT_FF = 512
EPS = 1e-6
LAMBDA_INIT = 0.8 - 0.6 * math.exp(-0.3 * 0)

LANES = 128
IN_COLS = 3 * DA_WIDTH + 4 * RET_WIDTH + 2 * D_MODEL
COL_QA, COL_KA, COL_VA = 0, DA_WIDTH, 2 * DA_WIDTH
COL_QR = 3 * DA_WIDTH
COL_KR = COL_QR + RET_WIDTH
COL_VR = COL_KR + RET_WIDTH
COL_GB = COL_VR + RET_WIDTH
COL_GATE_A = COL_GB + RET_WIDTH
COL_GATE_B = COL_GATE_A + D_MODEL

PROJ_ROWS = 512
PROJ_CHUNK = 256
ATT_TILE = 512
ATT_ROWS = 32
RET_CHUNK = 256
RET_UNROLL = 8
MOE_BLOCK = 256
COMBINE_ROWS = 256
GATHER_UNROLL = 8
DISPATCH_ROWS = 512
VMEM_LIMIT = 56 * 1024 * 1024


def _dot(a, b):
    return jnp.dot(a, b, preferred_element_type=jnp.float32)


def _dot_nt(a, b):
    return lax.dot_general(a, b, (((1,), (1,)), ((), ())), preferred_element_type=jnp.float32)


def _dot_tn(a, b):
    return lax.dot_general(a, b, (((0,), (0,)), ((), ())), preferred_element_type=jnp.float32)


def _sigmoid(x):
    return 0.5 * jnp.tanh(0.5 * x) + 0.5


def _split3(x):
    a = x.astype(jnp.bfloat16)
    r = x - a.astype(jnp.float32)
    b = r.astype(jnp.bfloat16)
    c = (r - b.astype(jnp.float32)).astype(jnp.bfloat16)
    return a, b, c


def _in_proj_kernel(x_ref, pos_ref, g1_ref, w_ref, gsum_ref, gq_ref, gk_ref, fa_ref, fr_ref, sel_ref,
                    o_ref, h_scr):
    x = x_ref[...]
    h = x * lax.rsqrt(jnp.mean(x * x, axis=-1, keepdims=True) + EPS) * g1_ref[...]
    h_scr[...] = h.astype(jnp.bfloat16)
    rows = x.shape[0]
    pos = pos_ref[...].astype(jnp.float32)

    lane = lax.broadcasted_iota(jnp.int32, (rows, LANES), 1)
    half_a = ROPE_DIM // 2
    ang_a = fa_ref[...] * pos
    pad = jnp.zeros((LANES - 2 * half_a, rows), jnp.float32)
    t_a = jnp.concatenate([jnp.cos(ang_a), jnp.sin(ang_a), pad], axis=0).T
    tab = sum(_dot(part, sel_ref[...]) for part in _split3(t_a))
    lane64 = lane % DA_HALF
    c_a = tab[:, :LANES] + jnp.where(lane64 < ROPE_DIM, 0.0, 1.0)
    s_lo = tab[:, LANES:2 * LANES]
    s_hi = tab[:, 2 * LANES:]
    c_a2 = jnp.concatenate([c_a, c_a], axis=1)
    s_lo2 = jnp.concatenate([s_lo, s_lo], axis=1)
    s_hi2 = jnp.concatenate([s_hi, s_hi], axis=1)
    ang_r = fr_ref[...] * pos
    t_r = jnp.concatenate([jnp.cos(ang_r), jnp.sin(ang_r)], axis=0).T
    sw_r = pltpu.roll(t_r, RET_KDIM // 2, axis=1)
    first = lane < RET_KDIM // 2
    c_r = jnp.where(first, t_r, sw_r)
    s_r = jnp.where(first, -sw_r, t_r)
    c_r2 = jnp.concatenate([c_r, c_r], axis=1)
    s_r2 = jnp.concatenate([s_r, s_r], axis=1)

    def qk_norm_rope(y, g, scale):
        ss = y * y
        hi = ss.astype(jnp.bfloat16)
        lo = (ss - hi.astype(jnp.float32)).astype(jnp.bfloat16)
        gs = _dot(hi, gsum_ref[...]) + _dot(lo, gsum_ref[...])
        n = y * lax.rsqrt(gs * (1.0 / DA_HALF) + EPS) * g
        up = pltpu.roll(n, PROJ_CHUNK - half_a, axis=1)
        dn = pltpu.roll(n, half_a, axis=1)
        r = n * c_a2 + up * s_lo2 + dn * s_hi2
        return r * scale if scale != 1.0 else r

    def ret_rope(y, scale):
        halves = [pltpu.roll(y[:, i * LANES:(i + 1) * LANES], RET_KDIM // 2, axis=1)
                  for i in range(PROJ_CHUNK // LANES)]
        sw = jnp.concatenate(halves, axis=1)
        r = y * c_r2 + sw * s_r2
        return r * scale if scale != 1.0 else r

    for c in range(IN_COLS // PROJ_CHUNK):
        c0 = c * PROJ_CHUNK
        y = _dot(h_scr[...], w_ref[:, c0:c0 + PROJ_CHUNK])
        if c0 < COL_KA:
            y = qk_norm_rope(y, gq_ref[...], DA_HALF ** -0.5)
        elif c0 < COL_VA:
            y = qk_norm_rope(y, gk_ref[...], 1.0)
        elif c0 < COL_QR:
            pass
        elif c0 < COL_KR:
            y = ret_rope(y, 1.0)
        elif c0 < COL_VR:
            y = ret_rope(y, RET_KDIM ** -0.5)
        elif c0 < COL_GB:
            pass
        elif c0 < COL_GATE_A:
            y = y * _sigmoid(y)
        else:
            y = _sigmoid(y)
        o_ref[:, c0:c0 + PROJ_CHUNK] = y.astype(o_ref.dtype)


def _in_proj(x2, pos2, g1, w_in, gq, gk):
    n = x2.shape[0]
    tm = min(PROJ_ROWS, n)
    grp = jnp.arange(PROJ_CHUNK) // DA_HALF
    gsum = (grp[:, None] == grp[None, :]).astype(jnp.bfloat16)
    half_a = ROPE_DIM // 2
    fa = jnp.power(jnp.float32(ROPE_THETA), -2.0 * jnp.arange(half_a, dtype=jnp.float32) / ROPE_DIM)[:, None]
    half_r = RET_KDIM // 2
    fr = jnp.power(jnp.float32(RET_THETA), -2.0 * jnp.arange(half_r, dtype=jnp.float32) / RET_KDIM)[:, None]
    j = jnp.arange(LANES)[:, None]
    l64 = (jnp.arange(LANES) % DA_HALF)[None, :]
    sel_c = (j < half_a) & (l64 < ROPE_DIM) & (l64 % half_a == j)
    sel_lo = (j >= half_a) & (j < ROPE_DIM) & (l64 < half_a) & (l64 == j - half_a)
    sel_hi = (j >= half_a) & (j < ROPE_DIM) & (l64 >= half_a) & (l64 < ROPE_DIM) & (l64 == j)
    sel = jnp.concatenate([sel_c.astype(jnp.float32), -sel_lo.astype(jnp.float32),
                           sel_hi.astype(jnp.float32)], axis=1).astype(jnp.bfloat16)
    reps = PROJ_CHUNK // DA_HALF
    full = lambda shape: pl.BlockSpec(shape, lambda i: (0,) * len(shape))
    return pl.pallas_call(
        _in_proj_kernel,
        grid=(n // tm,),
        in_specs=[
            pl.BlockSpec((tm, D_MODEL), lambda i: (i, 0)),
            pl.BlockSpec((1, tm), lambda i: (0, i)),
            full((1, D_MODEL)),
            full((D_MODEL, IN_COLS)),
            full((PROJ_CHUNK, PROJ_CHUNK)),
            full((1, PROJ_CHUNK)),
            full((1, PROJ_CHUNK)),
            full((half_a, 1)),
            full((half_r, 1)),
            full((LANES, 3 * LANES)),
        ],
        out_specs=pl.BlockSpec((tm, IN_COLS), lambda i: (i, 0)),
        out_shape=jax.ShapeDtypeStruct((n, IN_COLS), jnp.bfloat16),
        scratch_shapes=[pltpu.VMEM((tm, D_MODEL), jnp.bfloat16)],
        compiler_params=pltpu.CompilerParams(dimension_semantics=("arbitrary",),
                                             vmem_limit_bytes=VMEM_LIMIT),
        name="in_proj",
    )(x2, pos2, g1.reshape(1, D_MODEL), w_in.astype(jnp.bfloat16), gsum,
      jnp.tile(gq, reps)[None, :], jnp.tile(gk, reps)[None, :], fa, fr, sel)


def _diff_attn_kernel(q_ref, k_ref, v_ref, lam_ref, gsub_ref, o_ref,
                      qs_scr, vx_scr, s0_scr, s1_scr, p_scr, m_scr, alpha_scr, acc_scr):
    i = pl.program_id(2)
    t = q_ref.shape[0]

    @pl.when(i == 0)
    def _():
        vx_scr[:, :DA_VDIM] = v_ref[...]
        vx_scr[:, DA_VDIM:] = jnp.ones((vx_scr.shape[0], LANES), vx_scr.dtype)

    q = q_ref[...]
    lane = lax.broadcasted_iota(jnp.int32, q.shape, 1)
    zero = jnp.zeros_like(q)
    qs_scr[:t] = jnp.where(lane < DA_HALF, q, zero)
    qs_scr[t:] = jnp.where(lane >= DA_HALF, q, zero)
    m_scr[...] = jnp.full(m_scr.shape, -jnp.inf, jnp.float32)
    acc_scr[...] = jnp.zeros(acc_scr.shape, jnp.float32)

    def scores(j, s_ref):
        start = pl.multiple_of(j * t, t)
        s_ref[...] = _dot_nt(qs_scr[...], k_ref[pl.ds(start, t), :])

    def softmax_pv(j, s_ref, masked):
        for c in range(2 * t // ATT_ROWS):
            rows = pl.ds(c * ATT_ROWS, ATT_ROWS)
            s = s_ref[rows, :]
            if masked:
                r = lax.broadcasted_iota(jnp.int32, s.shape, 0) + (c * ATT_ROWS) % t
                col = lax.broadcasted_iota(jnp.int32, s.shape, 1)
                s = jnp.where(col <= r, s, -jnp.inf)
            m_prev = m_scr[rows, :]
            m_new = jnp.maximum(m_prev, jnp.max(s, axis=-1, keepdims=True))
            alpha_scr[rows, :] = jnp.exp(m_prev - m_new)
            m_scr[rows, :] = m_new
            p = jnp.exp(s - jnp.concatenate([m_new] * (t // LANES), axis=1))
            p_scr[rows, :] = p.astype(p_scr.dtype)
        start = pl.multiple_of(j * t, t)
        pv = _dot(p_scr[...], vx_scr[pl.ds(start, t), :])
        alpha = alpha_scr[...]
        for half in range(2):
            cols = pl.ds(half * LANES, LANES)
            acc_scr[:, cols] = alpha * acc_scr[:, cols] + pv[:, half * LANES:(half + 1) * LANES]

    scores(0, s0_scr)

    def pair(jj, carry):
        j = 2 * jj
        scores(j + 1, s1_scr)
        softmax_pv(j, s0_scr, False)
        scores(j + 2, s0_scr)
        softmax_pv(j + 1, s1_scr, False)
        return carry

    lax.fori_loop(0, i // 2, pair, 0)

    @pl.when(i % 2 == 1)
    def _():
        scores(i, s1_scr)
        softmax_pv(i - 1, s0_scr, False)
        softmax_pv(i, s1_scr, True)

    @pl.when(i % 2 == 0)
    def _():
        softmax_pv(i, s0_scr, True)

    lam4 = lam_ref[...]
    lam = (jnp.exp(jnp.sum(lam4[0:1] * lam4[1:2], axis=-1, keepdims=True))
           - jnp.exp(jnp.sum(lam4[2:3] * lam4[3:4], axis=-1, keepdims=True)) + LAMBDA_INIT)
    o_all = acc_scr[:, :DA_VDIM] / acc_scr[:, DA_VDIM:]
    o = o_all[:t] - lam * o_all[t:]
    o = o * lax.rsqrt(jnp.mean(o * o, axis=-1, keepdims=True) + EPS) * gsub_ref[...] * (1.0 - LAMBDA_INIT)
    o_ref[...] = o.astype(o_ref.dtype)


def _diff_attn(proj, lam4, gsub, batch, seq):
    n = proj.shape[0]
    t = min(ATT_TILE, seq)
    nq = seq // t
    qb, kb, vb = COL_QA // LANES, COL_KA // LANES, COL_VA // LANES
    return pl.pallas_call(
        _diff_attn_kernel,
        grid=(batch, DA_HEADS, nq),
        in_specs=[
            pl.BlockSpec((t, LANES), lambda b, h, i: (b * nq + i, qb + h)),
            pl.BlockSpec((seq, LANES), lambda b, h, i: (b, kb + h)),
            pl.BlockSpec((seq, LANES), lambda b, h, i: (b, vb + h)),
            pl.BlockSpec((4, LANES), lambda b, h, i: (0, 0)),
            pl.BlockSpec((1, LANES), lambda b, h, i: (0, 0)),
        ],
        out_specs=pl.BlockSpec((t, LANES), lambda b, h, i: (b * nq + i, h)),
        out_shape=jax.ShapeDtypeStruct((n, DA_WIDTH), jnp.bfloat16),
        scratch_shapes=[pltpu.VMEM((2 * t, LANES), jnp.bfloat16),
                        pltpu.VMEM((seq, DA_VDIM + LANES), jnp.bfloat16),
                        pltpu.VMEM((2 * t, t), jnp.float32),
                        pltpu.VMEM((2 * t, t), jnp.float32),
                        pltpu.VMEM((2 * t, t), jnp.bfloat16),
                        pltpu.VMEM((2 * t, LANES), jnp.float32),
                        pltpu.VMEM((2 * t, LANES), jnp.float32),
                        pltpu.VMEM((2 * t, DA_VDIM + LANES), jnp.float32)],
        compiler_params=pltpu.CompilerParams(dimension_semantics=("arbitrary",) * 3,
                                             vmem_limit_bytes=VMEM_LIMIT),
        name="diff_attn",
    )(proj, proj, proj, lam4, gsub)


def _retention_kernel(q_ref, k_ref, v_ref, g_ref, gng_ref, gnb_ref, o_ref, r_scr, *, chunk):
    hf = jnp.full((1, 1), pl.program_id(1), jnp.int32).astype(jnp.float32)
    log_g = jnp.log1p(-jnp.exp2(-5.0 - hf))
    ri = lax.broadcasted_iota(jnp.int32, (chunk, chunk), 0)
    ci = lax.broadcasted_iota(jnp.int32, (chunk, chunk), 1)
    rel = (ri - ci).astype(jnp.float32)
    dmask = jnp.where(rel >= 0, jnp.exp(jnp.maximum(rel, 0.0) * log_g), 0.0)
    idx = lax.broadcasted_iota(jnp.int32, (chunk, 1), 0).astype(jnp.float32)
    zeta = jnp.exp((chunk - 1 - idx) * log_g)
    xi = jnp.exp((idx + 1.0) * log_g)
    g_chunk = jnp.exp(chunk * log_g)
    r_scr[...] = jnp.zeros(r_scr.shape, jnp.float32)
    gng = gng_ref[...]
    gnb = gnb_ref[...]

    def body(n, carry):
        start = pl.multiple_of(n * chunk, chunk)
        q = q_ref[pl.ds(start, chunk), :]
        k = k_ref[pl.ds(start, chunk), :]
        v = v_ref[pl.ds(start, chunk), :]
        s = _dot_nt(q, k) * dmask
        r_old = r_scr[...]
        o = _dot(s.astype(jnp.bfloat16), v) + xi * _dot(q, r_old.astype(jnp.bfloat16))
        kz = (k.astype(jnp.float32) * zeta).astype(jnp.bfloat16)
        r_scr[...] = g_chunk * r_old + _dot_tn(kz, v)
        mu = jnp.mean(o, axis=-1, keepdims=True)
        d = o - mu
        var = jnp.mean(d * d, axis=-1, keepdims=True)
        y = d * lax.rsqrt(var + EPS) * gng + gnb
        y = y * g_ref[pl.ds(start, chunk), :].astype(jnp.float32)
        o_ref[pl.ds(start, chunk), :] = y.astype(o_ref.dtype)
        return carry

    lax.fori_loop(0, q_ref.shape[0] // chunk, body, 0, unroll=RET_UNROLL)


def _retention(proj, gn_g, gn_b, batch, seq):
    n = proj.shape[0]
    chunk = min(RET_CHUNK, seq)
    col = lambda c0: (lambda b, h: (b, c0 // LANES + h))
    return pl.pallas_call(
        functools.partial(_retention_kernel, chunk=chunk),
        grid=(batch, RET_HEADS),
        in_specs=[
            pl.BlockSpec((seq, LANES), col(COL_QR)),
            pl.BlockSpec((seq, LANES), col(COL_KR)),
            pl.BlockSpec((seq, LANES), col(COL_VR)),
            pl.BlockSpec((seq, LANES), col(COL_GB)),
            pl.BlockSpec((1, LANES), lambda b, h: (0, h)),
            pl.BlockSpec((1, LANES), lambda b, h: (0, h)),
        ],
        out_specs=pl.BlockSpec((seq, LANES), lambda b, h: (b, h)),
        out_shape=jax.ShapeDtypeStruct((n, RET_WIDTH), jnp.bfloat16),
        scratch_shapes=[pltpu.VMEM((RET_KDIM, RET_VDIM), jnp.float32)],
        compiler_params=pltpu.CompilerParams(dimension_semantics=("arbitrary",) * 2,
                                             vmem_limit_bytes=VMEM_LIMIT),
        name="retention",
    )(proj, proj, proj, proj, gn_g.reshape(1, RET_WIDTH), gn_b.reshape(1, RET_WIDTH))


def _merge_kernel(x_ref, oa_ref, ob_ref, sa0_ref, sa1_ref, sb0_ref, sb1_ref, wa_ref, wb_ref, wo_ref,
                  g2_ref, wr_hi_ref, wr_lo_ref, br_ref, tri_ref, x1_ref, h2_ref, route_ref, counts_ref,
                  base_scr):
    ya = _dot(oa_ref[...], wa_ref[...])
    yb = _dot(ob_ref[...], wb_ref[...])
    sa = jnp.concatenate([sa0_ref[...], sa1_ref[...]], axis=1).astype(jnp.float32)
    sb = jnp.concatenate([sb0_ref[...], sb1_ref[...]], axis=1).astype(jnp.float32)
    merged = sa * ya + sb * yb
    x1 = x_ref[...] + _dot(merged.astype(jnp.bfloat16), wo_ref[...])
    x1_ref[...] = x1
    h2 = x1 * lax.rsqrt(jnp.mean(x1 * x1, axis=-1, keepdims=True) + EPS) * g2_ref[...]
    h2_ref[...] = h2

    hi = h2.astype(jnp.bfloat16)
    lo = (h2 - hi.astype(jnp.float32)).astype(jnp.bfloat16)
    logits = (_dot(hi, wr_hi_ref[...]) + _dot(lo, wr_hi_ref[...]) + _dot(hi, wr_lo_ref[...])
              + br_ref[...])
    lane = lax.broadcasted_iota(jnp.int32, logits.shape, 1)
    neg = -jnp.inf
    gl = jnp.where(lane < N_GROUPS, logits, neg)
    gmax = jnp.max(gl, axis=-1, keepdims=True)
    g_idx = jnp.min(jnp.where(gl == gmax, lane, LANES), axis=-1, keepdims=True)
    p_g = 1.0 / jnp.sum(jnp.exp(gl - gmax), axis=-1, keepdims=True)
    e_lo = N_GROUPS + EXPERTS_PER_GROUP * g_idx
    el = jnp.where((lane >= e_lo) & (lane < e_lo + EXPERTS_PER_GROUP), logits, neg)
    v1 = jnp.max(el, axis=-1, keepdims=True)
    i1 = jnp.min(jnp.where(el == v1, lane, LANES), axis=-1, keepdims=True)
    el2 = jnp.where(lane == i1, neg, el)
    v2 = jnp.max(el2, axis=-1, keepdims=True)
    i2 = jnp.min(jnp.where(el2 == v2, lane, LANES), axis=-1, keepdims=True)
    t = jnp.exp(v2 - v1)
    w1 = p_g / (1.0 + t)
    w2 = p_g * t / (1.0 + t)
    e1 = i1 - N_GROUPS
    e2 = i2 - N_GROUPS

    @pl.when(pl.program_id(0) == 0)
    def _():
        base_scr[...] = jnp.zeros(base_scr.shape, jnp.float32)

    oh1 = lane == e1
    oh2 = lane == e2
    picked = jnp.where(oh1 | oh2, 1.0, 0.0)
    before = _dot(tri_ref[...], picked.astype(jnp.bfloat16)) + base_scr[0:1, :]
    rank1 = jnp.sum(jnp.where(oh1, before, 0.0), axis=-1, keepdims=True)
    rank2 = jnp.sum(jnp.where(oh2, before, 0.0), axis=-1, keepdims=True)
    base_scr[...] = base_scr[...] + jnp.sum(picked, axis=0, keepdims=True)
    counts_ref[...] = base_scr[...]

    cols = [e1.astype(jnp.float32), e2.astype(jnp.float32), w1, w2, rank1, rank2]
    route = jnp.zeros(logits.shape, jnp.float32)
    for c, val in enumerate(cols):
        route = jnp.where(lane == c, val, route)
    route_ref[...] = route


def _merge(x2, oa, ob, proj, wa, wb, wo, g2, w_gr, b_gr, w_er, b_er):
    n = x2.shape[0]
    tm = min(PROJ_ROWS, n)
    half = D_MODEL // 2
    wr = jnp.zeros((D_MODEL, LANES), jnp.float32)
    wr = wr.at[:, :N_GROUPS].set(w_gr).at[:, N_GROUPS:N_GROUPS + N_EXPERTS].set(w_er)
    wr_hi = wr.astype(jnp.bfloat16)
    wr_lo = (wr - wr_hi.astype(jnp.float32)).astype(jnp.bfloat16)
    br = jnp.zeros((1, LANES), jnp.float32)
    br = br.at[0, :N_GROUPS].set(b_gr).at[0, N_GROUPS:N_GROUPS + N_EXPERTS].set(b_er)
    tri = (jnp.arange(tm)[:, None] > jnp.arange(tm)[None, :]).astype(jnp.bfloat16)
    full = lambda shape: pl.BlockSpec(shape, lambda i: (0,) * len(shape))
    gate = lambda c0: pl.BlockSpec((tm, half), lambda i: (i, c0 // half))
    return pl.pallas_call(
        _merge_kernel,
        grid=(n // tm,),
        in_specs=[
            pl.BlockSpec((tm, D_MODEL), lambda i: (i, 0)),
            pl.BlockSpec((tm, DA_WIDTH), lambda i: (i, 0)),
            pl.BlockSpec((tm, RET_WIDTH), lambda i: (i, 0)),
            gate(COL_GATE_A), gate(COL_GATE_A + half), gate(COL_GATE_B), gate(COL_GATE_B + half),
            full((DA_WIDTH, D_MODEL)), full((RET_WIDTH, D_MODEL)), full((D_MODEL, D_MODEL)),
            full((1, D_MODEL)), full((D_MODEL, LANES)), full((D_MODEL, LANES)), full((1, LANES)),
            full((tm, tm)),
        ],
        out_specs=[
            pl.BlockSpec((tm, D_MODEL), lambda i: (i, 0)),
            pl.BlockSpec((tm, D_MODEL), lambda i: (i, 0)),
            pl.BlockSpec((tm, LANES), lambda i: (i, 0)),
            pl.BlockSpec((8, LANES), lambda i: (0, 0)),
        ],
        out_shape=[
            jax.ShapeDtypeStruct((n, D_MODEL), jnp.float32),
            jax.ShapeDtypeStruct((n, D_MODEL), jnp.float32),
            jax.ShapeDtypeStruct((n, LANES), jnp.float32),
            jax.ShapeDtypeStruct((8, LANES), jnp.float32),
        ],
        scratch_shapes=[pltpu.VMEM((8, LANES), jnp.float32)],
        compiler_params=pltpu.CompilerParams(dimension_semantics=("arbitrary",),
                                             vmem_limit_bytes=VMEM_LIMIT),
        name="merge",
    )(x2, oa, ob, proj, proj, proj, proj, wa.astype(jnp.bfloat16), wb.astype(jnp.bfloat16),
      wo.astype(jnp.bfloat16), g2.reshape(1, D_MODEL), wr_hi, wr_lo, br, tri)


def _row_dma_loop(count, make_copy, start):
    def trip(g, carry):
        for u in range(GATHER_UNROLL):
            copy = make_copy(g * GATHER_UNROLL + u)
            if start:
                copy.start()
            else:
                copy.wait()
        return carry
    lax.fori_loop(0, count // GATHER_UNROLL, trip, 0)


def _dispatch_kernel(dest_ref, seg_ref, h2_ref, xs_hbm, zero_buf, sem, zsem, *, rows):
    i = pl.program_id(0)
    n = pl.num_programs(0) * rows

    def zero_copy(e):
        start = pl.multiple_of(seg_ref[e] - MOE_BLOCK, MOE_BLOCK)
        return pltpu.make_async_copy(zero_buf, xs_hbm.at[pl.ds(start, MOE_BLOCK)], zsem)

    @pl.when(i == 0)
    def _():
        zero_buf[...] = jnp.zeros(zero_buf.shape, zero_buf.dtype)
        for e in range(N_EXPERTS):
            prev = seg_ref[e - 1] if e else 0
            @pl.when(seg_ref[e] > prev)
            def _():
                zero_copy(e).start()
        for e in range(N_EXPERTS):
            prev = seg_ref[e - 1] if e else 0
            @pl.when(seg_ref[e] > prev)
            def _():
                zero_copy(e).wait()

        def tail_copy(b):
            start = pl.multiple_of(b * MOE_BLOCK, MOE_BLOCK)
            return pltpu.make_async_copy(zero_buf, xs_hbm.at[pl.ds(start, MOE_BLOCK)], zsem)

        first_tail = seg_ref[N_EXPERTS - 1] // MOE_BLOCK
        n_blocks = xs_hbm.shape[0] // MOE_BLOCK
        lax.fori_loop(first_tail, n_blocks, lambda b, c: (tail_copy(b).start(), c)[1], 0)
        lax.fori_loop(first_tail, n_blocks, lambda b, c: (tail_copy(b).wait(), c)[1], 0)

    def row_copy(k):
        def make(r):
            return pltpu.make_async_copy(h2_ref.at[pl.ds(r, 1)],
                                         xs_hbm.at[pl.ds(dest_ref[k * n + i * rows + r], 1)], sem)
        return make

    for k in range(TOP_K):
        _row_dma_loop(rows, row_copy(k), True)
    for k in range(TOP_K):
        _row_dma_loop(rows, row_copy(k), False)


def _dispatch(h2, dest_kmajor, seg_end, p):
    n = h2.shape[0]
    rows = min(DISPATCH_ROWS, n)
    return pl.pallas_call(
        functools.partial(_dispatch_kernel, rows=rows),
        grid_spec=pltpu.PrefetchScalarGridSpec(
            num_scalar_prefetch=2,
            grid=(n // rows,),
            in_specs=[pl.BlockSpec((rows, D_MODEL), lambda i, d, sg: (i, 0))],
            out_specs=pl.BlockSpec(memory_space=pl.ANY),
            scratch_shapes=[pltpu.VMEM((MOE_BLOCK, D_MODEL), jnp.float32),
                            pltpu.SemaphoreType.DMA(()), pltpu.SemaphoreType.DMA(())],
        ),
        out_shape=jax.ShapeDtypeStruct((p, D_MODEL), jnp.float32),
        compiler_params=pltpu.CompilerParams(dimension_semantics=("arbitrary",),
                                             vmem_limit_bytes=VMEM_LIMIT),
        name="dispatch",
    )(dest_kmajor, seg_end, h2)


def _expert_kernel(blk_e_ref, n_used_ref, x_ref, wg_ref, wu_ref, wd_ref, o_ref):
    i = pl.program_id(0)

    @pl.when(i < n_used_ref[0])
    def _():
        x = x_ref[...].astype(jnp.bfloat16)
        a = _dot(x, wg_ref[0])
        u = _dot(x, wu_ref[0])
        hmid = (a * _sigmoid(a) * u).astype(jnp.bfloat16)
        o_ref[...] = _dot(hmid, wd_ref[0])

    @pl.when(i >= n_used_ref[0])
    def _():
        o_ref[...] = jnp.zeros(o_ref.shape, o_ref.dtype)


def _experts(xs, blk_expert, n_used, w_gate, w_up, w_down):
    p = xs.shape[0]
    live = lambda i, be, nu: jnp.minimum(i, nu[0] - 1)
    return pl.pallas_call(
        _expert_kernel,
        grid_spec=pltpu.PrefetchScalarGridSpec(
            num_scalar_prefetch=2,
            grid=(p // MOE_BLOCK,),
            in_specs=[
                pl.BlockSpec((MOE_BLOCK, D_MODEL), lambda i, be, nu: (live(i, be, nu), 0)),
                pl.BlockSpec((1, D_MODEL, EXPERT_FF), lambda i, be, nu: (be[i], 0, 0)),
                pl.BlockSpec((1, D_MODEL, EXPERT_FF), lambda i, be, nu: (be[i], 0, 0)),
                pl.BlockSpec((1, EXPERT_FF, D_MODEL), lambda i, be, nu: (be[i], 0, 0)),
            ],
            out_specs=pl.BlockSpec((MOE_BLOCK, D_MODEL), lambda i, be, nu: (i, 0)),
        ),
        out_shape=jax.ShapeDtypeStruct((p, D_MODEL), jnp.float32),
        compiler_params=pltpu.CompilerParams(dimension_semantics=("arbitrary",),
                                             vmem_limit_bytes=VMEM_LIMIT),
        name="experts",
    )(blk_expert, n_used, xs, w_gate.astype(jnp.bfloat16), w_up.astype(jnp.bfloat16),
      w_down.astype(jnp.bfloat16))


def _combine_kernel(dest_ref, x1_ref, route_ref, ys_hbm, o_ref, y_buf, sem, *, rows):
    i = pl.program_id(0)
    slot = i % 2
    n = pl.num_programs(0) * rows

    def row_copy(step, s, k):
        def make(r):
            return pltpu.make_async_copy(ys_hbm.at[pl.ds(dest_ref[k * n + step * rows + r], 1)],
                                         y_buf.at[s, k, pl.ds(r, 1)], sem.at[s])
        return make

    def start(step, s):
        for k in range(TOP_K):
            _row_dma_loop(rows, row_copy(step, s, k), True)

    @pl.when(i == 0)
    def _():
        start(0, 0)

    @pl.when(i + 1 < pl.num_programs(0))
    def _():
        start(i + 1, 1 - slot)

    for k in range(TOP_K):
        _row_dma_loop(rows, row_copy(i, slot, k), False)
    route = route_ref[...]
    o_ref[...] = x1_ref[...] + route[:, 2:3] * y_buf[slot, 0] + route[:, 3:4] * y_buf[slot, 1]


def _combine(x1, ys, route, dest_kmajor):
    n = x1.shape[0]
    tm = min(COMBINE_ROWS, n)
    return pl.pallas_call(
        functools.partial(_combine_kernel, rows=tm),
        grid_spec=pltpu.PrefetchScalarGridSpec(
            num_scalar_prefetch=1,
            grid=(n // tm,),
            in_specs=[
                pl.BlockSpec((tm, D_MODEL), lambda i, d: (i, 0)),
                pl.BlockSpec((tm, LANES), lambda i, d: (i, 0)),
                pl.BlockSpec(memory_space=pl.ANY),
            ],
            out_specs=pl.BlockSpec((tm, D_MODEL), lambda i, d: (i, 0)),
            scratch_shapes=[pltpu.VMEM((2, TOP_K, tm, D_MODEL), jnp.float32),
                            pltpu.SemaphoreType.DMA((2,))],
        ),
        out_shape=jax.ShapeDtypeStruct((n, D_MODEL), jnp.float32),
        compiler_params=pltpu.CompilerParams(dimension_semantics=("arbitrary",),
                                             vmem_limit_bytes=VMEM_LIMIT),
        name="combine",
    )(dest_kmajor, x1, route, ys)


def _dispatch_plan(route, counts, n):
    counts = counts[0, :N_EXPERTS].astype(jnp.int32)
    padded = ((counts + MOE_BLOCK - 1) // MOE_BLOCK) * MOE_BLOCK
    seg_end = jnp.cumsum(padded).astype(jnp.int32)
    seg_start = seg_end - padded
    e = route[:, 0:TOP_K].astype(jnp.int32)
    rank = route[:, 4:4 + TOP_K].astype(jnp.int32)
    dest = (seg_start[e] + rank).T.reshape(-1)
    p = n * TOP_K + N_EXPERTS * MOE_BLOCK
    blk_start = jnp.arange(p // MOE_BLOCK, dtype=jnp.int32) * MOE_BLOCK
    blk_expert = jnp.sum((seg_end[None, :] <= blk_start[:, None]).astype(jnp.int32), axis=1)
    blk_expert = jnp.minimum(blk_expert, N_EXPERTS - 1)
    n_used = (seg_end[-1] // MOE_BLOCK).reshape(1)
    return dest, seg_end, blk_expert, n_used, p


def _layer(x, positions, norm1_g, w_in, q_norm_g, k_norm_g, lam4, diff_subln_g, ret_gn_g, ret_gn_b,
           w_branch_a, w_branch_b, w_out, norm2_g, w_gr, b_gr, w_er, b_er, w_gate, w_up, w_down):
    batch, seq, _ = x.shape
    n = batch * seq
    x2 = x.reshape(n, D_MODEL)
    proj = _in_proj(x2, positions.reshape(1, n), norm1_g, w_in, q_norm_g, k_norm_g)
    oa = _diff_attn(proj, lam4, diff_subln_g.reshape(1, DA_VDIM), batch, seq)
    ob = _retention(proj, ret_gn_g, ret_gn_b, batch, seq)
    x1, h2, route, counts = _merge(x2, oa, ob, proj, w_branch_a, w_branch_b, w_out, norm2_g,
                                   w_gr, b_gr, w_er, b_er)
    dest, seg_end, blk_expert, n_used, p = _dispatch_plan(route, counts, n)
    xs = _dispatch(h2, dest, seg_end, p)
    ys = _experts(xs, blk_expert, n_used, w_gate, w_up, w_down)
    out = _combine(x1, ys, route, dest)
    return out.reshape(batch, seq, D_MODEL)


def kernel(x, positions, norm1_g, w_in, q_norm_g, k_norm_g, lambda_q1, lambda_k1, lambda_q2, lambda_k2, diff_subln_g, ret_gn_g, ret_gn_b, w_branch_a, w_branch_b, w_out, norm2_g, w_group_router, b_group_router, w_expert_router, b_expert_router, w_gate, w_up, w_down):
    assert x.shape[-1] == D_MODEL and norm1_g.shape[0] == 1, "single-layer, D_MODEL-wide input expected"
    lam4 = jnp.zeros((4, LANES), jnp.float32)
    lam4 = lam4.at[:, :DA_HALF].set(jnp.stack([lambda_q1[0], lambda_k1[0], lambda_q2[0], lambda_k2[0]]))
    return _layer(x, positions, norm1_g[0], w_in[0], q_norm_g[0], k_norm_g[0], lam4, diff_subln_g[0],
                  ret_gn_g[0], ret_gn_b[0], w_branch_a[0], w_branch_b[0], w_out[0], norm2_g[0],
                  w_group_router[0], b_group_router[0], w_expert_router[0], b_expert_router[0],
                  w_gate[0], w_up[0], w_down[0])
```

```python
import functools
import math

import jax
import jax.numpy as jnp
from jax import lax
from jax.experimental import pallas as pl
from jax.experimental.pallas import tpu as pltpu
from jax.experimental.pallas import tpu_sc as plsc

D_MODEL = 1024
DA_HEADS = 4
DA_HALF = 64
DA_VDIM = 2 * DA_HALF
DA_WIDTH = DA_HEADS * DA_VDIM
ROPE_THETA = 500000.0
ROPE_DIM = DA_HALF // 4
RET_HEADS = 4
RET_KDIM = 128
RET_VDIM = 128
RET_WIDTH = RET_HEADS * RET_VDIM
RET_THETA = 10000.0
N_GROUPS = 4
EXPERTS_PER_GROUP = 8
N_EXPERTS = N_GROUPS * EXPERTS_PER_GROUP
TOP_K = 2
EXPERT_FF = 512
EPS = 1e-6
LAMBDA_INIT = 0.8 - 0.6 * math.exp(-0.3 * 0)

LANES = 128
IN_COLS = 3 * DA_WIDTH + 4 * RET_WIDTH + 2 * D_MODEL
COL_QA, COL_KA, COL_VA = 0, DA_WIDTH, 2 * DA_WIDTH
COL_QR = 3 * DA_WIDTH
COL_KR = COL_QR + RET_WIDTH
COL_VR = COL_KR + RET_WIDTH
COL_GB = COL_VR + RET_WIDTH
COL_GATE_A = COL_GB + RET_WIDTH
COL_GATE_B = COL_GATE_A + D_MODEL

PROJ_ROWS = 512
PROJ_CHUNK = 256
ATT_TILE = 512
ATT_ROWS = 32
RET_CHUNK = 256
RET_UNROLL = 8
MOE_BLOCK = 256
ROW_PIECES = 4
PIECE = D_MODEL // ROW_PIECES
SC_CORES = 2
SC_WINDOW = 128
VMEM_LIMIT = 56 * 1024 * 1024


def _dot(a, b):
    return jnp.dot(a, b, preferred_element_type=jnp.float32)


def _dot_nt(a, b):
    return lax.dot_general(a, b, (((1,), (1,)), ((), ())), preferred_element_type=jnp.float32)


def _dot_tn(a, b):
    return lax.dot_general(a, b, (((0,), (0,)), ((), ())), preferred_element_type=jnp.float32)


def _sigmoid(x):
    return 0.5 * jnp.tanh(0.5 * x) + 0.5


def _split3(x):
    a = x.astype(jnp.bfloat16)
    r = x - a.astype(jnp.float32)
    b = r.astype(jnp.bfloat16)
    c = (r - b.astype(jnp.float32)).astype(jnp.bfloat16)
    return a, b, c


def _in_proj_kernel(x_ref, pos_ref, g1_ref, w_ref, gsum_ref, gq_ref, gk_ref, fa_ref, fr_ref, sel_ref,
                    o_ref, h_scr):
    x = x_ref[...]
    h = x * lax.rsqrt(jnp.mean(x * x, axis=-1, keepdims=True) + EPS) * g1_ref[...]
    h_scr[...] = h.astype(jnp.bfloat16)
    rows = x.shape[0]
    pos = pos_ref[...].astype(jnp.float32)

    lane = lax.broadcasted_iota(jnp.int32, (rows, LANES), 1)
    half_a = ROPE_DIM // 2
    ang_a = fa_ref[...] * pos
    pad = jnp.zeros((LANES - 2 * half_a, rows), jnp.float32)
    t_a = jnp.concatenate([jnp.cos(ang_a), jnp.sin(ang_a), pad], axis=0).T
    tab = sum(_dot(part, sel_ref[...]) for part in _split3(t_a))
    lane64 = lane % DA_HALF
    c_a = tab[:, :LANES] + jnp.where(lane64 < ROPE_DIM, 0.0, 1.0)
    s_lo = tab[:, LANES:2 * LANES]
    s_hi = tab[:, 2 * LANES:]
    c_a2 = jnp.concatenate([c_a, c_a], axis=1)
    s_lo2 = jnp.concatenate([s_lo, s_lo], axis=1)
    s_hi2 = jnp.concatenate([s_hi, s_hi], axis=1)
    ang_r = fr_ref[...] * pos
    t_r = jnp.concatenate([jnp.cos(ang_r), jnp.sin(ang_r)], axis=0).T
    sw_r = pltpu.roll(t_r, RET_KDIM // 2, axis=1)
    first = lane < RET_KDIM // 2
    c_r = jnp.where(first, t_r, sw_r)
    s_r = jnp.where(first, -sw_r, t_r)
    c_r2 = jnp.concatenate([c_r, c_r], axis=1)
    s_r2 = jnp.concatenate([s_r, s_r], axis=1)

    def qk_norm_rope(y, g, scale):
        ss = y * y
        hi = ss.astype(jnp.bfloat16)
        lo = (ss - hi.astype(jnp.float32)).astype(jnp.bfloat16)
        gs = _dot(hi, gsum_ref[...]) + _dot(lo, gsum_ref[...])
        n = y * lax.rsqrt(gs * (1.0 / DA_HALF) + EPS) * g
        up = pltpu.roll(n, PROJ_CHUNK - half_a, axis=1)
        dn = pltpu.roll(n, half_a, axis=1)
        r = n * c_a2 + up * s_lo2 + dn * s_hi2
        return r * scale if scale != 1.0 else r

    def ret_rope(y, scale):
        halves = [pltpu.roll(y[:, i * LANES:(i + 1) * LANES], RET_KDIM // 2, axis=1)
                  for i in range(PROJ_CHUNK // LANES)]
        sw = jnp.concatenate(halves, axis=1)
        r = y * c_r2 + sw * s_r2
        return r * scale if scale != 1.0 else r

    for c in range(IN_COLS // PROJ_CHUNK):
        c0 = c * PROJ_CHUNK
        y = _dot(h_scr[...], w_ref[:, c0:c0 + PROJ_CHUNK])
        if c0 < COL_KA:
            y = qk_norm_rope(y, gq_ref[...], DA_HALF ** -0.5)
        elif c0 < COL_VA:
            y = qk_norm_rope(y, gk_ref[...], 1.0)
        elif c0 < COL_QR:
            pass
        elif c0 < COL_KR:
            y = ret_rope(y, 1.0)
        elif c0 < COL_VR:
            y = ret_rope(y, RET_KDIM ** -0.5)
        elif c0 < COL_GB:
            pass
        elif c0 < COL_GATE_A:
            y = y * _sigmoid(y)
        else:
            y = _sigmoid(y)
        o_ref[:, c0:c0 + PROJ_CHUNK] = y.astype(o_ref.dtype)


def _in_proj(x2, pos2, g1, w_in, gq, gk):
    n = x2.shape[0]
    tm = min(PROJ_ROWS, n)
    grp = jnp.arange(PROJ_CHUNK) // DA_HALF
    gsum = (grp[:, None] == grp[None, :]).astype(jnp.bfloat16)
    half_a = ROPE_DIM // 2
    fa = jnp.power(jnp.float32(ROPE_THETA), -2.0 * jnp.arange(half_a, dtype=jnp.float32) / ROPE_DIM)[:, None]
    half_r = RET_KDIM // 2
    fr = jnp.power(jnp.float32(RET_THETA), -2.0 * jnp.arange(half_r, dtype=jnp.float32) / RET_KDIM)[:, None]
    j = jnp.arange(LANES)[:, None]
    l64 = (jnp.arange(LANES) % DA_HALF)[None, :]
    sel_c = (j < half_a) & (l64 < ROPE_DIM) & (l64 % half_a == j)
    sel_lo = (j >= half_a) & (j < ROPE_DIM) & (l64 < half_a) & (l64 == j - half_a)
    sel_hi = (j >= half_a) & (j < ROPE_DIM) & (l64 >= half_a) & (l64 < ROPE_DIM) & (l64 == j)
    sel = jnp.concatenate([sel_c.astype(jnp.float32), -sel_lo.astype(jnp.float32),
                           sel_hi.astype(jnp.float32)], axis=1).astype(jnp.bfloat16)
    reps = PROJ_CHUNK // DA_HALF
    full = lambda shape: pl.BlockSpec(shape, lambda i: (0,) * len(shape))
    return pl.pallas_call(
        _in_proj_kernel,
        grid=(n // tm,),
        in_specs=[
            pl.BlockSpec((tm, D_MODEL), lambda i: (i, 0)),
            pl.BlockSpec((1, tm), lambda i: (0, i)),
            full((1, D_MODEL)),
            full((D_MODEL, IN_COLS)),
            full((PROJ_CHUNK, PROJ_CHUNK)),
            full((1, PROJ_CHUNK)),
            full((1, PROJ_CHUNK)),
            full((half_a, 1)),
            full((half_r, 1)),
            full((LANES, 3 * LANES)),
        ],
        out_specs=pl.BlockSpec((tm, IN_COLS), lambda i: (i, 0)),
        out_shape=jax.ShapeDtypeStruct((n, IN_COLS), jnp.bfloat16),
        scratch_shapes=[pltpu.VMEM((tm, D_MODEL), jnp.bfloat16)],
        compiler_params=pltpu.CompilerParams(dimension_semantics=("arbitrary",),
                                             vmem_limit_bytes=VMEM_LIMIT),
        name="in_proj",
    )(x2, pos2, g1.reshape(1, D_MODEL), w_in.astype(jnp.bfloat16), gsum,
      jnp.tile(gq, reps)[None, :], jnp.tile(gk, reps)[None, :], fa, fr, sel)


def _diff_attn_kernel(q_ref, k_ref, v_ref, lam_ref, gsub_ref, o_ref,
                      qs_scr, vx_scr, s0_scr, s1_scr, p_scr, m_scr, alpha_scr, acc_scr):
    i = pl.program_id(2)
    t = q_ref.shape[0]

    @pl.when(i == 0)
    def _():
        vx_scr[:, :DA_VDIM] = v_ref[...]
        vx_scr[:, DA_VDIM:] = jnp.ones((vx_scr.shape[0], LANES), vx_scr.dtype)

    q = q_ref[...]
    lane = lax.broadcasted_iota(jnp.int32, q.shape, 1)
    zero = jnp.zeros_like(q)
    qs_scr[:t] = jnp.where(lane < DA_HALF, q, zero)
    qs_scr[t:] = jnp.where(lane >= DA_HALF, q, zero)
    m_scr[...] = jnp.full(m_scr.shape, -jnp.inf, jnp.float32)
    acc_scr[...] = jnp.zeros(acc_scr.shape, jnp.float32)

    def scores(j, s_ref):
        start = pl.multiple_of(j * t, t)
        s_ref[...] = _dot_nt(qs_scr[...], k_ref[pl.ds(start, t), :])

    def softmax_pv(j, s_ref, masked):
        for c in range(2 * t // ATT_ROWS):
            rows = pl.ds(c * ATT_ROWS, ATT_ROWS)
            s = s_ref[rows, :]
            if masked:
                r = lax.broadcasted_iota(jnp.int32, s.shape, 0) + (c * ATT_ROWS) % t
                col = lax.broadcasted_iota(jnp.int32, s.shape, 1)
                s = jnp.where(col <= r, s, -jnp.inf)
            m_prev = m_scr[rows, :]
            m_new = jnp.maximum(m_prev, jnp.max(s, axis=-1, keepdims=True))
            alpha_scr[rows, :] = jnp.exp(m_prev - m_new)
            m_scr[rows, :] = m_new
            p = jnp.exp(s - jnp.concatenate([m_new] * (t // LANES), axis=1))
            p_scr[rows, :] = p.astype(p_scr.dtype)
        start = pl.multiple_of(j * t, t)
        pv = _dot(p_scr[...], vx_scr[pl.ds(start, t), :])
        alpha = alpha_scr[...]
        for half in range(2):
            cols = pl.ds(half * LANES, LANES)
            acc_scr[:, cols] = alpha * acc_scr[:, cols] + pv[:, half * LANES:(half + 1) * LANES]

    scores(0, s0_scr)

    def pair(jj, carry):
        j = 2 * jj
        scores(j + 1, s1_scr)
        softmax_pv(j, s0_scr, False)
        scores(j + 2, s0_scr)
        softmax_pv(j + 1, s1_scr, False)
        return carry

    lax.fori_loop(0, i // 2, pair, 0)

    @pl.when(i % 2 == 1)
    def _():
        scores(i, s1_scr)
        softmax_pv(i - 1, s0_scr, False)
        softmax_pv(i, s1_scr, True)

    @pl.when(i % 2 == 0)
    def _():
        softmax_pv(i, s0_scr, True)

    lam4 = lam_ref[...]
    lam = (jnp.exp(jnp.sum(lam4[0:1] * lam4[1:2], axis=-1, keepdims=True))
           - jnp.exp(jnp.sum(lam4[2:3] * lam4[3:4], axis=-1, keepdims=True)) + LAMBDA_INIT)
    o_all = acc_scr[:, :DA_VDIM] / acc_scr[:, DA_VDIM:]
    o = o_all[:t] - lam * o_all[t:]
    o = o * lax.rsqrt(jnp.mean(o * o, axis=-1, keepdims=True) + EPS) * gsub_ref[...] * (1.0 - LAMBDA_INIT)
    o_ref[...] = o.astype(o_ref.dtype)


def _diff_attn(proj, lam4, gsub, batch, seq):
    n = proj.shape[0]
    t = min(ATT_TILE, seq)
    nq = seq // t
    qb, kb, vb = COL_QA // LANES, COL_KA // LANES, COL_VA // LANES
    return pl.pallas_call(
        _diff_attn_kernel,
        grid=(batch, DA_HEADS, nq),
        in_specs=[
            pl.BlockSpec((t, LANES), lambda b, h, i: (b * nq + i, qb + h)),
            pl.BlockSpec((seq, LANES), lambda b, h, i: (b, kb + h)),
            pl.BlockSpec((seq, LANES), lambda b, h, i: (b, vb + h)),
            pl.BlockSpec((4, LANES), lambda b, h, i: (0, 0)),
            pl.BlockSpec((1, LANES), lambda b, h, i: (0, 0)),
        ],
        out_specs=pl.BlockSpec((t, LANES), lambda b, h, i: (b * nq + i, h)),
        out_shape=jax.ShapeDtypeStruct((n, DA_WIDTH), jnp.bfloat16),
        scratch_shapes=[pltpu.VMEM((2 * t, LANES), jnp.bfloat16),
                        pltpu.VMEM((seq, DA_VDIM + LANES), jnp.bfloat16),
                        pltpu.VMEM((2 * t, t), jnp.float32),
                        pltpu.VMEM((2 * t, t), jnp.float32),
                        pltpu.VMEM((2 * t, t), jnp.bfloat16),
                        pltpu.VMEM((2 * t, LANES), jnp.float32),
                        pltpu.VMEM((2 * t, LANES), jnp.float32),
                        pltpu.VMEM((2 * t, DA_VDIM + LANES), jnp.float32)],
        compiler_params=pltpu.CompilerParams(dimension_semantics=("arbitrary",) * 3,
                                             vmem_limit_bytes=VMEM_LIMIT),
        name="diff_attn",
    )(proj, proj, proj, lam4, gsub)


def _retention_kernel(q_ref, k_ref, v_ref, g_ref, gng_ref, gnb_ref, o_ref, r_scr, *, chunk):
    hf = jnp.full((1, 1), pl.program_id(1), jnp.int32).astype(jnp.float32)
    log_g = jnp.log1p(-jnp.exp2(-5.0 - hf))
    ri = lax.broadcasted_iota(jnp.int32, (chunk, chunk), 0)
    ci = lax.broadcasted_iota(jnp.int32, (chunk, chunk), 1)
    rel = (ri - ci).astype(jnp.float32)
    dmask = jnp.where(rel >= 0, jnp.exp(jnp.maximum(rel, 0.0) * log_g), 0.0)
    idx = lax.broadcasted_iota(jnp.int32, (chunk, 1), 0).astype(jnp.float32)
    zeta = jnp.exp((chunk - 1 - idx) * log_g)
    xi = jnp.exp((idx + 1.0) * log_g)
    g_chunk = jnp.exp(chunk * log_g)
    r_scr[...] = jnp.zeros(r_scr.shape, jnp.float32)
    gng = gng_ref[...]
    gnb = gnb_ref[...]

    def body(n, carry):
        start = pl.multiple_of(n * chunk, chunk)
        q = q_ref[pl.ds(start, chunk), :]
        k = k_ref[pl.ds(start, chunk), :]
        v = v_ref[pl.ds(start, chunk), :]
        s = _dot_nt(q, k) * dmask
        r_old = r_scr[...]
        o = _dot(s.astype(jnp.bfloat16), v) + xi * _dot(q, r_old.astype(jnp.bfloat16))
        kz = (k.astype(jnp.float32) * zeta).astype(jnp.bfloat16)
        r_scr[...] = g_chunk * r_old + _dot_tn(kz, v)
        mu = jnp.mean(o, axis=-1, keepdims=True)
        d = o - mu
        var = jnp.mean(d * d, axis=-1, keepdims=True)
        y = d * lax.rsqrt(var + EPS) * gng + gnb
        y = y * g_ref[pl.ds(start, chunk), :].astype(jnp.float32)
        o_ref[pl.ds(start, chunk), :] = y.astype(o_ref.dtype)
        return carry

    lax.fori_loop(0, q_ref.shape[0] // chunk, body, 0, unroll=RET_UNROLL)


def _retention(proj, gn_g, gn_b, batch, seq):
    n = proj.shape[0]
    chunk = min(RET_CHUNK, seq)
    col = lambda c0: (lambda b, h: (b, c0 // LANES + h))
    return pl.pallas_call(
        functools.partial(_retention_kernel, chunk=chunk),
        grid=(batch, RET_HEADS),
        in_specs=[
            pl.BlockSpec((seq, LANES), col(COL_QR)),
            pl.BlockSpec((seq, LANES), col(COL_KR)),
            pl.BlockSpec((seq, LANES), col(COL_VR)),
            pl.BlockSpec((seq, LANES), col(COL_GB)),
            pl.BlockSpec((1, LANES), lambda b, h: (0, h)),
            pl.BlockSpec((1, LANES), lambda b, h: (0, h)),
        ],
        out_specs=pl.BlockSpec((seq, LANES), lambda b, h: (b, h)),
        out_shape=jax.ShapeDtypeStruct((n, RET_WIDTH), jnp.bfloat16),
        scratch_shapes=[pltpu.VMEM((RET_KDIM, RET_VDIM), jnp.float32)],
        compiler_params=pltpu.CompilerParams(dimension_semantics=("arbitrary",) * 2,
                                             vmem_limit_bytes=VMEM_LIMIT),
        name="retention",
    )(proj, proj, proj, proj, gn_g.reshape(1, RET_WIDTH), gn_b.reshape(1, RET_WIDTH))


def _merge_kernel(x_ref, oa_ref, ob_ref, sa0_ref, sa1_ref, sb0_ref, sb1_ref, wa_ref, wb_ref, wo_ref,
                  g2_ref, wr_hi_ref, wr_lo_ref, br_ref, tri_ref, x1_ref, h2_ref, route_ref, counts_ref,
                  base_scr):
    ya = _dot(oa_ref[...], wa_ref[...])
    yb = _dot(ob_ref[...], wb_ref[...])
    sa = jnp.concatenate([sa0_ref[...], sa1_ref[...]], axis=1).astype(jnp.float32)
    sb = jnp.concatenate([sb0_ref[...], sb1_ref[...]], axis=1).astype(jnp.float32)
    merged = sa * ya + sb * yb
    x1 = x_ref[...] + _dot(merged.astype(jnp.bfloat16), wo_ref[...])
    x1_ref[...] = x1
    h2 = x1 * lax.rsqrt(jnp.mean(x1 * x1, axis=-1, keepdims=True) + EPS) * g2_ref[...]
    for j in range(ROW_PIECES):
        h2_ref[j] = h2[:, j * PIECE:(j + 1) * PIECE]

    hi = h2.astype(jnp.bfloat16)
    lo = (h2 - hi.astype(jnp.float32)).astype(jnp.bfloat16)
    logits = (_dot(hi, wr_hi_ref[...]) + _dot(lo, wr_hi_ref[...]) + _dot(hi, wr_lo_ref[...])
              + br_ref[...])
    lane = lax.broadcasted_iota(jnp.int32, logits.shape, 1)
    neg = -jnp.inf
    gl = jnp.where(lane < N_GROUPS, logits, neg)
    gmax = jnp.max(gl, axis=-1, keepdims=True)
    g_idx = jnp.min(jnp.where(gl == gmax, lane, LANES), axis=-1, keepdims=True)
    p_g = 1.0 / jnp.sum(jnp.exp(gl - gmax), axis=-1, keepdims=True)
    e_lo = N_GROUPS + EXPERTS_PER_GROUP * g_idx
    el = jnp.where((lane >= e_lo) & (lane < e_lo + EXPERTS_PER_GROUP), logits, neg)
    v1 = jnp.max(el, axis=-1, keepdims=True)
    i1 = jnp.min(jnp.where(el == v1, lane, LANES), axis=-1, keepdims=True)
    el2 = jnp.where(lane == i1, neg, el)
    v2 = jnp.max(el2, axis=-1, keepdims=True)
    i2 = jnp.min(jnp.where(el2 == v2, lane, LANES), axis=-1, keepdims=True)
    t = jnp.exp(v2 - v1)
    w1 = p_g / (1.0 + t)
    w2 = p_g * t / (1.0 + t)
    e1 = i1 - N_GROUPS
    e2 = i2 - N_GROUPS

    @pl.when(pl.program_id(0) == 0)
    def _():
        base_scr[...] = jnp.zeros(base_scr.shape, jnp.float32)

    oh1 = lane == e1
    oh2 = lane == e2
    picked = jnp.where(oh1 | oh2, 1.0, 0.0)
    before = _dot(tri_ref[...], picked.astype(jnp.bfloat16)) + base_scr[0:1, :]
    rank1 = jnp.sum(jnp.where(oh1, before, 0.0), axis=-1, keepdims=True)
    rank2 = jnp.sum(jnp.where(oh2, before, 0.0), axis=-1, keepdims=True)
    base_scr[...] = base_scr[...] + jnp.sum(picked, axis=0, keepdims=True)
    counts_ref[...] = base_scr[...]

    cols = [e1.astype(jnp.float32), e2.astype(jnp.float32), w1, w2, rank1, rank2]
    route = jnp.zeros(logits.shape, jnp.float32)
    for c, val in enumerate(cols):
        route = jnp.where(lane == c, val, route)
    route_ref[...] = route


def _merge(x2, oa, ob, proj, wa, wb, wo, g2, w_gr, b_gr, w_er, b_er):
    n = x2.shape[0]
    tm = min(PROJ_ROWS, n)
    half = D_MODEL // 2
    wr = jnp.zeros((D_MODEL, LANES), jnp.float32)
    wr = wr.at[:, :N_GROUPS].set(w_gr).at[:, N_GROUPS:N_GROUPS + N_EXPERTS].set(w_er)
    wr_hi = wr.astype(jnp.bfloat16)
    wr_lo = (wr - wr_hi.astype(jnp.float32)).astype(jnp.bfloat16)
    br = jnp.zeros((1, LANES), jnp.float32)
    br = br.at[0, :N_GROUPS].set(b_gr).at[0, N_GROUPS:N_GROUPS + N_EXPERTS].set(b_er)
    tri = (jnp.arange(tm)[:, None] > jnp.arange(tm)[None, :]).astype(jnp.bfloat16)
    full = lambda shape: pl.BlockSpec(shape, lambda i: (0,) * len(shape))
    gate = lambda c0: pl.BlockSpec((tm, half), lambda i: (i, c0 // half))
    return pl.pallas_call(
        _merge_kernel,
        grid=(n // tm,),
        in_specs=[
            pl.BlockSpec((tm, D_MODEL), lambda i: (i, 0)),
            pl.BlockSpec((tm, DA_WIDTH), lambda i: (i, 0)),
            pl.BlockSpec((tm, RET_WIDTH), lambda i: (i, 0)),
            gate(COL_GATE_A), gate(COL_GATE_A + half), gate(COL_GATE_B), gate(COL_GATE_B + half),
            full((DA_WIDTH, D_MODEL)), full((RET_WIDTH, D_MODEL)), full((D_MODEL, D_MODEL)),
            full((1, D_MODEL)), full((D_MODEL, LANES)), full((D_MODEL, LANES)), full((1, LANES)),
            full((tm, tm)),
        ],
        out_specs=[
            pl.BlockSpec((tm, D_MODEL), lambda i: (i, 0)),
            pl.BlockSpec((ROW_PIECES, tm, PIECE), lambda i: (0, i, 0)),
            pl.BlockSpec((tm, LANES), lambda i: (i, 0)),
            pl.BlockSpec((8, LANES), lambda i: (0, 0)),
        ],
        out_shape=[
            jax.ShapeDtypeStruct((n, D_MODEL), jnp.float32),
            jax.ShapeDtypeStruct((ROW_PIECES, n, PIECE), jnp.float32),
            jax.ShapeDtypeStruct((n, LANES), jnp.float32),
            jax.ShapeDtypeStruct((8, LANES), jnp.float32),
        ],
        scratch_shapes=[pltpu.VMEM((8, LANES), jnp.float32)],
        compiler_params=pltpu.CompilerParams(dimension_semantics=("arbitrary",),
                                             vmem_limit_bytes=VMEM_LIMIT),
        name="merge",
    )(x2, oa, ob, proj, proj, proj, proj, wa.astype(jnp.bfloat16), wb.astype(jnp.bfloat16),
      wo.astype(jnp.bfloat16), g2.reshape(1, D_MODEL), wr_hi, wr_lo, br, tri)


def _sc_mesh():
    return plsc.VectorSubcoreMesh(core_axis_name="c", subcore_axis_name="s")


def _sc_scatter_rows(src, idx, out_rows, src_block):
    steps = idx.shape[1] // SC_WINDOW
    per_core = steps // SC_CORES

    @pl.kernel(out_type=jax.ShapeDtypeStruct((out_rows, PIECE), src.dtype), mesh=_sc_mesh())
    def scatter(src_hbm, idx_hbm, out_hbm):
        def body(src_vmem, idx_vmem):
            pltpu.sync_copy(src_vmem, out_hbm.at[idx_vmem.at[0]])

        pltpu.emit_pipeline(
            body,
            grid=(SC_CORES, per_core),
            in_specs=[pl.BlockSpec((SC_WINDOW, PIECE), lambda c, i: (src_block(c * per_core + i), 0)),
                      pl.BlockSpec((1, SC_WINDOW), lambda c, i: (0, c * per_core + i))],
            out_specs=[],
            core_axis_name=("c", "s"),
            dimension_semantics=(pltpu.PARALLEL, pltpu.PARALLEL),
        )(src_hbm, idx_hbm)

    return scatter(src, idx)


def _sc_gather_rows(table, idx):
    num = idx.shape[1]
    per_core = num // SC_WINDOW // SC_CORES

    @pl.kernel(out_type=jax.ShapeDtypeStruct((num, PIECE), table.dtype), mesh=_sc_mesh())
    def gather(table_hbm, idx_hbm, out_hbm):
        def body(idx_vmem, out_vmem):
            pltpu.sync_copy(table_hbm.at[idx_vmem.at[0]], out_vmem)

        pltpu.emit_pipeline(
            body,
            grid=(SC_CORES, per_core),
            in_specs=[pl.BlockSpec((1, SC_WINDOW), lambda c, i: (0, c * per_core + i))],
            out_specs=[pl.BlockSpec((SC_WINDOW, PIECE), lambda c, i: (c * per_core + i, 0))],
            core_axis_name=("c", "s"),
            dimension_semantics=(pltpu.PARALLEL, pltpu.PARALLEL),
        )(idx_hbm, out_hbm)

    return gather(table, idx)


def _load_pieces(ref):
    return jnp.concatenate([ref[j] for j in range(ROW_PIECES)], axis=1)


def _expert_kernel(blk_e_ref, n_used_ref, x_ref, wg_ref, wu_ref, wd_ref, o_ref):
    i = pl.program_id(0)

    @pl.when(i < n_used_ref[0])
    def _():
        x = _load_pieces(x_ref).astype(jnp.bfloat16)
        a = _dot(x, wg_ref[0])
        u = _dot(x, wu_ref[0])
        hmid = (a * _sigmoid(a) * u).astype(jnp.bfloat16)
        y = _dot(hmid, wd_ref[0])
        for j in range(ROW_PIECES):
            o_ref[j] = y[:, j * PIECE:(j + 1) * PIECE]

    @pl.when(i >= n_used_ref[0])
    def _():
        o_ref[...] = jnp.zeros(o_ref.shape, o_ref.dtype)


def _experts(xs, blk_expert, n_used, w_gate, w_up, w_down):
    p = xs.shape[1]
    live = lambda i, be, nu: jnp.minimum(i, nu[0] - 1)
    return pl.pallas_call(
        _expert_kernel,
        grid_spec=pltpu.PrefetchScalarGridSpec(
            num_scalar_prefetch=2,
            grid=(p // MOE_BLOCK,),
            in_specs=[
                pl.BlockSpec((ROW_PIECES, MOE_BLOCK, PIECE), lambda i, be, nu: (0, live(i, be, nu), 0)),
                pl.BlockSpec((1, D_MODEL, EXPERT_FF), lambda i, be, nu: (be[i], 0, 0)),
                pl.BlockSpec((1, D_MODEL, EXPERT_FF), lambda i, be, nu: (be[i], 0, 0)),
                pl.BlockSpec((1, EXPERT_FF, D_MODEL), lambda i, be, nu: (be[i], 0, 0)),
            ],
            out_specs=pl.BlockSpec((ROW_PIECES, MOE_BLOCK, PIECE), lambda i, be, nu: (0, i, 0)),
        ),
        out_shape=jax.ShapeDtypeStruct((ROW_PIECES, p, PIECE), jnp.float32),
        compiler_params=pltpu.CompilerParams(dimension_semantics=("arbitrary",),
                                             vmem_limit_bytes=VMEM_LIMIT),
        name="experts",
    )(blk_expert, n_used, xs, w_gate.astype(jnp.bfloat16), w_up.astype(jnp.bfloat16),
      w_down.astype(jnp.bfloat16))


def _combine_kernel(x1_ref, route_ref, y0_ref, y1_ref, o_ref):
    route = route_ref[...]
    o_ref[...] = x1_ref[...] + route[:, 2:3] * _load_pieces(y0_ref) + route[:, 3:4] * _load_pieces(y1_ref)


def _combine(x1, yg, route):
    n = x1.shape[0]
    tm = min(PROJ_ROWS, n)
    return pl.pallas_call(
        _combine_kernel,
        grid=(n // tm,),
        in_specs=[
            pl.BlockSpec((tm, D_MODEL), lambda i: (i, 0)),
            pl.BlockSpec((tm, LANES), lambda i: (i, 0)),
            pl.BlockSpec((ROW_PIECES, tm, PIECE), lambda i: (0, i, 0)),
            pl.BlockSpec((ROW_PIECES, tm, PIECE), lambda i: (0, i + n // tm, 0)),
        ],
        out_specs=pl.BlockSpec((tm, D_MODEL), lambda i: (i, 0)),
        out_shape=jax.ShapeDtypeStruct((n, D_MODEL), jnp.float32),
        compiler_params=pltpu.CompilerParams(dimension_semantics=("arbitrary",),
                                             vmem_limit_bytes=VMEM_LIMIT),
        name="combine",
    )(x1, route, yg, yg)


def _dispatch_plan(route, counts, n):
    counts = counts[0, :N_EXPERTS].astype(jnp.int32)
    padded = ((counts + MOE_BLOCK - 1) // MOE_BLOCK) * MOE_BLOCK
    seg_end = jnp.cumsum(padded).astype(jnp.int32)
    seg_start = seg_end - padded
    e = route[:, 0:TOP_K].astype(jnp.int32)
    rank = route[:, 4:4 + TOP_K].astype(jnp.int32)
    dest = (seg_start[e] + rank).T.reshape(-1)
    p = n * TOP_K + N_EXPERTS * MOE_BLOCK
    slot = (dest[None, :] + (jnp.arange(ROW_PIECES, dtype=jnp.int32) * p)[:, None]).reshape(1, -1)
    blk_start = jnp.arange(p // MOE_BLOCK, dtype=jnp.int32) * MOE_BLOCK
    blk_expert = jnp.sum((seg_end[None, :] <= blk_start[:, None]).astype(jnp.int32), axis=1)
    blk_expert = jnp.minimum(blk_expert, N_EXPERTS - 1)
    n_used = (seg_end[-1] // MOE_BLOCK).reshape(1)
    return slot, blk_expert, n_used, p


def _layer(x, positions, norm1_g, w_in, q_norm_g, k_norm_g, lam4, diff_subln_g, ret_gn_g, ret_gn_b,
           w_branch_a, w_branch_b, w_out, norm2_g, w_gr, b_gr, w_er, b_er, w_gate, w_up, w_down):
    batch, seq, _ = x.shape
    n = batch * seq
    x2 = x.reshape(n, D_MODEL)
    proj = _in_proj(x2, positions.reshape(1, n), norm1_g, w_in, q_norm_g, k_norm_g)
    oa = _diff_attn(proj, lam4, diff_subln_g.reshape(1, DA_VDIM), batch, seq)
    ob = _retention(proj, ret_gn_g, ret_gn_b, batch, seq)
    x1, h2, route, counts = _merge(x2, oa, ob, proj, w_branch_a, w_branch_b, w_out, norm2_g,
                                   w_gr, b_gr, w_er, b_er)
    slot, blk_expert, n_used, p = _dispatch_plan(route, counts, n)
    win_n = n // SC_WINDOW
    src_block = lambda s: (s // (TOP_K * win_n)) * win_n + s % win_n
    xs = _sc_scatter_rows(h2.reshape(ROW_PIECES * n, PIECE), slot, ROW_PIECES * p, src_block)
    ys = _experts(xs.reshape(ROW_PIECES, p, PIECE), blk_expert, n_used, w_gate, w_up, w_down)
    yg = _sc_gather_rows(ys.reshape(ROW_PIECES * p, PIECE), slot)
    out = _combine(x1, yg.reshape(ROW_PIECES, TOP_K * n, PIECE), route)
    return out.reshape(batch, seq, D_MODEL)


def kernel(x, positions, norm1_g, w_in, q_norm_g, k_norm_g, lambda_q1, lambda_k1, lambda_q2, lambda_k2, diff_subln_g, ret_gn_g, ret_gn_b, w_branch_a, w_branch_b, w_out, norm2_g, w_group_router, b_group_router, w_expert_router, b_expert_router, w_gate, w_up, w_down):
    assert x.shape[-1] == D_MODEL and norm1_g.shape[0] == 1, "single-layer, D_MODEL-wide input expected"
    lam4 = jnp.zeros((4, LANES), jnp.float32)
    lam4 = lam4.at[:, :DA_HALF].set(jnp.stack([lambda_q1[0], lambda_k1[0], lambda_q2[0], lambda_k2[0]]))
    return _layer(x, positions, norm1_g[0], w_in[0], q_norm_g[0], k_norm_g[0], lam4, diff_subln_g[0],
                  ret_gn_g[0], ret_gn_b[0], w_branch_a[0], w_branch_b[0], w_out[0], norm2_g[0],
                  w_group_router[0], b_group_router[0], w_expert_router[0], b_expert_router[0],
                  w_gate[0], w_up[0], w_down[0])
```

```python
import functools
import math

import jax
import jax.numpy as jnp
from jax import lax
from jax.experimental import pallas as pl
from jax.experimental.pallas import tpu as pltpu
from jax.experimental.pallas import tpu_sc as plsc

D_MODEL = 1024
DA_HEADS = 4
DA_HALF = 64
DA_VDIM = 2 * DA_HALF
DA_WIDTH = DA_HEADS * DA_VDIM
ROPE_THETA = 500000.0
ROPE_DIM = DA_HALF // 4
RET_HEADS = 4
RET_KDIM = 128
RET_VDIM = 128
RET_WIDTH = RET_HEADS * RET_VDIM
RET_THETA = 10000.0
N_GROUPS = 4
EXPERTS_PER_GROUP = 8
N_EXPERTS = N_GROUPS * EXPERTS_PER_GROUP
TOP_K = 2
EXPERT_FF = 512
EPS = 1e-6
LAMBDA_INIT = 0.8 - 0.6 * math.exp(-0.3 * 0)

LANES = 128
IN_COLS = 3 * DA_WIDTH + 4 * RET_WIDTH + 2 * D_MODEL
COL_QA, COL_KA, COL_VA = 0, DA_WIDTH, 2 * DA_WIDTH
COL_QR = 3 * DA_WIDTH
COL_KR = COL_QR + RET_WIDTH
COL_VR = COL_KR + RET_WIDTH
COL_GB = COL_VR + RET_WIDTH
COL_GATE_A = COL_GB + RET_WIDTH
COL_GATE_B = COL_GATE_A + D_MODEL

PROJ_ROWS = 512
PROJ_CHUNK = 256
ATT_TILE = 512
ATT_ROWS = 32
RET_CHUNK = 256
RET_UNROLL = 8
MOE_BLOCK = 512
ROW_PIECES = 4
PIECE = D_MODEL // ROW_PIECES
SC_CORES = 2
SC_WINDOW = 128
VMEM_LIMIT = 56 * 1024 * 1024


def _dot(a, b):
    return jnp.dot(a, b, preferred_element_type=jnp.float32)


def _dot_nt(a, b):
    return lax.dot_general(a, b, (((1,), (1,)), ((), ())), preferred_element_type=jnp.float32)


def _dot_tn(a, b):
    return lax.dot_general(a, b, (((0,), (0,)), ((), ())), preferred_element_type=jnp.float32)


def _sigmoid(x):
    return 0.5 * jnp.tanh(0.5 * x) + 0.5


def _split3(x):
    a = x.astype(jnp.bfloat16)
    r = x - a.astype(jnp.float32)
    b = r.astype(jnp.bfloat16)
    c = (r - b.astype(jnp.float32)).astype(jnp.bfloat16)
    return a, b, c


def _in_proj_kernel(x_ref, pos_ref, g1_ref, w_ref, gsum_ref, gq_ref, gk_ref, fa_ref, fr_ref, sel_ref,
                    o_ref, h_scr):
    x = x_ref[...]
    h = x * lax.rsqrt(jnp.mean(x * x, axis=-1, keepdims=True) + EPS) * g1_ref[...]
    h_scr[...] = h.astype(jnp.bfloat16)
    rows = x.shape[0]
    pos = pos_ref[...].astype(jnp.float32)

    lane = lax.broadcasted_iota(jnp.int32, (rows, LANES), 1)
    half_a = ROPE_DIM // 2
    ang_a = fa_ref[...] * pos
    pad = jnp.zeros((LANES - 2 * half_a, rows), jnp.float32)
    t_a = jnp.concatenate([jnp.cos(ang_a), jnp.sin(ang_a), pad], axis=0).T
    tab = sum(_dot(part, sel_ref[...]) for part in _split3(t_a))
    lane64 = lane % DA_HALF
    c_a = tab[:, :LANES] + jnp.where(lane64 < ROPE_DIM, 0.0, 1.0)
    s_lo = tab[:, LANES:2 * LANES]
    s_hi = tab[:, 2 * LANES:]
    c_a2 = jnp.concatenate([c_a, c_a], axis=1)
    s_lo2 = jnp.concatenate([s_lo, s_lo], axis=1)
    s_hi2 = jnp.concatenate([s_hi, s_hi], axis=1)
    ang_r = fr_ref[...] * pos
    t_r = jnp.concatenate([jnp.cos(ang_r), jnp.sin(ang_r)], axis=0).T
    sw_r = pltpu.roll(t_r, RET_KDIM // 2, axis=1)
    first = lane < RET_KDIM // 2
    c_r = jnp.where(first, t_r, sw_r)
    s_r = jnp.where(first, -sw_r, t_r)
    c_r2 = jnp.concatenate([c_r, c_r], axis=1)
    s_r2 = jnp.concatenate([s_r, s_r], axis=1)

    def qk_norm_rope(y, g, scale):
        ss = y * y
        hi = ss.astype(jnp.bfloat16)
        lo = (ss - hi.astype(jnp.float32)).astype(jnp.bfloat16)
        gs = _dot(hi, gsum_ref[...]) + _dot(lo, gsum_ref[...])
        n = y * lax.rsqrt(gs * (1.0 / DA_HALF) + EPS) * g
        up = pltpu.roll(n, PROJ_CHUNK - half_a, axis=1)
        dn = pltpu.roll(n, half_a, axis=1)
        r = n * c_a2 + up * s_lo2 + dn * s_hi2
        return r * scale if scale != 1.0 else r

    def ret_rope(y, scale):
        halves = [pltpu.roll(y[:, i * LANES:(i + 1) * LANES], RET_KDIM // 2, axis=1)
                  for i in range(PROJ_CHUNK // LANES)]
        sw = jnp.concatenate(halves, axis=1)
        r = y * c_r2 + sw * s_r2
        return r * scale if scale != 1.0 else r

    for c in range(IN_COLS // PROJ_CHUNK):
        c0 = c * PROJ_CHUNK
        y = _dot(h_scr[...], w_ref[:, c0:c0 + PROJ_CHUNK])
        if c0 < COL_KA:
            y = qk_norm_rope(y, gq_ref[...], DA_HALF ** -0.5)
        elif c0 < COL_VA:
            y = qk_norm_rope(y, gk_ref[...], 1.0)
        elif c0 < COL_QR:
            pass
        elif c0 < COL_KR:
            y = ret_rope(y, 1.0)
        elif c0 < COL_VR:
            y = ret_rope(y, RET_KDIM ** -0.5)
        elif c0 < COL_GB:
            pass
        elif c0 < COL_GATE_A:
            y = y * _sigmoid(y)
        else:
            y = _sigmoid(y)
        o_ref[:, c0:c0 + PROJ_CHUNK] = y.astype(o_ref.dtype)


def _in_proj(x2, pos2, g1, w_in, gq, gk):
    n = x2.shape[0]
    tm = min(PROJ_ROWS, n)
    grp = jnp.arange(PROJ_CHUNK) // DA_HALF
    gsum = (grp[:, None] == grp[None, :]).astype(jnp.bfloat16)
    half_a = ROPE_DIM // 2
    fa = jnp.power(jnp.float32(ROPE_THETA), -2.0 * jnp.arange(half_a, dtype=jnp.float32) / ROPE_DIM)[:, None]
    half_r = RET_KDIM // 2
    fr = jnp.power(jnp.float32(RET_THETA), -2.0 * jnp.arange(half_r, dtype=jnp.float32) / RET_KDIM)[:, None]
    j = jnp.arange(LANES)[:, None]
    l64 = (jnp.arange(LANES) % DA_HALF)[None, :]
    sel_c = (j < half_a) & (l64 < ROPE_DIM) & (l64 % half_a == j)
    sel_lo = (j >= half_a) & (j < ROPE_DIM) & (l64 < half_a) & (l64 == j - half_a)
    sel_hi = (j >= half_a) & (j < ROPE_DIM) & (l64 >= half_a) & (l64 < ROPE_DIM) & (l64 == j)
    sel = jnp.concatenate([sel_c.astype(jnp.float32), -sel_lo.astype(jnp.float32),
                           sel_hi.astype(jnp.float32)], axis=1).astype(jnp.bfloat16)
    reps = PROJ_CHUNK // DA_HALF
    full = lambda shape: pl.BlockSpec(shape, lambda i: (0,) * len(shape))
    return pl.pallas_call(
        _in_proj_kernel,
        grid=(n // tm,),
        in_specs=[
            pl.BlockSpec((tm, D_MODEL), lambda i: (i, 0)),
            pl.BlockSpec((1, tm), lambda i: (0, i)),
            full((1, D_MODEL)),
            full((D_MODEL, IN_COLS)),
            full((PROJ_CHUNK, PROJ_CHUNK)),
            full((1, PROJ_CHUNK)),
            full((1, PROJ_CHUNK)),
            full((half_a, 1)),
            full((half_r, 1)),
            full((LANES, 3 * LANES)),
        ],
        out_specs=pl.BlockSpec((tm, IN_COLS), lambda i: (i, 0)),
        out_shape=jax.ShapeDtypeStruct((n, IN_COLS), jnp.bfloat16),
        scratch_shapes=[pltpu.VMEM((tm, D_MODEL), jnp.bfloat16)],
        compiler_params=pltpu.CompilerParams(dimension_semantics=("arbitrary",),
                                             vmem_limit_bytes=VMEM_LIMIT),
        name="in_proj",
    )(x2, pos2, g1.reshape(1, D_MODEL), w_in.astype(jnp.bfloat16), gsum,
      jnp.tile(gq, reps)[None, :], jnp.tile(gk, reps)[None, :], fa, fr, sel)


def _diff_attn_kernel(q_ref, k_ref, v_ref, lam_ref, gsub_ref, o_ref,
                      qs_scr, vx_scr, s0_scr, s1_scr, p_scr, m_scr, alpha_scr, acc_scr):
    i = pl.program_id(2)
    t = q_ref.shape[0]

    @pl.when(i == 0)
    def _():
        vx_scr[:, :DA_VDIM] = v_ref[...]
        vx_scr[:, DA_VDIM:] = jnp.ones((vx_scr.shape[0], LANES), vx_scr.dtype)

    q = q_ref[...]
    lane = lax.broadcasted_iota(jnp.int32, q.shape, 1)
    zero = jnp.zeros_like(q)
    qs_scr[:t] = jnp.where(lane < DA_HALF, q, zero)
    qs_scr[t:] = jnp.where(lane >= DA_HALF, q, zero)
    m_scr[...] = jnp.full(m_scr.shape, -jnp.inf, jnp.float32)
    acc_scr[...] = jnp.zeros(acc_scr.shape, jnp.float32)

    def scores(j, s_ref):
        start = pl.multiple_of(j * t, t)
        s_ref[...] = _dot_nt(qs_scr[...], k_ref[pl.ds(start, t), :])

    def softmax_pv(j, s_ref, masked):
        for c in range(2 * t // ATT_ROWS):
            rows = pl.ds(c * ATT_ROWS, ATT_ROWS)
            s = s_ref[rows, :]
            if masked:
                r = lax.broadcasted_iota(jnp.int32, s.shape, 0) + (c * ATT_ROWS) % t
                col = lax.broadcasted_iota(jnp.int32, s.shape, 1)
                s = jnp.where(col <= r, s, -jnp.inf)
            m_prev = m_scr[rows, :]
            m_new = jnp.maximum(m_prev, jnp.max(s, axis=-1, keepdims=True))
            alpha_scr[rows, :] = jnp.exp(m_prev - m_new)
            m_scr[rows, :] = m_new
            p = jnp.exp(s - jnp.concatenate([m_new] * (t // LANES), axis=1))
            p_scr[rows, :] = p.astype(p_scr.dtype)
        start = pl.multiple_of(j * t, t)
        pv = _dot(p_scr[...], vx_scr[pl.ds(start, t), :])
        alpha = alpha_scr[...]
        for half in range(2):
            cols = pl.ds(half * LANES, LANES)
            acc_scr[:, cols] = alpha * acc_scr[:, cols] + pv[:, half * LANES:(half + 1) * LANES]

    scores(0, s0_scr)

    def pair(jj, carry):
        j = 2 * jj
        scores(j + 1, s1_scr)
        softmax_pv(j, s0_scr, False)
        scores(j + 2, s0_scr)
        softmax_pv(j + 1, s1_scr, False)
        return carry

    lax.fori_loop(0, i // 2, pair, 0)

    @pl.when(i % 2 == 1)
    def _():
        scores(i, s1_scr)
        softmax_pv(i - 1, s0_scr, False)
        softmax_pv(i, s1_scr, True)

    @pl.when(i % 2 == 0)
    def _():
        softmax_pv(i, s0_scr, True)

    lam4 = lam_ref[...]
    lam = (jnp.exp(jnp.sum(lam4[0:1] * lam4[1:2], axis=-1, keepdims=True))
           - jnp.exp(jnp.sum(lam4[2:3] * lam4[3:4], axis=-1, keepdims=True)) + LAMBDA_INIT)
    o_all = acc_scr[:, :DA_VDIM] / acc_scr[:, DA_VDIM:]
    o = o_all[:t] - lam * o_all[t:]
    o = o * lax.rsqrt(jnp.mean(o * o, axis=-1, keepdims=True) + EPS) * gsub_ref[...] * (1.0 - LAMBDA_INIT)
    o_ref[...] = o.astype(o_ref.dtype)


def _diff_attn(proj, lam4, gsub, batch, seq):
    n = proj.shape[0]
    t = min(ATT_TILE, seq)
    nq = seq // t
    qb, kb, vb = COL_QA // LANES, COL_KA // LANES, COL_VA // LANES
    return pl.pallas_call(
        _diff_attn_kernel,
        grid=(batch, DA_HEADS, nq),
        in_specs=[
            pl.BlockSpec((t, LANES), lambda b, h, i: (b * nq + i, qb + h)),
            pl.BlockSpec((seq, LANES), lambda b, h, i: (b, kb + h)),
            pl.BlockSpec((seq, LANES), lambda b, h, i: (b, vb + h)),
            pl.BlockSpec((4, LANES), lambda b, h, i: (0, 0)),
            pl.BlockSpec((1, LANES), lambda b, h, i: (0, 0)),
        ],
        out_specs=pl.BlockSpec((t, LANES), lambda b, h, i: (b * nq + i, h)),
        out_shape=jax.ShapeDtypeStruct((n, DA_WIDTH), jnp.bfloat16),
        scratch_shapes=[pltpu.VMEM((2 * t, LANES), jnp.bfloat16),
                        pltpu.VMEM((seq, DA_VDIM + LANES), jnp.bfloat16),
                        pltpu.VMEM((2 * t, t), jnp.float32),
                        pltpu.VMEM((2 * t, t), jnp.float32),
                        pltpu.VMEM((2 * t, t), jnp.bfloat16),
                        pltpu.VMEM((2 * t, LANES), jnp.float32),
                        pltpu.VMEM((2 * t, LANES), jnp.float32),
                        pltpu.VMEM((2 * t, DA_VDIM + LANES), jnp.float32)],
        compiler_params=pltpu.CompilerParams(dimension_semantics=("arbitrary",) * 3,
                                             vmem_limit_bytes=VMEM_LIMIT),
        name="diff_attn",
    )(proj, proj, proj, lam4, gsub)


def _retention_kernel(q_ref, k_ref, v_ref, g_ref, gng_ref, gnb_ref, o_ref, r_scr, *, chunk):
    hf = jnp.full((1, 1), pl.program_id(1), jnp.int32).astype(jnp.float32)
    log_g = jnp.log1p(-jnp.exp2(-5.0 - hf))
    ri = lax.broadcasted_iota(jnp.int32, (chunk, chunk), 0)
    ci = lax.broadcasted_iota(jnp.int32, (chunk, chunk), 1)
    rel = (ri - ci).astype(jnp.float32)
    dmask = jnp.where(rel >= 0, jnp.exp(jnp.maximum(rel, 0.0) * log_g), 0.0)
    idx = lax.broadcasted_iota(jnp.int32, (chunk, 1), 0).astype(jnp.float32)
    zeta = jnp.exp((chunk - 1 - idx) * log_g)
    xi = jnp.exp((idx + 1.0) * log_g)
    g_chunk = jnp.exp(chunk * log_g)
    r_scr[...] = jnp.zeros(r_scr.shape, jnp.float32)
    gng = gng_ref[...]
    gnb = gnb_ref[...]

    def body(n, carry):
        start = pl.multiple_of(n * chunk, chunk)
        q = q_ref[pl.ds(start, chunk), :]
        k = k_ref[pl.ds(start, chunk), :]
        v = v_ref[pl.ds(start, chunk), :]
        s = _dot_nt(q, k) * dmask
        r_old = r_scr[...]
        o = _dot(s.astype(jnp.bfloat16), v) + xi * _dot(q, r_old.astype(jnp.bfloat16))
        kz = (k.astype(jnp.float32) * zeta).astype(jnp.bfloat16)
        r_scr[...] = g_chunk * r_old + _dot_tn(kz, v)
        mu = jnp.mean(o, axis=-1, keepdims=True)
        d = o - mu
        var = jnp.mean(d * d, axis=-1, keepdims=True)
        y = d * lax.rsqrt(var + EPS) * gng + gnb
        y = y * g_ref[pl.ds(start, chunk), :].astype(jnp.float32)
        o_ref[pl.ds(start, chunk), :] = y.astype(o_ref.dtype)
        return carry

    lax.fori_loop(0, q_ref.shape[0] // chunk, body, 0, unroll=RET_UNROLL)


def _retention(proj, gn_g, gn_b, batch, seq):
    n = proj.shape[0]
    chunk = min(RET_CHUNK, seq)
    col = lambda c0: (lambda b, h: (b, c0 // LANES + h))
    return pl.pallas_call(
        functools.partial(_retention_kernel, chunk=chunk),
        grid=(batch, RET_HEADS),
        in_specs=[
            pl.BlockSpec((seq, LANES), col(COL_QR)),
            pl.BlockSpec((seq, LANES), col(COL_KR)),
            pl.BlockSpec((seq, LANES), col(COL_VR)),
            pl.BlockSpec((seq, LANES), col(COL_GB)),
            pl.BlockSpec((1, LANES), lambda b, h: (0, h)),
            pl.BlockSpec((1, LANES), lambda b, h: (0, h)),
        ],
        out_specs=pl.BlockSpec((seq, LANES), lambda b, h: (b, h)),
        out_shape=jax.ShapeDtypeStruct((n, RET_WIDTH), jnp.bfloat16),
        scratch_shapes=[pltpu.VMEM((RET_KDIM, RET_VDIM), jnp.float32)],
        compiler_params=pltpu.CompilerParams(dimension_semantics=("arbitrary",) * 2,
                                             vmem_limit_bytes=VMEM_LIMIT),
        name="retention",
    )(proj, proj, proj, proj, gn_g.reshape(1, RET_WIDTH), gn_b.reshape(1, RET_WIDTH))


def _merge_kernel(x_ref, oa_ref, ob_ref, sa0_ref, sa1_ref, sb0_ref, sb1_ref, wa_ref, wb_ref, wo_ref,
                  g2_ref, wr_hi_ref, wr_lo_ref, br_ref, tri_ref, x1_ref, h2_ref, route_ref, counts_ref,
                  base_scr):
    ya = _dot(oa_ref[...], wa_ref[...])
    yb = _dot(ob_ref[...], wb_ref[...])
    sa = jnp.concatenate([sa0_ref[...], sa1_ref[...]], axis=1).astype(jnp.float32)
    sb = jnp.concatenate([sb0_ref[...], sb1_ref[...]], axis=1).astype(jnp.float32)
    merged = sa * ya + sb * yb
    x1 = x_ref[...] + _dot(merged.astype(jnp.bfloat16), wo_ref[...])
    x1_ref[...] = x1
    h2 = x1 * lax.rsqrt(jnp.mean(x1 * x1, axis=-1, keepdims=True) + EPS) * g2_ref[...]
    for j in range(ROW_PIECES):
        h2_ref[j] = h2[:, j * PIECE:(j + 1) * PIECE]

    hi = h2.astype(jnp.bfloat16)
    lo = (h2 - hi.astype(jnp.float32)).astype(jnp.bfloat16)
    logits = (_dot(hi, wr_hi_ref[...]) + _dot(lo, wr_hi_ref[...]) + _dot(hi, wr_lo_ref[...])
              + br_ref[...])
    lane = lax.broadcasted_iota(jnp.int32, logits.shape, 1)
    neg = -jnp.inf
    gl = jnp.where(lane < N_GROUPS, logits, neg)
    gmax = jnp.max(gl, axis=-1, keepdims=True)
    g_idx = jnp.min(jnp.where(gl == gmax, lane, LANES), axis=-1, keepdims=True)
    p_g = 1.0 / jnp.sum(jnp.exp(gl - gmax), axis=-1, keepdims=True)
    e_lo = N_GROUPS + EXPERTS_PER_GROUP * g_idx
    el = jnp.where((lane >= e_lo) & (lane < e_lo + EXPERTS_PER_GROUP), logits, neg)
    v1 = jnp.max(el, axis=-1, keepdims=True)
    i1 = jnp.min(jnp.where(el == v1, lane, LANES), axis=-1, keepdims=True)
    el2 = jnp.where(lane == i1, neg, el)
    v2 = jnp.max(el2, axis=-1, keepdims=True)
    i2 = jnp.min(jnp.where(el2 == v2, lane, LANES), axis=-1, keepdims=True)
    t = jnp.exp(v2 - v1)
    w1 = p_g / (1.0 + t)
    w2 = p_g * t / (1.0 + t)
    e1 = i1 - N_GROUPS
    e2 = i2 - N_GROUPS

    @pl.when(pl.program_id(0) == 0)
    def _():
        base_scr[...] = jnp.zeros(base_scr.shape, jnp.float32)

    oh1 = lane == e1
    oh2 = lane == e2
    picked = jnp.where(oh1 | oh2, 1.0, 0.0)
    before = _dot(tri_ref[...], picked.astype(jnp.bfloat16)) + base_scr[0:1, :]
    rank1 = jnp.sum(jnp.where(oh1, before, 0.0), axis=-1, keepdims=True)
    rank2 = jnp.sum(jnp.where(oh2, before, 0.0), axis=-1, keepdims=True)
    base_scr[...] = base_scr[...] + jnp.sum(picked, axis=0, keepdims=True)
    counts_ref[...] = base_scr[...]

    cols = [e1.astype(jnp.float32), e2.astype(jnp.float32), w1, w2, rank1, rank2]
    route = jnp.zeros(logits.shape, jnp.float32)
    for c, val in enumerate(cols):
        route = jnp.where(lane == c, val, route)
    route_ref[...] = route


def _merge(x2, oa, ob, proj, wa, wb, wo, g2, w_gr, b_gr, w_er, b_er):
    n = x2.shape[0]
    tm = min(PROJ_ROWS, n)
    half = D_MODEL // 2
    wr = jnp.zeros((D_MODEL, LANES), jnp.float32)
    wr = wr.at[:, :N_GROUPS].set(w_gr).at[:, N_GROUPS:N_GROUPS + N_EXPERTS].set(w_er)
    wr_hi = wr.astype(jnp.bfloat16)
    wr_lo = (wr - wr_hi.astype(jnp.float32)).astype(jnp.bfloat16)
    br = jnp.zeros((1, LANES), jnp.float32)
    br = br.at[0, :N_GROUPS].set(b_gr).at[0, N_GROUPS:N_GROUPS + N_EXPERTS].set(b_er)
    tri = (jnp.arange(tm)[:, None] > jnp.arange(tm)[None, :]).astype(jnp.bfloat16)
    full = lambda shape: pl.BlockSpec(shape, lambda i: (0,) * len(shape))
    gate = lambda c0: pl.BlockSpec((tm, half), lambda i: (i, c0 // half))
    return pl.pallas_call(
        _merge_kernel,
        grid=(n // tm,),
        in_specs=[
            pl.BlockSpec((tm, D_MODEL), lambda i: (i, 0)),
            pl.BlockSpec((tm, DA_WIDTH), lambda i: (i, 0)),
            pl.BlockSpec((tm, RET_WIDTH), lambda i: (i, 0)),
            gate(COL_GATE_A), gate(COL_GATE_A + half), gate(COL_GATE_B), gate(COL_GATE_B + half),
            full((DA_WIDTH, D_MODEL)), full((RET_WIDTH, D_MODEL)), full((D_MODEL, D_MODEL)),
            full((1, D_MODEL)), full((D_MODEL, LANES)), full((D_MODEL, LANES)), full((1, LANES)),
            full((tm, tm)),
        ],
        out_specs=[
            pl.BlockSpec((tm, D_MODEL), lambda i: (i, 0)),
            pl.BlockSpec((ROW_PIECES, tm, PIECE), lambda i: (0, i, 0)),
            pl.BlockSpec((tm, LANES), lambda i: (i, 0)),
            pl.BlockSpec((8, LANES), lambda i: (0, 0)),
        ],
        out_shape=[
            jax.ShapeDtypeStruct((n, D_MODEL), jnp.float32),
            jax.ShapeDtypeStruct((ROW_PIECES, n, PIECE), jnp.float32),
            jax.ShapeDtypeStruct((n, LANES), jnp.float32),
            jax.ShapeDtypeStruct((8, LANES), jnp.float32),
        ],
        scratch_shapes=[pltpu.VMEM((8, LANES), jnp.float32)],
        compiler_params=pltpu.CompilerParams(dimension_semantics=("arbitrary",),
                                             vmem_limit_bytes=VMEM_LIMIT),
        name="merge",
    )(x2, oa, ob, proj, proj, proj, proj, wa.astype(jnp.bfloat16), wb.astype(jnp.bfloat16),
      wo.astype(jnp.bfloat16), g2.reshape(1, D_MODEL), wr_hi, wr_lo, br, tri)


def _sc_mesh():
    return plsc.VectorSubcoreMesh(core_axis_name="c", subcore_axis_name="s")


def _sc_scatter_rows(src, idx, out_rows, src_block):
    steps = idx.shape[1] // SC_WINDOW
    per_core = steps // SC_CORES

    @pl.kernel(out_type=jax.ShapeDtypeStruct((out_rows, PIECE), src.dtype), mesh=_sc_mesh())
    def scatter(src_hbm, idx_hbm, out_hbm):
        def body(src_vmem, idx_vmem):
            pltpu.sync_copy(src_vmem, out_hbm.at[idx_vmem.at[0]])

        pltpu.emit_pipeline(
            body,
            grid=(SC_CORES, per_core),
            in_specs=[pl.BlockSpec((SC_WINDOW, PIECE), lambda c, i: (src_block(c * per_core + i), 0)),
                      pl.BlockSpec((1, SC_WINDOW), lambda c, i: (0, c * per_core + i))],
            out_specs=[],
            core_axis_name=("c", "s"),
            dimension_semantics=(pltpu.PARALLEL, pltpu.PARALLEL),
        )(src_hbm, idx_hbm)

    return scatter(src, idx)


def _sc_gather_rows(table, idx):
    num = idx.shape[1]
    per_core = num // SC_WINDOW // SC_CORES

    @pl.kernel(out_type=jax.ShapeDtypeStruct((num, PIECE), table.dtype), mesh=_sc_mesh())
    def gather(table_hbm, idx_hbm, out_hbm):
        def body(idx_vmem, out_vmem):
            pltpu.sync_copy(table_hbm.at[idx_vmem.at[0]], out_vmem)

        pltpu.emit_pipeline(
            body,
            grid=(SC_CORES, per_core),
            in_specs=[pl.BlockSpec((1, SC_WINDOW), lambda c, i: (0, c * per_core + i))],
            out_specs=[pl.BlockSpec((SC_WINDOW, PIECE), lambda c, i: (c * per_core + i, 0))],
            core_axis_name=("c", "s"),
            dimension_semantics=(pltpu.PARALLEL, pltpu.PARALLEL),
        )(idx_hbm, out_hbm)

    return gather(table, idx)


def _load_pieces(ref):
    return jnp.concatenate([ref[j] for j in range(ROW_PIECES)], axis=1)


def _expert_kernel(blk_e_ref, n_used_ref, x_ref, wg_ref, wu_ref, wd_ref, o_ref):
    i = pl.program_id(0)

    @pl.when(i < n_used_ref[0])
    def _():
        x = _load_pieces(x_ref).astype(jnp.bfloat16)
        a = _dot(x, wg_ref[0])
        u = _dot(x, wu_ref[0])
        hmid = (a * _sigmoid(a) * u).astype(jnp.bfloat16)
        y = _dot(hmid, wd_ref[0])
        for j in range(ROW_PIECES):
            o_ref[j] = y[:, j * PIECE:(j + 1) * PIECE]

    @pl.when(i >= n_used_ref[0])
    def _():
        o_ref[...] = jnp.zeros(o_ref.shape, o_ref.dtype)


def _experts(xs, blk_expert, n_used, w_gate, w_up, w_down):
    p = xs.shape[1]
    live = lambda i, be, nu: jnp.minimum(i, nu[0] - 1)
    return pl.pallas_call(
        _expert_kernel,
        grid_spec=pltpu.PrefetchScalarGridSpec(
            num_scalar_prefetch=2,
            grid=(p // MOE_BLOCK,),
            in_specs=[
                pl.BlockSpec((ROW_PIECES, MOE_BLOCK, PIECE), lambda i, be, nu: (0, live(i, be, nu), 0)),
                pl.BlockSpec((1, D_MODEL, EXPERT_FF), lambda i, be, nu: (be[i], 0, 0)),
                pl.BlockSpec((1, D_MODEL, EXPERT_FF), lambda i, be, nu: (be[i], 0, 0)),
                pl.BlockSpec((1, EXPERT_FF, D_MODEL), lambda i, be, nu: (be[i], 0, 0)),
            ],
            out_specs=pl.BlockSpec((ROW_PIECES, MOE_BLOCK, PIECE), lambda i, be, nu: (0, i, 0)),
        ),
        out_shape=jax.ShapeDtypeStruct((ROW_PIECES, p, PIECE), jnp.float32),
        compiler_params=pltpu.CompilerParams(dimension_semantics=("arbitrary",),
                                             vmem_limit_bytes=VMEM_LIMIT),
        name="experts",
    )(blk_expert, n_used, xs, w_gate.astype(jnp.bfloat16), w_up.astype(jnp.bfloat16),
      w_down.astype(jnp.bfloat16))


def _combine_kernel(x1_ref, route_ref, y0_ref, y1_ref, o_ref):
    route = route_ref[...]
    o_ref[...] = x1_ref[...] + route[:, 2:3] * _load_pieces(y0_ref) + route[:, 3:4] * _load_pieces(y1_ref)


def _combine(x1, yg, route):
    n = x1.shape[0]
    tm = min(PROJ_ROWS, n)
    return pl.pallas_call(
        _combine_kernel,
        grid=(n // tm,),
        in_specs=[
            pl.BlockSpec((tm, D_MODEL), lambda i: (i, 0)),
            pl.BlockSpec((tm, LANES), lambda i: (i, 0)),
            pl.BlockSpec((ROW_PIECES, tm, PIECE), lambda i: (0, i, 0)),
            pl.BlockSpec((ROW_PIECES, tm, PIECE), lambda i: (0, i + n // tm, 0)),
        ],
        out_specs=pl.BlockSpec((tm, D_MODEL), lambda i: (i, 0)),
        out_shape=jax.ShapeDtypeStruct((n, D_MODEL), jnp.float32),
        compiler_params=pltpu.CompilerParams(dimension_semantics=("arbitrary",),
                                             vmem_limit_bytes=VMEM_LIMIT),
        name="combine",
    )(x1, route, yg, yg)


def _dispatch_plan(route, counts, n):
    counts = counts[0, :N_EXPERTS].astype(jnp.int32)
    padded = ((counts + MOE_BLOCK - 1) // MOE_BLOCK) * MOE_BLOCK
    seg_end = jnp.cumsum(padded).astype(jnp.int32)
    seg_start = seg_end - padded
    e = route[:, 0:TOP_K].astype(jnp.int32)
    rank = route[:, 4:4 + TOP_K].astype(jnp.int32)
    dest = (seg_start[e] + rank).T.reshape(-1)
    p = n * TOP_K + N_EXPERTS * MOE_BLOCK
    slot = (dest[None, :] + (jnp.arange(ROW_PIECES, dtype=jnp.int32) * p)[:, None]).reshape(1, -1)
    blk_start = jnp.arange(p // MOE_BLOCK, dtype=jnp.int32) * MOE_BLOCK
    blk_expert = jnp.sum((seg_end[None, :] <= blk_start[:, None]).astype(jnp.int32), axis=1)
    blk_expert = jnp.minimum(blk_expert, N_EXPERTS - 1)
    n_used = (seg_end[-1] // MOE_BLOCK).reshape(1)
    return slot, blk_expert, n_used, p


def _layer(x, positions, norm1_g, w_in, q_norm_g, k_norm_g, lam4, diff_subln_g, ret_gn_g, ret_gn_b,
           w_branch_a, w_branch_b, w_out, norm2_g, w_gr, b_gr, w_er, b_er, w_gate, w_up, w_down):
    batch, seq, _ = x.shape
    n = batch * seq
    x2 = x.reshape(n, D_MODEL)
    proj = _in_proj(x2, positions.reshape(1, n), norm1_g, w_in, q_norm_g, k_norm_g)
    oa = _diff_attn(proj, lam4, diff_subln_g.reshape(1, DA_VDIM), batch, seq)
    ob = _retention(proj, ret_gn_g, ret_gn_b, batch, seq)
    x1, h2, route, counts = _merge(x2, oa, ob, proj, w_branch_a, w_branch_b, w_out, norm2_g,
                                   w_gr, b_gr, w_er, b_er)
    slot, blk_expert, n_used, p = _dispatch_plan(route, counts, n)
    win_n = n // SC_WINDOW
    src_block = lambda s: (s // (TOP_K * win_n)) * win_n + s % win_n
    xs = _sc_scatter_rows(h2.reshape(ROW_PIECES * n, PIECE), slot, ROW_PIECES * p, src_block)
    ys = _experts(xs.reshape(ROW_PIECES, p, PIECE), blk_expert, n_used, w_gate, w_up, w_down)
    yg = _sc_gather_rows(ys.reshape(ROW_PIECES * p, PIECE), slot)
    out = _combine(x1, yg.reshape(ROW_PIECES, TOP_K * n, PIECE), route)
    return out.reshape(batch, seq, D_MODEL)


def kernel(x, positions, norm1_g, w_in, q_norm_g, k_norm_g, lambda_q1, lambda_k1, lambda_q2, lambda_k2, diff_subln_g, ret_gn_g, ret_gn_b, w_branch_a, w_branch_b, w_out, norm2_g, w_group_router, b_group_router, w_expert_router, b_expert_router, w_gate, w_up, w_down):
    assert x.shape[-1] == D_MODEL and norm1_g.shape[0] == 1, "single-layer, D_MODEL-wide input expected"
    lam4 = jnp.zeros((4, LANES), jnp.float32)
    lam4 = lam4.at[:, :DA_HALF].set(jnp.stack([lambda_q1[0], lambda_k1[0], lambda_q2[0], lambda_k2[0]]))
    return _layer(x, positions, norm1_g[0], w_in[0], q_norm_g[0], k_norm_g[0], lam4, diff_subln_g[0],
                  ret_gn_g[0], ret_gn_b[0], w_branch_a[0], w_branch_b[0], w_out[0], norm2_g[0],
                  w_group_router[0], b_group_router[0], w_expert_router[0], b_expert_router[0],
                  w_gate[0], w_up[0], w_down[0])
```

```python
import functools
import math

import jax
import jax.numpy as jnp
from jax import lax
from jax.experimental import pallas as pl
from jax.experimental.pallas import tpu as pltpu
from jax.experimental.pallas import tpu_sc as plsc

D_MODEL = 1024
DA_HEADS = 4
DA_HALF = 64
DA_VDIM = 2 * DA_HALF
DA_WIDTH = DA_HEADS * DA_VDIM
ROPE_THETA = 500000.0
ROPE_DIM = DA_HALF // 4
RET_HEADS = 4
RET_KDIM = 128
RET_VDIM = 128
RET_WIDTH = RET_HEADS * RET_VDIM
RET_THETA = 10000.0
N_GROUPS = 4
EXPERTS_PER_GROUP = 8
N_EXPERTS = N_GROUPS * EXPERTS_PER_GROUP
TOP_K = 2
EXPERT_FF = 512
EPS = 1e-6
LAMBDA_INIT = 0.8 - 0.6 * math.exp(-0.3 * 0)

LANES = 128
IN_COLS = 3 * DA_WIDTH + 4 * RET_WIDTH + 2 * D_MODEL
COL_QA, COL_KA, COL_VA = 0, DA_WIDTH, 2 * DA_WIDTH
COL_QR = 3 * DA_WIDTH
COL_KR = COL_QR + RET_WIDTH
COL_VR = COL_KR + RET_WIDTH
COL_GB = COL_VR + RET_WIDTH
COL_GATE_A = COL_GB + RET_WIDTH
COL_GATE_B = COL_GATE_A + D_MODEL

PROJ_ROWS = 512
PROJ_CHUNK = 256
ATT_TILE = 512
ATT_ROWS = 32
RET_CHUNK = 256
RET_UNROLL = 8
MOE_BLOCK = 512
PACKED = D_MODEL // 2
ROW_PIECES = 2
PIECE = PACKED // ROW_PIECES
SC_CORES = 2
SC_WINDOW = 128
VMEM_LIMIT = 56 * 1024 * 1024


def _dot(a, b):
    return jnp.dot(a, b, preferred_element_type=jnp.float32)


def _dot_nt(a, b):
    return lax.dot_general(a, b, (((1,), (1,)), ((), ())), preferred_element_type=jnp.float32)


def _dot_tn(a, b):
    return lax.dot_general(a, b, (((0,), (0,)), ((), ())), preferred_element_type=jnp.float32)


def _sigmoid(x):
    return 0.5 * jnp.tanh(0.5 * x) + 0.5


def _split3(x):
    a = x.astype(jnp.bfloat16)
    r = x - a.astype(jnp.float32)
    b = r.astype(jnp.bfloat16)
    c = (r - b.astype(jnp.float32)).astype(jnp.bfloat16)
    return a, b, c


def _in_proj_kernel(x_ref, pos_ref, g1_ref, w_ref, gsum_ref, gq_ref, gk_ref, fa_ref, fr_ref, sel_ref,
                    o_ref, h_scr):
    x = x_ref[...]
    h = x * lax.rsqrt(jnp.mean(x * x, axis=-1, keepdims=True) + EPS) * g1_ref[...]
    h_scr[...] = h.astype(jnp.bfloat16)
    rows = x.shape[0]
    pos = pos_ref[...].astype(jnp.float32)

    lane = lax.broadcasted_iota(jnp.int32, (rows, LANES), 1)
    half_a = ROPE_DIM // 2
    ang_a = fa_ref[...] * pos
    pad = jnp.zeros((LANES - 2 * half_a, rows), jnp.float32)
    t_a = jnp.concatenate([jnp.cos(ang_a), jnp.sin(ang_a), pad], axis=0).T
    tab = sum(_dot(part, sel_ref[...]) for part in _split3(t_a))
    lane64 = lane % DA_HALF
    c_a = tab[:, :LANES] + jnp.where(lane64 < ROPE_DIM, 0.0, 1.0)
    s_lo = tab[:, LANES:2 * LANES]
    s_hi = tab[:, 2 * LANES:]
    c_a2 = jnp.concatenate([c_a, c_a], axis=1)
    s_lo2 = jnp.concatenate([s_lo, s_lo], axis=1)
    s_hi2 = jnp.concatenate([s_hi, s_hi], axis=1)
    ang_r = fr_ref[...] * pos
    t_r = jnp.concatenate([jnp.cos(ang_r), jnp.sin(ang_r)], axis=0).T
    sw_r = pltpu.roll(t_r, RET_KDIM // 2, axis=1)
    first = lane < RET_KDIM // 2
    c_r = jnp.where(first, t_r, sw_r)
    s_r = jnp.where(first, -sw_r, t_r)
    c_r2 = jnp.concatenate([c_r, c_r], axis=1)
    s_r2 = jnp.concatenate([s_r, s_r], axis=1)

    def qk_norm_rope(y, g, scale):
        ss = y * y
        hi = ss.astype(jnp.bfloat16)
        lo = (ss - hi.astype(jnp.float32)).astype(jnp.bfloat16)
        gs = _dot(hi, gsum_ref[...]) + _dot(lo, gsum_ref[...])
        n = y * lax.rsqrt(gs * (1.0 / DA_HALF) + EPS) * g
        up = pltpu.roll(n, PROJ_CHUNK - half_a, axis=1)
        dn = pltpu.roll(n, half_a, axis=1)
        r = n * c_a2 + up * s_lo2 + dn * s_hi2
        return r * scale if scale != 1.0 else r

    def ret_rope(y, scale):
        halves = [pltpu.roll(y[:, i * LANES:(i + 1) * LANES], RET_KDIM // 2, axis=1)
                  for i in range(PROJ_CHUNK // LANES)]
        sw = jnp.concatenate(halves, axis=1)
        r = y * c_r2 + sw * s_r2
        return r * scale if scale != 1.0 else r

    for c in range(IN_COLS // PROJ_CHUNK):
        c0 = c * PROJ_CHUNK
        y = _dot(h_scr[...], w_ref[:, c0:c0 + PROJ_CHUNK])
        if c0 < COL_KA:
            y = qk_norm_rope(y, gq_ref[...], DA_HALF ** -0.5)
        elif c0 < COL_VA:
            y = qk_norm_rope(y, gk_ref[...], 1.0)
        elif c0 < COL_QR:
            pass
        elif c0 < COL_KR:
            y = ret_rope(y, 1.0)
        elif c0 < COL_VR:
            y = ret_rope(y, RET_KDIM ** -0.5)
        elif c0 < COL_GB:
            pass
        elif c0 < COL_GATE_A:
            y = y * _sigmoid(y)
        else:
            y = _sigmoid(y)
        o_ref[:, c0:c0 + PROJ_CHUNK] = y.astype(o_ref.dtype)


def _in_proj(x2, pos2, g1, w_in, gq, gk):
    n = x2.shape[0]
    tm = min(PROJ_ROWS, n)
    grp = jnp.arange(PROJ_CHUNK) // DA_HALF
    gsum = (grp[:, None] == grp[None, :]).astype(jnp.bfloat16)
    half_a = ROPE_DIM // 2
    fa = jnp.power(jnp.float32(ROPE_THETA), -2.0 * jnp.arange(half_a, dtype=jnp.float32) / ROPE_DIM)[:, None]
    half_r = RET_KDIM // 2
    fr = jnp.power(jnp.float32(RET_THETA), -2.0 * jnp.arange(half_r, dtype=jnp.float32) / RET_KDIM)[:, None]
    j = jnp.arange(LANES)[:, None]
    l64 = (jnp.arange(LANES) % DA_HALF)[None, :]
    sel_c = (j < half_a) & (l64 < ROPE_DIM) & (l64 % half_a == j)
    sel_lo = (j >= half_a) & (j < ROPE_DIM) & (l64 < half_a) & (l64 == j - half_a)
    sel_hi = (j >= half_a) & (j < ROPE_DIM) & (l64 >= half_a) & (l64 < ROPE_DIM) & (l64 == j)
    sel = jnp.concatenate([sel_c.astype(jnp.float32), -sel_lo.astype(jnp.float32),
                           sel_hi.astype(jnp.float32)], axis=1).astype(jnp.bfloat16)
    reps = PROJ_CHUNK // DA_HALF
    full = lambda shape: pl.BlockSpec(shape, lambda i: (0,) * len(shape))
    return pl.pallas_call(
        _in_proj_kernel,
        grid=(n // tm,),
        in_specs=[
            pl.BlockSpec((tm, D_MODEL), lambda i: (i, 0)),
            pl.BlockSpec((1, tm), lambda i: (0, i)),
            full((1, D_MODEL)),
            full((D_MODEL, IN_COLS)),
            full((PROJ_CHUNK, PROJ_CHUNK)),
            full((1, PROJ_CHUNK)),
            full((1, PROJ_CHUNK)),
            full((half_a, 1)),
            full((half_r, 1)),
            full((LANES, 3 * LANES)),
        ],
        out_specs=pl.BlockSpec((tm, IN_COLS), lambda i: (i, 0)),
        out_shape=jax.ShapeDtypeStruct((n, IN_COLS), jnp.bfloat16),
        scratch_shapes=[pltpu.VMEM((tm, D_MODEL), jnp.bfloat16)],
        compiler_params=pltpu.CompilerParams(dimension_semantics=("arbitrary",),
                                             vmem_limit_bytes=VMEM_LIMIT),
        name="in_proj",
    )(x2, pos2, g1.reshape(1, D_MODEL), w_in.astype(jnp.bfloat16), gsum,
      jnp.tile(gq, reps)[None, :], jnp.tile(gk, reps)[None, :], fa, fr, sel)


def _diff_attn_kernel(q_ref, k_ref, v_ref, lam_ref, gsub_ref, o_ref,
                      qs_scr, vx_scr, s0_scr, s1_scr, p_scr, m_scr, alpha_scr, acc_scr):
    i = pl.program_id(2)
    t = q_ref.shape[0]

    @pl.when(i == 0)
    def _():
        vx_scr[:, :DA_VDIM] = v_ref[...]
        vx_scr[:, DA_VDIM:] = jnp.ones((vx_scr.shape[0], LANES), vx_scr.dtype)

    q = q_ref[...]
    lane = lax.broadcasted_iota(jnp.int32, q.shape, 1)
    zero = jnp.zeros_like(q)
    qs_scr[:t] = jnp.where(lane < DA_HALF, q, zero)
    qs_scr[t:] = jnp.where(lane >= DA_HALF, q, zero)
    m_scr[...] = jnp.full(m_scr.shape, -jnp.inf, jnp.float32)
    acc_scr[...] = jnp.zeros(acc_scr.shape, jnp.float32)

    def scores(j, s_ref):
        start = pl.multiple_of(j * t, t)
        s_ref[...] = _dot_nt(qs_scr[...], k_ref[pl.ds(start, t), :])

    def softmax_pv(j, s_ref, masked):
        for c in range(2 * t // ATT_ROWS):
            rows = pl.ds(c * ATT_ROWS, ATT_ROWS)
            s = s_ref[rows, :]
            if masked:
                r = lax.broadcasted_iota(jnp.int32, s.shape, 0) + (c * ATT_ROWS) % t
                col = lax.broadcasted_iota(jnp.int32, s.shape, 1)
                s = jnp.where(col <= r, s, -jnp.inf)
            m_prev = m_scr[rows, :]
            m_new = jnp.maximum(m_prev, jnp.max(s, axis=-1, keepdims=True))
            alpha_scr[rows, :] = jnp.exp(m_prev - m_new)
            m_scr[rows, :] = m_new
            p = jnp.exp(s - jnp.concatenate([m_new] * (t // LANES), axis=1))
            p_scr[rows, :] = p.astype(p_scr.dtype)
        start = pl.multiple_of(j * t, t)
        pv = _dot(p_scr[...], vx_scr[pl.ds(start, t), :])
        alpha = alpha_scr[...]
        for half in range(2):
            cols = pl.ds(half * LANES, LANES)
            acc_scr[:, cols] = alpha * acc_scr[:, cols] + pv[:, half * LANES:(half + 1) * LANES]

    scores(0, s0_scr)

    def pair(jj, carry):
        j = 2 * jj
        scores(j + 1, s1_scr)
        softmax_pv(j, s0_scr, False)
        scores(j + 2, s0_scr)
        softmax_pv(j + 1, s1_scr, False)
        return carry

    lax.fori_loop(0, i // 2, pair, 0)

    @pl.when(i % 2 == 1)
    def _():
        scores(i, s1_scr)
        softmax_pv(i - 1, s0_scr, False)
        softmax_pv(i, s1_scr, True)

    @pl.when(i % 2 == 0)
    def _():
        softmax_pv(i, s0_scr, True)

    lam4 = lam_ref[...]
    lam = (jnp.exp(jnp.sum(lam4[0:1] * lam4[1:2], axis=-1, keepdims=True))
           - jnp.exp(jnp.sum(lam4[2:3] * lam4[3:4], axis=-1, keepdims=True)) + LAMBDA_INIT)
    o_all = acc_scr[:, :DA_VDIM] / acc_scr[:, DA_VDIM:]
    o = o_all[:t] - lam * o_all[t:]
    o = o * lax.rsqrt(jnp.mean(o * o, axis=-1, keepdims=True) + EPS) * gsub_ref[...] * (1.0 - LAMBDA_INIT)
    o_ref[...] = o.astype(o_ref.dtype)


def _diff_attn(proj, lam4, gsub, batch, seq):
    n = proj.shape[0]
    t = min(ATT_TILE, seq)
    nq = seq // t
    qb, kb, vb = COL_QA // LANES, COL_KA // LANES, COL_VA // LANES
    return pl.pallas_call(
        _diff_attn_kernel,
        grid=(batch, DA_HEADS, nq),
        in_specs=[
            pl.BlockSpec((t, LANES), lambda b, h, i: (b * nq + i, qb + h)),
            pl.BlockSpec((seq, LANES), lambda b, h, i: (b, kb + h)),
            pl.BlockSpec((seq, LANES), lambda b, h, i: (b, vb + h)),
            pl.BlockSpec((4, LANES), lambda b, h, i: (0, 0)),
            pl.BlockSpec((1, LANES), lambda b, h, i: (0, 0)),
        ],
        out_specs=pl.BlockSpec((t, LANES), lambda b, h, i: (b * nq + i, h)),
        out_shape=jax.ShapeDtypeStruct((n, DA_WIDTH), jnp.bfloat16),
        scratch_shapes=[pltpu.VMEM((2 * t, LANES), jnp.bfloat16),
                        pltpu.VMEM((seq, DA_VDIM + LANES), jnp.bfloat16),
                        pltpu.VMEM((2 * t, t), jnp.float32),
                        pltpu.VMEM((2 * t, t), jnp.float32),
                        pltpu.VMEM((2 * t, t), jnp.bfloat16),
                        pltpu.VMEM((2 * t, LANES), jnp.float32),
                        pltpu.VMEM((2 * t, LANES), jnp.float32),
                        pltpu.VMEM((2 * t, DA_VDIM + LANES), jnp.float32)],
        compiler_params=pltpu.CompilerParams(dimension_semantics=("arbitrary",) * 3,
                                             vmem_limit_bytes=VMEM_LIMIT),
        name="diff_attn",
    )(proj, proj, proj, lam4, gsub)


def _retention_kernel(q_ref, k_ref, v_ref, g_ref, gng_ref, gnb_ref, o_ref, r_scr, *, chunk):
    hf = jnp.full((1, 1), pl.program_id(1), jnp.int32).astype(jnp.float32)
    log_g = jnp.log1p(-jnp.exp2(-5.0 - hf))
    ri = lax.broadcasted_iota(jnp.int32, (chunk, chunk), 0)
    ci = lax.broadcasted_iota(jnp.int32, (chunk, chunk), 1)
    rel = (ri - ci).astype(jnp.float32)
    dmask = jnp.where(rel >= 0, jnp.exp(jnp.maximum(rel, 0.0) * log_g), 0.0)
    idx = lax.broadcasted_iota(jnp.int32, (chunk, 1), 0).astype(jnp.float32)
    zeta = jnp.exp((chunk - 1 - idx) * log_g)
    xi = jnp.exp((idx + 1.0) * log_g)
    g_chunk = jnp.exp(chunk * log_g)
    r_scr[...] = jnp.zeros(r_scr.shape, jnp.float32)
    gng = gng_ref[...]
    gnb = gnb_ref[...]

    def body(n, carry):
        start = pl.multiple_of(n * chunk, chunk)
        q = q_ref[pl.ds(start, chunk), :]
        k = k_ref[pl.ds(start, chunk), :]
        v = v_ref[pl.ds(start, chunk), :]
        s = _dot_nt(q, k) * dmask
        r_old = r_scr[...]
        o = _dot(s.astype(jnp.bfloat16), v) + xi * _dot(q, r_old.astype(jnp.bfloat16))
        kz = (k.astype(jnp.float32) * zeta).astype(jnp.bfloat16)
        r_scr[...] = g_chunk * r_old + _dot_tn(kz, v)
        mu = jnp.mean(o, axis=-1, keepdims=True)
        d = o - mu
        var = jnp.mean(d * d, axis=-1, keepdims=True)
        y = d * lax.rsqrt(var + EPS) * gng + gnb
        y = y * g_ref[pl.ds(start, chunk), :].astype(jnp.float32)
        o_ref[pl.ds(start, chunk), :] = y.astype(o_ref.dtype)
        return carry

    lax.fori_loop(0, q_ref.shape[0] // chunk, body, 0, unroll=RET_UNROLL)


def _retention(proj, gn_g, gn_b, batch, seq):
    n = proj.shape[0]
    chunk = min(RET_CHUNK, seq)
    col = lambda c0: (lambda b, h: (b, c0 // LANES + h))
    return pl.pallas_call(
        functools.partial(_retention_kernel, chunk=chunk),
        grid=(batch, RET_HEADS),
        in_specs=[
            pl.BlockSpec((seq, LANES), col(COL_QR)),
            pl.BlockSpec((seq, LANES), col(COL_KR)),
            pl.BlockSpec((seq, LANES), col(COL_VR)),
            pl.BlockSpec((seq, LANES), col(COL_GB)),
            pl.BlockSpec((1, LANES), lambda b, h: (0, h)),
            pl.BlockSpec((1, LANES), lambda b, h: (0, h)),
        ],
        out_specs=pl.BlockSpec((seq, LANES), lambda b, h: (b, h)),
        out_shape=jax.ShapeDtypeStruct((n, RET_WIDTH), jnp.bfloat16),
        scratch_shapes=[pltpu.VMEM((RET_KDIM, RET_VDIM), jnp.float32)],
        compiler_params=pltpu.CompilerParams(dimension_semantics=("arbitrary",) * 2,
                                             vmem_limit_bytes=VMEM_LIMIT),
        name="retention",
    )(proj, proj, proj, proj, gn_g.reshape(1, RET_WIDTH), gn_b.reshape(1, RET_WIDTH))


def _merge_kernel(x_ref, oa_ref, ob_ref, sa0_ref, sa1_ref, sb0_ref, sb1_ref, wa_ref, wb_ref, wo_ref,
                  g2_ref, wr_hi_ref, wr_lo_ref, br_ref, tri_ref, x1_ref, h2_ref, route_ref, counts_ref,
                  base_scr):
    ya = _dot(oa_ref[...], wa_ref[...])
    yb = _dot(ob_ref[...], wb_ref[...])
    sa = jnp.concatenate([sa0_ref[...], sa1_ref[...]], axis=1).astype(jnp.float32)
    sb = jnp.concatenate([sb0_ref[...], sb1_ref[...]], axis=1).astype(jnp.float32)
    merged = sa * ya + sb * yb
    x1 = x_ref[...] + _dot(merged.astype(jnp.bfloat16), wo_ref[...])
    x1_ref[...] = x1
    h2 = x1 * lax.rsqrt(jnp.mean(x1 * x1, axis=-1, keepdims=True) + EPS) * g2_ref[...]
    _store_packed(h2_ref, h2)

    hi = h2.astype(jnp.bfloat16)
    lo = (h2 - hi.astype(jnp.float32)).astype(jnp.bfloat16)
    logits = (_dot(hi, wr_hi_ref[...]) + _dot(lo, wr_hi_ref[...]) + _dot(hi, wr_lo_ref[...])
              + br_ref[...])
    lane = lax.broadcasted_iota(jnp.int32, logits.shape, 1)
    neg = -jnp.inf
    gl = jnp.where(lane < N_GROUPS, logits, neg)
    gmax = jnp.max(gl, axis=-1, keepdims=True)
    g_idx = jnp.min(jnp.where(gl == gmax, lane, LANES), axis=-1, keepdims=True)
    p_g = 1.0 / jnp.sum(jnp.exp(gl - gmax), axis=-1, keepdims=True)
    e_lo = N_GROUPS + EXPERTS_PER_GROUP * g_idx
    el = jnp.where((lane >= e_lo) & (lane < e_lo + EXPERTS_PER_GROUP), logits, neg)
    v1 = jnp.max(el, axis=-1, keepdims=True)
    i1 = jnp.min(jnp.where(el == v1, lane, LANES), axis=-1, keepdims=True)
    el2 = jnp.where(lane == i1, neg, el)
    v2 = jnp.max(el2, axis=-1, keepdims=True)
    i2 = jnp.min(jnp.where(el2 == v2, lane, LANES), axis=-1, keepdims=True)
    t = jnp.exp(v2 - v1)
    w1 = p_g / (1.0 + t)
    w2 = p_g * t / (1.0 + t)
    e1 = i1 - N_GROUPS
    e2 = i2 - N_GROUPS

    @pl.when(pl.program_id(0) == 0)
    def _():
        base_scr[...] = jnp.zeros(base_scr.shape, jnp.float32)

    oh1 = lane == e1
    oh2 = lane == e2
    picked = jnp.where(oh1 | oh2, 1.0, 0.0)
    before = _dot(tri_ref[...], picked.astype(jnp.bfloat16)) + base_scr[0:1, :]
    rank1 = jnp.sum(jnp.where(oh1, before, 0.0), axis=-1, keepdims=True)
    rank2 = jnp.sum(jnp.where(oh2, before, 0.0), axis=-1, keepdims=True)
    base_scr[...] = base_scr[...] + jnp.sum(picked, axis=0, keepdims=True)
    counts_ref[...] = base_scr[...]

    cols = [e1.astype(jnp.float32), e2.astype(jnp.float32), w1, w2, rank1, rank2]
    route = jnp.zeros(logits.shape, jnp.float32)
    for c, val in enumerate(cols):
        route = jnp.where(lane == c, val, route)
    route_ref[...] = route


def _merge(x2, oa, ob, proj, wa, wb, wo, g2, w_gr, b_gr, w_er, b_er):
    n = x2.shape[0]
    tm = min(PROJ_ROWS, n)
    half = D_MODEL // 2
    wr = jnp.zeros((D_MODEL, LANES), jnp.float32)
    wr = wr.at[:, :N_GROUPS].set(w_gr).at[:, N_GROUPS:N_GROUPS + N_EXPERTS].set(w_er)
    wr_hi = wr.astype(jnp.bfloat16)
    wr_lo = (wr - wr_hi.astype(jnp.float32)).astype(jnp.bfloat16)
    br = jnp.zeros((1, LANES), jnp.float32)
    br = br.at[0, :N_GROUPS].set(b_gr).at[0, N_GROUPS:N_GROUPS + N_EXPERTS].set(b_er)
    tri = (jnp.arange(tm)[:, None] > jnp.arange(tm)[None, :]).astype(jnp.bfloat16)
    full = lambda shape: pl.BlockSpec(shape, lambda i: (0,) * len(shape))
    gate = lambda c0: pl.BlockSpec((tm, half), lambda i: (i, c0 // half))
    return pl.pallas_call(
        _merge_kernel,
        grid=(n // tm,),
        in_specs=[
            pl.BlockSpec((tm, D_MODEL), lambda i: (i, 0)),
            pl.BlockSpec((tm, DA_WIDTH), lambda i: (i, 0)),
            pl.BlockSpec((tm, RET_WIDTH), lambda i: (i, 0)),
            gate(COL_GATE_A), gate(COL_GATE_A + half), gate(COL_GATE_B), gate(COL_GATE_B + half),
            full((DA_WIDTH, D_MODEL)), full((RET_WIDTH, D_MODEL)), full((D_MODEL, D_MODEL)),
            full((1, D_MODEL)), full((D_MODEL, LANES)), full((D_MODEL, LANES)), full((1, LANES)),
            full((tm, tm)),
        ],
        out_specs=[
            pl.BlockSpec((tm, D_MODEL), lambda i: (i, 0)),
            pl.BlockSpec((ROW_PIECES, tm, PIECE), lambda i: (0, i, 0)),
            pl.BlockSpec((tm, LANES), lambda i: (i, 0)),
            pl.BlockSpec((8, LANES), lambda i: (0, 0)),
        ],
        out_shape=[
            jax.ShapeDtypeStruct((n, D_MODEL), jnp.float32),
            jax.ShapeDtypeStruct((ROW_PIECES, n, PIECE), jnp.uint32),
            jax.ShapeDtypeStruct((n, LANES), jnp.float32),
            jax.ShapeDtypeStruct((8, LANES), jnp.float32),
        ],
        scratch_shapes=[pltpu.VMEM((8, LANES), jnp.float32)],
        compiler_params=pltpu.CompilerParams(dimension_semantics=("arbitrary",),
                                             vmem_limit_bytes=VMEM_LIMIT),
        name="merge",
    )(x2, oa, ob, proj, proj, proj, proj, wa.astype(jnp.bfloat16), wb.astype(jnp.bfloat16),
      wo.astype(jnp.bfloat16), g2.reshape(1, D_MODEL), wr_hi, wr_lo, br, tri)


def _sc_mesh():
    return plsc.VectorSubcoreMesh(core_axis_name="c", subcore_axis_name="s")


def _sc_scatter_rows(src, idx, out_rows, src_block):
    steps = idx.shape[1] // SC_WINDOW
    per_core = steps // SC_CORES

    @pl.kernel(out_type=jax.ShapeDtypeStruct((out_rows, PIECE), src.dtype), mesh=_sc_mesh())
    def scatter(src_hbm, idx_hbm, out_hbm):
        def body(src_vmem, idx_vmem):
            pltpu.sync_copy(src_vmem, out_hbm.at[idx_vmem.at[0]])

        pltpu.emit_pipeline(
            body,
            grid=(SC_CORES, per_core),
            in_specs=[pl.BlockSpec((SC_WINDOW, PIECE), lambda c, i: (src_block(c * per_core + i), 0)),
                      pl.BlockSpec((1, SC_WINDOW), lambda c, i: (0, c * per_core + i))],
            out_specs=[],
            core_axis_name=("c", "s"),
            dimension_semantics=(pltpu.PARALLEL, pltpu.PARALLEL),
        )(src_hbm, idx_hbm)

    return scatter(src, idx)


def _sc_gather_rows(table, idx):
    num = idx.shape[1]
    per_core = num // SC_WINDOW // SC_CORES

    @pl.kernel(out_type=jax.ShapeDtypeStruct((num, PIECE), table.dtype), mesh=_sc_mesh())
    def gather(table_hbm, idx_hbm, out_hbm):
        def body(idx_vmem, out_vmem):
            pltpu.sync_copy(table_hbm.at[idx_vmem.at[0]], out_vmem)

        pltpu.emit_pipeline(
            body,
            grid=(SC_CORES, per_core),
            in_specs=[pl.BlockSpec((1, SC_WINDOW), lambda c, i: (0, c * per_core + i))],
            out_specs=[pl.BlockSpec((SC_WINDOW, PIECE), lambda c, i: (c * per_core + i, 0))],
            core_axis_name=("c", "s"),
            dimension_semantics=(pltpu.PARALLEL, pltpu.PARALLEL),
        )(idx_hbm, out_hbm)

    return gather(table, idx)


def _store_packed(ref, val):
    as_bits = lambda v: lax.bitcast_convert_type(v.astype(jnp.bfloat16).astype(jnp.float32), jnp.uint32)
    words = (as_bits(val[:, :PACKED]) >> 16) | (as_bits(val[:, PACKED:]) & jnp.uint32(0xFFFF0000))
    for j in range(ROW_PIECES):
        ref[j] = words[:, j * PIECE:(j + 1) * PIECE]


def _load_packed(ref):
    words = jnp.concatenate([ref[j] for j in range(ROW_PIECES)], axis=1)
    low = lax.bitcast_convert_type(words << 16, jnp.float32)
    high = lax.bitcast_convert_type(words & jnp.uint32(0xFFFF0000), jnp.float32)
    return jnp.concatenate([low, high], axis=1)


def _expert_kernel(blk_e_ref, n_used_ref, x_ref, wg_ref, wu_ref, wd_ref, o_ref):
    i = pl.program_id(0)

    @pl.when(i < n_used_ref[0])
    def _():
        x = _load_packed(x_ref).astype(jnp.bfloat16)
        a = _dot(x, wg_ref[0])
        u = _dot(x, wu_ref[0])
        hmid = (a * _sigmoid(a) * u).astype(jnp.bfloat16)
        _store_packed(o_ref, _dot(hmid, wd_ref[0]))

    @pl.when(i >= n_used_ref[0])
    def _():
        o_ref[...] = jnp.zeros(o_ref.shape, o_ref.dtype)


def _experts(xs, blk_expert, n_used, w_gate, w_up, w_down):
    p = xs.shape[1]
    live = lambda i, be, nu: jnp.minimum(i, nu[0] - 1)
    return pl.pallas_call(
        _expert_kernel,
        grid_spec=pltpu.PrefetchScalarGridSpec(
            num_scalar_prefetch=2,
            grid=(p // MOE_BLOCK,),
            in_specs=[
                pl.BlockSpec((ROW_PIECES, MOE_BLOCK, PIECE), lambda i, be, nu: (0, live(i, be, nu), 0)),
                pl.BlockSpec((1, D_MODEL, EXPERT_FF), lambda i, be, nu: (be[i], 0, 0)),
                pl.BlockSpec((1, D_MODEL, EXPERT_FF), lambda i, be, nu: (be[i], 0, 0)),
                pl.BlockSpec((1, EXPERT_FF, D_MODEL), lambda i, be, nu: (be[i], 0, 0)),
            ],
            out_specs=pl.BlockSpec((ROW_PIECES, MOE_BLOCK, PIECE), lambda i, be, nu: (0, i, 0)),
        ),
        out_shape=jax.ShapeDtypeStruct((ROW_PIECES, p, PIECE), jnp.uint32),
        compiler_params=pltpu.CompilerParams(dimension_semantics=("arbitrary",),
                                             vmem_limit_bytes=VMEM_LIMIT),
        name="experts",
    )(blk_expert, n_used, xs, w_gate.astype(jnp.bfloat16), w_up.astype(jnp.bfloat16),
      w_down.astype(jnp.bfloat16))


def _combine_kernel(x1_ref, route_ref, y0_ref, y1_ref, o_ref):
    route = route_ref[...]
    o_ref[...] = x1_ref[...] + route[:, 2:3] * _load_packed(y0_ref) + route[:, 3:4] * _load_packed(y1_ref)


def _combine(x1, yg, route):
    n = x1.shape[0]
    tm = min(PROJ_ROWS, n)
    return pl.pallas_call(
        _combine_kernel,
        grid=(n // tm,),
        in_specs=[
            pl.BlockSpec((tm, D_MODEL), lambda i: (i, 0)),
            pl.BlockSpec((tm, LANES), lambda i: (i, 0)),
            pl.BlockSpec((ROW_PIECES, tm, PIECE), lambda i: (0, i, 0)),
            pl.BlockSpec((ROW_PIECES, tm, PIECE), lambda i: (0, i + n // tm, 0)),
        ],
        out_specs=pl.BlockSpec((tm, D_MODEL), lambda i: (i, 0)),
        out_shape=jax.ShapeDtypeStruct((n, D_MODEL), jnp.float32),
        compiler_params=pltpu.CompilerParams(dimension_semantics=("arbitrary",),
                                             vmem_limit_bytes=VMEM_LIMIT),
        name="combine",
    )(x1, route, yg, yg)


def _dispatch_plan(route, counts, n):
    counts = counts[0, :N_EXPERTS].astype(jnp.int32)
    padded = ((counts + MOE_BLOCK - 1) // MOE_BLOCK) * MOE_BLOCK
    seg_end = jnp.cumsum(padded).astype(jnp.int32)
    seg_start = seg_end - padded
    e = route[:, 0:TOP_K].astype(jnp.int32)
    rank = route[:, 4:4 + TOP_K].astype(jnp.int32)
    dest = (seg_start[e] + rank).T.reshape(-1)
    p = n * TOP_K + N_EXPERTS * MOE_BLOCK
    slot = (dest[None, :] + (jnp.arange(ROW_PIECES, dtype=jnp.int32) * p)[:, None]).reshape(1, -1)
    blk_start = jnp.arange(p // MOE_BLOCK, dtype=jnp.int32) * MOE_BLOCK
    blk_expert = jnp.sum((seg_end[None, :] <= blk_start[:, None]).astype(jnp.int32), axis=1)
    blk_expert = jnp.minimum(blk_expert, N_EXPERTS - 1)
    n_used = (seg_end[-1] // MOE_BLOCK).reshape(1)
    return slot, blk_expert, n_used, p


def _layer(x, positions, norm1_g, w_in, q_norm_g, k_norm_g, lam4, diff_subln_g, ret_gn_g, ret_gn_b,
           w_branch_a, w_branch_b, w_out, norm2_g, w_gr, b_gr, w_er, b_er, w_gate, w_up, w_down):
    batch, seq, _ = x.shape
    n = batch * seq
    x2 = x.reshape(n, D_MODEL)
    proj = _in_proj(x2, positions.reshape(1, n), norm1_g, w_in, q_norm_g, k_norm_g)
    oa = _diff_attn(proj, lam4, diff_subln_g.reshape(1, DA_VDIM), batch, seq)
    ob = _retention(proj, ret_gn_g, ret_gn_b, batch, seq)
    x1, h2, route, counts = _merge(x2, oa, ob, proj, w_branch_a, w_branch_b, w_out, norm2_g,
                                   w_gr, b_gr, w_er, b_er)
    slot, blk_expert, n_used, p = _dispatch_plan(route, counts, n)
    win_n = n // SC_WINDOW
    src_block = lambda s: (s // (TOP_K * win_n)) * win_n + s % win_n
    xs = _sc_scatter_rows(h2.reshape(ROW_PIECES * n, PIECE), slot, ROW_PIECES * p, src_block)
    ys = _experts(xs.reshape(ROW_PIECES, p, PIECE), blk_expert, n_used, w_gate, w_up, w_down)
    yg = _sc_gather_rows(ys.reshape(ROW_PIECES * p, PIECE), slot)
    out = _combine(x1, yg.reshape(ROW_PIECES, TOP_K * n, PIECE), route)
    return out.reshape(batch, seq, D_MODEL)


def kernel(x, positions, norm1_g, w_in, q_norm_g, k_norm_g, lambda_q1, lambda_k1, lambda_q2, lambda_k2, diff_subln_g, ret_gn_g, ret_gn_b, w_branch_a, w_branch_b, w_out, norm2_g, w_group_router, b_group_router, w_expert_router, b_expert_router, w_gate, w_up, w_down):
    assert x.shape[-1] == D_MODEL and norm1_g.shape[0] == 1, "single-layer, D_MODEL-wide input expected"
    lam4 = jnp.zeros((4, LANES), jnp.float32)
    lam4 = lam4.at[:, :DA_HALF].set(jnp.stack([lambda_q1[0], lambda_k1[0], lambda_q2[0], lambda_k2[0]]))
    return _layer(x, positions, norm1_g[0], w_in[0], q_norm_g[0], k_norm_g[0], lam4, diff_subln_g[0],
                  ret_gn_g[0], ret_gn_b[0], w_branch_a[0], w_branch_b[0], w_out[0], norm2_g[0],
                  w_group_router[0], b_group_router[0], w_expert_router[0], b_expert_router[0],
                  w_gate[0], w_up[0], w_down[0])
```

```python
import functools
import math

import jax
import jax.numpy as jnp
from jax import lax
from jax.experimental import pallas as pl
from jax.experimental.pallas import tpu as pltpu
from jax.experimental.pallas import tpu_sc as plsc

D_MODEL = 1024
DA_HEADS = 4
DA_HALF = 64
DA_VDIM = 2 * DA_HALF
DA_WIDTH = DA_HEADS * DA_VDIM
ROPE_THETA = 500000.0
ROPE_DIM = DA_HALF // 4
RET_HEADS = 4
RET_KDIM = 128
RET_VDIM = 128
RET_WIDTH = RET_HEADS * RET_VDIM
RET_THETA = 10000.0
N_GROUPS = 4
EXPERTS_PER_GROUP = 8
N_EXPERTS = N_GROUPS * EXPERTS_PER_GROUP
TOP_K = 2
EXPERT_FF = 512
EPS = 1e-6
LAMBDA_INIT = 0.8 - 0.6 * math.exp(-0.3 * 0)

LANES = 128
IN_COLS = 3 * DA_WIDTH + 4 * RET_WIDTH + 2 * D_MODEL
COL_QA, COL_KA, COL_VA = 0, DA_WIDTH, 2 * DA_WIDTH
COL_QR = 3 * DA_WIDTH
COL_KR = COL_QR + RET_WIDTH
COL_VR = COL_KR + RET_WIDTH
COL_GB = COL_VR + RET_WIDTH
COL_GATE_A = COL_GB + RET_WIDTH
COL_GATE_B = COL_GATE_A + D_MODEL

PROJ_ROWS = 512
PROJ_CHUNK = 256
ATT_TILE = 512
ATT_ROWS = 32
RET_CHUNK = 256
RET_UNROLL = 8
MOE_BLOCK = 512
PACKED = D_MODEL // 2
ROW_PIECES = 2
PIECE = PACKED // ROW_PIECES
SC_CORES = 2
SC_WINDOW = 128
BATCH_GROUPS = 2
VMEM_LIMIT = 56 * 1024 * 1024


def _dot(a, b):
    return jnp.dot(a, b, preferred_element_type=jnp.float32)


def _dot_nt(a, b):
    return lax.dot_general(a, b, (((1,), (1,)), ((), ())), preferred_element_type=jnp.float32)


def _dot_tn(a, b):
    return lax.dot_general(a, b, (((0,), (0,)), ((), ())), preferred_element_type=jnp.float32)


def _sigmoid(x):
    return 0.5 * jnp.tanh(0.5 * x) + 0.5


def _split3(x):
    a = x.astype(jnp.bfloat16)
    r = x - a.astype(jnp.float32)
    b = r.astype(jnp.bfloat16)
    c = (r - b.astype(jnp.float32)).astype(jnp.bfloat16)
    return a, b, c


def _in_proj_kernel(x_ref, pos_ref, g1_ref, w_ref, gsum_ref, gq_ref, gk_ref, fa_ref, fr_ref, sel_ref,
                    o_ref, h_scr):
    x = x_ref[...]
    h = x * lax.rsqrt(jnp.mean(x * x, axis=-1, keepdims=True) + EPS) * g1_ref[...]
    h_scr[...] = h.astype(jnp.bfloat16)
    rows = x.shape[0]
    pos = pos_ref[...].astype(jnp.float32)

    lane = lax.broadcasted_iota(jnp.int32, (rows, LANES), 1)
    half_a = ROPE_DIM // 2
    ang_a = fa_ref[...] * pos
    pad = jnp.zeros((LANES - 2 * half_a, rows), jnp.float32)
    t_a = jnp.concatenate([jnp.cos(ang_a), jnp.sin(ang_a), pad], axis=0).T
    tab = sum(_dot(part, sel_ref[...]) for part in _split3(t_a))
    lane64 = lane % DA_HALF
    c_a = tab[:, :LANES] + jnp.where(lane64 < ROPE_DIM, 0.0, 1.0)
    s_lo = tab[:, LANES:2 * LANES]
    s_hi = tab[:, 2 * LANES:]
    c_a2 = jnp.concatenate([c_a, c_a], axis=1)
    s_lo2 = jnp.concatenate([s_lo, s_lo], axis=1)
    s_hi2 = jnp.concatenate([s_hi, s_hi], axis=1)
    ang_r = fr_ref[...] * pos
    t_r = jnp.concatenate([jnp.cos(ang_r), jnp.sin(ang_r)], axis=0).T
    sw_r = pltpu.roll(t_r, RET_KDIM // 2, axis=1)
    first = lane < RET_KDIM // 2
    c_r = jnp.where(first, t_r, sw_r)
    s_r = jnp.where(first, -sw_r, t_r)
    c_r2 = jnp.concatenate([c_r, c_r], axis=1)
    s_r2 = jnp.concatenate([s_r, s_r], axis=1)

    def qk_norm_rope(y, g, scale):
        ss = y * y
        hi = ss.astype(jnp.bfloat16)
        lo = (ss - hi.astype(jnp.float32)).astype(jnp.bfloat16)
        gs = _dot(hi, gsum_ref[...]) + _dot(lo, gsum_ref[...])
        n = y * lax.rsqrt(gs * (1.0 / DA_HALF) + EPS) * g
        up = pltpu.roll(n, PROJ_CHUNK - half_a, axis=1)
        dn = pltpu.roll(n, half_a, axis=1)
        r = n * c_a2 + up * s_lo2 + dn * s_hi2
        return r * scale if scale != 1.0 else r

    def ret_rope(y, scale):
        halves = [pltpu.roll(y[:, i * LANES:(i + 1) * LANES], RET_KDIM // 2, axis=1)
                  for i in range(PROJ_CHUNK // LANES)]
        sw = jnp.concatenate(halves, axis=1)
        r = y * c_r2 + sw * s_r2
        return r * scale if scale != 1.0 else r

    for c in range(IN_COLS // PROJ_CHUNK):
        c0 = c * PROJ_CHUNK
        y = _dot(h_scr[...], w_ref[:, c0:c0 + PROJ_CHUNK])
        if c0 < COL_KA:
            y = qk_norm_rope(y, gq_ref[...], DA_HALF ** -0.5)
        elif c0 < COL_VA:
            y = qk_norm_rope(y, gk_ref[...], 1.0)
        elif c0 < COL_QR:
            pass
        elif c0 < COL_KR:
            y = ret_rope(y, 1.0)
        elif c0 < COL_VR:
            y = ret_rope(y, RET_KDIM ** -0.5)
        elif c0 < COL_GB:
            pass
        elif c0 < COL_GATE_A:
            y = y * _sigmoid(y)
        else:
            y = _sigmoid(y)
        o_ref[:, c0:c0 + PROJ_CHUNK] = y.astype(o_ref.dtype)


def _in_proj(x2, pos2, g1, w_in, gq, gk, row0, n):
    tm = min(PROJ_ROWS, n)
    blk0 = row0 // tm
    grp = jnp.arange(PROJ_CHUNK) // DA_HALF
    gsum = (grp[:, None] == grp[None, :]).astype(jnp.bfloat16)
    half_a = ROPE_DIM // 2
    fa = jnp.power(jnp.float32(ROPE_THETA), -2.0 * jnp.arange(half_a, dtype=jnp.float32) / ROPE_DIM)[:, None]
    half_r = RET_KDIM // 2
    fr = jnp.power(jnp.float32(RET_THETA), -2.0 * jnp.arange(half_r, dtype=jnp.float32) / RET_KDIM)[:, None]
    j = jnp.arange(LANES)[:, None]
    l64 = (jnp.arange(LANES) % DA_HALF)[None, :]
    sel_c = (j < half_a) & (l64 < ROPE_DIM) & (l64 % half_a == j)
    sel_lo = (j >= half_a) & (j < ROPE_DIM) & (l64 < half_a) & (l64 == j - half_a)
    sel_hi = (j >= half_a) & (j < ROPE_DIM) & (l64 >= half_a) & (l64 < ROPE_DIM) & (l64 == j)
    sel = jnp.concatenate([sel_c.astype(jnp.float32), -sel_lo.astype(jnp.float32),
                           sel_hi.astype(jnp.float32)], axis=1).astype(jnp.bfloat16)
    reps = PROJ_CHUNK // DA_HALF
    full = lambda shape: pl.BlockSpec(shape, lambda i: (0,) * len(shape))
    return pl.pallas_call(
        _in_proj_kernel,
        grid=(n // tm,),
        in_specs=[
            pl.BlockSpec((tm, D_MODEL), lambda i: (blk0 + i, 0)),
            pl.BlockSpec((1, tm), lambda i: (0, blk0 + i)),
            full((1, D_MODEL)),
            full((D_MODEL, IN_COLS)),
            full((PROJ_CHUNK, PROJ_CHUNK)),
            full((1, PROJ_CHUNK)),
            full((1, PROJ_CHUNK)),
            full((half_a, 1)),
            full((half_r, 1)),
            full((LANES, 3 * LANES)),
        ],
        out_specs=pl.BlockSpec((tm, IN_COLS), lambda i: (i, 0)),
        out_shape=jax.ShapeDtypeStruct((n, IN_COLS), jnp.bfloat16),
        scratch_shapes=[pltpu.VMEM((tm, D_MODEL), jnp.bfloat16)],
        compiler_params=pltpu.CompilerParams(dimension_semantics=("arbitrary",),
                                             vmem_limit_bytes=VMEM_LIMIT),
        name="in_proj",
    )(x2, pos2, g1.reshape(1, D_MODEL), w_in.astype(jnp.bfloat16), gsum,
      jnp.tile(gq, reps)[None, :], jnp.tile(gk, reps)[None, :], fa, fr, sel)


def _diff_attn_kernel(q_ref, k_ref, v_ref, lam_ref, gsub_ref, o_ref,
                      qs_scr, vx_scr, s0_scr, s1_scr, p_scr, m_scr, alpha_scr, acc_scr):
    i = pl.program_id(2)
    t = q_ref.shape[0]

    @pl.when(i == 0)
    def _():
        vx_scr[:, :DA_VDIM] = v_ref[...]
        vx_scr[:, DA_VDIM:] = jnp.ones((vx_scr.shape[0], LANES), vx_scr.dtype)

    q = q_ref[...]
    lane = lax.broadcasted_iota(jnp.int32, q.shape, 1)
    zero = jnp.zeros_like(q)
    qs_scr[:t] = jnp.where(lane < DA_HALF, q, zero)
    qs_scr[t:] = jnp.where(lane >= DA_HALF, q, zero)
    m_scr[...] = jnp.full(m_scr.shape, -jnp.inf, jnp.float32)
    acc_scr[...] = jnp.zeros(acc_scr.shape, jnp.float32)

    def scores(j, s_ref):
        start = pl.multiple_of(j * t, t)
        s_ref[...] = _dot_nt(qs_scr[...], k_ref[pl.ds(start, t), :])

    def softmax_pv(j, s_ref, masked):
        for c in range(2 * t // ATT_ROWS):
            rows = pl.ds(c * ATT_ROWS, ATT_ROWS)
            s = s_ref[rows, :]
            if masked:
                r = lax.broadcasted_iota(jnp.int32, s.shape, 0) + (c * ATT_ROWS) % t
                col = lax.broadcasted_iota(jnp.int32, s.shape, 1)
                s = jnp.where(col <= r, s, -jnp.inf)
            m_prev = m_scr[rows, :]
            m_new = jnp.maximum(m_prev, jnp.max(s, axis=-1, keepdims=True))
            alpha_scr[rows, :] = jnp.exp(m_prev - m_new)
            m_scr[rows, :] = m_new
            p = jnp.exp(s - jnp.concatenate([m_new] * (t // LANES), axis=1))
            p_scr[rows, :] = p.astype(p_scr.dtype)
        start = pl.multiple_of(j * t, t)
        pv = _dot(p_scr[...], vx_scr[pl.ds(start, t), :])
        alpha = alpha_scr[...]
        for half in range(2):
            cols = pl.ds(half * LANES, LANES)
            acc_scr[:, cols] = alpha * acc_scr[:, cols] + pv[:, half * LANES:(half + 1) * LANES]

    scores(0, s0_scr)

    def pair(jj, carry):
        j = 2 * jj
        scores(j + 1, s1_scr)
        softmax_pv(j, s0_scr, False)
        scores(j + 2, s0_scr)
        softmax_pv(j + 1, s1_scr, False)
        return carry

    lax.fori_loop(0, i // 2, pair, 0)

    @pl.when(i % 2 == 1)
    def _():
        scores(i, s1_scr)
        softmax_pv(i - 1, s0_scr, False)
        softmax_pv(i, s1_scr, True)

    @pl.when(i % 2 == 0)
    def _():
        softmax_pv(i, s0_scr, True)

    lam4 = lam_ref[...]
    lam = (jnp.exp(jnp.sum(lam4[0:1] * lam4[1:2], axis=-1, keepdims=True))
           - jnp.exp(jnp.sum(lam4[2:3] * lam4[3:4], axis=-1, keepdims=True)) + LAMBDA_INIT)
    o_all = acc_scr[:, :DA_VDIM] / acc_scr[:, DA_VDIM:]
    o = o_all[:t] - lam * o_all[t:]
    o = o * lax.rsqrt(jnp.mean(o * o, axis=-1, keepdims=True) + EPS) * gsub_ref[...] * (1.0 - LAMBDA_INIT)
    o_ref[...] = o.astype(o_ref.dtype)


def _diff_attn(proj, lam4, gsub, batch, seq):
    n = proj.shape[0]
    t = min(ATT_TILE, seq)
    nq = seq // t
    qb, kb, vb = COL_QA // LANES, COL_KA // LANES, COL_VA // LANES
    return pl.pallas_call(
        _diff_attn_kernel,
        grid=(batch, DA_HEADS, nq),
        in_specs=[
            pl.BlockSpec((t, LANES), lambda b, h, i: (b * nq + i, qb + h)),
            pl.BlockSpec((seq, LANES), lambda b, h, i: (b, kb + h)),
            pl.BlockSpec((seq, LANES), lambda b, h, i: (b, vb + h)),
            pl.BlockSpec((4, LANES), lambda b, h, i: (0, 0)),
            pl.BlockSpec((1, LANES), lambda b, h, i: (0, 0)),
        ],
        out_specs=pl.BlockSpec((t, LANES), lambda b, h, i: (b * nq + i, h)),
        out_shape=jax.ShapeDtypeStruct((n, DA_WIDTH), jnp.bfloat16),
        scratch_shapes=[pltpu.VMEM((2 * t, LANES), jnp.bfloat16),
                        pltpu.VMEM((seq, DA_VDIM + LANES), jnp.bfloat16),
                        pltpu.VMEM((2 * t, t), jnp.float32),
                        pltpu.VMEM((2 * t, t), jnp.float32),
                        pltpu.VMEM((2 * t, t), jnp.bfloat16),
                        pltpu.VMEM((2 * t, LANES), jnp.float32),
                        pltpu.VMEM((2 * t, LANES), jnp.float32),
                        pltpu.VMEM((2 * t, DA_VDIM + LANES), jnp.float32)],
        compiler_params=pltpu.CompilerParams(dimension_semantics=("arbitrary",) * 3,
                                             vmem_limit_bytes=VMEM_LIMIT),
        name="diff_attn",
    )(proj, proj, proj, lam4, gsub)


def _retention_kernel(q_ref, k_ref, v_ref, g_ref, gng_ref, gnb_ref, o_ref, r_scr, *, chunk):
    hf = jnp.full((1, 1), pl.program_id(1), jnp.int32).astype(jnp.float32)
    log_g = jnp.log1p(-jnp.exp2(-5.0 - hf))
    ri = lax.broadcasted_iota(jnp.int32, (chunk, chunk), 0)
    ci = lax.broadcasted_iota(jnp.int32, (chunk, chunk), 1)
    rel = (ri - ci).astype(jnp.float32)
    dmask = jnp.where(rel >= 0, jnp.exp(jnp.maximum(rel, 0.0) * log_g), 0.0)
    idx = lax.broadcasted_iota(jnp.int32, (chunk, 1), 0).astype(jnp.float32)
    zeta = jnp.exp((chunk - 1 - idx) * log_g)
    xi = jnp.exp((idx + 1.0) * log_g)
    g_chunk = jnp.exp(chunk * log_g)
    r_scr[...] = jnp.zeros(r_scr.shape, jnp.float32)
    gng = gng_ref[...]
    gnb = gnb_ref[...]

    def body(n, carry):
        start = pl.multiple_of(n * chunk, chunk)
        q = q_ref[pl.ds(start, chunk), :]
        k = k_ref[pl.ds(start, chunk), :]
        v = v_ref[pl.ds(start, chunk), :]
        s = _dot_nt(q, k) * dmask
        r_old = r_scr[...]
        o = _dot(s.astype(jnp.bfloat16), v) + xi * _dot(q, r_old.astype(jnp.bfloat16))
        kz = (k.astype(jnp.float32) * zeta).astype(jnp.bfloat16)
        r_scr[...] = g_chunk * r_old + _dot_tn(kz, v)
        mu = jnp.mean(o, axis=-1, keepdims=True)
        d = o - mu
        var = jnp.mean(d * d, axis=-1, keepdims=True)
        y = d * lax.rsqrt(var + EPS) * gng + gnb
        y = y * g_ref[pl.ds(start, chunk), :].astype(jnp.float32)
        o_ref[pl.ds(start, chunk), :] = y.astype(o_ref.dtype)
        return carry

    lax.fori_loop(0, q_ref.shape[0] // chunk, body, 0, unroll=RET_UNROLL)


def _retention(proj, gn_g, gn_b, batch, seq):
    n = proj.shape[0]
    chunk = min(RET_CHUNK, seq)
    col = lambda c0: (lambda b, h: (b, c0 // LANES + h))
    return pl.pallas_call(
        functools.partial(_retention_kernel, chunk=chunk),
        grid=(batch, RET_HEADS),
        in_specs=[
            pl.BlockSpec((seq, LANES), col(COL_QR)),
            pl.BlockSpec((seq, LANES), col(COL_KR)),
            pl.BlockSpec((seq, LANES), col(COL_VR)),
            pl.BlockSpec((seq, LANES), col(COL_GB)),
            pl.BlockSpec((1, LANES), lambda b, h: (0, h)),
            pl.BlockSpec((1, LANES), lambda b, h: (0, h)),
        ],
        out_specs=pl.BlockSpec((seq, LANES), lambda b, h: (b, h)),
        out_shape=jax.ShapeDtypeStruct((n, RET_WIDTH), jnp.bfloat16),
        scratch_shapes=[pltpu.VMEM((RET_KDIM, RET_VDIM), jnp.float32)],
        compiler_params=pltpu.CompilerParams(dimension_semantics=("arbitrary",) * 2,
                                             vmem_limit_bytes=VMEM_LIMIT),
        name="retention",
    )(proj, proj, proj, proj, gn_g.reshape(1, RET_WIDTH), gn_b.reshape(1, RET_WIDTH))


def _merge_kernel(x_ref, oa_ref, ob_ref, sa0_ref, sa1_ref, sb0_ref, sb1_ref, wa_ref, wb_ref, wo_ref,
                  g2_ref, wr_hi_ref, wr_lo_ref, br_ref, tri_ref, x1_ref, h2_ref, route_ref, counts_ref,
                  base_scr):
    ya = _dot(oa_ref[...], wa_ref[...])
    yb = _dot(ob_ref[...], wb_ref[...])
    sa = jnp.concatenate([sa0_ref[...], sa1_ref[...]], axis=1).astype(jnp.float32)
    sb = jnp.concatenate([sb0_ref[...], sb1_ref[...]], axis=1).astype(jnp.float32)
    merged = sa * ya + sb * yb
    x1 = x_ref[...] + _dot(merged.astype(jnp.bfloat16), wo_ref[...])
    x1_ref[...] = x1
    h2 = x1 * lax.rsqrt(jnp.mean(x1 * x1, axis=-1, keepdims=True) + EPS) * g2_ref[...]
    _store_packed(h2_ref, h2)

    hi = h2.astype(jnp.bfloat16)
    lo = (h2 - hi.astype(jnp.float32)).astype(jnp.bfloat16)
    logits = (_dot(hi, wr_hi_ref[...]) + _dot(lo, wr_hi_ref[...]) + _dot(hi, wr_lo_ref[...])
              + br_ref[...])
    lane = lax.broadcasted_iota(jnp.int32, logits.shape, 1)
    neg = -jnp.inf
    gl = jnp.where(lane < N_GROUPS, logits, neg)
    gmax = jnp.max(gl, axis=-1, keepdims=True)
    g_idx = jnp.min(jnp.where(gl == gmax, lane, LANES), axis=-1, keepdims=True)
    p_g = 1.0 / jnp.sum(jnp.exp(gl - gmax), axis=-1, keepdims=True)
    e_lo = N_GROUPS + EXPERTS_PER_GROUP * g_idx
    el = jnp.where((lane >= e_lo) & (lane < e_lo + EXPERTS_PER_GROUP), logits, neg)
    v1 = jnp.max(el, axis=-1, keepdims=True)
    i1 = jnp.min(jnp.where(el == v1, lane, LANES), axis=-1, keepdims=True)
    el2 = jnp.where(lane == i1, neg, el)
    v2 = jnp.max(el2, axis=-1, keepdims=True)
    i2 = jnp.min(jnp.where(el2 == v2, lane, LANES), axis=-1, keepdims=True)
    t = jnp.exp(v2 - v1)
    w1 = p_g / (1.0 + t)
    w2 = p_g * t / (1.0 + t)
    e1 = i1 - N_GROUPS
    e2 = i2 - N_GROUPS

    @pl.when(pl.program_id(0) == 0)
    def _():
        base_scr[...] = jnp.zeros(base_scr.shape, jnp.float32)

    oh1 = lane == e1
    oh2 = lane == e2
    picked = jnp.where(oh1 | oh2, 1.0, 0.0)
    before = _dot(tri_ref[...], picked.astype(jnp.bfloat16)) + base_scr[0:1, :]
    rank1 = jnp.sum(jnp.where(oh1, before, 0.0), axis=-1, keepdims=True)
    rank2 = jnp.sum(jnp.where(oh2, before, 0.0), axis=-1, keepdims=True)
    base_scr[...] = base_scr[...] + jnp.sum(picked, axis=0, keepdims=True)
    counts_ref[...] = base_scr[...]

    cols = [e1.astype(jnp.float32), e2.astype(jnp.float32), w1, w2, rank1, rank2]
    route = jnp.zeros(logits.shape, jnp.float32)
    for c, val in enumerate(cols):
        route = jnp.where(lane == c, val, route)
    route_ref[...] = route


def _merge(x2, oa, ob, proj, wa, wb, wo, g2, w_gr, b_gr, w_er, b_er, row0):
    n = oa.shape[0]
    tm = min(PROJ_ROWS, n)
    blk0 = row0 // tm
    half = D_MODEL // 2
    wr = jnp.zeros((D_MODEL, LANES), jnp.float32)
    wr = wr.at[:, :N_GROUPS].set(w_gr).at[:, N_GROUPS:N_GROUPS + N_EXPERTS].set(w_er)
    wr_hi = wr.astype(jnp.bfloat16)
    wr_lo = (wr - wr_hi.astype(jnp.float32)).astype(jnp.bfloat16)
    br = jnp.zeros((1, LANES), jnp.float32)
    br = br.at[0, :N_GROUPS].set(b_gr).at[0, N_GROUPS:N_GROUPS + N_EXPERTS].set(b_er)
    tri = (jnp.arange(tm)[:, None] > jnp.arange(tm)[None, :]).astype(jnp.bfloat16)
    full = lambda shape: pl.BlockSpec(shape, lambda i: (0,) * len(shape))
    gate = lambda c0: pl.BlockSpec((tm, half), lambda i: (i, c0 // half))
    return pl.pallas_call(
        _merge_kernel,
        grid=(n // tm,),
        in_specs=[
            pl.BlockSpec((tm, D_MODEL), lambda i: (blk0 + i, 0)),
            pl.BlockSpec((tm, DA_WIDTH), lambda i: (i, 0)),
            pl.BlockSpec((tm, RET_WIDTH), lambda i: (i, 0)),
            gate(COL_GATE_A), gate(COL_GATE_A + half), gate(COL_GATE_B), gate(COL_GATE_B + half),
            full((DA_WIDTH, D_MODEL)), full((RET_WIDTH, D_MODEL)), full((D_MODEL, D_MODEL)),
            full((1, D_MODEL)), full((D_MODEL, LANES)), full((D_MODEL, LANES)), full((1, LANES)),
            full((tm, tm)),
        ],
        out_specs=[
            pl.BlockSpec((tm, D_MODEL), lambda i: (i, 0)),
            pl.BlockSpec((ROW_PIECES, tm, PIECE), lambda i: (0, i, 0)),
            pl.BlockSpec((tm, LANES), lambda i: (i, 0)),
            pl.BlockSpec((8, LANES), lambda i: (0, 0)),
        ],
        out_shape=[
            jax.ShapeDtypeStruct((n, D_MODEL), jnp.float32),
            jax.ShapeDtypeStruct((ROW_PIECES, n, PIECE), jnp.uint32),
            jax.ShapeDtypeStruct((n, LANES), jnp.float32),
            jax.ShapeDtypeStruct((8, LANES), jnp.float32),
        ],
        scratch_shapes=[pltpu.VMEM((8, LANES), jnp.float32)],
        compiler_params=pltpu.CompilerParams(dimension_semantics=("arbitrary",),
                                             vmem_limit_bytes=VMEM_LIMIT),
        name="merge",
    )(x2, oa, ob, proj, proj, proj, proj, wa.astype(jnp.bfloat16), wb.astype(jnp.bfloat16),
      wo.astype(jnp.bfloat16), g2.reshape(1, D_MODEL), wr_hi, wr_lo, br, tri)


def _sc_mesh():
    return plsc.VectorSubcoreMesh(core_axis_name="c", subcore_axis_name="s")


def _sc_scatter_rows(src, idx, out_rows, src_block):
    steps = idx.shape[1] // SC_WINDOW
    per_core = steps // SC_CORES

    @pl.kernel(out_type=jax.ShapeDtypeStruct((out_rows, PIECE), src.dtype), mesh=_sc_mesh())
    def scatter(src_hbm, idx_hbm, out_hbm):
        def body(src_vmem, idx_vmem):
            pltpu.sync_copy(src_vmem, out_hbm.at[idx_vmem.at[0]])

        pltpu.emit_pipeline(
            body,
            grid=(SC_CORES, per_core),
            in_specs=[pl.BlockSpec((SC_WINDOW, PIECE), lambda c, i: (src_block(c * per_core + i), 0)),
                      pl.BlockSpec((1, SC_WINDOW), lambda c, i: (0, c * per_core + i))],
            out_specs=[],
            core_axis_name=("c", "s"),
            dimension_semantics=(pltpu.PARALLEL, pltpu.PARALLEL),
        )(src_hbm, idx_hbm)

    return scatter(src, idx)


def _sc_gather_rows(table, idx):
    num = idx.shape[1]
    per_core = num // SC_WINDOW // SC_CORES

    @pl.kernel(out_type=jax.ShapeDtypeStruct((num, PIECE), table.dtype), mesh=_sc_mesh())
    def gather(table_hbm, idx_hbm, out_hbm):
        def body(idx_vmem, out_vmem):
            pltpu.sync_copy(table_hbm.at[idx_vmem.at[0]], out_vmem)

        pltpu.emit_pipeline(
            body,
            grid=(SC_CORES, per_core),
            in_specs=[pl.BlockSpec((1, SC_WINDOW), lambda c, i: (0, c * per_core + i))],
            out_specs=[pl.BlockSpec((SC_WINDOW, PIECE), lambda c, i: (c * per_core + i, 0))],
            core_axis_name=("c", "s"),
            dimension_semantics=(pltpu.PARALLEL, pltpu.PARALLEL),
        )(idx_hbm, out_hbm)

    return gather(table, idx)


def _store_packed(ref, val):
    as_bits = lambda v: lax.bitcast_convert_type(v.astype(jnp.bfloat16).astype(jnp.float32), jnp.uint32)
    words = (as_bits(val[:, :PACKED]) >> 16) | (as_bits(val[:, PACKED:]) & jnp.uint32(0xFFFF0000))
    for j in range(ROW_PIECES):
        ref[j] = words[:, j * PIECE:(j + 1) * PIECE]


def _load_packed(ref):
    words = jnp.concatenate([ref[j] for j in range(ROW_PIECES)], axis=1)
    low = lax.bitcast_convert_type(words << 16, jnp.float32)
    high = lax.bitcast_convert_type(words & jnp.uint32(0xFFFF0000), jnp.float32)
    return jnp.concatenate([low, high], axis=1)


def _expert_kernel(blk_e_ref, n_used_ref, x_ref, wg_ref, wu_ref, wd_ref, o_ref):
    i = pl.program_id(0)

    @pl.when(i < n_used_ref[0])
    def _():
        x = _load_packed(x_ref).astype(jnp.bfloat16)
        a = _dot(x, wg_ref[0])
        u = _dot(x, wu_ref[0])
        hmid = (a * _sigmoid(a) * u).astype(jnp.bfloat16)
        _store_packed(o_ref, _dot(hmid, wd_ref[0]))

    @pl.when(i >= n_used_ref[0])
    def _():
        o_ref[...] = jnp.zeros(o_ref.shape, o_ref.dtype)


def _experts(xs, blk_expert, n_used, w_gate, w_up, w_down):
    p = xs.shape[1]
    live = lambda i, be, nu: jnp.minimum(i, nu[0] - 1)
    return pl.pallas_call(
        _expert_kernel,
        grid_spec=pltpu.PrefetchScalarGridSpec(
            num_scalar_prefetch=2,
            grid=(p // MOE_BLOCK,),
            in_specs=[
                pl.BlockSpec((ROW_PIECES, MOE_BLOCK, PIECE), lambda i, be, nu: (0, live(i, be, nu), 0)),
                pl.BlockSpec((1, D_MODEL, EXPERT_FF), lambda i, be, nu: (be[i], 0, 0)),
                pl.BlockSpec((1, D_MODEL, EXPERT_FF), lambda i, be, nu: (be[i], 0, 0)),
                pl.BlockSpec((1, EXPERT_FF, D_MODEL), lambda i, be, nu: (be[i], 0, 0)),
            ],
            out_specs=pl.BlockSpec((ROW_PIECES, MOE_BLOCK, PIECE), lambda i, be, nu: (0, i, 0)),
        ),
        out_shape=jax.ShapeDtypeStruct((ROW_PIECES, p, PIECE), jnp.uint32),
        compiler_params=pltpu.CompilerParams(dimension_semantics=("arbitrary",),
                                             vmem_limit_bytes=VMEM_LIMIT),
        name="experts",
    )(blk_expert, n_used, xs, w_gate.astype(jnp.bfloat16), w_up.astype(jnp.bfloat16),
      w_down.astype(jnp.bfloat16))


def _combine_kernel(x1_ref, route_ref, y0_ref, y1_ref, *rest):
    o_ref = rest[-1]
    route = route_ref[...]
    o_ref[...] = x1_ref[...] + route[:, 2:3] * _load_packed(y0_ref) + route[:, 3:4] * _load_packed(y1_ref)


def _combine(x1, yg, route, row0, n_total, out_prev):
    n = x1.shape[0]
    tm = min(PROJ_ROWS, n)
    blk0 = row0 // tm
    prev = () if out_prev is None else (out_prev,)
    return pl.pallas_call(
        _combine_kernel,
        grid=(n // tm,),
        in_specs=[
            pl.BlockSpec((tm, D_MODEL), lambda i: (i, 0)),
            pl.BlockSpec((tm, LANES), lambda i: (i, 0)),
            pl.BlockSpec((ROW_PIECES, tm, PIECE), lambda i: (0, i, 0)),
            pl.BlockSpec((ROW_PIECES, tm, PIECE), lambda i: (0, i + n // tm, 0)),
        ] + [pl.BlockSpec(memory_space=pl.ANY)] * len(prev),
        out_specs=pl.BlockSpec((tm, D_MODEL), lambda i: (blk0 + i, 0)),
        out_shape=jax.ShapeDtypeStruct((n_total, D_MODEL), jnp.float32),
        input_output_aliases={4: 0} if prev else {},
        compiler_params=pltpu.CompilerParams(dimension_semantics=("arbitrary",),
                                             vmem_limit_bytes=VMEM_LIMIT),
        name="combine",
    )(x1, route, yg, yg, *prev)


def _dispatch_plan(route, counts, n):
    counts = counts[0, :N_EXPERTS].astype(jnp.int32)
    padded = ((counts + MOE_BLOCK - 1) // MOE_BLOCK) * MOE_BLOCK
    seg_end = jnp.cumsum(padded).astype(jnp.int32)
    seg_start = seg_end - padded
    e = route[:, 0:TOP_K].astype(jnp.int32)
    rank = route[:, 4:4 + TOP_K].astype(jnp.int32)
    dest = (seg_start[e] + rank).T.reshape(-1)
    p = n * TOP_K + N_EXPERTS * MOE_BLOCK
    slot = (dest[None, :] + (jnp.arange(ROW_PIECES, dtype=jnp.int32) * p)[:, None]).reshape(1, -1)
    blk_start = jnp.arange(p // MOE_BLOCK, dtype=jnp.int32) * MOE_BLOCK
    blk_expert = jnp.sum((seg_end[None, :] <= blk_start[:, None]).astype(jnp.int32), axis=1)
    blk_expert = jnp.minimum(blk_expert, N_EXPERTS - 1)
    n_used = (seg_end[-1] // MOE_BLOCK).reshape(1)
    return slot, blk_expert, n_used, p


def _layer_rows(x2, pos2, row0, batch, seq, out_prev, norm1_g, w_in, q_norm_g, k_norm_g, lam4, diff_subln_g,
                ret_gn_g, ret_gn_b, w_branch_a, w_branch_b, w_out, norm2_g, w_gr, b_gr, w_er, b_er,
                w_gate, w_up, w_down):
    n = batch * seq
    proj = _in_proj(x2, pos2, norm1_g, w_in, q_norm_g, k_norm_g, row0, n)
    oa = _diff_attn(proj, lam4, diff_subln_g.reshape(1, DA_VDIM), batch, seq)
    ob = _retention(proj, ret_gn_g, ret_gn_b, batch, seq)
    x1, h2, route, counts = _merge(x2, oa, ob, proj, w_branch_a, w_branch_b, w_out, norm2_g,
                                   w_gr, b_gr, w_er, b_er, row0)
    slot, blk_expert, n_used, p = _dispatch_plan(route, counts, n)
    win_n = n // SC_WINDOW
    src_block = lambda s: (s // (TOP_K * win_n)) * win_n + s % win_n
    xs = _sc_scatter_rows(h2.reshape(ROW_PIECES * n, PIECE), slot, ROW_PIECES * p, src_block)
    ys = _experts(xs.reshape(ROW_PIECES, p, PIECE), blk_expert, n_used, w_gate, w_up, w_down)
    yg = _sc_gather_rows(ys.reshape(ROW_PIECES * p, PIECE), slot)
    return _combine(x1, yg.reshape(ROW_PIECES, TOP_K * n, PIECE), route, row0, x2.shape[0], out_prev)


def _layer(x, positions, *weights):
    batch, seq, _ = x.shape
    n = batch * seq
    x2 = x.reshape(n, D_MODEL)
    pos2 = positions.reshape(1, n)
    groups = BATCH_GROUPS if batch % BATCH_GROUPS == 0 else 1
    per = batch // groups
    out = None
    for g in range(groups):
        out = _layer_rows(x2, pos2, g * per * seq, per, seq, out, *weights)
    return out.reshape(batch, seq, D_MODEL)


def kernel(x, positions, norm1_g, w_in, q_norm_g, k_norm_g, lambda_q1, lambda_k1, lambda_q2, lambda_k2, diff_subln_g, ret_gn_g, ret_gn_b, w_branch_a, w_branch_b, w_out, norm2_g, w_group_router, b_group_router, w_expert_router, b_expert_router, w_gate, w_up, w_down):
    assert x.shape[-1] == D_MODEL and norm1_g.shape[0] == 1, "single-layer, D_MODEL-wide input expected"
    lam4 = jnp.zeros((4, LANES), jnp.float32)
    lam4 = lam4.at[:, :DA_HALF].set(jnp.stack([lambda_q1[0], lambda_k1[0], lambda_q2[0], lambda_k2[0]]))
    return _layer(x, positions, norm1_g[0], w_in[0], q_norm_g[0], k_norm_g[0], lam4, diff_subln_g[0],
                  ret_gn_g[0], ret_gn_b[0], w_branch_a[0], w_branch_b[0], w_out[0], norm2_g[0],
                  w_group_router[0], b_group_router[0], w_expert_router[0], b_expert_router[0],
                  w_gate[0], w_up[0], w_down[0])
```

```python
import functools
import math

import jax
import jax.numpy as jnp
from jax import lax
from jax.experimental import pallas as pl
from jax.experimental.pallas import tpu as pltpu
from jax.experimental.pallas import tpu_sc as plsc

D_MODEL = 1024
DA_HEADS = 4
DA_HALF = 64
DA_VDIM = 2 * DA_HALF
DA_WIDTH = DA_HEADS * DA_VDIM
ROPE_THETA = 500000.0
ROPE_DIM = DA_HALF // 4
RET_HEADS = 4
RET_KDIM = 128
RET_VDIM = 128
RET_WIDTH = RET_HEADS * RET_VDIM
RET_THETA = 10000.0
N_GROUPS = 4
EXPERTS_PER_GROUP = 8
N_EXPERTS = N_GROUPS * EXPERTS_PER_GROUP
TOP_K = 2
EXPERT_FF = 512
EPS = 1e-6
LAMBDA_INIT = 0.8 - 0.6 * math.exp(-0.3 * 0)

LANES = 128
IN_COLS = 3 * DA_WIDTH + 4 * RET_WIDTH + 2 * D_MODEL
COL_QA, COL_KA, COL_VA = 0, DA_WIDTH, 2 * DA_WIDTH
COL_QR = 3 * DA_WIDTH
COL_KR = COL_QR + RET_WIDTH
COL_VR = COL_KR + RET_WIDTH
COL_GB = COL_VR + RET_WIDTH
COL_GATE_A = COL_GB + RET_WIDTH
COL_GATE_B = COL_GATE_A + D_MODEL

PROJ_ROWS = 512
PROJ_CHUNK = 256
ATT_TILE = 512
ATT_ROWS = 32
RET_CHUNK = 256
RET_UNROLL = 8
MOE_BLOCK = 512
PACKED = D_MODEL // 2
ROW_PIECES = 2
PIECE = PACKED // ROW_PIECES
SC_CORES = 2
SC_WINDOW = 128
BATCH_GROUPS = 1
VMEM_LIMIT = 56 * 1024 * 1024


def _dot(a, b):
    return jnp.dot(a, b, preferred_element_type=jnp.float32)


def _dot_nt(a, b):
    return lax.dot_general(a, b, (((1,), (1,)), ((), ())), preferred_element_type=jnp.float32)


def _dot_tn(a, b):
    return lax.dot_general(a, b, (((0,), (0,)), ((), ())), preferred_element_type=jnp.float32)


def _sigmoid(x):
    return 0.5 * jnp.tanh(0.5 * x) + 0.5


def _split3(x):
    a = x.astype(jnp.bfloat16)
    r = x - a.astype(jnp.float32)
    b = r.astype(jnp.bfloat16)
    c = (r - b.astype(jnp.float32)).astype(jnp.bfloat16)
    return a, b, c


def _in_proj_kernel(x_ref, pos_ref, g1_ref, w_ref, gsum_ref, gq_ref, gk_ref, fa_ref, fr_ref, sel_ref,
                    o_ref, h_scr):
    x = x_ref[...]
    h = x * lax.rsqrt(jnp.mean(x * x, axis=-1, keepdims=True) + EPS) * g1_ref[...]
    h_scr[...] = h.astype(jnp.bfloat16)
    rows = x.shape[0]
    pos = pos_ref[...].astype(jnp.float32)

    lane = lax.broadcasted_iota(jnp.int32, (rows, LANES), 1)
    half_a = ROPE_DIM // 2
    ang_a = fa_ref[...] * pos
    pad = jnp.zeros((LANES - 2 * half_a, rows), jnp.float32)
    t_a = jnp.concatenate([jnp.cos(ang_a), jnp.sin(ang_a), pad], axis=0).T
    tab = sum(_dot(part, sel_ref[...]) for part in _split3(t_a))
    lane64 = lane % DA_HALF
    c_a = tab[:, :LANES] + jnp.where(lane64 < ROPE_DIM, 0.0, 1.0)
    s_lo = tab[:, LANES:2 * LANES]
    s_hi = tab[:, 2 * LANES:]
    c_a2 = jnp.concatenate([c_a, c_a], axis=1)
    s_lo2 = jnp.concatenate([s_lo, s_lo], axis=1)
    s_hi2 = jnp.concatenate([s_hi, s_hi], axis=1)
    ang_r = fr_ref[...] * pos
    t_r = jnp.concatenate([jnp.cos(ang_r), jnp.sin(ang_r)], axis=0).T
    sw_r = pltpu.roll(t_r, RET_KDIM // 2, axis=1)
    first = lane < RET_KDIM // 2
    c_r = jnp.where(first, t_r, sw_r)
    s_r = jnp.where(first, -sw_r, t_r)
    c_r2 = jnp.concatenate([c_r, c_r], axis=1)
    s_r2 = jnp.concatenate([s_r, s_r], axis=1)

    def qk_norm_rope(y, g, scale):
        ss = y * y
        hi = ss.astype(jnp.bfloat16)
        lo = (ss - hi.astype(jnp.float32)).astype(jnp.bfloat16)
        gs = _dot(hi, gsum_ref[...]) + _dot(lo, gsum_ref[...])
        n = y * lax.rsqrt(gs * (1.0 / DA_HALF) + EPS) * g
        up = pltpu.roll(n, PROJ_CHUNK - half_a, axis=1)
        dn = pltpu.roll(n, half_a, axis=1)
        r = n * c_a2 + up * s_lo2 + dn * s_hi2
        return r * scale if scale != 1.0 else r

    def ret_rope(y, scale):
        halves = [pltpu.roll(y[:, i * LANES:(i + 1) * LANES], RET_KDIM // 2, axis=1)
                  for i in range(PROJ_CHUNK // LANES)]
        sw = jnp.concatenate(halves, axis=1)
        r = y * c_r2 + sw * s_r2
        return r * scale if scale != 1.0 else r

    for c in range(IN_COLS // PROJ_CHUNK):
        c0 = c * PROJ_CHUNK
        y = _dot(h_scr[...], w_ref[:, c0:c0 + PROJ_CHUNK])
        if c0 < COL_KA:
            y = qk_norm_rope(y, gq_ref[...], DA_HALF ** -0.5)
        elif c0 < COL_VA:
            y = qk_norm_rope(y, gk_ref[...], 1.0)
        elif c0 < COL_QR:
            pass
        elif c0 < COL_KR:
            y = ret_rope(y, 1.0)
        elif c0 < COL_VR:
            y = ret_rope(y, RET_KDIM ** -0.5)
        elif c0 < COL_GB:
            pass
        elif c0 < COL_GATE_A:
            y = y * _sigmoid(y)
        else:
            y = _sigmoid(y)
        o_ref[:, c0:c0 + PROJ_CHUNK] = y.astype(o_ref.dtype)


def _in_proj(x2, pos2, g1, w_in, gq, gk, row0, n):
    tm = min(PROJ_ROWS, n)
    blk0 = row0 // tm
    grp = jnp.arange(PROJ_CHUNK) // DA_HALF
    gsum = (grp[:, None] == grp[None, :]).astype(jnp.bfloat16)
    half_a = ROPE_DIM // 2
    fa = jnp.power(jnp.float32(ROPE_THETA), -2.0 * jnp.arange(half_a, dtype=jnp.float32) / ROPE_DIM)[:, None]
    half_r = RET_KDIM // 2
    fr = jnp.power(jnp.float32(RET_THETA), -2.0 * jnp.arange(half_r, dtype=jnp.float32) / RET_KDIM)[:, None]
    j = jnp.arange(LANES)[:, None]
    l64 = (jnp.arange(LANES) % DA_HALF)[None, :]
    sel_c = (j < half_a) & (l64 < ROPE_DIM) & (l64 % half_a == j)
    sel_lo = (j >= half_a) & (j < ROPE_DIM) & (l64 < half_a) & (l64 == j - half_a)
    sel_hi = (j >= half_a) & (j < ROPE_DIM) & (l64 >= half_a) & (l64 < ROPE_DIM) & (l64 == j)
    sel = jnp.concatenate([sel_c.astype(jnp.float32), -sel_lo.astype(jnp.float32),
                           sel_hi.astype(jnp.float32)], axis=1).astype(jnp.bfloat16)
    reps = PROJ_CHUNK // DA_HALF
    full = lambda shape: pl.BlockSpec(shape, lambda i: (0,) * len(shape))
    return pl.pallas_call(
        _in_proj_kernel,
        grid=(n // tm,),
        in_specs=[
            pl.BlockSpec((tm, D_MODEL), lambda i: (blk0 + i, 0)),
            pl.BlockSpec((1, tm), lambda i: (0, blk0 + i)),
            full((1, D_MODEL)),
            full((D_MODEL, IN_COLS)),
            full((PROJ_CHUNK, PROJ_CHUNK)),
            full((1, PROJ_CHUNK)),
            full((1, PROJ_CHUNK)),
            full((half_a, 1)),
            full((half_r, 1)),
            full((LANES, 3 * LANES)),
        ],
        out_specs=pl.BlockSpec((tm, IN_COLS), lambda i: (i, 0)),
        out_shape=jax.ShapeDtypeStruct((n, IN_COLS), jnp.bfloat16),
        scratch_shapes=[pltpu.VMEM((tm, D_MODEL), jnp.bfloat16)],
        compiler_params=pltpu.CompilerParams(dimension_semantics=("arbitrary",),
                                             vmem_limit_bytes=VMEM_LIMIT),
        name="in_proj",
    )(x2, pos2, g1.reshape(1, D_MODEL), w_in.astype(jnp.bfloat16), gsum,
      jnp.tile(gq, reps)[None, :], jnp.tile(gk, reps)[None, :], fa, fr, sel)


def _diff_attn_kernel(q_ref, k_ref, v_ref, lam_ref, gsub_ref, o_ref,
                      qs_scr, vx_scr, s0_scr, s1_scr, p_scr, m_scr, alpha_scr, acc_scr):
    i = pl.program_id(2)
    t = q_ref.shape[0]

    @pl.when(i == 0)
    def _():
        vx_scr[:, :DA_VDIM] = v_ref[...]
        vx_scr[:, DA_VDIM:] = jnp.ones((vx_scr.shape[0], LANES), vx_scr.dtype)

    q = q_ref[...]
    lane = lax.broadcasted_iota(jnp.int32, q.shape, 1)
    zero = jnp.zeros_like(q)
    qs_scr[:t] = jnp.where(lane < DA_HALF, q, zero)
    qs_scr[t:] = jnp.where(lane >= DA_HALF, q, zero)
    m_scr[...] = jnp.full(m_scr.shape, -jnp.inf, jnp.float32)
    acc_scr[...] = jnp.zeros(acc_scr.shape, jnp.float32)

    def scores(j, s_ref):
        start = pl.multiple_of(j * t, t)
        s_ref[...] = _dot_nt(qs_scr[...], k_ref[pl.ds(start, t), :])

    def softmax_pv(j, s_ref, masked):
        for c in range(2 * t // ATT_ROWS):
            rows = pl.ds(c * ATT_ROWS, ATT_ROWS)
            s = s_ref[rows, :]
            if masked:
                r = lax.broadcasted_iota(jnp.int32, s.shape, 0) + (c * ATT_ROWS) % t
                col = lax.broadcasted_iota(jnp.int32, s.shape, 1)
                s = jnp.where(col <= r, s, -jnp.inf)
            m_prev = m_scr[rows, :]
            m_new = jnp.maximum(m_prev, jnp.max(s, axis=-1, keepdims=True))
            alpha_scr[rows, :] = jnp.exp(m_prev - m_new)
            m_scr[rows, :] = m_new
            p = jnp.exp(s - jnp.concatenate([m_new] * (t // LANES), axis=1))
            p_scr[rows, :] = p.astype(p_scr.dtype)
        start = pl.multiple_of(j * t, t)
        pv = _dot(p_scr[...], vx_scr[pl.ds(start, t), :])
        alpha = alpha_scr[...]
        for half in range(2):
            cols = pl.ds(half * LANES, LANES)
            acc_scr[:, cols] = alpha * acc_scr[:, cols] + pv[:, half * LANES:(half + 1) * LANES]

    scores(0, s0_scr)

    def pair(jj, carry):
        j = 2 * jj
        scores(j + 1, s1_scr)
        softmax_pv(j, s0_scr, False)
        scores(j + 2, s0_scr)
        softmax_pv(j + 1, s1_scr, False)
        return carry

    lax.fori_loop(0, i // 2, pair, 0)

    @pl.when(i % 2 == 1)
    def _():
        scores(i, s1_scr)
        softmax_pv(i - 1, s0_scr, False)
        softmax_pv(i, s1_scr, True)

    @pl.when(i % 2 == 0)
    def _():
        softmax_pv(i, s0_scr, True)

    lam4 = lam_ref[...]
    lam = (jnp.exp(jnp.sum(lam4[0:1] * lam4[1:2], axis=-1, keepdims=True))
           - jnp.exp(jnp.sum(lam4[2:3] * lam4[3:4], axis=-1, keepdims=True)) + LAMBDA_INIT)
    o_all = acc_scr[:, :DA_VDIM] / acc_scr[:, DA_VDIM:]
    o = o_all[:t] - lam * o_all[t:]
    o = o * lax.rsqrt(jnp.mean(o * o, axis=-1, keepdims=True) + EPS) * gsub_ref[...] * (1.0 - LAMBDA_INIT)
    o_ref[...] = o.astype(o_ref.dtype)


def _diff_attn(proj, lam4, gsub, batch, seq):
    n = proj.shape[0]
    t = min(ATT_TILE, seq)
    nq = seq // t
    qb, kb, vb = COL_QA // LANES, COL_KA // LANES, COL_VA // LANES
    return pl.pallas_call(
        _diff_attn_kernel,
        grid=(batch, DA_HEADS, nq),
        in_specs=[
            pl.BlockSpec((t, LANES), lambda b, h, i: (b * nq + i, qb + h)),
            pl.BlockSpec((seq, LANES), lambda b, h, i: (b, kb + h)),
            pl.BlockSpec((seq, LANES), lambda b, h, i: (b, vb + h)),
            pl.BlockSpec((4, LANES), lambda b, h, i: (0, 0)),
            pl.BlockSpec((1, LANES), lambda b, h, i: (0, 0)),
        ],
        out_specs=pl.BlockSpec((t, LANES), lambda b, h, i: (b * nq + i, h)),
        out_shape=jax.ShapeDtypeStruct((n, DA_WIDTH), jnp.bfloat16),
        scratch_shapes=[pltpu.VMEM((2 * t, LANES), jnp.bfloat16),
                        pltpu.VMEM((seq, DA_VDIM + LANES), jnp.bfloat16),
                        pltpu.VMEM((2 * t, t), jnp.float32),
                        pltpu.VMEM((2 * t, t), jnp.float32),
                        pltpu.VMEM((2 * t, t), jnp.bfloat16),
                        pltpu.VMEM((2 * t, LANES), jnp.float32),
                        pltpu.VMEM((2 * t, LANES), jnp.float32),
                        pltpu.VMEM((2 * t, DA_VDIM + LANES), jnp.float32)],
        compiler_params=pltpu.CompilerParams(dimension_semantics=("arbitrary",) * 3,
                                             vmem_limit_bytes=VMEM_LIMIT),
        name="diff_attn",
    )(proj, proj, proj, lam4, gsub)


def _retention_kernel(q_ref, k_ref, v_ref, g_ref, gng_ref, gnb_ref, o_ref, r_scr, *, chunk):
    hf = jnp.full((1, 1), pl.program_id(1), jnp.int32).astype(jnp.float32)
    log_g = jnp.log1p(-jnp.exp2(-5.0 - hf))
    ri = lax.broadcasted_iota(jnp.int32, (chunk, chunk), 0)
    ci = lax.broadcasted_iota(jnp.int32, (chunk, chunk), 1)
    rel = (ri - ci).astype(jnp.float32)
    dmask = jnp.where(rel >= 0, jnp.exp(jnp.maximum(rel, 0.0) * log_g), 0.0)
    idx = lax.broadcasted_iota(jnp.int32, (chunk, 1), 0).astype(jnp.float32)
    zeta = jnp.exp((chunk - 1 - idx) * log_g)
    xi = jnp.exp((idx + 1.0) * log_g)
    g_chunk = jnp.exp(chunk * log_g)
    r_scr[...] = jnp.zeros(r_scr.shape, jnp.float32)
    gng = gng_ref[...]
    gnb = gnb_ref[...]

    def body(n, carry):
        start = pl.multiple_of(n * chunk, chunk)
        q = q_ref[pl.ds(start, chunk), :]
        k = k_ref[pl.ds(start, chunk), :]
        v = v_ref[pl.ds(start, chunk), :]
        s = _dot_nt(q, k) * dmask
        r_old = r_scr[...]
        o = _dot(s.astype(jnp.bfloat16), v) + xi * _dot(q, r_old.astype(jnp.bfloat16))
        kz = (k.astype(jnp.float32) * zeta).astype(jnp.bfloat16)
        r_scr[...] = g_chunk * r_old + _dot_tn(kz, v)
        mu = jnp.mean(o, axis=-1, keepdims=True)
        d = o - mu
        var = jnp.mean(d * d, axis=-1, keepdims=True)
        y = d * lax.rsqrt(var + EPS) * gng + gnb
        y = y * g_ref[pl.ds(start, chunk), :].astype(jnp.float32)
        o_ref[pl.ds(start, chunk), :] = y.astype(o_ref.dtype)
        return carry

    lax.fori_loop(0, q_ref.shape[0] // chunk, body, 0, unroll=RET_UNROLL)


def _retention(proj, gn_g, gn_b, batch, seq):
    n = proj.shape[0]
    chunk = min(RET_CHUNK, seq)
    col = lambda c0: (lambda b, h: (b, c0 // LANES + h))
    return pl.pallas_call(
        functools.partial(_retention_kernel, chunk=chunk),
        grid=(batch, RET_HEADS),
        in_specs=[
            pl.BlockSpec((seq, LANES), col(COL_QR)),
            pl.BlockSpec((seq, LANES), col(COL_KR)),
            pl.BlockSpec((seq, LANES), col(COL_VR)),
            pl.BlockSpec((seq, LANES), col(COL_GB)),
            pl.BlockSpec((1, LANES), lambda b, h: (0, h)),
            pl.BlockSpec((1, LANES), lambda b, h: (0, h)),
        ],
        out_specs=pl.BlockSpec((seq, LANES), lambda b, h: (b, h)),
        out_shape=jax.ShapeDtypeStruct((n, RET_WIDTH), jnp.bfloat16),
        scratch_shapes=[pltpu.VMEM((RET_KDIM, RET_VDIM), jnp.float32)],
        compiler_params=pltpu.CompilerParams(dimension_semantics=("arbitrary",) * 2,
                                             vmem_limit_bytes=VMEM_LIMIT),
        name="retention",
    )(proj, proj, proj, proj, gn_g.reshape(1, RET_WIDTH), gn_b.reshape(1, RET_WIDTH))


def _merge_kernel(x_ref, oa_ref, ob_ref, sa0_ref, sa1_ref, sb0_ref, sb1_ref, wa_ref, wb_ref, wo_ref,
                  g2_ref, wr_hi_ref, wr_lo_ref, br_ref, tri_ref, x1_ref, h2_ref, route_ref, counts_ref,
                  base_scr):
    ya = _dot(oa_ref[...], wa_ref[...])
    yb = _dot(ob_ref[...], wb_ref[...])
    sa = jnp.concatenate([sa0_ref[...], sa1_ref[...]], axis=1).astype(jnp.float32)
    sb = jnp.concatenate([sb0_ref[...], sb1_ref[...]], axis=1).astype(jnp.float32)
    merged = sa * ya + sb * yb
    x1 = x_ref[...] + _dot(merged.astype(jnp.bfloat16), wo_ref[...])
    x1_ref[...] = x1
    h2 = x1 * lax.rsqrt(jnp.mean(x1 * x1, axis=-1, keepdims=True) + EPS) * g2_ref[...]
    _store_packed(h2_ref, h2)

    hi = h2.astype(jnp.bfloat16)
    lo = (h2 - hi.astype(jnp.float32)).astype(jnp.bfloat16)
    logits = (_dot(hi, wr_hi_ref[...]) + _dot(lo, wr_hi_ref[...]) + _dot(hi, wr_lo_ref[...])
              + br_ref[...])
    lane = lax.broadcasted_iota(jnp.int32, logits.shape, 1)
    neg = -jnp.inf
    gl = jnp.where(lane < N_GROUPS, logits, neg)
    gmax = jnp.max(gl, axis=-1, keepdims=True)
    g_idx = jnp.min(jnp.where(gl == gmax, lane, LANES), axis=-1, keepdims=True)
    p_g = 1.0 / jnp.sum(jnp.exp(gl - gmax), axis=-1, keepdims=True)
    e_lo = N_GROUPS + EXPERTS_PER_GROUP * g_idx
    el = jnp.where((lane >= e_lo) & (lane < e_lo + EXPERTS_PER_GROUP), logits, neg)
    v1 = jnp.max(el, axis=-1, keepdims=True)
    i1 = jnp.min(jnp.where(el == v1, lane, LANES), axis=-1, keepdims=True)
    el2 = jnp.where(lane == i1, neg, el)
    v2 = jnp.max(el2, axis=-1, keepdims=True)
    i2 = jnp.min(jnp.where(el2 == v2, lane, LANES), axis=-1, keepdims=True)
    t = jnp.exp(v2 - v1)
    w1 = p_g / (1.0 + t)
    w2 = p_g * t / (1.0 + t)
    e1 = i1 - N_GROUPS
    e2 = i2 - N_GROUPS

    @pl.when(pl.program_id(0) == 0)
    def _():
        base_scr[...] = jnp.zeros(base_scr.shape, jnp.float32)

    oh1 = lane == e1
    oh2 = lane == e2
    picked = jnp.where(oh1 | oh2, 1.0, 0.0)
    before = _dot(tri_ref[...], picked.astype(jnp.bfloat16)) + base_scr[0:1, :]
    rank1 = jnp.sum(jnp.where(oh1, before, 0.0), axis=-1, keepdims=True)
    rank2 = jnp.sum(jnp.where(oh2, before, 0.0), axis=-1, keepdims=True)
    base_scr[...] = base_scr[...] + jnp.sum(picked, axis=0, keepdims=True)
    counts_ref[...] = base_scr[...]

    cols = [e1.astype(jnp.float32), e2.astype(jnp.float32), w1, w2, rank1, rank2]
    route = jnp.zeros(logits.shape, jnp.float32)
    for c, val in enumerate(cols):
        route = jnp.where(lane == c, val, route)
    route_ref[...] = route


def _merge(x2, oa, ob, proj, wa, wb, wo, g2, w_gr, b_gr, w_er, b_er, row0):
    n = oa.shape[0]
    tm = min(PROJ_ROWS, n)
    blk0 = row0 // tm
    half = D_MODEL // 2
    wr = jnp.zeros((D_MODEL, LANES), jnp.float32)
    wr = wr.at[:, :N_GROUPS].set(w_gr).at[:, N_GROUPS:N_GROUPS + N_EXPERTS].set(w_er)
    wr_hi = wr.astype(jnp.bfloat16)
    wr_lo = (wr - wr_hi.astype(jnp.float32)).astype(jnp.bfloat16)
    br = jnp.zeros((1, LANES), jnp.float32)
    br = br.at[0, :N_GROUPS].set(b_gr).at[0, N_GROUPS:N_GROUPS + N_EXPERTS].set(b_er)
    tri = (jnp.arange(tm)[:, None] > jnp.arange(tm)[None, :]).astype(jnp.bfloat16)
    full = lambda shape: pl.BlockSpec(shape, lambda i: (0,) * len(shape))
    gate = lambda c0: pl.BlockSpec((tm, half), lambda i: (i, c0 // half))
    return pl.pallas_call(
        _merge_kernel,
        grid=(n // tm,),
        in_specs=[
            pl.BlockSpec((tm, D_MODEL), lambda i: (blk0 + i, 0)),
            pl.BlockSpec((tm, DA_WIDTH), lambda i: (i, 0)),
            pl.BlockSpec((tm, RET_WIDTH), lambda i: (i, 0)),
            gate(COL_GATE_A), gate(COL_GATE_A + half), gate(COL_GATE_B), gate(COL_GATE_B + half),
            full((DA_WIDTH, D_MODEL)), full((RET_WIDTH, D_MODEL)), full((D_MODEL, D_MODEL)),
            full((1, D_MODEL)), full((D_MODEL, LANES)), full((D_MODEL, LANES)), full((1, LANES)),
            full((tm, tm)),
        ],
        out_specs=[
            pl.BlockSpec((tm, D_MODEL), lambda i: (i, 0)),
            pl.BlockSpec((ROW_PIECES, tm, PIECE), lambda i: (0, i, 0)),
            pl.BlockSpec((tm, LANES), lambda i: (i, 0)),
            pl.BlockSpec((8, LANES), lambda i: (0, 0)),
        ],
        out_shape=[
            jax.ShapeDtypeStruct((n, D_MODEL), jnp.float32),
            jax.ShapeDtypeStruct((ROW_PIECES, n, PIECE), jnp.uint32),
            jax.ShapeDtypeStruct((n, LANES), jnp.float32),
            jax.ShapeDtypeStruct((8, LANES), jnp.float32),
        ],
        scratch_shapes=[pltpu.VMEM((8, LANES), jnp.float32)],
        compiler_params=pltpu.CompilerParams(dimension_semantics=("arbitrary",),
                                             vmem_limit_bytes=VMEM_LIMIT),
        name="merge",
    )(x2, oa, ob, proj, proj, proj, proj, wa.astype(jnp.bfloat16), wb.astype(jnp.bfloat16),
      wo.astype(jnp.bfloat16), g2.reshape(1, D_MODEL), wr_hi, wr_lo, br, tri)


def _sc_mesh():
    return plsc.VectorSubcoreMesh(core_axis_name="c", subcore_axis_name="s")


def _sc_scatter_rows(src, idx, out_rows, src_block):
    steps = idx.shape[1] // SC_WINDOW
    per_core = steps // SC_CORES

    @pl.kernel(out_type=jax.ShapeDtypeStruct((out_rows, PIECE), src.dtype), mesh=_sc_mesh())
    def scatter(src_hbm, idx_hbm, out_hbm):
        def body(src_vmem, idx_vmem):
            pltpu.sync_copy(src_vmem, out_hbm.at[idx_vmem.at[0]])

        pltpu.emit_pipeline(
            body,
            grid=(SC_CORES, per_core),
            in_specs=[pl.BlockSpec((SC_WINDOW, PIECE), lambda c, i: (src_block(c * per_core + i), 0)),
                      pl.BlockSpec((1, SC_WINDOW), lambda c, i: (0, c * per_core + i))],
            out_specs=[],
            core_axis_name=("c", "s"),
            dimension_semantics=(pltpu.PARALLEL, pltpu.PARALLEL),
        )(src_hbm, idx_hbm)

    return scatter(src, idx)


def _sc_gather_rows(table, idx):
    num = idx.shape[1]
    per_core = num // SC_WINDOW // SC_CORES

    @pl.kernel(out_type=jax.ShapeDtypeStruct((num, PIECE), table.dtype), mesh=_sc_mesh())
    def gather(table_hbm, idx_hbm, out_hbm):
        def body(idx_vmem, out_vmem):
            pltpu.sync_copy(table_hbm.at[idx_vmem.at[0]], out_vmem)

        pltpu.emit_pipeline(
            body,
            grid=(SC_CORES, per_core),
            in_specs=[pl.BlockSpec((1, SC_WINDOW), lambda c, i: (0, c * per_core + i))],
            out_specs=[pl.BlockSpec((SC_WINDOW, PIECE), lambda c, i: (c * per_core + i, 0))],
            core_axis_name=("c", "s"),
            dimension_semantics=(pltpu.PARALLEL, pltpu.PARALLEL),
        )(idx_hbm, out_hbm)

    return gather(table, idx)


def _store_packed(ref, val):
    as_bits = lambda v: lax.bitcast_convert_type(v.astype(jnp.bfloat16).astype(jnp.float32), jnp.uint32)
    words = (as_bits(val[:, :PACKED]) >> 16) | (as_bits(val[:, PACKED:]) & jnp.uint32(0xFFFF0000))
    for j in range(ROW_PIECES):
        ref[j] = words[:, j * PIECE:(j + 1) * PIECE]


def _load_packed(ref):
    words = jnp.concatenate([ref[j] for j in range(ROW_PIECES)], axis=1)
    low = lax.bitcast_convert_type(words << 16, jnp.float32)
    high = lax.bitcast_convert_type(words & jnp.uint32(0xFFFF0000), jnp.float32)
    return jnp.concatenate([low, high], axis=1)


def _expert_kernel(blk_e_ref, n_used_ref, x_ref, wg_ref, wu_ref, wd_ref, o_ref, wg_scr, wu_scr, wd_scr):
    i = pl.program_id(0)

    @pl.when((i == 0) | (blk_e_ref[i] != blk_e_ref[jnp.maximum(i - 1, 0)]))
    def _():
        wg_scr[...] = wg_ref[0].astype(jnp.bfloat16)
        wu_scr[...] = wu_ref[0].astype(jnp.bfloat16)
        wd_scr[...] = wd_ref[0].astype(jnp.bfloat16)

    @pl.when(i < n_used_ref[0])
    def _():
        x = _load_packed(x_ref).astype(jnp.bfloat16)
        a = _dot(x, wg_scr[...])
        u = _dot(x, wu_scr[...])
        hmid = (a * _sigmoid(a) * u).astype(jnp.bfloat16)
        _store_packed(o_ref, _dot(hmid, wd_scr[...]))

    @pl.when(i >= n_used_ref[0])
    def _():
        o_ref[...] = jnp.zeros(o_ref.shape, o_ref.dtype)


def _experts(xs, blk_expert, n_used, w_gate, w_up, w_down):
    p = xs.shape[1]
    live = lambda i, be, nu: jnp.minimum(i, nu[0] - 1)
    return pl.pallas_call(
        _expert_kernel,
        grid_spec=pltpu.PrefetchScalarGridSpec(
            num_scalar_prefetch=2,
            grid=(p // MOE_BLOCK,),
            in_specs=[
                pl.BlockSpec((ROW_PIECES, MOE_BLOCK, PIECE), lambda i, be, nu: (0, live(i, be, nu), 0)),
                pl.BlockSpec((1, D_MODEL, EXPERT_FF), lambda i, be, nu: (be[i], 0, 0)),
                pl.BlockSpec((1, D_MODEL, EXPERT_FF), lambda i, be, nu: (be[i], 0, 0)),
                pl.BlockSpec((1, EXPERT_FF, D_MODEL), lambda i, be, nu: (be[i], 0, 0)),
            ],
            out_specs=pl.BlockSpec((ROW_PIECES, MOE_BLOCK, PIECE), lambda i, be, nu: (0, i, 0)),
            scratch_shapes=[pltpu.VMEM((D_MODEL, EXPERT_FF), jnp.bfloat16),
                            pltpu.VMEM((D_MODEL, EXPERT_FF), jnp.bfloat16),
                            pltpu.VMEM((EXPERT_FF, D_MODEL), jnp.bfloat16)],
        ),
        out_shape=jax.ShapeDtypeStruct((ROW_PIECES, p, PIECE), jnp.uint32),
        compiler_params=pltpu.CompilerParams(dimension_semantics=("arbitrary",),
                                             vmem_limit_bytes=VMEM_LIMIT),
        name="experts",
    )(blk_expert, n_used, xs, w_gate, w_up, w_down)


def _combine_kernel(x1_ref, route_ref, y0_ref, y1_ref, *rest):
    o_ref = rest[-1]
    route = route_ref[...]
    o_ref[...] = x1_ref[...] + route[:, 2:3] * _load_packed(y0_ref) + route[:, 3:4] * _load_packed(y1_ref)


def _combine(x1, yg, route, row0, n_total, out_prev):
    n = x1.shape[0]
    tm = min(PROJ_ROWS, n)
    blk0 = row0 // tm
    prev = () if out_prev is None else (out_prev,)
    return pl.pallas_call(
        _combine_kernel,
        grid=(n // tm,),
        in_specs=[
            pl.BlockSpec((tm, D_MODEL), lambda i: (i, 0)),
            pl.BlockSpec((tm, LANES), lambda i: (i, 0)),
            pl.BlockSpec((ROW_PIECES, tm, PIECE), lambda i: (0, i, 0)),
            pl.BlockSpec((ROW_PIECES, tm, PIECE), lambda i: (0, i + n // tm, 0)),
        ] + [pl.BlockSpec(memory_space=pl.ANY)] * len(prev),
        out_specs=pl.BlockSpec((tm, D_MODEL), lambda i: (blk0 + i, 0)),
        out_shape=jax.ShapeDtypeStruct((n_total, D_MODEL), jnp.float32),
        input_output_aliases={4: 0} if prev else {},
        compiler_params=pltpu.CompilerParams(dimension_semantics=("arbitrary",),
                                             vmem_limit_bytes=VMEM_LIMIT),
        name="combine",
    )(x1, route, yg, yg, *prev)


def _dispatch_plan(route, counts, n):
    counts = counts[0, :N_EXPERTS].astype(jnp.int32)
    padded = ((counts + MOE_BLOCK - 1) // MOE_BLOCK) * MOE_BLOCK
    seg_end = jnp.cumsum(padded).astype(jnp.int32)
    seg_start = seg_end - padded
    e = route[:, 0:TOP_K].astype(jnp.int32)
    rank = route[:, 4:4 + TOP_K].astype(jnp.int32)
    dest = (seg_start[e] + rank).T.reshape(-1)
    p = n * TOP_K + N_EXPERTS * MOE_BLOCK
    slot = (dest[None, :] + (jnp.arange(ROW_PIECES, dtype=jnp.int32) * p)[:, None]).reshape(1, -1)
    blk_start = jnp.arange(p // MOE_BLOCK, dtype=jnp.int32) * MOE_BLOCK
    blk_expert = jnp.sum((seg_end[None, :] <= blk_start[:, None]).astype(jnp.int32), axis=1)
    blk_expert = jnp.minimum(blk_expert, N_EXPERTS - 1)
    n_used = (seg_end[-1] // MOE_BLOCK).reshape(1)
    return slot, blk_expert, n_used, p


def _layer_rows(x2, pos2, row0, batch, seq, out_prev, norm1_g, w_in, q_norm_g, k_norm_g, lam4, diff_subln_g,
                ret_gn_g, ret_gn_b, w_branch_a, w_branch_b, w_out, norm2_g, w_gr, b_gr, w_er, b_er,
                w_gate, w_up, w_down):
    n = batch * seq
    proj = _in_proj(x2, pos2, norm1_g, w_in, q_norm_g, k_norm_g, row0, n)
    oa = _diff_attn(proj, lam4, diff_subln_g.reshape(1, DA_VDIM), batch, seq)
    ob = _retention(proj, ret_gn_g, ret_gn_b, batch, seq)
    x1, h2, route, counts = _merge(x2, oa, ob, proj, w_branch_a, w_branch_b, w_out, norm2_g,
                                   w_gr, b_gr, w_er, b_er, row0)
    slot, blk_expert, n_used, p = _dispatch_plan(route, counts, n)
    win_n = n // SC_WINDOW
    src_block = lambda s: (s // (TOP_K * win_n)) * win_n + s % win_n
    xs = _sc_scatter_rows(h2.reshape(ROW_PIECES * n, PIECE), slot, ROW_PIECES * p, src_block)
    ys = _experts(xs.reshape(ROW_PIECES, p, PIECE), blk_expert, n_used, w_gate, w_up, w_down)
    yg = _sc_gather_rows(ys.reshape(ROW_PIECES * p, PIECE), slot)
    return _combine(x1, yg.reshape(ROW_PIECES, TOP_K * n, PIECE), route, row0, x2.shape[0], out_prev)


def _layer(x, positions, *weights):
    batch, seq, _ = x.shape
    n = batch * seq
    x2 = x.reshape(n, D_MODEL)
    pos2 = positions.reshape(1, n)
    groups = BATCH_GROUPS if batch % BATCH_GROUPS == 0 else 1
    per = batch // groups
    out = None
    for g in range(groups):
        out = _layer_rows(x2, pos2, g * per * seq, per, seq, out, *weights)
    return out.reshape(batch, seq, D_MODEL)


def kernel(x, positions, norm1_g, w_in, q_norm_g, k_norm_g, lambda_q1, lambda_k1, lambda_q2, lambda_k2, diff_subln_g, ret_gn_g, ret_gn_b, w_branch_a, w_branch_b, w_out, norm2_g, w_group_router, b_group_router, w_expert_router, b_expert_router, w_gate, w_up, w_down):
    assert x.shape[-1] == D_MODEL and norm1_g.shape[0] == 1, "single-layer, D_MODEL-wide input expected"
    lam4 = jnp.zeros((4, LANES), jnp.float32)
    lam4 = lam4.at[:, :DA_HALF].set(jnp.stack([lambda_q1[0], lambda_k1[0], lambda_q2[0], lambda_k2[0]]))
    return _layer(x, positions, norm1_g[0], w_in[0], q_norm_g[0], k_norm_g[0], lam4, diff_subln_g[0],
                  ret_gn_g[0], ret_gn_b[0], w_branch_a[0], w_branch_b[0], w_out[0], norm2_g[0],
                  w_group_router[0], b_group_router[0], w_expert_router[0], b_expert_router[0],
                  w_gate[0], w_up[0], w_down[0])
```

```python
import functools
import math

import jax
import jax.numpy as jnp
from jax import lax
from jax.experimental import pallas as pl
from jax.experimental.pallas import tpu as pltpu
from jax.experimental.pallas import tpu_sc as plsc

D_MODEL = 1024
DA_HEADS = 4
DA_HALF = 64
DA_VDIM = 2 * DA_HALF
DA_WIDTH = DA_HEADS * DA_VDIM
ROPE_THETA = 500000.0
ROPE_DIM = DA_HALF // 4
RET_HEADS = 4
RET_KDIM = 128
RET_VDIM = 128
RET_WIDTH = RET_HEADS * RET_VDIM
RET_THETA = 10000.0
N_GROUPS = 4
EXPERTS_PER_GROUP = 8
N_EXPERTS = N_GROUPS * EXPERTS_PER_GROUP
TOP_K = 2
EXPERT_FF = 512
EPS = 1e-6
LAMBDA_INIT = 0.8 - 0.6 * math.exp(-0.3 * 0)

LANES = 128
IN_COLS = 3 * DA_WIDTH + 4 * RET_WIDTH + 2 * D_MODEL
COL_QA, COL_KA, COL_VA = 0, DA_WIDTH, 2 * DA_WIDTH
COL_QR = 3 * DA_WIDTH
COL_KR = COL_QR + RET_WIDTH
COL_VR = COL_KR + RET_WIDTH
COL_GB = COL_VR + RET_WIDTH
COL_GATE_A = COL_GB + RET_WIDTH
COL_GATE_B = COL_GATE_A + D_MODEL

PROJ_ROWS = 512
PROJ_CHUNK = 256
ATT_TILE = 512
ATT_ROWS = 32
RET_CHUNK = 256
RET_UNROLL = 8
MOE_BLOCK = 512
PACKED = D_MODEL // 2
ROW_PIECES = 2
PIECE = PACKED // ROW_PIECES
SC_CORES = 2
SC_WINDOW = 128
BATCH_GROUPS = 1
VMEM_LIMIT = 56 * 1024 * 1024


def _dot(a, b):
    return jnp.dot(a, b, preferred_element_type=jnp.float32)


def _dot_nt(a, b):
    return lax.dot_general(a, b, (((1,), (1,)), ((), ())), preferred_element_type=jnp.float32)


def _dot_tn(a, b):
    return lax.dot_general(a, b, (((0,), (0,)), ((), ())), preferred_element_type=jnp.float32)


def _sigmoid(x):
    return 0.5 * jnp.tanh(0.5 * x) + 0.5


def _split3(x):
    a = x.astype(jnp.bfloat16)
    r = x - a.astype(jnp.float32)
    b = r.astype(jnp.bfloat16)
    c = (r - b.astype(jnp.float32)).astype(jnp.bfloat16)
    return a, b, c


def _in_proj_kernel(x_ref, pos_ref, g1_ref, w_ref, gsum_ref, gq_ref, gk_ref, fa_ref, fr_ref, sel_ref,
                    o_ref, h_scr):
    x = x_ref[...]
    h = x * lax.rsqrt(jnp.mean(x * x, axis=-1, keepdims=True) + EPS) * g1_ref[...]
    h_scr[...] = h.astype(jnp.bfloat16)
    rows = x.shape[0]
    pos = pos_ref[...].astype(jnp.float32)

    lane = lax.broadcasted_iota(jnp.int32, (rows, LANES), 1)
    half_a = ROPE_DIM // 2
    ang_a = fa_ref[...] * pos
    pad = jnp.zeros((LANES - 2 * half_a, rows), jnp.float32)
    t_a = jnp.concatenate([jnp.cos(ang_a), jnp.sin(ang_a), pad], axis=0).T
    tab = sum(_dot(part, sel_ref[...]) for part in _split3(t_a))
    lane64 = lane % DA_HALF
    c_a = tab[:, :LANES] + jnp.where(lane64 < ROPE_DIM, 0.0, 1.0)
    s_lo = tab[:, LANES:2 * LANES]
    s_hi = tab[:, 2 * LANES:]
    c_a2 = jnp.concatenate([c_a, c_a], axis=1)
    s_lo2 = jnp.concatenate([s_lo, s_lo], axis=1)
    s_hi2 = jnp.concatenate([s_hi, s_hi], axis=1)
    ang_r = fr_ref[...] * pos
    t_r = jnp.concatenate([jnp.cos(ang_r), jnp.sin(ang_r)], axis=0).T
    sw_r = pltpu.roll(t_r, RET_KDIM // 2, axis=1)
    first = lane < RET_KDIM // 2
    c_r = jnp.where(first, t_r, sw_r)
    s_r = jnp.where(first, -sw_r, t_r)
    c_r2 = jnp.concatenate([c_r, c_r], axis=1)
    s_r2 = jnp.concatenate([s_r, s_r], axis=1)

    def qk_norm_rope(y, g, scale):
        ss = y * y
        hi = ss.astype(jnp.bfloat16)
        lo = (ss - hi.astype(jnp.float32)).astype(jnp.bfloat16)
        gs = _dot(hi, gsum_ref[...]) + _dot(lo, gsum_ref[...])
        n = y * lax.rsqrt(gs * (1.0 / DA_HALF) + EPS) * g
        up = pltpu.roll(n, PROJ_CHUNK - half_a, axis=1)
        dn = pltpu.roll(n, half_a, axis=1)
        r = n * c_a2 + up * s_lo2 + dn * s_hi2
        return r * scale if scale != 1.0 else r

    def ret_rope(y, scale):
        halves = [pltpu.roll(y[:, i * LANES:(i + 1) * LANES], RET_KDIM // 2, axis=1)
                  for i in range(PROJ_CHUNK // LANES)]
        sw = jnp.concatenate(halves, axis=1)
        r = y * c_r2 + sw * s_r2
        return r * scale if scale != 1.0 else r

    for c in range(IN_COLS // PROJ_CHUNK):
        c0 = c * PROJ_CHUNK
        y = _dot(h_scr[...], w_ref[:, c0:c0 + PROJ_CHUNK])
        if c0 < COL_KA:
            y = qk_norm_rope(y, gq_ref[...], DA_HALF ** -0.5)
        elif c0 < COL_VA:
            y = qk_norm_rope(y, gk_ref[...], 1.0)
        elif c0 < COL_QR:
            pass
        elif c0 < COL_KR:
            y = ret_rope(y, 1.0)
        elif c0 < COL_VR:
            y = ret_rope(y, RET_KDIM ** -0.5)
        elif c0 < COL_GB:
            pass
        elif c0 < COL_GATE_A:
            y = y * _sigmoid(y)
        else:
            y = _sigmoid(y)
        o_ref[:, c0:c0 + PROJ_CHUNK] = y.astype(o_ref.dtype)


def _in_proj(x2, pos2, g1, w_in, gq, gk, row0, n):
    tm = min(PROJ_ROWS, n)
    blk0 = row0 // tm
    grp = jnp.arange(PROJ_CHUNK) // DA_HALF
    gsum = (grp[:, None] == grp[None, :]).astype(jnp.bfloat16)
    half_a = ROPE_DIM // 2
    fa = jnp.power(jnp.float32(ROPE_THETA), -2.0 * jnp.arange(half_a, dtype=jnp.float32) / ROPE_DIM)[:, None]
    half_r = RET_KDIM // 2
    fr = jnp.power(jnp.float32(RET_THETA), -2.0 * jnp.arange(half_r, dtype=jnp.float32) / RET_KDIM)[:, None]
    j = jnp.arange(LANES)[:, None]
    l64 = (jnp.arange(LANES) % DA_HALF)[None, :]
    sel_c = (j < half_a) & (l64 < ROPE_DIM) & (l64 % half_a == j)
    sel_lo = (j >= half_a) & (j < ROPE_DIM) & (l64 < half_a) & (l64 == j - half_a)
    sel_hi = (j >= half_a) & (j < ROPE_DIM) & (l64 >= half_a) & (l64 < ROPE_DIM) & (l64 == j)
    sel = jnp.concatenate([sel_c.astype(jnp.float32), -sel_lo.astype(jnp.float32),
                           sel_hi.astype(jnp.float32)], axis=1).astype(jnp.bfloat16)
    reps = PROJ_CHUNK // DA_HALF
    full = lambda shape: pl.BlockSpec(shape, lambda i: (0,) * len(shape))
    return pl.pallas_call(
        _in_proj_kernel,
        grid=(n // tm,),
        in_specs=[
            pl.BlockSpec((tm, D_MODEL), lambda i: (blk0 + i, 0)),
            pl.BlockSpec((1, tm), lambda i: (0, blk0 + i)),
            full((1, D_MODEL)),
            full((D_MODEL, IN_COLS)),
            full((PROJ_CHUNK, PROJ_CHUNK)),
            full((1, PROJ_CHUNK)),
            full((1, PROJ_CHUNK)),
            full((half_a, 1)),
            full((half_r, 1)),
            full((LANES, 3 * LANES)),
        ],
        out_specs=pl.BlockSpec((tm, IN_COLS), lambda i: (i, 0)),
        out_shape=jax.ShapeDtypeStruct((n, IN_COLS), jnp.bfloat16),
        scratch_shapes=[pltpu.VMEM((tm, D_MODEL), jnp.bfloat16)],
        compiler_params=pltpu.CompilerParams(dimension_semantics=("arbitrary",),
                                             vmem_limit_bytes=VMEM_LIMIT),
        name="in_proj",
    )(x2, pos2, g1.reshape(1, D_MODEL), w_in.astype(jnp.bfloat16), gsum,
      jnp.tile(gq, reps)[None, :], jnp.tile(gk, reps)[None, :], fa, fr, sel)


def _diff_attn_kernel(q_ref, k_ref, v_ref, lam_ref, gsub_ref, o_ref,
                      qs_scr, vx_scr, s0_scr, s1_scr, p_scr, m_scr, alpha_scr, acc_scr):
    i = pl.program_id(2)
    t = q_ref.shape[0]

    @pl.when(i == 0)
    def _():
        vx_scr[:, :DA_VDIM] = v_ref[...]
        vx_scr[:, DA_VDIM:] = jnp.ones((vx_scr.shape[0], LANES), vx_scr.dtype)

    q = q_ref[...]
    lane = lax.broadcasted_iota(jnp.int32, q.shape, 1)
    zero = jnp.zeros_like(q)
    qs_scr[:t] = jnp.where(lane < DA_HALF, q, zero)
    qs_scr[t:] = jnp.where(lane >= DA_HALF, q, zero)
    m_scr[...] = jnp.full(m_scr.shape, -jnp.inf, jnp.float32)
    acc_scr[...] = jnp.zeros(acc_scr.shape, jnp.float32)

    def scores(j, s_ref):
        start = pl.multiple_of(j * t, t)
        s_ref[...] = _dot_nt(qs_scr[...], k_ref[pl.ds(start, t), :])

    def softmax_pv(j, s_ref, masked):
        for c in range(2 * t // ATT_ROWS):
            rows = pl.ds(c * ATT_ROWS, ATT_ROWS)
            s = s_ref[rows, :]
            if masked:
                r = lax.broadcasted_iota(jnp.int32, s.shape, 0) + (c * ATT_ROWS) % t
                col = lax.broadcasted_iota(jnp.int32, s.shape, 1)
                s = jnp.where(col <= r, s, -jnp.inf)
            m_prev = m_scr[rows, :]
            m_new = jnp.maximum(m_prev, jnp.max(s, axis=-1, keepdims=True))
            alpha_scr[rows, :] = jnp.exp(m_prev - m_new)
            m_scr[rows, :] = m_new
            p = jnp.exp(s - jnp.concatenate([m_new] * (t // LANES), axis=1))
            p_scr[rows, :] = p.astype(p_scr.dtype)
        start = pl.multiple_of(j * t, t)
        pv = _dot(p_scr[...], vx_scr[pl.ds(start, t), :])
        alpha = alpha_scr[...]
        for half in range(2):
            cols = pl.ds(half * LANES, LANES)
            acc_scr[:, cols] = alpha * acc_scr[:, cols] + pv[:, half * LANES:(half + 1) * LANES]

    scores(0, s0_scr)

    def pair(jj, carry):
        j = 2 * jj
        scores(j + 1, s1_scr)
        softmax_pv(j, s0_scr, False)
        scores(j + 2, s0_scr)
        softmax_pv(j + 1, s1_scr, False)
        return carry

    lax.fori_loop(0, i // 2, pair, 0)

    @pl.when(i % 2 == 1)
    def _():
        scores(i, s1_scr)
        softmax_pv(i - 1, s0_scr, False)
        softmax_pv(i, s1_scr, True)

    @pl.when(i % 2 == 0)
    def _():
        softmax_pv(i, s0_scr, True)

    lam4 = lam_ref[...]
    lam = (jnp.exp(jnp.sum(lam4[0:1] * lam4[1:2], axis=-1, keepdims=True))
           - jnp.exp(jnp.sum(lam4[2:3] * lam4[3:4], axis=-1, keepdims=True)) + LAMBDA_INIT)
    o_all = acc_scr[:, :DA_VDIM] / acc_scr[:, DA_VDIM:]
    o = o_all[:t] - lam * o_all[t:]
    o = o * lax.rsqrt(jnp.mean(o * o, axis=-1, keepdims=True) + EPS) * gsub_ref[...] * (1.0 - LAMBDA_INIT)
    o_ref[...] = o.astype(o_ref.dtype)


def _diff_attn(proj, lam4, gsub, batch, seq):
    n = proj.shape[0]
    t = min(ATT_TILE, seq)
    nq = seq // t
    qb, kb, vb = COL_QA // LANES, COL_KA // LANES, COL_VA // LANES
    return pl.pallas_call(
        _diff_attn_kernel,
        grid=(batch, DA_HEADS, nq),
        in_specs=[
            pl.BlockSpec((t, LANES), lambda b, h, i: (b * nq + i, qb + h)),
            pl.BlockSpec((seq, LANES), lambda b, h, i: (b, kb + h)),
            pl.BlockSpec((seq, LANES), lambda b, h, i: (b, vb + h)),
            pl.BlockSpec((4, LANES), lambda b, h, i: (0, 0)),
            pl.BlockSpec((1, LANES), lambda b, h, i: (0, 0)),
        ],
        out_specs=pl.BlockSpec((t, LANES), lambda b, h, i: (b * nq + i, h)),
        out_shape=jax.ShapeDtypeStruct((n, DA_WIDTH), jnp.bfloat16),
        scratch_shapes=[pltpu.VMEM((2 * t, LANES), jnp.bfloat16),
                        pltpu.VMEM((seq, DA_VDIM + LANES), jnp.bfloat16),
                        pltpu.VMEM((2 * t, t), jnp.float32),
                        pltpu.VMEM((2 * t, t), jnp.float32),
                        pltpu.VMEM((2 * t, t), jnp.bfloat16),
                        pltpu.VMEM((2 * t, LANES), jnp.float32),
                        pltpu.VMEM((2 * t, LANES), jnp.float32),
                        pltpu.VMEM((2 * t, DA_VDIM + LANES), jnp.float32)],
        compiler_params=pltpu.CompilerParams(dimension_semantics=("arbitrary",) * 3,
                                             vmem_limit_bytes=VMEM_LIMIT),
        name="diff_attn",
    )(proj, proj, proj, lam4, gsub)


def _retention_kernel(q_ref, k_ref, v_ref, g_ref, gng_ref, gnb_ref, o_ref, r_scr, *, chunk):
    hf = jnp.full((1, 1), pl.program_id(1), jnp.int32).astype(jnp.float32)
    log_g = jnp.log1p(-jnp.exp2(-5.0 - hf))
    ri = lax.broadcasted_iota(jnp.int32, (chunk, chunk), 0)
    ci = lax.broadcasted_iota(jnp.int32, (chunk, chunk), 1)
    rel = (ri - ci).astype(jnp.float32)
    dmask = jnp.where(rel >= 0, jnp.exp(jnp.maximum(rel, 0.0) * log_g), 0.0)
    idx = lax.broadcasted_iota(jnp.int32, (chunk, 1), 0).astype(jnp.float32)
    zeta = jnp.exp((chunk - 1 - idx) * log_g)
    xi = jnp.exp((idx + 1.0) * log_g)
    g_chunk = jnp.exp(chunk * log_g)
    r_scr[...] = jnp.zeros(r_scr.shape, jnp.float32)
    gng = gng_ref[...]
    gnb = gnb_ref[...]

    def body(n, carry):
        start = pl.multiple_of(n * chunk, chunk)
        q = q_ref[pl.ds(start, chunk), :]
        k = k_ref[pl.ds(start, chunk), :]
        v = v_ref[pl.ds(start, chunk), :]
        s = _dot_nt(q, k) * dmask
        r_old = r_scr[...]
        o = _dot(s.astype(jnp.bfloat16), v) + xi * _dot(q, r_old.astype(jnp.bfloat16))
        kz = (k.astype(jnp.float32) * zeta).astype(jnp.bfloat16)
        r_scr[...] = g_chunk * r_old + _dot_tn(kz, v)
        mu = jnp.mean(o, axis=-1, keepdims=True)
        d = o - mu
        var = jnp.mean(d * d, axis=-1, keepdims=True)
        y = d * lax.rsqrt(var + EPS) * gng + gnb
        y = y * g_ref[pl.ds(start, chunk), :].astype(jnp.float32)
        o_ref[pl.ds(start, chunk), :] = y.astype(o_ref.dtype)
        return carry

    lax.fori_loop(0, q_ref.shape[0] // chunk, body, 0, unroll=RET_UNROLL)


def _retention(proj, gn_g, gn_b, batch, seq):
    n = proj.shape[0]
    chunk = min(RET_CHUNK, seq)
    col = lambda c0: (lambda b, h: (b, c0 // LANES + h))
    return pl.pallas_call(
        functools.partial(_retention_kernel, chunk=chunk),
        grid=(batch, RET_HEADS),
        in_specs=[
            pl.BlockSpec((seq, LANES), col(COL_QR)),
            pl.BlockSpec((seq, LANES), col(COL_KR)),
            pl.BlockSpec((seq, LANES), col(COL_VR)),
            pl.BlockSpec((seq, LANES), col(COL_GB)),
            pl.BlockSpec((1, LANES), lambda b, h: (0, h)),
            pl.BlockSpec((1, LANES), lambda b, h: (0, h)),
        ],
        out_specs=pl.BlockSpec((seq, LANES), lambda b, h: (b, h)),
        out_shape=jax.ShapeDtypeStruct((n, RET_WIDTH), jnp.bfloat16),
        scratch_shapes=[pltpu.VMEM((RET_KDIM, RET_VDIM), jnp.float32)],
        compiler_params=pltpu.CompilerParams(dimension_semantics=("arbitrary",) * 2,
                                             vmem_limit_bytes=VMEM_LIMIT),
        name="retention",
    )(proj, proj, proj, proj, gn_g.reshape(1, RET_WIDTH), gn_b.reshape(1, RET_WIDTH))


def _merge_kernel(x_ref, oa_ref, ob_ref, sa0_ref, sa1_ref, sb0_ref, sb1_ref, wa_ref, wb_ref, wo_ref,
                  g2_ref, wr_hi_ref, wr_lo_ref, br_ref, tri_ref, x1_ref, h2_ref, route_ref, counts_ref,
                  base_scr):
    ya = _dot(oa_ref[...], wa_ref[...])
    yb = _dot(ob_ref[...], wb_ref[...])
    sa = jnp.concatenate([sa0_ref[...], sa1_ref[...]], axis=1).astype(jnp.float32)
    sb = jnp.concatenate([sb0_ref[...], sb1_ref[...]], axis=1).astype(jnp.float32)
    merged = sa * ya + sb * yb
    x1 = x_ref[...] + _dot(merged.astype(jnp.bfloat16), wo_ref[...])
    x1_ref[...] = x1
    h2 = x1 * lax.rsqrt(jnp.mean(x1 * x1, axis=-1, keepdims=True) + EPS) * g2_ref[...]
    _store_packed(h2_ref, h2)

    hi = h2.astype(jnp.bfloat16)
    lo = (h2 - hi.astype(jnp.float32)).astype(jnp.bfloat16)
    logits = (_dot(hi, wr_hi_ref[...]) + _dot(lo, wr_hi_ref[...]) + _dot(hi, wr_lo_ref[...])
              + br_ref[...])
    lane = lax.broadcasted_iota(jnp.int32, logits.shape, 1)
    neg = -jnp.inf
    gl = jnp.where(lane < N_GROUPS, logits, neg)
    gmax = jnp.max(gl, axis=-1, keepdims=True)
    g_idx = jnp.min(jnp.where(gl == gmax, lane, LANES), axis=-1, keepdims=True)
    p_g = 1.0 / jnp.sum(jnp.exp(gl - gmax), axis=-1, keepdims=True)
    e_lo = N_GROUPS + EXPERTS_PER_GROUP * g_idx
    el = jnp.where((lane >= e_lo) & (lane < e_lo + EXPERTS_PER_GROUP), logits, neg)
    v1 = jnp.max(el, axis=-1, keepdims=True)
    i1 = jnp.min(jnp.where(el == v1, lane, LANES), axis=-1, keepdims=True)
    el2 = jnp.where(lane == i1, neg, el)
    v2 = jnp.max(el2, axis=-1, keepdims=True)
    i2 = jnp.min(jnp.where(el2 == v2, lane, LANES), axis=-1, keepdims=True)
    t = jnp.exp(v2 - v1)
    w1 = p_g / (1.0 + t)
    w2 = p_g * t / (1.0 + t)
    e1 = i1 - N_GROUPS
    e2 = i2 - N_GROUPS

    @pl.when(pl.program_id(0) == 0)
    def _():
        base_scr[...] = jnp.zeros(base_scr.shape, jnp.float32)

    oh1 = lane == e1
    oh2 = lane == e2
    picked = jnp.where(oh1 | oh2, 1.0, 0.0)
    before = _dot(tri_ref[...], picked.astype(jnp.bfloat16)) + base_scr[0:1, :]
    rank1 = jnp.sum(jnp.where(oh1, before, 0.0), axis=-1, keepdims=True)
    rank2 = jnp.sum(jnp.where(oh2, before, 0.0), axis=-1, keepdims=True)
    base_scr[...] = base_scr[...] + jnp.sum(picked, axis=0, keepdims=True)
    counts_ref[...] = base_scr[...]

    cols = [e1.astype(jnp.float32), e2.astype(jnp.float32), w1, w2, rank1, rank2]
    route = jnp.zeros(logits.shape, jnp.float32)
    for c, val in enumerate(cols):
        route = jnp.where(lane == c, val, route)
    route_ref[...] = route


def _merge(x2, oa, ob, proj, wa, wb, wo, g2, w_gr, b_gr, w_er, b_er, row0):
    n = oa.shape[0]
    tm = min(PROJ_ROWS, n)
    blk0 = row0 // tm
    half = D_MODEL // 2
    wr = jnp.zeros((D_MODEL, LANES), jnp.float32)
    wr = wr.at[:, :N_GROUPS].set(w_gr).at[:, N_GROUPS:N_GROUPS + N_EXPERTS].set(w_er)
    wr_hi = wr.astype(jnp.bfloat16)
    wr_lo = (wr - wr_hi.astype(jnp.float32)).astype(jnp.bfloat16)
    br = jnp.zeros((1, LANES), jnp.float32)
    br = br.at[0, :N_GROUPS].set(b_gr).at[0, N_GROUPS:N_GROUPS + N_EXPERTS].set(b_er)
    tri = (jnp.arange(tm)[:, None] > jnp.arange(tm)[None, :]).astype(jnp.bfloat16)
    full = lambda shape: pl.BlockSpec(shape, lambda i: (0,) * len(shape))
    gate = lambda c0: pl.BlockSpec((tm, half), lambda i: (i, c0 // half))
    return pl.pallas_call(
        _merge_kernel,
        grid=(n // tm,),
        in_specs=[
            pl.BlockSpec((tm, D_MODEL), lambda i: (blk0 + i, 0)),
            pl.BlockSpec((tm, DA_WIDTH), lambda i: (i, 0)),
            pl.BlockSpec((tm, RET_WIDTH), lambda i: (i, 0)),
            gate(COL_GATE_A), gate(COL_GATE_A + half), gate(COL_GATE_B), gate(COL_GATE_B + half),
            full((DA_WIDTH, D_MODEL)), full((RET_WIDTH, D_MODEL)), full((D_MODEL, D_MODEL)),
            full((1, D_MODEL)), full((D_MODEL, LANES)), full((D_MODEL, LANES)), full((1, LANES)),
            full((tm, tm)),
        ],
        out_specs=[
            pl.BlockSpec((tm, D_MODEL), lambda i: (i, 0)),
            pl.BlockSpec((ROW_PIECES, tm, PIECE), lambda i: (0, i, 0)),
            pl.BlockSpec((tm, LANES), lambda i: (i, 0)),
            pl.BlockSpec((8, LANES), lambda i: (0, 0)),
        ],
        out_shape=[
            jax.ShapeDtypeStruct((n, D_MODEL), jnp.float32),
            jax.ShapeDtypeStruct((ROW_PIECES, n, PIECE), jnp.uint32),
            jax.ShapeDtypeStruct((n, LANES), jnp.float32),
            jax.ShapeDtypeStruct((8, LANES), jnp.float32),
        ],
        scratch_shapes=[pltpu.VMEM((8, LANES), jnp.float32)],
        compiler_params=pltpu.CompilerParams(dimension_semantics=("arbitrary",),
                                             vmem_limit_bytes=VMEM_LIMIT),
        name="merge",
    )(x2, oa, ob, proj, proj, proj, proj, wa.astype(jnp.bfloat16), wb.astype(jnp.bfloat16),
      wo.astype(jnp.bfloat16), g2.reshape(1, D_MODEL), wr_hi, wr_lo, br, tri)


def _sc_mesh():
    return plsc.VectorSubcoreMesh(core_axis_name="c", subcore_axis_name="s")


def _sc_scatter_rows(src, idx, out_rows, src_block):
    steps = idx.shape[1] // SC_WINDOW
    per_core = steps // SC_CORES

    @pl.kernel(out_type=jax.ShapeDtypeStruct((out_rows, PIECE), src.dtype), mesh=_sc_mesh())
    def scatter(src_hbm, idx_hbm, out_hbm):
        def body(src_vmem, idx_vmem):
            pltpu.sync_copy(src_vmem, out_hbm.at[idx_vmem.at[0]])

        pltpu.emit_pipeline(
            body,
            grid=(SC_CORES, per_core),
            in_specs=[pl.BlockSpec((SC_WINDOW, PIECE), lambda c, i: (src_block(c * per_core + i), 0)),
                      pl.BlockSpec((1, SC_WINDOW), lambda c, i: (0, c * per_core + i))],
            out_specs=[],
            core_axis_name=("c", "s"),
            dimension_semantics=(pltpu.PARALLEL, pltpu.PARALLEL),
        )(src_hbm, idx_hbm)

    return scatter(src, idx)


def _sc_gather_rows(table, idx):
    num = idx.shape[1]
    per_core = num // SC_WINDOW // SC_CORES

    @pl.kernel(out_type=jax.ShapeDtypeStruct((num, PIECE), table.dtype), mesh=_sc_mesh())
    def gather(table_hbm, idx_hbm, out_hbm):
        def body(idx_vmem, out_vmem):
            pltpu.sync_copy(table_hbm.at[idx_vmem.at[0]], out_vmem)

        pltpu.emit_pipeline(
            body,
            grid=(SC_CORES, per_core),
            in_specs=[pl.BlockSpec((1, SC_WINDOW), lambda c, i: (0, c * per_core + i))],
            out_specs=[pl.BlockSpec((SC_WINDOW, PIECE), lambda c, i: (c * per_core + i, 0))],
            core_axis_name=("c", "s"),
            dimension_semantics=(pltpu.PARALLEL, pltpu.PARALLEL),
        )(idx_hbm, out_hbm)

    return gather(table, idx)


def _store_packed(ref, val):
    as_bits = lambda v: lax.bitcast_convert_type(v.astype(jnp.bfloat16).astype(jnp.float32), jnp.uint32)
    words = (as_bits(val[:, :PACKED]) >> 16) | (as_bits(val[:, PACKED:]) & jnp.uint32(0xFFFF0000))
    for j in range(ROW_PIECES):
        ref[j] = words[:, j * PIECE:(j + 1) * PIECE]


def _load_packed(ref):
    words = jnp.concatenate([ref[j] for j in range(ROW_PIECES)], axis=1)
    low = lax.bitcast_convert_type(words << 16, jnp.float32)
    high = lax.bitcast_convert_type(words & jnp.uint32(0xFFFF0000), jnp.float32)
    return jnp.concatenate([low, high], axis=1)


def _expert_kernel(blk_e_ref, n_used_ref, nxt_ref, run_ref, x_ref, wg_hbm, wu_hbm, wd_hbm, o_ref,
                   wg_stage, wu_stage, wd_stage, wg_scr, wu_scr, wd_scr, sem):
    i = pl.program_id(0)
    used = i < n_used_ref[0]

    def weight_copies(e, s):
        return (pltpu.make_async_copy(wg_hbm.at[e], wg_stage.at[s], sem.at[s, 0]),
                pltpu.make_async_copy(wu_hbm.at[e], wu_stage.at[s], sem.at[s, 1]),
                pltpu.make_async_copy(wd_hbm.at[e], wd_stage.at[s], sem.at[s, 2]))

    @pl.when(i == 0)
    def _():
        for c in weight_copies(blk_e_ref[0], 0):
            c.start()

    @pl.when(used & ((i == 0) | (blk_e_ref[i] != blk_e_ref[jnp.maximum(i - 1, 0)])))
    def _():
        s = run_ref[i] % 2
        for c in weight_copies(blk_e_ref[i], s):
            c.wait()
        wg_scr[...] = wg_stage[s].astype(jnp.bfloat16)
        wu_scr[...] = wu_stage[s].astype(jnp.bfloat16)
        wd_scr[...] = wd_stage[s].astype(jnp.bfloat16)

        @pl.when(nxt_ref[i] >= 0)
        def _():
            for c in weight_copies(nxt_ref[i], 1 - s):
                c.start()

    @pl.when(used)
    def _():
        x = _load_packed(x_ref).astype(jnp.bfloat16)
        a = _dot(x, wg_scr[...])
        u = _dot(x, wu_scr[...])
        hmid = (a * _sigmoid(a) * u).astype(jnp.bfloat16)
        _store_packed(o_ref, _dot(hmid, wd_scr[...]))

    @pl.when(jnp.logical_not(used))
    def _():
        o_ref[...] = jnp.zeros(o_ref.shape, o_ref.dtype)


def _experts(xs, blk_expert, n_used, w_gate, w_up, w_down):
    p = xs.shape[1]
    nblk = p // MOE_BLOCK
    idx = jnp.arange(nblk, dtype=jnp.int32)
    starts = (idx < n_used[0]) & ((idx == 0) | (blk_expert != jnp.roll(blk_expert, 1)))
    run = jnp.cumsum(starts.astype(jnp.int32)) - 1
    next_start = lax.cummin(jnp.where(starts, idx, nblk)[::-1])[::-1]
    after = jnp.concatenate([next_start[1:], jnp.full((1,), nblk, jnp.int32)])
    nxt = jnp.where(after < nblk, blk_expert[jnp.minimum(after, nblk - 1)], -1).astype(jnp.int32)
    live = lambda i, be, nu, nx, rn: jnp.minimum(i, nu[0] - 1)
    any_spec = pl.BlockSpec(memory_space=pl.ANY)
    return pl.pallas_call(
        _expert_kernel,
        grid_spec=pltpu.PrefetchScalarGridSpec(
            num_scalar_prefetch=4,
            grid=(nblk,),
            in_specs=[
                pl.BlockSpec((ROW_PIECES, MOE_BLOCK, PIECE), lambda i, be, nu, nx, rn: (0, live(i, be, nu, nx, rn), 0)),
                any_spec, any_spec, any_spec,
            ],
            out_specs=pl.BlockSpec((ROW_PIECES, MOE_BLOCK, PIECE), lambda i, be, nu, nx, rn: (0, i, 0)),
            scratch_shapes=[pltpu.VMEM((2, D_MODEL, EXPERT_FF), jnp.float32),
                            pltpu.VMEM((2, D_MODEL, EXPERT_FF), jnp.float32),
                            pltpu.VMEM((2, EXPERT_FF, D_MODEL), jnp.float32),
                            pltpu.VMEM((D_MODEL, EXPERT_FF), jnp.bfloat16),
                            pltpu.VMEM((D_MODEL, EXPERT_FF), jnp.bfloat16),
                            pltpu.VMEM((EXPERT_FF, D_MODEL), jnp.bfloat16),
                            pltpu.SemaphoreType.DMA((2, 3))],
        ),
        out_shape=jax.ShapeDtypeStruct((ROW_PIECES, p, PIECE), jnp.uint32),
        compiler_params=pltpu.CompilerParams(dimension_semantics=("arbitrary",),
                                             vmem_limit_bytes=VMEM_LIMIT),
        name="experts",
    )(blk_expert, n_used, nxt, run.astype(jnp.int32), xs, w_gate, w_up, w_down)


def _combine_kernel(x1_ref, route_ref, y0_ref, y1_ref, *rest):
    o_ref = rest[-1]
    route = route_ref[...]
    o_ref[...] = x1_ref[...] + route[:, 2:3] * _load_packed(y0_ref) + route[:, 3:4] * _load_packed(y1_ref)


def _combine(x1, yg, route, row0, n_total, out_prev):
    n = x1.shape[0]
    tm = min(PROJ_ROWS, n)
    blk0 = row0 // tm
    prev = () if out_prev is None else (out_prev,)
    return pl.pallas_call(
        _combine_kernel,
        grid=(n // tm,),
        in_specs=[
            pl.BlockSpec((tm, D_MODEL), lambda i: (i, 0)),
            pl.BlockSpec((tm, LANES), lambda i: (i, 0)),
            pl.BlockSpec((ROW_PIECES, tm, PIECE), lambda i: (0, i, 0)),
            pl.BlockSpec((ROW_PIECES, tm, PIECE), lambda i: (0, i + n // tm, 0)),
        ] + [pl.BlockSpec(memory_space=pl.ANY)] * len(prev),
        out_specs=pl.BlockSpec((tm, D_MODEL), lambda i: (blk0 + i, 0)),
        out_shape=jax.ShapeDtypeStruct((n_total, D_MODEL), jnp.float32),
        input_output_aliases={4: 0} if prev else {},
        compiler_params=pltpu.CompilerParams(dimension_semantics=("arbitrary",),
                                             vmem_limit_bytes=VMEM_LIMIT),
        name="combine",
    )(x1, route, yg, yg, *prev)


def _dispatch_plan(route, counts, n):
    counts = counts[0, :N_EXPERTS].astype(jnp.int32)
    padded = ((counts + MOE_BLOCK - 1) // MOE_BLOCK) * MOE_BLOCK
    seg_end = jnp.cumsum(padded).astype(jnp.int32)
    seg_start = seg_end - padded
    e = route[:, 0:TOP_K].astype(jnp.int32)
    rank = route[:, 4:4 + TOP_K].astype(jnp.int32)
    dest = (seg_start[e] + rank).T.reshape(-1)
    p = n * TOP_K + N_EXPERTS * MOE_BLOCK
    slot = (dest[None, :] + (jnp.arange(ROW_PIECES, dtype=jnp.int32) * p)[:, None]).reshape(1, -1)
    blk_start = jnp.arange(p // MOE_BLOCK, dtype=jnp.int32) * MOE_BLOCK
    blk_expert = jnp.sum((seg_end[None, :] <= blk_start[:, None]).astype(jnp.int32), axis=1)
    blk_expert = jnp.minimum(blk_expert, N_EXPERTS - 1)
    n_used = (seg_end[-1] // MOE_BLOCK).reshape(1)
    return slot, blk_expert, n_used, p


def _layer_rows(x2, pos2, row0, batch, seq, out_prev, norm1_g, w_in, q_norm_g, k_norm_g, lam4, diff_subln_g,
                ret_gn_g, ret_gn_b, w_branch_a, w_branch_b, w_out, norm2_g, w_gr, b_gr, w_er, b_er,
                w_gate, w_up, w_down):
    n = batch * seq
    proj = _in_proj(x2, pos2, norm1_g, w_in, q_norm_g, k_norm_g, row0, n)
    oa = _diff_attn(proj, lam4, diff_subln_g.reshape(1, DA_VDIM), batch, seq)
    ob = _retention(proj, ret_gn_g, ret_gn_b, batch, seq)
    x1, h2, route, counts = _merge(x2, oa, ob, proj, w_branch_a, w_branch_b, w_out, norm2_g,
                                   w_gr, b_gr, w_er, b_er, row0)
    slot, blk_expert, n_used, p = _dispatch_plan(route, counts, n)
    win_n = n // SC_WINDOW
    src_block = lambda s: (s // (TOP_K * win_n)) * win_n + s % win_n
    xs = _sc_scatter_rows(h2.reshape(ROW_PIECES * n, PIECE), slot, ROW_PIECES * p, src_block)
    ys = _experts(xs.reshape(ROW_PIECES, p, PIECE), blk_expert, n_used, w_gate, w_up, w_down)
    yg = _sc_gather_rows(ys.reshape(ROW_PIECES * p, PIECE), slot)
    return _combine(x1, yg.reshape(ROW_PIECES, TOP_K * n, PIECE), route, row0, x2.shape[0], out_prev)


def _layer(x, positions, *weights):
    batch, seq, _ = x.shape
    n = batch * seq
    x2 = x.reshape(n, D_MODEL)
    pos2 = positions.reshape(1, n)
    groups = BATCH_GROUPS if batch % BATCH_GROUPS == 0 else 1
    per = batch // groups
    out = None
    for g in range(groups):
        out = _layer_rows(x2, pos2, g * per * seq, per, seq, out, *weights)
    return out.reshape(batch, seq, D_MODEL)


def kernel(x, positions, norm1_g, w_in, q_norm_g, k_norm_g, lambda_q1, lambda_k1, lambda_q2, lambda_k2, diff_subln_g, ret_gn_g, ret_gn_b, w_branch_a, w_branch_b, w_out, norm2_g, w_group_router, b_group_router, w_expert_router, b_expert_router, w_gate, w_up, w_down):
    assert x.shape[-1] == D_MODEL and norm1_g.shape[0] == 1, "single-layer, D_MODEL-wide input expected"
    lam4 = jnp.zeros((4, LANES), jnp.float32)
    lam4 = lam4.at[:, :DA_HALF].set(jnp.stack([lambda_q1[0], lambda_k1[0], lambda_q2[0], lambda_k2[0]]))
    return _layer(x, positions, norm1_g[0], w_in[0], q_norm_g[0], k_norm_g[0], lam4, diff_subln_g[0],
                  ret_gn_g[0], ret_gn_b[0], w_branch_a[0], w_branch_b[0], w_out[0], norm2_g[0],
                  w_group_router[0], b_group_router[0], w_expert_router[0], b_expert_router[0],
                  w_gate[0], w_up[0], w_down[0])
```

```python
import functools
import math

import jax
import jax.numpy as jnp
from jax import lax
from jax.experimental import pallas as pl
from jax.experimental.pallas import tpu as pltpu
from jax.experimental.pallas import tpu_sc as plsc

D_MODEL = 1024
DA_HEADS = 4
DA_HALF = 64
DA_VDIM = 2 * DA_HALF
DA_WIDTH = DA_HEADS * DA_VDIM
ROPE_THETA = 500000.0
ROPE_DIM = DA_HALF // 4
RET_HEADS = 4
RET_KDIM = 128
RET_VDIM = 128
RET_WIDTH = RET_HEADS * RET_VDIM
RET_THETA = 10000.0
N_GROUPS = 4
EXPERTS_PER_GROUP = 8
N_EXPERTS = N_GROUPS * EXPERTS_PER_GROUP
TOP_K = 2
EXPERT_FF = 512
EPS = 1e-6
LAMBDA_INIT = 0.8 - 0.6 * math.exp(-0.3 * 0)

LANES = 128
IN_COLS = 3 * DA_WIDTH + 4 * RET_WIDTH + 2 * D_MODEL
COL_QA, COL_KA, COL_VA = 0, DA_WIDTH, 2 * DA_WIDTH
COL_QR = 3 * DA_WIDTH
COL_KR = COL_QR + RET_WIDTH
COL_VR = COL_KR + RET_WIDTH
COL_GB = COL_VR + RET_WIDTH
COL_GATE_A = COL_GB + RET_WIDTH
COL_GATE_B = COL_GATE_A + D_MODEL

PROJ_ROWS = 512
PROJ_CHUNK = 256
ATT_TILE = 512
ATT_ROWS = 32
RET_CHUNK = 256
RET_UNROLL = 8
MOE_BLOCK = 512
PACKED = D_MODEL // 2
ROW_PIECES = 2
PIECE = PACKED // ROW_PIECES
SC_CORES = 2
SC_WINDOW = 128
COMBINE_PARTS = 2
BATCH_GROUPS = 1
VMEM_LIMIT = 56 * 1024 * 1024


def _dot(a, b):
    return jnp.dot(a, b, preferred_element_type=jnp.float32)


def _dot_nt(a, b):
    return lax.dot_general(a, b, (((1,), (1,)), ((), ())), preferred_element_type=jnp.float32)


def _dot_tn(a, b):
    return lax.dot_general(a, b, (((0,), (0,)), ((), ())), preferred_element_type=jnp.float32)


def _sigmoid(x):
    return 0.5 * jnp.tanh(0.5 * x) + 0.5


def _split3(x):
    a = x.astype(jnp.bfloat16)
    r = x - a.astype(jnp.float32)
    b = r.astype(jnp.bfloat16)
    c = (r - b.astype(jnp.float32)).astype(jnp.bfloat16)
    return a, b, c


def _in_proj_kernel(x_ref, pos_ref, g1_ref, w_ref, gsum_ref, gq_ref, gk_ref, fa_ref, fr_ref, sel_ref,
                    o_ref, h_scr):
    x = x_ref[...]
    h = x * lax.rsqrt(jnp.mean(x * x, axis=-1, keepdims=True) + EPS) * g1_ref[...]
    h_scr[...] = h.astype(jnp.bfloat16)
    rows = x.shape[0]
    pos = pos_ref[...].astype(jnp.float32)

    lane = lax.broadcasted_iota(jnp.int32, (rows, LANES), 1)
    half_a = ROPE_DIM // 2
    ang_a = fa_ref[...] * pos
    pad = jnp.zeros((LANES - 2 * half_a, rows), jnp.float32)
    t_a = jnp.concatenate([jnp.cos(ang_a), jnp.sin(ang_a), pad], axis=0).T
    tab = sum(_dot(part, sel_ref[...]) for part in _split3(t_a))
    lane64 = lane % DA_HALF
    c_a = tab[:, :LANES] + jnp.where(lane64 < ROPE_DIM, 0.0, 1.0)
    s_lo = tab[:, LANES:2 * LANES]
    s_hi = tab[:, 2 * LANES:]
    c_a2 = jnp.concatenate([c_a, c_a], axis=1)
    s_lo2 = jnp.concatenate([s_lo, s_lo], axis=1)
    s_hi2 = jnp.concatenate([s_hi, s_hi], axis=1)
    ang_r = fr_ref[...] * pos
    t_r = jnp.concatenate([jnp.cos(ang_r), jnp.sin(ang_r)], axis=0).T
    sw_r = pltpu.roll(t_r, RET_KDIM // 2, axis=1)
    first = lane < RET_KDIM // 2
    c_r = jnp.where(first, t_r, sw_r)
    s_r = jnp.where(first, -sw_r, t_r)
    c_r2 = jnp.concatenate([c_r, c_r], axis=1)
    s_r2 = jnp.concatenate([s_r, s_r], axis=1)

    def qk_norm_rope(y, g, scale):
        ss = y * y
        hi = ss.astype(jnp.bfloat16)
        lo = (ss - hi.astype(jnp.float32)).astype(jnp.bfloat16)
        gs = _dot(hi, gsum_ref[...]) + _dot(lo, gsum_ref[...])
        n = y * lax.rsqrt(gs * (1.0 / DA_HALF) + EPS) * g
        up = pltpu.roll(n, PROJ_CHUNK - half_a, axis=1)
        dn = pltpu.roll(n, half_a, axis=1)
        r = n * c_a2 + up * s_lo2 + dn * s_hi2
        return r * scale if scale != 1.0 else r

    def ret_rope(y, scale):
        halves = [pltpu.roll(y[:, i * LANES:(i + 1) * LANES], RET_KDIM // 2, axis=1)
                  for i in range(PROJ_CHUNK // LANES)]
        sw = jnp.concatenate(halves, axis=1)
        r = y * c_r2 + sw * s_r2
        return r * scale if scale != 1.0 else r

    for c in range(IN_COLS // PROJ_CHUNK):
        c0 = c * PROJ_CHUNK
        y = _dot(h_scr[...], w_ref[:, c0:c0 + PROJ_CHUNK])
        if c0 < COL_KA:
            y = qk_norm_rope(y, gq_ref[...], DA_HALF ** -0.5)
        elif c0 < COL_VA:
            y = qk_norm_rope(y, gk_ref[...], 1.0)
        elif c0 < COL_QR:
            pass
        elif c0 < COL_KR:
            y = ret_rope(y, 1.0)
        elif c0 < COL_VR:
            y = ret_rope(y, RET_KDIM ** -0.5)
        elif c0 < COL_GB:
            pass
        elif c0 < COL_GATE_A:
            y = y * _sigmoid(y)
        else:
            y = _sigmoid(y)
        o_ref[:, c0:c0 + PROJ_CHUNK] = y.astype(o_ref.dtype)


def _in_proj(x2, pos2, g1, w_in, gq, gk, row0, n):
    tm = min(PROJ_ROWS, n)
    blk0 = row0 // tm
    grp = jnp.arange(PROJ_CHUNK) // DA_HALF
    gsum = (grp[:, None] == grp[None, :]).astype(jnp.bfloat16)
    half_a = ROPE_DIM // 2
    fa = jnp.power(jnp.float32(ROPE_THETA), -2.0 * jnp.arange(half_a, dtype=jnp.float32) / ROPE_DIM)[:, None]
    half_r = RET_KDIM // 2
    fr = jnp.power(jnp.float32(RET_THETA), -2.0 * jnp.arange(half_r, dtype=jnp.float32) / RET_KDIM)[:, None]
    j = jnp.arange(LANES)[:, None]
    l64 = (jnp.arange(LANES) % DA_HALF)[None, :]
    sel_c = (j < half_a) & (l64 < ROPE_DIM) & (l64 % half_a == j)
    sel_lo = (j >= half_a) & (j < ROPE_DIM) & (l64 < half_a) & (l64 == j - half_a)
    sel_hi = (j >= half_a) & (j < ROPE_DIM) & (l64 >= half_a) & (l64 < ROPE_DIM) & (l64 == j)
    sel = jnp.concatenate([sel_c.astype(jnp.float32), -sel_lo.astype(jnp.float32),
                           sel_hi.astype(jnp.float32)], axis=1).astype(jnp.bfloat16)
    reps = PROJ_CHUNK // DA_HALF
    full = lambda shape: pl.BlockSpec(shape, lambda i: (0,) * len(shape))
    return pl.pallas_call(
        _in_proj_kernel,
        grid=(n // tm,),
        in_specs=[
            pl.BlockSpec((tm, D_MODEL), lambda i: (blk0 + i, 0)),
            pl.BlockSpec((1, tm), lambda i: (0, blk0 + i)),
            full((1, D_MODEL)),
            full((D_MODEL, IN_COLS)),
            full((PROJ_CHUNK, PROJ_CHUNK)),
            full((1, PROJ_CHUNK)),
            full((1, PROJ_CHUNK)),
            full((half_a, 1)),
            full((half_r, 1)),
            full((LANES, 3 * LANES)),
        ],
        out_specs=pl.BlockSpec((tm, IN_COLS), lambda i: (i, 0)),
        out_shape=jax.ShapeDtypeStruct((n, IN_COLS), jnp.bfloat16),
        scratch_shapes=[pltpu.VMEM((tm, D_MODEL), jnp.bfloat16)],
        compiler_params=pltpu.CompilerParams(dimension_semantics=("arbitrary",),
                                             vmem_limit_bytes=VMEM_LIMIT),
        name="in_proj",
    )(x2, pos2, g1.reshape(1, D_MODEL), w_in.astype(jnp.bfloat16), gsum,
      jnp.tile(gq, reps)[None, :], jnp.tile(gk, reps)[None, :], fa, fr, sel)


def _diff_attn_kernel(q_ref, k_ref, v_ref, lam_ref, gsub_ref, o_ref,
                      qs_scr, vx_scr, s0_scr, s1_scr, p_scr, m_scr, alpha_scr, acc_scr):
    i = pl.program_id(2)
    t = q_ref.shape[0]

    @pl.when(i == 0)
    def _():
        vx_scr[:, :DA_VDIM] = v_ref[...]
        vx_scr[:, DA_VDIM:] = jnp.ones((vx_scr.shape[0], LANES), vx_scr.dtype)

    q = q_ref[...]
    lane = lax.broadcasted_iota(jnp.int32, q.shape, 1)
    zero = jnp.zeros_like(q)
    qs_scr[:t] = jnp.where(lane < DA_HALF, q, zero)
    qs_scr[t:] = jnp.where(lane >= DA_HALF, q, zero)
    m_scr[...] = jnp.full(m_scr.shape, -jnp.inf, jnp.float32)
    acc_scr[...] = jnp.zeros(acc_scr.shape, jnp.float32)

    def scores(j, s_ref):
        start = pl.multiple_of(j * t, t)
        s_ref[...] = _dot_nt(qs_scr[...], k_ref[pl.ds(start, t), :])

    def softmax_pv(j, s_ref, masked):
        for c in range(2 * t // ATT_ROWS):
            rows = pl.ds(c * ATT_ROWS, ATT_ROWS)
            s = s_ref[rows, :]
            if masked:
                r = lax.broadcasted_iota(jnp.int32, s.shape, 0) + (c * ATT_ROWS) % t
                col = lax.broadcasted_iota(jnp.int32, s.shape, 1)
                s = jnp.where(col <= r, s, -jnp.inf)
            m_prev = m_scr[rows, :]
            m_new = jnp.maximum(m_prev, jnp.max(s, axis=-1, keepdims=True))
            alpha_scr[rows, :] = jnp.exp(m_prev - m_new)
            m_scr[rows, :] = m_new
            p = jnp.exp(s - jnp.concatenate([m_new] * (t // LANES), axis=1))
            p_scr[rows, :] = p.astype(p_scr.dtype)
        start = pl.multiple_of(j * t, t)
        pv = _dot(p_scr[...], vx_scr[pl.ds(start, t), :])
        alpha = alpha_scr[...]
        for half in range(2):
            cols = pl.ds(half * LANES, LANES)
            acc_scr[:, cols] = alpha * acc_scr[:, cols] + pv[:, half * LANES:(half + 1) * LANES]

    scores(0, s0_scr)

    def pair(jj, carry):
        j = 2 * jj
        scores(j + 1, s1_scr)
        softmax_pv(j, s0_scr, False)
        scores(j + 2, s0_scr)
        softmax_pv(j + 1, s1_scr, False)
        return carry

    lax.fori_loop(0, i // 2, pair, 0)

    @pl.when(i % 2 == 1)
    def _():
        scores(i, s1_scr)
        softmax_pv(i - 1, s0_scr, False)
        softmax_pv(i, s1_scr, True)

    @pl.when(i % 2 == 0)
    def _():
        softmax_pv(i, s0_scr, True)

    lam4 = lam_ref[...]
    lam = (jnp.exp(jnp.sum(lam4[0:1] * lam4[1:2], axis=-1, keepdims=True))
           - jnp.exp(jnp.sum(lam4[2:3] * lam4[3:4], axis=-1, keepdims=True)) + LAMBDA_INIT)
    o_all = acc_scr[:, :DA_VDIM] / acc_scr[:, DA_VDIM:]
    o = o_all[:t] - lam * o_all[t:]
    o = o * lax.rsqrt(jnp.mean(o * o, axis=-1, keepdims=True) + EPS) * gsub_ref[...] * (1.0 - LAMBDA_INIT)
    o_ref[...] = o.astype(o_ref.dtype)


def _diff_attn(proj, lam4, gsub, batch, seq):
    n = proj.shape[0]
    t = min(ATT_TILE, seq)
    nq = seq // t
    qb, kb, vb = COL_QA // LANES, COL_KA // LANES, COL_VA // LANES
    return pl.pallas_call(
        _diff_attn_kernel,
        grid=(batch, DA_HEADS, nq),
        in_specs=[
            pl.BlockSpec((t, LANES), lambda b, h, i: (b * nq + i, qb + h)),
            pl.BlockSpec((seq, LANES), lambda b, h, i: (b, kb + h)),
            pl.BlockSpec((seq, LANES), lambda b, h, i: (b, vb + h)),
            pl.BlockSpec((4, LANES), lambda b, h, i: (0, 0)),
            pl.BlockSpec((1, LANES), lambda b, h, i: (0, 0)),
        ],
        out_specs=pl.BlockSpec((t, LANES), lambda b, h, i: (b * nq + i, h)),
        out_shape=jax.ShapeDtypeStruct((n, DA_WIDTH), jnp.bfloat16),
        scratch_shapes=[pltpu.VMEM((2 * t, LANES), jnp.bfloat16),
                        pltpu.VMEM((seq, DA_VDIM + LANES), jnp.bfloat16),
                        pltpu.VMEM((2 * t, t), jnp.float32),
                        pltpu.VMEM((2 * t, t), jnp.float32),
                        pltpu.VMEM((2 * t, t), jnp.bfloat16),
                        pltpu.VMEM((2 * t, LANES), jnp.float32),
                        pltpu.VMEM((2 * t, LANES), jnp.float32),
                        pltpu.VMEM((2 * t, DA_VDIM + LANES), jnp.float32)],
        compiler_params=pltpu.CompilerParams(dimension_semantics=("arbitrary",) * 3,
                                             vmem_limit_bytes=VMEM_LIMIT),
        name="diff_attn",
    )(proj, proj, proj, lam4, gsub)


def _retention_kernel(q_ref, k_ref, v_ref, g_ref, gng_ref, gnb_ref, o_ref, r_scr, *, chunk):
    hf = jnp.full((1, 1), pl.program_id(1), jnp.int32).astype(jnp.float32)
    log_g = jnp.log1p(-jnp.exp2(-5.0 - hf))
    ri = lax.broadcasted_iota(jnp.int32, (chunk, chunk), 0)
    ci = lax.broadcasted_iota(jnp.int32, (chunk, chunk), 1)
    rel = (ri - ci).astype(jnp.float32)
    dmask = jnp.where(rel >= 0, jnp.exp(jnp.maximum(rel, 0.0) * log_g), 0.0)
    idx = lax.broadcasted_iota(jnp.int32, (chunk, 1), 0).astype(jnp.float32)
    zeta = jnp.exp((chunk - 1 - idx) * log_g)
    xi = jnp.exp((idx + 1.0) * log_g)
    g_chunk = jnp.exp(chunk * log_g)
    r_scr[...] = jnp.zeros(r_scr.shape, jnp.float32)
    gng = gng_ref[...]
    gnb = gnb_ref[...]

    def body(n, carry):
        start = pl.multiple_of(n * chunk, chunk)
        q = q_ref[pl.ds(start, chunk), :]
        k = k_ref[pl.ds(start, chunk), :]
        v = v_ref[pl.ds(start, chunk), :]
        s = _dot_nt(q, k) * dmask
        r_old = r_scr[...]
        o = _dot(s.astype(jnp.bfloat16), v) + xi * _dot(q, r_old.astype(jnp.bfloat16))
        kz = (k.astype(jnp.float32) * zeta).astype(jnp.bfloat16)
        r_scr[...] = g_chunk * r_old + _dot_tn(kz, v)
        mu = jnp.mean(o, axis=-1, keepdims=True)
        d = o - mu
        var = jnp.mean(d * d, axis=-1, keepdims=True)
        y = d * lax.rsqrt(var + EPS) * gng + gnb
        y = y * g_ref[pl.ds(start, chunk), :].astype(jnp.float32)
        o_ref[pl.ds(start, chunk), :] = y.astype(o_ref.dtype)
        return carry

    lax.fori_loop(0, q_ref.shape[0] // chunk, body, 0, unroll=RET_UNROLL)


def _retention(proj, gn_g, gn_b, batch, seq):
    n = proj.shape[0]
    chunk = min(RET_CHUNK, seq)
    col = lambda c0: (lambda b, h: (b, c0 // LANES + h))
    return pl.pallas_call(
        functools.partial(_retention_kernel, chunk=chunk),
        grid=(batch, RET_HEADS),
        in_specs=[
            pl.BlockSpec((seq, LANES), col(COL_QR)),
            pl.BlockSpec((seq, LANES), col(COL_KR)),
            pl.BlockSpec((seq, LANES), col(COL_VR)),
            pl.BlockSpec((seq, LANES), col(COL_GB)),
            pl.BlockSpec((1, LANES), lambda b, h: (0, h)),
            pl.BlockSpec((1, LANES), lambda b, h: (0, h)),
        ],
        out_specs=pl.BlockSpec((seq, LANES), lambda b, h: (b, h)),
        out_shape=jax.ShapeDtypeStruct((n, RET_WIDTH), jnp.bfloat16),
        scratch_shapes=[pltpu.VMEM((RET_KDIM, RET_VDIM), jnp.float32)],
        compiler_params=pltpu.CompilerParams(dimension_semantics=("arbitrary",) * 2,
                                             vmem_limit_bytes=VMEM_LIMIT),
        name="retention",
    )(proj, proj, proj, proj, gn_g.reshape(1, RET_WIDTH), gn_b.reshape(1, RET_WIDTH))


def _merge_kernel(x_ref, oa_ref, ob_ref, sa0_ref, sa1_ref, sb0_ref, sb1_ref, wa_ref, wb_ref, wo_ref,
                  g2_ref, wr_hi_ref, wr_lo_ref, br_ref, tri_ref, x1_ref, h2_ref, route_ref, counts_ref,
                  base_scr):
    ya = _dot(oa_ref[...], wa_ref[...])
    yb = _dot(ob_ref[...], wb_ref[...])
    sa = jnp.concatenate([sa0_ref[...], sa1_ref[...]], axis=1).astype(jnp.float32)
    sb = jnp.concatenate([sb0_ref[...], sb1_ref[...]], axis=1).astype(jnp.float32)
    merged = sa * ya + sb * yb
    x1 = x_ref[...] + _dot(merged.astype(jnp.bfloat16), wo_ref[...])
    x1_ref[...] = x1
    h2 = x1 * lax.rsqrt(jnp.mean(x1 * x1, axis=-1, keepdims=True) + EPS) * g2_ref[...]
    _store_packed(h2_ref, h2)

    hi = h2.astype(jnp.bfloat16)
    lo = (h2 - hi.astype(jnp.float32)).astype(jnp.bfloat16)
    logits = (_dot(hi, wr_hi_ref[...]) + _dot(lo, wr_hi_ref[...]) + _dot(hi, wr_lo_ref[...])
              + br_ref[...])
    lane = lax.broadcasted_iota(jnp.int32, logits.shape, 1)
    neg = -jnp.inf
    gl = jnp.where(lane < N_GROUPS, logits, neg)
    gmax = jnp.max(gl, axis=-1, keepdims=True)
    g_idx = jnp.min(jnp.where(gl == gmax, lane, LANES), axis=-1, keepdims=True)
    p_g = 1.0 / jnp.sum(jnp.exp(gl - gmax), axis=-1, keepdims=True)
    e_lo = N_GROUPS + EXPERTS_PER_GROUP * g_idx
    el = jnp.where((lane >= e_lo) & (lane < e_lo + EXPERTS_PER_GROUP), logits, neg)
    v1 = jnp.max(el, axis=-1, keepdims=True)
    i1 = jnp.min(jnp.where(el == v1, lane, LANES), axis=-1, keepdims=True)
    el2 = jnp.where(lane == i1, neg, el)
    v2 = jnp.max(el2, axis=-1, keepdims=True)
    i2 = jnp.min(jnp.where(el2 == v2, lane, LANES), axis=-1, keepdims=True)
    t = jnp.exp(v2 - v1)
    w1 = p_g / (1.0 + t)
    w2 = p_g * t / (1.0 + t)
    e1 = i1 - N_GROUPS
    e2 = i2 - N_GROUPS

    @pl.when(pl.program_id(0) == 0)
    def _():
        base_scr[...] = jnp.zeros(base_scr.shape, jnp.float32)

    oh1 = lane == e1
    oh2 = lane == e2
    picked = jnp.where(oh1 | oh2, 1.0, 0.0)
    before = _dot(tri_ref[...], picked.astype(jnp.bfloat16)) + base_scr[0:1, :]
    rank1 = jnp.sum(jnp.where(oh1, before, 0.0), axis=-1, keepdims=True)
    rank2 = jnp.sum(jnp.where(oh2, before, 0.0), axis=-1, keepdims=True)
    base_scr[...] = base_scr[...] + jnp.sum(picked, axis=0, keepdims=True)
    counts_ref[...] = base_scr[...]

    cols = [e1.astype(jnp.float32), e2.astype(jnp.float32), w1, w2, rank1, rank2]
    route = jnp.zeros(logits.shape, jnp.float32)
    for c, val in enumerate(cols):
        route = jnp.where(lane == c, val, route)
    route_ref[...] = route


def _merge(x2, oa, ob, proj, wa, wb, wo, g2, w_gr, b_gr, w_er, b_er, row0):
    n = oa.shape[0]
    tm = min(PROJ_ROWS, n)
    blk0 = row0 // tm
    half = D_MODEL // 2
    wr = jnp.zeros((D_MODEL, LANES), jnp.float32)
    wr = wr.at[:, :N_GROUPS].set(w_gr).at[:, N_GROUPS:N_GROUPS + N_EXPERTS].set(w_er)
    wr_hi = wr.astype(jnp.bfloat16)
    wr_lo = (wr - wr_hi.astype(jnp.float32)).astype(jnp.bfloat16)
    br = jnp.zeros((1, LANES), jnp.float32)
    br = br.at[0, :N_GROUPS].set(b_gr).at[0, N_GROUPS:N_GROUPS + N_EXPERTS].set(b_er)
    tri = (jnp.arange(tm)[:, None] > jnp.arange(tm)[None, :]).astype(jnp.bfloat16)
    full = lambda shape: pl.BlockSpec(shape, lambda i: (0,) * len(shape))
    gate = lambda c0: pl.BlockSpec((tm, half), lambda i: (i, c0 // half))
    return pl.pallas_call(
        _merge_kernel,
        grid=(n // tm,),
        in_specs=[
            pl.BlockSpec((tm, D_MODEL), lambda i: (blk0 + i, 0)),
            pl.BlockSpec((tm, DA_WIDTH), lambda i: (i, 0)),
            pl.BlockSpec((tm, RET_WIDTH), lambda i: (i, 0)),
            gate(COL_GATE_A), gate(COL_GATE_A + half), gate(COL_GATE_B), gate(COL_GATE_B + half),
            full((DA_WIDTH, D_MODEL)), full((RET_WIDTH, D_MODEL)), full((D_MODEL, D_MODEL)),
            full((1, D_MODEL)), full((D_MODEL, LANES)), full((D_MODEL, LANES)), full((1, LANES)),
            full((tm, tm)),
        ],
        out_specs=[
            pl.BlockSpec((tm, D_MODEL), lambda i: (i, 0)),
            pl.BlockSpec((ROW_PIECES, tm, PIECE), lambda i: (0, i, 0)),
            pl.BlockSpec((tm, LANES), lambda i: (i, 0)),
            pl.BlockSpec((8, LANES), lambda i: (0, 0)),
        ],
        out_shape=[
            jax.ShapeDtypeStruct((n, D_MODEL), jnp.float32),
            jax.ShapeDtypeStruct((ROW_PIECES, n, PIECE), jnp.uint32),
            jax.ShapeDtypeStruct((n, LANES), jnp.float32),
            jax.ShapeDtypeStruct((8, LANES), jnp.float32),
        ],
        scratch_shapes=[pltpu.VMEM((8, LANES), jnp.float32)],
        compiler_params=pltpu.CompilerParams(dimension_semantics=("arbitrary",),
                                             vmem_limit_bytes=VMEM_LIMIT),
        name="merge",
    )(x2, oa, ob, proj, proj, proj, proj, wa.astype(jnp.bfloat16), wb.astype(jnp.bfloat16),
      wo.astype(jnp.bfloat16), g2.reshape(1, D_MODEL), wr_hi, wr_lo, br, tri)


def _sc_mesh():
    return plsc.VectorSubcoreMesh(core_axis_name="c", subcore_axis_name="s")


def _sc_scatter_rows(src, idx, out_rows, src_block):
    steps = idx.shape[1] // SC_WINDOW
    per_core = steps // SC_CORES

    @pl.kernel(out_type=jax.ShapeDtypeStruct((out_rows, PIECE), src.dtype), mesh=_sc_mesh())
    def scatter(src_hbm, idx_hbm, out_hbm):
        def body(src_vmem, idx_vmem):
            pltpu.sync_copy(src_vmem, out_hbm.at[idx_vmem.at[0]])

        pltpu.emit_pipeline(
            body,
            grid=(SC_CORES, per_core),
            in_specs=[pl.BlockSpec((SC_WINDOW, PIECE), lambda c, i: (src_block(c * per_core + i), 0)),
                      pl.BlockSpec((1, SC_WINDOW), lambda c, i: (0, c * per_core + i))],
            out_specs=[],
            core_axis_name=("c", "s"),
            dimension_semantics=(pltpu.PARALLEL, pltpu.PARALLEL),
        )(src_hbm, idx_hbm)

    return scatter(src, idx)


def _sc_gather_rows(table, idx):
    num = idx.shape[1]
    per_core = num // SC_WINDOW // SC_CORES

    @pl.kernel(out_type=jax.ShapeDtypeStruct((num, PIECE), table.dtype), mesh=_sc_mesh())
    def gather(table_hbm, idx_hbm, out_hbm):
        def body(idx_vmem, out_vmem):
            pltpu.sync_copy(table_hbm.at[idx_vmem.at[0]], out_vmem)

        pltpu.emit_pipeline(
            body,
            grid=(SC_CORES, per_core),
            in_specs=[pl.BlockSpec((1, SC_WINDOW), lambda c, i: (0, c * per_core + i))],
            out_specs=[pl.BlockSpec((SC_WINDOW, PIECE), lambda c, i: (c * per_core + i, 0))],
            core_axis_name=("c", "s"),
            dimension_semantics=(pltpu.PARALLEL, pltpu.PARALLEL),
        )(idx_hbm, out_hbm)

    return gather(table, idx)


def _store_packed(ref, val):
    as_bits = lambda v: lax.bitcast_convert_type(v.astype(jnp.bfloat16).astype(jnp.float32), jnp.uint32)
    words = (as_bits(val[:, :PACKED]) >> 16) | (as_bits(val[:, PACKED:]) & jnp.uint32(0xFFFF0000))
    for j in range(ROW_PIECES):
        ref[j] = words[:, j * PIECE:(j + 1) * PIECE]


def _load_packed(ref):
    words = jnp.concatenate([ref[j] for j in range(ROW_PIECES)], axis=1)
    low = lax.bitcast_convert_type(words << 16, jnp.float32)
    high = lax.bitcast_convert_type(words & jnp.uint32(0xFFFF0000), jnp.float32)
    return jnp.concatenate([low, high], axis=1)


def _expert_kernel(blk_e_ref, n_used_ref, nxt_ref, run_ref, x_ref, wg_hbm, wu_hbm, wd_hbm, o_ref,
                   wg_stage, wu_stage, wd_stage, wg_scr, wu_scr, wd_scr, sem):
    i = pl.program_id(0)
    used = i < n_used_ref[0]

    def weight_copies(e, s):
        return (pltpu.make_async_copy(wg_hbm.at[e], wg_stage.at[s], sem.at[s, 0]),
                pltpu.make_async_copy(wu_hbm.at[e], wu_stage.at[s], sem.at[s, 1]),
                pltpu.make_async_copy(wd_hbm.at[e], wd_stage.at[s], sem.at[s, 2]))

    @pl.when(i == 0)
    def _():
        for c in weight_copies(blk_e_ref[0], 0):
            c.start()

    @pl.when(used & ((i == 0) | (blk_e_ref[i] != blk_e_ref[jnp.maximum(i - 1, 0)])))
    def _():
        s = run_ref[i] % 2
        for c in weight_copies(blk_e_ref[i], s):
            c.wait()
        wg_scr[...] = wg_stage[s].astype(jnp.bfloat16)
        wu_scr[...] = wu_stage[s].astype(jnp.bfloat16)
        wd_scr[...] = wd_stage[s].astype(jnp.bfloat16)

        @pl.when(nxt_ref[i] >= 0)
        def _():
            for c in weight_copies(nxt_ref[i], 1 - s):
                c.start()

    @pl.when(used)
    def _():
        x = _load_packed(x_ref).astype(jnp.bfloat16)
        a = _dot(x, wg_scr[...])
        u = _dot(x, wu_scr[...])
        hmid = (a * _sigmoid(a) * u).astype(jnp.bfloat16)
        _store_packed(o_ref, _dot(hmid, wd_scr[...]))

    @pl.when(jnp.logical_not(used))
    def _():
        o_ref[...] = jnp.zeros(o_ref.shape, o_ref.dtype)


def _experts(xs, blk_expert, n_used, w_gate, w_up, w_down):
    p = xs.shape[1]
    nblk = p // MOE_BLOCK
    idx = jnp.arange(nblk, dtype=jnp.int32)
    starts = (idx < n_used[0]) & ((idx == 0) | (blk_expert != jnp.roll(blk_expert, 1)))
    run = jnp.cumsum(starts.astype(jnp.int32)) - 1
    next_start = lax.cummin(jnp.where(starts, idx, nblk)[::-1])[::-1]
    after = jnp.concatenate([next_start[1:], jnp.full((1,), nblk, jnp.int32)])
    nxt = jnp.where(after < nblk, blk_expert[jnp.minimum(after, nblk - 1)], -1).astype(jnp.int32)
    live = lambda i, be, nu, nx, rn: jnp.minimum(i, nu[0] - 1)
    any_spec = pl.BlockSpec(memory_space=pl.ANY)
    return pl.pallas_call(
        _expert_kernel,
        grid_spec=pltpu.PrefetchScalarGridSpec(
            num_scalar_prefetch=4,
            grid=(nblk,),
            in_specs=[
                pl.BlockSpec((ROW_PIECES, MOE_BLOCK, PIECE), lambda i, be, nu, nx, rn: (0, live(i, be, nu, nx, rn), 0)),
                any_spec, any_spec, any_spec,
            ],
            out_specs=pl.BlockSpec((ROW_PIECES, MOE_BLOCK, PIECE), lambda i, be, nu, nx, rn: (0, i, 0)),
            scratch_shapes=[pltpu.VMEM((2, D_MODEL, EXPERT_FF), jnp.float32),
                            pltpu.VMEM((2, D_MODEL, EXPERT_FF), jnp.float32),
                            pltpu.VMEM((2, EXPERT_FF, D_MODEL), jnp.float32),
                            pltpu.VMEM((D_MODEL, EXPERT_FF), jnp.bfloat16),
                            pltpu.VMEM((D_MODEL, EXPERT_FF), jnp.bfloat16),
                            pltpu.VMEM((EXPERT_FF, D_MODEL), jnp.bfloat16),
                            pltpu.SemaphoreType.DMA((2, 3))],
        ),
        out_shape=jax.ShapeDtypeStruct((ROW_PIECES, p, PIECE), jnp.uint32),
        compiler_params=pltpu.CompilerParams(dimension_semantics=("arbitrary",),
                                             vmem_limit_bytes=VMEM_LIMIT),
        name="experts",
    )(blk_expert, n_used, nxt, run.astype(jnp.int32), xs, w_gate, w_up, w_down)


def _combine_kernel(x1_ref, route_ref, y0_ref, y1_ref, *rest):
    o_ref = rest[-1]
    route = route_ref[...]
    o_ref[...] = x1_ref[...] + route[:, 2:3] * _load_packed(y0_ref) + route[:, 3:4] * _load_packed(y1_ref)


def _combine(x1, yg, route, in_row0, out_row0, n_total, out_prev):
    n = yg.shape[1] // TOP_K
    tm = min(PROJ_ROWS, n)
    blk0 = out_row0 // tm
    in0 = in_row0 // tm
    prev = () if out_prev is None else (out_prev,)
    return pl.pallas_call(
        _combine_kernel,
        grid=(n // tm,),
        in_specs=[
            pl.BlockSpec((tm, D_MODEL), lambda i: (in0 + i, 0)),
            pl.BlockSpec((tm, LANES), lambda i: (in0 + i, 0)),
            pl.BlockSpec((ROW_PIECES, tm, PIECE), lambda i: (0, i, 0)),
            pl.BlockSpec((ROW_PIECES, tm, PIECE), lambda i: (0, i + n // tm, 0)),
        ] + [pl.BlockSpec(memory_space=pl.ANY)] * len(prev),
        out_specs=pl.BlockSpec((tm, D_MODEL), lambda i: (blk0 + i, 0)),
        out_shape=jax.ShapeDtypeStruct((n_total, D_MODEL), jnp.float32),
        input_output_aliases={4: 0} if prev else {},
        compiler_params=pltpu.CompilerParams(dimension_semantics=("arbitrary",),
                                             vmem_limit_bytes=VMEM_LIMIT),
        name="combine",
    )(x1, route, yg, yg, *prev)


def _dispatch_plan(route, counts, n):
    counts = counts[0, :N_EXPERTS].astype(jnp.int32)
    padded = ((counts + MOE_BLOCK - 1) // MOE_BLOCK) * MOE_BLOCK
    seg_end = jnp.cumsum(padded).astype(jnp.int32)
    seg_start = seg_end - padded
    e = route[:, 0:TOP_K].astype(jnp.int32)
    rank = route[:, 4:4 + TOP_K].astype(jnp.int32)
    picked = e[:, :, None] == jnp.arange(N_EXPERTS, dtype=jnp.int32)
    dest = (jnp.sum(jnp.where(picked, seg_start, 0), axis=-1) + rank).T
    p = n * TOP_K + N_EXPERTS * MOE_BLOCK
    slot = dest[None] + (jnp.arange(ROW_PIECES, dtype=jnp.int32) * p)[:, None, None]
    blk_start = jnp.arange(p // MOE_BLOCK, dtype=jnp.int32) * MOE_BLOCK
    blk_expert = jnp.sum((seg_end[None, :] <= blk_start[:, None]).astype(jnp.int32), axis=1)
    blk_expert = jnp.minimum(blk_expert, N_EXPERTS - 1)
    n_used = (seg_end[-1] // MOE_BLOCK).reshape(1)
    return slot, blk_expert, n_used, p


def _layer_rows(x2, pos2, row0, batch, seq, out_prev, norm1_g, w_in, q_norm_g, k_norm_g, lam4, diff_subln_g,
                ret_gn_g, ret_gn_b, w_branch_a, w_branch_b, w_out, norm2_g, w_gr, b_gr, w_er, b_er,
                w_gate, w_up, w_down):
    n = batch * seq
    proj = _in_proj(x2, pos2, norm1_g, w_in, q_norm_g, k_norm_g, row0, n)
    oa = _diff_attn(proj, lam4, diff_subln_g.reshape(1, DA_VDIM), batch, seq)
    ob = _retention(proj, ret_gn_g, ret_gn_b, batch, seq)
    x1, h2, route, counts = _merge(x2, oa, ob, proj, w_branch_a, w_branch_b, w_out, norm2_g,
                                   w_gr, b_gr, w_er, b_er, row0)
    slot, blk_expert, n_used, p = _dispatch_plan(route, counts, n)
    win_n = n // SC_WINDOW
    src_block = lambda s: (s // (TOP_K * win_n)) * win_n + s % win_n
    xs = _sc_scatter_rows(h2.reshape(ROW_PIECES * n, PIECE), slot.reshape(1, -1), ROW_PIECES * p, src_block)
    ys = _experts(xs.reshape(ROW_PIECES, p, PIECE), blk_expert, n_used, w_gate, w_up, w_down)
    parts = COMBINE_PARTS if n % (COMBINE_PARTS * PROJ_ROWS) == 0 else 1
    m = n // parts
    out = out_prev
    for t in range(parts):
        yg = _sc_gather_rows(ys.reshape(ROW_PIECES * p, PIECE), slot[:, :, t * m:(t + 1) * m].reshape(1, -1))
        out = _combine(x1, yg.reshape(ROW_PIECES, TOP_K * m, PIECE), route, t * m, row0 + t * m,
                       x2.shape[0], out)
    return out


def _layer(x, positions, *weights):
    batch, seq, _ = x.shape
    n = batch * seq
    x2 = x.reshape(n, D_MODEL)
    pos2 = positions.reshape(1, n)
    groups = BATCH_GROUPS if batch % BATCH_GROUPS == 0 else 1
    per = batch // groups
    out = None
    for g in range(groups):
        out = _layer_rows(x2, pos2, g * per * seq, per, seq, out, *weights)
    return out.reshape(batch, seq, D_MODEL)


def kernel(x, positions, norm1_g, w_in, q_norm_g, k_norm_g, lambda_q1, lambda_k1, lambda_q2, lambda_k2, diff_subln_g, ret_gn_g, ret_gn_b, w_branch_a, w_branch_b, w_out, norm2_g, w_group_router, b_group_router, w_expert_router, b_expert_router, w_gate, w_up, w_down):
    assert x.shape[-1] == D_MODEL and norm1_g.shape[0] == 1, "single-layer, D_MODEL-wide input expected"
    lam4 = jnp.zeros((4, LANES), jnp.float32)
    lam4 = lam4.at[:, :DA_HALF].set(jnp.stack([lambda_q1[0], lambda_k1[0], lambda_q2[0], lambda_k2[0]]))
    return _layer(x, positions, norm1_g[0], w_in[0], q_norm_g[0], k_norm_g[0], lam4, diff_subln_g[0],
                  ret_gn_g[0], ret_gn_b[0], w_branch_a[0], w_branch_b[0], w_out[0], norm2_g[0],
                  w_group_router[0], b_group_router[0], w_expert_router[0], b_expert_router[0],
                  w_gate[0], w_up[0], w_down[0])
```

```python
import functools
import math

import jax
import jax.numpy as jnp
from jax import lax
from jax.experimental import pallas as pl
from jax.experimental.pallas import tpu as pltpu
from jax.experimental.pallas import tpu_sc as plsc

D_MODEL = 1024
DA_HEADS = 4
DA_HALF = 64
DA_VDIM = 2 * DA_HALF
DA_WIDTH = DA_HEADS * DA_VDIM
ROPE_THETA = 500000.0
ROPE_DIM = DA_HALF // 4
RET_HEADS = 4
RET_KDIM = 128
RET_VDIM = 128
RET_WIDTH = RET_HEADS * RET_VDIM
RET_THETA = 10000.0
N_GROUPS = 4
EXPERTS_PER_GROUP = 8
N_EXPERTS = N_GROUPS * EXPERTS_PER_GROUP
TOP_K = 2
EXPERT_FF = 512
EPS = 1e-6
LAMBDA_INIT = 0.8 - 0.6 * math.exp(-0.3 * 0)

LANES = 128
IN_COLS = 3 * DA_WIDTH + 4 * RET_WIDTH + 2 * D_MODEL
COL_QA, COL_KA, COL_VA = 0, DA_WIDTH, 2 * DA_WIDTH
COL_QR = 3 * DA_WIDTH
COL_KR = COL_QR + RET_WIDTH
COL_VR = COL_KR + RET_WIDTH
COL_GB = COL_VR + RET_WIDTH
COL_GATE_A = COL_GB + RET_WIDTH
COL_GATE_B = COL_GATE_A + D_MODEL

PROJ_ROWS = 512
PROJ_CHUNK = 256
ATT_TILE = 512
ATT_ROWS = 32
RET_CHUNK = 256
RET_UNROLL = 8
MOE_BLOCK = 512
PACKED = D_MODEL // 2
ROW_PIECES = 2
PIECE = PACKED // ROW_PIECES
SC_CORES = 2
SC_WINDOW = 128
COMBINE_PARTS = 4
BATCH_GROUPS = 1
VMEM_LIMIT = 56 * 1024 * 1024


def _dot(a, b):
    return jnp.dot(a, b, preferred_element_type=jnp.float32)


def _dot_nt(a, b):
    return lax.dot_general(a, b, (((1,), (1,)), ((), ())), preferred_element_type=jnp.float32)


def _dot_tn(a, b):
    return lax.dot_general(a, b, (((0,), (0,)), ((), ())), preferred_element_type=jnp.float32)


def _sigmoid(x):
    return 0.5 * jnp.tanh(0.5 * x) + 0.5


def _split3(x):
    a = x.astype(jnp.bfloat16)
    r = x - a.astype(jnp.float32)
    b = r.astype(jnp.bfloat16)
    c = (r - b.astype(jnp.float32)).astype(jnp.bfloat16)
    return a, b, c


def _in_proj_kernel(x_ref, pos_ref, g1_ref, w_ref, gsum_ref, gq_ref, gk_ref, fa_ref, fr_ref, sel_ref,
                    o_ref, h_scr):
    x = x_ref[...]
    h = x * lax.rsqrt(jnp.mean(x * x, axis=-1, keepdims=True) + EPS) * g1_ref[...]
    h_scr[...] = h.astype(jnp.bfloat16)
    rows = x.shape[0]
    pos = pos_ref[...].astype(jnp.float32)

    lane = lax.broadcasted_iota(jnp.int32, (rows, LANES), 1)
    half_a = ROPE_DIM // 2
    ang_a = fa_ref[...] * pos
    pad = jnp.zeros((LANES - 2 * half_a, rows), jnp.float32)
    t_a = jnp.concatenate([jnp.cos(ang_a), jnp.sin(ang_a), pad], axis=0).T
    tab = sum(_dot(part, sel_ref[...]) for part in _split3(t_a))
    lane64 = lane % DA_HALF
    c_a = tab[:, :LANES] + jnp.where(lane64 < ROPE_DIM, 0.0, 1.0)
    s_lo = tab[:, LANES:2 * LANES]
    s_hi = tab[:, 2 * LANES:]
    c_a2 = jnp.concatenate([c_a, c_a], axis=1)
    s_lo2 = jnp.concatenate([s_lo, s_lo], axis=1)
    s_hi2 = jnp.concatenate([s_hi, s_hi], axis=1)
    ang_r = fr_ref[...] * pos
    t_r = jnp.concatenate([jnp.cos(ang_r), jnp.sin(ang_r)], axis=0).T
    sw_r = pltpu.roll(t_r, RET_KDIM // 2, axis=1)
    first = lane < RET_KDIM // 2
    c_r = jnp.where(first, t_r, sw_r)
    s_r = jnp.where(first, -sw_r, t_r)
    c_r2 = jnp.concatenate([c_r, c_r], axis=1)
    s_r2 = jnp.concatenate([s_r, s_r], axis=1)

    def qk_norm_rope(y, g, scale):
        ss = y * y
        hi = ss.astype(jnp.bfloat16)
        lo = (ss - hi.astype(jnp.float32)).astype(jnp.bfloat16)
        gs = _dot(hi, gsum_ref[...]) + _dot(lo, gsum_ref[...])
        n = y * lax.rsqrt(gs * (1.0 / DA_HALF) + EPS) * g
        up = pltpu.roll(n, PROJ_CHUNK - half_a, axis=1)
        dn = pltpu.roll(n, half_a, axis=1)
        r = n * c_a2 + up * s_lo2 + dn * s_hi2
        return r * scale if scale != 1.0 else r

    def ret_rope(y, scale):
        halves = [pltpu.roll(y[:, i * LANES:(i + 1) * LANES], RET_KDIM // 2, axis=1)
                  for i in range(PROJ_CHUNK // LANES)]
        sw = jnp.concatenate(halves, axis=1)
        r = y * c_r2 + sw * s_r2
        return r * scale if scale != 1.0 else r

    for c in range(IN_COLS // PROJ_CHUNK):
        c0 = c * PROJ_CHUNK
        y = _dot(h_scr[...], w_ref[:, c0:c0 + PROJ_CHUNK])
        if c0 < COL_KA:
            y = qk_norm_rope(y, gq_ref[...], DA_HALF ** -0.5)
        elif c0 < COL_VA:
            y = qk_norm_rope(y, gk_ref[...], 1.0)
        elif c0 < COL_QR:
            pass
        elif c0 < COL_KR:
            y = ret_rope(y, 1.0)
        elif c0 < COL_VR:
            y = ret_rope(y, RET_KDIM ** -0.5)
        elif c0 < COL_GB:
            pass
        elif c0 < COL_GATE_A:
            y = y * _sigmoid(y)
        else:
            y = _sigmoid(y)
        o_ref[:, c0:c0 + PROJ_CHUNK] = y.astype(o_ref.dtype)


def _in_proj(x2, pos2, g1, w_in, gq, gk, row0, n):
    tm = min(PROJ_ROWS, n)
    blk0 = row0 // tm
    grp = jnp.arange(PROJ_CHUNK) // DA_HALF
    gsum = (grp[:, None] == grp[None, :]).astype(jnp.bfloat16)
    half_a = ROPE_DIM // 2
    fa = jnp.power(jnp.float32(ROPE_THETA), -2.0 * jnp.arange(half_a, dtype=jnp.float32) / ROPE_DIM)[:, None]
    half_r = RET_KDIM // 2
    fr = jnp.power(jnp.float32(RET_THETA), -2.0 * jnp.arange(half_r, dtype=jnp.float32) / RET_KDIM)[:, None]
    j = jnp.arange(LANES)[:, None]
    l64 = (jnp.arange(LANES) % DA_HALF)[None, :]
    sel_c = (j < half_a) & (l64 < ROPE_DIM) & (l64 % half_a == j)
    sel_lo = (j >= half_a) & (j < ROPE_DIM) & (l64 < half_a) & (l64 == j - half_a)
    sel_hi = (j >= half_a) & (j < ROPE_DIM) & (l64 >= half_a) & (l64 < ROPE_DIM) & (l64 == j)
    sel = jnp.concatenate([sel_c.astype(jnp.float32), -sel_lo.astype(jnp.float32),
                           sel_hi.astype(jnp.float32)], axis=1).astype(jnp.bfloat16)
    reps = PROJ_CHUNK // DA_HALF
    full = lambda shape: pl.BlockSpec(shape, lambda i: (0,) * len(shape))
    return pl.pallas_call(
        _in_proj_kernel,
        grid=(n // tm,),
        in_specs=[
            pl.BlockSpec((tm, D_MODEL), lambda i: (blk0 + i, 0)),
            pl.BlockSpec((1, tm), lambda i: (0, blk0 + i)),
            full((1, D_MODEL)),
            full((D_MODEL, IN_COLS)),
            full((PROJ_CHUNK, PROJ_CHUNK)),
            full((1, PROJ_CHUNK)),
            full((1, PROJ_CHUNK)),
            full((half_a, 1)),
            full((half_r, 1)),
            full((LANES, 3 * LANES)),
        ],
        out_specs=pl.BlockSpec((tm, IN_COLS), lambda i: (i, 0)),
        out_shape=jax.ShapeDtypeStruct((n, IN_COLS), jnp.bfloat16),
        scratch_shapes=[pltpu.VMEM((tm, D_MODEL), jnp.bfloat16)],
        compiler_params=pltpu.CompilerParams(dimension_semantics=("arbitrary",),
                                             vmem_limit_bytes=VMEM_LIMIT),
        name="in_proj",
    )(x2, pos2, g1.reshape(1, D_MODEL), w_in.astype(jnp.bfloat16), gsum,
      jnp.tile(gq, reps)[None, :], jnp.tile(gk, reps)[None, :], fa, fr, sel)


def _diff_attn_kernel(q_ref, k_ref, v_ref, lam_ref, gsub_ref, o_ref,
                      qs_scr, vx_scr, s0_scr, s1_scr, p_scr, m_scr, alpha_scr, acc_scr):
    i = pl.program_id(2)
    t = q_ref.shape[0]

    @pl.when(i == 0)
    def _():
        vx_scr[:, :DA_VDIM] = v_ref[...]
        vx_scr[:, DA_VDIM:] = jnp.ones((vx_scr.shape[0], LANES), vx_scr.dtype)

    q = q_ref[...]
    lane = lax.broadcasted_iota(jnp.int32, q.shape, 1)
    zero = jnp.zeros_like(q)
    qs_scr[:t] = jnp.where(lane < DA_HALF, q, zero)
    qs_scr[t:] = jnp.where(lane >= DA_HALF, q, zero)
    m_scr[...] = jnp.full(m_scr.shape, -jnp.inf, jnp.float32)
    acc_scr[...] = jnp.zeros(acc_scr.shape, jnp.float32)

    def scores(j, s_ref):
        start = pl.multiple_of(j * t, t)
        s_ref[...] = _dot_nt(qs_scr[...], k_ref[pl.ds(start, t), :])

    def softmax_pv(j, s_ref, masked):
        for c in range(2 * t // ATT_ROWS):
            rows = pl.ds(c * ATT_ROWS, ATT_ROWS)
            s = s_ref[rows, :]
            if masked:
                r = lax.broadcasted_iota(jnp.int32, s.shape, 0) + (c * ATT_ROWS) % t
                col = lax.broadcasted_iota(jnp.int32, s.shape, 1)
                s = jnp.where(col <= r, s, -jnp.inf)
            m_prev = m_scr[rows, :]
            m_new = jnp.maximum(m_prev, jnp.max(s, axis=-1, keepdims=True))
            alpha_scr[rows, :] = jnp.exp(m_prev - m_new)
            m_scr[rows, :] = m_new
            p = jnp.exp(s - jnp.concatenate([m_new] * (t // LANES), axis=1))
            p_scr[rows, :] = p.astype(p_scr.dtype)
        start = pl.multiple_of(j * t, t)
        pv = _dot(p_scr[...], vx_scr[pl.ds(start, t), :])
        alpha = alpha_scr[...]
        for half in range(2):
            cols = pl.ds(half * LANES, LANES)
            acc_scr[:, cols] = alpha * acc_scr[:, cols] + pv[:, half * LANES:(half + 1) * LANES]

    scores(0, s0_scr)

    def pair(jj, carry):
        j = 2 * jj
        scores(j + 1, s1_scr)
        softmax_pv(j, s0_scr, False)
        scores(j + 2, s0_scr)
        softmax_pv(j + 1, s1_scr, False)
        return carry

    lax.fori_loop(0, i // 2, pair, 0)

    @pl.when(i % 2 == 1)
    def _():
        scores(i, s1_scr)
        softmax_pv(i - 1, s0_scr, False)
        softmax_pv(i, s1_scr, True)

    @pl.when(i % 2 == 0)
    def _():
        softmax_pv(i, s0_scr, True)

    lam4 = lam_ref[...]
    lam = (jnp.exp(jnp.sum(lam4[0:1] * lam4[1:2], axis=-1, keepdims=True))
           - jnp.exp(jnp.sum(lam4[2:3] * lam4[3:4], axis=-1, keepdims=True)) + LAMBDA_INIT)
    o_all = acc_scr[:, :DA_VDIM] / acc_scr[:, DA_VDIM:]
    o = o_all[:t] - lam * o_all[t:]
    o = o * lax.rsqrt(jnp.mean(o * o, axis=-1, keepdims=True) + EPS) * gsub_ref[...] * (1.0 - LAMBDA_INIT)
    o_ref[...] = o.astype(o_ref.dtype)


def _diff_attn(proj, lam4, gsub, batch, seq):
    n = proj.shape[0]
    t = min(ATT_TILE, seq)
    nq = seq // t
    qb, kb, vb = COL_QA // LANES, COL_KA // LANES, COL_VA // LANES
    return pl.pallas_call(
        _diff_attn_kernel,
        grid=(batch, DA_HEADS, nq),
        in_specs=[
            pl.BlockSpec((t, LANES), lambda b, h, i: (b * nq + i, qb + h)),
            pl.BlockSpec((seq, LANES), lambda b, h, i: (b, kb + h)),
            pl.BlockSpec((seq, LANES), lambda b, h, i: (b, vb + h)),
            pl.BlockSpec((4, LANES), lambda b, h, i: (0, 0)),
            pl.BlockSpec((1, LANES), lambda b, h, i: (0, 0)),
        ],
        out_specs=pl.BlockSpec((t, LANES), lambda b, h, i: (b * nq + i, h)),
        out_shape=jax.ShapeDtypeStruct((n, DA_WIDTH), jnp.bfloat16),
        scratch_shapes=[pltpu.VMEM((2 * t, LANES), jnp.bfloat16),
                        pltpu.VMEM((seq, DA_VDIM + LANES), jnp.bfloat16),
                        pltpu.VMEM((2 * t, t), jnp.float32),
                        pltpu.VMEM((2 * t, t), jnp.float32),
                        pltpu.VMEM((2 * t, t), jnp.bfloat16),
                        pltpu.VMEM((2 * t, LANES), jnp.float32),
                        pltpu.VMEM((2 * t, LANES), jnp.float32),
                        pltpu.VMEM((2 * t, DA_VDIM + LANES), jnp.float32)],
        compiler_params=pltpu.CompilerParams(dimension_semantics=("arbitrary",) * 3,
                                             vmem_limit_bytes=VMEM_LIMIT),
        name="diff_attn",
    )(proj, proj, proj, lam4, gsub)


def _retention_kernel(q_ref, k_ref, v_ref, g_ref, gng_ref, gnb_ref, o_ref, r_scr, *, chunk):
    hf = jnp.full((1, 1), pl.program_id(1), jnp.int32).astype(jnp.float32)
    log_g = jnp.log1p(-jnp.exp2(-5.0 - hf))
    ri = lax.broadcasted_iota(jnp.int32, (chunk, chunk), 0)
    ci = lax.broadcasted_iota(jnp.int32, (chunk, chunk), 1)
    rel = (ri - ci).astype(jnp.float32)
    dmask = jnp.where(rel >= 0, jnp.exp(jnp.maximum(rel, 0.0) * log_g), 0.0)
    idx = lax.broadcasted_iota(jnp.int32, (chunk, 1), 0).astype(jnp.float32)
    zeta = jnp.exp((chunk - 1 - idx) * log_g)
    xi = jnp.exp((idx + 1.0) * log_g)
    g_chunk = jnp.exp(chunk * log_g)
    r_scr[...] = jnp.zeros(r_scr.shape, jnp.float32)
    gng = gng_ref[...]
    gnb = gnb_ref[...]

    def body(n, carry):
        start = pl.multiple_of(n * chunk, chunk)
        q = q_ref[pl.ds(start, chunk), :]
        k = k_ref[pl.ds(start, chunk), :]
        v = v_ref[pl.ds(start, chunk), :]
        s = _dot_nt(q, k) * dmask
        r_old = r_scr[...]
        o = _dot(s.astype(jnp.bfloat16), v) + xi * _dot(q, r_old.astype(jnp.bfloat16))
        kz = (k.astype(jnp.float32) * zeta).astype(jnp.bfloat16)
        r_scr[...] = g_chunk * r_old + _dot_tn(kz, v)
        mu = jnp.mean(o, axis=-1, keepdims=True)
        d = o - mu
        var = jnp.mean(d * d, axis=-1, keepdims=True)
        y = d * lax.rsqrt(var + EPS) * gng + gnb
        y = y * g_ref[pl.ds(start, chunk), :].astype(jnp.float32)
        o_ref[pl.ds(start, chunk), :] = y.astype(o_ref.dtype)
        return carry

    lax.fori_loop(0, q_ref.shape[0] // chunk, body, 0, unroll=RET_UNROLL)


def _retention(proj, gn_g, gn_b, batch, seq):
    n = proj.shape[0]
    chunk = min(RET_CHUNK, seq)
    col = lambda c0: (lambda b, h: (b, c0 // LANES + h))
    return pl.pallas_call(
        functools.partial(_retention_kernel, chunk=chunk),
        grid=(batch, RET_HEADS),
        in_specs=[
            pl.BlockSpec((seq, LANES), col(COL_QR)),
            pl.BlockSpec((seq, LANES), col(COL_KR)),
            pl.BlockSpec((seq, LANES), col(COL_VR)),
            pl.BlockSpec((seq, LANES), col(COL_GB)),
            pl.BlockSpec((1, LANES), lambda b, h: (0, h)),
            pl.BlockSpec((1, LANES), lambda b, h: (0, h)),
        ],
        out_specs=pl.BlockSpec((seq, LANES), lambda b, h: (b, h)),
        out_shape=jax.ShapeDtypeStruct((n, RET_WIDTH), jnp.bfloat16),
        scratch_shapes=[pltpu.VMEM((RET_KDIM, RET_VDIM), jnp.float32)],
        compiler_params=pltpu.CompilerParams(dimension_semantics=("arbitrary",) * 2,
                                             vmem_limit_bytes=VMEM_LIMIT),
        name="retention",
    )(proj, proj, proj, proj, gn_g.reshape(1, RET_WIDTH), gn_b.reshape(1, RET_WIDTH))


def _merge_kernel(x_ref, oa_ref, ob_ref, sa0_ref, sa1_ref, sb0_ref, sb1_ref, wa_ref, wb_ref, wo_ref,
                  g2_ref, wr_hi_ref, wr_lo_ref, br_ref, tri_ref, x1_ref, h2_ref, route_ref, counts_ref,
                  base_scr):
    ya = _dot(oa_ref[...], wa_ref[...])
    yb = _dot(ob_ref[...], wb_ref[...])
    sa = jnp.concatenate([sa0_ref[...], sa1_ref[...]], axis=1).astype(jnp.float32)
    sb = jnp.concatenate([sb0_ref[...], sb1_ref[...]], axis=1).astype(jnp.float32)
    merged = sa * ya + sb * yb
    x1 = x_ref[...] + _dot(merged.astype(jnp.bfloat16), wo_ref[...])
    x1_ref[...] = x1
    h2 = x1 * lax.rsqrt(jnp.mean(x1 * x1, axis=-1, keepdims=True) + EPS) * g2_ref[...]
    _store_packed(h2_ref, h2)

    hi = h2.astype(jnp.bfloat16)
    lo = (h2 - hi.astype(jnp.float32)).astype(jnp.bfloat16)
    logits = (_dot(hi, wr_hi_ref[...]) + _dot(lo, wr_hi_ref[...]) + _dot(hi, wr_lo_ref[...])
              + br_ref[...])
    lane = lax.broadcasted_iota(jnp.int32, logits.shape, 1)
    neg = -jnp.inf
    gl = jnp.where(lane < N_GROUPS, logits, neg)
    gmax = jnp.max(gl, axis=-1, keepdims=True)
    g_idx = jnp.min(jnp.where(gl == gmax, lane, LANES), axis=-1, keepdims=True)
    p_g = 1.0 / jnp.sum(jnp.exp(gl - gmax), axis=-1, keepdims=True)
    e_lo = N_GROUPS + EXPERTS_PER_GROUP * g_idx
    el = jnp.where((lane >= e_lo) & (lane < e_lo + EXPERTS_PER_GROUP), logits, neg)
    v1 = jnp.max(el, axis=-1, keepdims=True)
    i1 = jnp.min(jnp.where(el == v1, lane, LANES), axis=-1, keepdims=True)
    el2 = jnp.where(lane == i1, neg, el)
    v2 = jnp.max(el2, axis=-1, keepdims=True)
    i2 = jnp.min(jnp.where(el2 == v2, lane, LANES), axis=-1, keepdims=True)
    t = jnp.exp(v2 - v1)
    w1 = p_g / (1.0 + t)
    w2 = p_g * t / (1.0 + t)
    e1 = i1 - N_GROUPS
    e2 = i2 - N_GROUPS

    @pl.when(pl.program_id(0) == 0)
    def _():
        base_scr[...] = jnp.zeros(base_scr.shape, jnp.float32)

    oh1 = lane == e1
    oh2 = lane == e2
    picked = jnp.where(oh1 | oh2, 1.0, 0.0)
    before = _dot(tri_ref[...], picked.astype(jnp.bfloat16)) + base_scr[0:1, :]
    rank1 = jnp.sum(jnp.where(oh1, before, 0.0), axis=-1, keepdims=True)
    rank2 = jnp.sum(jnp.where(oh2, before, 0.0), axis=-1, keepdims=True)
    base_scr[...] = base_scr[...] + jnp.sum(picked, axis=0, keepdims=True)
    counts_ref[...] = base_scr[...]

    cols = [e1.astype(jnp.float32), e2.astype(jnp.float32), w1, w2, rank1, rank2]
    route = jnp.zeros(logits.shape, jnp.float32)
    for c, val in enumerate(cols):
        route = jnp.where(lane == c, val, route)
    route_ref[...] = route


def _merge(x2, oa, ob, proj, wa, wb, wo, g2, w_gr, b_gr, w_er, b_er, row0):
    n = oa.shape[0]
    tm = min(PROJ_ROWS, n)
    blk0 = row0 // tm
    half = D_MODEL // 2
    wr = jnp.zeros((D_MODEL, LANES), jnp.float32)
    wr = wr.at[:, :N_GROUPS].set(w_gr).at[:, N_GROUPS:N_GROUPS + N_EXPERTS].set(w_er)
    wr_hi = wr.astype(jnp.bfloat16)
    wr_lo = (wr - wr_hi.astype(jnp.float32)).astype(jnp.bfloat16)
    br = jnp.zeros((1, LANES), jnp.float32)
    br = br.at[0, :N_GROUPS].set(b_gr).at[0, N_GROUPS:N_GROUPS + N_EXPERTS].set(b_er)
    tri = (jnp.arange(tm)[:, None] > jnp.arange(tm)[None, :]).astype(jnp.bfloat16)
    full = lambda shape: pl.BlockSpec(shape, lambda i: (0,) * len(shape))
    gate = lambda c0: pl.BlockSpec((tm, half), lambda i: (i, c0 // half))
    return pl.pallas_call(
        _merge_kernel,
        grid=(n // tm,),
        in_specs=[
            pl.BlockSpec((tm, D_MODEL), lambda i: (blk0 + i, 0)),
            pl.BlockSpec((tm, DA_WIDTH), lambda i: (i, 0)),
            pl.BlockSpec((tm, RET_WIDTH), lambda i: (i, 0)),
            gate(COL_GATE_A), gate(COL_GATE_A + half), gate(COL_GATE_B), gate(COL_GATE_B + half),
            full((DA_WIDTH, D_MODEL)), full((RET_WIDTH, D_MODEL)), full((D_MODEL, D_MODEL)),
            full((1, D_MODEL)), full((D_MODEL, LANES)), full((D_MODEL, LANES)), full((1, LANES)),
            full((tm, tm)),
        ],
        out_specs=[
            pl.BlockSpec((tm, D_MODEL), lambda i: (i, 0)),
            pl.BlockSpec((ROW_PIECES, tm, PIECE), lambda i: (0, i, 0)),
            pl.BlockSpec((tm, LANES), lambda i: (i, 0)),
            pl.BlockSpec((8, LANES), lambda i: (0, 0)),
        ],
        out_shape=[
            jax.ShapeDtypeStruct((n, D_MODEL), jnp.float32),
            jax.ShapeDtypeStruct((ROW_PIECES, n, PIECE), jnp.uint32),
            jax.ShapeDtypeStruct((n, LANES), jnp.float32),
            jax.ShapeDtypeStruct((8, LANES), jnp.float32),
        ],
        scratch_shapes=[pltpu.VMEM((8, LANES), jnp.float32)],
        compiler_params=pltpu.CompilerParams(dimension_semantics=("arbitrary",),
                                             vmem_limit_bytes=VMEM_LIMIT),
        name="merge",
    )(x2, oa, ob, proj, proj, proj, proj, wa.astype(jnp.bfloat16), wb.astype(jnp.bfloat16),
      wo.astype(jnp.bfloat16), g2.reshape(1, D_MODEL), wr_hi, wr_lo, br, tri)


def _sc_mesh():
    return plsc.VectorSubcoreMesh(core_axis_name="c", subcore_axis_name="s")


def _sc_scatter_rows(src, idx, out_rows, src_block):
    steps = idx.shape[1] // SC_WINDOW
    per_core = steps // SC_CORES

    @pl.kernel(out_type=jax.ShapeDtypeStruct((out_rows, PIECE), src.dtype), mesh=_sc_mesh())
    def scatter(src_hbm, idx_hbm, out_hbm):
        def body(src_vmem, idx_vmem):
            pltpu.sync_copy(src_vmem, out_hbm.at[idx_vmem.at[0]])

        pltpu.emit_pipeline(
            body,
            grid=(SC_CORES, per_core),
            in_specs=[pl.BlockSpec((SC_WINDOW, PIECE), lambda c, i: (src_block(c * per_core + i), 0)),
                      pl.BlockSpec((1, SC_WINDOW), lambda c, i: (0, c * per_core + i))],
            out_specs=[],
            core_axis_name=("c", "s"),
            dimension_semantics=(pltpu.PARALLEL, pltpu.PARALLEL),
        )(src_hbm, idx_hbm)

    return scatter(src, idx)


def _sc_gather_rows(table, idx):
    num = idx.shape[1]
    per_core = num // SC_WINDOW // SC_CORES

    @pl.kernel(out_type=jax.ShapeDtypeStruct((num, PIECE), table.dtype), mesh=_sc_mesh())
    def gather(table_hbm, idx_hbm, out_hbm):
        def body(idx_vmem, out_vmem):
            pltpu.sync_copy(table_hbm.at[idx_vmem.at[0]], out_vmem)

        pltpu.emit_pipeline(
            body,
            grid=(SC_CORES, per_core),
            in_specs=[pl.BlockSpec((1, SC_WINDOW), lambda c, i: (0, c * per_core + i))],
            out_specs=[pl.BlockSpec((SC_WINDOW, PIECE), lambda c, i: (c * per_core + i, 0))],
            core_axis_name=("c", "s"),
            dimension_semantics=(pltpu.PARALLEL, pltpu.PARALLEL),
        )(idx_hbm, out_hbm)

    return gather(table, idx)


def _store_packed(ref, val):
    as_bits = lambda v: lax.bitcast_convert_type(v.astype(jnp.bfloat16).astype(jnp.float32), jnp.uint32)
    words = (as_bits(val[:, :PACKED]) >> 16) | (as_bits(val[:, PACKED:]) & jnp.uint32(0xFFFF0000))
    for j in range(ROW_PIECES):
        ref[j] = words[:, j * PIECE:(j + 1) * PIECE]


def _load_packed(ref):
    words = jnp.concatenate([ref[j] for j in range(ROW_PIECES)], axis=1)
    low = lax.bitcast_convert_type(words << 16, jnp.float32)
    high = lax.bitcast_convert_type(words & jnp.uint32(0xFFFF0000), jnp.float32)
    return jnp.concatenate([low, high], axis=1)


def _expert_kernel(blk_e_ref, n_used_ref, nxt_ref, run_ref, x_ref, wg_hbm, wu_hbm, wd_hbm, o_ref,
                   wg_stage, wu_stage, wd_stage, wg_scr, wu_scr, wd_scr, sem):
    i = pl.program_id(0)
    used = i < n_used_ref[0]

    def weight_copies(e, s):
        return (pltpu.make_async_copy(wg_hbm.at[e], wg_stage.at[s], sem.at[s, 0]),
                pltpu.make_async_copy(wu_hbm.at[e], wu_stage.at[s], sem.at[s, 1]),
                pltpu.make_async_copy(wd_hbm.at[e], wd_stage.at[s], sem.at[s, 2]))

    @pl.when(i == 0)
    def _():
        for c in weight_copies(blk_e_ref[0], 0):
            c.start()

    @pl.when(used & ((i == 0) | (blk_e_ref[i] != blk_e_ref[jnp.maximum(i - 1, 0)])))
    def _():
        s = run_ref[i] % 2
        for c in weight_copies(blk_e_ref[i], s):
            c.wait()
        wg_scr[...] = wg_stage[s].astype(jnp.bfloat16)
        wu_scr[...] = wu_stage[s].astype(jnp.bfloat16)
        wd_scr[...] = wd_stage[s].astype(jnp.bfloat16)

        @pl.when(nxt_ref[i] >= 0)
        def _():
            for c in weight_copies(nxt_ref[i], 1 - s):
                c.start()

    @pl.when(used)
    def _():
        x = _load_packed(x_ref).astype(jnp.bfloat16)
        a = _dot(x, wg_scr[...])
        u = _dot(x, wu_scr[...])
        hmid = (a * _sigmoid(a) * u).astype(jnp.bfloat16)
        _store_packed(o_ref, _dot(hmid, wd_scr[...]))

    @pl.when(jnp.logical_not(used))
    def _():
        o_ref[...] = jnp.zeros(o_ref.shape, o_ref.dtype)


def _experts(xs, blk_expert, n_used, w_gate, w_up, w_down):
    p = xs.shape[1]
    nblk = p // MOE_BLOCK
    idx = jnp.arange(nblk, dtype=jnp.int32)
    starts = (idx < n_used[0]) & ((idx == 0) | (blk_expert != jnp.roll(blk_expert, 1)))
    run = jnp.cumsum(starts.astype(jnp.int32)) - 1
    next_start = lax.cummin(jnp.where(starts, idx, nblk)[::-1])[::-1]
    after = jnp.concatenate([next_start[1:], jnp.full((1,), nblk, jnp.int32)])
    nxt = jnp.where(after < nblk, blk_expert[jnp.minimum(after, nblk - 1)], -1).astype(jnp.int32)
    live = lambda i, be, nu, nx, rn: jnp.minimum(i, nu[0] - 1)
    any_spec = pl.BlockSpec(memory_space=pl.ANY)
    return pl.pallas_call(
        _expert_kernel,
        grid_spec=pltpu.PrefetchScalarGridSpec(
            num_scalar_prefetch=4,
            grid=(nblk,),
            in_specs=[
                pl.BlockSpec((ROW_PIECES, MOE_BLOCK, PIECE), lambda i, be, nu, nx, rn: (0, live(i, be, nu, nx, rn), 0)),
                any_spec, any_spec, any_spec,
            ],
            out_specs=pl.BlockSpec((ROW_PIECES, MOE_BLOCK, PIECE), lambda i, be, nu, nx, rn: (0, i, 0)),
            scratch_shapes=[pltpu.VMEM((2, D_MODEL, EXPERT_FF), jnp.float32),
                            pltpu.VMEM((2, D_MODEL, EXPERT_FF), jnp.float32),
                            pltpu.VMEM((2, EXPERT_FF, D_MODEL), jnp.float32),
                            pltpu.VMEM((D_MODEL, EXPERT_FF), jnp.bfloat16),
                            pltpu.VMEM((D_MODEL, EXPERT_FF), jnp.bfloat16),
                            pltpu.VMEM((EXPERT_FF, D_MODEL), jnp.bfloat16),
                            pltpu.SemaphoreType.DMA((2, 3))],
        ),
        out_shape=jax.ShapeDtypeStruct((ROW_PIECES, p, PIECE), jnp.uint32),
        compiler_params=pltpu.CompilerParams(dimension_semantics=("arbitrary",),
                                             vmem_limit_bytes=VMEM_LIMIT),
        name="experts",
    )(blk_expert, n_used, nxt, run.astype(jnp.int32), xs, w_gate, w_up, w_down)


def _combine_kernel(x1_ref, route_ref, y0_ref, y1_ref, *rest):
    o_ref = rest[-1]
    route = route_ref[...]
    o_ref[...] = x1_ref[...] + route[:, 2:3] * _load_packed(y0_ref) + route[:, 3:4] * _load_packed(y1_ref)


def _combine(x1, yg, route, in_row0, out_row0, n_total, out_prev):
    n = yg.shape[1] // TOP_K
    tm = min(PROJ_ROWS, n)
    blk0 = out_row0 // tm
    in0 = in_row0 // tm
    prev = () if out_prev is None else (out_prev,)
    return pl.pallas_call(
        _combine_kernel,
        grid=(n // tm,),
        in_specs=[
            pl.BlockSpec((tm, D_MODEL), lambda i: (in0 + i, 0)),
            pl.BlockSpec((tm, LANES), lambda i: (in0 + i, 0)),
            pl.BlockSpec((ROW_PIECES, tm, PIECE), lambda i: (0, i, 0)),
            pl.BlockSpec((ROW_PIECES, tm, PIECE), lambda i: (0, i + n // tm, 0)),
        ] + [pl.BlockSpec(memory_space=pl.ANY)] * len(prev),
        out_specs=pl.BlockSpec((tm, D_MODEL), lambda i: (blk0 + i, 0)),
        out_shape=jax.ShapeDtypeStruct((n_total, D_MODEL), jnp.float32),
        input_output_aliases={4: 0} if prev else {},
        compiler_params=pltpu.CompilerParams(dimension_semantics=("arbitrary",),
                                             vmem_limit_bytes=VMEM_LIMIT),
        name="combine",
    )(x1, route, yg, yg, *prev)


def _dispatch_plan(route, counts, n):
    counts = counts[0, :N_EXPERTS].astype(jnp.int32)
    padded = ((counts + MOE_BLOCK - 1) // MOE_BLOCK) * MOE_BLOCK
    seg_end = jnp.cumsum(padded).astype(jnp.int32)
    seg_start = seg_end - padded
    e = route[:, 0:TOP_K].astype(jnp.int32)
    rank = route[:, 4:4 + TOP_K].astype(jnp.int32)
    picked = e[:, :, None] == jnp.arange(N_EXPERTS, dtype=jnp.int32)
    dest = (jnp.sum(jnp.where(picked, seg_start, 0), axis=-1) + rank).T
    p = n * TOP_K + N_EXPERTS * MOE_BLOCK
    slot = dest[None] + (jnp.arange(ROW_PIECES, dtype=jnp.int32) * p)[:, None, None]
    blk_start = jnp.arange(p // MOE_BLOCK, dtype=jnp.int32) * MOE_BLOCK
    blk_expert = jnp.sum((seg_end[None, :] <= blk_start[:, None]).astype(jnp.int32), axis=1)
    blk_expert = jnp.minimum(blk_expert, N_EXPERTS - 1)
    n_used = (seg_end[-1] // MOE_BLOCK).reshape(1)
    return slot, blk_expert, n_used, p


def _layer_rows(x2, pos2, row0, batch, seq, out_prev, norm1_g, w_in, q_norm_g, k_norm_g, lam4, diff_subln_g,
                ret_gn_g, ret_gn_b, w_branch_a, w_branch_b, w_out, norm2_g, w_gr, b_gr, w_er, b_er,
                w_gate, w_up, w_down):
    n = batch * seq
    proj = _in_proj(x2, pos2, norm1_g, w_in, q_norm_g, k_norm_g, row0, n)
    oa = _diff_attn(proj, lam4, diff_subln_g.reshape(1, DA_VDIM), batch, seq)
    ob = _retention(proj, ret_gn_g, ret_gn_b, batch, seq)
    x1, h2, route, counts = _merge(x2, oa, ob, proj, w_branch_a, w_branch_b, w_out, norm2_g,
                                   w_gr, b_gr, w_er, b_er, row0)
    slot, blk_expert, n_used, p = _dispatch_plan(route, counts, n)
    win_n = n // SC_WINDOW
    src_block = lambda s: (s // (TOP_K * win_n)) * win_n + s % win_n
    xs = _sc_scatter_rows(h2.reshape(ROW_PIECES * n, PIECE), slot.reshape(1, -1), ROW_PIECES * p, src_block)
    ys = _experts(xs.reshape(ROW_PIECES, p, PIECE), blk_expert, n_used, w_gate, w_up, w_down)
    parts = COMBINE_PARTS if n % (COMBINE_PARTS * PROJ_ROWS) == 0 else 1
    m = n // parts
    out = out_prev
    for t in range(parts):
        yg = _sc_gather_rows(ys.reshape(ROW_PIECES * p, PIECE), slot[:, :, t * m:(t + 1) * m].reshape(1, -1))
        out = _combine(x1, yg.reshape(ROW_PIECES, TOP_K * m, PIECE), route, t * m, row0 + t * m,
                       x2.shape[0], out)
    return out


def _layer(x, positions, *weights):
    batch, seq, _ = x.shape
    n = batch * seq
    x2 = x.reshape(n, D_MODEL)
    pos2 = positions.reshape(1, n)
    groups = BATCH_GROUPS if batch % BATCH_GROUPS == 0 else 1
    per = batch // groups
    out = None
    for g in range(groups):
        out = _layer_rows(x2, pos2, g * per * seq, per, seq, out, *weights)
    return out.reshape(batch, seq, D_MODEL)


def kernel(x, positions, norm1_g, w_in, q_norm_g, k_norm_g, lambda_q1, lambda_k1, lambda_q2, lambda_k2, diff_subln_g, ret_gn_g, ret_gn_b, w_branch_a, w_branch_b, w_out, norm2_g, w_group_router, b_group_router, w_expert_router, b_expert_router, w_gate, w_up, w_down):
    assert x.shape[-1] == D_MODEL and norm1_g.shape[0] == 1, "single-layer, D_MODEL-wide input expected"
    lam4 = jnp.zeros((4, LANES), jnp.float32)
    lam4 = lam4.at[:, :DA_HALF].set(jnp.stack([lambda_q1[0], lambda_k1[0], lambda_q2[0], lambda_k2[0]]))
    return _layer(x, positions, norm1_g[0], w_in[0], q_norm_g[0], k_norm_g[0], lam4, diff_subln_g[0],
                  ret_gn_g[0], ret_gn_b[0], w_branch_a[0], w_branch_b[0], w_out[0], norm2_g[0],
                  w_group_router[0], b_group_router[0], w_expert_router[0], b_expert_router[0],
                  w_gate[0], w_up[0], w_down[0])
```

```python
import functools
import math

import jax
import jax.numpy as jnp
from jax import lax
from jax.experimental import pallas as pl
from jax.experimental.pallas import tpu as pltpu
from jax.experimental.pallas import tpu_sc as plsc

D_MODEL = 1024
DA_HEADS = 4
DA_HALF = 64
DA_VDIM = 2 * DA_HALF
DA_WIDTH = DA_HEADS * DA_VDIM
ROPE_THETA = 500000.0
ROPE_DIM = DA_HALF // 4
RET_HEADS = 4
RET_KDIM = 128
RET_VDIM = 128
RET_WIDTH = RET_HEADS * RET_VDIM
RET_THETA = 10000.0
N_GROUPS = 4
EXPERTS_PER_GROUP = 8
N_EXPERTS = N_GROUPS * EXPERTS_PER_GROUP
TOP_K = 2
EXPERT_FF = 512
EPS = 1e-6
LAMBDA_INIT = 0.8 - 0.6 * math.exp(-0.3 * 0)

LANES = 128
IN_COLS = 3 * DA_WIDTH + 4 * RET_WIDTH + 2 * D_MODEL
COL_QA, COL_KA, COL_VA = 0, DA_WIDTH, 2 * DA_WIDTH
COL_QR = 3 * DA_WIDTH
COL_KR = COL_QR + RET_WIDTH
COL_VR = COL_KR + RET_WIDTH
COL_GB = COL_VR + RET_WIDTH
COL_GATE_A = COL_GB + RET_WIDTH
COL_GATE_B = COL_GATE_A + D_MODEL

PROJ_ROWS = 512
PROJ_CHUNK = 256
ATT_TILE = 512
ATT_ROWS = 32
RET_CHUNK = 256
RET_UNROLL = 8
MOE_BLOCK = 512
PACKED = D_MODEL // 2
ROW_PIECES = 2
PIECE = PACKED // ROW_PIECES
SC_CORES = 2
SC_WINDOW = 128
COMBINE_PARTS = 2
BATCH_GROUPS = 1
VMEM_LIMIT = 56 * 1024 * 1024


def _dot(a, b):
    return jnp.dot(a, b, preferred_element_type=jnp.float32)


def _dot_nt(a, b):
    return lax.dot_general(a, b, (((1,), (1,)), ((), ())), preferred_element_type=jnp.float32)


def _dot_tn(a, b):
    return lax.dot_general(a, b, (((0,), (0,)), ((), ())), preferred_element_type=jnp.float32)


def _sigmoid(x):
    return 0.5 * jnp.tanh(0.5 * x) + 0.5


def _split3(x):
    a = x.astype(jnp.bfloat16)
    r = x - a.astype(jnp.float32)
    b = r.astype(jnp.bfloat16)
    c = (r - b.astype(jnp.float32)).astype(jnp.bfloat16)
    return a, b, c


def _in_proj_kernel(x_ref, pos_ref, g1_ref, w_ref, gsum_ref, gq_ref, gk_ref, fa_ref, fr_ref, sel_ref,
                    o_ref, h_scr):
    x = x_ref[...]
    h = x * lax.rsqrt(jnp.mean(x * x, axis=-1, keepdims=True) + EPS) * g1_ref[...]
    h_scr[...] = h.astype(jnp.bfloat16)
    rows = x.shape[0]
    pos = pos_ref[...].astype(jnp.float32)

    lane = lax.broadcasted_iota(jnp.int32, (rows, LANES), 1)
    half_a = ROPE_DIM // 2
    ang_a = fa_ref[...] * pos
    pad = jnp.zeros((LANES - 2 * half_a, rows), jnp.float32)
    t_a = jnp.concatenate([jnp.cos(ang_a), jnp.sin(ang_a), pad], axis=0).T
    tab = sum(_dot(part, sel_ref[...]) for part in _split3(t_a))
    lane64 = lane % DA_HALF
    c_a = tab[:, :LANES] + jnp.where(lane64 < ROPE_DIM, 0.0, 1.0)
    s_lo = tab[:, LANES:2 * LANES]
    s_hi = tab[:, 2 * LANES:]
    c_a2 = jnp.concatenate([c_a, c_a], axis=1)
    s_lo2 = jnp.concatenate([s_lo, s_lo], axis=1)
    s_hi2 = jnp.concatenate([s_hi, s_hi], axis=1)
    ang_r = fr_ref[...] * pos
    t_r = jnp.concatenate([jnp.cos(ang_r), jnp.sin(ang_r)], axis=0).T
    sw_r = pltpu.roll(t_r, RET_KDIM // 2, axis=1)
    first = lane < RET_KDIM // 2
    c_r = jnp.where(first, t_r, sw_r)
    s_r = jnp.where(first, -sw_r, t_r)
    c_r2 = jnp.concatenate([c_r, c_r], axis=1)
    s_r2 = jnp.concatenate([s_r, s_r], axis=1)

    def qk_norm_rope(y, g, scale):
        ss = y * y
        hi = ss.astype(jnp.bfloat16)
        lo = (ss - hi.astype(jnp.float32)).astype(jnp.bfloat16)
        gs = _dot(hi, gsum_ref[...]) + _dot(lo, gsum_ref[...])
        n = y * lax.rsqrt(gs * (1.0 / DA_HALF) + EPS) * g
        up = pltpu.roll(n, PROJ_CHUNK - half_a, axis=1)
        dn = pltpu.roll(n, half_a, axis=1)
        r = n * c_a2 + up * s_lo2 + dn * s_hi2
        return r * scale if scale != 1.0 else r

    def ret_rope(y, scale):
        halves = [pltpu.roll(y[:, i * LANES:(i + 1) * LANES], RET_KDIM // 2, axis=1)
                  for i in range(PROJ_CHUNK // LANES)]
        sw = jnp.concatenate(halves, axis=1)
        r = y * c_r2 + sw * s_r2
        return r * scale if scale != 1.0 else r

    for c in range(IN_COLS // PROJ_CHUNK):
        c0 = c * PROJ_CHUNK
        y = _dot(h_scr[...], w_ref[:, c0:c0 + PROJ_CHUNK])
        if c0 < COL_KA:
            y = qk_norm_rope(y, gq_ref[...], DA_HALF ** -0.5)
        elif c0 < COL_VA:
            y = qk_norm_rope(y, gk_ref[...], 1.0)
        elif c0 < COL_QR:
            pass
        elif c0 < COL_KR:
            y = ret_rope(y, 1.0)
        elif c0 < COL_VR:
            y = ret_rope(y, RET_KDIM ** -0.5)
        elif c0 < COL_GB:
            pass
        elif c0 < COL_GATE_A:
            y = y * _sigmoid(y)
        else:
            y = _sigmoid(y)
        o_ref[:, c0:c0 + PROJ_CHUNK] = y.astype(o_ref.dtype)


def _in_proj(x2, pos2, g1, w_in, gq, gk, row0, n):
    tm = min(PROJ_ROWS, n)
    blk0 = row0 // tm
    grp = jnp.arange(PROJ_CHUNK) // DA_HALF
    gsum = (grp[:, None] == grp[None, :]).astype(jnp.bfloat16)
    half_a = ROPE_DIM // 2
    fa = jnp.power(jnp.float32(ROPE_THETA), -2.0 * jnp.arange(half_a, dtype=jnp.float32) / ROPE_DIM)[:, None]
    half_r = RET_KDIM // 2
    fr = jnp.power(jnp.float32(RET_THETA), -2.0 * jnp.arange(half_r, dtype=jnp.float32) / RET_KDIM)[:, None]
    j = jnp.arange(LANES)[:, None]
    l64 = (jnp.arange(LANES) % DA_HALF)[None, :]
    sel_c = (j < half_a) & (l64 < ROPE_DIM) & (l64 % half_a == j)
    sel_lo = (j >= half_a) & (j < ROPE_DIM) & (l64 < half_a) & (l64 == j - half_a)
    sel_hi = (j >= half_a) & (j < ROPE_DIM) & (l64 >= half_a) & (l64 < ROPE_DIM) & (l64 == j)
    sel = jnp.concatenate([sel_c.astype(jnp.float32), -sel_lo.astype(jnp.float32),
                           sel_hi.astype(jnp.float32)], axis=1).astype(jnp.bfloat16)
    reps = PROJ_CHUNK // DA_HALF
    full = lambda shape: pl.BlockSpec(shape, lambda i: (0,) * len(shape))
    return pl.pallas_call(
        _in_proj_kernel,
        grid=(n // tm,),
        in_specs=[
            pl.BlockSpec((tm, D_MODEL), lambda i: (blk0 + i, 0)),
            pl.BlockSpec((1, tm), lambda i: (0, blk0 + i)),
            full((1, D_MODEL)),
            full((D_MODEL, IN_COLS)),
            full((PROJ_CHUNK, PROJ_CHUNK)),
            full((1, PROJ_CHUNK)),
            full((1, PROJ_CHUNK)),
            full((half_a, 1)),
            full((half_r, 1)),
            full((LANES, 3 * LANES)),
        ],
        out_specs=pl.BlockSpec((tm, IN_COLS), lambda i: (i, 0)),
        out_shape=jax.ShapeDtypeStruct((n, IN_COLS), jnp.bfloat16),
        scratch_shapes=[pltpu.VMEM((tm, D_MODEL), jnp.bfloat16)],
        compiler_params=pltpu.CompilerParams(dimension_semantics=("arbitrary",),
                                             vmem_limit_bytes=VMEM_LIMIT),
        name="in_proj",
    )(x2, pos2, g1.reshape(1, D_MODEL), w_in.astype(jnp.bfloat16), gsum,
      jnp.tile(gq, reps)[None, :], jnp.tile(gk, reps)[None, :], fa, fr, sel)


def _diff_attn_kernel(q_ref, k_ref, v_ref, lam_ref, gsub_ref, o_ref,
                      qs_scr, vx_scr, s0_scr, s1_scr, p_scr, m_scr, alpha_scr, acc_scr):
    i = pl.program_id(2)
    t = q_ref.shape[0]

    @pl.when(i == 0)
    def _():
        vx_scr[:, :DA_VDIM] = v_ref[...]
        vx_scr[:, DA_VDIM:] = jnp.ones((vx_scr.shape[0], LANES), vx_scr.dtype)

    q = q_ref[...]
    lane = lax.broadcasted_iota(jnp.int32, q.shape, 1)
    zero = jnp.zeros_like(q)
    qs_scr[:t] = jnp.where(lane < DA_HALF, q, zero)
    qs_scr[t:] = jnp.where(lane >= DA_HALF, q, zero)
    m_scr[...] = jnp.full(m_scr.shape, -jnp.inf, jnp.float32)
    acc_scr[...] = jnp.zeros(acc_scr.shape, jnp.float32)

    def scores(j, s_ref):
        start = pl.multiple_of(j * t, t)
        s_ref[...] = _dot_nt(qs_scr[...], k_ref[pl.ds(start, t), :])

    def softmax_pv(j, s_ref, masked):
        for c in range(2 * t // ATT_ROWS):
            rows = pl.ds(c * ATT_ROWS, ATT_ROWS)
            s = s_ref[rows, :]
            if masked:
                r = lax.broadcasted_iota(jnp.int32, s.shape, 0) + (c * ATT_ROWS) % t
                col = lax.broadcasted_iota(jnp.int32, s.shape, 1)
                s = jnp.where(col <= r, s, -jnp.inf)
            m_prev = m_scr[rows, :]
            m_new = jnp.maximum(m_prev, jnp.max(s, axis=-1, keepdims=True))
            alpha_scr[rows, :] = jnp.exp(m_prev - m_new)
            m_scr[rows, :] = m_new
            p = jnp.exp(s - jnp.concatenate([m_new] * (t // LANES), axis=1))
            p_scr[rows, :] = p.astype(p_scr.dtype)
        start = pl.multiple_of(j * t, t)
        pv = _dot(p_scr[...], vx_scr[pl.ds(start, t), :])
        alpha = alpha_scr[...]
        for half in range(2):
            cols = pl.ds(half * LANES, LANES)
            acc_scr[:, cols] = alpha * acc_scr[:, cols] + pv[:, half * LANES:(half + 1) * LANES]

    scores(0, s0_scr)

    def pair(jj, carry):
        j = 2 * jj
        scores(j + 1, s1_scr)
        softmax_pv(j, s0_scr, False)
        scores(j + 2, s0_scr)
        softmax_pv(j + 1, s1_scr, False)
        return carry

    lax.fori_loop(0, i // 2, pair, 0)

    @pl.when(i % 2 == 1)
    def _():
        scores(i, s1_scr)
        softmax_pv(i - 1, s0_scr, False)
        softmax_pv(i, s1_scr, True)

    @pl.when(i % 2 == 0)
    def _():
        softmax_pv(i, s0_scr, True)

    lam4 = lam_ref[...]
    lam = (jnp.exp(jnp.sum(lam4[0:1] * lam4[1:2], axis=-1, keepdims=True))
           - jnp.exp(jnp.sum(lam4[2:3] * lam4[3:4], axis=-1, keepdims=True)) + LAMBDA_INIT)
    o_all = acc_scr[:, :DA_VDIM] / acc_scr[:, DA_VDIM:]
    o = o_all[:t] - lam * o_all[t:]
    o = o * lax.rsqrt(jnp.mean(o * o, axis=-1, keepdims=True) + EPS) * gsub_ref[...] * (1.0 - LAMBDA_INIT)
    o_ref[...] = o.astype(o_ref.dtype)


def _diff_attn(proj, lam4, gsub, batch, seq):
    n = proj.shape[0]
    t = min(ATT_TILE, seq)
    nq = seq // t
    qb, kb, vb = COL_QA // LANES, COL_KA // LANES, COL_VA // LANES
    return pl.pallas_call(
        _diff_attn_kernel,
        grid=(batch, DA_HEADS, nq),
        in_specs=[
            pl.BlockSpec((t, LANES), lambda b, h, i: (b * nq + i, qb + h)),
            pl.BlockSpec((seq, LANES), lambda b, h, i: (b, kb + h)),
            pl.BlockSpec((seq, LANES), lambda b, h, i: (b, vb + h)),
            pl.BlockSpec((4, LANES), lambda b, h, i: (0, 0)),
            pl.BlockSpec((1, LANES), lambda b, h, i: (0, 0)),
        ],
        out_specs=pl.BlockSpec((t, LANES), lambda b, h, i: (b * nq + i, h)),
        out_shape=jax.ShapeDtypeStruct((n, DA_WIDTH), jnp.bfloat16),
        scratch_shapes=[pltpu.VMEM((2 * t, LANES), jnp.bfloat16),
                        pltpu.VMEM((seq, DA_VDIM + LANES), jnp.bfloat16),
                        pltpu.VMEM((2 * t, t), jnp.float32),
                        pltpu.VMEM((2 * t, t), jnp.float32),
                        pltpu.VMEM((2 * t, t), jnp.bfloat16),
                        pltpu.VMEM((2 * t, LANES), jnp.float32),
                        pltpu.VMEM((2 * t, LANES), jnp.float32),
                        pltpu.VMEM((2 * t, DA_VDIM + LANES), jnp.float32)],
        compiler_params=pltpu.CompilerParams(dimension_semantics=("arbitrary",) * 3,
                                             vmem_limit_bytes=VMEM_LIMIT),
        name="diff_attn",
    )(proj, proj, proj, lam4, gsub)


def _retention_kernel(q_ref, k_ref, v_ref, g_ref, gng_ref, gnb_ref, o_ref, r_scr, *, chunk):
    hf = jnp.full((1, 1), pl.program_id(1), jnp.int32).astype(jnp.float32)
    log_g = jnp.log1p(-jnp.exp2(-5.0 - hf))
    ri = lax.broadcasted_iota(jnp.int32, (chunk, chunk), 0)
    ci = lax.broadcasted_iota(jnp.int32, (chunk, chunk), 1)
    rel = (ri - ci).astype(jnp.float32)
    dmask = jnp.where(rel >= 0, jnp.exp(jnp.maximum(rel, 0.0) * log_g), 0.0)
    idx = lax.broadcasted_iota(jnp.int32, (chunk, 1), 0).astype(jnp.float32)
    zeta = jnp.exp((chunk - 1 - idx) * log_g)
    xi = jnp.exp((idx + 1.0) * log_g)
    g_chunk = jnp.exp(chunk * log_g)
    r_scr[...] = jnp.zeros(r_scr.shape, jnp.float32)
    gng = gng_ref[...]
    gnb = gnb_ref[...]

    def body(n, carry):
        start = pl.multiple_of(n * chunk, chunk)
        q = q_ref[pl.ds(start, chunk), :]
        k = k_ref[pl.ds(start, chunk), :]
        v = v_ref[pl.ds(start, chunk), :]
        s = _dot_nt(q, k) * dmask
        r_old = r_scr[...]
        o = _dot(s.astype(jnp.bfloat16), v) + xi * _dot(q, r_old.astype(jnp.bfloat16))
        kz = (k.astype(jnp.float32) * zeta).astype(jnp.bfloat16)
        r_scr[...] = g_chunk * r_old + _dot_tn(kz, v)
        mu = jnp.mean(o, axis=-1, keepdims=True)
        d = o - mu
        var = jnp.mean(d * d, axis=-1, keepdims=True)
        y = d * lax.rsqrt(var + EPS) * gng + gnb
        y = y * g_ref[pl.ds(start, chunk), :].astype(jnp.float32)
        o_ref[pl.ds(start, chunk), :] = y.astype(o_ref.dtype)
        return carry

    lax.fori_loop(0, q_ref.shape[0] // chunk, body, 0, unroll=RET_UNROLL)


def _retention(proj, gn_g, gn_b, batch, seq):
    n = proj.shape[0]
    chunk = min(RET_CHUNK, seq)
    col = lambda c0: (lambda b, h: (b, c0 // LANES + h))
    return pl.pallas_call(
        functools.partial(_retention_kernel, chunk=chunk),
        grid=(batch, RET_HEADS),
        in_specs=[
            pl.BlockSpec((seq, LANES), col(COL_QR)),
            pl.BlockSpec((seq, LANES), col(COL_KR)),
            pl.BlockSpec((seq, LANES), col(COL_VR)),
            pl.BlockSpec((seq, LANES), col(COL_GB)),
            pl.BlockSpec((1, LANES), lambda b, h: (0, h)),
            pl.BlockSpec((1, LANES), lambda b, h: (0, h)),
        ],
        out_specs=pl.BlockSpec((seq, LANES), lambda b, h: (b, h)),
        out_shape=jax.ShapeDtypeStruct((n, RET_WIDTH), jnp.bfloat16),
        scratch_shapes=[pltpu.VMEM((RET_KDIM, RET_VDIM), jnp.float32)],
        compiler_params=pltpu.CompilerParams(dimension_semantics=("arbitrary",) * 2,
                                             vmem_limit_bytes=VMEM_LIMIT),
        name="retention",
    )(proj, proj, proj, proj, gn_g.reshape(1, RET_WIDTH), gn_b.reshape(1, RET_WIDTH))


def _merge_kernel(x_ref, oa_ref, ob_ref, sa0_ref, sa1_ref, sb0_ref, sb1_ref, wa_ref, wb_ref, wo_ref,
                  g2_ref, wr_hi_ref, wr_lo_ref, br_ref, tri_ref, x1_ref, h2_ref, route_ref, counts_ref,
                  base_scr):
    ya = _dot(oa_ref[...], wa_ref[...])
    yb = _dot(ob_ref[...], wb_ref[...])
    sa = jnp.concatenate([sa0_ref[...], sa1_ref[...]], axis=1).astype(jnp.float32)
    sb = jnp.concatenate([sb0_ref[...], sb1_ref[...]], axis=1).astype(jnp.float32)
    merged = sa * ya + sb * yb
    x1 = x_ref[...] + _dot(merged.astype(jnp.bfloat16), wo_ref[...])
    x1_ref[...] = x1
    h2 = x1 * lax.rsqrt(jnp.mean(x1 * x1, axis=-1, keepdims=True) + EPS) * g2_ref[...]
    _store_packed(h2_ref, h2)

    hi = h2.astype(jnp.bfloat16)
    lo = (h2 - hi.astype(jnp.float32)).astype(jnp.bfloat16)
    logits = (_dot(hi, wr_hi_ref[...]) + _dot(lo, wr_hi_ref[...]) + _dot(hi, wr_lo_ref[...])
              + br_ref[...])
    lane = lax.broadcasted_iota(jnp.int32, logits.shape, 1)
    neg = -jnp.inf
    gl = jnp.where(lane < N_GROUPS, logits, neg)
    gmax = jnp.max(gl, axis=-1, keepdims=True)
    g_idx = jnp.min(jnp.where(gl == gmax, lane, LANES), axis=-1, keepdims=True)
    p_g = 1.0 / jnp.sum(jnp.exp(gl - gmax), axis=-1, keepdims=True)
    e_lo = N_GROUPS + EXPERTS_PER_GROUP * g_idx
    el = jnp.where((lane >= e_lo) & (lane < e_lo + EXPERTS_PER_GROUP), logits, neg)
    v1 = jnp.max(el, axis=-1, keepdims=True)
    i1 = jnp.min(jnp.where(el == v1, lane, LANES), axis=-1, keepdims=True)
    el2 = jnp.where(lane == i1, neg, el)
    v2 = jnp.max(el2, axis=-1, keepdims=True)
    i2 = jnp.min(jnp.where(el2 == v2, lane, LANES), axis=-1, keepdims=True)
    t = jnp.exp(v2 - v1)
    w1 = p_g / (1.0 + t)
    w2 = p_g * t / (1.0 + t)
    e1 = i1 - N_GROUPS
    e2 = i2 - N_GROUPS

    @pl.when(pl.program_id(0) == 0)
    def _():
        base_scr[...] = jnp.zeros(base_scr.shape, jnp.float32)

    oh1 = lane == e1
    oh2 = lane == e2
    picked = jnp.where(oh1 | oh2, 1.0, 0.0)
    before = _dot(tri_ref[...], picked.astype(jnp.bfloat16)) + base_scr[0:1, :]
    rank1 = jnp.sum(jnp.where(oh1, before, 0.0), axis=-1, keepdims=True)
    rank2 = jnp.sum(jnp.where(oh2, before, 0.0), axis=-1, keepdims=True)
    base_scr[...] = base_scr[...] + jnp.sum(picked, axis=0, keepdims=True)
    counts_ref[...] = base_scr[...]

    cols = [e1.astype(jnp.float32), e2.astype(jnp.float32), w1, w2, rank1, rank2]
    route = jnp.zeros(logits.shape, jnp.float32)
    for c, val in enumerate(cols):
        route = jnp.where(lane == c, val, route)
    route_ref[...] = route


def _merge(x2, oa, ob, proj, wa, wb, wo, g2, w_gr, b_gr, w_er, b_er, row0):
    n = oa.shape[0]
    tm = min(PROJ_ROWS, n)
    blk0 = row0 // tm
    half = D_MODEL // 2
    wr = jnp.zeros((D_MODEL, LANES), jnp.float32)
    wr = wr.at[:, :N_GROUPS].set(w_gr).at[:, N_GROUPS:N_GROUPS + N_EXPERTS].set(w_er)
    wr_hi = wr.astype(jnp.bfloat16)
    wr_lo = (wr - wr_hi.astype(jnp.float32)).astype(jnp.bfloat16)
    br = jnp.zeros((1, LANES), jnp.float32)
    br = br.at[0, :N_GROUPS].set(b_gr).at[0, N_GROUPS:N_GROUPS + N_EXPERTS].set(b_er)
    tri = (jnp.arange(tm)[:, None] > jnp.arange(tm)[None, :]).astype(jnp.bfloat16)
    full = lambda shape: pl.BlockSpec(shape, lambda i: (0,) * len(shape))
    gate = lambda c0: pl.BlockSpec((tm, half), lambda i: (i, c0 // half))
    return pl.pallas_call(
        _merge_kernel,
        grid=(n // tm,),
        in_specs=[
            pl.BlockSpec((tm, D_MODEL), lambda i: (blk0 + i, 0)),
            pl.BlockSpec((tm, DA_WIDTH), lambda i: (i, 0)),
            pl.BlockSpec((tm, RET_WIDTH), lambda i: (i, 0)),
            gate(COL_GATE_A), gate(COL_GATE_A + half), gate(COL_GATE_B), gate(COL_GATE_B + half),
            full((DA_WIDTH, D_MODEL)), full((RET_WIDTH, D_MODEL)), full((D_MODEL, D_MODEL)),
            full((1, D_MODEL)), full((D_MODEL, LANES)), full((D_MODEL, LANES)), full((1, LANES)),
            full((tm, tm)),
        ],
        out_specs=[
            pl.BlockSpec((tm, D_MODEL), lambda i: (i, 0)),
            pl.BlockSpec((ROW_PIECES, tm, PIECE), lambda i: (0, i, 0)),
            pl.BlockSpec((tm, LANES), lambda i: (i, 0)),
            pl.BlockSpec((8, LANES), lambda i: (0, 0)),
        ],
        out_shape=[
            jax.ShapeDtypeStruct((n, D_MODEL), jnp.float32),
            jax.ShapeDtypeStruct((ROW_PIECES, n, PIECE), jnp.uint32),
            jax.ShapeDtypeStruct((n, LANES), jnp.float32),
            jax.ShapeDtypeStruct((8, LANES), jnp.float32),
        ],
        scratch_shapes=[pltpu.VMEM((8, LANES), jnp.float32)],
        compiler_params=pltpu.CompilerParams(dimension_semantics=("arbitrary",),
                                             vmem_limit_bytes=VMEM_LIMIT),
        name="merge",
    )(x2, oa, ob, proj, proj, proj, proj, wa.astype(jnp.bfloat16), wb.astype(jnp.bfloat16),
      wo.astype(jnp.bfloat16), g2.reshape(1, D_MODEL), wr_hi, wr_lo, br, tri)


def _sc_mesh():
    return plsc.VectorSubcoreMesh(core_axis_name="c", subcore_axis_name="s")


def _sc_scatter_rows(src, idx, out_rows, src_block):
    steps = idx.shape[1] // SC_WINDOW
    per_core = steps // SC_CORES

    @pl.kernel(out_type=jax.ShapeDtypeStruct((out_rows, PIECE), src.dtype), mesh=_sc_mesh())
    def scatter(src_hbm, idx_hbm, out_hbm):
        def body(src_vmem, idx_vmem):
            pltpu.sync_copy(src_vmem, out_hbm.at[idx_vmem.at[0]])

        pltpu.emit_pipeline(
            body,
            grid=(SC_CORES, per_core),
            in_specs=[pl.BlockSpec((SC_WINDOW, PIECE), lambda c, i: (src_block(c * per_core + i), 0)),
                      pl.BlockSpec((1, SC_WINDOW), lambda c, i: (0, c * per_core + i))],
            out_specs=[],
            core_axis_name=("c", "s"),
            dimension_semantics=(pltpu.PARALLEL, pltpu.PARALLEL),
        )(src_hbm, idx_hbm)

    return scatter(src, idx)


def _sc_gather_rows(table, idx):
    num = idx.shape[1]
    per_core = num // SC_WINDOW // SC_CORES

    @pl.kernel(out_type=jax.ShapeDtypeStruct((num, PIECE), table.dtype), mesh=_sc_mesh())
    def gather(table_hbm, idx_hbm, out_hbm):
        def body(idx_vmem, out_vmem):
            pltpu.sync_copy(table_hbm.at[idx_vmem.at[0]], out_vmem)

        pltpu.emit_pipeline(
            body,
            grid=(SC_CORES, per_core),
            in_specs=[pl.BlockSpec((1, SC_WINDOW), lambda c, i: (0, c * per_core + i))],
            out_specs=[pl.BlockSpec((SC_WINDOW, PIECE), lambda c, i: (c * per_core + i, 0))],
            core_axis_name=("c", "s"),
            dimension_semantics=(pltpu.PARALLEL, pltpu.PARALLEL),
        )(idx_hbm, out_hbm)

    return gather(table, idx)


def _store_packed(ref, val):
    as_bits = lambda v: lax.bitcast_convert_type(v.astype(jnp.bfloat16).astype(jnp.float32), jnp.uint32)
    words = (as_bits(val[:, :PACKED]) >> 16) | (as_bits(val[:, PACKED:]) & jnp.uint32(0xFFFF0000))
    for j in range(ROW_PIECES):
        ref[j] = words[:, j * PIECE:(j + 1) * PIECE]


def _load_packed(ref):
    words = jnp.concatenate([ref[j] for j in range(ROW_PIECES)], axis=1)
    low = lax.bitcast_convert_type(words << 16, jnp.float32)
    high = lax.bitcast_convert_type(words & jnp.uint32(0xFFFF0000), jnp.float32)
    return jnp.concatenate([low, high], axis=1)


def _expert_kernel(blk_e_ref, n_used_ref, nxt_ref, run_ref, x_ref, wg_hbm, wu_hbm, wd_hbm, o_ref,
                   wg_stage, wu_stage, wd_stage, wg_scr, wu_scr, wd_scr, sem):
    i = pl.program_id(0)
    used = i < n_used_ref[0]

    def weight_copies(e, s):
        return (pltpu.make_async_copy(wg_hbm.at[e], wg_stage.at[s], sem.at[s, 0]),
                pltpu.make_async_copy(wu_hbm.at[e], wu_stage.at[s], sem.at[s, 1]),
                pltpu.make_async_copy(wd_hbm.at[e], wd_stage.at[s], sem.at[s, 2]))

    @pl.when(i == 0)
    def _():
        for c in weight_copies(blk_e_ref[0], 0):
            c.start()

    @pl.when(used & ((i == 0) | (blk_e_ref[i] != blk_e_ref[jnp.maximum(i - 1, 0)])))
    def _():
        s = run_ref[i] % 2
        for c in weight_copies(blk_e_ref[i], s):
            c.wait()
        wg_scr[...] = wg_stage[s].astype(jnp.bfloat16)
        wu_scr[...] = wu_stage[s].astype(jnp.bfloat16)
        wd_scr[...] = wd_stage[s].astype(jnp.bfloat16)

        @pl.when(nxt_ref[i] >= 0)
        def _():
            for c in weight_copies(nxt_ref[i], 1 - s):
                c.start()

    @pl.when(used)
    def _():
        x = _load_packed(x_ref).astype(jnp.bfloat16)
        a = _dot(x, wg_scr[...])
        u = _dot(x, wu_scr[...])
        hmid = (a * _sigmoid(a) * u).astype(jnp.bfloat16)
        _store_packed(o_ref, _dot(hmid, wd_scr[...]))

    @pl.when(jnp.logical_not(used))
    def _():
        o_ref[...] = jnp.zeros(o_ref.shape, o_ref.dtype)


def _experts(xs, blk_expert, n_used, w_gate, w_up, w_down):
    p = xs.shape[1]
    nblk = p // MOE_BLOCK
    idx = jnp.arange(nblk, dtype=jnp.int32)
    starts = (idx < n_used[0]) & ((idx == 0) | (blk_expert != jnp.roll(blk_expert, 1)))
    run = jnp.cumsum(starts.astype(jnp.int32)) - 1
    next_start = lax.cummin(jnp.where(starts, idx, nblk)[::-1])[::-1]
    after = jnp.concatenate([next_start[1:], jnp.full((1,), nblk, jnp.int32)])
    nxt = jnp.where(after < nblk, blk_expert[jnp.minimum(after, nblk - 1)], -1).astype(jnp.int32)
    live = lambda i, be, nu, nx, rn: jnp.minimum(i, nu[0] - 1)
    any_spec = pl.BlockSpec(memory_space=pl.ANY)
    return pl.pallas_call(
        _expert_kernel,
        grid_spec=pltpu.PrefetchScalarGridSpec(
            num_scalar_prefetch=4,
            grid=(nblk,),
            in_specs=[
                pl.BlockSpec((ROW_PIECES, MOE_BLOCK, PIECE), lambda i, be, nu, nx, rn: (0, live(i, be, nu, nx, rn), 0)),
                any_spec, any_spec, any_spec,
            ],
            out_specs=pl.BlockSpec((ROW_PIECES, MOE_BLOCK, PIECE), lambda i, be, nu, nx, rn: (0, i, 0)),
            scratch_shapes=[pltpu.VMEM((2, D_MODEL, EXPERT_FF), jnp.float32),
                            pltpu.VMEM((2, D_MODEL, EXPERT_FF), jnp.float32),
                            pltpu.VMEM((2, EXPERT_FF, D_MODEL), jnp.float32),
                            pltpu.VMEM((D_MODEL, EXPERT_FF), jnp.bfloat16),
                            pltpu.VMEM((D_MODEL, EXPERT_FF), jnp.bfloat16),
                            pltpu.VMEM((EXPERT_FF, D_MODEL), jnp.bfloat16),
                            pltpu.SemaphoreType.DMA((2, 3))],
        ),
        out_shape=jax.ShapeDtypeStruct((ROW_PIECES, p, PIECE), jnp.uint32),
        compiler_params=pltpu.CompilerParams(dimension_semantics=("arbitrary",),
                                             vmem_limit_bytes=VMEM_LIMIT),
        name="experts",
    )(blk_expert, n_used, nxt, run.astype(jnp.int32), xs, w_gate, w_up, w_down)


def _combine_kernel(x1_ref, route_ref, y0_ref, y1_ref, *rest):
    o_ref = rest[-1]
    route = route_ref[...]
    o_ref[...] = x1_ref[...] + route[:, 2:3] * _load_packed(y0_ref) + route[:, 3:4] * _load_packed(y1_ref)


def _combine(x1, yg, route, in_row0, out_row0, n_total, out_prev):
    n = yg.shape[1] // TOP_K
    tm = min(PROJ_ROWS, n)
    blk0 = out_row0 // tm
    in0 = in_row0 // tm
    prev = () if out_prev is None else (out_prev,)
    return pl.pallas_call(
        _combine_kernel,
        grid=(n // tm,),
        in_specs=[
            pl.BlockSpec((tm, D_MODEL), lambda i: (in0 + i, 0)),
            pl.BlockSpec((tm, LANES), lambda i: (in0 + i, 0)),
            pl.BlockSpec((ROW_PIECES, tm, PIECE), lambda i: (0, i, 0)),
            pl.BlockSpec((ROW_PIECES, tm, PIECE), lambda i: (0, i + n // tm, 0)),
        ] + [pl.BlockSpec(memory_space=pl.ANY)] * len(prev),
        out_specs=pl.BlockSpec((tm, D_MODEL), lambda i: (blk0 + i, 0)),
        out_shape=jax.ShapeDtypeStruct((n_total, D_MODEL), jnp.float32),
        input_output_aliases={4: 0} if prev else {},
        compiler_params=pltpu.CompilerParams(dimension_semantics=("arbitrary",),
                                             vmem_limit_bytes=VMEM_LIMIT),
        name="combine",
    )(x1, route, yg, yg, *prev)


def _dispatch_plan(route, counts, n):
    counts = counts[0, :N_EXPERTS].astype(jnp.int32)
    padded = ((counts + MOE_BLOCK - 1) // MOE_BLOCK) * MOE_BLOCK
    seg_end = jnp.cumsum(padded).astype(jnp.int32)
    seg_start = seg_end - padded
    cols = route[:, :8].T.astype(jnp.int32)
    e, rank = cols[0:TOP_K], cols[4:4 + TOP_K]
    picked = e[None] == jnp.arange(N_EXPERTS, dtype=jnp.int32)[:, None, None]
    dest = jnp.sum(jnp.where(picked, seg_start[:, None, None], 0), axis=0) + rank
    p = n * TOP_K + N_EXPERTS * MOE_BLOCK
    slot = dest[None] + (jnp.arange(ROW_PIECES, dtype=jnp.int32) * p)[:, None, None]
    blk_start = jnp.arange(p // MOE_BLOCK, dtype=jnp.int32) * MOE_BLOCK
    blk_expert = jnp.sum((seg_end[None, :] <= blk_start[:, None]).astype(jnp.int32), axis=1)
    blk_expert = jnp.minimum(blk_expert, N_EXPERTS - 1)
    n_used = (seg_end[-1] // MOE_BLOCK).reshape(1)
    return slot, blk_expert, n_used, p


def _layer_rows(x2, pos2, row0, batch, seq, out_prev, norm1_g, w_in, q_norm_g, k_norm_g, lam4, diff_subln_g,
                ret_gn_g, ret_gn_b, w_branch_a, w_branch_b, w_out, norm2_g, w_gr, b_gr, w_er, b_er,
                w_gate, w_up, w_down):
    n = batch * seq
    proj = _in_proj(x2, pos2, norm1_g, w_in, q_norm_g, k_norm_g, row0, n)
    oa = _diff_attn(proj, lam4, diff_subln_g.reshape(1, DA_VDIM), batch, seq)
    ob = _retention(proj, ret_gn_g, ret_gn_b, batch, seq)
    x1, h2, route, counts = _merge(x2, oa, ob, proj, w_branch_a, w_branch_b, w_out, norm2_g,
                                   w_gr, b_gr, w_er, b_er, row0)
    slot, blk_expert, n_used, p = _dispatch_plan(route, counts, n)
    win_n = n // SC_WINDOW
    src_block = lambda s: (s // (TOP_K * win_n)) * win_n + s % win_n
    xs = _sc_scatter_rows(h2.reshape(ROW_PIECES * n, PIECE), slot.reshape(1, -1), ROW_PIECES * p, src_block)
    ys = _experts(xs.reshape(ROW_PIECES, p, PIECE), blk_expert, n_used, w_gate, w_up, w_down)
    parts = COMBINE_PARTS if n % (COMBINE_PARTS * PROJ_ROWS) == 0 else 1
    m = n // parts
    out = out_prev
    for t in range(parts):
        yg = _sc_gather_rows(ys.reshape(ROW_PIECES * p, PIECE), slot[:, :, t * m:(t + 1) * m].reshape(1, -1))
        out = _combine(x1, yg.reshape(ROW_PIECES, TOP_K * m, PIECE), route, t * m, row0 + t * m,
                       x2.shape[0], out)
    return out


def _layer(x, positions, *weights):
    batch, seq, _ = x.shape
    n = batch * seq
    x2 = x.reshape(n, D_MODEL)
    pos2 = positions.reshape(1, n)
    groups = BATCH_GROUPS if batch % BATCH_GROUPS == 0 else 1
    per = batch // groups
    out = None
    for g in range(groups):
        out = _layer_rows(x2, pos2, g * per * seq, per, seq, out, *weights)
    return out.reshape(batch, seq, D_MODEL)


def kernel(x, positions, norm1_g, w_in, q_norm_g, k_norm_g, lambda_q1, lambda_k1, lambda_q2, lambda_k2, diff_subln_g, ret_gn_g, ret_gn_b, w_branch_a, w_branch_b, w_out, norm2_g, w_group_router, b_group_router, w_expert_router, b_expert_router, w_gate, w_up, w_down):
    assert x.shape[-1] == D_MODEL and norm1_g.shape[0] == 1, "single-layer, D_MODEL-wide input expected"
    lam4 = jnp.zeros((4, LANES), jnp.float32)
    lam4 = lam4.at[:, :DA_HALF].set(jnp.stack([lambda_q1[0], lambda_k1[0], lambda_q2[0], lambda_k2[0]]))
    return _layer(x, positions, norm1_g[0], w_in[0], q_norm_g[0], k_norm_g[0], lam4, diff_subln_g[0],
                  ret_gn_g[0], ret_gn_b[0], w_branch_a[0], w_branch_b[0], w_out[0], norm2_g[0],
                  w_group_router[0], b_group_router[0], w_expert_router[0], b_expert_router[0],
                  w_gate[0], w_up[0], w_down[0])
```

```python
import functools
import math

import jax
import jax.numpy as jnp
from jax import lax
from jax.experimental import pallas as pl
from jax.experimental.pallas import tpu as pltpu
from jax.experimental.pallas import tpu_sc as plsc

D_MODEL = 1024
DA_HEADS = 4
DA_HALF = 64
DA_VDIM = 2 * DA_HALF
DA_WIDTH = DA_HEADS * DA_VDIM
ROPE_THETA = 500000.0
ROPE_DIM = DA_HALF // 4
RET_HEADS = 4
RET_KDIM = 128
RET_VDIM = 128
RET_WIDTH = RET_HEADS * RET_VDIM
RET_THETA = 10000.0
N_GROUPS = 4
EXPERTS_PER_GROUP = 8
N_EXPERTS = N_GROUPS * EXPERTS_PER_GROUP
TOP_K = 2
EXPERT_FF = 512
EPS = 1e-6
LAMBDA_INIT = 0.8 - 0.6 * math.exp(-0.3 * 0)

LANES = 128
IN_COLS = 3 * DA_WIDTH + 4 * RET_WIDTH + 2 * D_MODEL
COL_QA, COL_KA, COL_VA = 0, DA_WIDTH, 2 * DA_WIDTH
COL_QR = 3 * DA_WIDTH
COL_KR = COL_QR + RET_WIDTH
COL_VR = COL_KR + RET_WIDTH
COL_GB = COL_VR + RET_WIDTH
COL_GATE_A = COL_GB + RET_WIDTH
COL_GATE_B = COL_GATE_A + D_MODEL

PROJ_ROWS = 512
PROJ_CHUNK = 256
ATT_TILE = 512
ATT_ROWS = 32
RET_CHUNK = 256
RET_UNROLL = 8
MOE_BLOCK = 512
PACKED = D_MODEL // 2
ROW_PIECES = 2
PIECE = PACKED // ROW_PIECES
SC_CORES = 2
SC_WINDOW = 128
COMBINE_PARTS = 2
BATCH_GROUPS = 1
VMEM_LIMIT = 56 * 1024 * 1024


def _dot(a, b):
    return jnp.dot(a, b, preferred_element_type=jnp.float32)


def _dot_nt(a, b):
    return lax.dot_general(a, b, (((1,), (1,)), ((), ())), preferred_element_type=jnp.float32)


def _dot_tn(a, b):
    return lax.dot_general(a, b, (((0,), (0,)), ((), ())), preferred_element_type=jnp.float32)


def _sigmoid(x):
    return 0.5 * jnp.tanh(0.5 * x) + 0.5


def _split3(x):
    a = x.astype(jnp.bfloat16)
    r = x - a.astype(jnp.float32)
    b = r.astype(jnp.bfloat16)
    c = (r - b.astype(jnp.float32)).astype(jnp.bfloat16)
    return a, b, c


def _in_proj_kernel(x_ref, pos_ref, g1_ref, w_ref, gsum_ref, gq_ref, gk_ref, fa_ref, fr_ref, sel_ref,
                    o_ref, h_scr):
    x = x_ref[...]
    h = x * lax.rsqrt(jnp.mean(x * x, axis=-1, keepdims=True) + EPS) * g1_ref[...]
    h_scr[...] = h.astype(jnp.bfloat16)
    rows = x.shape[0]
    pos = pos_ref[...].astype(jnp.float32)

    lane = lax.broadcasted_iota(jnp.int32, (rows, LANES), 1)
    half_a = ROPE_DIM // 2
    ang_a = fa_ref[...] * pos
    pad = jnp.zeros((LANES - 2 * half_a, rows), jnp.float32)
    t_a = jnp.concatenate([jnp.cos(ang_a), jnp.sin(ang_a), pad], axis=0).T
    tab = sum(_dot(part, sel_ref[...]) for part in _split3(t_a))
    lane64 = lane % DA_HALF
    c_a = tab[:, :LANES] + jnp.where(lane64 < ROPE_DIM, 0.0, 1.0)
    s_lo = tab[:, LANES:2 * LANES]
    s_hi = tab[:, 2 * LANES:]
    c_a2 = jnp.concatenate([c_a, c_a], axis=1)
    s_lo2 = jnp.concatenate([s_lo, s_lo], axis=1)
    s_hi2 = jnp.concatenate([s_hi, s_hi], axis=1)
    ang_r = fr_ref[...] * pos
    t_r = jnp.concatenate([jnp.cos(ang_r), jnp.sin(ang_r)], axis=0).T
    sw_r = pltpu.roll(t_r, RET_KDIM // 2, axis=1)
    first = lane < RET_KDIM // 2
    c_r = jnp.where(first, t_r, sw_r)
    s_r = jnp.where(first, -sw_r, t_r)
    c_r2 = jnp.concatenate([c_r, c_r], axis=1)
    s_r2 = jnp.concatenate([s_r, s_r], axis=1)

    def qk_norm_rope(y, g, scale):
        ss = y * y
        hi = ss.astype(jnp.bfloat16)
        lo = (ss - hi.astype(jnp.float32)).astype(jnp.bfloat16)
        gs = _dot(hi, gsum_ref[...]) + _dot(lo, gsum_ref[...])
        n = y * lax.rsqrt(gs * (1.0 / DA_HALF) + EPS) * g
        up = pltpu.roll(n, PROJ_CHUNK - half_a, axis=1)
        dn = pltpu.roll(n, half_a, axis=1)
        r = n * c_a2 + up * s_lo2 + dn * s_hi2
        return r * scale if scale != 1.0 else r

    def ret_rope(y, scale):
        halves = [pltpu.roll(y[:, i * LANES:(i + 1) * LANES], RET_KDIM // 2, axis=1)
                  for i in range(PROJ_CHUNK // LANES)]
        sw = jnp.concatenate(halves, axis=1)
        r = y * c_r2 + sw * s_r2
        return r * scale if scale != 1.0 else r

    n_chunks = IN_COLS // PROJ_CHUNK
    is_long = lambda c: c * PROJ_CHUNK < COL_VA or COL_QR <= c * PROJ_CHUNK < COL_VR
    long_chunks = [c for c in range(n_chunks) if is_long(c)]
    short_chunks = [c for c in range(n_chunks) if not is_long(c)][::-1]
    order = []
    while long_chunks or short_chunks:
        if long_chunks:
            order.append(long_chunks.pop(0))
        if short_chunks:
            order.append(short_chunks.pop(0))
    for c in order:
        c0 = c * PROJ_CHUNK
        y = _dot(h_scr[...], w_ref[:, c0:c0 + PROJ_CHUNK])
        if c0 < COL_KA:
            y = qk_norm_rope(y, gq_ref[...], DA_HALF ** -0.5)
        elif c0 < COL_VA:
            y = qk_norm_rope(y, gk_ref[...], 1.0)
        elif c0 < COL_QR:
            pass
        elif c0 < COL_KR:
            y = ret_rope(y, 1.0)
        elif c0 < COL_VR:
            y = ret_rope(y, RET_KDIM ** -0.5)
        elif c0 < COL_GB:
            pass
        elif c0 < COL_GATE_A:
            y = y * _sigmoid(y)
        else:
            y = _sigmoid(y)
        o_ref[:, c0:c0 + PROJ_CHUNK] = y.astype(o_ref.dtype)


def _in_proj(x2, pos2, g1, w_in, gq, gk, row0, n):
    tm = min(PROJ_ROWS, n)
    blk0 = row0 // tm
    grp = jnp.arange(PROJ_CHUNK) // DA_HALF
    gsum = (grp[:, None] == grp[None, :]).astype(jnp.bfloat16)
    half_a = ROPE_DIM // 2
    fa = jnp.power(jnp.float32(ROPE_THETA), -2.0 * jnp.arange(half_a, dtype=jnp.float32) / ROPE_DIM)[:, None]
    half_r = RET_KDIM // 2
    fr = jnp.power(jnp.float32(RET_THETA), -2.0 * jnp.arange(half_r, dtype=jnp.float32) / RET_KDIM)[:, None]
    j = jnp.arange(LANES)[:, None]
    l64 = (jnp.arange(LANES) % DA_HALF)[None, :]
    sel_c = (j < half_a) & (l64 < ROPE_DIM) & (l64 % half_a == j)
    sel_lo = (j >= half_a) & (j < ROPE_DIM) & (l64 < half_a) & (l64 == j - half_a)
    sel_hi = (j >= half_a) & (j < ROPE_DIM) & (l64 >= half_a) & (l64 < ROPE_DIM) & (l64 == j)
    sel = jnp.concatenate([sel_c.astype(jnp.float32), -sel_lo.astype(jnp.float32),
                           sel_hi.astype(jnp.float32)], axis=1).astype(jnp.bfloat16)
    reps = PROJ_CHUNK // DA_HALF
    full = lambda shape: pl.BlockSpec(shape, lambda i: (0,) * len(shape))
    return pl.pallas_call(
        _in_proj_kernel,
        grid=(n // tm,),
        in_specs=[
            pl.BlockSpec((tm, D_MODEL), lambda i: (blk0 + i, 0)),
            pl.BlockSpec((1, tm), lambda i: (0, blk0 + i)),
            full((1, D_MODEL)),
            full((D_MODEL, IN_COLS)),
            full((PROJ_CHUNK, PROJ_CHUNK)),
            full((1, PROJ_CHUNK)),
            full((1, PROJ_CHUNK)),
            full((half_a, 1)),
            full((half_r, 1)),
            full((LANES, 3 * LANES)),
        ],
        out_specs=pl.BlockSpec((tm, IN_COLS), lambda i: (i, 0)),
        out_shape=jax.ShapeDtypeStruct((n, IN_COLS), jnp.bfloat16),
        scratch_shapes=[pltpu.VMEM((tm, D_MODEL), jnp.bfloat16)],
        compiler_params=pltpu.CompilerParams(dimension_semantics=("arbitrary",),
                                             vmem_limit_bytes=VMEM_LIMIT),
        name="in_proj",
    )(x2, pos2, g1.reshape(1, D_MODEL), w_in.astype(jnp.bfloat16), gsum,
      jnp.tile(gq, reps)[None, :], jnp.tile(gk, reps)[None, :], fa, fr, sel)


def _diff_attn_kernel(q_ref, k_ref, v_ref, lam_ref, gsub_ref, o_ref,
                      qs_scr, vx_scr, s0_scr, s1_scr, p_scr, m_scr, alpha_scr, acc_scr):
    i = pl.program_id(2)
    t = q_ref.shape[0]

    @pl.when(i == 0)
    def _():
        vx_scr[:, :DA_VDIM] = v_ref[...]
        vx_scr[:, DA_VDIM:] = jnp.ones((vx_scr.shape[0], LANES), vx_scr.dtype)

    q = q_ref[...]
    lane = lax.broadcasted_iota(jnp.int32, q.shape, 1)
    zero = jnp.zeros_like(q)
    qs_scr[:t] = jnp.where(lane < DA_HALF, q, zero)
    qs_scr[t:] = jnp.where(lane >= DA_HALF, q, zero)
    m_scr[...] = jnp.full(m_scr.shape, -jnp.inf, jnp.float32)
    acc_scr[...] = jnp.zeros(acc_scr.shape, jnp.float32)

    def scores(j, s_ref):
        start = pl.multiple_of(j * t, t)
        s_ref[...] = _dot_nt(qs_scr[...], k_ref[pl.ds(start, t), :])

    def softmax_pv(j, s_ref, masked):
        for c in range(2 * t // ATT_ROWS):
            rows = pl.ds(c * ATT_ROWS, ATT_ROWS)
            s = s_ref[rows, :]
            if masked:
                r = lax.broadcasted_iota(jnp.int32, s.shape, 0) + (c * ATT_ROWS) % t
                col = lax.broadcasted_iota(jnp.int32, s.shape, 1)
                s = jnp.where(col <= r, s, -jnp.inf)
            m_prev = m_scr[rows, :]
            m_new = jnp.maximum(m_prev, jnp.max(s, axis=-1, keepdims=True))
            alpha_scr[rows, :] = jnp.exp(m_prev - m_new)
            m_scr[rows, :] = m_new
            p = jnp.exp(s - jnp.concatenate([m_new] * (t // LANES), axis=1))
            p_scr[rows, :] = p.astype(p_scr.dtype)
        start = pl.multiple_of(j * t, t)
        pv = _dot(p_scr[...], vx_scr[pl.ds(start, t), :])
        alpha = alpha_scr[...]
        for half in range(2):
            cols = pl.ds(half * LANES, LANES)
            acc_scr[:, cols] = alpha * acc_scr[:, cols] + pv[:, half * LANES:(half + 1) * LANES]

    scores(0, s0_scr)

    def pair(jj, carry):
        j = 2 * jj
        scores(j + 1, s1_scr)
        softmax_pv(j, s0_scr, False)
        scores(j + 2, s0_scr)
        softmax_pv(j + 1, s1_scr, False)
        return carry

    lax.fori_loop(0, i // 2, pair, 0)

    @pl.when(i % 2 == 1)
    def _():
        scores(i, s1_scr)
        softmax_pv(i - 1, s0_scr, False)
        softmax_pv(i, s1_scr, True)

    @pl.when(i % 2 == 0)
    def _():
        softmax_pv(i, s0_scr, True)

    lam4 = lam_ref[...]
    lam = (jnp.exp(jnp.sum(lam4[0:1] * lam4[1:2], axis=-1, keepdims=True))
           - jnp.exp(jnp.sum(lam4[2:3] * lam4[3:4], axis=-1, keepdims=True)) + LAMBDA_INIT)
    o_all = acc_scr[:, :DA_VDIM] / acc_scr[:, DA_VDIM:]
    o = o_all[:t] - lam * o_all[t:]
    o = o * lax.rsqrt(jnp.mean(o * o, axis=-1, keepdims=True) + EPS) * gsub_ref[...] * (1.0 - LAMBDA_INIT)
    o_ref[...] = o.astype(o_ref.dtype)


def _diff_attn(proj, lam4, gsub, batch, seq):
    n = proj.shape[0]
    t = min(ATT_TILE, seq)
    nq = seq // t
    qb, kb, vb = COL_QA // LANES, COL_KA // LANES, COL_VA // LANES
    return pl.pallas_call(
        _diff_attn_kernel,
        grid=(batch, DA_HEADS, nq),
        in_specs=[
            pl.BlockSpec((t, LANES), lambda b, h, i: (b * nq + i, qb + h)),
            pl.BlockSpec((seq, LANES), lambda b, h, i: (b, kb + h)),
            pl.BlockSpec((seq, LANES), lambda b, h, i: (b, vb + h)),
            pl.BlockSpec((4, LANES), lambda b, h, i: (0, 0)),
            pl.BlockSpec((1, LANES), lambda b, h, i: (0, 0)),
        ],
        out_specs=pl.BlockSpec((t, LANES), lambda b, h, i: (b * nq + i, h)),
        out_shape=jax.ShapeDtypeStruct((n, DA_WIDTH), jnp.bfloat16),
        scratch_shapes=[pltpu.VMEM((2 * t, LANES), jnp.bfloat16),
                        pltpu.VMEM((seq, DA_VDIM + LANES), jnp.bfloat16),
                        pltpu.VMEM((2 * t, t), jnp.float32),
                        pltpu.VMEM((2 * t, t), jnp.float32),
                        pltpu.VMEM((2 * t, t), jnp.bfloat16),
                        pltpu.VMEM((2 * t, LANES), jnp.float32),
                        pltpu.VMEM((2 * t, LANES), jnp.float32),
                        pltpu.VMEM((2 * t, DA_VDIM + LANES), jnp.float32)],
        compiler_params=pltpu.CompilerParams(dimension_semantics=("arbitrary",) * 3,
                                             vmem_limit_bytes=VMEM_LIMIT),
        name="diff_attn",
    )(proj, proj, proj, lam4, gsub)


def _retention_kernel(q_ref, k_ref, v_ref, g_ref, gng_ref, gnb_ref, o_ref, r_scr, *, chunk):
    hf = jnp.full((1, 1), pl.program_id(1), jnp.int32).astype(jnp.float32)
    log_g = jnp.log1p(-jnp.exp2(-5.0 - hf))
    ri = lax.broadcasted_iota(jnp.int32, (chunk, chunk), 0)
    ci = lax.broadcasted_iota(jnp.int32, (chunk, chunk), 1)
    rel = (ri - ci).astype(jnp.float32)
    dmask = jnp.where(rel >= 0, jnp.exp(jnp.maximum(rel, 0.0) * log_g), 0.0)
    idx = lax.broadcasted_iota(jnp.int32, (chunk, 1), 0).astype(jnp.float32)
    zeta = jnp.exp((chunk - 1 - idx) * log_g)
    xi = jnp.exp((idx + 1.0) * log_g)
    g_chunk = jnp.exp(chunk * log_g)
    r_scr[...] = jnp.zeros(r_scr.shape, jnp.float32)
    gng = gng_ref[...]
    gnb = gnb_ref[...]

    def body(n, carry):
        start = pl.multiple_of(n * chunk, chunk)
        q = q_ref[pl.ds(start, chunk), :]
        k = k_ref[pl.ds(start, chunk), :]
        v = v_ref[pl.ds(start, chunk), :]
        s = _dot_nt(q, k) * dmask
        r_old = r_scr[...]
        o = _dot(s.astype(jnp.bfloat16), v) + xi * _dot(q, r_old.astype(jnp.bfloat16))
        kz = (k.astype(jnp.float32) * zeta).astype(jnp.bfloat16)
        r_scr[...] = g_chunk * r_old + _dot_tn(kz, v)
        mu = jnp.mean(o, axis=-1, keepdims=True)
        d = o - mu
        var = jnp.mean(d * d, axis=-1, keepdims=True)
        y = d * lax.rsqrt(var + EPS) * gng + gnb
        y = y * g_ref[pl.ds(start, chunk), :].astype(jnp.float32)
        o_ref[pl.ds(start, chunk), :] = y.astype(o_ref.dtype)
        return carry

    lax.fori_loop(0, q_ref.shape[0] // chunk, body, 0, unroll=RET_UNROLL)


def _retention(proj, gn_g, gn_b, batch, seq):
    n = proj.shape[0]
    chunk = min(RET_CHUNK, seq)
    col = lambda c0: (lambda b, h: (b, c0 // LANES + h))
    return pl.pallas_call(
        functools.partial(_retention_kernel, chunk=chunk),
        grid=(batch, RET_HEADS),
        in_specs=[
            pl.BlockSpec((seq, LANES), col(COL_QR)),
            pl.BlockSpec((seq, LANES), col(COL_KR)),
            pl.BlockSpec((seq, LANES), col(COL_VR)),
            pl.BlockSpec((seq, LANES), col(COL_GB)),
            pl.BlockSpec((1, LANES), lambda b, h: (0, h)),
            pl.BlockSpec((1, LANES), lambda b, h: (0, h)),
        ],
        out_specs=pl.BlockSpec((seq, LANES), lambda b, h: (b, h)),
        out_shape=jax.ShapeDtypeStruct((n, RET_WIDTH), jnp.bfloat16),
        scratch_shapes=[pltpu.VMEM((RET_KDIM, RET_VDIM), jnp.float32)],
        compiler_params=pltpu.CompilerParams(dimension_semantics=("arbitrary",) * 2,
                                             vmem_limit_bytes=VMEM_LIMIT),
        name="retention",
    )(proj, proj, proj, proj, gn_g.reshape(1, RET_WIDTH), gn_b.reshape(1, RET_WIDTH))


def _merge_kernel(x_ref, oa_ref, ob_ref, sa0_ref, sa1_ref, sb0_ref, sb1_ref, wa_ref, wb_ref, wo_ref,
                  g2_ref, wr_hi_ref, wr_lo_ref, br_ref, tri_ref, x1_ref, h2_ref, route_ref, counts_ref,
                  base_scr):
    ya = _dot(oa_ref[...], wa_ref[...])
    yb = _dot(ob_ref[...], wb_ref[...])
    sa = jnp.concatenate([sa0_ref[...], sa1_ref[...]], axis=1).astype(jnp.float32)
    sb = jnp.concatenate([sb0_ref[...], sb1_ref[...]], axis=1).astype(jnp.float32)
    merged = sa * ya + sb * yb
    x1 = x_ref[...] + _dot(merged.astype(jnp.bfloat16), wo_ref[...])
    x1_ref[...] = x1
    h2 = x1 * lax.rsqrt(jnp.mean(x1 * x1, axis=-1, keepdims=True) + EPS) * g2_ref[...]
    _store_packed(h2_ref, h2)

    hi = h2.astype(jnp.bfloat16)
    lo = (h2 - hi.astype(jnp.float32)).astype(jnp.bfloat16)
    logits = (_dot(hi, wr_hi_ref[...]) + _dot(lo, wr_hi_ref[...]) + _dot(hi, wr_lo_ref[...])
              + br_ref[...])
    lane = lax.broadcasted_iota(jnp.int32, logits.shape, 1)
    neg = -jnp.inf
    gl = jnp.where(lane < N_GROUPS, logits, neg)
    gmax = jnp.max(gl, axis=-1, keepdims=True)
    g_idx = jnp.min(jnp.where(gl == gmax, lane, LANES), axis=-1, keepdims=True)
    p_g = 1.0 / jnp.sum(jnp.exp(gl - gmax), axis=-1, keepdims=True)
    e_lo = N_GROUPS + EXPERTS_PER_GROUP * g_idx
    el = jnp.where((lane >= e_lo) & (lane < e_lo + EXPERTS_PER_GROUP), logits, neg)
    v1 = jnp.max(el, axis=-1, keepdims=True)
    i1 = jnp.min(jnp.where(el == v1, lane, LANES), axis=-1, keepdims=True)
    el2 = jnp.where(lane == i1, neg, el)
    v2 = jnp.max(el2, axis=-1, keepdims=True)
    i2 = jnp.min(jnp.where(el2 == v2, lane, LANES), axis=-1, keepdims=True)
    t = jnp.exp(v2 - v1)
    w1 = p_g / (1.0 + t)
    w2 = p_g * t / (1.0 + t)
    e1 = i1 - N_GROUPS
    e2 = i2 - N_GROUPS

    @pl.when(pl.program_id(0) == 0)
    def _():
        base_scr[...] = jnp.zeros(base_scr.shape, jnp.float32)

    oh1 = lane == e1
    oh2 = lane == e2
    picked = jnp.where(oh1 | oh2, 1.0, 0.0)
    before = _dot(tri_ref[...], picked.astype(jnp.bfloat16)) + base_scr[0:1, :]
    rank1 = jnp.sum(jnp.where(oh1, before, 0.0), axis=-1, keepdims=True)
    rank2 = jnp.sum(jnp.where(oh2, before, 0.0), axis=-1, keepdims=True)
    base_scr[...] = base_scr[...] + jnp.sum(picked, axis=0, keepdims=True)
    counts_ref[...] = base_scr[...]

    cols = [e1.astype(jnp.float32), e2.astype(jnp.float32), w1, w2, rank1, rank2]
    route = jnp.zeros(logits.shape, jnp.float32)
    for c, val in enumerate(cols):
        route = jnp.where(lane == c, val, route)
    route_ref[...] = route


def _merge(x2, oa, ob, proj, wa, wb, wo, g2, w_gr, b_gr, w_er, b_er, row0):
    n = oa.shape[0]
    tm = min(PROJ_ROWS, n)
    blk0 = row0 // tm
    half = D_MODEL // 2
    wr = jnp.zeros((D_MODEL, LANES), jnp.float32)
    wr = wr.at[:, :N_GROUPS].set(w_gr).at[:, N_GROUPS:N_GROUPS + N_EXPERTS].set(w_er)
    wr_hi = wr.astype(jnp.bfloat16)
    wr_lo = (wr - wr_hi.astype(jnp.float32)).astype(jnp.bfloat16)
    br = jnp.zeros((1, LANES), jnp.float32)
    br = br.at[0, :N_GROUPS].set(b_gr).at[0, N_GROUPS:N_GROUPS + N_EXPERTS].set(b_er)
    tri = (jnp.arange(tm)[:, None] > jnp.arange(tm)[None, :]).astype(jnp.bfloat16)
    full = lambda shape: pl.BlockSpec(shape, lambda i: (0,) * len(shape))
    gate = lambda c0: pl.BlockSpec((tm, half), lambda i: (i, c0 // half))
    return pl.pallas_call(
        _merge_kernel,
        grid=(n // tm,),
        in_specs=[
            pl.BlockSpec((tm, D_MODEL), lambda i: (blk0 + i, 0)),
            pl.BlockSpec((tm, DA_WIDTH), lambda i: (i, 0)),
            pl.BlockSpec((tm, RET_WIDTH), lambda i: (i, 0)),
            gate(COL_GATE_A), gate(COL_GATE_A + half), gate(COL_GATE_B), gate(COL_GATE_B + half),
            full((DA_WIDTH, D_MODEL)), full((RET_WIDTH, D_MODEL)), full((D_MODEL, D_MODEL)),
            full((1, D_MODEL)), full((D_MODEL, LANES)), full((D_MODEL, LANES)), full((1, LANES)),
            full((tm, tm)),
        ],
        out_specs=[
            pl.BlockSpec((tm, D_MODEL), lambda i: (i, 0)),
            pl.BlockSpec((ROW_PIECES, tm, PIECE), lambda i: (0, i, 0)),
            pl.BlockSpec((tm, LANES), lambda i: (i, 0)),
            pl.BlockSpec((8, LANES), lambda i: (0, 0)),
        ],
        out_shape=[
            jax.ShapeDtypeStruct((n, D_MODEL), jnp.float32),
            jax.ShapeDtypeStruct((ROW_PIECES, n, PIECE), jnp.uint32),
            jax.ShapeDtypeStruct((n, LANES), jnp.float32),
            jax.ShapeDtypeStruct((8, LANES), jnp.float32),
        ],
        scratch_shapes=[pltpu.VMEM((8, LANES), jnp.float32)],
        compiler_params=pltpu.CompilerParams(dimension_semantics=("arbitrary",),
                                             vmem_limit_bytes=VMEM_LIMIT),
        name="merge",
    )(x2, oa, ob, proj, proj, proj, proj, wa.astype(jnp.bfloat16), wb.astype(jnp.bfloat16),
      wo.astype(jnp.bfloat16), g2.reshape(1, D_MODEL), wr_hi, wr_lo, br, tri)


def _sc_mesh():
    return plsc.VectorSubcoreMesh(core_axis_name="c", subcore_axis_name="s")


def _sc_scatter_rows(src, idx, out_rows, src_block):
    steps = idx.shape[1] // SC_WINDOW
    per_core = steps // SC_CORES

    @pl.kernel(out_type=jax.ShapeDtypeStruct((out_rows, PIECE), src.dtype), mesh=_sc_mesh())
    def scatter(src_hbm, idx_hbm, out_hbm):
        def body(src_vmem, idx_vmem):
            pltpu.sync_copy(src_vmem, out_hbm.at[idx_vmem.at[0]])

        pltpu.emit_pipeline(
            body,
            grid=(SC_CORES, per_core),
            in_specs=[pl.BlockSpec((SC_WINDOW, PIECE), lambda c, i: (src_block(c * per_core + i), 0)),
                      pl.BlockSpec((1, SC_WINDOW), lambda c, i: (0, c * per_core + i))],
            out_specs=[],
            core_axis_name=("c", "s"),
            dimension_semantics=(pltpu.PARALLEL, pltpu.PARALLEL),
        )(src_hbm, idx_hbm)

    return scatter(src, idx)


def _sc_gather_rows(table, idx):
    num = idx.shape[1]
    per_core = num // SC_WINDOW // SC_CORES

    @pl.kernel(out_type=jax.ShapeDtypeStruct((num, PIECE), table.dtype), mesh=_sc_mesh())
    def gather(table_hbm, idx_hbm, out_hbm):
        def body(idx_vmem, out_vmem):
            pltpu.sync_copy(table_hbm.at[idx_vmem.at[0]], out_vmem)

        pltpu.emit_pipeline(
            body,
            grid=(SC_CORES, per_core),
            in_specs=[pl.BlockSpec((1, SC_WINDOW), lambda c, i: (0, c * per_core + i))],
            out_specs=[pl.BlockSpec((SC_WINDOW, PIECE), lambda c, i: (c * per_core + i, 0))],
            core_axis_name=("c", "s"),
            dimension_semantics=(pltpu.PARALLEL, pltpu.PARALLEL),
        )(idx_hbm, out_hbm)

    return gather(table, idx)


def _store_packed(ref, val):
    as_bits = lambda v: lax.bitcast_convert_type(v.astype(jnp.bfloat16).astype(jnp.float32), jnp.uint32)
    words = (as_bits(val[:, :PACKED]) >> 16) | (as_bits(val[:, PACKED:]) & jnp.uint32(0xFFFF0000))
    for j in range(ROW_PIECES):
        ref[j] = words[:, j * PIECE:(j + 1) * PIECE]


def _load_packed(ref):
    words = jnp.concatenate([ref[j] for j in range(ROW_PIECES)], axis=1)
    low = lax.bitcast_convert_type(words << 16, jnp.float32)
    high = lax.bitcast_convert_type(words & jnp.uint32(0xFFFF0000), jnp.float32)
    return jnp.concatenate([low, high], axis=1)


def _expert_kernel(blk_e_ref, n_used_ref, nxt_ref, run_ref, x_ref, wg_hbm, wu_hbm, wd_hbm, o_ref,
                   wg_stage, wu_stage, wd_stage, wg_scr, wu_scr, wd_scr, sem):
    i = pl.program_id(0)
    used = i < n_used_ref[0]

    def weight_copies(e, s):
        return (pltpu.make_async_copy(wg_hbm.at[e], wg_stage.at[s], sem.at[s, 0]),
                pltpu.make_async_copy(wu_hbm.at[e], wu_stage.at[s], sem.at[s, 1]),
                pltpu.make_async_copy(wd_hbm.at[e], wd_stage.at[s], sem.at[s, 2]))

    @pl.when(i == 0)
    def _():
        for c in weight_copies(blk_e_ref[0], 0):
            c.start()

    @pl.when(used & ((i == 0) | (blk_e_ref[i] != blk_e_ref[jnp.maximum(i - 1, 0)])))
    def _():
        s = run_ref[i] % 2
        for c in weight_copies(blk_e_ref[i], s):
            c.wait()
        wg_scr[...] = wg_stage[s].astype(jnp.bfloat16)
        wu_scr[...] = wu_stage[s].astype(jnp.bfloat16)
        wd_scr[...] = wd_stage[s].astype(jnp.bfloat16)

        @pl.when(nxt_ref[i] >= 0)
        def _():
            for c in weight_copies(nxt_ref[i], 1 - s):
                c.start()

    @pl.when(used)
    def _():
        x = _load_packed(x_ref).astype(jnp.bfloat16)
        a = _dot(x, wg_scr[...])
        u = _dot(x, wu_scr[...])
        hmid = (a * _sigmoid(a) * u).astype(jnp.bfloat16)
        _store_packed(o_ref, _dot(hmid, wd_scr[...]))

    @pl.when(jnp.logical_not(used))
    def _():
        o_ref[...] = jnp.zeros(o_ref.shape, o_ref.dtype)


def _experts(xs, blk_expert, n_used, w_gate, w_up, w_down):
    p = xs.shape[1]
    nblk = p // MOE_BLOCK
    idx = jnp.arange(nblk, dtype=jnp.int32)
    starts = (idx < n_used[0]) & ((idx == 0) | (blk_expert != jnp.roll(blk_expert, 1)))
    run = jnp.cumsum(starts.astype(jnp.int32)) - 1
    next_start = lax.cummin(jnp.where(starts, idx, nblk)[::-1])[::-1]
    after = jnp.concatenate([next_start[1:], jnp.full((1,), nblk, jnp.int32)])
    nxt = jnp.where(after < nblk, blk_expert[jnp.minimum(after, nblk - 1)], -1).astype(jnp.int32)
    live = lambda i, be, nu, nx, rn: jnp.minimum(i, nu[0] - 1)
    any_spec = pl.BlockSpec(memory_space=pl.ANY)
    return pl.pallas_call(
        _expert_kernel,
        grid_spec=pltpu.PrefetchScalarGridSpec(
            num_scalar_prefetch=4,
            grid=(nblk,),
            in_specs=[
                pl.BlockSpec((ROW_PIECES, MOE_BLOCK, PIECE), lambda i, be, nu, nx, rn: (0, live(i, be, nu, nx, rn), 0)),
                any_spec, any_spec, any_spec,
            ],
            out_specs=pl.BlockSpec((ROW_PIECES, MOE_BLOCK, PIECE), lambda i, be, nu, nx, rn: (0, i, 0)),
            scratch_shapes=[pltpu.VMEM((2, D_MODEL, EXPERT_FF), jnp.float32),
                            pltpu.VMEM((2, D_MODEL, EXPERT_FF), jnp.float32),
                            pltpu.VMEM((2, EXPERT_FF, D_MODEL), jnp.float32),
                            pltpu.VMEM((D_MODEL, EXPERT_FF), jnp.bfloat16),
                            pltpu.VMEM((D_MODEL, EXPERT_FF), jnp.bfloat16),
                            pltpu.VMEM((EXPERT_FF, D_MODEL), jnp.bfloat16),
                            pltpu.SemaphoreType.DMA((2, 3))],
        ),
        out_shape=jax.ShapeDtypeStruct((ROW_PIECES, p, PIECE), jnp.uint32),
        compiler_params=pltpu.CompilerParams(dimension_semantics=("arbitrary",),
                                             vmem_limit_bytes=VMEM_LIMIT),
        name="experts",
    )(blk_expert, n_used, nxt, run.astype(jnp.int32), xs, w_gate, w_up, w_down)


def _combine_kernel(x1_ref, route_ref, y0_ref, y1_ref, *rest):
    o_ref = rest[-1]
    route = route_ref[...]
    o_ref[...] = x1_ref[...] + route[:, 2:3] * _load_packed(y0_ref) + route[:, 3:4] * _load_packed(y1_ref)


def _combine(x1, yg, route, in_row0, out_row0, n_total, out_prev):
    n = yg.shape[1] // TOP_K
    tm = min(PROJ_ROWS, n)
    blk0 = out_row0 // tm
    in0 = in_row0 // tm
    prev = () if out_prev is None else (out_prev,)
    return pl.pallas_call(
        _combine_kernel,
        grid=(n // tm,),
        in_specs=[
            pl.BlockSpec((tm, D_MODEL), lambda i: (in0 + i, 0)),
            pl.BlockSpec((tm, LANES), lambda i: (in0 + i, 0)),
            pl.BlockSpec((ROW_PIECES, tm, PIECE), lambda i: (0, i, 0)),
            pl.BlockSpec((ROW_PIECES, tm, PIECE), lambda i: (0, i + n // tm, 0)),
        ] + [pl.BlockSpec(memory_space=pl.ANY)] * len(prev),
        out_specs=pl.BlockSpec((tm, D_MODEL), lambda i: (blk0 + i, 0)),
        out_shape=jax.ShapeDtypeStruct((n_total, D_MODEL), jnp.float32),
        input_output_aliases={4: 0} if prev else {},
        compiler_params=pltpu.CompilerParams(dimension_semantics=("arbitrary",),
                                             vmem_limit_bytes=VMEM_LIMIT),
        name="combine",
    )(x1, route, yg, yg, *prev)


def _dispatch_plan(route, counts, n):
    counts = counts[0, :N_EXPERTS].astype(jnp.int32)
    padded = ((counts + MOE_BLOCK - 1) // MOE_BLOCK) * MOE_BLOCK
    seg_end = jnp.cumsum(padded).astype(jnp.int32)
    seg_start = seg_end - padded
    cols = route[:, :8].T.astype(jnp.int32)
    e, rank = cols[0:TOP_K], cols[4:4 + TOP_K]
    picked = e[None] == jnp.arange(N_EXPERTS, dtype=jnp.int32)[:, None, None]
    dest = jnp.sum(jnp.where(picked, seg_start[:, None, None], 0), axis=0) + rank
    p = n * TOP_K + N_EXPERTS * MOE_BLOCK
    slot = dest[None] + (jnp.arange(ROW_PIECES, dtype=jnp.int32) * p)[:, None, None]
    blk_start = jnp.arange(p // MOE_BLOCK, dtype=jnp.int32) * MOE_BLOCK
    blk_expert = jnp.sum((seg_end[None, :] <= blk_start[:, None]).astype(jnp.int32), axis=1)
    blk_expert = jnp.minimum(blk_expert, N_EXPERTS - 1)
    n_used = (seg_end[-1] // MOE_BLOCK).reshape(1)
    return slot, blk_expert, n_used, p


def _layer_rows(x2, pos2, row0, batch, seq, out_prev, norm1_g, w_in, q_norm_g, k_norm_g, lam4, diff_subln_g,
                ret_gn_g, ret_gn_b, w_branch_a, w_branch_b, w_out, norm2_g, w_gr, b_gr, w_er, b_er,
                w_gate, w_up, w_down):
    n = batch * seq
    proj = _in_proj(x2, pos2, norm1_g, w_in, q_norm_g, k_norm_g, row0, n)
    oa = _diff_attn(proj, lam4, diff_subln_g.reshape(1, DA_VDIM), batch, seq)
    ob = _retention(proj, ret_gn_g, ret_gn_b, batch, seq)
    x1, h2, route, counts = _merge(x2, oa, ob, proj, w_branch_a, w_branch_b, w_out, norm2_g,
                                   w_gr, b_gr, w_er, b_er, row0)
    slot, blk_expert, n_used, p = _dispatch_plan(route, counts, n)
    win_n = n // SC_WINDOW
    src_block = lambda s: (s // (TOP_K * win_n)) * win_n + s % win_n
    xs = _sc_scatter_rows(h2.reshape(ROW_PIECES * n, PIECE), slot.reshape(1, -1), ROW_PIECES * p, src_block)
    ys = _experts(xs.reshape(ROW_PIECES, p, PIECE), blk_expert, n_used, w_gate, w_up, w_down)
    parts = COMBINE_PARTS if n % (COMBINE_PARTS * PROJ_ROWS) == 0 else 1
    m = n // parts
    out = out_prev
    for t in range(parts):
        yg = _sc_gather_rows(ys.reshape(ROW_PIECES * p, PIECE), slot[:, :, t * m:(t + 1) * m].reshape(1, -1))
        out = _combine(x1, yg.reshape(ROW_PIECES, TOP_K * m, PIECE), route, t * m, row0 + t * m,
                       x2.shape[0], out)
    return out


def _layer(x, positions, *weights):
    batch, seq, _ = x.shape
    n = batch * seq
    x2 = x.reshape(n, D_MODEL)
    pos2 = positions.reshape(1, n)
    groups = BATCH_GROUPS if batch % BATCH_GROUPS == 0 else 1
    per = batch // groups
    out = None
    for g in range(groups):
        out = _layer_rows(x2, pos2, g * per * seq, per, seq, out, *weights)
    return out.reshape(batch, seq, D_MODEL)


def kernel(x, positions, norm1_g, w_in, q_norm_g, k_norm_g, lambda_q1, lambda_k1, lambda_q2, lambda_k2, diff_subln_g, ret_gn_g, ret_gn_b, w_branch_a, w_branch_b, w_out, norm2_g, w_group_router, b_group_router, w_expert_router, b_expert_router, w_gate, w_up, w_down):
    assert x.shape[-1] == D_MODEL and norm1_g.shape[0] == 1, "single-layer, D_MODEL-wide input expected"
    lam4 = jnp.zeros((4, LANES), jnp.float32)
    lam4 = lam4.at[:, :DA_HALF].set(jnp.stack([lambda_q1[0], lambda_k1[0], lambda_q2[0], lambda_k2[0]]))
    return _layer(x, positions, norm1_g[0], w_in[0], q_norm_g[0], k_norm_g[0], lam4, diff_subln_g[0],
                  ret_gn_g[0], ret_gn_b[0], w_branch_a[0], w_branch_b[0], w_out[0], norm2_g[0],
                  w_group_router[0], b_group_router[0], w_expert_router[0], b_expert_router[0],
                  w_gate[0], w_up[0], w_down[0])
```

```python
import functools
import math

import jax
import jax.numpy as jnp
from jax import lax
from jax.experimental import pallas as pl
from jax.experimental.pallas import tpu as pltpu
from jax.experimental.pallas import tpu_sc as plsc

D_MODEL = 1024
DA_HEADS = 4
DA_HALF = 64
DA_VDIM = 2 * DA_HALF
DA_WIDTH = DA_HEADS * DA_VDIM
ROPE_THETA = 500000.0
ROPE_DIM = DA_HALF // 4
RET_HEADS = 4
RET_KDIM = 128
RET_VDIM = 128
RET_WIDTH = RET_HEADS * RET_VDIM
RET_THETA = 10000.0
N_GROUPS = 4
EXPERTS_PER_GROUP = 8
N_EXPERTS = N_GROUPS * EXPERTS_PER_GROUP
TOP_K = 2
EXPERT_FF = 512
EPS = 1e-6
LAMBDA_INIT = 0.8 - 0.6 * math.exp(-0.3 * 0)

LANES = 128
IN_COLS = 3 * DA_WIDTH + 4 * RET_WIDTH + 2 * D_MODEL
COL_QA, COL_KA, COL_VA = 0, DA_WIDTH, 2 * DA_WIDTH
COL_QR = 3 * DA_WIDTH
COL_KR = COL_QR + RET_WIDTH
COL_VR = COL_KR + RET_WIDTH
COL_GB = COL_VR + RET_WIDTH
COL_GATE_A = COL_GB + RET_WIDTH
COL_GATE_B = COL_GATE_A + D_MODEL

PROJ_ROWS = 512
PROJ_CHUNK = 256
ATT_TILE = 512
ATT_ROWS = 32
RET_CHUNK = 256
RET_UNROLL = 16
MOE_BLOCK = 512
PACKED = D_MODEL // 2
ROW_PIECES = 2
PIECE = PACKED // ROW_PIECES
SC_CORES = 2
SC_WINDOW = 128
COMBINE_PARTS = 2
BATCH_GROUPS = 1
VMEM_LIMIT = 56 * 1024 * 1024


def _dot(a, b):
    return jnp.dot(a, b, preferred_element_type=jnp.float32)


def _dot_nt(a, b):
    return lax.dot_general(a, b, (((1,), (1,)), ((), ())), preferred_element_type=jnp.float32)


def _dot_tn(a, b):
    return lax.dot_general(a, b, (((0,), (0,)), ((), ())), preferred_element_type=jnp.float32)


def _sigmoid(x):
    return 0.5 * jnp.tanh(0.5 * x) + 0.5


def _split3(x):
    a = x.astype(jnp.bfloat16)
    r = x - a.astype(jnp.float32)
    b = r.astype(jnp.bfloat16)
    c = (r - b.astype(jnp.float32)).astype(jnp.bfloat16)
    return a, b, c


def _in_proj_kernel(x_ref, pos_ref, g1_ref, w_ref, gsum_ref, gq_ref, gk_ref, fa_ref, fr_ref, sel_ref,
                    o_ref, h_scr):
    x = x_ref[...]
    rows = x.shape[0]
    h_scr[...] = (x * g1_ref[...]).astype(jnp.bfloat16)
    rms_scale = jnp.broadcast_to(lax.rsqrt(jnp.mean(x * x, axis=-1, keepdims=True) + EPS), (rows, PROJ_CHUNK))
    pos = pos_ref[...].astype(jnp.float32)

    lane = lax.broadcasted_iota(jnp.int32, (rows, LANES), 1)
    half_a = ROPE_DIM // 2
    tables = {}

    def da_tables():
        if "da" not in tables:
            ang_a = fa_ref[...] * pos
            pad = jnp.zeros((LANES - 2 * half_a, rows), jnp.float32)
            t_a = jnp.concatenate([jnp.cos(ang_a), jnp.sin(ang_a), pad], axis=0).T
            tab = sum(_dot(part, sel_ref[...]) for part in _split3(t_a))
            c_a = tab[:, :LANES] + jnp.where(lane % DA_HALF < ROPE_DIM, 0.0, 1.0)
            s_lo = tab[:, LANES:2 * LANES]
            s_hi = tab[:, 2 * LANES:]
            tables["da"] = tuple(jnp.concatenate([v, v], axis=1) for v in (c_a, s_lo, s_hi))
        return tables["da"]

    def ret_tables():
        if "ret" not in tables:
            ang_r = fr_ref[...] * pos
            t_r = jnp.concatenate([jnp.cos(ang_r), jnp.sin(ang_r)], axis=0).T
            sw_r = pltpu.roll(t_r, RET_KDIM // 2, axis=1)
            first = lane < RET_KDIM // 2
            c_r = jnp.where(first, t_r, sw_r)
            s_r = jnp.where(first, -sw_r, t_r)
            tables["ret"] = tuple(jnp.concatenate([v, v], axis=1) for v in (c_r, s_r))
        return tables["ret"]

    def qk_norm_rope(y, g, scale):
        c_a2, s_lo2, s_hi2 = da_tables()
        ss = y * y
        hi = ss.astype(jnp.bfloat16)
        lo = (ss - hi.astype(jnp.float32)).astype(jnp.bfloat16)
        gs = _dot(hi, gsum_ref[...]) + _dot(lo, gsum_ref[...])
        n = y * lax.rsqrt(gs * (1.0 / DA_HALF) + EPS) * g
        up = pltpu.roll(n, PROJ_CHUNK - half_a, axis=1)
        dn = pltpu.roll(n, half_a, axis=1)
        r = n * c_a2 + up * s_lo2 + dn * s_hi2
        return r * scale if scale != 1.0 else r

    def ret_rope(y, scale):
        c_r2, s_r2 = ret_tables()
        halves = [pltpu.roll(y[:, i * LANES:(i + 1) * LANES], RET_KDIM // 2, axis=1)
                  for i in range(PROJ_CHUNK // LANES)]
        sw = jnp.concatenate(halves, axis=1)
        r = y * c_r2 + sw * s_r2
        return r * scale if scale != 1.0 else r

    n_chunks = IN_COLS // PROJ_CHUNK
    is_long = lambda c: c * PROJ_CHUNK < COL_VA or COL_QR <= c * PROJ_CHUNK < COL_VR
    long_chunks = [c for c in range(n_chunks) if is_long(c)]
    short_chunks = [c for c in range(n_chunks) if not is_long(c)][::-1]
    order = [short_chunks.pop(0) for _ in range(3)]
    while long_chunks or short_chunks:
        if long_chunks:
            order.append(long_chunks.pop(0))
        if short_chunks:
            order.append(short_chunks.pop(0))
    for c in order:
        c0 = c * PROJ_CHUNK
        y = _dot(h_scr[...], w_ref[:, c0:c0 + PROJ_CHUNK]) * rms_scale
        if c0 < COL_KA:
            y = qk_norm_rope(y, gq_ref[...], DA_HALF ** -0.5)
        elif c0 < COL_VA:
            y = qk_norm_rope(y, gk_ref[...], 1.0)
        elif c0 < COL_QR:
            pass
        elif c0 < COL_KR:
            y = ret_rope(y, 1.0)
        elif c0 < COL_VR:
            y = ret_rope(y, RET_KDIM ** -0.5)
        elif c0 < COL_GB:
            pass
        elif c0 < COL_GATE_A:
            y = y * _sigmoid(y)
        else:
            y = _sigmoid(y)
        o_ref[:, c0:c0 + PROJ_CHUNK] = y.astype(o_ref.dtype)


def _in_proj(x2, pos2, g1, w_in, gq, gk, row0, n):
    tm = min(PROJ_ROWS, n)
    blk0 = row0 // tm
    grp = jnp.arange(PROJ_CHUNK) // DA_HALF
    gsum = (grp[:, None] == grp[None, :]).astype(jnp.bfloat16)
    half_a = ROPE_DIM // 2
    fa = jnp.power(jnp.float32(ROPE_THETA), -2.0 * jnp.arange(half_a, dtype=jnp.float32) / ROPE_DIM)[:, None]
    half_r = RET_KDIM // 2
    fr = jnp.power(jnp.float32(RET_THETA), -2.0 * jnp.arange(half_r, dtype=jnp.float32) / RET_KDIM)[:, None]
    j = jnp.arange(LANES)[:, None]
    l64 = (jnp.arange(LANES) % DA_HALF)[None, :]
    sel_c = (j < half_a) & (l64 < ROPE_DIM) & (l64 % half_a == j)
    sel_lo = (j >= half_a) & (j < ROPE_DIM) & (l64 < half_a) & (l64 == j - half_a)
    sel_hi = (j >= half_a) & (j < ROPE_DIM) & (l64 >= half_a) & (l64 < ROPE_DIM) & (l64 == j)
    sel = jnp.concatenate([sel_c.astype(jnp.float32), -sel_lo.astype(jnp.float32),
                           sel_hi.astype(jnp.float32)], axis=1).astype(jnp.bfloat16)
    reps = PROJ_CHUNK // DA_HALF
    full = lambda shape: pl.BlockSpec(shape, lambda i: (0,) * len(shape))
    return pl.pallas_call(
        _in_proj_kernel,
        grid=(n // tm,),
        in_specs=[
            pl.BlockSpec((tm, D_MODEL), lambda i: (blk0 + i, 0)),
            pl.BlockSpec((1, tm), lambda i: (0, blk0 + i)),
            full((1, D_MODEL)),
            full((D_MODEL, IN_COLS)),
            full((PROJ_CHUNK, PROJ_CHUNK)),
            full((1, PROJ_CHUNK)),
            full((1, PROJ_CHUNK)),
            full((half_a, 1)),
            full((half_r, 1)),
            full((LANES, 3 * LANES)),
        ],
        out_specs=pl.BlockSpec((tm, IN_COLS), lambda i: (i, 0)),
        out_shape=jax.ShapeDtypeStruct((n, IN_COLS), jnp.bfloat16),
        scratch_shapes=[pltpu.VMEM((tm, D_MODEL), jnp.bfloat16)],
        compiler_params=pltpu.CompilerParams(dimension_semantics=("arbitrary",),
                                             vmem_limit_bytes=VMEM_LIMIT),
        name="in_proj",
    )(x2, pos2, g1.reshape(1, D_MODEL), w_in.astype(jnp.bfloat16), gsum,
      jnp.tile(gq, reps)[None, :], jnp.tile(gk, reps)[None, :], fa, fr, sel)


def _diff_attn_kernel(q_ref, k_ref, v_ref, lam_ref, gsub_ref, o_ref,
                      qs_scr, vx_scr, s0_scr, s1_scr, p_scr, m_scr, alpha_scr, acc_scr):
    i = pl.program_id(2)
    t = q_ref.shape[0]

    @pl.when(i == 0)
    def _():
        vx_scr[:, :DA_VDIM] = v_ref[...]
        vx_scr[:, DA_VDIM:] = jnp.ones((vx_scr.shape[0], LANES), vx_scr.dtype)

    q = q_ref[...]
    lane = lax.broadcasted_iota(jnp.int32, q.shape, 1)
    zero = jnp.zeros_like(q)
    qs_scr[:t] = jnp.where(lane < DA_HALF, q, zero)
    qs_scr[t:] = jnp.where(lane >= DA_HALF, q, zero)
    m_scr[...] = jnp.full(m_scr.shape, -jnp.inf, jnp.float32)
    acc_scr[...] = jnp.zeros(acc_scr.shape, jnp.float32)

    def scores(j, s_ref):
        start = pl.multiple_of(j * t, t)
        s_ref[...] = _dot_nt(qs_scr[...], k_ref[pl.ds(start, t), :])

    def softmax_pv(j, s_ref, masked):
        for c in range(2 * t // ATT_ROWS):
            rows = pl.ds(c * ATT_ROWS, ATT_ROWS)
            s = s_ref[rows, :]
            if masked:
                r = lax.broadcasted_iota(jnp.int32, s.shape, 0) + (c * ATT_ROWS) % t
                col = lax.broadcasted_iota(jnp.int32, s.shape, 1)
                s = jnp.where(col <= r, s, -jnp.inf)
            m_prev = m_scr[rows, :]
            m_new = jnp.maximum(m_prev, jnp.max(s, axis=-1, keepdims=True))
            alpha_scr[rows, :] = jnp.exp(m_prev - m_new)
            m_scr[rows, :] = m_new
            p = jnp.exp(s - jnp.concatenate([m_new] * (t // LANES), axis=1))
            p_scr[rows, :] = p.astype(p_scr.dtype)
        start = pl.multiple_of(j * t, t)
        pv = _dot(p_scr[...], vx_scr[pl.ds(start, t), :])
        alpha = alpha_scr[...]
        for half in range(2):
            cols = pl.ds(half * LANES, LANES)
            acc_scr[:, cols] = alpha * acc_scr[:, cols] + pv[:, half * LANES:(half + 1) * LANES]

    scores(0, s0_scr)

    def pair(jj, carry):
        j = 2 * jj
        scores(j + 1, s1_scr)
        softmax_pv(j, s0_scr, False)
        scores(j + 2, s0_scr)
        softmax_pv(j + 1, s1_scr, False)
        return carry

    lax.fori_loop(0, i // 2, pair, 0)

    @pl.when(i % 2 == 1)
    def _():
        scores(i, s1_scr)
        softmax_pv(i - 1, s0_scr, False)
        softmax_pv(i, s1_scr, True)

    @pl.when(i % 2 == 0)
    def _():
        softmax_pv(i, s0_scr, True)

    lam4 = lam_ref[...]
    lam = (jnp.exp(jnp.sum(lam4[0:1] * lam4[1:2], axis=-1, keepdims=True))
           - jnp.exp(jnp.sum(lam4[2:3] * lam4[3:4], axis=-1, keepdims=True)) + LAMBDA_INIT)
    o_all = acc_scr[:, :DA_VDIM] / acc_scr[:, DA_VDIM:]
    o = o_all[:t] - lam * o_all[t:]
    o = o * lax.rsqrt(jnp.mean(o * o, axis=-1, keepdims=True) + EPS) * gsub_ref[...] * (1.0 - LAMBDA_INIT)
    o_ref[...] = o.astype(o_ref.dtype)


def _diff_attn(proj, lam4, gsub, batch, seq):
    n = proj.shape[0]
    t = min(ATT_TILE, seq)
    nq = seq // t
    qb, kb, vb = COL_QA // LANES, COL_KA // LANES, COL_VA // LANES
    return pl.pallas_call(
        _diff_attn_kernel,
        grid=(batch, DA_HEADS, nq),
        in_specs=[
            pl.BlockSpec((t, LANES), lambda b, h, i: (b * nq + i, qb + h)),
            pl.BlockSpec((seq, LANES), lambda b, h, i: (b, kb + h)),
            pl.BlockSpec((seq, LANES), lambda b, h, i: (b, vb + h)),
            pl.BlockSpec((4, LANES), lambda b, h, i: (0, 0)),
            pl.BlockSpec((1, LANES), lambda b, h, i: (0, 0)),
        ],
        out_specs=pl.BlockSpec((t, LANES), lambda b, h, i: (b * nq + i, h)),
        out_shape=jax.ShapeDtypeStruct((n, DA_WIDTH), jnp.bfloat16),
        scratch_shapes=[pltpu.VMEM((2 * t, LANES), jnp.bfloat16),
                        pltpu.VMEM((seq, DA_VDIM + LANES), jnp.bfloat16),
                        pltpu.VMEM((2 * t, t), jnp.float32),
                        pltpu.VMEM((2 * t, t), jnp.float32),
                        pltpu.VMEM((2 * t, t), jnp.bfloat16),
                        pltpu.VMEM((2 * t, LANES), jnp.float32),
                        pltpu.VMEM((2 * t, LANES), jnp.float32),
                        pltpu.VMEM((2 * t, DA_VDIM + LANES), jnp.float32)],
        compiler_params=pltpu.CompilerParams(dimension_semantics=("arbitrary",) * 3,
                                             vmem_limit_bytes=VMEM_LIMIT),
        name="diff_attn",
    )(proj, proj, proj, lam4, gsub)


def _retention_kernel(q_ref, k_ref, v_ref, g_ref, gng_ref, gnb_ref, o_ref, r_scr, *, chunk):
    hf = jnp.full((1, 1), pl.program_id(1), jnp.int32).astype(jnp.float32)
    log_g = jnp.log1p(-jnp.exp2(-5.0 - hf))
    ri = lax.broadcasted_iota(jnp.int32, (chunk, chunk), 0)
    ci = lax.broadcasted_iota(jnp.int32, (chunk, chunk), 1)
    rel = (ri - ci).astype(jnp.float32)
    dmask = jnp.where(rel >= 0, jnp.exp(jnp.maximum(rel, 0.0) * log_g), 0.0)
    idx = lax.broadcasted_iota(jnp.int32, (chunk, 1), 0).astype(jnp.float32)
    zeta = jnp.exp((chunk - 1 - idx) * log_g)
    xi = jnp.exp((idx + 1.0) * log_g)
    g_chunk = jnp.exp(chunk * log_g)
    r_scr[...] = jnp.zeros(r_scr.shape, jnp.float32)
    gng = gng_ref[...]
    gnb = gnb_ref[...]

    def body(n, carry):
        start = pl.multiple_of(n * chunk, chunk)
        q = q_ref[pl.ds(start, chunk), :]
        k = k_ref[pl.ds(start, chunk), :]
        v = v_ref[pl.ds(start, chunk), :]
        s = _dot_nt(q, k) * dmask
        r_old = r_scr[...]
        o = _dot(s.astype(jnp.bfloat16), v) + xi * _dot(q, r_old.astype(jnp.bfloat16))
        kz = (k.astype(jnp.float32) * zeta).astype(jnp.bfloat16)
        r_scr[...] = g_chunk * r_old + _dot_tn(kz, v)
        mu = jnp.mean(o, axis=-1, keepdims=True)
        d = o - mu
        var = jnp.mean(d * d, axis=-1, keepdims=True)
        y = d * lax.rsqrt(var + EPS) * gng + gnb
        y = y * g_ref[pl.ds(start, chunk), :].astype(jnp.float32)
        o_ref[pl.ds(start, chunk), :] = y.astype(o_ref.dtype)
        return carry

    lax.fori_loop(0, q_ref.shape[0] // chunk, body, 0, unroll=RET_UNROLL)


def _retention(proj, gn_g, gn_b, batch, seq):
    n = proj.shape[0]
    chunk = min(RET_CHUNK, seq)
    col = lambda c0: (lambda b, h: (b, c0 // LANES + h))
    return pl.pallas_call(
        functools.partial(_retention_kernel, chunk=chunk),
        grid=(batch, RET_HEADS),
        in_specs=[
            pl.BlockSpec((seq, LANES), col(COL_QR)),
            pl.BlockSpec((seq, LANES), col(COL_KR)),
            pl.BlockSpec((seq, LANES), col(COL_VR)),
            pl.BlockSpec((seq, LANES), col(COL_GB)),
            pl.BlockSpec((1, LANES), lambda b, h: (0, h)),
            pl.BlockSpec((1, LANES), lambda b, h: (0, h)),
        ],
        out_specs=pl.BlockSpec((seq, LANES), lambda b, h: (b, h)),
        out_shape=jax.ShapeDtypeStruct((n, RET_WIDTH), jnp.bfloat16),
        scratch_shapes=[pltpu.VMEM((RET_KDIM, RET_VDIM), jnp.float32)],
        compiler_params=pltpu.CompilerParams(dimension_semantics=("arbitrary",) * 2,
                                             vmem_limit_bytes=VMEM_LIMIT),
        name="retention",
    )(proj, proj, proj, proj, gn_g.reshape(1, RET_WIDTH), gn_b.reshape(1, RET_WIDTH))


def _merge_kernel(x_ref, oa_ref, ob_ref, sa0_ref, sa1_ref, sb0_ref, sb1_ref, wa_ref, wb_ref, wo_ref,
                  g2_ref, wr_hi_ref, wr_lo_ref, br_ref, tri_ref, x1_ref, h2_ref, route_ref, counts_ref,
                  base_scr):
    ya = _dot(oa_ref[...], wa_ref[...])
    yb = _dot(ob_ref[...], wb_ref[...])
    sa = jnp.concatenate([sa0_ref[...], sa1_ref[...]], axis=1).astype(jnp.float32)
    sb = jnp.concatenate([sb0_ref[...], sb1_ref[...]], axis=1).astype(jnp.float32)
    merged = sa * ya + sb * yb
    x1 = x_ref[...] + _dot(merged.astype(jnp.bfloat16), wo_ref[...])
    x1_ref[...] = x1
    h2 = x1 * lax.rsqrt(jnp.mean(x1 * x1, axis=-1, keepdims=True) + EPS) * g2_ref[...]
    _store_packed(h2_ref, h2)

    hi = h2.astype(jnp.bfloat16)
    lo = (h2 - hi.astype(jnp.float32)).astype(jnp.bfloat16)
    logits = (_dot(hi, wr_hi_ref[...]) + _dot(lo, wr_hi_ref[...]) + _dot(hi, wr_lo_ref[...])
              + br_ref[...])
    lane = lax.broadcasted_iota(jnp.int32, logits.shape, 1)
    neg = -jnp.inf
    gl = jnp.where(lane < N_GROUPS, logits, neg)
    gmax = jnp.max(gl, axis=-1, keepdims=True)
    g_idx = jnp.min(jnp.where(gl == gmax, lane, LANES), axis=-1, keepdims=True)
    p_g = 1.0 / jnp.sum(jnp.exp(gl - gmax), axis=-1, keepdims=True)
    e_lo = N_GROUPS + EXPERTS_PER_GROUP * g_idx
    el = jnp.where((lane >= e_lo) & (lane < e_lo + EXPERTS_PER_GROUP), logits, neg)
    v1 = jnp.max(el, axis=-1, keepdims=True)
    i1 = jnp.min(jnp.where(el == v1, lane, LANES), axis=-1, keepdims=True)
    el2 = jnp.where(lane == i1, neg, el)
    v2 = jnp.max(el2, axis=-1, keepdims=True)
    i2 = jnp.min(jnp.where(el2 == v2, lane, LANES), axis=-1, keepdims=True)
    t = jnp.exp(v2 - v1)
    w1 = p_g / (1.0 + t)
    w2 = p_g * t / (1.0 + t)
    e1 = i1 - N_GROUPS
    e2 = i2 - N_GROUPS

    @pl.when(pl.program_id(0) == 0)
    def _():
        base_scr[...] = jnp.zeros(base_scr.shape, jnp.float32)

    oh1 = lane == e1
    oh2 = lane == e2
    picked = jnp.where(oh1 | oh2, 1.0, 0.0)
    before = _dot(tri_ref[...], picked.astype(jnp.bfloat16)) + base_scr[0:1, :]
    rank1 = jnp.sum(jnp.where(oh1, before, 0.0), axis=-1, keepdims=True)
    rank2 = jnp.sum(jnp.where(oh2, before, 0.0), axis=-1, keepdims=True)
    base_scr[...] = base_scr[...] + jnp.sum(picked, axis=0, keepdims=True)
    counts_ref[...] = base_scr[...]

    cols = [e1.astype(jnp.float32), e2.astype(jnp.float32), w1, w2, rank1, rank2]
    route = jnp.zeros(logits.shape, jnp.float32)
    for c, val in enumerate(cols):
        route = jnp.where(lane == c, val, route)
    route_ref[...] = route


def _merge(x2, oa, ob, proj, wa, wb, wo, g2, w_gr, b_gr, w_er, b_er, row0):
    n = oa.shape[0]
    tm = min(PROJ_ROWS, n)
    blk0 = row0 // tm
    half = D_MODEL // 2
    wr = jnp.zeros((D_MODEL, LANES), jnp.float32)
    wr = wr.at[:, :N_GROUPS].set(w_gr).at[:, N_GROUPS:N_GROUPS + N_EXPERTS].set(w_er)
    wr_hi = wr.astype(jnp.bfloat16)
    wr_lo = (wr - wr_hi.astype(jnp.float32)).astype(jnp.bfloat16)
    br = jnp.zeros((1, LANES), jnp.float32)
    br = br.at[0, :N_GROUPS].set(b_gr).at[0, N_GROUPS:N_GROUPS + N_EXPERTS].set(b_er)
    tri = (jnp.arange(tm)[:, None] > jnp.arange(tm)[None, :]).astype(jnp.bfloat16)
    full = lambda shape: pl.BlockSpec(shape, lambda i: (0,) * len(shape))
    gate = lambda c0: pl.BlockSpec((tm, half), lambda i: (i, c0 // half))
    return pl.pallas_call(
        _merge_kernel,
        grid=(n // tm,),
        in_specs=[
            pl.BlockSpec((tm, D_MODEL), lambda i: (blk0 + i, 0)),
            pl.BlockSpec((tm, DA_WIDTH), lambda i: (i, 0)),
            pl.BlockSpec((tm, RET_WIDTH), lambda i: (i, 0)),
            gate(COL_GATE_A), gate(COL_GATE_A + half), gate(COL_GATE_B), gate(COL_GATE_B + half),
            full((DA_WIDTH, D_MODEL)), full((RET_WIDTH, D_MODEL)), full((D_MODEL, D_MODEL)),
            full((1, D_MODEL)), full((D_MODEL, LANES)), full((D_MODEL, LANES)), full((1, LANES)),
            full((tm, tm)),
        ],
        out_specs=[
            pl.BlockSpec((tm, D_MODEL), lambda i: (i, 0)),
            pl.BlockSpec((ROW_PIECES, tm, PIECE), lambda i: (0, i, 0)),
            pl.BlockSpec((tm, LANES), lambda i: (i, 0)),
            pl.BlockSpec((8, LANES), lambda i: (0, 0)),
        ],
        out_shape=[
            jax.ShapeDtypeStruct((n, D_MODEL), jnp.float32),
            jax.ShapeDtypeStruct((ROW_PIECES, n, PIECE), jnp.uint32),
            jax.ShapeDtypeStruct((n, LANES), jnp.float32),
            jax.ShapeDtypeStruct((8, LANES), jnp.float32),
        ],
        scratch_shapes=[pltpu.VMEM((8, LANES), jnp.float32)],
        compiler_params=pltpu.CompilerParams(dimension_semantics=("arbitrary",),
                                             vmem_limit_bytes=VMEM_LIMIT),
        name="merge",
    )(x2, oa, ob, proj, proj, proj, proj, wa.astype(jnp.bfloat16), wb.astype(jnp.bfloat16),
      wo.astype(jnp.bfloat16), g2.reshape(1, D_MODEL), wr_hi, wr_lo, br, tri)


def _sc_mesh():
    return plsc.VectorSubcoreMesh(core_axis_name="c", subcore_axis_name="s")


def _sc_scatter_rows(src, idx, out_rows, src_block):
    steps = idx.shape[1] // SC_WINDOW
    per_core = steps // SC_CORES

    @pl.kernel(out_type=jax.ShapeDtypeStruct((out_rows, PIECE), src.dtype), mesh=_sc_mesh())
    def scatter(src_hbm, idx_hbm, out_hbm):
        def body(src_vmem, idx_vmem):
            pltpu.sync_copy(src_vmem, out_hbm.at[idx_vmem.at[0]])

        pltpu.emit_pipeline(
            body,
            grid=(SC_CORES, per_core),
            in_specs=[pl.BlockSpec((SC_WINDOW, PIECE), lambda c, i: (src_block(c * per_core + i), 0)),
                      pl.BlockSpec((1, SC_WINDOW), lambda c, i: (0, c * per_core + i))],
            out_specs=[],
            core_axis_name=("c", "s"),
            dimension_semantics=(pltpu.PARALLEL, pltpu.PARALLEL),
        )(src_hbm, idx_hbm)

    return scatter(src, idx)


def _sc_gather_rows(table, idx):
    num = idx.shape[1]
    per_core = num // SC_WINDOW // SC_CORES

    @pl.kernel(out_type=jax.ShapeDtypeStruct((num, PIECE), table.dtype), mesh=_sc_mesh())
    def gather(table_hbm, idx_hbm, out_hbm):
        def body(idx_vmem, out_vmem):
            pltpu.sync_copy(table_hbm.at[idx_vmem.at[0]], out_vmem)

        pltpu.emit_pipeline(
            body,
            grid=(SC_CORES, per_core),
            in_specs=[pl.BlockSpec((1, SC_WINDOW), lambda c, i: (0, c * per_core + i))],
            out_specs=[pl.BlockSpec((SC_WINDOW, PIECE), lambda c, i: (c * per_core + i, 0))],
            core_axis_name=("c", "s"),
            dimension_semantics=(pltpu.PARALLEL, pltpu.PARALLEL),
        )(idx_hbm, out_hbm)

    return gather(table, idx)


def _store_packed(ref, val):
    as_bits = lambda v: lax.bitcast_convert_type(v.astype(jnp.bfloat16).astype(jnp.float32), jnp.uint32)
    words = (as_bits(val[:, :PACKED]) >> 16) | (as_bits(val[:, PACKED:]) & jnp.uint32(0xFFFF0000))
    for j in range(ROW_PIECES):
        ref[j] = words[:, j * PIECE:(j + 1) * PIECE]


def _load_packed(ref):
    words = jnp.concatenate([ref[j] for j in range(ROW_PIECES)], axis=1)
    low = lax.bitcast_convert_type(words << 16, jnp.float32)
    high = lax.bitcast_convert_type(words & jnp.uint32(0xFFFF0000), jnp.float32)
    return jnp.concatenate([low, high], axis=1)


def _expert_kernel(blk_e_ref, n_used_ref, nxt_ref, run_ref, x_ref, wg_hbm, wu_hbm, wd_hbm, o_ref,
                   wg_stage, wu_stage, wd_stage, wg_scr, wu_scr, wd_scr, sem):
    i = pl.program_id(0)
    used = i < n_used_ref[0]

    def weight_copies(e, s):
        return (pltpu.make_async_copy(wg_hbm.at[e], wg_stage.at[s], sem.at[s, 0]),
                pltpu.make_async_copy(wu_hbm.at[e], wu_stage.at[s], sem.at[s, 1]),
                pltpu.make_async_copy(wd_hbm.at[e], wd_stage.at[s], sem.at[s, 2]))

    @pl.when(i == 0)
    def _():
        for c in weight_copies(blk_e_ref[0], 0):
            c.start()

    @pl.when(used & ((i == 0) | (blk_e_ref[i] != blk_e_ref[jnp.maximum(i - 1, 0)])))
    def _():
        s = run_ref[i] % 2
        for c in weight_copies(blk_e_ref[i], s):
            c.wait()
        wg_scr[...] = wg_stage[s].astype(jnp.bfloat16)
        wu_scr[...] = wu_stage[s].astype(jnp.bfloat16)
        wd_scr[...] = wd_stage[s].astype(jnp.bfloat16)

        @pl.when(nxt_ref[i] >= 0)
        def _():
            for c in weight_copies(nxt_ref[i], 1 - s):
                c.start()

    @pl.when(used)
    def _():
        x = _load_packed(x_ref).astype(jnp.bfloat16)
        a = _dot(x, wg_scr[...])
        u = _dot(x, wu_scr[...])
        hmid = (a * _sigmoid(a) * u).astype(jnp.bfloat16)
        _store_packed(o_ref, _dot(hmid, wd_scr[...]))

    @pl.when(jnp.logical_not(used))
    def _():
        o_ref[...] = jnp.zeros(o_ref.shape, o_ref.dtype)


def _experts(xs, blk_expert, n_used, w_gate, w_up, w_down):
    p = xs.shape[1]
    nblk = p // MOE_BLOCK
    idx = jnp.arange(nblk, dtype=jnp.int32)
    starts = (idx < n_used[0]) & ((idx == 0) | (blk_expert != jnp.roll(blk_expert, 1)))
    run = jnp.cumsum(starts.astype(jnp.int32)) - 1
    next_start = lax.cummin(jnp.where(starts, idx, nblk)[::-1])[::-1]
    after = jnp.concatenate([next_start[1:], jnp.full((1,), nblk, jnp.int32)])
    nxt = jnp.where(after < nblk, blk_expert[jnp.minimum(after, nblk - 1)], -1).astype(jnp.int32)
    live = lambda i, be, nu, nx, rn: jnp.minimum(i, nu[0] - 1)
    any_spec = pl.BlockSpec(memory_space=pl.ANY)
    return pl.pallas_call(
        _expert_kernel,
        grid_spec=pltpu.PrefetchScalarGridSpec(
            num_scalar_prefetch=4,
            grid=(nblk,),
            in_specs=[
                pl.BlockSpec((ROW_PIECES, MOE_BLOCK, PIECE), lambda i, be, nu, nx, rn: (0, live(i, be, nu, nx, rn), 0)),
                any_spec, any_spec, any_spec,
            ],
            out_specs=pl.BlockSpec((ROW_PIECES, MOE_BLOCK, PIECE), lambda i, be, nu, nx, rn: (0, i, 0)),
            scratch_shapes=[pltpu.VMEM((2, D_MODEL, EXPERT_FF), jnp.float32),
                            pltpu.VMEM((2, D_MODEL, EXPERT_FF), jnp.float32),
                            pltpu.VMEM((2, EXPERT_FF, D_MODEL), jnp.float32),
                            pltpu.VMEM((D_MODEL, EXPERT_FF), jnp.bfloat16),
                            pltpu.VMEM((D_MODEL, EXPERT_FF), jnp.bfloat16),
                            pltpu.VMEM((EXPERT_FF, D_MODEL), jnp.bfloat16),
                            pltpu.SemaphoreType.DMA((2, 3))],
        ),
        out_shape=jax.ShapeDtypeStruct((ROW_PIECES, p, PIECE), jnp.uint32),
        compiler_params=pltpu.CompilerParams(dimension_semantics=("arbitrary",),
                                             vmem_limit_bytes=VMEM_LIMIT),
        name="experts",
    )(blk_expert, n_used, nxt, run.astype(jnp.int32), xs, w_gate, w_up, w_down)


def _combine_kernel(x1_ref, route_ref, y0_ref, y1_ref, *rest):
    o_ref = rest[-1]
    route = route_ref[...]
    o_ref[...] = x1_ref[...] + route[:, 2:3] * _load_packed(y0_ref) + route[:, 3:4] * _load_packed(y1_ref)


def _combine(x1, yg, route, in_row0, out_row0, n_total, out_prev):
    n = yg.shape[1] // TOP_K
    tm = min(PROJ_ROWS, n)
    blk0 = out_row0 // tm
    in0 = in_row0 // tm
    prev = () if out_prev is None else (out_prev,)
    return pl.pallas_call(
        _combine_kernel,
        grid=(n // tm,),
        in_specs=[
            pl.BlockSpec((tm, D_MODEL), lambda i: (in0 + i, 0)),
            pl.BlockSpec((tm, LANES), lambda i: (in0 + i, 0)),
            pl.BlockSpec((ROW_PIECES, tm, PIECE), lambda i: (0, i, 0)),
            pl.BlockSpec((ROW_PIECES, tm, PIECE), lambda i: (0, i + n // tm, 0)),
        ] + [pl.BlockSpec(memory_space=pl.ANY)] * len(prev),
        out_specs=pl.BlockSpec((tm, D_MODEL), lambda i: (blk0 + i, 0)),
        out_shape=jax.ShapeDtypeStruct((n_total, D_MODEL), jnp.float32),
        input_output_aliases={4: 0} if prev else {},
        compiler_params=pltpu.CompilerParams(dimension_semantics=("arbitrary",),
                                             vmem_limit_bytes=VMEM_LIMIT),
        name="combine",
    )(x1, route, yg, yg, *prev)


def _dispatch_plan(route, counts, n):
    counts = counts[0, :N_EXPERTS].astype(jnp.int32)
    padded = ((counts + MOE_BLOCK - 1) // MOE_BLOCK) * MOE_BLOCK
    seg_end = jnp.cumsum(padded).astype(jnp.int32)
    seg_start = seg_end - padded
    cols = route[:, :8].T.astype(jnp.int32)
    e, rank = cols[0:TOP_K], cols[4:4 + TOP_K]
    picked = e[None] == jnp.arange(N_EXPERTS, dtype=jnp.int32)[:, None, None]
    dest = jnp.sum(jnp.where(picked, seg_start[:, None, None], 0), axis=0) + rank
    p = n * TOP_K + N_EXPERTS * MOE_BLOCK
    slot = dest[None] + (jnp.arange(ROW_PIECES, dtype=jnp.int32) * p)[:, None, None]
    blk_start = jnp.arange(p // MOE_BLOCK, dtype=jnp.int32) * MOE_BLOCK
    blk_expert = jnp.sum((seg_end[None, :] <= blk_start[:, None]).astype(jnp.int32), axis=1)
    blk_expert = jnp.minimum(blk_expert, N_EXPERTS - 1)
    n_used = (seg_end[-1] // MOE_BLOCK).reshape(1)
    return slot, blk_expert, n_used, p


def _layer_rows(x2, pos2, row0, batch, seq, out_prev, norm1_g, w_in, q_norm_g, k_norm_g, lam4, diff_subln_g,
                ret_gn_g, ret_gn_b, w_branch_a, w_branch_b, w_out, norm2_g, w_gr, b_gr, w_er, b_er,
                w_gate, w_up, w_down):
    n = batch * seq
    proj = _in_proj(x2, pos2, norm1_g, w_in, q_norm_g, k_norm_g, row0, n)
    oa = _diff_attn(proj, lam4, diff_subln_g.reshape(1, DA_VDIM), batch, seq)
    ob = _retention(proj, ret_gn_g, ret_gn_b, batch, seq)
    x1, h2, route, counts = _merge(x2, oa, ob, proj, w_branch_a, w_branch_b, w_out, norm2_g,
                                   w_gr, b_gr, w_er, b_er, row0)
    slot, blk_expert, n_used, p = _dispatch_plan(route, counts, n)
    win_n = n // SC_WINDOW
    src_block = lambda s: (s // (TOP_K * win_n)) * win_n + s % win_n
    xs = _sc_scatter_rows(h2.reshape(ROW_PIECES * n, PIECE), slot.reshape(1, -1), ROW_PIECES * p, src_block)
    ys = _experts(xs.reshape(ROW_PIECES, p, PIECE), blk_expert, n_used, w_gate, w_up, w_down)
    parts = COMBINE_PARTS if n % (COMBINE_PARTS * PROJ_ROWS) == 0 else 1
    m = n // parts
    out = out_prev
    for t in range(parts):
        yg = _sc_gather_rows(ys.reshape(ROW_PIECES * p, PIECE), slot[:, :, t * m:(t + 1) * m].reshape(1, -1))
        out = _combine(x1, yg.reshape(ROW_PIECES, TOP_K * m, PIECE), route, t * m, row0 + t * m,
                       x2.shape[0], out)
    return out


def _layer(x, positions, *weights):
    batch, seq, _ = x.shape
    n = batch * seq
    x2 = x.reshape(n, D_MODEL)
    pos2 = positions.reshape(1, n)
    groups = BATCH_GROUPS if batch % BATCH_GROUPS == 0 else 1
    per = batch // groups
    out = None
    for g in range(groups):
        out = _layer_rows(x2, pos2, g * per * seq, per, seq, out, *weights)
    return out.reshape(batch, seq, D_MODEL)


def kernel(x, positions, norm1_g, w_in, q_norm_g, k_norm_g, lambda_q1, lambda_k1, lambda_q2, lambda_k2, diff_subln_g, ret_gn_g, ret_gn_b, w_branch_a, w_branch_b, w_out, norm2_g, w_group_router, b_group_router, w_expert_router, b_expert_router, w_gate, w_up, w_down):
    assert x.shape[-1] == D_MODEL and norm1_g.shape[0] == 1, "single-layer, D_MODEL-wide input expected"
    lam4 = jnp.zeros((4, LANES), jnp.float32)
    lam4 = lam4.at[:, :DA_HALF].set(jnp.stack([lambda_q1[0], lambda_k1[0], lambda_q2[0], lambda_k2[0]]))
    return _layer(x, positions, norm1_g[0], w_in[0], q_norm_g[0], k_norm_g[0], lam4, diff_subln_g[0],
                  ret_gn_g[0], ret_gn_b[0], w_branch_a[0], w_branch_b[0], w_out[0], norm2_g[0],
                  w_group_router[0], b_group_router[0], w_expert_router[0], b_expert_router[0],
                  w_gate[0], w_up[0], w_down[0])
```

```python
import functools
import math

import jax
import jax.numpy as jnp
from jax import lax
from jax.experimental import pallas as pl
from jax.experimental.pallas import tpu as pltpu
from jax.experimental.pallas import tpu_sc as plsc

D_MODEL = 1024
DA_HEADS = 4
DA_HALF = 64
DA_VDIM = 2 * DA_HALF
DA_WIDTH = DA_HEADS * DA_VDIM
ROPE_THETA = 500000.0
ROPE_DIM = DA_HALF // 4
RET_HEADS = 4
RET_KDIM = 128
RET_VDIM = 128
RET_WIDTH = RET_HEADS * RET_VDIM
RET_THETA = 10000.0
N_GROUPS = 4
EXPERTS_PER_GROUP = 8
N_EXPERTS = N_GROUPS * EXPERTS_PER_GROUP
TOP_K = 2
EXPERT_FF = 512
EPS = 1e-6
LAMBDA_INIT = 0.8 - 0.6 * math.exp(-0.3 * 0)

LANES = 128
IN_COLS = 3 * DA_WIDTH + 4 * RET_WIDTH + 2 * D_MODEL
COL_QA, COL_KA, COL_VA = 0, DA_WIDTH, 2 * DA_WIDTH
COL_QR = 3 * DA_WIDTH
COL_KR = COL_QR + RET_WIDTH
COL_VR = COL_KR + RET_WIDTH
COL_GB = COL_VR + RET_WIDTH
COL_GATE_A = COL_GB + RET_WIDTH
COL_GATE_B = COL_GATE_A + D_MODEL

PROJ_ROWS = 512
PROJ_CHUNK = 256
ATT_TILE = 512
ATT_ROWS = 32
RET_CHUNK = 256
RET_UNROLL = 16
MOE_BLOCK = 512
PACKED = D_MODEL // 2
ROW_PIECES = 2
PIECE = PACKED // ROW_PIECES
SC_CORES = 2
SC_WINDOW = 128
COMBINE_PARTS = 2
BATCH_GROUPS = 1
VMEM_LIMIT = 56 * 1024 * 1024


def _dot(a, b):
    return jnp.dot(a, b, preferred_element_type=jnp.float32)


def _dot_nt(a, b):
    return lax.dot_general(a, b, (((1,), (1,)), ((), ())), preferred_element_type=jnp.float32)


def _dot_tn(a, b):
    return lax.dot_general(a, b, (((0,), (0,)), ((), ())), preferred_element_type=jnp.float32)


def _sigmoid(x):
    return 0.5 * jnp.tanh(0.5 * x) + 0.5


def _split3(x):
    a = x.astype(jnp.bfloat16)
    r = x - a.astype(jnp.float32)
    b = r.astype(jnp.bfloat16)
    c = (r - b.astype(jnp.float32)).astype(jnp.bfloat16)
    return a, b, c


def _in_proj_kernel(x_ref, pos_ref, g1_ref, w_ref, gsum_ref, gq_ref, gk_ref, fa_ref, fr_ref, sel_ref,
                    o_ref, h_scr):
    x = x_ref[...]
    rows = x.shape[0]
    h_scr[...] = (x * g1_ref[...]).astype(jnp.bfloat16)
    rms_scale = jnp.broadcast_to(lax.rsqrt(jnp.mean(x * x, axis=-1, keepdims=True) + EPS), (rows, PROJ_CHUNK))
    pos = pos_ref[...].astype(jnp.float32)

    lane = lax.broadcasted_iota(jnp.int32, (rows, LANES), 1)
    half_a = ROPE_DIM // 2
    tables = {}

    def da_tables():
        if "da" not in tables:
            ang_a = fa_ref[...] * pos
            pad = jnp.zeros((LANES - 2 * half_a, rows), jnp.float32)
            t_a = jnp.concatenate([jnp.cos(ang_a), jnp.sin(ang_a), pad], axis=0).T
            tab = sum(_dot(part, sel_ref[...]) for part in _split3(t_a))
            c_a = tab[:, :LANES] + jnp.where(lane % DA_HALF < ROPE_DIM, 0.0, 1.0)
            s_lo = tab[:, LANES:2 * LANES]
            s_hi = tab[:, 2 * LANES:]
            tables["da"] = tuple(jnp.concatenate([v, v], axis=1) for v in (c_a, s_lo, s_hi))
        return tables["da"]

    def ret_tables():
        if "ret" not in tables:
            ang_r = fr_ref[...] * pos
            t_r = jnp.concatenate([jnp.cos(ang_r), jnp.sin(ang_r)], axis=0).T
            sw_r = pltpu.roll(t_r, RET_KDIM // 2, axis=1)
            first = lane < RET_KDIM // 2
            c_r = jnp.where(first, t_r, sw_r)
            s_r = jnp.where(first, -sw_r, t_r)
            tables["ret"] = tuple(jnp.concatenate([v, v], axis=1) for v in (c_r, s_r))
        return tables["ret"]

    def qk_norm_rope(y, g, scale):
        c_a2, s_lo2, s_hi2 = da_tables()
        ss = y * y
        hi = ss.astype(jnp.bfloat16)
        lo = (ss - hi.astype(jnp.float32)).astype(jnp.bfloat16)
        gs = _dot(hi, gsum_ref[...]) + _dot(lo, gsum_ref[...])
        n = y * lax.rsqrt(gs * (1.0 / DA_HALF) + EPS) * g
        up = pltpu.roll(n, PROJ_CHUNK - half_a, axis=1)
        dn = pltpu.roll(n, half_a, axis=1)
        r = n * c_a2 + up * s_lo2 + dn * s_hi2
        return r * scale if scale != 1.0 else r

    def ret_rope(y, scale):
        c_r2, s_r2 = ret_tables()
        halves = [pltpu.roll(y[:, i * LANES:(i + 1) * LANES], RET_KDIM // 2, axis=1)
                  for i in range(PROJ_CHUNK // LANES)]
        sw = jnp.concatenate(halves, axis=1)
        r = y * c_r2 + sw * s_r2
        return r * scale if scale != 1.0 else r

    n_chunks = IN_COLS // PROJ_CHUNK
    is_long = lambda c: c * PROJ_CHUNK < COL_VA or COL_QR <= c * PROJ_CHUNK < COL_VR
    long_chunks = [c for c in range(n_chunks) if is_long(c)]
    short_chunks = [c for c in range(n_chunks) if not is_long(c)][::-1]
    order = [short_chunks.pop(0) for _ in range(3)]
    while long_chunks or short_chunks:
        if long_chunks:
            order.append(long_chunks.pop(0))
        if short_chunks:
            order.append(short_chunks.pop(0))
    for c in order:
        c0 = c * PROJ_CHUNK
        y = _dot(h_scr[...], w_ref[:, c0:c0 + PROJ_CHUNK]) * rms_scale
        if c0 < COL_KA:
            y = qk_norm_rope(y, gq_ref[...], DA_HALF ** -0.5)
        elif c0 < COL_VA:
            y = qk_norm_rope(y, gk_ref[...], 1.0)
        elif c0 < COL_QR:
            pass
        elif c0 < COL_KR:
            y = ret_rope(y, 1.0)
        elif c0 < COL_VR:
            y = ret_rope(y, RET_KDIM ** -0.5)
        elif c0 < COL_GB:
            pass
        elif c0 < COL_GATE_A:
            y = y * _sigmoid(y)
        else:
            y = _sigmoid(y)
        o_ref[:, c0:c0 + PROJ_CHUNK] = y.astype(o_ref.dtype)


def _in_proj(x2, pos2, g1, w_in, gq, gk, row0, n):
    tm = min(PROJ_ROWS, n)
    blk0 = row0 // tm
    grp = jnp.arange(PROJ_CHUNK) // DA_HALF
    gsum = (grp[:, None] == grp[None, :]).astype(jnp.bfloat16)
    half_a = ROPE_DIM // 2
    fa = jnp.power(jnp.float32(ROPE_THETA), -2.0 * jnp.arange(half_a, dtype=jnp.float32) / ROPE_DIM)[:, None]
    half_r = RET_KDIM // 2
    fr = jnp.power(jnp.float32(RET_THETA), -2.0 * jnp.arange(half_r, dtype=jnp.float32) / RET_KDIM)[:, None]
    j = jnp.arange(LANES)[:, None]
    l64 = (jnp.arange(LANES) % DA_HALF)[None, :]
    sel_c = (j < half_a) & (l64 < ROPE_DIM) & (l64 % half_a == j)
    sel_lo = (j >= half_a) & (j < ROPE_DIM) & (l64 < half_a) & (l64 == j - half_a)
    sel_hi = (j >= half_a) & (j < ROPE_DIM) & (l64 >= half_a) & (l64 < ROPE_DIM) & (l64 == j)
    sel = jnp.concatenate([sel_c.astype(jnp.float32), -sel_lo.astype(jnp.float32),
                           sel_hi.astype(jnp.float32)], axis=1).astype(jnp.bfloat16)
    reps = PROJ_CHUNK // DA_HALF
    full = lambda shape: pl.BlockSpec(shape, lambda i: (0,) * len(shape))
    return pl.pallas_call(
        _in_proj_kernel,
        grid=(n // tm,),
        in_specs=[
            pl.BlockSpec((tm, D_MODEL), lambda i: (blk0 + i, 0)),
            pl.BlockSpec((1, tm), lambda i: (0, blk0 + i)),
            full((1, D_MODEL)),
            full((D_MODEL, IN_COLS)),
            full((PROJ_CHUNK, PROJ_CHUNK)),
            full((1, PROJ_CHUNK)),
            full((1, PROJ_CHUNK)),
            full((half_a, 1)),
            full((half_r, 1)),
            full((LANES, 3 * LANES)),
        ],
        out_specs=pl.BlockSpec((tm, IN_COLS), lambda i: (i, 0)),
        out_shape=jax.ShapeDtypeStruct((n, IN_COLS), jnp.bfloat16),
        scratch_shapes=[pltpu.VMEM((tm, D_MODEL), jnp.bfloat16)],
        compiler_params=pltpu.CompilerParams(dimension_semantics=("arbitrary",),
                                             vmem_limit_bytes=VMEM_LIMIT),
        name="in_proj",
    )(x2, pos2, g1.reshape(1, D_MODEL), w_in.astype(jnp.bfloat16), gsum,
      jnp.tile(gq, reps)[None, :], jnp.tile(gk, reps)[None, :], fa, fr, sel)


def _diff_attn_kernel(q_ref, k_ref, v_ref, lam_ref, gsub_ref, o_ref,
                      qs_scr, vx_scr, s0_scr, s1_scr, p_scr, m_scr, alpha_scr, acc_scr):
    i = pl.program_id(2)
    t = q_ref.shape[0]

    @pl.when(i == 0)
    def _():
        vx_scr[:, :DA_VDIM] = v_ref[...]
        vx_scr[:, DA_VDIM:] = jnp.ones((vx_scr.shape[0], LANES), vx_scr.dtype)

    q = q_ref[...]
    lane = lax.broadcasted_iota(jnp.int32, q.shape, 1)
    zero = jnp.zeros_like(q)
    qs_scr[:t] = jnp.where(lane < DA_HALF, q, zero)
    qs_scr[t:] = jnp.where(lane >= DA_HALF, q, zero)
    m_scr[...] = jnp.full(m_scr.shape, -jnp.inf, jnp.float32)
    acc_scr[...] = jnp.zeros(acc_scr.shape, jnp.float32)

    def scores(j, s_ref):
        start = pl.multiple_of(j * t, t)
        s_ref[...] = _dot_nt(qs_scr[...], k_ref[pl.ds(start, t), :])

    def softmax_pv(j, s_ref, masked):
        for c in range(2 * t // ATT_ROWS):
            rows = pl.ds(c * ATT_ROWS, ATT_ROWS)
            s = s_ref[rows, :]
            if masked:
                r = lax.broadcasted_iota(jnp.int32, s.shape, 0) + (c * ATT_ROWS) % t
                col = lax.broadcasted_iota(jnp.int32, s.shape, 1)
                s = jnp.where(col <= r, s, -jnp.inf)
            m_prev = m_scr[rows, :]
            m_new = jnp.maximum(m_prev, jnp.max(s, axis=-1, keepdims=True))
            alpha_scr[rows, :] = jnp.exp(m_prev - m_new)
            m_scr[rows, :] = m_new
            p = jnp.exp(s - jnp.concatenate([m_new] * (t // LANES), axis=1))
            p_scr[rows, :] = p.astype(p_scr.dtype)
        start = pl.multiple_of(j * t, t)
        pv = _dot(p_scr[...], vx_scr[pl.ds(start, t), :])
        alpha = alpha_scr[...]
        for half in range(2):
            cols = pl.ds(half * LANES, LANES)
            acc_scr[:, cols] = alpha * acc_scr[:, cols] + pv[:, half * LANES:(half + 1) * LANES]

    scores(0, s0_scr)

    def pair(jj, carry):
        j = 2 * jj
        scores(j + 1, s1_scr)
        softmax_pv(j, s0_scr, False)
        scores(j + 2, s0_scr)
        softmax_pv(j + 1, s1_scr, False)
        return carry

    lax.fori_loop(0, i // 2, pair, 0)

    @pl.when(i % 2 == 1)
    def _():
        scores(i, s1_scr)
        softmax_pv(i - 1, s0_scr, False)
        softmax_pv(i, s1_scr, True)

    @pl.when(i % 2 == 0)
    def _():
        softmax_pv(i, s0_scr, True)

    lam4 = lam_ref[...]
    lam = (jnp.exp(jnp.sum(lam4[0:1] * lam4[1:2], axis=-1, keepdims=True))
           - jnp.exp(jnp.sum(lam4[2:3] * lam4[3:4], axis=-1, keepdims=True)) + LAMBDA_INIT)
    o_all = acc_scr[:, :DA_VDIM] / acc_scr[:, DA_VDIM:]
    o = o_all[:t] - lam * o_all[t:]
    o = o * lax.rsqrt(jnp.mean(o * o, axis=-1, keepdims=True) + EPS) * gsub_ref[...] * (1.0 - LAMBDA_INIT)
    o_ref[...] = o.astype(o_ref.dtype)


def _diff_attn(proj, lam4, gsub, batch, seq):
    n = proj.shape[0]
    t = min(ATT_TILE, seq)
    nq = seq // t
    qb, kb, vb = COL_QA // LANES, COL_KA // LANES, COL_VA // LANES
    return pl.pallas_call(
        _diff_attn_kernel,
        grid=(batch, DA_HEADS, nq),
        in_specs=[
            pl.BlockSpec((t, LANES), lambda b, h, i: (b * nq + i, qb + h)),
            pl.BlockSpec((seq, LANES), lambda b, h, i: (b, kb + h)),
            pl.BlockSpec((seq, LANES), lambda b, h, i: (b, vb + h)),
            pl.BlockSpec((4, LANES), lambda b, h, i: (0, 0)),
            pl.BlockSpec((1, LANES), lambda b, h, i: (0, 0)),
        ],
        out_specs=pl.BlockSpec((t, LANES), lambda b, h, i: (b * nq + i, h)),
        out_shape=jax.ShapeDtypeStruct((n, DA_WIDTH), jnp.bfloat16),
        scratch_shapes=[pltpu.VMEM((2 * t, LANES), jnp.bfloat16),
                        pltpu.VMEM((seq, DA_VDIM + LANES), jnp.bfloat16),
                        pltpu.VMEM((2 * t, t), jnp.float32),
                        pltpu.VMEM((2 * t, t), jnp.float32),
                        pltpu.VMEM((2 * t, t), jnp.bfloat16),
                        pltpu.VMEM((2 * t, LANES), jnp.float32),
                        pltpu.VMEM((2 * t, LANES), jnp.float32),
                        pltpu.VMEM((2 * t, DA_VDIM + LANES), jnp.float32)],
        compiler_params=pltpu.CompilerParams(dimension_semantics=("arbitrary",) * 3,
                                             vmem_limit_bytes=VMEM_LIMIT),
        name="diff_attn",
    )(proj, proj, proj, lam4, gsub)


def _retention_kernel(q_ref, k_ref, v_ref, g_ref, gng_ref, gnb_ref, o_ref, r_scr, *, chunk):
    hf = jnp.full((1, 1), pl.program_id(1), jnp.int32).astype(jnp.float32)
    log_g = jnp.log1p(-jnp.exp2(-5.0 - hf))
    ri = lax.broadcasted_iota(jnp.int32, (chunk, chunk), 0)
    ci = lax.broadcasted_iota(jnp.int32, (chunk, chunk), 1)
    rel = (ri - ci).astype(jnp.float32)
    dmask = jnp.where(rel >= 0, jnp.exp(jnp.maximum(rel, 0.0) * log_g), 0.0)
    idx = lax.broadcasted_iota(jnp.int32, (chunk, 1), 0).astype(jnp.float32)
    zeta = jnp.exp((chunk - 1 - idx) * log_g)
    xi = jnp.exp((idx + 1.0) * log_g)
    g_chunk = jnp.exp(chunk * log_g)
    r_scr[...] = jnp.zeros(r_scr.shape, jnp.float32)
    gng = gng_ref[...]
    gnb = gnb_ref[...]

    def body(n, carry):
        start = pl.multiple_of(n * chunk, chunk)
        q = q_ref[pl.ds(start, chunk), :]
        k = k_ref[pl.ds(start, chunk), :]
        v = v_ref[pl.ds(start, chunk), :]
        s = _dot_nt(q, k) * dmask
        r_old = r_scr[...]
        o = _dot(s.astype(jnp.bfloat16), v) + xi * _dot(q, r_old.astype(jnp.bfloat16))
        kz = (k.astype(jnp.float32) * zeta).astype(jnp.bfloat16)
        r_scr[...] = g_chunk * r_old + _dot_tn(kz, v)
        mu = jnp.mean(o, axis=-1, keepdims=True)
        d = o - mu
        var = jnp.mean(d * d, axis=-1, keepdims=True)
        y = d * lax.rsqrt(var + EPS) * gng + gnb
        y = y * g_ref[pl.ds(start, chunk), :].astype(jnp.float32)
        o_ref[pl.ds(start, chunk), :] = y.astype(o_ref.dtype)
        return carry

    lax.fori_loop(0, q_ref.shape[0] // chunk, body, 0, unroll=RET_UNROLL)


def _retention(proj, gn_g, gn_b, batch, seq):
    n = proj.shape[0]
    chunk = min(RET_CHUNK, seq)
    col = lambda c0: (lambda b, h: (b, c0 // LANES + h))
    return pl.pallas_call(
        functools.partial(_retention_kernel, chunk=chunk),
        grid=(batch, RET_HEADS),
        in_specs=[
            pl.BlockSpec((seq, LANES), col(COL_QR)),
            pl.BlockSpec((seq, LANES), col(COL_KR)),
            pl.BlockSpec((seq, LANES), col(COL_VR)),
            pl.BlockSpec((seq, LANES), col(COL_GB)),
            pl.BlockSpec((1, LANES), lambda b, h: (0, h)),
            pl.BlockSpec((1, LANES), lambda b, h: (0, h)),
        ],
        out_specs=pl.BlockSpec((seq, LANES), lambda b, h: (b, h)),
        out_shape=jax.ShapeDtypeStruct((n, RET_WIDTH), jnp.bfloat16),
        scratch_shapes=[pltpu.VMEM((RET_KDIM, RET_VDIM), jnp.float32)],
        compiler_params=pltpu.CompilerParams(dimension_semantics=("arbitrary",) * 2,
                                             vmem_limit_bytes=VMEM_LIMIT),
        name="retention",
    )(proj, proj, proj, proj, gn_g.reshape(1, RET_WIDTH), gn_b.reshape(1, RET_WIDTH))


def _merge_kernel(x_ref, oa_ref, ob_ref, sa0_ref, sa1_ref, sb0_ref, sb1_ref, wa_ref, wb_ref, wo_ref,
                  g2_ref, wr_hi_ref, wr_lo_ref, br_ref, tri_ref, x1_ref, h2_ref, route_ref, counts_ref,
                  base_scr, logits_scr):
    i = pl.program_id(0)

    @pl.when(i == 0)
    def _():
        base_scr[...] = jnp.zeros(base_scr.shape, jnp.float32)
        logits_scr[...] = jnp.zeros(logits_scr.shape, jnp.float32)

    ya = _dot(oa_ref[...], wa_ref[...])
    yb = _dot(ob_ref[...], wb_ref[...])

    logits = logits_scr[...]
    lane = lax.broadcasted_iota(jnp.int32, logits.shape, 1)
    neg = -jnp.inf
    gl = jnp.where(lane < N_GROUPS, logits, neg)
    gmax = jnp.max(gl, axis=-1, keepdims=True)
    g_idx = jnp.min(jnp.where(gl == gmax, lane, LANES), axis=-1, keepdims=True)
    p_g = 1.0 / jnp.sum(jnp.exp(gl - gmax), axis=-1, keepdims=True)
    e_lo = N_GROUPS + EXPERTS_PER_GROUP * g_idx
    el = jnp.where((lane >= e_lo) & (lane < e_lo + EXPERTS_PER_GROUP), logits, neg)
    v1 = jnp.max(el, axis=-1, keepdims=True)
    i1 = jnp.min(jnp.where(el == v1, lane, LANES), axis=-1, keepdims=True)
    el2 = jnp.where(lane == i1, neg, el)
    v2 = jnp.max(el2, axis=-1, keepdims=True)
    i2 = jnp.min(jnp.where(el2 == v2, lane, LANES), axis=-1, keepdims=True)
    t = jnp.exp(v2 - v1)
    w1 = p_g / (1.0 + t)
    w2 = p_g * t / (1.0 + t)
    e1 = i1 - N_GROUPS
    e2 = i2 - N_GROUPS

    oh1 = lane == e1
    oh2 = lane == e2
    real = jnp.where(i > 0, 1.0, 0.0)
    picked = jnp.where(oh1 | oh2, real, 0.0)
    before = _dot(tri_ref[...], picked.astype(jnp.bfloat16)) + base_scr[0:1, :]
    rank1 = jnp.sum(jnp.where(oh1, before, 0.0), axis=-1, keepdims=True)
    rank2 = jnp.sum(jnp.where(oh2, before, 0.0), axis=-1, keepdims=True)
    base_scr[...] = base_scr[...] + jnp.sum(picked, axis=0, keepdims=True)
    counts_ref[...] = base_scr[...]

    cols = [e1.astype(jnp.float32), e2.astype(jnp.float32), w1, w2, rank1, rank2]
    route = jnp.zeros(logits.shape, jnp.float32)
    for c, val in enumerate(cols):
        route = jnp.where(lane == c, val, route)
    route_ref[...] = route

    sa = jnp.concatenate([sa0_ref[...], sa1_ref[...]], axis=1).astype(jnp.float32)
    sb = jnp.concatenate([sb0_ref[...], sb1_ref[...]], axis=1).astype(jnp.float32)
    merged = sa * ya + sb * yb
    x1 = x_ref[...] + _dot(merged.astype(jnp.bfloat16), wo_ref[...])
    x1_ref[...] = x1
    h2 = x1 * lax.rsqrt(jnp.mean(x1 * x1, axis=-1, keepdims=True) + EPS) * g2_ref[...]
    _store_packed(h2_ref, h2)

    hi = h2.astype(jnp.bfloat16)
    lo = (h2 - hi.astype(jnp.float32)).astype(jnp.bfloat16)
    logits_scr[...] = (_dot(hi, wr_hi_ref[...]) + _dot(lo, wr_hi_ref[...]) + _dot(hi, wr_lo_ref[...])
                       + br_ref[...])


def _merge(x2, oa, ob, proj, wa, wb, wo, g2, w_gr, b_gr, w_er, b_er, row0):
    n = oa.shape[0]
    tm = min(PROJ_ROWS, n)
    blk0 = row0 // tm
    half = D_MODEL // 2
    wr = jnp.zeros((D_MODEL, LANES), jnp.float32)
    wr = wr.at[:, :N_GROUPS].set(w_gr).at[:, N_GROUPS:N_GROUPS + N_EXPERTS].set(w_er)
    wr_hi = wr.astype(jnp.bfloat16)
    wr_lo = (wr - wr_hi.astype(jnp.float32)).astype(jnp.bfloat16)
    br = jnp.zeros((1, LANES), jnp.float32)
    br = br.at[0, :N_GROUPS].set(b_gr).at[0, N_GROUPS:N_GROUPS + N_EXPERTS].set(b_er)
    tri = (jnp.arange(tm)[:, None] > jnp.arange(tm)[None, :]).astype(jnp.bfloat16)
    full = lambda shape: pl.BlockSpec(shape, lambda i: (0,) * len(shape))
    nt = n // tm
    cur = lambda i: jnp.minimum(i, nt - 1)
    gate = lambda c0: pl.BlockSpec((tm, half), lambda i: (cur(i), c0 // half))
    return pl.pallas_call(
        _merge_kernel,
        grid=(nt + 1,),
        in_specs=[
            pl.BlockSpec((tm, D_MODEL), lambda i: (blk0 + cur(i), 0)),
            pl.BlockSpec((tm, DA_WIDTH), lambda i: (cur(i), 0)),
            pl.BlockSpec((tm, RET_WIDTH), lambda i: (cur(i), 0)),
            gate(COL_GATE_A), gate(COL_GATE_A + half), gate(COL_GATE_B), gate(COL_GATE_B + half),
            full((DA_WIDTH, D_MODEL)), full((RET_WIDTH, D_MODEL)), full((D_MODEL, D_MODEL)),
            full((1, D_MODEL)), full((D_MODEL, LANES)), full((D_MODEL, LANES)), full((1, LANES)),
            full((tm, tm)),
        ],
        out_specs=[
            pl.BlockSpec((tm, D_MODEL), lambda i: (cur(i), 0)),
            pl.BlockSpec((ROW_PIECES, tm, PIECE), lambda i: (0, cur(i), 0)),
            pl.BlockSpec((tm, LANES), lambda i: (jnp.maximum(i - 1, 0), 0)),
            pl.BlockSpec((8, LANES), lambda i: (0, 0)),
        ],
        out_shape=[
            jax.ShapeDtypeStruct((n, D_MODEL), jnp.float32),
            jax.ShapeDtypeStruct((ROW_PIECES, n, PIECE), jnp.uint32),
            jax.ShapeDtypeStruct((n, LANES), jnp.float32),
            jax.ShapeDtypeStruct((8, LANES), jnp.float32),
        ],
        scratch_shapes=[pltpu.VMEM((8, LANES), jnp.float32), pltpu.VMEM((tm, LANES), jnp.float32)],
        compiler_params=pltpu.CompilerParams(dimension_semantics=("arbitrary",),
                                             vmem_limit_bytes=VMEM_LIMIT),
        name="merge",
    )(x2, oa, ob, proj, proj, proj, proj, wa.astype(jnp.bfloat16), wb.astype(jnp.bfloat16),
      wo.astype(jnp.bfloat16), g2.reshape(1, D_MODEL), wr_hi, wr_lo, br, tri)


def _sc_mesh():
    return plsc.VectorSubcoreMesh(core_axis_name="c", subcore_axis_name="s")


def _sc_scatter_rows(src, idx, out_rows, src_block):
    steps = idx.shape[1] // SC_WINDOW
    per_core = steps // SC_CORES

    @pl.kernel(out_type=jax.ShapeDtypeStruct((out_rows, PIECE), src.dtype), mesh=_sc_mesh())
    def scatter(src_hbm, idx_hbm, out_hbm):
        def body(src_vmem, idx_vmem):
            pltpu.sync_copy(src_vmem, out_hbm.at[idx_vmem.at[0]])

        pltpu.emit_pipeline(
            body,
            grid=(SC_CORES, per_core),
            in_specs=[pl.BlockSpec((SC_WINDOW, PIECE), lambda c, i: (src_block(c * per_core + i), 0)),
                      pl.BlockSpec((1, SC_WINDOW), lambda c, i: (0, c * per_core + i))],
            out_specs=[],
            core_axis_name=("c", "s"),
            dimension_semantics=(pltpu.PARALLEL, pltpu.PARALLEL),
        )(src_hbm, idx_hbm)

    return scatter(src, idx)


def _sc_gather_rows(table, idx):
    num = idx.shape[1]
    per_core = num // SC_WINDOW // SC_CORES

    @pl.kernel(out_type=jax.ShapeDtypeStruct((num, PIECE), table.dtype), mesh=_sc_mesh())
    def gather(table_hbm, idx_hbm, out_hbm):
        def body(idx_vmem, out_vmem):
            pltpu.sync_copy(table_hbm.at[idx_vmem.at[0]], out_vmem)

        pltpu.emit_pipeline(
            body,
            grid=(SC_CORES, per_core),
            in_specs=[pl.BlockSpec((1, SC_WINDOW), lambda c, i: (0, c * per_core + i))],
            out_specs=[pl.BlockSpec((SC_WINDOW, PIECE), lambda c, i: (c * per_core + i, 0))],
            core_axis_name=("c", "s"),
            dimension_semantics=(pltpu.PARALLEL, pltpu.PARALLEL),
        )(idx_hbm, out_hbm)

    return gather(table, idx)


def _store_packed(ref, val):
    as_bits = lambda v: lax.bitcast_convert_type(v.astype(jnp.bfloat16).astype(jnp.float32), jnp.uint32)
    words = (as_bits(val[:, :PACKED]) >> 16) | (as_bits(val[:, PACKED:]) & jnp.uint32(0xFFFF0000))
    for j in range(ROW_PIECES):
        ref[j] = words[:, j * PIECE:(j + 1) * PIECE]


def _load_packed(ref):
    words = jnp.concatenate([ref[j] for j in range(ROW_PIECES)], axis=1)
    low = lax.bitcast_convert_type(words << 16, jnp.float32)
    high = lax.bitcast_convert_type(words & jnp.uint32(0xFFFF0000), jnp.float32)
    return jnp.concatenate([low, high], axis=1)


def _expert_kernel(blk_e_ref, n_used_ref, nxt_ref, run_ref, x_ref, wg_hbm, wu_hbm, wd_hbm, o_ref,
                   wg_stage, wu_stage, wd_stage, wg_scr, wu_scr, wd_scr, sem):
    i = pl.program_id(0)
    used = i < n_used_ref[0]

    def weight_copies(e, s):
        return (pltpu.make_async_copy(wg_hbm.at[e], wg_stage.at[s], sem.at[s, 0]),
                pltpu.make_async_copy(wu_hbm.at[e], wu_stage.at[s], sem.at[s, 1]),
                pltpu.make_async_copy(wd_hbm.at[e], wd_stage.at[s], sem.at[s, 2]))

    @pl.when(i == 0)
    def _():
        for c in weight_copies(blk_e_ref[0], 0):
            c.start()

    @pl.when(used & ((i == 0) | (blk_e_ref[i] != blk_e_ref[jnp.maximum(i - 1, 0)])))
    def _():
        s = run_ref[i] % 2
        for c in weight_copies(blk_e_ref[i], s):
            c.wait()
        wg_scr[...] = wg_stage[s].astype(jnp.bfloat16)
        wu_scr[...] = wu_stage[s].astype(jnp.bfloat16)
        wd_scr[...] = wd_stage[s].astype(jnp.bfloat16)

        @pl.when(nxt_ref[i] >= 0)
        def _():
            for c in weight_copies(nxt_ref[i], 1 - s):
                c.start()

    @pl.when(used)
    def _():
        x = _load_packed(x_ref).astype(jnp.bfloat16)
        a = _dot(x, wg_scr[...])
        u = _dot(x, wu_scr[...])
        hmid = (a * _sigmoid(a) * u).astype(jnp.bfloat16)
        _store_packed(o_ref, _dot(hmid, wd_scr[...]))

    @pl.when(jnp.logical_not(used))
    def _():
        o_ref[...] = jnp.zeros(o_ref.shape, o_ref.dtype)


def _experts(xs, blk_expert, n_used, w_gate, w_up, w_down):
    p = xs.shape[1]
    nblk = p // MOE_BLOCK
    idx = jnp.arange(nblk, dtype=jnp.int32)
    starts = (idx < n_used[0]) & ((idx == 0) | (blk_expert != jnp.roll(blk_expert, 1)))
    run = jnp.cumsum(starts.astype(jnp.int32)) - 1
    next_start = lax.cummin(jnp.where(starts, idx, nblk)[::-1])[::-1]
    after = jnp.concatenate([next_start[1:], jnp.full((1,), nblk, jnp.int32)])
    nxt = jnp.where(after < nblk, blk_expert[jnp.minimum(after, nblk - 1)], -1).astype(jnp.int32)
    live = lambda i, be, nu, nx, rn: jnp.minimum(i, nu[0] - 1)
    any_spec = pl.BlockSpec(memory_space=pl.ANY)
    return pl.pallas_call(
        _expert_kernel,
        grid_spec=pltpu.PrefetchScalarGridSpec(
            num_scalar_prefetch=4,
            grid=(nblk,),
            in_specs=[
                pl.BlockSpec((ROW_PIECES, MOE_BLOCK, PIECE), lambda i, be, nu, nx, rn: (0, live(i, be, nu, nx, rn), 0)),
                any_spec, any_spec, any_spec,
            ],
            out_specs=pl.BlockSpec((ROW_PIECES, MOE_BLOCK, PIECE), lambda i, be, nu, nx, rn: (0, i, 0)),
            scratch_shapes=[pltpu.VMEM((2, D_MODEL, EXPERT_FF), jnp.float32),
                            pltpu.VMEM((2, D_MODEL, EXPERT_FF), jnp.float32),
                            pltpu.VMEM((2, EXPERT_FF, D_MODEL), jnp.float32),
                            pltpu.VMEM((D_MODEL, EXPERT_FF), jnp.bfloat16),
                            pltpu.VMEM((D_MODEL, EXPERT_FF), jnp.bfloat16),
                            pltpu.VMEM((EXPERT_FF, D_MODEL), jnp.bfloat16),
                            pltpu.SemaphoreType.DMA((2, 3))],
        ),
        out_shape=jax.ShapeDtypeStruct((ROW_PIECES, p, PIECE), jnp.uint32),
        compiler_params=pltpu.CompilerParams(dimension_semantics=("arbitrary",),
                                             vmem_limit_bytes=VMEM_LIMIT),
        name="experts",
    )(blk_expert, n_used, nxt, run.astype(jnp.int32), xs, w_gate, w_up, w_down)


def _combine_kernel(x1_ref, route_ref, y0_ref, y1_ref, *rest):
    o_ref = rest[-1]
    route = route_ref[...]
    o_ref[...] = x1_ref[...] + route[:, 2:3] * _load_packed(y0_ref) + route[:, 3:4] * _load_packed(y1_ref)


def _combine(x1, yg, route, in_row0, out_row0, n_total, out_prev):
    n = yg.shape[1] // TOP_K
    tm = min(PROJ_ROWS, n)
    blk0 = out_row0 // tm
    in0 = in_row0 // tm
    prev = () if out_prev is None else (out_prev,)
    return pl.pallas_call(
        _combine_kernel,
        grid=(n // tm,),
        in_specs=[
            pl.BlockSpec((tm, D_MODEL), lambda i: (in0 + i, 0)),
            pl.BlockSpec((tm, LANES), lambda i: (in0 + i, 0)),
            pl.BlockSpec((ROW_PIECES, tm, PIECE), lambda i: (0, i, 0)),
            pl.BlockSpec((ROW_PIECES, tm, PIECE), lambda i: (0, i + n // tm, 0)),
        ] + [pl.BlockSpec(memory_space=pl.ANY)] * len(prev),
        out_specs=pl.BlockSpec((tm, D_MODEL), lambda i: (blk0 + i, 0)),
        out_shape=jax.ShapeDtypeStruct((n_total, D_MODEL), jnp.float32),
        input_output_aliases={4: 0} if prev else {},
        compiler_params=pltpu.CompilerParams(dimension_semantics=("arbitrary",),
                                             vmem_limit_bytes=VMEM_LIMIT),
        name="combine",
    )(x1, route, yg, yg, *prev)


def _dispatch_plan(route, counts, n):
    counts = counts[0, :N_EXPERTS].astype(jnp.int32)
    padded = ((counts + MOE_BLOCK - 1) // MOE_BLOCK) * MOE_BLOCK
    seg_end = jnp.cumsum(padded).astype(jnp.int32)
    seg_start = seg_end - padded
    cols = route[:, :8].T.astype(jnp.int32)
    e, rank = cols[0:TOP_K], cols[4:4 + TOP_K]
    picked = e[None] == jnp.arange(N_EXPERTS, dtype=jnp.int32)[:, None, None]
    dest = jnp.sum(jnp.where(picked, seg_start[:, None, None], 0), axis=0) + rank
    p = n * TOP_K + N_EXPERTS * MOE_BLOCK
    slot = dest[None] + (jnp.arange(ROW_PIECES, dtype=jnp.int32) * p)[:, None, None]
    blk_start = jnp.arange(p // MOE_BLOCK, dtype=jnp.int32) * MOE_BLOCK
    blk_expert = jnp.sum((seg_end[None, :] <= blk_start[:, None]).astype(jnp.int32), axis=1)
    blk_expert = jnp.minimum(blk_expert, N_EXPERTS - 1)
    n_used = (seg_end[-1] // MOE_BLOCK).reshape(1)
    return slot, blk_expert, n_used, p


def _layer_rows(x2, pos2, row0, batch, seq, out_prev, norm1_g, w_in, q_norm_g, k_norm_g, lam4, diff_subln_g,
                ret_gn_g, ret_gn_b, w_branch_a, w_branch_b, w_out, norm2_g, w_gr, b_gr, w_er, b_er,
                w_gate, w_up, w_down):
    n = batch * seq
    proj = _in_proj(x2, pos2, norm1_g, w_in, q_norm_g, k_norm_g, row0, n)
    oa = _diff_attn(proj, lam4, diff_subln_g.reshape(1, DA_VDIM), batch, seq)
    ob = _retention(proj, ret_gn_g, ret_gn_b, batch, seq)
    x1, h2, route, counts = _merge(x2, oa, ob, proj, w_branch_a, w_branch_b, w_out, norm2_g,
                                   w_gr, b_gr, w_er, b_er, row0)
    slot, blk_expert, n_used, p = _dispatch_plan(route, counts, n)
    win_n = n // SC_WINDOW
    src_block = lambda s: (s // (TOP_K * win_n)) * win_n + s % win_n
    xs = _sc_scatter_rows(h2.reshape(ROW_PIECES * n, PIECE), slot.reshape(1, -1), ROW_PIECES * p, src_block)
    ys = _experts(xs.reshape(ROW_PIECES, p, PIECE), blk_expert, n_used, w_gate, w_up, w_down)
    parts = COMBINE_PARTS if n % (COMBINE_PARTS * PROJ_ROWS) == 0 else 1
    m = n // parts
    out = out_prev
    for t in range(parts):
        yg = _sc_gather_rows(ys.reshape(ROW_PIECES * p, PIECE), slot[:, :, t * m:(t + 1) * m].reshape(1, -1))
        out = _combine(x1, yg.reshape(ROW_PIECES, TOP_K * m, PIECE), route, t * m, row0 + t * m,
                       x2.shape[0], out)
    return out


def _layer(x, positions, *weights):
    batch, seq, _ = x.shape
    n = batch * seq
    x2 = x.reshape(n, D_MODEL)
    pos2 = positions.reshape(1, n)
    groups = BATCH_GROUPS if batch % BATCH_GROUPS == 0 else 1
    per = batch // groups
    out = None
    for g in range(groups):
        out = _layer_rows(x2, pos2, g * per * seq, per, seq, out, *weights)
    return out.reshape(batch, seq, D_MODEL)


def kernel(x, positions, norm1_g, w_in, q_norm_g, k_norm_g, lambda_q1, lambda_k1, lambda_q2, lambda_k2, diff_subln_g, ret_gn_g, ret_gn_b, w_branch_a, w_branch_b, w_out, norm2_g, w_group_router, b_group_router, w_expert_router, b_expert_router, w_gate, w_up, w_down):
    assert x.shape[-1] == D_MODEL and norm1_g.shape[0] == 1, "single-layer, D_MODEL-wide input expected"
    lam4 = jnp.zeros((4, LANES), jnp.float32)
    lam4 = lam4.at[:, :DA_HALF].set(jnp.stack([lambda_q1[0], lambda_k1[0], lambda_q2[0], lambda_k2[0]]))
    return _layer(x, positions, norm1_g[0], w_in[0], q_norm_g[0], k_norm_g[0], lam4, diff_subln_g[0],
                  ret_gn_g[0], ret_gn_b[0], w_branch_a[0], w_branch_b[0], w_out[0], norm2_g[0],
                  w_group_router[0], b_group_router[0], w_expert_router[0], b_expert_router[0],
                  w_gate[0], w_up[0], w_down[0])
```

```python
import functools
import math

import jax
import jax.numpy as jnp
from jax import lax
from jax.experimental import pallas as pl
from jax.experimental.pallas import tpu as pltpu
from jax.experimental.pallas import tpu_sc as plsc

D_MODEL = 1024
DA_HEADS = 4
DA_HALF = 64
DA_VDIM = 2 * DA_HALF
DA_WIDTH = DA_HEADS * DA_VDIM
ROPE_THETA = 500000.0
ROPE_DIM = DA_HALF // 4
RET_HEADS = 4
RET_KDIM = 128
RET_VDIM = 128
RET_WIDTH = RET_HEADS * RET_VDIM
RET_THETA = 10000.0
N_GROUPS = 4
EXPERTS_PER_GROUP = 8
N_EXPERTS = N_GROUPS * EXPERTS_PER_GROUP
TOP_K = 2
EXPERT_FF = 512
EPS = 1e-6
LAMBDA_INIT = 0.8 - 0.6 * math.exp(-0.3 * 0)

LANES = 128
IN_COLS = 3 * DA_WIDTH + 4 * RET_WIDTH + 2 * D_MODEL
COL_QA, COL_KA, COL_VA = 0, DA_WIDTH, 2 * DA_WIDTH
COL_QR = 3 * DA_WIDTH
COL_KR = COL_QR + RET_WIDTH
COL_VR = COL_KR + RET_WIDTH
COL_GB = COL_VR + RET_WIDTH
COL_GATE_A = COL_GB + RET_WIDTH
COL_GATE_B = COL_GATE_A + D_MODEL

PROJ_ROWS = 512
PROJ_CHUNK = 256
ATT_TILE = 512
ATT_ROWS = 32
RET_CHUNK = 256
RET_UNROLL = 16
MOE_BLOCK = 256
PACKED = D_MODEL // 2
ROW_PIECES = 2
PIECE = PACKED // ROW_PIECES
SC_CORES = 2
SC_WINDOW = 128
COMBINE_PARTS = 2
BATCH_GROUPS = 1
VMEM_LIMIT = 56 * 1024 * 1024


def _dot(a, b):
    return jnp.dot(a, b, preferred_element_type=jnp.float32)


def _dot_nt(a, b):
    return lax.dot_general(a, b, (((1,), (1,)), ((), ())), preferred_element_type=jnp.float32)


def _dot_tn(a, b):
    return lax.dot_general(a, b, (((0,), (0,)), ((), ())), preferred_element_type=jnp.float32)


def _sigmoid(x):
    return 0.5 * jnp.tanh(0.5 * x) + 0.5


def _split3(x):
    a = x.astype(jnp.bfloat16)
    r = x - a.astype(jnp.float32)
    b = r.astype(jnp.bfloat16)
    c = (r - b.astype(jnp.float32)).astype(jnp.bfloat16)
    return a, b, c


def _in_proj_kernel(x_ref, pos_ref, g1_ref, w_ref, gsum_ref, gq_ref, gk_ref, fa_ref, fr_ref, sel_ref,
                    o_ref, h_scr):
    x = x_ref[...]
    rows = x.shape[0]
    h_scr[...] = (x * g1_ref[...]).astype(jnp.bfloat16)
    rms_scale = jnp.broadcast_to(lax.rsqrt(jnp.mean(x * x, axis=-1, keepdims=True) + EPS), (rows, PROJ_CHUNK))
    pos = pos_ref[...].astype(jnp.float32)

    lane = lax.broadcasted_iota(jnp.int32, (rows, LANES), 1)
    half_a = ROPE_DIM // 2
    tables = {}

    def da_tables():
        if "da" not in tables:
            ang_a = fa_ref[...] * pos
            pad = jnp.zeros((LANES - 2 * half_a, rows), jnp.float32)
            t_a = jnp.concatenate([jnp.cos(ang_a), jnp.sin(ang_a), pad], axis=0).T
            tab = sum(_dot(part, sel_ref[...]) for part in _split3(t_a))
            c_a = tab[:, :LANES] + jnp.where(lane % DA_HALF < ROPE_DIM, 0.0, 1.0)
            s_lo = tab[:, LANES:2 * LANES]
            s_hi = tab[:, 2 * LANES:]
            tables["da"] = tuple(jnp.concatenate([v, v], axis=1) for v in (c_a, s_lo, s_hi))
        return tables["da"]

    def ret_tables():
        if "ret" not in tables:
            ang_r = fr_ref[...] * pos
            t_r = jnp.concatenate([jnp.cos(ang_r), jnp.sin(ang_r)], axis=0).T
            sw_r = pltpu.roll(t_r, RET_KDIM // 2, axis=1)
            first = lane < RET_KDIM // 2
            c_r = jnp.where(first, t_r, sw_r)
            s_r = jnp.where(first, -sw_r, t_r)
            tables["ret"] = tuple(jnp.concatenate([v, v], axis=1) for v in (c_r, s_r))
        return tables["ret"]

    def qk_norm_rope(y, g, scale):
        c_a2, s_lo2, s_hi2 = da_tables()
        ss = y * y
        hi = ss.astype(jnp.bfloat16)
        lo = (ss - hi.astype(jnp.float32)).astype(jnp.bfloat16)
        gs = _dot(hi, gsum_ref[...]) + _dot(lo, gsum_ref[...])
        n = y * lax.rsqrt(gs * (1.0 / DA_HALF) + EPS) * g
        up = pltpu.roll(n, PROJ_CHUNK - half_a, axis=1)
        dn = pltpu.roll(n, half_a, axis=1)
        r = n * c_a2 + up * s_lo2 + dn * s_hi2
        return r * scale if scale != 1.0 else r

    def ret_rope(y, scale):
        c_r2, s_r2 = ret_tables()
        halves = [pltpu.roll(y[:, i * LANES:(i + 1) * LANES], RET_KDIM // 2, axis=1)
                  for i in range(PROJ_CHUNK // LANES)]
        sw = jnp.concatenate(halves, axis=1)
        r = y * c_r2 + sw * s_r2
        return r * scale if scale != 1.0 else r

    n_chunks = IN_COLS // PROJ_CHUNK
    is_long = lambda c: c * PROJ_CHUNK < COL_VA or COL_QR <= c * PROJ_CHUNK < COL_VR
    long_chunks = [c for c in range(n_chunks) if is_long(c)]
    short_chunks = [c for c in range(n_chunks) if not is_long(c)][::-1]
    order = [short_chunks.pop(0) for _ in range(3)]
    while long_chunks or short_chunks:
        if long_chunks:
            order.append(long_chunks.pop(0))
        if short_chunks:
            order.append(short_chunks.pop(0))
    for c in order:
        c0 = c * PROJ_CHUNK
        y = _dot(h_scr[...], w_ref[:, c0:c0 + PROJ_CHUNK]) * rms_scale
        if c0 < COL_KA:
            y = qk_norm_rope(y, gq_ref[...], DA_HALF ** -0.5)
        elif c0 < COL_VA:
            y = qk_norm_rope(y, gk_ref[...], 1.0)
        elif c0 < COL_QR:
            pass
        elif c0 < COL_KR:
            y = ret_rope(y, 1.0)
        elif c0 < COL_VR:
            y = ret_rope(y, RET_KDIM ** -0.5)
        elif c0 < COL_GB:
            pass
        elif c0 < COL_GATE_A:
            y = y * _sigmoid(y)
        else:
            y = _sigmoid(y)
        o_ref[:, c0:c0 + PROJ_CHUNK] = y.astype(o_ref.dtype)


def _in_proj(x2, pos2, g1, w_in, gq, gk, row0, n):
    tm = min(PROJ_ROWS, n)
    blk0 = row0 // tm
    grp = jnp.arange(PROJ_CHUNK) // DA_HALF
    gsum = (grp[:, None] == grp[None, :]).astype(jnp.bfloat16)
    half_a = ROPE_DIM // 2
    fa = jnp.power(jnp.float32(ROPE_THETA), -2.0 * jnp.arange(half_a, dtype=jnp.float32) / ROPE_DIM)[:, None]
    half_r = RET_KDIM // 2
    fr = jnp.power(jnp.float32(RET_THETA), -2.0 * jnp.arange(half_r, dtype=jnp.float32) / RET_KDIM)[:, None]
    j = jnp.arange(LANES)[:, None]
    l64 = (jnp.arange(LANES) % DA_HALF)[None, :]
    sel_c = (j < half_a) & (l64 < ROPE_DIM) & (l64 % half_a == j)
    sel_lo = (j >= half_a) & (j < ROPE_DIM) & (l64 < half_a) & (l64 == j - half_a)
    sel_hi = (j >= half_a) & (j < ROPE_DIM) & (l64 >= half_a) & (l64 < ROPE_DIM) & (l64 == j)
    sel = jnp.concatenate([sel_c.astype(jnp.float32), -sel_lo.astype(jnp.float32),
                           sel_hi.astype(jnp.float32)], axis=1).astype(jnp.bfloat16)
    reps = PROJ_CHUNK // DA_HALF
    full = lambda shape: pl.BlockSpec(shape, lambda i: (0,) * len(shape))
    return pl.pallas_call(
        _in_proj_kernel,
        grid=(n // tm,),
        in_specs=[
            pl.BlockSpec((tm, D_MODEL), lambda i: (blk0 + i, 0)),
            pl.BlockSpec((1, tm), lambda i: (0, blk0 + i)),
            full((1, D_MODEL)),
            full((D_MODEL, IN_COLS)),
            full((PROJ_CHUNK, PROJ_CHUNK)),
            full((1, PROJ_CHUNK)),
            full((1, PROJ_CHUNK)),
            full((half_a, 1)),
            full((half_r, 1)),
            full((LANES, 3 * LANES)),
        ],
        out_specs=pl.BlockSpec((tm, IN_COLS), lambda i: (i, 0)),
        out_shape=jax.ShapeDtypeStruct((n, IN_COLS), jnp.bfloat16),
        scratch_shapes=[pltpu.VMEM((tm, D_MODEL), jnp.bfloat16)],
        compiler_params=pltpu.CompilerParams(dimension_semantics=("arbitrary",),
                                             vmem_limit_bytes=VMEM_LIMIT),
        name="in_proj",
    )(x2, pos2, g1.reshape(1, D_MODEL), w_in.astype(jnp.bfloat16), gsum,
      jnp.tile(gq, reps)[None, :], jnp.tile(gk, reps)[None, :], fa, fr, sel)


def _diff_attn_kernel(q_ref, k_ref, v_ref, lam_ref, gsub_ref, o_ref,
                      qs_scr, vx_scr, s0_scr, s1_scr, p_scr, m_scr, alpha_scr, acc_scr):
    i = pl.program_id(2)
    t = q_ref.shape[0]

    @pl.when(i == 0)
    def _():
        vx_scr[:, :DA_VDIM] = v_ref[...]
        vx_scr[:, DA_VDIM:] = jnp.ones((vx_scr.shape[0], LANES), vx_scr.dtype)

    q = q_ref[...]
    lane = lax.broadcasted_iota(jnp.int32, q.shape, 1)
    zero = jnp.zeros_like(q)
    qs_scr[:t] = jnp.where(lane < DA_HALF, q, zero)
    qs_scr[t:] = jnp.where(lane >= DA_HALF, q, zero)
    m_scr[...] = jnp.full(m_scr.shape, -jnp.inf, jnp.float32)
    acc_scr[...] = jnp.zeros(acc_scr.shape, jnp.float32)

    def scores(j, s_ref):
        start = pl.multiple_of(j * t, t)
        s_ref[...] = _dot_nt(qs_scr[...], k_ref[pl.ds(start, t), :])

    def softmax_pv(j, s_ref, masked):
        for c in range(2 * t // ATT_ROWS):
            rows = pl.ds(c * ATT_ROWS, ATT_ROWS)
            s = s_ref[rows, :]
            if masked:
                r = lax.broadcasted_iota(jnp.int32, s.shape, 0) + (c * ATT_ROWS) % t
                col = lax.broadcasted_iota(jnp.int32, s.shape, 1)
                s = jnp.where(col <= r, s, -jnp.inf)
            m_prev = m_scr[rows, :]
            m_new = jnp.maximum(m_prev, jnp.max(s, axis=-1, keepdims=True))
            alpha_scr[rows, :] = jnp.exp(m_prev - m_new)
            m_scr[rows, :] = m_new
            p = jnp.exp(s - jnp.concatenate([m_new] * (t // LANES), axis=1))
            p_scr[rows, :] = p.astype(p_scr.dtype)
        start = pl.multiple_of(j * t, t)
        pv = _dot(p_scr[...], vx_scr[pl.ds(start, t), :])
        alpha = alpha_scr[...]
        for half in range(2):
            cols = pl.ds(half * LANES, LANES)
            acc_scr[:, cols] = alpha * acc_scr[:, cols] + pv[:, half * LANES:(half + 1) * LANES]

    scores(0, s0_scr)

    def pair(jj, carry):
        j = 2 * jj
        scores(j + 1, s1_scr)
        softmax_pv(j, s0_scr, False)
        scores(j + 2, s0_scr)
        softmax_pv(j + 1, s1_scr, False)
        return carry

    lax.fori_loop(0, i // 2, pair, 0)

    @pl.when(i % 2 == 1)
    def _():
        scores(i, s1_scr)
        softmax_pv(i - 1, s0_scr, False)
        softmax_pv(i, s1_scr, True)

    @pl.when(i % 2 == 0)
    def _():
        softmax_pv(i, s0_scr, True)

    lam4 = lam_ref[...]
    lam = (jnp.exp(jnp.sum(lam4[0:1] * lam4[1:2], axis=-1, keepdims=True))
           - jnp.exp(jnp.sum(lam4[2:3] * lam4[3:4], axis=-1, keepdims=True)) + LAMBDA_INIT)
    o_all = acc_scr[:, :DA_VDIM] / acc_scr[:, DA_VDIM:]
    o = o_all[:t] - lam * o_all[t:]
    o = o * lax.rsqrt(jnp.mean(o * o, axis=-1, keepdims=True) + EPS) * gsub_ref[...] * (1.0 - LAMBDA_INIT)
    o_ref[...] = o.astype(o_ref.dtype)


def _diff_attn(proj, lam4, gsub, batch, seq):
    n = proj.shape[0]
    t = min(ATT_TILE, seq)
    nq = seq // t
    qb, kb, vb = COL_QA // LANES, COL_KA // LANES, COL_VA // LANES
    return pl.pallas_call(
        _diff_attn_kernel,
        grid=(batch, DA_HEADS, nq),
        in_specs=[
            pl.BlockSpec((t, LANES), lambda b, h, i: (b * nq + i, qb + h)),
            pl.BlockSpec((seq, LANES), lambda b, h, i: (b, kb + h)),
            pl.BlockSpec((seq, LANES), lambda b, h, i: (b, vb + h)),
            pl.BlockSpec((4, LANES), lambda b, h, i: (0, 0)),
            pl.BlockSpec((1, LANES), lambda b, h, i: (0, 0)),
        ],
        out_specs=pl.BlockSpec((t, LANES), lambda b, h, i: (b * nq + i, h)),
        out_shape=jax.ShapeDtypeStruct((n, DA_WIDTH), jnp.bfloat16),
        scratch_shapes=[pltpu.VMEM((2 * t, LANES), jnp.bfloat16),
                        pltpu.VMEM((seq, DA_VDIM + LANES), jnp.bfloat16),
                        pltpu.VMEM((2 * t, t), jnp.float32),
                        pltpu.VMEM((2 * t, t), jnp.float32),
                        pltpu.VMEM((2 * t, t), jnp.bfloat16),
                        pltpu.VMEM((2 * t, LANES), jnp.float32),
                        pltpu.VMEM((2 * t, LANES), jnp.float32),
                        pltpu.VMEM((2 * t, DA_VDIM + LANES), jnp.float32)],
        compiler_params=pltpu.CompilerParams(dimension_semantics=("arbitrary",) * 3,
                                             vmem_limit_bytes=VMEM_LIMIT),
        name="diff_attn",
    )(proj, proj, proj, lam4, gsub)


def _retention_kernel(q_ref, k_ref, v_ref, g_ref, gng_ref, gnb_ref, o_ref, r_scr, *, chunk):
    hf = jnp.full((1, 1), pl.program_id(1), jnp.int32).astype(jnp.float32)
    log_g = jnp.log1p(-jnp.exp2(-5.0 - hf))
    ri = lax.broadcasted_iota(jnp.int32, (chunk, chunk), 0)
    ci = lax.broadcasted_iota(jnp.int32, (chunk, chunk), 1)
    rel = (ri - ci).astype(jnp.float32)
    dmask = jnp.where(rel >= 0, jnp.exp(jnp.maximum(rel, 0.0) * log_g), 0.0)
    idx = lax.broadcasted_iota(jnp.int32, (chunk, 1), 0).astype(jnp.float32)
    zeta = jnp.exp((chunk - 1 - idx) * log_g)
    xi = jnp.exp((idx + 1.0) * log_g)
    g_chunk = jnp.exp(chunk * log_g)
    r_scr[...] = jnp.zeros(r_scr.shape, jnp.float32)
    gng = gng_ref[...]
    gnb = gnb_ref[...]

    def body(n, carry):
        start = pl.multiple_of(n * chunk, chunk)
        q = q_ref[pl.ds(start, chunk), :]
        k = k_ref[pl.ds(start, chunk), :]
        v = v_ref[pl.ds(start, chunk), :]
        s = _dot_nt(q, k) * dmask
        r_old = r_scr[...]
        o = _dot(s.astype(jnp.bfloat16), v) + xi * _dot(q, r_old.astype(jnp.bfloat16))
        kz = (k.astype(jnp.float32) * zeta).astype(jnp.bfloat16)
        r_scr[...] = g_chunk * r_old + _dot_tn(kz, v)
        mu = jnp.mean(o, axis=-1, keepdims=True)
        d = o - mu
        var = jnp.mean(d * d, axis=-1, keepdims=True)
        y = d * lax.rsqrt(var + EPS) * gng + gnb
        y = y * g_ref[pl.ds(start, chunk), :].astype(jnp.float32)
        o_ref[pl.ds(start, chunk), :] = y.astype(o_ref.dtype)
        return carry

    lax.fori_loop(0, q_ref.shape[0] // chunk, body, 0, unroll=RET_UNROLL)


def _retention(proj, gn_g, gn_b, batch, seq):
    n = proj.shape[0]
    chunk = min(RET_CHUNK, seq)
    col = lambda c0: (lambda b, h: (b, c0 // LANES + h))
    return pl.pallas_call(
        functools.partial(_retention_kernel, chunk=chunk),
        grid=(batch, RET_HEADS),
        in_specs=[
            pl.BlockSpec((seq, LANES), col(COL_QR)),
            pl.BlockSpec((seq, LANES), col(COL_KR)),
            pl.BlockSpec((seq, LANES), col(COL_VR)),
            pl.BlockSpec((seq, LANES), col(COL_GB)),
            pl.BlockSpec((1, LANES), lambda b, h: (0, h)),
            pl.BlockSpec((1, LANES), lambda b, h: (0, h)),
        ],
        out_specs=pl.BlockSpec((seq, LANES), lambda b, h: (b, h)),
        out_shape=jax.ShapeDtypeStruct((n, RET_WIDTH), jnp.bfloat16),
        scratch_shapes=[pltpu.VMEM((RET_KDIM, RET_VDIM), jnp.float32)],
        compiler_params=pltpu.CompilerParams(dimension_semantics=("arbitrary",) * 2,
                                             vmem_limit_bytes=VMEM_LIMIT),
        name="retention",
    )(proj, proj, proj, proj, gn_g.reshape(1, RET_WIDTH), gn_b.reshape(1, RET_WIDTH))


def _merge_kernel(x_ref, oa_ref, ob_ref, sa0_ref, sa1_ref, sb0_ref, sb1_ref, wa_ref, wb_ref, wo_ref,
                  g2_ref, wr_hi_ref, wr_lo_ref, br_ref, tri_ref, x1_ref, h2_ref, route_ref, counts_ref,
                  base_scr, logits_scr):
    i = pl.program_id(0)

    @pl.when(i == 0)
    def _():
        base_scr[...] = jnp.zeros(base_scr.shape, jnp.float32)
        logits_scr[...] = jnp.zeros(logits_scr.shape, jnp.float32)

    ya = _dot(oa_ref[...], wa_ref[...])
    yb = _dot(ob_ref[...], wb_ref[...])

    logits = logits_scr[...]
    lane = lax.broadcasted_iota(jnp.int32, logits.shape, 1)
    neg = -jnp.inf
    gl = jnp.where(lane < N_GROUPS, logits, neg)
    gmax = jnp.max(gl, axis=-1, keepdims=True)
    g_idx = jnp.min(jnp.where(gl == gmax, lane, LANES), axis=-1, keepdims=True)
    p_g = 1.0 / jnp.sum(jnp.exp(gl - gmax), axis=-1, keepdims=True)
    e_lo = N_GROUPS + EXPERTS_PER_GROUP * g_idx
    el = jnp.where((lane >= e_lo) & (lane < e_lo + EXPERTS_PER_GROUP), logits, neg)
    v1 = jnp.max(el, axis=-1, keepdims=True)
    i1 = jnp.min(jnp.where(el == v1, lane, LANES), axis=-1, keepdims=True)
    el2 = jnp.where(lane == i1, neg, el)
    v2 = jnp.max(el2, axis=-1, keepdims=True)
    i2 = jnp.min(jnp.where(el2 == v2, lane, LANES), axis=-1, keepdims=True)
    t = jnp.exp(v2 - v1)
    w1 = p_g / (1.0 + t)
    w2 = p_g * t / (1.0 + t)
    e1 = i1 - N_GROUPS
    e2 = i2 - N_GROUPS

    oh1 = lane == e1
    oh2 = lane == e2
    real = jnp.where(i > 0, 1.0, 0.0)
    picked = jnp.where(oh1 | oh2, real, 0.0)
    before = _dot(tri_ref[...], picked.astype(jnp.bfloat16)) + base_scr[0:1, :]
    rank1 = jnp.sum(jnp.where(oh1, before, 0.0), axis=-1, keepdims=True)
    rank2 = jnp.sum(jnp.where(oh2, before, 0.0), axis=-1, keepdims=True)
    base_scr[...] = base_scr[...] + jnp.sum(picked, axis=0, keepdims=True)
    counts_ref[...] = base_scr[...]

    cols = [e1.astype(jnp.float32), e2.astype(jnp.float32), w1, w2, rank1, rank2]
    route = jnp.zeros(logits.shape, jnp.float32)
    for c, val in enumerate(cols):
        route = jnp.where(lane == c, val, route)
    route_ref[...] = route

    sa = jnp.concatenate([sa0_ref[...], sa1_ref[...]], axis=1).astype(jnp.float32)
    sb = jnp.concatenate([sb0_ref[...], sb1_ref[...]], axis=1).astype(jnp.float32)
    merged = sa * ya + sb * yb
    x1 = x_ref[...] + _dot(merged.astype(jnp.bfloat16), wo_ref[...])
    x1_ref[...] = x1
    h2 = x1 * lax.rsqrt(jnp.mean(x1 * x1, axis=-1, keepdims=True) + EPS) * g2_ref[...]
    _store_packed(h2_ref, h2)

    hi = h2.astype(jnp.bfloat16)
    lo = (h2 - hi.astype(jnp.float32)).astype(jnp.bfloat16)
    logits_scr[...] = (_dot(hi, wr_hi_ref[...]) + _dot(lo, wr_hi_ref[...]) + _dot(hi, wr_lo_ref[...])
                       + br_ref[...])


def _merge(x2, oa, ob, proj, wa, wb, wo, g2, w_gr, b_gr, w_er, b_er, row0):
    n = oa.shape[0]
    tm = min(PROJ_ROWS, n)
    blk0 = row0 // tm
    half = D_MODEL // 2
    wr = jnp.zeros((D_MODEL, LANES), jnp.float32)
    wr = wr.at[:, :N_GROUPS].set(w_gr).at[:, N_GROUPS:N_GROUPS + N_EXPERTS].set(w_er)
    wr_hi = wr.astype(jnp.bfloat16)
    wr_lo = (wr - wr_hi.astype(jnp.float32)).astype(jnp.bfloat16)
    br = jnp.zeros((1, LANES), jnp.float32)
    br = br.at[0, :N_GROUPS].set(b_gr).at[0, N_GROUPS:N_GROUPS + N_EXPERTS].set(b_er)
    tri = (jnp.arange(tm)[:, None] > jnp.arange(tm)[None, :]).astype(jnp.bfloat16)
    full = lambda shape: pl.BlockSpec(shape, lambda i: (0,) * len(shape))
    nt = n // tm
    cur = lambda i: jnp.minimum(i, nt - 1)
    gate = lambda c0: pl.BlockSpec((tm, half), lambda i: (cur(i), c0 // half))
    return pl.pallas_call(
        _merge_kernel,
        grid=(nt + 1,),
        in_specs=[
            pl.BlockSpec((tm, D_MODEL), lambda i: (blk0 + cur(i), 0)),
            pl.BlockSpec((tm, DA_WIDTH), lambda i: (cur(i), 0)),
            pl.BlockSpec((tm, RET_WIDTH), lambda i: (cur(i), 0)),
            gate(COL_GATE_A), gate(COL_GATE_A + half), gate(COL_GATE_B), gate(COL_GATE_B + half),
            full((DA_WIDTH, D_MODEL)), full((RET_WIDTH, D_MODEL)), full((D_MODEL, D_MODEL)),
            full((1, D_MODEL)), full((D_MODEL, LANES)), full((D_MODEL, LANES)), full((1, LANES)),
            full((tm, tm)),
        ],
        out_specs=[
            pl.BlockSpec((tm, D_MODEL), lambda i: (cur(i), 0)),
            pl.BlockSpec((ROW_PIECES, tm, PIECE), lambda i: (0, cur(i), 0)),
            pl.BlockSpec((tm, LANES), lambda i: (jnp.maximum(i - 1, 0), 0)),
            pl.BlockSpec((8, LANES), lambda i: (0, 0)),
        ],
        out_shape=[
            jax.ShapeDtypeStruct((n, D_MODEL), jnp.float32),
            jax.ShapeDtypeStruct((ROW_PIECES, n, PIECE), jnp.uint32),
            jax.ShapeDtypeStruct((n, LANES), jnp.float32),
            jax.ShapeDtypeStruct((8, LANES), jnp.float32),
        ],
        scratch_shapes=[pltpu.VMEM((8, LANES), jnp.float32), pltpu.VMEM((tm, LANES), jnp.float32)],
        compiler_params=pltpu.CompilerParams(dimension_semantics=("arbitrary",),
                                             vmem_limit_bytes=VMEM_LIMIT),
        name="merge",
    )(x2, oa, ob, proj, proj, proj, proj, wa.astype(jnp.bfloat16), wb.astype(jnp.bfloat16),
      wo.astype(jnp.bfloat16), g2.reshape(1, D_MODEL), wr_hi, wr_lo, br, tri)


def _sc_mesh():
    return plsc.VectorSubcoreMesh(core_axis_name="c", subcore_axis_name="s")


def _sc_scatter_rows(src, idx, out_rows, src_block):
    steps = idx.shape[1] // SC_WINDOW
    per_core = steps // SC_CORES

    @pl.kernel(out_type=jax.ShapeDtypeStruct((out_rows, PIECE), src.dtype), mesh=_sc_mesh())
    def scatter(src_hbm, idx_hbm, out_hbm):
        def body(src_vmem, idx_vmem):
            pltpu.sync_copy(src_vmem, out_hbm.at[idx_vmem.at[0]])

        pltpu.emit_pipeline(
            body,
            grid=(SC_CORES, per_core),
            in_specs=[pl.BlockSpec((SC_WINDOW, PIECE), lambda c, i: (src_block(c * per_core + i), 0)),
                      pl.BlockSpec((1, SC_WINDOW), lambda c, i: (0, c * per_core + i))],
            out_specs=[],
            core_axis_name=("c", "s"),
            dimension_semantics=(pltpu.PARALLEL, pltpu.PARALLEL),
        )(src_hbm, idx_hbm)

    return scatter(src, idx)


def _sc_gather_rows(table, idx):
    num = idx.shape[1]
    per_core = num // SC_WINDOW // SC_CORES

    @pl.kernel(out_type=jax.ShapeDtypeStruct((num, PIECE), table.dtype), mesh=_sc_mesh())
    def gather(table_hbm, idx_hbm, out_hbm):
        def body(idx_vmem, out_vmem):
            pltpu.sync_copy(table_hbm.at[idx_vmem.at[0]], out_vmem)

        pltpu.emit_pipeline(
            body,
            grid=(SC_CORES, per_core),
            in_specs=[pl.BlockSpec((1, SC_WINDOW), lambda c, i: (0, c * per_core + i))],
            out_specs=[pl.BlockSpec((SC_WINDOW, PIECE), lambda c, i: (c * per_core + i, 0))],
            core_axis_name=("c", "s"),
            dimension_semantics=(pltpu.PARALLEL, pltpu.PARALLEL),
        )(idx_hbm, out_hbm)

    return gather(table, idx)


def _store_packed(ref, val):
    as_bits = lambda v: lax.bitcast_convert_type(v.astype(jnp.bfloat16).astype(jnp.float32), jnp.uint32)
    words = (as_bits(val[:, :PACKED]) >> 16) | (as_bits(val[:, PACKED:]) & jnp.uint32(0xFFFF0000))
    for j in range(ROW_PIECES):
        ref[j] = words[:, j * PIECE:(j + 1) * PIECE]


def _load_packed(ref):
    words = jnp.concatenate([ref[j] for j in range(ROW_PIECES)], axis=1)
    low = lax.bitcast_convert_type(words << 16, jnp.float32)
    high = lax.bitcast_convert_type(words & jnp.uint32(0xFFFF0000), jnp.float32)
    return jnp.concatenate([low, high], axis=1)


def _expert_kernel(blk_e_ref, n_used_ref, nxt_ref, run_ref, x_ref, wg_hbm, wu_hbm, wd_hbm, o_ref,
                   wg_stage, wu_stage, wd_stage, wg_scr, wu_scr, wd_scr, sem):
    i = pl.program_id(0)
    used = i < n_used_ref[0]

    def weight_copies(e, s):
        return (pltpu.make_async_copy(wg_hbm.at[e], wg_stage.at[s], sem.at[s, 0]),
                pltpu.make_async_copy(wu_hbm.at[e], wu_stage.at[s], sem.at[s, 1]),
                pltpu.make_async_copy(wd_hbm.at[e], wd_stage.at[s], sem.at[s, 2]))

    @pl.when(i == 0)
    def _():
        for c in weight_copies(blk_e_ref[0], 0):
            c.start()

    @pl.when(used & ((i == 0) | (blk_e_ref[i] != blk_e_ref[jnp.maximum(i - 1, 0)])))
    def _():
        s = run_ref[i] % 2
        for c in weight_copies(blk_e_ref[i], s):
            c.wait()
        wg_scr[...] = wg_stage[s].astype(jnp.bfloat16)
        wu_scr[...] = wu_stage[s].astype(jnp.bfloat16)
        wd_scr[...] = wd_stage[s].astype(jnp.bfloat16)

        @pl.when(nxt_ref[i] >= 0)
        def _():
            for c in weight_copies(nxt_ref[i], 1 - s):
                c.start()

    @pl.when(used)
    def _():
        x = _load_packed(x_ref).astype(jnp.bfloat16)
        a = _dot(x, wg_scr[...])
        u = _dot(x, wu_scr[...])
        hmid = (a * _sigmoid(a) * u).astype(jnp.bfloat16)
        _store_packed(o_ref, _dot(hmid, wd_scr[...]))

    @pl.when(jnp.logical_not(used))
    def _():
        o_ref[...] = jnp.zeros(o_ref.shape, o_ref.dtype)


def _experts(xs, blk_expert, n_used, w_gate, w_up, w_down):
    p = xs.shape[1]
    nblk = p // MOE_BLOCK
    idx = jnp.arange(nblk, dtype=jnp.int32)
    starts = (idx < n_used[0]) & ((idx == 0) | (blk_expert != jnp.roll(blk_expert, 1)))
    run = jnp.cumsum(starts.astype(jnp.int32)) - 1
    next_start = lax.cummin(jnp.where(starts, idx, nblk)[::-1])[::-1]
    after = jnp.concatenate([next_start[1:], jnp.full((1,), nblk, jnp.int32)])
    nxt = jnp.where(after < nblk, blk_expert[jnp.minimum(after, nblk - 1)], -1).astype(jnp.int32)
    live = lambda i, be, nu, nx, rn: jnp.minimum(i, nu[0] - 1)
    any_spec = pl.BlockSpec(memory_space=pl.ANY)
    return pl.pallas_call(
        _expert_kernel,
        grid_spec=pltpu.PrefetchScalarGridSpec(
            num_scalar_prefetch=4,
            grid=(nblk,),
            in_specs=[
                pl.BlockSpec((ROW_PIECES, MOE_BLOCK, PIECE), lambda i, be, nu, nx, rn: (0, live(i, be, nu, nx, rn), 0)),
                any_spec, any_spec, any_spec,
            ],
            out_specs=pl.BlockSpec((ROW_PIECES, MOE_BLOCK, PIECE), lambda i, be, nu, nx, rn: (0, i, 0)),
            scratch_shapes=[pltpu.VMEM((2, D_MODEL, EXPERT_FF), jnp.float32),
                            pltpu.VMEM((2, D_MODEL, EXPERT_FF), jnp.float32),
                            pltpu.VMEM((2, EXPERT_FF, D_MODEL), jnp.float32),
                            pltpu.VMEM((D_MODEL, EXPERT_FF), jnp.bfloat16),
                            pltpu.VMEM((D_MODEL, EXPERT_FF), jnp.bfloat16),
                            pltpu.VMEM((EXPERT_FF, D_MODEL), jnp.bfloat16),
                            pltpu.SemaphoreType.DMA((2, 3))],
        ),
        out_shape=jax.ShapeDtypeStruct((ROW_PIECES, p, PIECE), jnp.uint32),
        compiler_params=pltpu.CompilerParams(dimension_semantics=("arbitrary",),
                                             vmem_limit_bytes=VMEM_LIMIT),
        name="experts",
    )(blk_expert, n_used, nxt, run.astype(jnp.int32), xs, w_gate, w_up, w_down)


def _combine_kernel(x1_ref, route_ref, y0_ref, y1_ref, *rest):
    o_ref = rest[-1]
    route = route_ref[...]
    o_ref[...] = x1_ref[...] + route[:, 2:3] * _load_packed(y0_ref) + route[:, 3:4] * _load_packed(y1_ref)


def _combine(x1, yg, route, in_row0, out_row0, n_total, out_prev):
    n = yg.shape[1] // TOP_K
    tm = min(PROJ_ROWS, n)
    blk0 = out_row0 // tm
    in0 = in_row0 // tm
    prev = () if out_prev is None else (out_prev,)
    return pl.pallas_call(
        _combine_kernel,
        grid=(n // tm,),
        in_specs=[
            pl.BlockSpec((tm, D_MODEL), lambda i: (in0 + i, 0)),
            pl.BlockSpec((tm, LANES), lambda i: (in0 + i, 0)),
            pl.BlockSpec((ROW_PIECES, tm, PIECE), lambda i: (0, i, 0)),
            pl.BlockSpec((ROW_PIECES, tm, PIECE), lambda i: (0, i + n // tm, 0)),
        ] + [pl.BlockSpec(memory_space=pl.ANY)] * len(prev),
        out_specs=pl.BlockSpec((tm, D_MODEL), lambda i: (blk0 + i, 0)),
        out_shape=jax.ShapeDtypeStruct((n_total, D_MODEL), jnp.float32),
        input_output_aliases={4: 0} if prev else {},
        compiler_params=pltpu.CompilerParams(dimension_semantics=("arbitrary",),
                                             vmem_limit_bytes=VMEM_LIMIT),
        name="combine",
    )(x1, route, yg, yg, *prev)


def _dispatch_plan(route, counts, n):
    counts = counts[0, :N_EXPERTS].astype(jnp.int32)
    padded = ((counts + MOE_BLOCK - 1) // MOE_BLOCK) * MOE_BLOCK
    seg_end = jnp.cumsum(padded).astype(jnp.int32)
    seg_start = seg_end - padded
    cols = route[:, :8].T.astype(jnp.int32)
    e, rank = cols[0:TOP_K], cols[4:4 + TOP_K]
    picked = e[None] == jnp.arange(N_EXPERTS, dtype=jnp.int32)[:, None, None]
    dest = jnp.sum(jnp.where(picked, seg_start[:, None, None], 0), axis=0) + rank
    p = n * TOP_K + N_EXPERTS * MOE_BLOCK
    slot = dest[None] + (jnp.arange(ROW_PIECES, dtype=jnp.int32) * p)[:, None, None]
    blk_start = jnp.arange(p // MOE_BLOCK, dtype=jnp.int32) * MOE_BLOCK
    blk_expert = jnp.sum((seg_end[None, :] <= blk_start[:, None]).astype(jnp.int32), axis=1)
    blk_expert = jnp.minimum(blk_expert, N_EXPERTS - 1)
    n_used = (seg_end[-1] // MOE_BLOCK).reshape(1)
    return slot, blk_expert, n_used, p


def _layer_rows(x2, pos2, row0, batch, seq, out_prev, norm1_g, w_in, q_norm_g, k_norm_g, lam4, diff_subln_g,
                ret_gn_g, ret_gn_b, w_branch_a, w_branch_b, w_out, norm2_g, w_gr, b_gr, w_er, b_er,
                w_gate, w_up, w_down):
    n = batch * seq
    proj = _in_proj(x2, pos2, norm1_g, w_in, q_norm_g, k_norm_g, row0, n)
    oa = _diff_attn(proj, lam4, diff_subln_g.reshape(1, DA_VDIM), batch, seq)
    ob = _retention(proj, ret_gn_g, ret_gn_b, batch, seq)
    x1, h2, route, counts = _merge(x2, oa, ob, proj, w_branch_a, w_branch_b, w_out, norm2_g,
                                   w_gr, b_gr, w_er, b_er, row0)
    slot, blk_expert, n_used, p = _dispatch_plan(route, counts, n)
    win_n = n // SC_WINDOW
    src_block = lambda s: (s // (TOP_K * win_n)) * win_n + s % win_n
    xs = _sc_scatter_rows(h2.reshape(ROW_PIECES * n, PIECE), slot.reshape(1, -1), ROW_PIECES * p, src_block)
    ys = _experts(xs.reshape(ROW_PIECES, p, PIECE), blk_expert, n_used, w_gate, w_up, w_down)
    parts = COMBINE_PARTS if n % (COMBINE_PARTS * PROJ_ROWS) == 0 else 1
    m = n // parts
    out = out_prev
    for t in range(parts):
        yg = _sc_gather_rows(ys.reshape(ROW_PIECES * p, PIECE), slot[:, :, t * m:(t + 1) * m].reshape(1, -1))
        out = _combine(x1, yg.reshape(ROW_PIECES, TOP_K * m, PIECE), route, t * m, row0 + t * m,
                       x2.shape[0], out)
    return out


def _layer(x, positions, *weights):
    batch, seq, _ = x.shape
    n = batch * seq
    x2 = x.reshape(n, D_MODEL)
    pos2 = positions.reshape(1, n)
    groups = BATCH_GROUPS if batch % BATCH_GROUPS == 0 else 1
    per = batch // groups
    out = None
    for g in range(groups):
        out = _layer_rows(x2, pos2, g * per * seq, per, seq, out, *weights)
    return out.reshape(batch, seq, D_MODEL)


def kernel(x, positions, norm1_g, w_in, q_norm_g, k_norm_g, lambda_q1, lambda_k1, lambda_q2, lambda_k2, diff_subln_g, ret_gn_g, ret_gn_b, w_branch_a, w_branch_b, w_out, norm2_g, w_group_router, b_group_router, w_expert_router, b_expert_router, w_gate, w_up, w_down):
    assert x.shape[-1] == D_MODEL and norm1_g.shape[0] == 1, "single-layer, D_MODEL-wide input expected"
    lam4 = jnp.zeros((4, LANES), jnp.float32)
    lam4 = lam4.at[:, :DA_HALF].set(jnp.stack([lambda_q1[0], lambda_k1[0], lambda_q2[0], lambda_k2[0]]))
    return _layer(x, positions, norm1_g[0], w_in[0], q_norm_g[0], k_norm_g[0], lam4, diff_subln_g[0],
                  ret_gn_g[0], ret_gn_b[0], w_branch_a[0], w_branch_b[0], w_out[0], norm2_g[0],
                  w_group_router[0], b_group_router[0], w_expert_router[0], b_expert_router[0],
                  w_gate[0], w_up[0], w_down[0])
```

```python
import functools
import math

import jax
import jax.numpy as jnp
from jax import lax
from jax.experimental import pallas as pl
from jax.experimental.pallas import tpu as pltpu
from jax.experimental.pallas import tpu_sc as plsc

D_MODEL = 1024
DA_HEADS = 4
DA_HALF = 64
DA_VDIM = 2 * DA_HALF
DA_WIDTH = DA_HEADS * DA_VDIM
ROPE_THETA = 500000.0
ROPE_DIM = DA_HALF // 4
RET_HEADS = 4
RET_KDIM = 128
RET_VDIM = 128
RET_WIDTH = RET_HEADS * RET_VDIM
RET_THETA = 10000.0
N_GROUPS = 4
EXPERTS_PER_GROUP = 8
N_EXPERTS = N_GROUPS * EXPERTS_PER_GROUP
TOP_K = 2
EXPERT_FF = 512
EPS = 1e-6
LAMBDA_INIT = 0.8 - 0.6 * math.exp(-0.3 * 0)

LANES = 128
IN_COLS = 3 * DA_WIDTH + 4 * RET_WIDTH + 2 * D_MODEL
COL_QA, COL_KA, COL_VA = 0, DA_WIDTH, 2 * DA_WIDTH
COL_QR = 3 * DA_WIDTH
COL_KR = COL_QR + RET_WIDTH
COL_VR = COL_KR + RET_WIDTH
COL_GB = COL_VR + RET_WIDTH
COL_GATE_A = COL_GB + RET_WIDTH
COL_GATE_B = COL_GATE_A + D_MODEL

PROJ_ROWS = 512
PROJ_CHUNK = 256
ATT_TILE = 512
ATT_ROWS = 32
RET_CHUNK = 256
RET_UNROLL = 16
MOE_BLOCK = 512
PACKED = D_MODEL // 2
ROW_PIECES = 2
PIECE = PACKED // ROW_PIECES
SC_CORES = 2
SC_WINDOW = 128
COMBINE_PARTS = 2
VMEM_LIMIT = 56 * 1024 * 1024


def _dot(a, b):
    return jnp.dot(a, b, preferred_element_type=jnp.float32)


def _dot_nt(a, b):
    return lax.dot_general(a, b, (((1,), (1,)), ((), ())), preferred_element_type=jnp.float32)


def _dot_tn(a, b):
    return lax.dot_general(a, b, (((0,), (0,)), ((), ())), preferred_element_type=jnp.float32)


def _sigmoid(x):
    return 0.5 * jnp.tanh(0.5 * x) + 0.5


def _split3(x):
    a = x.astype(jnp.bfloat16)
    r = x - a.astype(jnp.float32)
    b = r.astype(jnp.bfloat16)
    c = (r - b.astype(jnp.float32)).astype(jnp.bfloat16)
    return a, b, c


def _in_proj_kernel(x_ref, pos_ref, g1_ref, w_ref, gsum_ref, gq_ref, gk_ref, fa_ref, fr_ref, sel_ref,
                    o_ref, h_scr):
    x = x_ref[...]
    rows = x.shape[0]
    h_scr[...] = (x * g1_ref[...]).astype(jnp.bfloat16)
    rms_scale = jnp.broadcast_to(lax.rsqrt(jnp.mean(x * x, axis=-1, keepdims=True) + EPS), (rows, PROJ_CHUNK))
    pos = pos_ref[...].astype(jnp.float32)

    lane = lax.broadcasted_iota(jnp.int32, (rows, LANES), 1)
    half_a = ROPE_DIM // 2
    tables = {}

    def da_tables():
        if "da" not in tables:
            ang_a = fa_ref[...] * pos
            pad = jnp.zeros((LANES - 2 * half_a, rows), jnp.float32)
            t_a = jnp.concatenate([jnp.cos(ang_a), jnp.sin(ang_a), pad], axis=0).T
            tab = sum(_dot(part, sel_ref[...]) for part in _split3(t_a))
            c_a = tab[:, :LANES] + jnp.where(lane % DA_HALF < ROPE_DIM, 0.0, 1.0)
            s_lo = tab[:, LANES:2 * LANES]
            s_hi = tab[:, 2 * LANES:]
            tables["da"] = tuple(jnp.concatenate([v, v], axis=1) for v in (c_a, s_lo, s_hi))
        return tables["da"]

    def ret_tables():
        if "ret" not in tables:
            ang_r = fr_ref[...] * pos
            t_r = jnp.concatenate([jnp.cos(ang_r), jnp.sin(ang_r)], axis=0).T
            sw_r = pltpu.roll(t_r, RET_KDIM // 2, axis=1)
            first = lane < RET_KDIM // 2
            c_r = jnp.where(first, t_r, sw_r)
            s_r = jnp.where(first, -sw_r, t_r)
            tables["ret"] = tuple(jnp.concatenate([v, v], axis=1) for v in (c_r, s_r))
        return tables["ret"]

    def qk_norm_rope(y, g, scale):
        c_a2, s_lo2, s_hi2 = da_tables()
        ss = y * y
        hi = ss.astype(jnp.bfloat16)
        lo = (ss - hi.astype(jnp.float32)).astype(jnp.bfloat16)
        gs = _dot(hi, gsum_ref[...]) + _dot(lo, gsum_ref[...])
        n = y * lax.rsqrt(gs * (1.0 / DA_HALF) + EPS) * g
        up = pltpu.roll(n, PROJ_CHUNK - half_a, axis=1)
        dn = pltpu.roll(n, half_a, axis=1)
        r = n * c_a2 + up * s_lo2 + dn * s_hi2
        return r * scale if scale != 1.0 else r

    def ret_rope(y, scale):
        c_r2, s_r2 = ret_tables()
        halves = [pltpu.roll(y[:, i * LANES:(i + 1) * LANES], RET_KDIM // 2, axis=1)
                  for i in range(PROJ_CHUNK // LANES)]
        sw = jnp.concatenate(halves, axis=1)
        r = y * c_r2 + sw * s_r2
        return r * scale if scale != 1.0 else r

    n_chunks = IN_COLS // PROJ_CHUNK
    is_long = lambda c: c * PROJ_CHUNK < COL_VA or COL_QR <= c * PROJ_CHUNK < COL_VR
    long_chunks = [c for c in range(n_chunks) if is_long(c)]
    short_chunks = [c for c in range(n_chunks) if not is_long(c)][::-1]
    order = [short_chunks.pop(0) for _ in range(3)]
    while long_chunks or short_chunks:
        if long_chunks:
            order.append(long_chunks.pop(0))
        if short_chunks:
            order.append(short_chunks.pop(0))
    for c in order:
        c0 = c * PROJ_CHUNK
        y = _dot(h_scr[...], w_ref[:, c0:c0 + PROJ_CHUNK]) * rms_scale
        if c0 < COL_KA:
            y = qk_norm_rope(y, gq_ref[...], DA_HALF ** -0.5)
        elif c0 < COL_VA:
            y = qk_norm_rope(y, gk_ref[...], 1.0)
        elif c0 < COL_QR:
            pass
        elif c0 < COL_KR:
            y = ret_rope(y, 1.0)
        elif c0 < COL_VR:
            y = ret_rope(y, RET_KDIM ** -0.5)
        elif c0 < COL_GB:
            pass
        elif c0 < COL_GATE_A:
            y = y * _sigmoid(y)
        else:
            y = _sigmoid(y)
        o_ref[:, c0:c0 + PROJ_CHUNK] = y.astype(o_ref.dtype)


def _in_proj(x2, pos2, g1, w_in, gq, gk):
    n = x2.shape[0]
    tm = min(PROJ_ROWS, n)
    grp = jnp.arange(PROJ_CHUNK) // DA_HALF
    gsum = (grp[:, None] == grp[None, :]).astype(jnp.bfloat16)
    half_a = ROPE_DIM // 2
    fa = jnp.power(jnp.float32(ROPE_THETA), -2.0 * jnp.arange(half_a, dtype=jnp.float32) / ROPE_DIM)[:, None]
    half_r = RET_KDIM // 2
    fr = jnp.power(jnp.float32(RET_THETA), -2.0 * jnp.arange(half_r, dtype=jnp.float32) / RET_KDIM)[:, None]
    j = jnp.arange(LANES)[:, None]
    l64 = (jnp.arange(LANES) % DA_HALF)[None, :]
    sel_c = (j < half_a) & (l64 < ROPE_DIM) & (l64 % half_a == j)
    sel_lo = (j >= half_a) & (j < ROPE_DIM) & (l64 < half_a) & (l64 == j - half_a)
    sel_hi = (j >= half_a) & (j < ROPE_DIM) & (l64 >= half_a) & (l64 < ROPE_DIM) & (l64 == j)
    sel = jnp.concatenate([sel_c.astype(jnp.float32), -sel_lo.astype(jnp.float32),
                           sel_hi.astype(jnp.float32)], axis=1).astype(jnp.bfloat16)
    reps = PROJ_CHUNK // DA_HALF
    full = lambda shape: pl.BlockSpec(shape, lambda i: (0,) * len(shape))
    return pl.pallas_call(
        _in_proj_kernel,
        grid=(n // tm,),
        in_specs=[
            pl.BlockSpec((tm, D_MODEL), lambda i: (i, 0)),
            pl.BlockSpec((1, tm), lambda i: (0, i)),
            full((1, D_MODEL)),
            full((D_MODEL, IN_COLS)),
            full((PROJ_CHUNK, PROJ_CHUNK)),
            full((1, PROJ_CHUNK)),
            full((1, PROJ_CHUNK)),
            full((half_a, 1)),
            full((half_r, 1)),
            full((LANES, 3 * LANES)),
        ],
        out_specs=pl.BlockSpec((tm, IN_COLS), lambda i: (i, 0)),
        out_shape=jax.ShapeDtypeStruct((n, IN_COLS), jnp.bfloat16),
        scratch_shapes=[pltpu.VMEM((tm, D_MODEL), jnp.bfloat16)],
        compiler_params=pltpu.CompilerParams(dimension_semantics=("arbitrary",),
                                             vmem_limit_bytes=VMEM_LIMIT),
        name="in_proj",
    )(x2, pos2, g1.reshape(1, D_MODEL), w_in.astype(jnp.bfloat16), gsum,
      jnp.tile(gq, reps)[None, :], jnp.tile(gk, reps)[None, :], fa, fr, sel)


def _diff_attn_kernel(q_ref, k_ref, v_ref, lam_ref, gsub_ref, o_ref,
                      qs_scr, vx_scr, s0_scr, s1_scr, p_scr, m_scr, alpha_scr, acc_scr):
    i = pl.program_id(2)
    t = q_ref.shape[0]

    @pl.when(i == 0)
    def _():
        vx_scr[:, :DA_VDIM] = v_ref[...]
        vx_scr[:, DA_VDIM:] = jnp.ones((vx_scr.shape[0], LANES), vx_scr.dtype)

    q = q_ref[...]
    lane = lax.broadcasted_iota(jnp.int32, q.shape, 1)
    zero = jnp.zeros_like(q)
    qs_scr[:t] = jnp.where(lane < DA_HALF, q, zero)
    qs_scr[t:] = jnp.where(lane >= DA_HALF, q, zero)
    m_scr[...] = jnp.full(m_scr.shape, -jnp.inf, jnp.float32)
    acc_scr[...] = jnp.zeros(acc_scr.shape, jnp.float32)

    def scores(j, s_ref):
        start = pl.multiple_of(j * t, t)
        s_ref[...] = _dot_nt(qs_scr[...], k_ref[pl.ds(start, t), :])

    def softmax_pv(j, s_ref, masked):
        for c in range(2 * t // ATT_ROWS):
            rows = pl.ds(c * ATT_ROWS, ATT_ROWS)
            s = s_ref[rows, :]
            if masked:
                r = lax.broadcasted_iota(jnp.int32, s.shape, 0) + (c * ATT_ROWS) % t
                col = lax.broadcasted_iota(jnp.int32, s.shape, 1)
                s = jnp.where(col <= r, s, -jnp.inf)
            m_prev = m_scr[rows, :]
            m_new = jnp.maximum(m_prev, jnp.max(s, axis=-1, keepdims=True))
            alpha_scr[rows, :] = jnp.exp(m_prev - m_new)
            m_scr[rows, :] = m_new
            p = jnp.exp(s - jnp.concatenate([m_new] * (t // LANES), axis=1))
            p_scr[rows, :] = p.astype(p_scr.dtype)
        start = pl.multiple_of(j * t, t)
        pv = _dot(p_scr[...], vx_scr[pl.ds(start, t), :])
        alpha = alpha_scr[...]
        for half in range(2):
            cols = pl.ds(half * LANES, LANES)
            acc_scr[:, cols] = alpha * acc_scr[:, cols] + pv[:, half * LANES:(half + 1) * LANES]

    scores(0, s0_scr)

    def pair(jj, carry):
        j = 2 * jj
        scores(j + 1, s1_scr)
        softmax_pv(j, s0_scr, False)
        scores(j + 2, s0_scr)
        softmax_pv(j + 1, s1_scr, False)
        return carry

    lax.fori_loop(0, i // 2, pair, 0)

    @pl.when(i % 2 == 1)
    def _():
        scores(i, s1_scr)
        softmax_pv(i - 1, s0_scr, False)
        softmax_pv(i, s1_scr, True)

    @pl.when(i % 2 == 0)
    def _():
        softmax_pv(i, s0_scr, True)

    lam4 = lam_ref[...]
    lam = (jnp.exp(jnp.sum(lam4[0:1] * lam4[1:2], axis=-1, keepdims=True))
           - jnp.exp(jnp.sum(lam4[2:3] * lam4[3:4], axis=-1, keepdims=True)) + LAMBDA_INIT)
    o_all = acc_scr[:, :DA_VDIM] / acc_scr[:, DA_VDIM:]
    o = o_all[:t] - lam * o_all[t:]
    o = o * lax.rsqrt(jnp.mean(o * o, axis=-1, keepdims=True) + EPS) * gsub_ref[...] * (1.0 - LAMBDA_INIT)
    o_ref[...] = o.astype(o_ref.dtype)


def _diff_attn(proj, lam4, gsub, batch, seq):
    n = proj.shape[0]
    t = min(ATT_TILE, seq)
    nq = seq // t
    qb, kb, vb = COL_QA // LANES, COL_KA // LANES, COL_VA // LANES
    return pl.pallas_call(
        _diff_attn_kernel,
        grid=(batch, DA_HEADS, nq),
        in_specs=[
            pl.BlockSpec((t, LANES), lambda b, h, i: (b * nq + i, qb + h)),
            pl.BlockSpec((seq, LANES), lambda b, h, i: (b, kb + h)),
            pl.BlockSpec((seq, LANES), lambda b, h, i: (b, vb + h)),
            pl.BlockSpec((4, LANES), lambda b, h, i: (0, 0)),
            pl.BlockSpec((1, LANES), lambda b, h, i: (0, 0)),
        ],
        out_specs=pl.BlockSpec((t, LANES), lambda b, h, i: (b * nq + i, h)),
        out_shape=jax.ShapeDtypeStruct((n, DA_WIDTH), jnp.bfloat16),
        scratch_shapes=[pltpu.VMEM((2 * t, LANES), jnp.bfloat16),
                        pltpu.VMEM((seq, DA_VDIM + LANES), jnp.bfloat16),
                        pltpu.VMEM((2 * t, t), jnp.float32),
                        pltpu.VMEM((2 * t, t), jnp.float32),
                        pltpu.VMEM((2 * t, t), jnp.bfloat16),
                        pltpu.VMEM((2 * t, LANES), jnp.float32),
                        pltpu.VMEM((2 * t, LANES), jnp.float32),
                        pltpu.VMEM((2 * t, DA_VDIM + LANES), jnp.float32)],
        compiler_params=pltpu.CompilerParams(dimension_semantics=("arbitrary",) * 3,
                                             vmem_limit_bytes=VMEM_LIMIT),
        name="diff_attn",
    )(proj, proj, proj, lam4, gsub)


def _retention_kernel(q_ref, k_ref, v_ref, g_ref, gng_ref, gnb_ref, o_ref, r_scr, *, chunk):
    hf = jnp.full((1, 1), pl.program_id(1), jnp.int32).astype(jnp.float32)
    log_g = jnp.log1p(-jnp.exp2(-5.0 - hf))
    ri = lax.broadcasted_iota(jnp.int32, (chunk, chunk), 0)
    ci = lax.broadcasted_iota(jnp.int32, (chunk, chunk), 1)
    rel = (ri - ci).astype(jnp.float32)
    dmask = jnp.where(rel >= 0, jnp.exp(jnp.maximum(rel, 0.0) * log_g), 0.0)
    idx = lax.broadcasted_iota(jnp.int32, (chunk, 1), 0).astype(jnp.float32)
    zeta = jnp.exp((chunk - 1 - idx) * log_g)
    xi = jnp.exp((idx + 1.0) * log_g)
    g_chunk = jnp.exp(chunk * log_g)
    r_scr[...] = jnp.zeros(r_scr.shape, jnp.float32)
    gng = gng_ref[...]
    gnb = gnb_ref[...]

    def body(n, carry):
        start = pl.multiple_of(n * chunk, chunk)
        q = q_ref[pl.ds(start, chunk), :]
        k = k_ref[pl.ds(start, chunk), :]
        v = v_ref[pl.ds(start, chunk), :]
        s = _dot_nt(q, k) * dmask
        r_old = r_scr[...]
        o = _dot(s.astype(jnp.bfloat16), v) + xi * _dot(q, r_old.astype(jnp.bfloat16))
        kz = (k.astype(jnp.float32) * zeta).astype(jnp.bfloat16)
        r_scr[...] = g_chunk * r_old + _dot_tn(kz, v)
        mu = jnp.mean(o, axis=-1, keepdims=True)
        d = o - mu
        var = jnp.mean(d * d, axis=-1, keepdims=True)
        y = d * lax.rsqrt(var + EPS) * gng + gnb
        y = y * g_ref[pl.ds(start, chunk), :].astype(jnp.float32)
        o_ref[pl.ds(start, chunk), :] = y.astype(o_ref.dtype)
        return carry

    lax.fori_loop(0, q_ref.shape[0] // chunk, body, 0, unroll=RET_UNROLL)


def _retention(proj, gn_g, gn_b, batch, seq):
    n = proj.shape[0]
    chunk = min(RET_CHUNK, seq)
    col = lambda c0: (lambda b, h: (b, c0 // LANES + h))
    return pl.pallas_call(
        functools.partial(_retention_kernel, chunk=chunk),
        grid=(batch, RET_HEADS),
        in_specs=[
            pl.BlockSpec((seq, LANES), col(COL_QR)),
            pl.BlockSpec((seq, LANES), col(COL_KR)),
            pl.BlockSpec((seq, LANES), col(COL_VR)),
            pl.BlockSpec((seq, LANES), col(COL_GB)),
            pl.BlockSpec((1, LANES), lambda b, h: (0, h)),
            pl.BlockSpec((1, LANES), lambda b, h: (0, h)),
        ],
        out_specs=pl.BlockSpec((seq, LANES), lambda b, h: (b, h)),
        out_shape=jax.ShapeDtypeStruct((n, RET_WIDTH), jnp.bfloat16),
        scratch_shapes=[pltpu.VMEM((RET_KDIM, RET_VDIM), jnp.float32)],
        compiler_params=pltpu.CompilerParams(dimension_semantics=("arbitrary",) * 2,
                                             vmem_limit_bytes=VMEM_LIMIT),
        name="retention",
    )(proj, proj, proj, proj, gn_g.reshape(1, RET_WIDTH), gn_b.reshape(1, RET_WIDTH))


def _merge_kernel(x_ref, oa_ref, ob_ref, sa0_ref, sa1_ref, sb0_ref, sb1_ref, wa_ref, wb_ref, wo_ref,
                  g2_ref, wr_hi_ref, wr_lo_ref, br_ref, tri_ref, x1_ref, h2_ref, route_ref, counts_ref,
                  base_scr, logits_scr):
    i = pl.program_id(0)

    @pl.when(i == 0)
    def _():
        base_scr[...] = jnp.zeros(base_scr.shape, jnp.float32)
        logits_scr[...] = jnp.zeros(logits_scr.shape, jnp.float32)

    ya = _dot(oa_ref[...], wa_ref[...])
    yb = _dot(ob_ref[...], wb_ref[...])

    logits = logits_scr[...]
    lane = lax.broadcasted_iota(jnp.int32, logits.shape, 1)
    neg = -jnp.inf
    gl = jnp.where(lane < N_GROUPS, logits, neg)
    gmax = jnp.max(gl, axis=-1, keepdims=True)
    g_idx = jnp.min(jnp.where(gl == gmax, lane, LANES), axis=-1, keepdims=True)
    p_g = 1.0 / jnp.sum(jnp.exp(gl - gmax), axis=-1, keepdims=True)
    e_lo = N_GROUPS + EXPERTS_PER_GROUP * g_idx
    el = jnp.where((lane >= e_lo) & (lane < e_lo + EXPERTS_PER_GROUP), logits, neg)
    v1 = jnp.max(el, axis=-1, keepdims=True)
    i1 = jnp.min(jnp.where(el == v1, lane, LANES), axis=-1, keepdims=True)
    el2 = jnp.where(lane == i1, neg, el)
    v2 = jnp.max(el2, axis=-1, keepdims=True)
    i2 = jnp.min(jnp.where(el2 == v2, lane, LANES), axis=-1, keepdims=True)
    t = jnp.exp(v2 - v1)
    w1 = p_g / (1.0 + t)
    w2 = p_g * t / (1.0 + t)
    e1 = i1 - N_GROUPS
    e2 = i2 - N_GROUPS

    oh1 = lane == e1
    oh2 = lane == e2
    real = jnp.where(i > 0, 1.0, 0.0)
    picked = jnp.where(oh1 | oh2, real, 0.0)
    before = _dot(tri_ref[...], picked.astype(jnp.bfloat16)) + base_scr[0:1, :]
    rank1 = jnp.sum(jnp.where(oh1, before, 0.0), axis=-1, keepdims=True)
    rank2 = jnp.sum(jnp.where(oh2, before, 0.0), axis=-1, keepdims=True)
    base_scr[...] = base_scr[...] + jnp.sum(picked, axis=0, keepdims=True)
    counts_ref[...] = base_scr[...]

    cols = [e1.astype(jnp.float32), e2.astype(jnp.float32), w1, w2, rank1, rank2]
    route = jnp.zeros(logits.shape, jnp.float32)
    for c, val in enumerate(cols):
        route = jnp.where(lane == c, val, route)
    route_ref[...] = route

    sa = jnp.concatenate([sa0_ref[...], sa1_ref[...]], axis=1).astype(jnp.float32)
    sb = jnp.concatenate([sb0_ref[...], sb1_ref[...]], axis=1).astype(jnp.float32)
    merged = sa * ya + sb * yb
    x1 = x_ref[...] + _dot(merged.astype(jnp.bfloat16), wo_ref[...])
    x1_ref[...] = x1
    h2 = x1 * lax.rsqrt(jnp.mean(x1 * x1, axis=-1, keepdims=True) + EPS) * g2_ref[...]
    _store_packed(h2_ref, h2)

    hi = h2.astype(jnp.bfloat16)
    lo = (h2 - hi.astype(jnp.float32)).astype(jnp.bfloat16)
    logits_scr[...] = (_dot(hi, wr_hi_ref[...]) + _dot(lo, wr_hi_ref[...]) + _dot(hi, wr_lo_ref[...])
                       + br_ref[...])


def _merge(x2, oa, ob, proj, wa, wb, wo, g2, w_gr, b_gr, w_er, b_er):
    n = x2.shape[0]
    tm = min(PROJ_ROWS, n)
    half = D_MODEL // 2
    wr = jnp.zeros((D_MODEL, LANES), jnp.float32)
    wr = wr.at[:, :N_GROUPS].set(w_gr).at[:, N_GROUPS:N_GROUPS + N_EXPERTS].set(w_er)
    wr_hi = wr.astype(jnp.bfloat16)
    wr_lo = (wr - wr_hi.astype(jnp.float32)).astype(jnp.bfloat16)
    br = jnp.zeros((1, LANES), jnp.float32)
    br = br.at[0, :N_GROUPS].set(b_gr).at[0, N_GROUPS:N_GROUPS + N_EXPERTS].set(b_er)
    tri = (jnp.arange(tm)[:, None] > jnp.arange(tm)[None, :]).astype(jnp.bfloat16)
    full = lambda shape: pl.BlockSpec(shape, lambda i: (0,) * len(shape))
    nt = n // tm
    cur = lambda i: jnp.minimum(i, nt - 1)
    gate = lambda c0: pl.BlockSpec((tm, half), lambda i: (cur(i), c0 // half))
    return pl.pallas_call(
        _merge_kernel,
        grid=(nt + 1,),
        in_specs=[
            pl.BlockSpec((tm, D_MODEL), lambda i: (cur(i), 0)),
            pl.BlockSpec((tm, DA_WIDTH), lambda i: (cur(i), 0)),
            pl.BlockSpec((tm, RET_WIDTH), lambda i: (cur(i), 0)),
            gate(COL_GATE_A), gate(COL_GATE_A + half), gate(COL_GATE_B), gate(COL_GATE_B + half),
            full((DA_WIDTH, D_MODEL)), full((RET_WIDTH, D_MODEL)), full((D_MODEL, D_MODEL)),
            full((1, D_MODEL)), full((D_MODEL, LANES)), full((D_MODEL, LANES)), full((1, LANES)),
            full((tm, tm)),
        ],
        out_specs=[
            pl.BlockSpec((tm, D_MODEL), lambda i: (cur(i), 0)),
            pl.BlockSpec((ROW_PIECES, tm, PIECE), lambda i: (0, cur(i), 0)),
            pl.BlockSpec((tm, LANES), lambda i: (jnp.maximum(i - 1, 0), 0)),
            pl.BlockSpec((8, LANES), lambda i: (0, 0)),
        ],
        out_shape=[
            jax.ShapeDtypeStruct((n, D_MODEL), jnp.float32),
            jax.ShapeDtypeStruct((ROW_PIECES, n, PIECE), jnp.uint32),
            jax.ShapeDtypeStruct((n, LANES), jnp.float32),
            jax.ShapeDtypeStruct((8, LANES), jnp.float32),
        ],
        scratch_shapes=[pltpu.VMEM((8, LANES), jnp.float32), pltpu.VMEM((tm, LANES), jnp.float32)],
        compiler_params=pltpu.CompilerParams(dimension_semantics=("arbitrary",),
                                             vmem_limit_bytes=VMEM_LIMIT),
        name="merge",
    )(x2, oa, ob, proj, proj, proj, proj, wa.astype(jnp.bfloat16), wb.astype(jnp.bfloat16),
      wo.astype(jnp.bfloat16), g2.reshape(1, D_MODEL), wr_hi, wr_lo, br, tri)


def _sc_mesh():
    return plsc.VectorSubcoreMesh(core_axis_name="c", subcore_axis_name="s")


def _sc_scatter_rows(src, idx, out_rows, src_block):
    steps = idx.shape[1] // SC_WINDOW
    per_core = steps // SC_CORES

    @pl.kernel(out_type=jax.ShapeDtypeStruct((out_rows, PIECE), src.dtype), mesh=_sc_mesh())
    def scatter(src_hbm, idx_hbm, out_hbm):
        def body(src_vmem, idx_vmem):
            pltpu.sync_copy(src_vmem, out_hbm.at[idx_vmem.at[0]])

        pltpu.emit_pipeline(
            body,
            grid=(SC_CORES, per_core),
            in_specs=[pl.BlockSpec((SC_WINDOW, PIECE), lambda c, i: (src_block(c * per_core + i), 0)),
                      pl.BlockSpec((1, SC_WINDOW), lambda c, i: (0, c * per_core + i))],
            out_specs=[],
            core_axis_name=("c", "s"),
            dimension_semantics=(pltpu.PARALLEL, pltpu.PARALLEL),
        )(src_hbm, idx_hbm)

    return scatter(src, idx)


def _sc_gather_rows(table, idx):
    num = idx.shape[1]
    per_core = num // SC_WINDOW // SC_CORES

    @pl.kernel(out_type=jax.ShapeDtypeStruct((num, PIECE), table.dtype), mesh=_sc_mesh())
    def gather(table_hbm, idx_hbm, out_hbm):
        def body(idx_vmem, out_vmem):
            pltpu.sync_copy(table_hbm.at[idx_vmem.at[0]], out_vmem)

        pltpu.emit_pipeline(
            body,
            grid=(SC_CORES, per_core),
            in_specs=[pl.BlockSpec((1, SC_WINDOW), lambda c, i: (0, c * per_core + i))],
            out_specs=[pl.BlockSpec((SC_WINDOW, PIECE), lambda c, i: (c * per_core + i, 0))],
            core_axis_name=("c", "s"),
            dimension_semantics=(pltpu.PARALLEL, pltpu.PARALLEL),
        )(idx_hbm, out_hbm)

    return gather(table, idx)


def _store_packed(ref, val):
    as_bits = lambda v: lax.bitcast_convert_type(v.astype(jnp.bfloat16).astype(jnp.float32), jnp.uint32)
    words = (as_bits(val[:, :PACKED]) >> 16) | (as_bits(val[:, PACKED:]) & jnp.uint32(0xFFFF0000))
    for j in range(ROW_PIECES):
        ref[j] = words[:, j * PIECE:(j + 1) * PIECE]


def _load_packed(ref):
    words = jnp.concatenate([ref[j] for j in range(ROW_PIECES)], axis=1)
    low = lax.bitcast_convert_type(words << 16, jnp.float32)
    high = lax.bitcast_convert_type(words & jnp.uint32(0xFFFF0000), jnp.float32)
    return jnp.concatenate([low, high], axis=1)


def _expert_kernel(blk_e_ref, n_used_ref, nxt_ref, run_ref, x_ref, wg_hbm, wu_hbm, wd_hbm, o_ref,
                   wg_stage, wu_stage, wd_stage, wg_scr, wu_scr, wd_scr, sem):
    i = pl.program_id(0)
    used = i < n_used_ref[0]

    def weight_copies(e, s):
        return (pltpu.make_async_copy(wg_hbm.at[e], wg_stage.at[s], sem.at[s, 0]),
                pltpu.make_async_copy(wu_hbm.at[e], wu_stage.at[s], sem.at[s, 1]),
                pltpu.make_async_copy(wd_hbm.at[e], wd_stage.at[s], sem.at[s, 2]))

    @pl.when(i == 0)
    def _():
        for c in weight_copies(blk_e_ref[0], 0):
            c.start()

    @pl.when(used & ((i == 0) | (blk_e_ref[i] != blk_e_ref[jnp.maximum(i - 1, 0)])))
    def _():
        s = run_ref[i] % 2
        for c in weight_copies(blk_e_ref[i], s):
            c.wait()
        wg_scr[...] = wg_stage[s].astype(jnp.bfloat16)
        wu_scr[...] = wu_stage[s].astype(jnp.bfloat16)
        wd_scr[...] = wd_stage[s].astype(jnp.bfloat16)

        @pl.when(nxt_ref[i] >= 0)
        def _():
            for c in weight_copies(nxt_ref[i], 1 - s):
                c.start()

    @pl.when(used)
    def _():
        x = _load_packed(x_ref).astype(jnp.bfloat16)
        a = _dot(x, wg_scr[...])
        u = _dot(x, wu_scr[...])
        hmid = (a * _sigmoid(a) * u).astype(jnp.bfloat16)
        _store_packed(o_ref, _dot(hmid, wd_scr[...]))

    @pl.when(jnp.logical_not(used))
    def _():
        o_ref[...] = jnp.zeros(o_ref.shape, o_ref.dtype)


def _experts(xs, blk_expert, n_used, w_gate, w_up, w_down):
    p = xs.shape[1]
    nblk = p // MOE_BLOCK
    idx = jnp.arange(nblk, dtype=jnp.int32)
    starts = (idx < n_used[0]) & ((idx == 0) | (blk_expert != jnp.roll(blk_expert, 1)))
    run = jnp.cumsum(starts.astype(jnp.int32)) - 1
    next_start = lax.cummin(jnp.where(starts, idx, nblk)[::-1])[::-1]
    after = jnp.concatenate([next_start[1:], jnp.full((1,), nblk, jnp.int32)])
    nxt = jnp.where(after < nblk, blk_expert[jnp.minimum(after, nblk - 1)], -1).astype(jnp.int32)
    live = lambda i, be, nu, nx, rn: jnp.minimum(i, nu[0] - 1)
    any_spec = pl.BlockSpec(memory_space=pl.ANY)
    return pl.pallas_call(
        _expert_kernel,
        grid_spec=pltpu.PrefetchScalarGridSpec(
            num_scalar_prefetch=4,
            grid=(nblk,),
            in_specs=[
                pl.BlockSpec((ROW_PIECES, MOE_BLOCK, PIECE), lambda i, be, nu, nx, rn: (0, live(i, be, nu, nx, rn), 0)),
                any_spec, any_spec, any_spec,
            ],
            out_specs=pl.BlockSpec((ROW_PIECES, MOE_BLOCK, PIECE), lambda i, be, nu, nx, rn: (0, i, 0)),
            scratch_shapes=[pltpu.VMEM((2, D_MODEL, EXPERT_FF), jnp.float32),
                            pltpu.VMEM((2, D_MODEL, EXPERT_FF), jnp.float32),
                            pltpu.VMEM((2, EXPERT_FF, D_MODEL), jnp.float32),
                            pltpu.VMEM((D_MODEL, EXPERT_FF), jnp.bfloat16),
                            pltpu.VMEM((D_MODEL, EXPERT_FF), jnp.bfloat16),
                            pltpu.VMEM((EXPERT_FF, D_MODEL), jnp.bfloat16),
                            pltpu.SemaphoreType.DMA((2, 3))],
        ),
        out_shape=jax.ShapeDtypeStruct((ROW_PIECES, p, PIECE), jnp.uint32),
        compiler_params=pltpu.CompilerParams(dimension_semantics=("arbitrary",),
                                             vmem_limit_bytes=VMEM_LIMIT),
        name="experts",
    )(blk_expert, n_used, nxt, run.astype(jnp.int32), xs, w_gate, w_up, w_down)


def _combine_kernel(x1_ref, route_ref, y0_ref, y1_ref, *rest):
    o_ref = rest[-1]
    route = route_ref[...]
    o_ref[...] = x1_ref[...] + route[:, 2:3] * _load_packed(y0_ref) + route[:, 3:4] * _load_packed(y1_ref)


def _combine(x1, yg, route, row0, out_prev):
    n = yg.shape[1] // TOP_K
    tm = min(PROJ_ROWS, n)
    blk0 = row0 // tm
    prev = () if out_prev is None else (out_prev,)
    return pl.pallas_call(
        _combine_kernel,
        grid=(n // tm,),
        in_specs=[
            pl.BlockSpec((tm, D_MODEL), lambda i: (blk0 + i, 0)),
            pl.BlockSpec((tm, LANES), lambda i: (blk0 + i, 0)),
            pl.BlockSpec((ROW_PIECES, tm, PIECE), lambda i: (0, i, 0)),
            pl.BlockSpec((ROW_PIECES, tm, PIECE), lambda i: (0, i + n // tm, 0)),
        ] + [pl.BlockSpec(memory_space=pl.ANY)] * len(prev),
        out_specs=pl.BlockSpec((tm, D_MODEL), lambda i: (blk0 + i, 0)),
        out_shape=jax.ShapeDtypeStruct(x1.shape, jnp.float32),
        input_output_aliases={4: 0} if prev else {},
        compiler_params=pltpu.CompilerParams(dimension_semantics=("arbitrary",),
                                             vmem_limit_bytes=VMEM_LIMIT),
        name="combine",
    )(x1, route, yg, yg, *prev)


def _dispatch_plan(route, counts, n):
    counts = counts[0, :N_EXPERTS].astype(jnp.int32)
    padded = ((counts + MOE_BLOCK - 1) // MOE_BLOCK) * MOE_BLOCK
    seg_end = jnp.cumsum(padded).astype(jnp.int32)
    seg_start = seg_end - padded
    cols = route[:, :8].T.astype(jnp.int32)
    e, rank = cols[0:TOP_K], cols[4:4 + TOP_K]
    picked = e[None] == jnp.arange(N_EXPERTS, dtype=jnp.int32)[:, None, None]
    dest = jnp.sum(jnp.where(picked, seg_start[:, None, None], 0), axis=0) + rank
    p = n * TOP_K + N_EXPERTS * MOE_BLOCK
    slot = dest[None] + (jnp.arange(ROW_PIECES, dtype=jnp.int32) * p)[:, None, None]
    blk_start = jnp.arange(p // MOE_BLOCK, dtype=jnp.int32) * MOE_BLOCK
    blk_expert = jnp.sum((seg_end[None, :] <= blk_start[:, None]).astype(jnp.int32), axis=1)
    blk_expert = jnp.minimum(blk_expert, N_EXPERTS - 1)
    n_used = (seg_end[-1] // MOE_BLOCK).reshape(1)
    return slot, blk_expert, n_used, p


def _layer(x, positions, norm1_g, w_in, q_norm_g, k_norm_g, lam4, diff_subln_g, ret_gn_g, ret_gn_b,
           w_branch_a, w_branch_b, w_out, norm2_g, w_gr, b_gr, w_er, b_er, w_gate, w_up, w_down):
    batch, seq, _ = x.shape
    n = batch * seq
    x2 = x.reshape(n, D_MODEL)
    proj = _in_proj(x2, positions.reshape(1, n), norm1_g, w_in, q_norm_g, k_norm_g)
    oa = _diff_attn(proj, lam4, diff_subln_g.reshape(1, DA_VDIM), batch, seq)
    ob = _retention(proj, ret_gn_g, ret_gn_b, batch, seq)
    x1, h2, route, counts = _merge(x2, oa, ob, proj, w_branch_a, w_branch_b, w_out, norm2_g, w_gr, b_gr, w_er, b_er)
    slot, blk_expert, n_used, p = _dispatch_plan(route, counts, n)
    win_n = n // SC_WINDOW
    src_block = lambda s: (s // (TOP_K * win_n)) * win_n + s % win_n
    xs = _sc_scatter_rows(h2.reshape(ROW_PIECES * n, PIECE), slot.reshape(1, -1), ROW_PIECES * p, src_block)
    ys = _experts(xs.reshape(ROW_PIECES, p, PIECE), blk_expert, n_used, w_gate, w_up, w_down)
    parts = COMBINE_PARTS if n % (COMBINE_PARTS * PROJ_ROWS) == 0 else 1
    m = n // parts
    out = None
    for t in range(parts):
        yg = _sc_gather_rows(ys.reshape(ROW_PIECES * p, PIECE), slot[:, :, t * m:(t + 1) * m].reshape(1, -1))
        out = _combine(x1, yg.reshape(ROW_PIECES, TOP_K * m, PIECE), route, t * m, out)
    return out.reshape(batch, seq, D_MODEL)


def kernel(x, positions, norm1_g, w_in, q_norm_g, k_norm_g, lambda_q1, lambda_k1, lambda_q2, lambda_k2, diff_subln_g, ret_gn_g, ret_gn_b, w_branch_a, w_branch_b, w_out, norm2_g, w_group_router, b_group_router, w_expert_router, b_expert_router, w_gate, w_up, w_down):
    assert x.shape[-1] == D_MODEL and norm1_g.shape[0] == 1, "single-layer, D_MODEL-wide input expected"
    lam4 = jnp.zeros((4, LANES), jnp.float32)
    lam4 = lam4.at[:, :DA_HALF].set(jnp.stack([lambda_q1[0], lambda_k1[0], lambda_q2[0], lambda_k2[0]]))
    return _layer(x, positions, norm1_g[0], w_in[0], q_norm_g[0], k_norm_g[0], lam4, diff_subln_g[0],
                  ret_gn_g[0], ret_gn_b[0], w_branch_a[0], w_branch_b[0], w_out[0], norm2_g[0],
                  w_group_router[0], b_group_router[0], w_expert_router[0], b_expert_router[0],
                  w_gate[0], w_up[0], w_down[0])
```

```python
import functools
import math

import jax
import jax.numpy as jnp
from jax import lax
from jax.experimental import pallas as pl
from jax.experimental.pallas import tpu as pltpu
from jax.experimental.pallas import tpu_sc as plsc

D_MODEL = 1024
DA_HEADS = 4
DA_HALF = 64
DA_VDIM = 2 * DA_HALF
DA_WIDTH = DA_HEADS * DA_VDIM
ROPE_THETA = 500000.0
ROPE_DIM = DA_HALF // 4
RET_HEADS = 4
RET_KDIM = 128
RET_VDIM = 128
RET_WIDTH = RET_HEADS * RET_VDIM
RET_THETA = 10000.0
N_GROUPS = 4
EXPERTS_PER_GROUP = 8
N_EXPERTS = N_GROUPS * EXPERTS_PER_GROUP
TOP_K = 2
EXPERT_FF = 512
EPS = 1e-6
LAMBDA_INIT = 0.8 - 0.6 * math.exp(-0.3 * 0)

LANES = 128
IN_COLS = 3 * DA_WIDTH + 4 * RET_WIDTH + 2 * D_MODEL
COL_QA, COL_KA, COL_VA = 0, DA_WIDTH, 2 * DA_WIDTH
COL_QR = 3 * DA_WIDTH
COL_KR = COL_QR + RET_WIDTH
COL_VR = COL_KR + RET_WIDTH
COL_GB = COL_VR + RET_WIDTH
COL_GATE_A = COL_GB + RET_WIDTH
COL_GATE_B = COL_GATE_A + D_MODEL

PROJ_ROWS = 512
PROJ_CHUNK = 256
ATT_TILE = 512
ATT_ROWS = 32
RET_CHUNK = 256
RET_UNROLL = 16
MOE_BLOCK = 512
PACKED = D_MODEL // 2
ROW_PIECES = 2
PIECE = PACKED // ROW_PIECES
SC_CORES = 2
SC_WINDOW = 128
COMBINE_PARTS = 2
VMEM_LIMIT = 56 * 1024 * 1024


def _dot(a, b):
    return jnp.dot(a, b, preferred_element_type=jnp.float32)


def _dot_nt(a, b):
    return lax.dot_general(a, b, (((1,), (1,)), ((), ())), preferred_element_type=jnp.float32)


def _dot_tn(a, b):
    return lax.dot_general(a, b, (((0,), (0,)), ((), ())), preferred_element_type=jnp.float32)


def _sigmoid(x):
    return 0.5 * jnp.tanh(0.5 * x) + 0.5


def _split3(x):
    a = x.astype(jnp.bfloat16)
    r = x - a.astype(jnp.float32)
    b = r.astype(jnp.bfloat16)
    c = (r - b.astype(jnp.float32)).astype(jnp.bfloat16)
    return a, b, c


def _in_proj_kernel(x_ref, pos_ref, g1_ref, w_ref, gsum_ref, gq_ref, gk_ref, fa_ref, fr_ref, sel_ref,
                    o_ref, h_scr):
    x = x_ref[...]
    rows = x.shape[0]
    h_scr[...] = (x * g1_ref[...]).astype(jnp.bfloat16)
    rms_scale = jnp.broadcast_to(lax.rsqrt(jnp.mean(x * x, axis=-1, keepdims=True) + EPS), (rows, PROJ_CHUNK))
    pos = pos_ref[...].astype(jnp.float32)

    lane = lax.broadcasted_iota(jnp.int32, (rows, LANES), 1)
    half_a = ROPE_DIM // 2
    tables = {}

    def da_tables():
        if "da" not in tables:
            ang_a = fa_ref[...] * pos
            pad = jnp.zeros((LANES - 2 * half_a, rows), jnp.float32)
            t_a = jnp.concatenate([jnp.cos(ang_a), jnp.sin(ang_a), pad], axis=0).T
            tab = sum(_dot(part, sel_ref[...]) for part in _split3(t_a))
            c_a = tab[:, :LANES] + jnp.where(lane % DA_HALF < ROPE_DIM, 0.0, 1.0)
            s_lo = tab[:, LANES:2 * LANES]
            s_hi = tab[:, 2 * LANES:]
            tables["da"] = tuple(jnp.concatenate([v, v], axis=1) for v in (c_a, s_lo, s_hi))
        return tables["da"]

    def ret_tables():
        if "ret" not in tables:
            ang_r = fr_ref[...] * pos
            t_r = jnp.concatenate([jnp.cos(ang_r), jnp.sin(ang_r)], axis=0).T
            sw_r = pltpu.roll(t_r, RET_KDIM // 2, axis=1)
            first = lane < RET_KDIM // 2
            c_r = jnp.where(first, t_r, sw_r)
            s_r = jnp.where(first, -sw_r, t_r)
            tables["ret"] = tuple(jnp.concatenate([v, v], axis=1) for v in (c_r, s_r))
        return tables["ret"]

    def qk_norm_rope(y, g, scale):
        c_a2, s_lo2, s_hi2 = da_tables()
        ss = y * y
        hi = ss.astype(jnp.bfloat16)
        lo = (ss - hi.astype(jnp.float32)).astype(jnp.bfloat16)
        gs = _dot(hi, gsum_ref[...]) + _dot(lo, gsum_ref[...])
        n = y * lax.rsqrt(gs * (1.0 / DA_HALF) + EPS) * g
        up = pltpu.roll(n, PROJ_CHUNK - half_a, axis=1)
        dn = pltpu.roll(n, half_a, axis=1)
        r = n * c_a2 + up * s_lo2 + dn * s_hi2
        return r * scale if scale != 1.0 else r

    def ret_rope(y, scale):
        c_r2, s_r2 = ret_tables()
        halves = [pltpu.roll(y[:, i * LANES:(i + 1) * LANES], RET_KDIM // 2, axis=1)
                  for i in range(PROJ_CHUNK // LANES)]
        sw = jnp.concatenate(halves, axis=1)
        r = y * c_r2 + sw * s_r2
        return r * scale if scale != 1.0 else r

    n_chunks = IN_COLS // PROJ_CHUNK
    is_long = lambda c: c * PROJ_CHUNK < COL_VA or COL_QR <= c * PROJ_CHUNK < COL_VR
    long_chunks = [c for c in range(n_chunks) if is_long(c)]
    short_chunks = [c for c in range(n_chunks) if not is_long(c)][::-1]
    order = [short_chunks.pop(0) for _ in range(3)]
    while long_chunks or short_chunks:
        if long_chunks:
            order.append(long_chunks.pop(0))
        if short_chunks:
            order.append(short_chunks.pop(0))
    for c in order:
        c0 = c * PROJ_CHUNK
        y = _dot(h_scr[...], w_ref[:, c0:c0 + PROJ_CHUNK]) * rms_scale
        if c0 < COL_KA:
            y = qk_norm_rope(y, gq_ref[...], DA_HALF ** -0.5)
        elif c0 < COL_VA:
            y = qk_norm_rope(y, gk_ref[...], 1.0)
        elif c0 < COL_QR:
            pass
        elif c0 < COL_KR:
            y = ret_rope(y, 1.0)
        elif c0 < COL_VR:
            y = ret_rope(y, RET_KDIM ** -0.5)
        elif c0 < COL_GB:
            pass
        elif c0 < COL_GATE_A:
            y = y * _sigmoid(y)
        else:
            y = _sigmoid(y)
        o_ref[:, c0:c0 + PROJ_CHUNK] = y.astype(o_ref.dtype)


def _in_proj(x2, pos2, g1, w_in, gq, gk):
    n = x2.shape[0]
    tm = min(PROJ_ROWS, n)
    grp = jnp.arange(PROJ_CHUNK) // DA_HALF
    gsum = (grp[:, None] == grp[None, :]).astype(jnp.bfloat16)
    half_a = ROPE_DIM // 2
    fa = jnp.power(jnp.float32(ROPE_THETA), -2.0 * jnp.arange(half_a, dtype=jnp.float32) / ROPE_DIM)[:, None]
    half_r = RET_KDIM // 2
    fr = jnp.power(jnp.float32(RET_THETA), -2.0 * jnp.arange(half_r, dtype=jnp.float32) / RET_KDIM)[:, None]
    j = jnp.arange(LANES)[:, None]
    l64 = (jnp.arange(LANES) % DA_HALF)[None, :]
    sel_c = (j < half_a) & (l64 < ROPE_DIM) & (l64 % half_a == j)
    sel_lo = (j >= half_a) & (j < ROPE_DIM) & (l64 < half_a) & (l64 == j - half_a)
    sel_hi = (j >= half_a) & (j < ROPE_DIM) & (l64 >= half_a) & (l64 < ROPE_DIM) & (l64 == j)
    sel = jnp.concatenate([sel_c.astype(jnp.float32), -sel_lo.astype(jnp.float32),
                           sel_hi.astype(jnp.float32)], axis=1).astype(jnp.bfloat16)
    reps = PROJ_CHUNK // DA_HALF
    full = lambda shape: pl.BlockSpec(shape, lambda i: (0,) * len(shape))
    return pl.pallas_call(
        _in_proj_kernel,
        grid=(n // tm,),
        in_specs=[
            pl.BlockSpec((tm, D_MODEL), lambda i: (i, 0)),
            pl.BlockSpec((1, tm), lambda i: (0, i)),
            full((1, D_MODEL)),
            full((D_MODEL, IN_COLS)),
            full((PROJ_CHUNK, PROJ_CHUNK)),
            full((1, PROJ_CHUNK)),
            full((1, PROJ_CHUNK)),
            full((half_a, 1)),
            full((half_r, 1)),
            full((LANES, 3 * LANES)),
        ],
        out_specs=pl.BlockSpec((tm, IN_COLS), lambda i: (i, 0)),
        out_shape=jax.ShapeDtypeStruct((n, IN_COLS), jnp.bfloat16),
        scratch_shapes=[pltpu.VMEM((tm, D_MODEL), jnp.bfloat16)],
        compiler_params=pltpu.CompilerParams(dimension_semantics=("arbitrary",),
                                             vmem_limit_bytes=VMEM_LIMIT),
        name="in_proj",
    )(x2, pos2, g1.reshape(1, D_MODEL), w_in.astype(jnp.bfloat16), gsum,
      jnp.tile(gq, reps)[None, :], jnp.tile(gk, reps)[None, :], fa, fr, sel)


def _diff_attn_kernel(q_ref, k_ref, v_ref, lam_ref, gsub_ref, o_ref,
                      qs_scr, vx_scr, s0_scr, s1_scr, p_scr, m_scr, alpha_scr, acc_scr):
    i = pl.program_id(2)
    t = q_ref.shape[0]

    @pl.when(i == 0)
    def _():
        vx_scr[:, :DA_VDIM] = v_ref[...]
        vx_scr[:, DA_VDIM:] = jnp.ones((vx_scr.shape[0], LANES), vx_scr.dtype)

    q = q_ref[...]
    lane = lax.broadcasted_iota(jnp.int32, q.shape, 1)
    zero = jnp.zeros_like(q)
    qs_scr[:t] = jnp.where(lane < DA_HALF, q, zero)
    qs_scr[t:] = jnp.where(lane >= DA_HALF, q, zero)
    m_scr[...] = jnp.full(m_scr.shape, -jnp.inf, jnp.float32)
    acc_scr[...] = jnp.zeros(acc_scr.shape, jnp.float32)

    def scores(j, s_ref):
        start = pl.multiple_of(j * t, t)
        s_ref[...] = _dot_nt(qs_scr[...], k_ref[pl.ds(start, t), :])

    def softmax_pv(j, s_ref, masked):
        for c in range(2 * t // ATT_ROWS):
            rows = pl.ds(c * ATT_ROWS, ATT_ROWS)
            s = s_ref[rows, :]
            if masked:
                r = lax.broadcasted_iota(jnp.int32, s.shape, 0) + (c * ATT_ROWS) % t
                col = lax.broadcasted_iota(jnp.int32, s.shape, 1)
                s = jnp.where(col <= r, s, -jnp.inf)
            m_prev = m_scr[rows, :]
            m_new = jnp.maximum(m_prev, jnp.max(s, axis=-1, keepdims=True))
            alpha_scr[rows, :] = jnp.exp(m_prev - m_new)
            m_scr[rows, :] = m_new
            p = jnp.exp(s - jnp.concatenate([m_new] * (t // LANES), axis=1))
            p_scr[rows, :] = p.astype(p_scr.dtype)
        start = pl.multiple_of(j * t, t)
        pv = _dot(p_scr[...], vx_scr[pl.ds(start, t), :])
        alpha = alpha_scr[...]
        for half in range(2):
            cols = pl.ds(half * LANES, LANES)
            acc_scr[:, cols] = alpha * acc_scr[:, cols] + pv[:, half * LANES:(half + 1) * LANES]

    scores(0, s0_scr)

    def pair(jj, carry):
        j = 2 * jj
        scores(j + 1, s1_scr)
        softmax_pv(j, s0_scr, False)
        scores(j + 2, s0_scr)
        softmax_pv(j + 1, s1_scr, False)
        return carry

    lax.fori_loop(0, i // 2, pair, 0)

    @pl.when(i % 2 == 1)
    def _():
        scores(i, s1_scr)
        softmax_pv(i - 1, s0_scr, False)
        softmax_pv(i, s1_scr, True)

    @pl.when(i % 2 == 0)
    def _():
        softmax_pv(i, s0_scr, True)

    lam4 = lam_ref[...]
    lam = (jnp.exp(jnp.sum(lam4[0:1] * lam4[1:2], axis=-1, keepdims=True))
           - jnp.exp(jnp.sum(lam4[2:3] * lam4[3:4], axis=-1, keepdims=True)) + LAMBDA_INIT)
    o_all = acc_scr[:, :DA_VDIM] / acc_scr[:, DA_VDIM:]
    o = o_all[:t] - lam * o_all[t:]
    o = o * lax.rsqrt(jnp.mean(o * o, axis=-1, keepdims=True) + EPS) * gsub_ref[...] * (1.0 - LAMBDA_INIT)
    o_ref[...] = o.astype(o_ref.dtype)


def _diff_attn(proj, lam4, gsub, batch, seq):
    n = proj.shape[0]
    t = min(ATT_TILE, seq)
    nq = seq // t
    qb, kb, vb = COL_QA // LANES, COL_KA // LANES, COL_VA // LANES
    return pl.pallas_call(
        _diff_attn_kernel,
        grid=(batch, DA_HEADS, nq),
        in_specs=[
            pl.BlockSpec((t, LANES), lambda b, h, i: (b * nq + i, qb + h)),
            pl.BlockSpec((seq, LANES), lambda b, h, i: (b, kb + h)),
            pl.BlockSpec((seq, LANES), lambda b, h, i: (b, vb + h)),
            pl.BlockSpec((4, LANES), lambda b, h, i: (0, 0)),
            pl.BlockSpec((1, LANES), lambda b, h, i: (0, 0)),
        ],
        out_specs=pl.BlockSpec((t, LANES), lambda b, h, i: (b * nq + i, h)),
        out_shape=jax.ShapeDtypeStruct((n, DA_WIDTH), jnp.bfloat16),
        scratch_shapes=[pltpu.VMEM((2 * t, LANES), jnp.bfloat16),
                        pltpu.VMEM((seq, DA_VDIM + LANES), jnp.bfloat16),
                        pltpu.VMEM((2 * t, t), jnp.float32),
                        pltpu.VMEM((2 * t, t), jnp.float32),
                        pltpu.VMEM((2 * t, t), jnp.bfloat16),
                        pltpu.VMEM((2 * t, LANES), jnp.float32),
                        pltpu.VMEM((2 * t, LANES), jnp.float32),
                        pltpu.VMEM((2 * t, DA_VDIM + LANES), jnp.float32)],
        compiler_params=pltpu.CompilerParams(dimension_semantics=("arbitrary",) * 3,
                                             vmem_limit_bytes=VMEM_LIMIT),
        name="diff_attn",
    )(proj, proj, proj, lam4, gsub)


def _retention_kernel(q_ref, k_ref, v_ref, g_ref, gng_ref, gnb_ref, o_ref, r_scr, *, chunk):
    hf = jnp.full((1, 1), pl.program_id(1), jnp.int32).astype(jnp.float32)
    log_g = jnp.log1p(-jnp.exp2(-5.0 - hf))
    ri = lax.broadcasted_iota(jnp.int32, (chunk, chunk), 0)
    ci = lax.broadcasted_iota(jnp.int32, (chunk, chunk), 1)
    rel = (ri - ci).astype(jnp.float32)
    dmask = jnp.where(rel >= 0, jnp.exp(jnp.maximum(rel, 0.0) * log_g), 0.0)
    idx = lax.broadcasted_iota(jnp.int32, (chunk, 1), 0).astype(jnp.float32)
    zeta = jnp.exp((chunk - 1 - idx) * log_g)
    xi = jnp.exp((idx + 1.0) * log_g)
    g_chunk = jnp.exp(chunk * log_g)
    r_scr[...] = jnp.zeros(r_scr.shape, jnp.float32)
    gng = gng_ref[...]
    gnb = gnb_ref[...]

    def body(n, carry):
        start = pl.multiple_of(n * chunk, chunk)
        q = q_ref[pl.ds(start, chunk), :]
        k = k_ref[pl.ds(start, chunk), :]
        v = v_ref[pl.ds(start, chunk), :]
        s = _dot_nt(q, k) * dmask
        r_old = r_scr[...]
        o = _dot(s.astype(jnp.bfloat16), v) + xi * _dot(q, r_old.astype(jnp.bfloat16))
        kz = (k.astype(jnp.float32) * zeta).astype(jnp.bfloat16)
        r_scr[...] = g_chunk * r_old + _dot_tn(kz, v)
        mu = jnp.mean(o, axis=-1, keepdims=True)
        d = o - mu
        var = jnp.mean(d * d, axis=-1, keepdims=True)
        y = d * lax.rsqrt(var + EPS) * gng + gnb
        y = y * g_ref[pl.ds(start, chunk), :].astype(jnp.float32)
        o_ref[pl.ds(start, chunk), :] = y.astype(o_ref.dtype)
        return carry

    lax.fori_loop(0, q_ref.shape[0] // chunk, body, 0, unroll=RET_UNROLL)


def _retention(proj, gn_g, gn_b, batch, seq):
    n = proj.shape[0]
    chunk = min(RET_CHUNK, seq)
    col = lambda c0: (lambda b, h: (b, c0 // LANES + h))
    return pl.pallas_call(
        functools.partial(_retention_kernel, chunk=chunk),
        grid=(batch, RET_HEADS),
        in_specs=[
            pl.BlockSpec((seq, LANES), col(COL_QR)),
            pl.BlockSpec((seq, LANES), col(COL_KR)),
            pl.BlockSpec((seq, LANES), col(COL_VR)),
            pl.BlockSpec((seq, LANES), col(COL_GB)),
            pl.BlockSpec((1, LANES), lambda b, h: (0, h)),
            pl.BlockSpec((1, LANES), lambda b, h: (0, h)),
        ],
        out_specs=pl.BlockSpec((seq, LANES), lambda b, h: (b, h)),
        out_shape=jax.ShapeDtypeStruct((n, RET_WIDTH), jnp.bfloat16),
        scratch_shapes=[pltpu.VMEM((RET_KDIM, RET_VDIM), jnp.float32)],
        compiler_params=pltpu.CompilerParams(dimension_semantics=("arbitrary",) * 2,
                                             vmem_limit_bytes=VMEM_LIMIT),
        name="retention",
    )(proj, proj, proj, proj, gn_g.reshape(1, RET_WIDTH), gn_b.reshape(1, RET_WIDTH))


def _merge_kernel(x_ref, oa_ref, ob_ref, sa0_ref, sa1_ref, sb0_ref, sb1_ref, wa_ref, wb_ref, wo_ref,
                  g2_ref, wr_hi_ref, wr_lo_ref, br_ref, tri_ref, x1_ref, h2_ref, route_ref, cols_ref, counts_ref,
                  base_scr, logits_scr):
    i = pl.program_id(0)

    @pl.when(i == 0)
    def _():
        base_scr[...] = jnp.zeros(base_scr.shape, jnp.float32)
        logits_scr[...] = jnp.zeros(logits_scr.shape, jnp.float32)

    ya = _dot(oa_ref[...], wa_ref[...])
    yb = _dot(ob_ref[...], wb_ref[...])

    logits = logits_scr[...]
    lane = lax.broadcasted_iota(jnp.int32, logits.shape, 1)
    neg = -jnp.inf
    gl = jnp.where(lane < N_GROUPS, logits, neg)
    gmax = jnp.max(gl, axis=-1, keepdims=True)
    g_idx = jnp.min(jnp.where(gl == gmax, lane, LANES), axis=-1, keepdims=True)
    p_g = 1.0 / jnp.sum(jnp.exp(gl - gmax), axis=-1, keepdims=True)
    e_lo = N_GROUPS + EXPERTS_PER_GROUP * g_idx
    el = jnp.where((lane >= e_lo) & (lane < e_lo + EXPERTS_PER_GROUP), logits, neg)
    v1 = jnp.max(el, axis=-1, keepdims=True)
    i1 = jnp.min(jnp.where(el == v1, lane, LANES), axis=-1, keepdims=True)
    el2 = jnp.where(lane == i1, neg, el)
    v2 = jnp.max(el2, axis=-1, keepdims=True)
    i2 = jnp.min(jnp.where(el2 == v2, lane, LANES), axis=-1, keepdims=True)
    t = jnp.exp(v2 - v1)
    w1 = p_g / (1.0 + t)
    w2 = p_g * t / (1.0 + t)
    e1 = i1 - N_GROUPS
    e2 = i2 - N_GROUPS

    oh1 = lane == e1
    oh2 = lane == e2
    real = jnp.where(i > 0, 1.0, 0.0)
    picked = jnp.where(oh1 | oh2, real, 0.0)
    before = _dot(tri_ref[...], picked.astype(jnp.bfloat16)) + base_scr[0:1, :]
    rank1 = jnp.sum(jnp.where(oh1, before, 0.0), axis=-1, keepdims=True)
    rank2 = jnp.sum(jnp.where(oh2, before, 0.0), axis=-1, keepdims=True)
    base_scr[...] = base_scr[...] + jnp.sum(picked, axis=0, keepdims=True)
    counts_ref[...] = base_scr[...]

    cols = [e1.astype(jnp.float32), e2.astype(jnp.float32), w1, w2, rank1, rank2]
    route = jnp.zeros(logits.shape, jnp.float32)
    for c, val in enumerate(cols):
        route = jnp.where(lane == c, val, route)
    route_ref[...] = route
    cols_ref[...] = route.T[:8]

    sa = jnp.concatenate([sa0_ref[...], sa1_ref[...]], axis=1).astype(jnp.float32)
    sb = jnp.concatenate([sb0_ref[...], sb1_ref[...]], axis=1).astype(jnp.float32)
    merged = sa * ya + sb * yb
    x1 = x_ref[...] + _dot(merged.astype(jnp.bfloat16), wo_ref[...])
    x1_ref[...] = x1
    h2 = x1 * lax.rsqrt(jnp.mean(x1 * x1, axis=-1, keepdims=True) + EPS) * g2_ref[...]
    _store_packed(h2_ref, h2)

    hi = h2.astype(jnp.bfloat16)
    lo = (h2 - hi.astype(jnp.float32)).astype(jnp.bfloat16)
    logits_scr[...] = (_dot(hi, wr_hi_ref[...]) + _dot(lo, wr_hi_ref[...]) + _dot(hi, wr_lo_ref[...])
                       + br_ref[...])


def _merge(x2, oa, ob, proj, wa, wb, wo, g2, w_gr, b_gr, w_er, b_er):
    n = x2.shape[0]
    tm = min(PROJ_ROWS, n)
    half = D_MODEL // 2
    pad = LANES - N_GROUPS - N_EXPERTS
    wr = jnp.concatenate([w_gr, w_er, jnp.zeros((D_MODEL, pad), jnp.float32)], axis=1)
    wr_hi = wr.astype(jnp.bfloat16)
    wr_lo = (wr - wr_hi.astype(jnp.float32)).astype(jnp.bfloat16)
    br = jnp.concatenate([b_gr, b_er, jnp.zeros((pad,), jnp.float32)])[None, :]
    tri = (jnp.arange(tm)[:, None] > jnp.arange(tm)[None, :]).astype(jnp.bfloat16)
    full = lambda shape: pl.BlockSpec(shape, lambda i: (0,) * len(shape))
    nt = n // tm
    cur = lambda i: jnp.minimum(i, nt - 1)
    gate = lambda c0: pl.BlockSpec((tm, half), lambda i: (cur(i), c0 // half))
    return pl.pallas_call(
        _merge_kernel,
        grid=(nt + 1,),
        in_specs=[
            pl.BlockSpec((tm, D_MODEL), lambda i: (cur(i), 0)),
            pl.BlockSpec((tm, DA_WIDTH), lambda i: (cur(i), 0)),
            pl.BlockSpec((tm, RET_WIDTH), lambda i: (cur(i), 0)),
            gate(COL_GATE_A), gate(COL_GATE_A + half), gate(COL_GATE_B), gate(COL_GATE_B + half),
            full((DA_WIDTH, D_MODEL)), full((RET_WIDTH, D_MODEL)), full((D_MODEL, D_MODEL)),
            full((1, D_MODEL)), full((D_MODEL, LANES)), full((D_MODEL, LANES)), full((1, LANES)),
            full((tm, tm)),
        ],
        out_specs=[
            pl.BlockSpec((tm, D_MODEL), lambda i: (cur(i), 0)),
            pl.BlockSpec((ROW_PIECES, tm, PIECE), lambda i: (0, cur(i), 0)),
            pl.BlockSpec((tm, LANES), lambda i: (jnp.maximum(i - 1, 0), 0)),
            pl.BlockSpec((8, tm), lambda i: (0, jnp.maximum(i - 1, 0))),
            pl.BlockSpec((8, LANES), lambda i: (0, 0)),
        ],
        out_shape=[
            jax.ShapeDtypeStruct((n, D_MODEL), jnp.float32),
            jax.ShapeDtypeStruct((ROW_PIECES, n, PIECE), jnp.uint32),
            jax.ShapeDtypeStruct((n, LANES), jnp.float32),
            jax.ShapeDtypeStruct((8, n), jnp.float32),
            jax.ShapeDtypeStruct((8, LANES), jnp.float32),
        ],
        scratch_shapes=[pltpu.VMEM((8, LANES), jnp.float32), pltpu.VMEM((tm, LANES), jnp.float32)],
        compiler_params=pltpu.CompilerParams(dimension_semantics=("arbitrary",),
                                             vmem_limit_bytes=VMEM_LIMIT),
        name="merge",
    )(x2, oa, ob, proj, proj, proj, proj, wa.astype(jnp.bfloat16), wb.astype(jnp.bfloat16),
      wo.astype(jnp.bfloat16), g2.reshape(1, D_MODEL), wr_hi, wr_lo, br, tri)


def _sc_mesh():
    return plsc.VectorSubcoreMesh(core_axis_name="c", subcore_axis_name="s")


def _sc_scatter_rows(src, idx, out_rows, src_block):
    steps = idx.shape[1] // SC_WINDOW
    per_core = steps // SC_CORES

    @pl.kernel(out_type=jax.ShapeDtypeStruct((out_rows, PIECE), src.dtype), mesh=_sc_mesh())
    def scatter(src_hbm, idx_hbm, out_hbm):
        def body(src_vmem, idx_vmem):
            pltpu.sync_copy(src_vmem, out_hbm.at[idx_vmem.at[0]])

        pltpu.emit_pipeline(
            body,
            grid=(SC_CORES, per_core),
            in_specs=[pl.BlockSpec((SC_WINDOW, PIECE), lambda c, i: (src_block(c * per_core + i), 0)),
                      pl.BlockSpec((1, SC_WINDOW), lambda c, i: (0, c * per_core + i))],
            out_specs=[],
            core_axis_name=("c", "s"),
            dimension_semantics=(pltpu.PARALLEL, pltpu.PARALLEL),
        )(src_hbm, idx_hbm)

    return scatter(src, idx)


def _sc_gather_rows(table, idx):
    num = idx.shape[1]
    per_core = num // SC_WINDOW // SC_CORES

    @pl.kernel(out_type=jax.ShapeDtypeStruct((num, PIECE), table.dtype), mesh=_sc_mesh())
    def gather(table_hbm, idx_hbm, out_hbm):
        def body(idx_vmem, out_vmem):
            pltpu.sync_copy(table_hbm.at[idx_vmem.at[0]], out_vmem)

        pltpu.emit_pipeline(
            body,
            grid=(SC_CORES, per_core),
            in_specs=[pl.BlockSpec((1, SC_WINDOW), lambda c, i: (0, c * per_core + i))],
            out_specs=[pl.BlockSpec((SC_WINDOW, PIECE), lambda c, i: (c * per_core + i, 0))],
            core_axis_name=("c", "s"),
            dimension_semantics=(pltpu.PARALLEL, pltpu.PARALLEL),
        )(idx_hbm, out_hbm)

    return gather(table, idx)


def _store_packed(ref, val):
    as_bits = lambda v: lax.bitcast_convert_type(v.astype(jnp.bfloat16).astype(jnp.float32), jnp.uint32)
    words = (as_bits(val[:, :PACKED]) >> 16) | (as_bits(val[:, PACKED:]) & jnp.uint32(0xFFFF0000))
    for j in range(ROW_PIECES):
        ref[j] = words[:, j * PIECE:(j + 1) * PIECE]


def _load_packed(ref):
    words = jnp.concatenate([ref[j] for j in range(ROW_PIECES)], axis=1)
    low = lax.bitcast_convert_type(words << 16, jnp.float32)
    high = lax.bitcast_convert_type(words & jnp.uint32(0xFFFF0000), jnp.float32)
    return jnp.concatenate([low, high], axis=1)


def _expert_kernel(blk_e_ref, n_used_ref, nxt_ref, run_ref, x_ref, wg_hbm, wu_hbm, wd_hbm, o_ref,
                   wg_stage, wu_stage, wd_stage, wg_scr, wu_scr, wd_scr, sem):
    i = pl.program_id(0)
    used = i < n_used_ref[0]

    def weight_copies(e, s):
        return (pltpu.make_async_copy(wg_hbm.at[e], wg_stage.at[s], sem.at[s, 0]),
                pltpu.make_async_copy(wu_hbm.at[e], wu_stage.at[s], sem.at[s, 1]),
                pltpu.make_async_copy(wd_hbm.at[e], wd_stage.at[s], sem.at[s, 2]))

    @pl.when(i == 0)
    def _():
        for c in weight_copies(blk_e_ref[0], 0):
            c.start()

    @pl.when(used & ((i == 0) | (blk_e_ref[i] != blk_e_ref[jnp.maximum(i - 1, 0)])))
    def _():
        s = run_ref[i] % 2
        for c in weight_copies(blk_e_ref[i], s):
            c.wait()
        wg_scr[...] = wg_stage[s].astype(jnp.bfloat16)
        wu_scr[...] = wu_stage[s].astype(jnp.bfloat16)
        wd_scr[...] = wd_stage[s].astype(jnp.bfloat16)

        @pl.when(nxt_ref[i] >= 0)
        def _():
            for c in weight_copies(nxt_ref[i], 1 - s):
                c.start()

    @pl.when(used)
    def _():
        x = _load_packed(x_ref).astype(jnp.bfloat16)
        a = _dot(x, wg_scr[...])
        u = _dot(x, wu_scr[...])
        hmid = (a * _sigmoid(a) * u).astype(jnp.bfloat16)
        _store_packed(o_ref, _dot(hmid, wd_scr[...]))

    @pl.when(jnp.logical_not(used))
    def _():
        o_ref[...] = jnp.zeros(o_ref.shape, o_ref.dtype)


def _experts(xs, blk_expert, n_used, w_gate, w_up, w_down):
    p = xs.shape[1]
    nblk = p // MOE_BLOCK
    idx = jnp.arange(nblk, dtype=jnp.int32)
    starts = (idx < n_used[0]) & ((idx == 0) | (blk_expert != jnp.roll(blk_expert, 1)))
    run = jnp.cumsum(starts.astype(jnp.int32)) - 1
    next_start = lax.cummin(jnp.where(starts, idx, nblk)[::-1])[::-1]
    after = jnp.concatenate([next_start[1:], jnp.full((1,), nblk, jnp.int32)])
    nxt = jnp.where(after < nblk, blk_expert[jnp.minimum(after, nblk - 1)], -1).astype(jnp.int32)
    live = lambda i, be, nu, nx, rn: jnp.minimum(i, nu[0] - 1)
    any_spec = pl.BlockSpec(memory_space=pl.ANY)
    return pl.pallas_call(
        _expert_kernel,
        grid_spec=pltpu.PrefetchScalarGridSpec(
            num_scalar_prefetch=4,
            grid=(nblk,),
            in_specs=[
                pl.BlockSpec((ROW_PIECES, MOE_BLOCK, PIECE), lambda i, be, nu, nx, rn: (0, live(i, be, nu, nx, rn), 0)),
                any_spec, any_spec, any_spec,
            ],
            out_specs=pl.BlockSpec((ROW_PIECES, MOE_BLOCK, PIECE), lambda i, be, nu, nx, rn: (0, i, 0)),
            scratch_shapes=[pltpu.VMEM((2, D_MODEL, EXPERT_FF), jnp.float32),
                            pltpu.VMEM((2, D_MODEL, EXPERT_FF), jnp.float32),
                            pltpu.VMEM((2, EXPERT_FF, D_MODEL), jnp.float32),
                            pltpu.VMEM((D_MODEL, EXPERT_FF), jnp.bfloat16),
                            pltpu.VMEM((D_MODEL, EXPERT_FF), jnp.bfloat16),
                            pltpu.VMEM((EXPERT_FF, D_MODEL), jnp.bfloat16),
                            pltpu.SemaphoreType.DMA((2, 3))],
        ),
        out_shape=jax.ShapeDtypeStruct((ROW_PIECES, p, PIECE), jnp.uint32),
        compiler_params=pltpu.CompilerParams(dimension_semantics=("arbitrary",),
                                             vmem_limit_bytes=VMEM_LIMIT),
        name="experts",
    )(blk_expert, n_used, nxt, run.astype(jnp.int32), xs, w_gate, w_up, w_down)


def _combine_kernel(x1_ref, route_ref, y0_ref, y1_ref, *rest):
    o_ref = rest[-1]
    route = route_ref[...]
    o_ref[...] = x1_ref[...] + route[:, 2:3] * _load_packed(y0_ref) + route[:, 3:4] * _load_packed(y1_ref)


def _combine(x1, yg, route, row0, out_prev):
    n = yg.shape[1] // TOP_K
    tm = min(PROJ_ROWS, n)
    blk0 = row0 // tm
    prev = () if out_prev is None else (out_prev,)
    return pl.pallas_call(
        _combine_kernel,
        grid=(n // tm,),
        in_specs=[
            pl.BlockSpec((tm, D_MODEL), lambda i: (blk0 + i, 0)),
            pl.BlockSpec((tm, LANES), lambda i: (blk0 + i, 0)),
            pl.BlockSpec((ROW_PIECES, tm, PIECE), lambda i: (0, i, 0)),
            pl.BlockSpec((ROW_PIECES, tm, PIECE), lambda i: (0, i + n // tm, 0)),
        ] + [pl.BlockSpec(memory_space=pl.ANY)] * len(prev),
        out_specs=pl.BlockSpec((tm, D_MODEL), lambda i: (blk0 + i, 0)),
        out_shape=jax.ShapeDtypeStruct(x1.shape, jnp.float32),
        input_output_aliases={4: 0} if prev else {},
        compiler_params=pltpu.CompilerParams(dimension_semantics=("arbitrary",),
                                             vmem_limit_bytes=VMEM_LIMIT),
        name="combine",
    )(x1, route, yg, yg, *prev)


def _dispatch_plan(route_cols, counts, n):
    counts = counts[0, :N_EXPERTS].astype(jnp.int32)
    padded = ((counts + MOE_BLOCK - 1) // MOE_BLOCK) * MOE_BLOCK
    seg_end = jnp.cumsum(padded).astype(jnp.int32)
    seg_start = seg_end - padded
    cols = route_cols.astype(jnp.int32)
    e, rank = cols[0:TOP_K], cols[4:4 + TOP_K]
    picked = e[None] == jnp.arange(N_EXPERTS, dtype=jnp.int32)[:, None, None]
    dest = jnp.sum(jnp.where(picked, seg_start[:, None, None], 0), axis=0) + rank
    p = n * TOP_K + N_EXPERTS * MOE_BLOCK
    slot = dest[None] + (jnp.arange(ROW_PIECES, dtype=jnp.int32) * p)[:, None, None]
    blk_start = jnp.arange(p // MOE_BLOCK, dtype=jnp.int32) * MOE_BLOCK
    blk_expert = jnp.sum((seg_end[None, :] <= blk_start[:, None]).astype(jnp.int32), axis=1)
    blk_expert = jnp.minimum(blk_expert, N_EXPERTS - 1)
    n_used = (seg_end[-1] // MOE_BLOCK).reshape(1)
    return slot, blk_expert, n_used, p


def _layer(x, positions, norm1_g, w_in, q_norm_g, k_norm_g, lam4, diff_subln_g, ret_gn_g, ret_gn_b,
           w_branch_a, w_branch_b, w_out, norm2_g, w_gr, b_gr, w_er, b_er, w_gate, w_up, w_down):
    batch, seq, _ = x.shape
    n = batch * seq
    x2 = x.reshape(n, D_MODEL)
    proj = _in_proj(x2, positions.reshape(1, n), norm1_g, w_in, q_norm_g, k_norm_g)
    oa = _diff_attn(proj, lam4, diff_subln_g.reshape(1, DA_VDIM), batch, seq)
    ob = _retention(proj, ret_gn_g, ret_gn_b, batch, seq)
    x1, h2, route, route_cols, counts = _merge(x2, oa, ob, proj, w_branch_a, w_branch_b, w_out, norm2_g,
                                               w_gr, b_gr, w_er, b_er)
    slot, blk_expert, n_used, p = _dispatch_plan(route_cols, counts, n)
    win_n = n // SC_WINDOW
    src_block = lambda s: (s // (TOP_K * win_n)) * win_n + s % win_n
    xs = _sc_scatter_rows(h2.reshape(ROW_PIECES * n, PIECE), slot.reshape(1, -1), ROW_PIECES * p, src_block)
    ys = _experts(xs.reshape(ROW_PIECES, p, PIECE), blk_expert, n_used, w_gate, w_up, w_down)
    parts = COMBINE_PARTS if n % (COMBINE_PARTS * PROJ_ROWS) == 0 else 1
    m = n // parts
    out = None
    for t in range(parts):
        yg = _sc_gather_rows(ys.reshape(ROW_PIECES * p, PIECE), slot[:, :, t * m:(t + 1) * m].reshape(1, -1))
        out = _combine(x1, yg.reshape(ROW_PIECES, TOP_K * m, PIECE), route, t * m, out)
    return out.reshape(batch, seq, D_MODEL)


def kernel(x, positions, norm1_g, w_in, q_norm_g, k_norm_g, lambda_q1, lambda_k1, lambda_q2, lambda_k2, diff_subln_g, ret_gn_g, ret_gn_b, w_branch_a, w_branch_b, w_out, norm2_g, w_group_router, b_group_router, w_expert_router, b_expert_router, w_gate, w_up, w_down):
    assert x.shape[-1] == D_MODEL and norm1_g.shape[0] == 1, "single-layer, D_MODEL-wide input expected"
    lam4 = jnp.pad(jnp.stack([lambda_q1[0], lambda_k1[0], lambda_q2[0], lambda_k2[0]]), ((0, 0), (0, LANES - DA_HALF)))
    return _layer(x, positions, norm1_g[0], w_in[0], q_norm_g[0], k_norm_g[0], lam4, diff_subln_g[0],
                  ret_gn_g[0], ret_gn_b[0], w_branch_a[0], w_branch_b[0], w_out[0], norm2_g[0],
                  w_group_router[0], b_group_router[0], w_expert_router[0], b_expert_router[0],
                  w_gate[0], w_up[0], w_down[0])
```

```python
import functools
import math

import jax
import jax.numpy as jnp
from jax import lax
from jax.experimental import pallas as pl
from jax.experimental.pallas import tpu as pltpu
from jax.experimental.pallas import tpu_sc as plsc

D_MODEL = 1024
DA_HEADS = 4
DA_HALF = 64
DA_VDIM = 2 * DA_HALF
DA_WIDTH = DA_HEADS * DA_VDIM
ROPE_THETA = 500000.0
ROPE_DIM = DA_HALF // 4
RET_HEADS = 4
RET_KDIM = 128
RET_VDIM = 128
RET_WIDTH = RET_HEADS * RET_VDIM
RET_THETA = 10000.0
N_GROUPS = 4
EXPERTS_PER_GROUP = 8
N_EXPERTS = N_GROUPS * EXPERTS_PER_GROUP
TOP_K = 2
EXPERT_FF = 512
EPS = 1e-6
LAMBDA_INIT = 0.8 - 0.6 * math.exp(-0.3 * 0)

LANES = 128
IN_COLS = 3 * DA_WIDTH + 4 * RET_WIDTH + 2 * D_MODEL
COL_QA, COL_KA, COL_VA = 0, DA_WIDTH, 2 * DA_WIDTH
COL_QR = 3 * DA_WIDTH
COL_KR = COL_QR + RET_WIDTH
COL_VR = COL_KR + RET_WIDTH
COL_GB = COL_VR + RET_WIDTH
COL_GATE_A = COL_GB + RET_WIDTH
COL_GATE_B = COL_GATE_A + D_MODEL

PROJ_ROWS = 512
PROJ_CHUNK = 256
ATT_TILE = 512
ATT_ROWS = 32
RET_CHUNK = 256
RET_UNROLL = 16
MOE_BLOCK = 512
PACKED = D_MODEL // 2
ROW_PIECES = 2
PIECE = PACKED // ROW_PIECES
SC_CORES = 2
SC_WINDOW = 128
COMBINE_PARTS = 2
VMEM_LIMIT = 56 * 1024 * 1024


def _dot(a, b):
    return jnp.dot(a, b, preferred_element_type=jnp.float32)


def _dot_nt(a, b):
    return lax.dot_general(a, b, (((1,), (1,)), ((), ())), preferred_element_type=jnp.float32)


def _dot_tn(a, b):
    return lax.dot_general(a, b, (((0,), (0,)), ((), ())), preferred_element_type=jnp.float32)


def _sigmoid(x):
    return 0.5 * jnp.tanh(0.5 * x) + 0.5


def _split3(x):
    a = x.astype(jnp.bfloat16)
    r = x - a.astype(jnp.float32)
    b = r.astype(jnp.bfloat16)
    c = (r - b.astype(jnp.float32)).astype(jnp.bfloat16)
    return a, b, c


def _in_proj_kernel(x_ref, pos_ref, g1_ref, w_ref, gsum_ref, gq_ref, gk_ref, fa_ref, fr_ref, sel_ref,
                    o_ref, h_scr):
    x = x_ref[...]
    rows = x.shape[0]
    h_scr[...] = (x * g1_ref[...]).astype(jnp.bfloat16)
    rms_scale = jnp.broadcast_to(lax.rsqrt(jnp.mean(x * x, axis=-1, keepdims=True) + EPS), (rows, PROJ_CHUNK))
    pos = pos_ref[...].astype(jnp.float32)

    lane = lax.broadcasted_iota(jnp.int32, (rows, LANES), 1)
    half_a = ROPE_DIM // 2
    tables = {}

    def da_tables():
        if "da" not in tables:
            ang_a = fa_ref[...] * pos
            pad = jnp.zeros((LANES - 2 * half_a, rows), jnp.float32)
            t_a = jnp.concatenate([jnp.cos(ang_a), jnp.sin(ang_a), pad], axis=0).T
            tab = sum(_dot(part, sel_ref[...]) for part in _split3(t_a))
            c_a = tab[:, :LANES] + jnp.where(lane % DA_HALF < ROPE_DIM, 0.0, 1.0)
            s_lo = tab[:, LANES:2 * LANES]
            s_hi = tab[:, 2 * LANES:]
            tables["da"] = tuple(jnp.concatenate([v, v], axis=1) for v in (c_a, s_lo, s_hi))
        return tables["da"]

    def ret_tables():
        if "ret" not in tables:
            ang_r = fr_ref[...] * pos
            t_r = jnp.concatenate([jnp.cos(ang_r), jnp.sin(ang_r)], axis=0).T
            sw_r = pltpu.roll(t_r, RET_KDIM // 2, axis=1)
            first = lane < RET_KDIM // 2
            c_r = jnp.where(first, t_r, sw_r)
            s_r = jnp.where(first, -sw_r, t_r)
            tables["ret"] = tuple(jnp.concatenate([v, v], axis=1) for v in (c_r, s_r))
        return tables["ret"]

    def qk_norm_rope(y, g, scale):
        c_a2, s_lo2, s_hi2 = da_tables()
        ss = y * y
        hi = ss.astype(jnp.bfloat16)
        lo = (ss - hi.astype(jnp.float32)).astype(jnp.bfloat16)
        gs = _dot(hi, gsum_ref[...]) + _dot(lo, gsum_ref[...])
        n = y * lax.rsqrt(gs * (1.0 / DA_HALF) + EPS) * g
        up = pltpu.roll(n, PROJ_CHUNK - half_a, axis=1)
        dn = pltpu.roll(n, half_a, axis=1)
        r = n * c_a2 + up * s_lo2 + dn * s_hi2
        return r * scale if scale != 1.0 else r

    def ret_rope(y, scale):
        c_r2, s_r2 = ret_tables()
        halves = [pltpu.roll(y[:, i * LANES:(i + 1) * LANES], RET_KDIM // 2, axis=1)
                  for i in range(PROJ_CHUNK // LANES)]
        sw = jnp.concatenate(halves, axis=1)
        r = y * c_r2 + sw * s_r2
        return r * scale if scale != 1.0 else r

    n_chunks = IN_COLS // PROJ_CHUNK
    is_long = lambda c: c * PROJ_CHUNK < COL_VA or COL_QR <= c * PROJ_CHUNK < COL_VR
    long_chunks = [c for c in range(n_chunks) if is_long(c)]
    short_chunks = [c for c in range(n_chunks) if not is_long(c)][::-1]
    order = [short_chunks.pop(0) for _ in range(3)]
    while long_chunks or short_chunks:
        if long_chunks:
            order.append(long_chunks.pop(0))
        if short_chunks:
            order.append(short_chunks.pop(0))
    for c in order:
        c0 = c * PROJ_CHUNK
        y = _dot(h_scr[...], w_ref[:, c0:c0 + PROJ_CHUNK]) * rms_scale
        if c0 < COL_KA:
            y = qk_norm_rope(y, gq_ref[...], DA_HALF ** -0.5)
        elif c0 < COL_VA:
            y = qk_norm_rope(y, gk_ref[...], 1.0)
        elif c0 < COL_QR:
            pass
        elif c0 < COL_KR:
            y = ret_rope(y, 1.0)
        elif c0 < COL_VR:
            y = ret_rope(y, RET_KDIM ** -0.5)
        elif c0 < COL_GB:
            pass
        elif c0 < COL_GATE_A:
            y = y * _sigmoid(y)
        else:
            y = _sigmoid(y)
        o_ref[:, c0:c0 + PROJ_CHUNK] = y.astype(o_ref.dtype)


def _in_proj(x2, pos2, g1, w_in, gq, gk):
    n = x2.shape[0]
    tm = min(PROJ_ROWS, n)
    grp = jnp.arange(PROJ_CHUNK) // DA_HALF
    gsum = (grp[:, None] == grp[None, :]).astype(jnp.bfloat16)
    half_a = ROPE_DIM // 2
    fa = jnp.power(jnp.float32(ROPE_THETA), -2.0 * jnp.arange(half_a, dtype=jnp.float32) / ROPE_DIM)[:, None]
    half_r = RET_KDIM // 2
    fr = jnp.power(jnp.float32(RET_THETA), -2.0 * jnp.arange(half_r, dtype=jnp.float32) / RET_KDIM)[:, None]
    j = jnp.arange(LANES)[:, None]
    l64 = (jnp.arange(LANES) % DA_HALF)[None, :]
    sel_c = (j < half_a) & (l64 < ROPE_DIM) & (l64 % half_a == j)
    sel_lo = (j >= half_a) & (j < ROPE_DIM) & (l64 < half_a) & (l64 == j - half_a)
    sel_hi = (j >= half_a) & (j < ROPE_DIM) & (l64 >= half_a) & (l64 < ROPE_DIM) & (l64 == j)
    sel = jnp.concatenate([sel_c.astype(jnp.float32), -sel_lo.astype(jnp.float32),
                           sel_hi.astype(jnp.float32)], axis=1).astype(jnp.bfloat16)
    reps = PROJ_CHUNK // DA_HALF
    full = lambda shape: pl.BlockSpec(shape, lambda i: (0,) * len(shape))
    return pl.pallas_call(
        _in_proj_kernel,
        grid=(n // tm,),
        in_specs=[
            pl.BlockSpec((tm, D_MODEL), lambda i: (i, 0)),
            pl.BlockSpec((1, tm), lambda i: (0, i)),
            full((1, D_MODEL)),
            full((D_MODEL, IN_COLS)),
            full((PROJ_CHUNK, PROJ_CHUNK)),
            full((1, PROJ_CHUNK)),
            full((1, PROJ_CHUNK)),
            full((half_a, 1)),
            full((half_r, 1)),
            full((LANES, 3 * LANES)),
        ],
        out_specs=pl.BlockSpec((tm, IN_COLS), lambda i: (i, 0)),
        out_shape=jax.ShapeDtypeStruct((n, IN_COLS), jnp.bfloat16),
        scratch_shapes=[pltpu.VMEM((tm, D_MODEL), jnp.bfloat16)],
        compiler_params=pltpu.CompilerParams(dimension_semantics=("arbitrary",),
                                             vmem_limit_bytes=VMEM_LIMIT),
        name="in_proj",
    )(x2, pos2, g1.reshape(1, D_MODEL), w_in.astype(jnp.bfloat16), gsum,
      jnp.tile(gq, reps)[None, :], jnp.tile(gk, reps)[None, :], fa, fr, sel)


def _diff_attn_kernel(q_ref, k_ref, v_ref, lam_ref, gsub_ref, o_ref,
                      qs_scr, vx_scr, s0_scr, s1_scr, p_scr, m_scr, alpha_scr, acc_scr):
    i = pl.program_id(2)
    t = q_ref.shape[0]

    @pl.when(i == 0)
    def _():
        vx_scr[:, :DA_VDIM] = v_ref[...]
        vx_scr[:, DA_VDIM:] = jnp.ones((vx_scr.shape[0], LANES), vx_scr.dtype)

    q = q_ref[...]
    lane = lax.broadcasted_iota(jnp.int32, q.shape, 1)
    zero = jnp.zeros_like(q)
    qs_scr[:t] = jnp.where(lane < DA_HALF, q, zero)
    qs_scr[t:] = jnp.where(lane >= DA_HALF, q, zero)

    def scores(j, s_ref):
        start = pl.multiple_of(j * t, t)
        s_ref[...] = _dot_nt(qs_scr[...], k_ref[pl.ds(start, t), :])

    def softmax_pv(j, s_ref, masked, first=False):
        for c in range(2 * t // ATT_ROWS):
            rows = pl.ds(c * ATT_ROWS, ATT_ROWS)
            s = s_ref[rows, :]
            if masked:
                r = lax.broadcasted_iota(jnp.int32, s.shape, 0) + (c * ATT_ROWS) % t
                col = lax.broadcasted_iota(jnp.int32, s.shape, 1)
                s = jnp.where(col <= r, s, -jnp.inf)
            m_cur = jnp.max(s, axis=-1, keepdims=True)
            if first:
                m_new = jnp.broadcast_to(m_cur, (ATT_ROWS, LANES))
            else:
                m_prev = m_scr[rows, :]
                m_new = jnp.maximum(m_prev, m_cur)
                alpha_scr[rows, :] = jnp.exp(m_prev - m_new)
            m_scr[rows, :] = m_new
            p = jnp.exp(s - jnp.concatenate([m_new] * (t // LANES), axis=1))
            p_scr[rows, :] = p.astype(p_scr.dtype)
        start = pl.multiple_of(j * t, t)
        pv = _dot(p_scr[...], vx_scr[pl.ds(start, t), :])
        if first:
            acc_scr[...] = pv
            return
        alpha = alpha_scr[...]
        for half in range(2):
            cols = pl.ds(half * LANES, LANES)
            acc_scr[:, cols] = alpha * acc_scr[:, cols] + pv[:, half * LANES:(half + 1) * LANES]

    scores(0, s0_scr)

    @pl.when(i == 0)
    def _():
        softmax_pv(0, s0_scr, True, first=True)

    @pl.when(i == 1)
    def _():
        scores(1, s1_scr)
        softmax_pv(0, s0_scr, False, first=True)
        softmax_pv(1, s1_scr, True)

    @pl.when(i >= 2)
    def _():
        scores(1, s1_scr)
        softmax_pv(0, s0_scr, False, first=True)
        scores(2, s0_scr)
        softmax_pv(1, s1_scr, False)

        def pair(jj, carry):
            j = 2 * jj
            scores(j + 1, s1_scr)
            softmax_pv(j, s0_scr, False)
            scores(j + 2, s0_scr)
            softmax_pv(j + 1, s1_scr, False)
            return carry

        lax.fori_loop(1, i // 2, pair, 0)

        @pl.when(i % 2 == 1)
        def _():
            scores(i, s1_scr)
            softmax_pv(i - 1, s0_scr, False)
            softmax_pv(i, s1_scr, True)

        @pl.when(i % 2 == 0)
        def _():
            softmax_pv(i, s0_scr, True)

    lam4 = lam_ref[...]
    lam = (jnp.exp(jnp.sum(lam4[0:1] * lam4[1:2], axis=-1, keepdims=True))
           - jnp.exp(jnp.sum(lam4[2:3] * lam4[3:4], axis=-1, keepdims=True)) + LAMBDA_INIT)
    o_all = acc_scr[:, :DA_VDIM] / acc_scr[:, DA_VDIM:]
    o = o_all[:t] - lam * o_all[t:]
    o = o * lax.rsqrt(jnp.mean(o * o, axis=-1, keepdims=True) + EPS) * gsub_ref[...] * (1.0 - LAMBDA_INIT)
    o_ref[...] = o.astype(o_ref.dtype)


def _diff_attn(proj, lam4, gsub, batch, seq):
    n = proj.shape[0]
    t = min(ATT_TILE, seq)
    nq = seq // t
    qb, kb, vb = COL_QA // LANES, COL_KA // LANES, COL_VA // LANES
    return pl.pallas_call(
        _diff_attn_kernel,
        grid=(batch, DA_HEADS, nq),
        in_specs=[
            pl.BlockSpec((t, LANES), lambda b, h, i: (b * nq + i, qb + h)),
            pl.BlockSpec((seq, LANES), lambda b, h, i: (b, kb + h)),
            pl.BlockSpec((seq, LANES), lambda b, h, i: (b, vb + h)),
            pl.BlockSpec((4, LANES), lambda b, h, i: (0, 0)),
            pl.BlockSpec((1, LANES), lambda b, h, i: (0, 0)),
        ],
        out_specs=pl.BlockSpec((t, LANES), lambda b, h, i: (b * nq + i, h)),
        out_shape=jax.ShapeDtypeStruct((n, DA_WIDTH), jnp.bfloat16),
        scratch_shapes=[pltpu.VMEM((2 * t, LANES), jnp.bfloat16),
                        pltpu.VMEM((seq, DA_VDIM + LANES), jnp.bfloat16),
                        pltpu.VMEM((2 * t, t), jnp.float32),
                        pltpu.VMEM((2 * t, t), jnp.float32),
                        pltpu.VMEM((2 * t, t), jnp.bfloat16),
                        pltpu.VMEM((2 * t, LANES), jnp.float32),
                        pltpu.VMEM((2 * t, LANES), jnp.float32),
                        pltpu.VMEM((2 * t, DA_VDIM + LANES), jnp.float32)],
        compiler_params=pltpu.CompilerParams(dimension_semantics=("arbitrary",) * 3,
                                             vmem_limit_bytes=VMEM_LIMIT),
        name="diff_attn",
    )(proj, proj, proj, lam4, gsub)


def _retention_kernel(q_ref, k_ref, v_ref, g_ref, gng_ref, gnb_ref, o_ref, r_scr, *, chunk):
    hf = jnp.full((1, 1), pl.program_id(1), jnp.int32).astype(jnp.float32)
    log_g = jnp.log1p(-jnp.exp2(-5.0 - hf))
    ri = lax.broadcasted_iota(jnp.int32, (chunk, chunk), 0)
    ci = lax.broadcasted_iota(jnp.int32, (chunk, chunk), 1)
    rel = (ri - ci).astype(jnp.float32)
    dmask = jnp.where(rel >= 0, jnp.exp(jnp.maximum(rel, 0.0) * log_g), 0.0)
    idx = lax.broadcasted_iota(jnp.int32, (chunk, 1), 0).astype(jnp.float32)
    zeta = jnp.exp((chunk - 1 - idx) * log_g)
    xi = jnp.exp((idx + 1.0) * log_g)
    g_chunk = jnp.exp(chunk * log_g)
    r_scr[...] = jnp.zeros(r_scr.shape, jnp.float32)
    gng = gng_ref[...]
    gnb = gnb_ref[...]

    def body(n, carry):
        start = pl.multiple_of(n * chunk, chunk)
        q = q_ref[pl.ds(start, chunk), :]
        k = k_ref[pl.ds(start, chunk), :]
        v = v_ref[pl.ds(start, chunk), :]
        s = _dot_nt(q, k) * dmask
        r_old = r_scr[...]
        o = _dot(s.astype(jnp.bfloat16), v) + xi * _dot(q, r_old.astype(jnp.bfloat16))
        kz = (k.astype(jnp.float32) * zeta).astype(jnp.bfloat16)
        r_scr[...] = g_chunk * r_old + _dot_tn(kz, v)
        mu = jnp.mean(o, axis=-1, keepdims=True)
        d = o - mu
        var = jnp.mean(d * d, axis=-1, keepdims=True)
        y = d * lax.rsqrt(var + EPS) * gng + gnb
        y = y * g_ref[pl.ds(start, chunk), :].astype(jnp.float32)
        o_ref[pl.ds(start, chunk), :] = y.astype(o_ref.dtype)
        return carry

    lax.fori_loop(0, q_ref.shape[0] // chunk, body, 0, unroll=RET_UNROLL)


def _retention(proj, gn_g, gn_b, batch, seq):
    n = proj.shape[0]
    chunk = min(RET_CHUNK, seq)
    col = lambda c0: (lambda b, h: (b, c0 // LANES + h))
    return pl.pallas_call(
        functools.partial(_retention_kernel, chunk=chunk),
        grid=(batch, RET_HEADS),
        in_specs=[
            pl.BlockSpec((seq, LANES), col(COL_QR)),
            pl.BlockSpec((seq, LANES), col(COL_KR)),
            pl.BlockSpec((seq, LANES), col(COL_VR)),
            pl.BlockSpec((seq, LANES), col(COL_GB)),
            pl.BlockSpec((1, LANES), lambda b, h: (0, h)),
            pl.BlockSpec((1, LANES), lambda b, h: (0, h)),
        ],
        out_specs=pl.BlockSpec((seq, LANES), lambda b, h: (b, h)),
        out_shape=jax.ShapeDtypeStruct((n, RET_WIDTH), jnp.bfloat16),
        scratch_shapes=[pltpu.VMEM((RET_KDIM, RET_VDIM), jnp.float32)],
        compiler_params=pltpu.CompilerParams(dimension_semantics=("arbitrary",) * 2,
                                             vmem_limit_bytes=VMEM_LIMIT),
        name="retention",
    )(proj, proj, proj, proj, gn_g.reshape(1, RET_WIDTH), gn_b.reshape(1, RET_WIDTH))


def _merge_kernel(x_ref, oa_ref, ob_ref, sa0_ref, sa1_ref, sb0_ref, sb1_ref, wa_ref, wb_ref, wo_ref,
                  g2_ref, wr_hi_ref, wr_lo_ref, br_ref, tri_ref, x1_ref, h2_ref, route_ref, cols_ref, counts_ref,
                  base_scr, logits_scr):
    i = pl.program_id(0)

    @pl.when(i == 0)
    def _():
        base_scr[...] = jnp.zeros(base_scr.shape, jnp.float32)
        logits_scr[...] = jnp.zeros(logits_scr.shape, jnp.float32)

    ya = _dot(oa_ref[...], wa_ref[...])
    yb = _dot(ob_ref[...], wb_ref[...])

    logits = logits_scr[...]
    lane = lax.broadcasted_iota(jnp.int32, logits.shape, 1)
    neg = -jnp.inf
    gl = jnp.where(lane < N_GROUPS, logits, neg)
    gmax = jnp.max(gl, axis=-1, keepdims=True)
    g_idx = jnp.min(jnp.where(gl == gmax, lane, LANES), axis=-1, keepdims=True)
    p_g = 1.0 / jnp.sum(jnp.exp(gl - gmax), axis=-1, keepdims=True)
    e_lo = N_GROUPS + EXPERTS_PER_GROUP * g_idx
    el = jnp.where((lane >= e_lo) & (lane < e_lo + EXPERTS_PER_GROUP), logits, neg)
    v1 = jnp.max(el, axis=-1, keepdims=True)
    i1 = jnp.min(jnp.where(el == v1, lane, LANES), axis=-1, keepdims=True)
    el2 = jnp.where(lane == i1, neg, el)
    v2 = jnp.max(el2, axis=-1, keepdims=True)
    i2 = jnp.min(jnp.where(el2 == v2, lane, LANES), axis=-1, keepdims=True)
    t = jnp.exp(v2 - v1)
    w1 = p_g / (1.0 + t)
    w2 = p_g * t / (1.0 + t)
    e1 = i1 - N_GROUPS
    e2 = i2 - N_GROUPS

    oh1 = lane == e1
    oh2 = lane == e2
    real = jnp.where(i > 0, 1.0, 0.0)
    picked = jnp.where(oh1 | oh2, real, 0.0)
    before = _dot(tri_ref[...], picked.astype(jnp.bfloat16)) + base_scr[0:1, :]
    rank1 = jnp.sum(jnp.where(oh1, before, 0.0), axis=-1, keepdims=True)
    rank2 = jnp.sum(jnp.where(oh2, before, 0.0), axis=-1, keepdims=True)
    base_scr[...] = base_scr[...] + jnp.sum(picked, axis=0, keepdims=True)
    counts_ref[...] = base_scr[...]

    cols = [e1.astype(jnp.float32), e2.astype(jnp.float32), w1, w2, rank1, rank2]
    route = jnp.zeros(logits.shape, jnp.float32)
    for c, val in enumerate(cols):
        route = jnp.where(lane == c, val, route)
    route_ref[...] = route
    cols_ref[...] = route.T[:8]

    sa = jnp.concatenate([sa0_ref[...], sa1_ref[...]], axis=1).astype(jnp.float32)
    sb = jnp.concatenate([sb0_ref[...], sb1_ref[...]], axis=1).astype(jnp.float32)
    merged = sa * ya + sb * yb
    x1 = x_ref[...] + _dot(merged.astype(jnp.bfloat16), wo_ref[...])
    x1_ref[...] = x1
    h2 = x1 * lax.rsqrt(jnp.mean(x1 * x1, axis=-1, keepdims=True) + EPS) * g2_ref[...]
    _store_packed(h2_ref, h2)

    hi = h2.astype(jnp.bfloat16)
    lo = (h2 - hi.astype(jnp.float32)).astype(jnp.bfloat16)
    logits_scr[...] = (_dot(hi, wr_hi_ref[...]) + _dot(lo, wr_hi_ref[...]) + _dot(hi, wr_lo_ref[...])
                       + br_ref[...])


def _merge(x2, oa, ob, proj, wa, wb, wo, g2, w_gr, b_gr, w_er, b_er):
    n = x2.shape[0]
    tm = min(PROJ_ROWS, n)
    half = D_MODEL // 2
    pad = LANES - N_GROUPS - N_EXPERTS
    wr = jnp.concatenate([w_gr, w_er, jnp.zeros((D_MODEL, pad), jnp.float32)], axis=1)
    wr_hi = wr.astype(jnp.bfloat16)
    wr_lo = (wr - wr_hi.astype(jnp.float32)).astype(jnp.bfloat16)
    br = jnp.concatenate([b_gr, b_er, jnp.zeros((pad,), jnp.float32)])[None, :]
    tri = (jnp.arange(tm)[:, None] > jnp.arange(tm)[None, :]).astype(jnp.bfloat16)
    full = lambda shape: pl.BlockSpec(shape, lambda i: (0,) * len(shape))
    nt = n // tm
    cur = lambda i: jnp.minimum(i, nt - 1)
    gate = lambda c0: pl.BlockSpec((tm, half), lambda i: (cur(i), c0 // half))
    return pl.pallas_call(
        _merge_kernel,
        grid=(nt + 1,),
        in_specs=[
            pl.BlockSpec((tm, D_MODEL), lambda i: (cur(i), 0)),
            pl.BlockSpec((tm, DA_WIDTH), lambda i: (cur(i), 0)),
            pl.BlockSpec((tm, RET_WIDTH), lambda i: (cur(i), 0)),
            gate(COL_GATE_A), gate(COL_GATE_A + half), gate(COL_GATE_B), gate(COL_GATE_B + half),
            full((DA_WIDTH, D_MODEL)), full((RET_WIDTH, D_MODEL)), full((D_MODEL, D_MODEL)),
            full((1, D_MODEL)), full((D_MODEL, LANES)), full((D_MODEL, LANES)), full((1, LANES)),
            full((tm, tm)),
        ],
        out_specs=[
            pl.BlockSpec((tm, D_MODEL), lambda i: (cur(i), 0)),
            pl.BlockSpec((ROW_PIECES, tm, PIECE), lambda i: (0, cur(i), 0)),
            pl.BlockSpec((tm, LANES), lambda i: (jnp.maximum(i - 1, 0), 0)),
            pl.BlockSpec((8, tm), lambda i: (0, jnp.maximum(i - 1, 0))),
            pl.BlockSpec((8, LANES), lambda i: (0, 0)),
        ],
        out_shape=[
            jax.ShapeDtypeStruct((n, D_MODEL), jnp.float32),
            jax.ShapeDtypeStruct((ROW_PIECES, n, PIECE), jnp.uint32),
            jax.ShapeDtypeStruct((n, LANES), jnp.float32),
            jax.ShapeDtypeStruct((8, n), jnp.float32),
            jax.ShapeDtypeStruct((8, LANES), jnp.float32),
        ],
        scratch_shapes=[pltpu.VMEM((8, LANES), jnp.float32), pltpu.VMEM((tm, LANES), jnp.float32)],
        compiler_params=pltpu.CompilerParams(dimension_semantics=("arbitrary",),
                                             vmem_limit_bytes=VMEM_LIMIT),
        name="merge",
    )(x2, oa, ob, proj, proj, proj, proj, wa.astype(jnp.bfloat16), wb.astype(jnp.bfloat16),
      wo.astype(jnp.bfloat16), g2.reshape(1, D_MODEL), wr_hi, wr_lo, br, tri)


def _sc_mesh():
    return plsc.VectorSubcoreMesh(core_axis_name="c", subcore_axis_name="s")


def _sc_scatter_rows(src, idx, out_rows, src_block):
    steps = idx.shape[1] // SC_WINDOW
    per_core = steps // SC_CORES

    @pl.kernel(out_type=jax.ShapeDtypeStruct((out_rows, PIECE), src.dtype), mesh=_sc_mesh())
    def scatter(src_hbm, idx_hbm, out_hbm):
        def body(src_vmem, idx_vmem):
            pltpu.sync_copy(src_vmem, out_hbm.at[idx_vmem.at[0]])

        pltpu.emit_pipeline(
            body,
            grid=(SC_CORES, per_core),
            in_specs=[pl.BlockSpec((SC_WINDOW, PIECE), lambda c, i: (src_block(c * per_core + i), 0)),
                      pl.BlockSpec((1, SC_WINDOW), lambda c, i: (0, c * per_core + i))],
            out_specs=[],
            core_axis_name=("c", "s"),
            dimension_semantics=(pltpu.PARALLEL, pltpu.PARALLEL),
        )(src_hbm, idx_hbm)

    return scatter(src, idx)


def _sc_gather_rows(table, idx):
    num = idx.shape[1]
    per_core = num // SC_WINDOW // SC_CORES

    @pl.kernel(out_type=jax.ShapeDtypeStruct((num, PIECE), table.dtype), mesh=_sc_mesh())
    def gather(table_hbm, idx_hbm, out_hbm):
        def body(idx_vmem, out_vmem):
            pltpu.sync_copy(table_hbm.at[idx_vmem.at[0]], out_vmem)

        pltpu.emit_pipeline(
            body,
            grid=(SC_CORES, per_core),
            in_specs=[pl.BlockSpec((1, SC_WINDOW), lambda c, i: (0, c * per_core + i))],
            out_specs=[pl.BlockSpec((SC_WINDOW, PIECE), lambda c, i: (c * per_core + i, 0))],
            core_axis_name=("c", "s"),
            dimension_semantics=(pltpu.PARALLEL, pltpu.PARALLEL),
        )(idx_hbm, out_hbm)

    return gather(table, idx)


def _store_packed(ref, val):
    as_bits = lambda v: lax.bitcast_convert_type(v.astype(jnp.bfloat16).astype(jnp.float32), jnp.uint32)
    words = (as_bits(val[:, :PACKED]) >> 16) | (as_bits(val[:, PACKED:]) & jnp.uint32(0xFFFF0000))
    for j in range(ROW_PIECES):
        ref[j] = words[:, j * PIECE:(j + 1) * PIECE]


def _load_packed(ref):
    words = jnp.concatenate([ref[j] for j in range(ROW_PIECES)], axis=1)
    low = lax.bitcast_convert_type(words << 16, jnp.float32)
    high = lax.bitcast_convert_type(words & jnp.uint32(0xFFFF0000), jnp.float32)
    return jnp.concatenate([low, high], axis=1)


def _expert_kernel(blk_e_ref, n_used_ref, nxt_ref, run_ref, x_ref, wg_hbm, wu_hbm, wd_hbm, o_ref,
                   wg_stage, wu_stage, wd_stage, wg_scr, wu_scr, wd_scr, sem):
    i = pl.program_id(0)
    used = i < n_used_ref[0]

    def weight_copies(e, s):
        return (pltpu.make_async_copy(wg_hbm.at[e], wg_stage.at[s], sem.at[s, 0]),
                pltpu.make_async_copy(wu_hbm.at[e], wu_stage.at[s], sem.at[s, 1]),
                pltpu.make_async_copy(wd_hbm.at[e], wd_stage.at[s], sem.at[s, 2]))

    @pl.when(i == 0)
    def _():
        for c in weight_copies(blk_e_ref[0], 0):
            c.start()

    @pl.when(used & ((i == 0) | (blk_e_ref[i] != blk_e_ref[jnp.maximum(i - 1, 0)])))
    def _():
        s = run_ref[i] % 2
        for c in weight_copies(blk_e_ref[i], s):
            c.wait()
        wg_scr[...] = wg_stage[s].astype(jnp.bfloat16)
        wu_scr[...] = wu_stage[s].astype(jnp.bfloat16)
        wd_scr[...] = wd_stage[s].astype(jnp.bfloat16)

        @pl.when(nxt_ref[i] >= 0)
        def _():
            for c in weight_copies(nxt_ref[i], 1 - s):
                c.start()

    @pl.when(used)
    def _():
        x = _load_packed(x_ref).astype(jnp.bfloat16)
        a = _dot(x, wg_scr[...])
        u = _dot(x, wu_scr[...])
        hmid = (a * _sigmoid(a) * u).astype(jnp.bfloat16)
        _store_packed(o_ref, _dot(hmid, wd_scr[...]))

    @pl.when(jnp.logical_not(used))
    def _():
        o_ref[...] = jnp.zeros(o_ref.shape, o_ref.dtype)


def _experts(xs, blk_expert, n_used, w_gate, w_up, w_down):
    p = xs.shape[1]
    nblk = p // MOE_BLOCK
    idx = jnp.arange(nblk, dtype=jnp.int32)
    starts = (idx < n_used[0]) & ((idx == 0) | (blk_expert != jnp.roll(blk_expert, 1)))
    run = jnp.cumsum(starts.astype(jnp.int32)) - 1
    next_start = lax.cummin(jnp.where(starts, idx, nblk)[::-1])[::-1]
    after = jnp.concatenate([next_start[1:], jnp.full((1,), nblk, jnp.int32)])
    nxt = jnp.where(after < nblk, blk_expert[jnp.minimum(after, nblk - 1)], -1).astype(jnp.int32)
    live = lambda i, be, nu, nx, rn: jnp.minimum(i, nu[0] - 1)
    any_spec = pl.BlockSpec(memory_space=pl.ANY)
    return pl.pallas_call(
        _expert_kernel,
        grid_spec=pltpu.PrefetchScalarGridSpec(
            num_scalar_prefetch=4,
            grid=(nblk,),
            in_specs=[
                pl.BlockSpec((ROW_PIECES, MOE_BLOCK, PIECE), lambda i, be, nu, nx, rn: (0, live(i, be, nu, nx, rn), 0)),
                any_spec, any_spec, any_spec,
            ],
            out_specs=pl.BlockSpec((ROW_PIECES, MOE_BLOCK, PIECE), lambda i, be, nu, nx, rn: (0, i, 0)),
            scratch_shapes=[pltpu.VMEM((2, D_MODEL, EXPERT_FF), jnp.float32),
                            pltpu.VMEM((2, D_MODEL, EXPERT_FF), jnp.float32),
                            pltpu.VMEM((2, EXPERT_FF, D_MODEL), jnp.float32),
                            pltpu.VMEM((D_MODEL, EXPERT_FF), jnp.bfloat16),
                            pltpu.VMEM((D_MODEL, EXPERT_FF), jnp.bfloat16),
                            pltpu.VMEM((EXPERT_FF, D_MODEL), jnp.bfloat16),
                            pltpu.SemaphoreType.DMA((2, 3))],
        ),
        out_shape=jax.ShapeDtypeStruct((ROW_PIECES, p, PIECE), jnp.uint32),
        compiler_params=pltpu.CompilerParams(dimension_semantics=("arbitrary",),
                                             vmem_limit_bytes=VMEM_LIMIT),
        name="experts",
    )(blk_expert, n_used, nxt, run.astype(jnp.int32), xs, w_gate, w_up, w_down)


def _combine_kernel(x1_ref, route_ref, y0_ref, y1_ref, *rest):
    o_ref = rest[-1]
    route = route_ref[...]
    o_ref[...] = x1_ref[...] + route[:, 2:3] * _load_packed(y0_ref) + route[:, 3:4] * _load_packed(y1_ref)


def _combine(x1, yg, route, row0, out_prev):
    n = yg.shape[1] // TOP_K
    tm = min(PROJ_ROWS, n)
    blk0 = row0 // tm
    prev = () if out_prev is None else (out_prev,)
    return pl.pallas_call(
        _combine_kernel,
        grid=(n // tm,),
        in_specs=[
            pl.BlockSpec((tm, D_MODEL), lambda i: (blk0 + i, 0)),
            pl.BlockSpec((tm, LANES), lambda i: (blk0 + i, 0)),
            pl.BlockSpec((ROW_PIECES, tm, PIECE), lambda i: (0, i, 0)),
            pl.BlockSpec((ROW_PIECES, tm, PIECE), lambda i: (0, i + n // tm, 0)),
        ] + [pl.BlockSpec(memory_space=pl.ANY)] * len(prev),
        out_specs=pl.BlockSpec((tm, D_MODEL), lambda i: (blk0 + i, 0)),
        out_shape=jax.ShapeDtypeStruct(x1.shape, jnp.float32),
        input_output_aliases={4: 0} if prev else {},
        compiler_params=pltpu.CompilerParams(dimension_semantics=("arbitrary",),
                                             vmem_limit_bytes=VMEM_LIMIT),
        name="combine",
    )(x1, route, yg, yg, *prev)


def _dispatch_plan(route_cols, counts, n):
    counts = counts[0, :N_EXPERTS].astype(jnp.int32)
    padded = ((counts + MOE_BLOCK - 1) // MOE_BLOCK) * MOE_BLOCK
    seg_end = jnp.cumsum(padded).astype(jnp.int32)
    seg_start = seg_end - padded
    cols = route_cols.astype(jnp.int32)
    e, rank = cols[0:TOP_K], cols[4:4 + TOP_K]
    picked = e[None] == jnp.arange(N_EXPERTS, dtype=jnp.int32)[:, None, None]
    dest = jnp.sum(jnp.where(picked, seg_start[:, None, None], 0), axis=0) + rank
    p = n * TOP_K + N_EXPERTS * MOE_BLOCK
    slot = dest[None] + (jnp.arange(ROW_PIECES, dtype=jnp.int32) * p)[:, None, None]
    blk_start = jnp.arange(p // MOE_BLOCK, dtype=jnp.int32) * MOE_BLOCK
    blk_expert = jnp.sum((seg_end[None, :] <= blk_start[:, None]).astype(jnp.int32), axis=1)
    blk_expert = jnp.minimum(blk_expert, N_EXPERTS - 1)
    n_used = (seg_end[-1] // MOE_BLOCK).reshape(1)
    return slot, blk_expert, n_used, p


def _layer(x, positions, norm1_g, w_in, q_norm_g, k_norm_g, lam4, diff_subln_g, ret_gn_g, ret_gn_b,
           w_branch_a, w_branch_b, w_out, norm2_g, w_gr, b_gr, w_er, b_er, w_gate, w_up, w_down):
    batch, seq, _ = x.shape
    n = batch * seq
    x2 = x.reshape(n, D_MODEL)
    proj = _in_proj(x2, positions.reshape(1, n), norm1_g, w_in, q_norm_g, k_norm_g)
    oa = _diff_attn(proj, lam4, diff_subln_g.reshape(1, DA_VDIM), batch, seq)
    ob = _retention(proj, ret_gn_g, ret_gn_b, batch, seq)
    x1, h2, route, route_cols, counts = _merge(x2, oa, ob, proj, w_branch_a, w_branch_b, w_out, norm2_g,
                                               w_gr, b_gr, w_er, b_er)
    slot, blk_expert, n_used, p = _dispatch_plan(route_cols, counts, n)
    win_n = n // SC_WINDOW
    src_block = lambda s: (s // (TOP_K * win_n)) * win_n + s % win_n
    xs = _sc_scatter_rows(h2.reshape(ROW_PIECES * n, PIECE), slot.reshape(1, -1), ROW_PIECES * p, src_block)
    ys = _experts(xs.reshape(ROW_PIECES, p, PIECE), blk_expert, n_used, w_gate, w_up, w_down)
    parts = COMBINE_PARTS if n % (COMBINE_PARTS * PROJ_ROWS) == 0 else 1
    m = n // parts
    out = None
    for t in range(parts):
        yg = _sc_gather_rows(ys.reshape(ROW_PIECES * p, PIECE), slot[:, :, t * m:(t + 1) * m].reshape(1, -1))
        out = _combine(x1, yg.reshape(ROW_PIECES, TOP_K * m, PIECE), route, t * m, out)
    return out.reshape(batch, seq, D_MODEL)


def kernel(x, positions, norm1_g, w_in, q_norm_g, k_norm_g, lambda_q1, lambda_k1, lambda_q2, lambda_k2, diff_subln_g, ret_gn_g, ret_gn_b, w_branch_a, w_branch_b, w_out, norm2_g, w_group_router, b_group_router, w_expert_router, b_expert_router, w_gate, w_up, w_down):
    assert x.shape[-1] == D_MODEL and norm1_g.shape[0] == 1, "single-layer, D_MODEL-wide input expected"
    lam4 = jnp.pad(jnp.stack([lambda_q1[0], lambda_k1[0], lambda_q2[0], lambda_k2[0]]), ((0, 0), (0, LANES - DA_HALF)))
    return _layer(x, positions, norm1_g[0], w_in[0], q_norm_g[0], k_norm_g[0], lam4, diff_subln_g[0],
                  ret_gn_g[0], ret_gn_b[0], w_branch_a[0], w_branch_b[0], w_out[0], norm2_g[0],
                  w_group_router[0], b_group_router[0], w_expert_router[0], b_expert_router[0],
                  w_gate[0], w_up[0], w_down[0])
```

```python
import functools
import math

import jax
import jax.numpy as jnp
from jax import lax
from jax.experimental import pallas as pl
from jax.experimental.pallas import tpu as pltpu
from jax.experimental.pallas import tpu_sc as plsc

D_MODEL = 1024
DA_HEADS = 4
DA_HALF = 64
DA_VDIM = 2 * DA_HALF
DA_WIDTH = DA_HEADS * DA_VDIM
ROPE_THETA = 500000.0
ROPE_DIM = DA_HALF // 4
RET_HEADS = 4
RET_KDIM = 128
RET_VDIM = 128
RET_WIDTH = RET_HEADS * RET_VDIM
RET_THETA = 10000.0
N_GROUPS = 4
EXPERTS_PER_GROUP = 8
N_EXPERTS = N_GROUPS * EXPERTS_PER_GROUP
TOP_K = 2
EXPERT_FF = 512
EPS = 1e-6
LAMBDA_INIT = 0.8 - 0.6 * math.exp(-0.3 * 0)

LANES = 128
IN_COLS = 3 * DA_WIDTH + 4 * RET_WIDTH + 2 * D_MODEL
COL_QA, COL_KA, COL_VA = 0, DA_WIDTH, 2 * DA_WIDTH
COL_QR = 3 * DA_WIDTH
COL_KR = COL_QR + RET_WIDTH
COL_VR = COL_KR + RET_WIDTH
COL_GB = COL_VR + RET_WIDTH
COL_GATE_A = COL_GB + RET_WIDTH
COL_GATE_B = COL_GATE_A + D_MODEL

PROJ_ROWS = 512
PROJ_CHUNK = 256
ATT_TILE = 512
ATT_ROWS = 32
RET_CHUNK = 256
RET_UNROLL = 16
MOE_BLOCK = 512
PACKED = D_MODEL // 2
ROW_PIECES = 2
PIECE = PACKED // ROW_PIECES
SC_CORES = 2
SC_WINDOW = 128
COMBINE_PARTS = 2
VMEM_LIMIT = 56 * 1024 * 1024


def _dot(a, b):
    return jnp.dot(a, b, preferred_element_type=jnp.float32)


def _dot_nt(a, b):
    return lax.dot_general(a, b, (((1,), (1,)), ((), ())), preferred_element_type=jnp.float32)


def _dot_tn(a, b):
    return lax.dot_general(a, b, (((0,), (0,)), ((), ())), preferred_element_type=jnp.float32)


def _sigmoid(x):
    return 0.5 * jnp.tanh(0.5 * x) + 0.5


def _split3(x):
    a = x.astype(jnp.bfloat16)
    r = x - a.astype(jnp.float32)
    b = r.astype(jnp.bfloat16)
    c = (r - b.astype(jnp.float32)).astype(jnp.bfloat16)
    return a, b, c


def _in_proj_kernel(x_ref, pos_ref, g1_ref, w_ref, gsum_ref, gq_ref, gk_ref, fa_ref, fr_ref, sel_ref,
                    o_ref, h_scr):
    x = x_ref[...]
    rows = x.shape[0]
    h_scr[...] = (x * g1_ref[...]).astype(jnp.bfloat16)
    rms_scale = jnp.broadcast_to(lax.rsqrt(jnp.mean(x * x, axis=-1, keepdims=True) + EPS), (rows, PROJ_CHUNK))
    pos = pos_ref[...].astype(jnp.float32)

    lane = lax.broadcasted_iota(jnp.int32, (rows, LANES), 1)
    half_a = ROPE_DIM // 2
    tables = {}

    def da_tables():
        if "da" not in tables:
            ang_a = fa_ref[...] * pos
            pad = jnp.zeros((LANES - 2 * half_a, rows), jnp.float32)
            t_a = jnp.concatenate([jnp.cos(ang_a), jnp.sin(ang_a), pad], axis=0).T
            tab = sum(_dot(part, sel_ref[...]) for part in _split3(t_a))
            c_a = tab[:, :LANES] + jnp.where(lane % DA_HALF < ROPE_DIM, 0.0, 1.0)
            s_lo = tab[:, LANES:2 * LANES]
            s_hi = tab[:, 2 * LANES:]
            tables["da"] = tuple(jnp.concatenate([v, v], axis=1) for v in (c_a, s_lo, s_hi))
        return tables["da"]

    def ret_tables():
        if "ret" not in tables:
            ang_r = fr_ref[...] * pos
            t_r = jnp.concatenate([jnp.cos(ang_r), jnp.sin(ang_r)], axis=0).T
            sw_r = pltpu.roll(t_r, RET_KDIM // 2, axis=1)
            first = lane < RET_KDIM // 2
            c_r = jnp.where(first, t_r, sw_r)
            s_r = jnp.where(first, -sw_r, t_r)
            tables["ret"] = tuple(jnp.concatenate([v, v], axis=1) for v in (c_r, s_r))
        return tables["ret"]

    def qk_norm_rope(y, g, scale):
        c_a2, s_lo2, s_hi2 = da_tables()
        ss = y * y
        hi = ss.astype(jnp.bfloat16)
        lo = (ss - hi.astype(jnp.float32)).astype(jnp.bfloat16)
        gs = _dot(hi, gsum_ref[...]) + _dot(lo, gsum_ref[...])
        n = y * lax.rsqrt(gs * (1.0 / DA_HALF) + EPS) * g
        up = pltpu.roll(n, PROJ_CHUNK - half_a, axis=1)
        dn = pltpu.roll(n, half_a, axis=1)
        r = n * c_a2 + up * s_lo2 + dn * s_hi2
        return r * scale if scale != 1.0 else r

    def ret_rope(y, scale):
        c_r2, s_r2 = ret_tables()
        halves = [pltpu.roll(y[:, i * LANES:(i + 1) * LANES], RET_KDIM // 2, axis=1)
                  for i in range(PROJ_CHUNK // LANES)]
        sw = jnp.concatenate(halves, axis=1)
        r = y * c_r2 + sw * s_r2
        return r * scale if scale != 1.0 else r

    n_chunks = IN_COLS // PROJ_CHUNK
    is_long = lambda c: c * PROJ_CHUNK < COL_VA or COL_QR <= c * PROJ_CHUNK < COL_VR
    long_chunks = [c for c in range(n_chunks) if is_long(c)]
    short_chunks = [c for c in range(n_chunks) if not is_long(c)][::-1]
    order = [short_chunks.pop(0) for _ in range(3)]
    while long_chunks or short_chunks:
        if long_chunks:
            order.append(long_chunks.pop(0))
        if short_chunks:
            order.append(short_chunks.pop(0))
    for c in order:
        c0 = c * PROJ_CHUNK
        y = _dot(h_scr[...], w_ref[:, c0:c0 + PROJ_CHUNK]) * rms_scale
        if c0 < COL_KA:
            y = qk_norm_rope(y, gq_ref[...], DA_HALF ** -0.5)
        elif c0 < COL_VA:
            y = qk_norm_rope(y, gk_ref[...], 1.0)
        elif c0 < COL_QR:
            pass
        elif c0 < COL_KR:
            y = ret_rope(y, 1.0)
        elif c0 < COL_VR:
            y = ret_rope(y, RET_KDIM ** -0.5)
        elif c0 < COL_GB:
            pass
        elif c0 < COL_GATE_A:
            y = y * _sigmoid(y)
        else:
            y = _sigmoid(y)
        o_ref[:, c0:c0 + PROJ_CHUNK] = y.astype(o_ref.dtype)


def _in_proj(x2, pos2, g1, w_in, gq, gk):
    n = x2.shape[0]
    tm = min(PROJ_ROWS, n)
    grp = jnp.arange(PROJ_CHUNK) // DA_HALF
    gsum = (grp[:, None] == grp[None, :]).astype(jnp.bfloat16)
    half_a = ROPE_DIM // 2
    fa = jnp.power(jnp.float32(ROPE_THETA), -2.0 * jnp.arange(half_a, dtype=jnp.float32) / ROPE_DIM)[:, None]
    half_r = RET_KDIM // 2
    fr = jnp.power(jnp.float32(RET_THETA), -2.0 * jnp.arange(half_r, dtype=jnp.float32) / RET_KDIM)[:, None]
    j = jnp.arange(LANES)[:, None]
    l64 = (jnp.arange(LANES) % DA_HALF)[None, :]
    sel_c = (j < half_a) & (l64 < ROPE_DIM) & (l64 % half_a == j)
    sel_lo = (j >= half_a) & (j < ROPE_DIM) & (l64 < half_a) & (l64 == j - half_a)
    sel_hi = (j >= half_a) & (j < ROPE_DIM) & (l64 >= half_a) & (l64 < ROPE_DIM) & (l64 == j)
    sel = jnp.concatenate([sel_c.astype(jnp.float32), -sel_lo.astype(jnp.float32),
                           sel_hi.astype(jnp.float32)], axis=1).astype(jnp.bfloat16)
    reps = PROJ_CHUNK // DA_HALF
    full = lambda shape: pl.BlockSpec(shape, lambda i: (0,) * len(shape))
    return pl.pallas_call(
        _in_proj_kernel,
        grid=(n // tm,),
        in_specs=[
            pl.BlockSpec((tm, D_MODEL), lambda i: (i, 0)),
            pl.BlockSpec((1, tm), lambda i: (0, i)),
            full((1, D_MODEL)),
            full((D_MODEL, IN_COLS)),
            full((PROJ_CHUNK, PROJ_CHUNK)),
            full((1, PROJ_CHUNK)),
            full((1, PROJ_CHUNK)),
            full((half_a, 1)),
            full((half_r, 1)),
            full((LANES, 3 * LANES)),
        ],
        out_specs=pl.BlockSpec((tm, IN_COLS), lambda i: (i, 0)),
        out_shape=jax.ShapeDtypeStruct((n, IN_COLS), jnp.bfloat16),
        scratch_shapes=[pltpu.VMEM((tm, D_MODEL), jnp.bfloat16)],
        compiler_params=pltpu.CompilerParams(dimension_semantics=("arbitrary",),
                                             vmem_limit_bytes=VMEM_LIMIT),
        name="in_proj",
    )(x2, pos2, g1.reshape(1, D_MODEL), w_in.astype(jnp.bfloat16), gsum,
      jnp.tile(gq, reps)[None, :], jnp.tile(gk, reps)[None, :], fa, fr, sel)


def _diff_attn_kernel(q_ref, k_ref, v_ref, lam_ref, gsub_ref, o_ref,
                      qs_scr, vx_scr, s0_scr, s1_scr, p_scr, m_scr, alpha_scr, acc_scr):
    i = pl.program_id(2)
    t = q_ref.shape[0]

    @pl.when(i == 0)
    def _():
        vx_scr[:, :DA_VDIM] = v_ref[...]
        vx_scr[:, DA_VDIM:] = jnp.ones((vx_scr.shape[0], LANES), vx_scr.dtype)

    q = q_ref[...]
    lane = lax.broadcasted_iota(jnp.int32, q.shape, 1)
    zero = jnp.zeros_like(q)
    qs_scr[:t] = jnp.where(lane < DA_HALF, q, zero)
    qs_scr[t:] = jnp.where(lane >= DA_HALF, q, zero)

    def scores(j, s_ref):
        start = pl.multiple_of(j * t, t)
        s_ref[...] = _dot_nt(qs_scr[...], k_ref[pl.ds(start, t), :])

    def softmax_pv(j, s_ref, masked, first=False):
        for c in range(2 * t // ATT_ROWS):
            rows = pl.ds(c * ATT_ROWS, ATT_ROWS)
            s = s_ref[rows, :]
            if masked:
                r = lax.broadcasted_iota(jnp.int32, s.shape, 0) + (c * ATT_ROWS) % t
                col = lax.broadcasted_iota(jnp.int32, s.shape, 1)
                s = jnp.where(col <= r, s, -jnp.inf)
            m_cur = jnp.max(s, axis=-1, keepdims=True)
            if first:
                m_new = jnp.broadcast_to(m_cur, (ATT_ROWS, LANES))
            else:
                m_prev = m_scr[rows, :]
                m_new = jnp.maximum(m_prev, m_cur)
                alpha_scr[rows, :] = jnp.exp(m_prev - m_new)
            m_scr[rows, :] = m_new
            p = jnp.exp(s - jnp.concatenate([m_new] * (t // LANES), axis=1))
            p_scr[rows, :] = p.astype(p_scr.dtype)
        start = pl.multiple_of(j * t, t)
        pv = _dot(p_scr[...], vx_scr[pl.ds(start, t), :])
        if first:
            acc_scr[...] = pv
            return
        alpha = alpha_scr[...]
        for half in range(2):
            cols = pl.ds(half * LANES, LANES)
            acc_scr[:, cols] = alpha * acc_scr[:, cols] + pv[:, half * LANES:(half + 1) * LANES]

    scores(0, s0_scr)

    @pl.when(i == 0)
    def _():
        softmax_pv(0, s0_scr, True, first=True)

    @pl.when(i == 1)
    def _():
        scores(1, s1_scr)
        softmax_pv(0, s0_scr, False, first=True)
        softmax_pv(1, s1_scr, True)

    @pl.when(i >= 2)
    def _():
        scores(1, s1_scr)
        softmax_pv(0, s0_scr, False, first=True)
        scores(2, s0_scr)
        softmax_pv(1, s1_scr, False)

        def pair(jj, carry):
            j = 2 * jj
            scores(j + 1, s1_scr)
            softmax_pv(j, s0_scr, False)
            scores(j + 2, s0_scr)
            softmax_pv(j + 1, s1_scr, False)
            return carry

        lax.fori_loop(1, i // 2, pair, 0)

        @pl.when(i % 2 == 1)
        def _():
            scores(i, s1_scr)
            softmax_pv(i - 1, s0_scr, False)
            softmax_pv(i, s1_scr, True)

        @pl.when(i % 2 == 0)
        def _():
            softmax_pv(i, s0_scr, True)

    lam4 = lam_ref[...]
    lam = (jnp.exp(jnp.sum(lam4[0:1] * lam4[1:2], axis=-1, keepdims=True))
           - jnp.exp(jnp.sum(lam4[2:3] * lam4[3:4], axis=-1, keepdims=True)) + LAMBDA_INIT)
    o_all = acc_scr[:, :DA_VDIM] / acc_scr[:, DA_VDIM:]
    o = o_all[:t] - lam * o_all[t:]
    o = o * lax.rsqrt(jnp.mean(o * o, axis=-1, keepdims=True) + EPS) * gsub_ref[...] * (1.0 - LAMBDA_INIT)
    o_ref[...] = o.astype(o_ref.dtype)


def _diff_attn(proj, lam4, gsub, batch, seq):
    n = proj.shape[0]
    t = min(ATT_TILE, seq)
    nq = seq // t
    qb, kb, vb = COL_QA // LANES, COL_KA // LANES, COL_VA // LANES
    return pl.pallas_call(
        _diff_attn_kernel,
        grid=(batch, DA_HEADS, nq),
        in_specs=[
            pl.BlockSpec((t, LANES), lambda b, h, i: (b * nq + i, qb + h)),
            pl.BlockSpec((seq, LANES), lambda b, h, i: (b, kb + h)),
            pl.BlockSpec((seq, LANES), lambda b, h, i: (b, vb + h)),
            pl.BlockSpec((4, LANES), lambda b, h, i: (0, 0)),
            pl.BlockSpec((1, LANES), lambda b, h, i: (0, 0)),
        ],
        out_specs=pl.BlockSpec((t, LANES), lambda b, h, i: (b * nq + i, h)),
        out_shape=jax.ShapeDtypeStruct((n, DA_WIDTH), jnp.bfloat16),
        scratch_shapes=[pltpu.VMEM((2 * t, LANES), jnp.bfloat16),
                        pltpu.VMEM((seq, DA_VDIM + LANES), jnp.bfloat16),
                        pltpu.VMEM((2 * t, t), jnp.float32),
                        pltpu.VMEM((2 * t, t), jnp.float32),
                        pltpu.VMEM((2 * t, t), jnp.bfloat16),
                        pltpu.VMEM((2 * t, LANES), jnp.float32),
                        pltpu.VMEM((2 * t, LANES), jnp.float32),
                        pltpu.VMEM((2 * t, DA_VDIM + LANES), jnp.float32)],
        compiler_params=pltpu.CompilerParams(dimension_semantics=("arbitrary",) * 3,
                                             vmem_limit_bytes=VMEM_LIMIT),
        name="diff_attn",
    )(proj, proj, proj, lam4, gsub)


def _retention_kernel(q_ref, k_ref, v_ref, g_ref, gng_ref, gnb_ref, o_ref, r_scr, *, chunk):
    hf = jnp.full((1, 1), pl.program_id(1), jnp.int32).astype(jnp.float32)
    log_g = jnp.log1p(-jnp.exp2(-5.0 - hf))
    ri = lax.broadcasted_iota(jnp.int32, (chunk, chunk), 0)
    ci = lax.broadcasted_iota(jnp.int32, (chunk, chunk), 1)
    rel = (ri - ci).astype(jnp.float32)
    dmask = jnp.where(rel >= 0, jnp.exp(jnp.maximum(rel, 0.0) * log_g), 0.0)
    idx = lax.broadcasted_iota(jnp.int32, (chunk, 1), 0).astype(jnp.float32)
    zeta = jnp.exp((chunk - 1 - idx) * log_g)
    xi = jnp.exp((idx + 1.0) * log_g)
    g_chunk = jnp.exp(chunk * log_g)
    r_scr[...] = jnp.zeros(r_scr.shape, jnp.float32)
    gng = gng_ref[...]
    gnb = gnb_ref[...]

    def body(n, carry):
        start = pl.multiple_of(n * chunk, chunk)
        q = q_ref[pl.ds(start, chunk), :]
        k = k_ref[pl.ds(start, chunk), :]
        v = v_ref[pl.ds(start, chunk), :]
        s = _dot_nt(q, k) * dmask
        r_old = r_scr[...]
        o = _dot(s.astype(jnp.bfloat16), v) + xi * _dot(q, r_old.astype(jnp.bfloat16))
        kz = (k.astype(jnp.float32) * zeta).astype(jnp.bfloat16)
        r_scr[...] = g_chunk * r_old + _dot_tn(kz, v)
        mu = jnp.mean(o, axis=-1, keepdims=True)
        d = o - mu
        var = jnp.mean(d * d, axis=-1, keepdims=True)
        y = d * lax.rsqrt(var + EPS) * gng + gnb
        y = y * g_ref[pl.ds(start, chunk), :].astype(jnp.float32)
        o_ref[pl.ds(start, chunk), :] = y.astype(o_ref.dtype)
        return carry

    lax.fori_loop(0, q_ref.shape[0] // chunk, body, 0, unroll=RET_UNROLL)


def _retention(proj, gn_g, gn_b, batch, seq):
    n = proj.shape[0]
    chunk = min(RET_CHUNK, seq)
    col = lambda c0: (lambda b, h: (b, c0 // LANES + h))
    return pl.pallas_call(
        functools.partial(_retention_kernel, chunk=chunk),
        grid=(batch, RET_HEADS),
        in_specs=[
            pl.BlockSpec((seq, LANES), col(COL_QR)),
            pl.BlockSpec((seq, LANES), col(COL_KR)),
            pl.BlockSpec((seq, LANES), col(COL_VR)),
            pl.BlockSpec((seq, LANES), col(COL_GB)),
            pl.BlockSpec((1, LANES), lambda b, h: (0, h)),
            pl.BlockSpec((1, LANES), lambda b, h: (0, h)),
        ],
        out_specs=pl.BlockSpec((seq, LANES), lambda b, h: (b, h)),
        out_shape=jax.ShapeDtypeStruct((n, RET_WIDTH), jnp.bfloat16),
        scratch_shapes=[pltpu.VMEM((RET_KDIM, RET_VDIM), jnp.float32)],
        compiler_params=pltpu.CompilerParams(dimension_semantics=("arbitrary",) * 2,
                                             vmem_limit_bytes=VMEM_LIMIT),
        name="retention",
    )(proj, proj, proj, proj, gn_g.reshape(1, RET_WIDTH), gn_b.reshape(1, RET_WIDTH))


def _merge_kernel(x_ref, oa_ref, ob_ref, sa0_ref, sa1_ref, sb0_ref, sb1_ref, wa_ref, wb_ref, wo_ref,
                  g2_ref, wr_hi_ref, wr_lo_ref, br_ref, tri_ref, x1_ref, h2_ref, route_ref, cols_ref, counts_ref,
                  base_scr, logits_scr):
    i = pl.program_id(0)

    @pl.when(i == 0)
    def _():
        base_scr[...] = jnp.zeros(base_scr.shape, jnp.float32)
        logits_scr[...] = jnp.zeros(logits_scr.shape, jnp.float32)

    ya = _dot(oa_ref[...], wa_ref[...])
    yb = _dot(ob_ref[...], wb_ref[...])

    logits = logits_scr[...]
    lane = lax.broadcasted_iota(jnp.int32, logits.shape, 1)
    neg = -jnp.inf
    gl = jnp.where(lane < N_GROUPS, logits, neg)
    gmax = jnp.max(gl, axis=-1, keepdims=True)
    g_idx = jnp.min(jnp.where(gl == gmax, lane, LANES), axis=-1, keepdims=True)
    p_g = 1.0 / jnp.sum(jnp.exp(gl - gmax), axis=-1, keepdims=True)
    e_lo = N_GROUPS + EXPERTS_PER_GROUP * g_idx
    el = jnp.where((lane >= e_lo) & (lane < e_lo + EXPERTS_PER_GROUP), logits, neg)
    v1 = jnp.max(el, axis=-1, keepdims=True)
    i1 = jnp.min(jnp.where(el == v1, lane, LANES), axis=-1, keepdims=True)
    el2 = jnp.where(lane == i1, neg, el)
    v2 = jnp.max(el2, axis=-1, keepdims=True)
    i2 = jnp.min(jnp.where(el2 == v2, lane, LANES), axis=-1, keepdims=True)
    t = jnp.exp(v2 - v1)
    w1 = p_g / (1.0 + t)
    w2 = p_g * t / (1.0 + t)
    e1 = i1 - N_GROUPS
    e2 = i2 - N_GROUPS

    oh1 = lane == e1
    oh2 = lane == e2
    real = jnp.where(i > 0, 1.0, 0.0)
    picked = jnp.where(oh1 | oh2, real, 0.0)
    before = _dot(tri_ref[...], picked.astype(jnp.bfloat16)) + base_scr[0:1, :]
    rank1 = jnp.sum(jnp.where(oh1, before, 0.0), axis=-1, keepdims=True)
    rank2 = jnp.sum(jnp.where(oh2, before, 0.0), axis=-1, keepdims=True)
    base_scr[...] = base_scr[...] + jnp.sum(picked, axis=0, keepdims=True)
    counts_ref[...] = base_scr[...]

    cols = [e1.astype(jnp.float32), e2.astype(jnp.float32), w1, w2, rank1, rank2]
    route = jnp.zeros(logits.shape, jnp.float32)
    for c, val in enumerate(cols):
        route = jnp.where(lane == c, val, route)
    route_ref[...] = route
    cols_ref[...] = route.T[:8]

    sa = jnp.concatenate([sa0_ref[...], sa1_ref[...]], axis=1).astype(jnp.float32)
    sb = jnp.concatenate([sb0_ref[...], sb1_ref[...]], axis=1).astype(jnp.float32)
    merged = sa * ya + sb * yb
    x1 = x_ref[...] + _dot(merged.astype(jnp.bfloat16), wo_ref[...])
    x1_ref[...] = x1
    h2 = x1 * lax.rsqrt(jnp.mean(x1 * x1, axis=-1, keepdims=True) + EPS) * g2_ref[...]
    _store_packed(h2_ref, h2)

    hi = h2.astype(jnp.bfloat16)
    lo = (h2 - hi.astype(jnp.float32)).astype(jnp.bfloat16)
    logits_scr[...] = (_dot(hi, wr_hi_ref[...]) + _dot(lo, wr_hi_ref[...]) + _dot(hi, wr_lo_ref[...])
                       + br_ref[...])


def _merge(x2, oa, ob, proj, wa, wb, wo, g2, w_gr, b_gr, w_er, b_er):
    n = x2.shape[0]
    tm = min(PROJ_ROWS, n)
    half = D_MODEL // 2
    pad = LANES - N_GROUPS - N_EXPERTS
    wr = jnp.concatenate([w_gr, w_er, jnp.zeros((D_MODEL, pad), jnp.float32)], axis=1)
    wr_hi = wr.astype(jnp.bfloat16)
    wr_lo = (wr - wr_hi.astype(jnp.float32)).astype(jnp.bfloat16)
    br = jnp.concatenate([b_gr, b_er, jnp.zeros((pad,), jnp.float32)])[None, :]
    tri = (jnp.arange(tm)[:, None] > jnp.arange(tm)[None, :]).astype(jnp.bfloat16)
    full = lambda shape: pl.BlockSpec(shape, lambda i: (0,) * len(shape))
    nt = n // tm
    cur = lambda i: jnp.minimum(i, nt - 1)
    gate = lambda c0: pl.BlockSpec((tm, half), lambda i: (cur(i), c0 // half))
    return pl.pallas_call(
        _merge_kernel,
        grid=(nt + 1,),
        in_specs=[
            pl.BlockSpec((tm, D_MODEL), lambda i: (cur(i), 0)),
            pl.BlockSpec((tm, DA_WIDTH), lambda i: (cur(i), 0)),
            pl.BlockSpec((tm, RET_WIDTH), lambda i: (cur(i), 0)),
            gate(COL_GATE_A), gate(COL_GATE_A + half), gate(COL_GATE_B), gate(COL_GATE_B + half),
            full((DA_WIDTH, D_MODEL)), full((RET_WIDTH, D_MODEL)), full((D_MODEL, D_MODEL)),
            full((1, D_MODEL)), full((D_MODEL, LANES)), full((D_MODEL, LANES)), full((1, LANES)),
            full((tm, tm)),
        ],
        out_specs=[
            pl.BlockSpec((tm, D_MODEL), lambda i: (cur(i), 0)),
            pl.BlockSpec((ROW_PIECES, tm, PIECE), lambda i: (0, cur(i), 0)),
            pl.BlockSpec((tm, LANES), lambda i: (jnp.maximum(i - 1, 0), 0)),
            pl.BlockSpec((8, tm), lambda i: (0, jnp.maximum(i - 1, 0))),
            pl.BlockSpec((8, LANES), lambda i: (0, 0)),
        ],
        out_shape=[
            jax.ShapeDtypeStruct((n, D_MODEL), jnp.float32),
            jax.ShapeDtypeStruct((ROW_PIECES, n, PIECE), jnp.uint32),
            jax.ShapeDtypeStruct((n, LANES), jnp.float32),
            jax.ShapeDtypeStruct((8, n), jnp.float32),
            jax.ShapeDtypeStruct((8, LANES), jnp.float32),
        ],
        scratch_shapes=[pltpu.VMEM((8, LANES), jnp.float32), pltpu.VMEM((tm, LANES), jnp.float32)],
        compiler_params=pltpu.CompilerParams(dimension_semantics=("arbitrary",),
                                             vmem_limit_bytes=VMEM_LIMIT),
        name="merge",
    )(x2, oa, ob, proj, proj, proj, proj, wa.astype(jnp.bfloat16), wb.astype(jnp.bfloat16),
      wo.astype(jnp.bfloat16), g2.reshape(1, D_MODEL), wr_hi, wr_lo, br, tri)


def _sc_mesh():
    return plsc.VectorSubcoreMesh(core_axis_name="c", subcore_axis_name="s")


def _sc_scatter_rows(src, idx, out_rows, src_block):
    steps = idx.shape[1] // SC_WINDOW
    per_core = steps // SC_CORES

    @pl.kernel(out_type=jax.ShapeDtypeStruct((out_rows, PIECE), src.dtype), mesh=_sc_mesh())
    def scatter(src_hbm, idx_hbm, out_hbm):
        def body(src_vmem, idx_vmem):
            pltpu.sync_copy(src_vmem, out_hbm.at[idx_vmem.at[0]])

        pltpu.emit_pipeline(
            body,
            grid=(SC_CORES, per_core),
            in_specs=[pl.BlockSpec((SC_WINDOW, PIECE), lambda c, i: (src_block(c * per_core + i), 0)),
                      pl.BlockSpec((1, SC_WINDOW), lambda c, i: (0, c * per_core + i))],
            out_specs=[],
            core_axis_name=("c", "s"),
            dimension_semantics=(pltpu.PARALLEL, pltpu.PARALLEL),
        )(src_hbm, idx_hbm)

    return scatter(src, idx)


def _sc_gather_rows(table, idx):
    num = idx.shape[1]
    per_core = num // SC_WINDOW // SC_CORES

    @pl.kernel(out_type=jax.ShapeDtypeStruct((num, PIECE), table.dtype), mesh=_sc_mesh())
    def gather(table_hbm, idx_hbm, out_hbm):
        def body(idx_vmem, out_vmem):
            pltpu.sync_copy(table_hbm.at[idx_vmem.at[0]], out_vmem)

        pltpu.emit_pipeline(
            body,
            grid=(SC_CORES, per_core),
            in_specs=[pl.BlockSpec((1, SC_WINDOW), lambda c, i: (0, c * per_core + i))],
            out_specs=[pl.BlockSpec((SC_WINDOW, PIECE), lambda c, i: (c * per_core + i, 0))],
            core_axis_name=("c", "s"),
            dimension_semantics=(pltpu.PARALLEL, pltpu.PARALLEL),
        )(idx_hbm, out_hbm)

    return gather(table, idx)


def _store_packed(ref, val):
    as_bits = lambda v: lax.bitcast_convert_type(v.astype(jnp.bfloat16).astype(jnp.float32), jnp.uint32)
    words = (as_bits(val[:, :PACKED]) >> 16) | (as_bits(val[:, PACKED:]) & jnp.uint32(0xFFFF0000))
    for j in range(ROW_PIECES):
        ref[j, :val.shape[0], :] = words[:, j * PIECE:(j + 1) * PIECE]


def _load_packed(ref, rows=None):
    rows = ref.shape[1] if rows is None else rows
    words = jnp.concatenate([ref[j, :rows, :] for j in range(ROW_PIECES)], axis=1)
    low = lax.bitcast_convert_type(words << 16, jnp.float32)
    high = lax.bitcast_convert_type(words & jnp.uint32(0xFFFF0000), jnp.float32)
    return jnp.concatenate([low, high], axis=1)


def _expert_kernel(blk_e_ref, n_used_ref, nxt_ref, run_ref, rows_ref, x_ref, wg_hbm, wu_hbm, wd_hbm, o_ref,
                   wg_stage, wu_stage, wd_stage, wg_scr, wu_scr, wd_scr, sem):
    i = pl.program_id(0)
    used = i < n_used_ref[0]

    def weight_copies(e, s):
        return (pltpu.make_async_copy(wg_hbm.at[e], wg_stage.at[s], sem.at[s, 0]),
                pltpu.make_async_copy(wu_hbm.at[e], wu_stage.at[s], sem.at[s, 1]),
                pltpu.make_async_copy(wd_hbm.at[e], wd_stage.at[s], sem.at[s, 2]))

    @pl.when(i == 0)
    def _():
        for c in weight_copies(blk_e_ref[0], 0):
            c.start()

    @pl.when(used & ((i == 0) | (blk_e_ref[i] != blk_e_ref[jnp.maximum(i - 1, 0)])))
    def _():
        s = run_ref[i] % 2
        for c in weight_copies(blk_e_ref[i], s):
            c.wait()
        wg_scr[...] = wg_stage[s].astype(jnp.bfloat16)
        wu_scr[...] = wu_stage[s].astype(jnp.bfloat16)
        wd_scr[...] = wd_stage[s].astype(jnp.bfloat16)

        @pl.when(nxt_ref[i] >= 0)
        def _():
            for c in weight_copies(nxt_ref[i], 1 - s):
                c.start()

    def expert_mlp(rows):
        x = _load_packed(x_ref, rows).astype(jnp.bfloat16)
        a = _dot(x, wg_scr[...])
        u = _dot(x, wu_scr[...])
        hmid = (a * _sigmoid(a) * u).astype(jnp.bfloat16)
        _store_packed(o_ref, _dot(hmid, wd_scr[...]))

    half = MOE_BLOCK // 2
    short = rows_ref[i] <= half

    @pl.when(used & jnp.logical_not(short))
    def _():
        expert_mlp(MOE_BLOCK)

    @pl.when(used & short)
    def _():
        expert_mlp(half)
        o_ref[:, half:, :] = jnp.zeros((ROW_PIECES, MOE_BLOCK - half, PIECE), o_ref.dtype)

    @pl.when(jnp.logical_not(used))
    def _():
        o_ref[...] = jnp.zeros(o_ref.shape, o_ref.dtype)


def _experts(xs, blk_expert, blk_rows, n_used, w_gate, w_up, w_down):
    p = xs.shape[1]
    nblk = p // MOE_BLOCK
    idx = jnp.arange(nblk, dtype=jnp.int32)
    starts = (idx < n_used[0]) & ((idx == 0) | (blk_expert != jnp.roll(blk_expert, 1)))
    run = jnp.cumsum(starts.astype(jnp.int32)) - 1
    next_start = lax.cummin(jnp.where(starts, idx, nblk)[::-1])[::-1]
    after = jnp.concatenate([next_start[1:], jnp.full((1,), nblk, jnp.int32)])
    nxt = jnp.where(after < nblk, blk_expert[jnp.minimum(after, nblk - 1)], -1).astype(jnp.int32)
    live = lambda i, be, nu, *_: jnp.minimum(i, nu[0] - 1)
    any_spec = pl.BlockSpec(memory_space=pl.ANY)
    return pl.pallas_call(
        _expert_kernel,
        grid_spec=pltpu.PrefetchScalarGridSpec(
            num_scalar_prefetch=5,
            grid=(nblk,),
            in_specs=[
                pl.BlockSpec((ROW_PIECES, MOE_BLOCK, PIECE), lambda i, *pre: (0, live(i, *pre), 0)),
                any_spec, any_spec, any_spec,
            ],
            out_specs=pl.BlockSpec((ROW_PIECES, MOE_BLOCK, PIECE), lambda i, *pre: (0, i, 0)),
            scratch_shapes=[pltpu.VMEM((2, D_MODEL, EXPERT_FF), jnp.float32),
                            pltpu.VMEM((2, D_MODEL, EXPERT_FF), jnp.float32),
                            pltpu.VMEM((2, EXPERT_FF, D_MODEL), jnp.float32),
                            pltpu.VMEM((D_MODEL, EXPERT_FF), jnp.bfloat16),
                            pltpu.VMEM((D_MODEL, EXPERT_FF), jnp.bfloat16),
                            pltpu.VMEM((EXPERT_FF, D_MODEL), jnp.bfloat16),
                            pltpu.SemaphoreType.DMA((2, 3))],
        ),
        out_shape=jax.ShapeDtypeStruct((ROW_PIECES, p, PIECE), jnp.uint32),
        compiler_params=pltpu.CompilerParams(dimension_semantics=("arbitrary",),
                                             vmem_limit_bytes=VMEM_LIMIT),
        name="experts",
    )(blk_expert, n_used, nxt, run.astype(jnp.int32), blk_rows, xs, w_gate, w_up, w_down)


def _combine_kernel(x1_ref, route_ref, y0_ref, y1_ref, *rest):
    o_ref = rest[-1]
    route = route_ref[...]
    o_ref[...] = x1_ref[...] + route[:, 2:3] * _load_packed(y0_ref) + route[:, 3:4] * _load_packed(y1_ref)


def _combine(x1, yg, route, row0, out_prev):
    n = yg.shape[1] // TOP_K
    tm = min(PROJ_ROWS, n)
    blk0 = row0 // tm
    prev = () if out_prev is None else (out_prev,)
    return pl.pallas_call(
        _combine_kernel,
        grid=(n // tm,),
        in_specs=[
            pl.BlockSpec((tm, D_MODEL), lambda i: (blk0 + i, 0)),
            pl.BlockSpec((tm, LANES), lambda i: (blk0 + i, 0)),
            pl.BlockSpec((ROW_PIECES, tm, PIECE), lambda i: (0, i, 0)),
            pl.BlockSpec((ROW_PIECES, tm, PIECE), lambda i: (0, i + n // tm, 0)),
        ] + [pl.BlockSpec(memory_space=pl.ANY)] * len(prev),
        out_specs=pl.BlockSpec((tm, D_MODEL), lambda i: (blk0 + i, 0)),
        out_shape=jax.ShapeDtypeStruct(x1.shape, jnp.float32),
        input_output_aliases={4: 0} if prev else {},
        compiler_params=pltpu.CompilerParams(dimension_semantics=("arbitrary",),
                                             vmem_limit_bytes=VMEM_LIMIT),
        name="combine",
    )(x1, route, yg, yg, *prev)


def _dispatch_plan(route_cols, counts, n):
    counts = counts[0, :N_EXPERTS].astype(jnp.int32)
    padded = ((counts + MOE_BLOCK - 1) // MOE_BLOCK) * MOE_BLOCK
    seg_end = jnp.cumsum(padded).astype(jnp.int32)
    seg_start = seg_end - padded
    cols = route_cols.astype(jnp.int32)
    e, rank = cols[0:TOP_K], cols[4:4 + TOP_K]
    picked = e[None] == jnp.arange(N_EXPERTS, dtype=jnp.int32)[:, None, None]
    dest = jnp.sum(jnp.where(picked, seg_start[:, None, None], 0), axis=0) + rank
    p = n * TOP_K + N_EXPERTS * MOE_BLOCK
    slot = dest[None] + (jnp.arange(ROW_PIECES, dtype=jnp.int32) * p)[:, None, None]
    blk_start = jnp.arange(p // MOE_BLOCK, dtype=jnp.int32) * MOE_BLOCK
    blk_expert = jnp.sum((seg_end[None, :] <= blk_start[:, None]).astype(jnp.int32), axis=1)
    blk_expert = jnp.minimum(blk_expert, N_EXPERTS - 1)
    of_block = blk_expert[:, None] == jnp.arange(N_EXPERTS, dtype=jnp.int32)[None, :]
    real_end = jnp.sum(jnp.where(of_block, (seg_start + counts)[None, :], 0), axis=1)
    blk_rows = jnp.clip(real_end - blk_start, 0, MOE_BLOCK)
    n_used = (seg_end[-1] // MOE_BLOCK).reshape(1)
    return slot, blk_expert, blk_rows, n_used, p


def _layer(x, positions, norm1_g, w_in, q_norm_g, k_norm_g, lam4, diff_subln_g, ret_gn_g, ret_gn_b,
           w_branch_a, w_branch_b, w_out, norm2_g, w_gr, b_gr, w_er, b_er, w_gate, w_up, w_down):
    batch, seq, _ = x.shape
    n = batch * seq
    x2 = x.reshape(n, D_MODEL)
    proj = _in_proj(x2, positions.reshape(1, n), norm1_g, w_in, q_norm_g, k_norm_g)
    oa = _diff_attn(proj, lam4, diff_subln_g.reshape(1, DA_VDIM), batch, seq)
    ob = _retention(proj, ret_gn_g, ret_gn_b, batch, seq)
    x1, h2, route, route_cols, counts = _merge(x2, oa, ob, proj, w_branch_a, w_branch_b, w_out, norm2_g,
                                               w_gr, b_gr, w_er, b_er)
    slot, blk_expert, blk_rows, n_used, p = _dispatch_plan(route_cols, counts, n)
    win_n = n // SC_WINDOW
    src_block = lambda s: (s // (TOP_K * win_n)) * win_n + s % win_n
    xs = _sc_scatter_rows(h2.reshape(ROW_PIECES * n, PIECE), slot.reshape(1, -1), ROW_PIECES * p, src_block)
    ys = _experts(xs.reshape(ROW_PIECES, p, PIECE), blk_expert, blk_rows, n_used, w_gate, w_up, w_down)
    parts = COMBINE_PARTS if n % (COMBINE_PARTS * PROJ_ROWS) == 0 else 1
    m = n // parts
    out = None
    for t in range(parts):
        yg = _sc_gather_rows(ys.reshape(ROW_PIECES * p, PIECE), slot[:, :, t * m:(t + 1) * m].reshape(1, -1))
        out = _combine(x1, yg.reshape(ROW_PIECES, TOP_K * m, PIECE), route, t * m, out)
    return out.reshape(batch, seq, D_MODEL)


def kernel(x, positions, norm1_g, w_in, q_norm_g, k_norm_g, lambda_q1, lambda_k1, lambda_q2, lambda_k2, diff_subln_g, ret_gn_g, ret_gn_b, w_branch_a, w_branch_b, w_out, norm2_g, w_group_router, b_group_router, w_expert_router, b_expert_router, w_gate, w_up, w_down):
    assert x.shape[-1] == D_MODEL and norm1_g.shape[0] == 1, "single-layer, D_MODEL-wide input expected"
    lam4 = jnp.pad(jnp.stack([lambda_q1[0], lambda_k1[0], lambda_q2[0], lambda_k2[0]]), ((0, 0), (0, LANES - DA_HALF)))
    return _layer(x, positions, norm1_g[0], w_in[0], q_norm_g[0], k_norm_g[0], lam4, diff_subln_g[0],
                  ret_gn_g[0], ret_gn_b[0], w_branch_a[0], w_branch_b[0], w_out[0], norm2_g[0],
                  w_group_router[0], b_group_router[0], w_expert_router[0], b_expert_router[0],
                  w_gate[0], w_up[0], w_down[0])
```

```python
import functools
import math

import jax
import jax.numpy as jnp
from jax import lax
from jax.experimental import pallas as pl
from jax.experimental.pallas import tpu as pltpu
from jax.experimental.pallas import tpu_sc as plsc

D_MODEL = 1024
DA_HEADS = 4
DA_HALF = 64
DA_VDIM = 2 * DA_HALF
DA_WIDTH = DA_HEADS * DA_VDIM
ROPE_THETA = 500000.0
ROPE_DIM = DA_HALF // 4
RET_HEADS = 4
RET_KDIM = 128
RET_VDIM = 128
RET_WIDTH = RET_HEADS * RET_VDIM
RET_THETA = 10000.0
N_GROUPS = 4
EXPERTS_PER_GROUP = 8
N_EXPERTS = N_GROUPS * EXPERTS_PER_GROUP
TOP_K = 2
EXPERT_FF = 512
EPS = 1e-6
LAMBDA_INIT = 0.8 - 0.6 * math.exp(-0.3 * 0)

LANES = 128
IN_COLS = 3 * DA_WIDTH + 4 * RET_WIDTH + 2 * D_MODEL
COL_QA, COL_KA, COL_VA = 0, DA_WIDTH, 2 * DA_WIDTH
COL_QR = 3 * DA_WIDTH
COL_KR = COL_QR + RET_WIDTH
COL_VR = COL_KR + RET_WIDTH
COL_GB = COL_VR + RET_WIDTH
COL_GATE_A = COL_GB + RET_WIDTH
COL_GATE_B = COL_GATE_A + D_MODEL

PROJ_ROWS = 512
PROJ_CHUNK = 256
ATT_TILE = 1024
ATT_ROWS = 32
RET_CHUNK = 256
RET_UNROLL = 16
MOE_BLOCK = 512
PACKED = D_MODEL // 2
ROW_PIECES = 2
PIECE = PACKED // ROW_PIECES
SC_CORES = 2
SC_WINDOW = 128
COMBINE_PARTS = 2
VMEM_LIMIT = 56 * 1024 * 1024


def _dot(a, b):
    return jnp.dot(a, b, preferred_element_type=jnp.float32)


def _dot_nt(a, b):
    return lax.dot_general(a, b, (((1,), (1,)), ((), ())), preferred_element_type=jnp.float32)


def _dot_tn(a, b):
    return lax.dot_general(a, b, (((0,), (0,)), ((), ())), preferred_element_type=jnp.float32)


def _sigmoid(x):
    return 0.5 * jnp.tanh(0.5 * x) + 0.5


def _split3(x):
    a = x.astype(jnp.bfloat16)
    r = x - a.astype(jnp.float32)
    b = r.astype(jnp.bfloat16)
    c = (r - b.astype(jnp.float32)).astype(jnp.bfloat16)
    return a, b, c


def _in_proj_kernel(x_ref, pos_ref, g1_ref, w_ref, gsum_ref, gq_ref, gk_ref, fa_ref, fr_ref, sel_ref,
                    o_ref, h_scr):
    x = x_ref[...]
    rows = x.shape[0]
    h_scr[...] = (x * g1_ref[...]).astype(jnp.bfloat16)
    rms_scale = jnp.broadcast_to(lax.rsqrt(jnp.mean(x * x, axis=-1, keepdims=True) + EPS), (rows, PROJ_CHUNK))
    pos = pos_ref[...].astype(jnp.float32)

    lane = lax.broadcasted_iota(jnp.int32, (rows, LANES), 1)
    half_a = ROPE_DIM // 2
    tables = {}

    def da_tables():
        if "da" not in tables:
            ang_a = fa_ref[...] * pos
            pad = jnp.zeros((LANES - 2 * half_a, rows), jnp.float32)
            t_a = jnp.concatenate([jnp.cos(ang_a), jnp.sin(ang_a), pad], axis=0).T
            tab = sum(_dot(part, sel_ref[...]) for part in _split3(t_a))
            c_a = tab[:, :LANES] + jnp.where(lane % DA_HALF < ROPE_DIM, 0.0, 1.0)
            s_lo = tab[:, LANES:2 * LANES]
            s_hi = tab[:, 2 * LANES:]
            tables["da"] = tuple(jnp.concatenate([v, v], axis=1) for v in (c_a, s_lo, s_hi))
        return tables["da"]

    def ret_tables():
        if "ret" not in tables:
            ang_r = fr_ref[...] * pos
            t_r = jnp.concatenate([jnp.cos(ang_r), jnp.sin(ang_r)], axis=0).T
            sw_r = pltpu.roll(t_r, RET_KDIM // 2, axis=1)
            first = lane < RET_KDIM // 2
            c_r = jnp.where(first, t_r, sw_r)
            s_r = jnp.where(first, -sw_r, t_r)
            tables["ret"] = tuple(jnp.concatenate([v, v], axis=1) for v in (c_r, s_r))
        return tables["ret"]

    def qk_norm_rope(y, g, scale):
        c_a2, s_lo2, s_hi2 = da_tables()
        ss = y * y
        hi = ss.astype(jnp.bfloat16)
        lo = (ss - hi.astype(jnp.float32)).astype(jnp.bfloat16)
        gs = _dot(hi, gsum_ref[...]) + _dot(lo, gsum_ref[...])
        n = y * lax.rsqrt(gs * (1.0 / DA_HALF) + EPS) * g
        up = pltpu.roll(n, PROJ_CHUNK - half_a, axis=1)
        dn = pltpu.roll(n, half_a, axis=1)
        r = n * c_a2 + up * s_lo2 + dn * s_hi2
        return r * scale if scale != 1.0 else r

    def ret_rope(y, scale):
        c_r2, s_r2 = ret_tables()
        halves = [pltpu.roll(y[:, i * LANES:(i + 1) * LANES], RET_KDIM // 2, axis=1)
                  for i in range(PROJ_CHUNK // LANES)]
        sw = jnp.concatenate(halves, axis=1)
        r = y * c_r2 + sw * s_r2
        return r * scale if scale != 1.0 else r

    n_chunks = IN_COLS // PROJ_CHUNK
    is_long = lambda c: c * PROJ_CHUNK < COL_VA or COL_QR <= c * PROJ_CHUNK < COL_VR
    long_chunks = [c for c in range(n_chunks) if is_long(c)]
    short_chunks = [c for c in range(n_chunks) if not is_long(c)][::-1]
    order = [short_chunks.pop(0) for _ in range(3)]
    while long_chunks or short_chunks:
        if long_chunks:
            order.append(long_chunks.pop(0))
        if short_chunks:
            order.append(short_chunks.pop(0))
    for c in order:
        c0 = c * PROJ_CHUNK
        y = _dot(h_scr[...], w_ref[:, c0:c0 + PROJ_CHUNK]) * rms_scale
        if c0 < COL_KA:
            y = qk_norm_rope(y, gq_ref[...], DA_HALF ** -0.5)
        elif c0 < COL_VA:
            y = qk_norm_rope(y, gk_ref[...], 1.0)
        elif c0 < COL_QR:
            pass
        elif c0 < COL_KR:
            y = ret_rope(y, 1.0)
        elif c0 < COL_VR:
            y = ret_rope(y, RET_KDIM ** -0.5)
        elif c0 < COL_GB:
            pass
        elif c0 < COL_GATE_A:
            y = y * _sigmoid(y)
        else:
            y = _sigmoid(y)
        o_ref[:, c0:c0 + PROJ_CHUNK] = y.astype(o_ref.dtype)


def _in_proj(x2, pos2, g1, w_in, gq, gk):
    n = x2.shape[0]
    tm = min(PROJ_ROWS, n)
    grp = jnp.arange(PROJ_CHUNK) // DA_HALF
    gsum = (grp[:, None] == grp[None, :]).astype(jnp.bfloat16)
    half_a = ROPE_DIM // 2
    fa = jnp.power(jnp.float32(ROPE_THETA), -2.0 * jnp.arange(half_a, dtype=jnp.float32) / ROPE_DIM)[:, None]
    half_r = RET_KDIM // 2
    fr = jnp.power(jnp.float32(RET_THETA), -2.0 * jnp.arange(half_r, dtype=jnp.float32) / RET_KDIM)[:, None]
    j = jnp.arange(LANES)[:, None]
    l64 = (jnp.arange(LANES) % DA_HALF)[None, :]
    sel_c = (j < half_a) & (l64 < ROPE_DIM) & (l64 % half_a == j)
    sel_lo = (j >= half_a) & (j < ROPE_DIM) & (l64 < half_a) & (l64 == j - half_a)
    sel_hi = (j >= half_a) & (j < ROPE_DIM) & (l64 >= half_a) & (l64 < ROPE_DIM) & (l64 == j)
    sel = jnp.concatenate([sel_c.astype(jnp.float32), -sel_lo.astype(jnp.float32),
                           sel_hi.astype(jnp.float32)], axis=1).astype(jnp.bfloat16)
    reps = PROJ_CHUNK // DA_HALF
    full = lambda shape: pl.BlockSpec(shape, lambda i: (0,) * len(shape))
    return pl.pallas_call(
        _in_proj_kernel,
        grid=(n // tm,),
        in_specs=[
            pl.BlockSpec((tm, D_MODEL), lambda i: (i, 0)),
            pl.BlockSpec((1, tm), lambda i: (0, i)),
            full((1, D_MODEL)),
            full((D_MODEL, IN_COLS)),
            full((PROJ_CHUNK, PROJ_CHUNK)),
            full((1, PROJ_CHUNK)),
            full((1, PROJ_CHUNK)),
            full((half_a, 1)),
            full((half_r, 1)),
            full((LANES, 3 * LANES)),
        ],
        out_specs=pl.BlockSpec((tm, IN_COLS), lambda i: (i, 0)),
        out_shape=jax.ShapeDtypeStruct((n, IN_COLS), jnp.bfloat16),
        scratch_shapes=[pltpu.VMEM((tm, D_MODEL), jnp.bfloat16)],
        compiler_params=pltpu.CompilerParams(dimension_semantics=("arbitrary",),
                                             vmem_limit_bytes=VMEM_LIMIT),
        name="in_proj",
    )(x2, pos2, g1.reshape(1, D_MODEL), w_in.astype(jnp.bfloat16), gsum,
      jnp.tile(gq, reps)[None, :], jnp.tile(gk, reps)[None, :], fa, fr, sel)


def _diff_attn_kernel(q_ref, k_ref, v_ref, lam_ref, gsub_ref, o_ref,
                      qs_scr, vx_scr, s0_scr, s1_scr, p_scr, m_scr, alpha_scr, acc_scr):
    i = pl.program_id(2)
    t = q_ref.shape[0]

    @pl.when(i == 0)
    def _():
        vx_scr[:, :DA_VDIM] = v_ref[...]
        vx_scr[:, DA_VDIM:] = jnp.ones((vx_scr.shape[0], LANES), vx_scr.dtype)

    q = q_ref[...]
    lane = lax.broadcasted_iota(jnp.int32, q.shape, 1)
    zero = jnp.zeros_like(q)
    qs_scr[:t] = jnp.where(lane < DA_HALF, q, zero)
    qs_scr[t:] = jnp.where(lane >= DA_HALF, q, zero)

    def scores(j, s_ref):
        start = pl.multiple_of(j * t, t)
        s_ref[...] = _dot_nt(qs_scr[...], k_ref[pl.ds(start, t), :])

    def softmax_pv(j, s_ref, masked, first=False):
        for c in range(2 * t // ATT_ROWS):
            rows = pl.ds(c * ATT_ROWS, ATT_ROWS)
            s = s_ref[rows, :]
            if masked:
                r = lax.broadcasted_iota(jnp.int32, s.shape, 0) + (c * ATT_ROWS) % t
                col = lax.broadcasted_iota(jnp.int32, s.shape, 1)
                s = jnp.where(col <= r, s, -jnp.inf)
            m_cur = jnp.max(s, axis=-1, keepdims=True)
            if first:
                m_new = jnp.broadcast_to(m_cur, (ATT_ROWS, LANES))
            else:
                m_prev = m_scr[rows, :]
                m_new = jnp.maximum(m_prev, m_cur)
                alpha_scr[rows, :] = jnp.exp(m_prev - m_new)
            m_scr[rows, :] = m_new
            p = jnp.exp(s - jnp.concatenate([m_new] * (t // LANES), axis=1))
            p_scr[rows, :] = p.astype(p_scr.dtype)
        start = pl.multiple_of(j * t, t)
        pv = _dot(p_scr[...], vx_scr[pl.ds(start, t), :])
        if first:
            acc_scr[...] = pv
            return
        alpha = alpha_scr[...]
        for half in range(2):
            cols = pl.ds(half * LANES, LANES)
            acc_scr[:, cols] = alpha * acc_scr[:, cols] + pv[:, half * LANES:(half + 1) * LANES]

    scores(0, s0_scr)

    @pl.when(i == 0)
    def _():
        softmax_pv(0, s0_scr, True, first=True)

    @pl.when(i == 1)
    def _():
        scores(1, s1_scr)
        softmax_pv(0, s0_scr, False, first=True)
        softmax_pv(1, s1_scr, True)

    @pl.when(i >= 2)
    def _():
        scores(1, s1_scr)
        softmax_pv(0, s0_scr, False, first=True)
        scores(2, s0_scr)
        softmax_pv(1, s1_scr, False)

        def pair(jj, carry):
            j = 2 * jj
            scores(j + 1, s1_scr)
            softmax_pv(j, s0_scr, False)
            scores(j + 2, s0_scr)
            softmax_pv(j + 1, s1_scr, False)
            return carry

        lax.fori_loop(1, i // 2, pair, 0)

        @pl.when(i % 2 == 1)
        def _():
            scores(i, s1_scr)
            softmax_pv(i - 1, s0_scr, False)
            softmax_pv(i, s1_scr, True)

        @pl.when(i % 2 == 0)
        def _():
            softmax_pv(i, s0_scr, True)

    lam4 = lam_ref[...]
    lam = (jnp.exp(jnp.sum(lam4[0:1] * lam4[1:2], axis=-1, keepdims=True))
           - jnp.exp(jnp.sum(lam4[2:3] * lam4[3:4], axis=-1, keepdims=True)) + LAMBDA_INIT)
    o_all = acc_scr[:, :DA_VDIM] / acc_scr[:, DA_VDIM:]
    o = o_all[:t] - lam * o_all[t:]
    o = o * lax.rsqrt(jnp.mean(o * o, axis=-1, keepdims=True) + EPS) * gsub_ref[...] * (1.0 - LAMBDA_INIT)
    o_ref[...] = o.astype(o_ref.dtype)


def _diff_attn(proj, lam4, gsub, batch, seq):
    n = proj.shape[0]
    t = min(ATT_TILE, seq)
    nq = seq // t
    qb, kb, vb = COL_QA // LANES, COL_KA // LANES, COL_VA // LANES
    return pl.pallas_call(
        _diff_attn_kernel,
        grid=(batch, DA_HEADS, nq),
        in_specs=[
            pl.BlockSpec((t, LANES), lambda b, h, i: (b * nq + i, qb + h)),
            pl.BlockSpec((seq, LANES), lambda b, h, i: (b, kb + h)),
            pl.BlockSpec((seq, LANES), lambda b, h, i: (b, vb + h)),
            pl.BlockSpec((4, LANES), lambda b, h, i: (0, 0)),
            pl.BlockSpec((1, LANES), lambda b, h, i: (0, 0)),
        ],
        out_specs=pl.BlockSpec((t, LANES), lambda b, h, i: (b * nq + i, h)),
        out_shape=jax.ShapeDtypeStruct((n, DA_WIDTH), jnp.bfloat16),
        scratch_shapes=[pltpu.VMEM((2 * t, LANES), jnp.bfloat16),
                        pltpu.VMEM((seq, DA_VDIM + LANES), jnp.bfloat16),
                        pltpu.VMEM((2 * t, t), jnp.float32),
                        pltpu.VMEM((2 * t, t), jnp.float32),
                        pltpu.VMEM((2 * t, t), jnp.bfloat16),
                        pltpu.VMEM((2 * t, LANES), jnp.float32),
                        pltpu.VMEM((2 * t, LANES), jnp.float32),
                        pltpu.VMEM((2 * t, DA_VDIM + LANES), jnp.float32)],
        compiler_params=pltpu.CompilerParams(dimension_semantics=("arbitrary",) * 3,
                                             vmem_limit_bytes=VMEM_LIMIT),
        name="diff_attn",
    )(proj, proj, proj, lam4, gsub)


def _retention_kernel(q_ref, k_ref, v_ref, g_ref, gng_ref, gnb_ref, o_ref, r_scr, *, chunk):
    hf = jnp.full((1, 1), pl.program_id(1), jnp.int32).astype(jnp.float32)
    log_g = jnp.log1p(-jnp.exp2(-5.0 - hf))
    ri = lax.broadcasted_iota(jnp.int32, (chunk, chunk), 0)
    ci = lax.broadcasted_iota(jnp.int32, (chunk, chunk), 1)
    rel = (ri - ci).astype(jnp.float32)
    dmask = jnp.where(rel >= 0, jnp.exp(jnp.maximum(rel, 0.0) * log_g), 0.0)
    idx = lax.broadcasted_iota(jnp.int32, (chunk, 1), 0).astype(jnp.float32)
    zeta = jnp.exp((chunk - 1 - idx) * log_g)
    xi = jnp.exp((idx + 1.0) * log_g)
    g_chunk = jnp.exp(chunk * log_g)
    r_scr[...] = jnp.zeros(r_scr.shape, jnp.float32)
    gng = gng_ref[...]
    gnb = gnb_ref[...]

    def body(n, carry):
        start = pl.multiple_of(n * chunk, chunk)
        q = q_ref[pl.ds(start, chunk), :]
        k = k_ref[pl.ds(start, chunk), :]
        v = v_ref[pl.ds(start, chunk), :]
        s = _dot_nt(q, k) * dmask
        r_old = r_scr[...]
        o = _dot(s.astype(jnp.bfloat16), v) + xi * _dot(q, r_old.astype(jnp.bfloat16))
        kz = (k.astype(jnp.float32) * zeta).astype(jnp.bfloat16)
        r_scr[...] = g_chunk * r_old + _dot_tn(kz, v)
        mu = jnp.mean(o, axis=-1, keepdims=True)
        d = o - mu
        var = jnp.mean(d * d, axis=-1, keepdims=True)
        y = d * lax.rsqrt(var + EPS) * gng + gnb
        y = y * g_ref[pl.ds(start, chunk), :].astype(jnp.float32)
        o_ref[pl.ds(start, chunk), :] = y.astype(o_ref.dtype)
        return carry

    lax.fori_loop(0, q_ref.shape[0] // chunk, body, 0, unroll=RET_UNROLL)


def _retention(proj, gn_g, gn_b, batch, seq):
    n = proj.shape[0]
    chunk = min(RET_CHUNK, seq)
    col = lambda c0: (lambda b, h: (b, c0 // LANES + h))
    return pl.pallas_call(
        functools.partial(_retention_kernel, chunk=chunk),
        grid=(batch, RET_HEADS),
        in_specs=[
            pl.BlockSpec((seq, LANES), col(COL_QR)),
            pl.BlockSpec((seq, LANES), col(COL_KR)),
            pl.BlockSpec((seq, LANES), col(COL_VR)),
            pl.BlockSpec((seq, LANES), col(COL_GB)),
            pl.BlockSpec((1, LANES), lambda b, h: (0, h)),
            pl.BlockSpec((1, LANES), lambda b, h: (0, h)),
        ],
        out_specs=pl.BlockSpec((seq, LANES), lambda b, h: (b, h)),
        out_shape=jax.ShapeDtypeStruct((n, RET_WIDTH), jnp.bfloat16),
        scratch_shapes=[pltpu.VMEM((RET_KDIM, RET_VDIM), jnp.float32)],
        compiler_params=pltpu.CompilerParams(dimension_semantics=("arbitrary",) * 2,
                                             vmem_limit_bytes=VMEM_LIMIT),
        name="retention",
    )(proj, proj, proj, proj, gn_g.reshape(1, RET_WIDTH), gn_b.reshape(1, RET_WIDTH))


def _merge_kernel(x_ref, oa_ref, ob_ref, sa0_ref, sa1_ref, sb0_ref, sb1_ref, wa_ref, wb_ref, wo_ref,
                  g2_ref, wr_hi_ref, wr_lo_ref, br_ref, tri_ref, x1_ref, h2_ref, route_ref, cols_ref, counts_ref,
                  base_scr, logits_scr):
    i = pl.program_id(0)

    @pl.when(i == 0)
    def _():
        base_scr[...] = jnp.zeros(base_scr.shape, jnp.float32)
        logits_scr[...] = jnp.zeros(logits_scr.shape, jnp.float32)

    ya = _dot(oa_ref[...], wa_ref[...])
    yb = _dot(ob_ref[...], wb_ref[...])

    logits = logits_scr[...]
    lane = lax.broadcasted_iota(jnp.int32, logits.shape, 1)
    neg = -jnp.inf
    gl = jnp.where(lane < N_GROUPS, logits, neg)
    gmax = jnp.max(gl, axis=-1, keepdims=True)
    g_idx = jnp.min(jnp.where(gl == gmax, lane, LANES), axis=-1, keepdims=True)
    p_g = 1.0 / jnp.sum(jnp.exp(gl - gmax), axis=-1, keepdims=True)
    e_lo = N_GROUPS + EXPERTS_PER_GROUP * g_idx
    el = jnp.where((lane >= e_lo) & (lane < e_lo + EXPERTS_PER_GROUP), logits, neg)
    v1 = jnp.max(el, axis=-1, keepdims=True)
    i1 = jnp.min(jnp.where(el == v1, lane, LANES), axis=-1, keepdims=True)
    el2 = jnp.where(lane == i1, neg, el)
    v2 = jnp.max(el2, axis=-1, keepdims=True)
    i2 = jnp.min(jnp.where(el2 == v2, lane, LANES), axis=-1, keepdims=True)
    t = jnp.exp(v2 - v1)
    w1 = p_g / (1.0 + t)
    w2 = p_g * t / (1.0 + t)
    e1 = i1 - N_GROUPS
    e2 = i2 - N_GROUPS

    oh1 = lane == e1
    oh2 = lane == e2
    real = jnp.where(i > 0, 1.0, 0.0)
    picked = jnp.where(oh1 | oh2, real, 0.0)
    before = _dot(tri_ref[...], picked.astype(jnp.bfloat16)) + base_scr[0:1, :]
    rank1 = jnp.sum(jnp.where(oh1, before, 0.0), axis=-1, keepdims=True)
    rank2 = jnp.sum(jnp.where(oh2, before, 0.0), axis=-1, keepdims=True)
    base_scr[...] = base_scr[...] + jnp.sum(picked, axis=0, keepdims=True)
    counts_ref[...] = base_scr[...]

    cols = [e1.astype(jnp.float32), e2.astype(jnp.float32), w1, w2, rank1, rank2]
    route = jnp.zeros(logits.shape, jnp.float32)
    for c, val in enumerate(cols):
        route = jnp.where(lane == c, val, route)
    route_ref[...] = route
    cols_ref[...] = route.T[:8]

    sa = jnp.concatenate([sa0_ref[...], sa1_ref[...]], axis=1).astype(jnp.float32)
    sb = jnp.concatenate([sb0_ref[...], sb1_ref[...]], axis=1).astype(jnp.float32)
    merged = sa * ya + sb * yb
    x1 = x_ref[...] + _dot(merged.astype(jnp.bfloat16), wo_ref[...])
    x1_ref[...] = x1
    h2 = x1 * lax.rsqrt(jnp.mean(x1 * x1, axis=-1, keepdims=True) + EPS) * g2_ref[...]
    _store_packed(h2_ref, h2)

    hi = h2.astype(jnp.bfloat16)
    lo = (h2 - hi.astype(jnp.float32)).astype(jnp.bfloat16)
    logits_scr[...] = (_dot(hi, wr_hi_ref[...]) + _dot(lo, wr_hi_ref[...]) + _dot(hi, wr_lo_ref[...])
                       + br_ref[...])


def _merge(x2, oa, ob, proj, wa, wb, wo, g2, w_gr, b_gr, w_er, b_er):
    n = x2.shape[0]
    tm = min(PROJ_ROWS, n)
    half = D_MODEL // 2
    pad = LANES - N_GROUPS - N_EXPERTS
    wr = jnp.concatenate([w_gr, w_er, jnp.zeros((D_MODEL, pad), jnp.float32)], axis=1)
    wr_hi = wr.astype(jnp.bfloat16)
    wr_lo = (wr - wr_hi.astype(jnp.float32)).astype(jnp.bfloat16)
    br = jnp.concatenate([b_gr, b_er, jnp.zeros((pad,), jnp.float32)])[None, :]
    tri = (jnp.arange(tm)[:, None] > jnp.arange(tm)[None, :]).astype(jnp.bfloat16)
    full = lambda shape: pl.BlockSpec(shape, lambda i: (0,) * len(shape))
    nt = n // tm
    cur = lambda i: jnp.minimum(i, nt - 1)
    gate = lambda c0: pl.BlockSpec((tm, half), lambda i: (cur(i), c0 // half))
    return pl.pallas_call(
        _merge_kernel,
        grid=(nt + 1,),
        in_specs=[
            pl.BlockSpec((tm, D_MODEL), lambda i: (cur(i), 0)),
            pl.BlockSpec((tm, DA_WIDTH), lambda i: (cur(i), 0)),
            pl.BlockSpec((tm, RET_WIDTH), lambda i: (cur(i), 0)),
            gate(COL_GATE_A), gate(COL_GATE_A + half), gate(COL_GATE_B), gate(COL_GATE_B + half),
            full((DA_WIDTH, D_MODEL)), full((RET_WIDTH, D_MODEL)), full((D_MODEL, D_MODEL)),
            full((1, D_MODEL)), full((D_MODEL, LANES)), full((D_MODEL, LANES)), full((1, LANES)),
            full((tm, tm)),
        ],
        out_specs=[
            pl.BlockSpec((tm, D_MODEL), lambda i: (cur(i), 0)),
            pl.BlockSpec((ROW_PIECES, tm, PIECE), lambda i: (0, cur(i), 0)),
            pl.BlockSpec((tm, LANES), lambda i: (jnp.maximum(i - 1, 0), 0)),
            pl.BlockSpec((8, tm), lambda i: (0, jnp.maximum(i - 1, 0))),
            pl.BlockSpec((8, LANES), lambda i: (0, 0)),
        ],
        out_shape=[
            jax.ShapeDtypeStruct((n, D_MODEL), jnp.float32),
            jax.ShapeDtypeStruct((ROW_PIECES, n, PIECE), jnp.uint32),
            jax.ShapeDtypeStruct((n, LANES), jnp.float32),
            jax.ShapeDtypeStruct((8, n), jnp.float32),
            jax.ShapeDtypeStruct((8, LANES), jnp.float32),
        ],
        scratch_shapes=[pltpu.VMEM((8, LANES), jnp.float32), pltpu.VMEM((tm, LANES), jnp.float32)],
        compiler_params=pltpu.CompilerParams(dimension_semantics=("arbitrary",),
                                             vmem_limit_bytes=VMEM_LIMIT),
        name="merge",
    )(x2, oa, ob, proj, proj, proj, proj, wa.astype(jnp.bfloat16), wb.astype(jnp.bfloat16),
      wo.astype(jnp.bfloat16), g2.reshape(1, D_MODEL), wr_hi, wr_lo, br, tri)


def _sc_mesh():
    return plsc.VectorSubcoreMesh(core_axis_name="c", subcore_axis_name="s")


def _sc_scatter_rows(src, idx, out_rows, src_block):
    steps = idx.shape[1] // SC_WINDOW
    per_core = steps // SC_CORES

    @pl.kernel(out_type=jax.ShapeDtypeStruct((out_rows, PIECE), src.dtype), mesh=_sc_mesh())
    def scatter(src_hbm, idx_hbm, out_hbm):
        def body(src_vmem, idx_vmem):
            pltpu.sync_copy(src_vmem, out_hbm.at[idx_vmem.at[0]])

        pltpu.emit_pipeline(
            body,
            grid=(SC_CORES, per_core),
            in_specs=[pl.BlockSpec((SC_WINDOW, PIECE), lambda c, i: (src_block(c * per_core + i), 0)),
                      pl.BlockSpec((1, SC_WINDOW), lambda c, i: (0, c * per_core + i))],
            out_specs=[],
            core_axis_name=("c", "s"),
            dimension_semantics=(pltpu.PARALLEL, pltpu.PARALLEL),
        )(src_hbm, idx_hbm)

    return scatter(src, idx)


def _sc_gather_rows(table, idx):
    num = idx.shape[1]
    per_core = num // SC_WINDOW // SC_CORES

    @pl.kernel(out_type=jax.ShapeDtypeStruct((num, PIECE), table.dtype), mesh=_sc_mesh())
    def gather(table_hbm, idx_hbm, out_hbm):
        def body(idx_vmem, out_vmem):
            pltpu.sync_copy(table_hbm.at[idx_vmem.at[0]], out_vmem)

        pltpu.emit_pipeline(
            body,
            grid=(SC_CORES, per_core),
            in_specs=[pl.BlockSpec((1, SC_WINDOW), lambda c, i: (0, c * per_core + i))],
            out_specs=[pl.BlockSpec((SC_WINDOW, PIECE), lambda c, i: (c * per_core + i, 0))],
            core_axis_name=("c", "s"),
            dimension_semantics=(pltpu.PARALLEL, pltpu.PARALLEL),
        )(idx_hbm, out_hbm)

    return gather(table, idx)


def _store_packed(ref, val):
    as_bits = lambda v: lax.bitcast_convert_type(v.astype(jnp.bfloat16).astype(jnp.float32), jnp.uint32)
    words = (as_bits(val[:, :PACKED]) >> 16) | (as_bits(val[:, PACKED:]) & jnp.uint32(0xFFFF0000))
    for j in range(ROW_PIECES):
        ref[j, :val.shape[0], :] = words[:, j * PIECE:(j + 1) * PIECE]


def _load_packed(ref, rows=None):
    rows = ref.shape[1] if rows is None else rows
    words = jnp.concatenate([ref[j, :rows, :] for j in range(ROW_PIECES)], axis=1)
    low = lax.bitcast_convert_type(words << 16, jnp.float32)
    high = lax.bitcast_convert_type(words & jnp.uint32(0xFFFF0000), jnp.float32)
    return jnp.concatenate([low, high], axis=1)


def _expert_kernel(blk_e_ref, n_used_ref, nxt_ref, run_ref, rows_ref, x_ref, wg_hbm, wu_hbm, wd_hbm, o_ref,
                   wg_stage, wu_stage, wd_stage, wg_scr, wu_scr, wd_scr, sem):
    i = pl.program_id(0)
    used = i < n_used_ref[0]

    def weight_copies(e, s):
        return (pltpu.make_async_copy(wg_hbm.at[e], wg_stage.at[s], sem.at[s, 0]),
                pltpu.make_async_copy(wu_hbm.at[e], wu_stage.at[s], sem.at[s, 1]),
                pltpu.make_async_copy(wd_hbm.at[e], wd_stage.at[s], sem.at[s, 2]))

    @pl.when(i == 0)
    def _():
        for c in weight_copies(blk_e_ref[0], 0):
            c.start()

    @pl.when(used & ((i == 0) | (blk_e_ref[i] != blk_e_ref[jnp.maximum(i - 1, 0)])))
    def _():
        s = run_ref[i] % 2
        for c in weight_copies(blk_e_ref[i], s):
            c.wait()
        wg_scr[...] = wg_stage[s].astype(jnp.bfloat16)
        wu_scr[...] = wu_stage[s].astype(jnp.bfloat16)
        wd_scr[...] = wd_stage[s].astype(jnp.bfloat16)

        @pl.when(nxt_ref[i] >= 0)
        def _():
            for c in weight_copies(nxt_ref[i], 1 - s):
                c.start()

    def expert_mlp(rows):
        x = _load_packed(x_ref, rows).astype(jnp.bfloat16)
        a = _dot(x, wg_scr[...])
        u = _dot(x, wu_scr[...])
        hmid = (a * _sigmoid(a) * u).astype(jnp.bfloat16)
        _store_packed(o_ref, _dot(hmid, wd_scr[...]))

    half = MOE_BLOCK // 2
    short = rows_ref[i] <= half

    @pl.when(used & jnp.logical_not(short))
    def _():
        expert_mlp(MOE_BLOCK)

    @pl.when(used & short)
    def _():
        expert_mlp(half)
        o_ref[:, half:, :] = jnp.zeros((ROW_PIECES, MOE_BLOCK - half, PIECE), o_ref.dtype)

    @pl.when(jnp.logical_not(used))
    def _():
        o_ref[...] = jnp.zeros(o_ref.shape, o_ref.dtype)


def _experts(xs, blk_expert, blk_rows, n_used, w_gate, w_up, w_down):
    p = xs.shape[1]
    nblk = p // MOE_BLOCK
    idx = jnp.arange(nblk, dtype=jnp.int32)
    starts = (idx < n_used[0]) & ((idx == 0) | (blk_expert != jnp.roll(blk_expert, 1)))
    run = jnp.cumsum(starts.astype(jnp.int32)) - 1
    next_start = lax.cummin(jnp.where(starts, idx, nblk)[::-1])[::-1]
    after = jnp.concatenate([next_start[1:], jnp.full((1,), nblk, jnp.int32)])
    nxt = jnp.where(after < nblk, blk_expert[jnp.minimum(after, nblk - 1)], -1).astype(jnp.int32)
    live = lambda i, be, nu, *_: jnp.minimum(i, nu[0] - 1)
    any_spec = pl.BlockSpec(memory_space=pl.ANY)
    return pl.pallas_call(
        _expert_kernel,
        grid_spec=pltpu.PrefetchScalarGridSpec(
            num_scalar_prefetch=5,
            grid=(nblk,),
            in_specs=[
                pl.BlockSpec((ROW_PIECES, MOE_BLOCK, PIECE), lambda i, *pre: (0, live(i, *pre), 0)),
                any_spec, any_spec, any_spec,
            ],
            out_specs=pl.BlockSpec((ROW_PIECES, MOE_BLOCK, PIECE), lambda i, *pre: (0, i, 0)),
            scratch_shapes=[pltpu.VMEM((2, D_MODEL, EXPERT_FF), jnp.float32),
                            pltpu.VMEM((2, D_MODEL, EXPERT_FF), jnp.float32),
                            pltpu.VMEM((2, EXPERT_FF, D_MODEL), jnp.float32),
                            pltpu.VMEM((D_MODEL, EXPERT_FF), jnp.bfloat16),
                            pltpu.VMEM((D_MODEL, EXPERT_FF), jnp.bfloat16),
                            pltpu.VMEM((EXPERT_FF, D_MODEL), jnp.bfloat16),
                            pltpu.SemaphoreType.DMA((2, 3))],
        ),
        out_shape=jax.ShapeDtypeStruct((ROW_PIECES, p, PIECE), jnp.uint32),
        compiler_params=pltpu.CompilerParams(dimension_semantics=("arbitrary",),
                                             vmem_limit_bytes=VMEM_LIMIT),
        name="experts",
    )(blk_expert, n_used, nxt, run.astype(jnp.int32), blk_rows, xs, w_gate, w_up, w_down)


def _combine_kernel(x1_ref, route_ref, y0_ref, y1_ref, *rest):
    o_ref = rest[-1]
    route = route_ref[...]
    o_ref[...] = x1_ref[...] + route[:, 2:3] * _load_packed(y0_ref) + route[:, 3:4] * _load_packed(y1_ref)


def _combine(x1, yg, route, row0, out_prev):
    n = yg.shape[1] // TOP_K
    tm = min(PROJ_ROWS, n)
    blk0 = row0 // tm
    prev = () if out_prev is None else (out_prev,)
    return pl.pallas_call(
        _combine_kernel,
        grid=(n // tm,),
        in_specs=[
            pl.BlockSpec((tm, D_MODEL), lambda i: (blk0 + i, 0)),
            pl.BlockSpec((tm, LANES), lambda i: (blk0 + i, 0)),
            pl.BlockSpec((ROW_PIECES, tm, PIECE), lambda i: (0, i, 0)),
            pl.BlockSpec((ROW_PIECES, tm, PIECE), lambda i: (0, i + n // tm, 0)),
        ] + [pl.BlockSpec(memory_space=pl.ANY)] * len(prev),
        out_specs=pl.BlockSpec((tm, D_MODEL), lambda i: (blk0 + i, 0)),
        out_shape=jax.ShapeDtypeStruct(x1.shape, jnp.float32),
        input_output_aliases={4: 0} if prev else {},
        compiler_params=pltpu.CompilerParams(dimension_semantics=("arbitrary",),
                                             vmem_limit_bytes=VMEM_LIMIT),
        name="combine",
    )(x1, route, yg, yg, *prev)


def _dispatch_plan(route_cols, counts, n):
    counts = counts[0, :N_EXPERTS].astype(jnp.int32)
    padded = ((counts + MOE_BLOCK - 1) // MOE_BLOCK) * MOE_BLOCK
    seg_end = jnp.cumsum(padded).astype(jnp.int32)
    seg_start = seg_end - padded
    cols = route_cols.astype(jnp.int32)
    e, rank = cols[0:TOP_K], cols[4:4 + TOP_K]
    picked = e[None] == jnp.arange(N_EXPERTS, dtype=jnp.int32)[:, None, None]
    dest = jnp.sum(jnp.where(picked, seg_start[:, None, None], 0), axis=0) + rank
    p = n * TOP_K + N_EXPERTS * MOE_BLOCK
    slot = dest[None] + (jnp.arange(ROW_PIECES, dtype=jnp.int32) * p)[:, None, None]
    blk_start = jnp.arange(p // MOE_BLOCK, dtype=jnp.int32) * MOE_BLOCK
    blk_expert = jnp.sum((seg_end[None, :] <= blk_start[:, None]).astype(jnp.int32), axis=1)
    blk_expert = jnp.minimum(blk_expert, N_EXPERTS - 1)
    of_block = blk_expert[:, None] == jnp.arange(N_EXPERTS, dtype=jnp.int32)[None, :]
    real_end = jnp.sum(jnp.where(of_block, (seg_start + counts)[None, :], 0), axis=1)
    blk_rows = jnp.clip(real_end - blk_start, 0, MOE_BLOCK)
    n_used = (seg_end[-1] // MOE_BLOCK).reshape(1)
    return slot, blk_expert, blk_rows, n_used, p


def _layer(x, positions, norm1_g, w_in, q_norm_g, k_norm_g, lam4, diff_subln_g, ret_gn_g, ret_gn_b,
           w_branch_a, w_branch_b, w_out, norm2_g, w_gr, b_gr, w_er, b_er, w_gate, w_up, w_down):
    batch, seq, _ = x.shape
    n = batch * seq
    x2 = x.reshape(n, D_MODEL)
    proj = _in_proj(x2, positions.reshape(1, n), norm1_g, w_in, q_norm_g, k_norm_g)
    oa = _diff_attn(proj, lam4, diff_subln_g.reshape(1, DA_VDIM), batch, seq)
    ob = _retention(proj, ret_gn_g, ret_gn_b, batch, seq)
    x1, h2, route, route_cols, counts = _merge(x2, oa, ob, proj, w_branch_a, w_branch_b, w_out, norm2_g,
                                               w_gr, b_gr, w_er, b_er)
    slot, blk_expert, blk_rows, n_used, p = _dispatch_plan(route_cols, counts, n)
    win_n = n // SC_WINDOW
    src_block = lambda s: (s // (TOP_K * win_n)) * win_n + s % win_n
    xs = _sc_scatter_rows(h2.reshape(ROW_PIECES * n, PIECE), slot.reshape(1, -1), ROW_PIECES * p, src_block)
    ys = _experts(xs.reshape(ROW_PIECES, p, PIECE), blk_expert, blk_rows, n_used, w_gate, w_up, w_down)
    parts = COMBINE_PARTS if n % (COMBINE_PARTS * PROJ_ROWS) == 0 else 1
    m = n // parts
    out = None
    for t in range(parts):
        yg = _sc_gather_rows(ys.reshape(ROW_PIECES * p, PIECE), slot[:, :, t * m:(t + 1) * m].reshape(1, -1))
        out = _combine(x1, yg.reshape(ROW_PIECES, TOP_K * m, PIECE), route, t * m, out)
    return out.reshape(batch, seq, D_MODEL)


def kernel(x, positions, norm1_g, w_in, q_norm_g, k_norm_g, lambda_q1, lambda_k1, lambda_q2, lambda_k2, diff_subln_g, ret_gn_g, ret_gn_b, w_branch_a, w_branch_b, w_out, norm2_g, w_group_router, b_group_router, w_expert_router, b_expert_router, w_gate, w_up, w_down):
    assert x.shape[-1] == D_MODEL and norm1_g.shape[0] == 1, "single-layer, D_MODEL-wide input expected"
    lam4 = jnp.pad(jnp.stack([lambda_q1[0], lambda_k1[0], lambda_q2[0], lambda_k2[0]]), ((0, 0), (0, LANES - DA_HALF)))
    return _layer(x, positions, norm1_g[0], w_in[0], q_norm_g[0], k_norm_g[0], lam4, diff_subln_g[0],
                  ret_gn_g[0], ret_gn_b[0], w_branch_a[0], w_branch_b[0], w_out[0], norm2_g[0],
                  w_group_router[0], b_group_router[0], w_expert_router[0], b_expert_router[0],
                  w_gate[0], w_up[0], w_down[0])
```

```python
import functools
import math

import jax
import jax.numpy as jnp
from jax import lax
from jax.experimental import pallas as pl
from jax.experimental.pallas import tpu as pltpu
from jax.experimental.pallas import tpu_sc as plsc

D_MODEL = 1024
DA_HEADS = 4
DA_HALF = 64
DA_VDIM = 2 * DA_HALF
DA_WIDTH = DA_HEADS * DA_VDIM
ROPE_THETA = 500000.0
ROPE_DIM = DA_HALF // 4
RET_HEADS = 4
RET_KDIM = 128
RET_VDIM = 128
RET_WIDTH = RET_HEADS * RET_VDIM
RET_THETA = 10000.0
N_GROUPS = 4
EXPERTS_PER_GROUP = 8
N_EXPERTS = N_GROUPS * EXPERTS_PER_GROUP
TOP_K = 2
EXPERT_FF = 512
EPS = 1e-6
LAMBDA_INIT = 0.8 - 0.6 * math.exp(-0.3 * 0)

LANES = 128
IN_COLS = 3 * DA_WIDTH + 4 * RET_WIDTH + 2 * D_MODEL
COL_QA, COL_KA, COL_VA = 0, DA_WIDTH, 2 * DA_WIDTH
COL_QR = 3 * DA_WIDTH
COL_KR = COL_QR + RET_WIDTH
COL_VR = COL_KR + RET_WIDTH
COL_GB = COL_VR + RET_WIDTH
COL_GATE_A = COL_GB + RET_WIDTH
COL_GATE_B = COL_GATE_A + D_MODEL

PROJ_ROWS = 512
IN_PROJ_ROWS = 1024
PROJ_CHUNK = 256
ATT_TILE = 1024
ATT_ROWS = 32
RET_CHUNK = 256
RET_UNROLL = 16
MOE_BLOCK = 512
PACKED = D_MODEL // 2
ROW_PIECES = 2
PIECE = PACKED // ROW_PIECES
SC_CORES = 2
SC_WINDOW = 128
COMBINE_PARTS = 2
VMEM_LIMIT = 56 * 1024 * 1024


def _dot(a, b):
    return jnp.dot(a, b, preferred_element_type=jnp.float32)


def _dot_nt(a, b):
    return lax.dot_general(a, b, (((1,), (1,)), ((), ())), preferred_element_type=jnp.float32)


def _dot_tn(a, b):
    return lax.dot_general(a, b, (((0,), (0,)), ((), ())), preferred_element_type=jnp.float32)


def _sigmoid(x):
    return 0.5 * jnp.tanh(0.5 * x) + 0.5


def _split3(x):
    a = x.astype(jnp.bfloat16)
    r = x - a.astype(jnp.float32)
    b = r.astype(jnp.bfloat16)
    c = (r - b.astype(jnp.float32)).astype(jnp.bfloat16)
    return a, b, c


def _in_proj_kernel(x_ref, pos_ref, g1_ref, w_ref, gsum_ref, gq_ref, gk_ref, fa_ref, fr_ref, sel_ref,
                    o_ref, h_scr):
    x = x_ref[...]
    rows = x.shape[0]
    h_scr[...] = (x * g1_ref[...]).astype(jnp.bfloat16)
    rms_scale = jnp.broadcast_to(lax.rsqrt(jnp.mean(x * x, axis=-1, keepdims=True) + EPS), (rows, PROJ_CHUNK))
    pos = pos_ref[...].astype(jnp.float32)

    lane = lax.broadcasted_iota(jnp.int32, (rows, LANES), 1)
    half_a = ROPE_DIM // 2
    tables = {}

    def da_tables():
        if "da" not in tables:
            ang_a = fa_ref[...] * pos
            pad = jnp.zeros((LANES - 2 * half_a, rows), jnp.float32)
            t_a = jnp.concatenate([jnp.cos(ang_a), jnp.sin(ang_a), pad], axis=0).T
            tab = sum(_dot(part, sel_ref[...]) for part in _split3(t_a))
            c_a = tab[:, :LANES] + jnp.where(lane % DA_HALF < ROPE_DIM, 0.0, 1.0)
            s_lo = tab[:, LANES:2 * LANES]
            s_hi = tab[:, 2 * LANES:]
            tables["da"] = tuple(jnp.concatenate([v, v], axis=1) for v in (c_a, s_lo, s_hi))
        return tables["da"]

    def ret_tables():
        if "ret" not in tables:
            ang_r = fr_ref[...] * pos
            t_r = jnp.concatenate([jnp.cos(ang_r), jnp.sin(ang_r)], axis=0).T
            sw_r = pltpu.roll(t_r, RET_KDIM // 2, axis=1)
            first = lane < RET_KDIM // 2
            c_r = jnp.where(first, t_r, sw_r)
            s_r = jnp.where(first, -sw_r, t_r)
            tables["ret"] = tuple(jnp.concatenate([v, v], axis=1) for v in (c_r, s_r))
        return tables["ret"]

    def qk_norm_rope(y, g, scale):
        c_a2, s_lo2, s_hi2 = da_tables()
        ss = y * y
        hi = ss.astype(jnp.bfloat16)
        lo = (ss - hi.astype(jnp.float32)).astype(jnp.bfloat16)
        gs = _dot(hi, gsum_ref[...]) + _dot(lo, gsum_ref[...])
        n = y * lax.rsqrt(gs * (1.0 / DA_HALF) + EPS) * g
        up = pltpu.roll(n, PROJ_CHUNK - half_a, axis=1)
        dn = pltpu.roll(n, half_a, axis=1)
        r = n * c_a2 + up * s_lo2 + dn * s_hi2
        return r * scale if scale != 1.0 else r

    def ret_rope(y, scale):
        c_r2, s_r2 = ret_tables()
        halves = [pltpu.roll(y[:, i * LANES:(i + 1) * LANES], RET_KDIM // 2, axis=1)
                  for i in range(PROJ_CHUNK // LANES)]
        sw = jnp.concatenate(halves, axis=1)
        r = y * c_r2 + sw * s_r2
        return r * scale if scale != 1.0 else r

    n_chunks = IN_COLS // PROJ_CHUNK
    is_long = lambda c: c * PROJ_CHUNK < COL_VA or COL_QR <= c * PROJ_CHUNK < COL_VR
    long_chunks = [c for c in range(n_chunks) if is_long(c)]
    short_chunks = [c for c in range(n_chunks) if not is_long(c)][::-1]
    order = [short_chunks.pop(0) for _ in range(3)]
    while long_chunks or short_chunks:
        if long_chunks:
            order.append(long_chunks.pop(0))
        if short_chunks:
            order.append(short_chunks.pop(0))
    for c in order:
        c0 = c * PROJ_CHUNK
        y = _dot(h_scr[...], w_ref[:, c0:c0 + PROJ_CHUNK]) * rms_scale
        if c0 < COL_KA:
            y = qk_norm_rope(y, gq_ref[...], DA_HALF ** -0.5)
        elif c0 < COL_VA:
            y = qk_norm_rope(y, gk_ref[...], 1.0)
        elif c0 < COL_QR:
            pass
        elif c0 < COL_KR:
            y = ret_rope(y, 1.0)
        elif c0 < COL_VR:
            y = ret_rope(y, RET_KDIM ** -0.5)
        elif c0 < COL_GB:
            pass
        elif c0 < COL_GATE_A:
            y = y * _sigmoid(y)
        else:
            y = _sigmoid(y)
        o_ref[:, c0:c0 + PROJ_CHUNK] = y.astype(o_ref.dtype)


def _in_proj(x2, pos2, g1, w_in, gq, gk):
    n = x2.shape[0]
    tm = min(IN_PROJ_ROWS, n)
    grp = jnp.arange(PROJ_CHUNK) // DA_HALF
    gsum = (grp[:, None] == grp[None, :]).astype(jnp.bfloat16)
    half_a = ROPE_DIM // 2
    fa = jnp.power(jnp.float32(ROPE_THETA), -2.0 * jnp.arange(half_a, dtype=jnp.float32) / ROPE_DIM)[:, None]
    half_r = RET_KDIM // 2
    fr = jnp.power(jnp.float32(RET_THETA), -2.0 * jnp.arange(half_r, dtype=jnp.float32) / RET_KDIM)[:, None]
    j = jnp.arange(LANES)[:, None]
    l64 = (jnp.arange(LANES) % DA_HALF)[None, :]
    sel_c = (j < half_a) & (l64 < ROPE_DIM) & (l64 % half_a == j)
    sel_lo = (j >= half_a) & (j < ROPE_DIM) & (l64 < half_a) & (l64 == j - half_a)
    sel_hi = (j >= half_a) & (j < ROPE_DIM) & (l64 >= half_a) & (l64 < ROPE_DIM) & (l64 == j)
    sel = jnp.concatenate([sel_c.astype(jnp.float32), -sel_lo.astype(jnp.float32),
                           sel_hi.astype(jnp.float32)], axis=1).astype(jnp.bfloat16)
    reps = PROJ_CHUNK // DA_HALF
    full = lambda shape: pl.BlockSpec(shape, lambda i: (0,) * len(shape))
    return pl.pallas_call(
        _in_proj_kernel,
        grid=(n // tm,),
        in_specs=[
            pl.BlockSpec((tm, D_MODEL), lambda i: (i, 0)),
            pl.BlockSpec((1, tm), lambda i: (0, i)),
            full((1, D_MODEL)),
            pl.BlockSpec((D_MODEL, IN_COLS), lambda i: (0, 0), pipeline_mode=pl.Buffered(1)),
            full((PROJ_CHUNK, PROJ_CHUNK)),
            full((1, PROJ_CHUNK)),
            full((1, PROJ_CHUNK)),
            full((half_a, 1)),
            full((half_r, 1)),
            full((LANES, 3 * LANES)),
        ],
        out_specs=pl.BlockSpec((tm, IN_COLS), lambda i: (i, 0)),
        out_shape=jax.ShapeDtypeStruct((n, IN_COLS), jnp.bfloat16),
        scratch_shapes=[pltpu.VMEM((tm, D_MODEL), jnp.bfloat16)],
        compiler_params=pltpu.CompilerParams(dimension_semantics=("arbitrary",),
                                             vmem_limit_bytes=VMEM_LIMIT),
        name="in_proj",
    )(x2, pos2, g1.reshape(1, D_MODEL), w_in.astype(jnp.bfloat16), gsum,
      jnp.tile(gq, reps)[None, :], jnp.tile(gk, reps)[None, :], fa, fr, sel)


def _diff_attn_kernel(q_ref, k_ref, v_ref, lam_ref, gsub_ref, o_ref,
                      qs_scr, vx_scr, s0_scr, s1_scr, p_scr, m_scr, alpha_scr, acc_scr):
    i = pl.program_id(2)
    t = q_ref.shape[0]

    @pl.when(i == 0)
    def _():
        vx_scr[:, :DA_VDIM] = v_ref[...]
        vx_scr[:, DA_VDIM:] = jnp.ones((vx_scr.shape[0], LANES), vx_scr.dtype)

    q = q_ref[...]
    lane = lax.broadcasted_iota(jnp.int32, q.shape, 1)
    zero = jnp.zeros_like(q)
    qs_scr[:t] = jnp.where(lane < DA_HALF, q, zero)
    qs_scr[t:] = jnp.where(lane >= DA_HALF, q, zero)

    def scores(j, s_ref):
        start = pl.multiple_of(j * t, t)
        s_ref[...] = _dot_nt(qs_scr[...], k_ref[pl.ds(start, t), :])

    def softmax_pv(j, s_ref, masked, first=False):
        for c in range(2 * t // ATT_ROWS):
            rows = pl.ds(c * ATT_ROWS, ATT_ROWS)
            s = s_ref[rows, :]
            if masked:
                r = lax.broadcasted_iota(jnp.int32, s.shape, 0) + (c * ATT_ROWS) % t
                col = lax.broadcasted_iota(jnp.int32, s.shape, 1)
                s = jnp.where(col <= r, s, -jnp.inf)
            m_cur = jnp.max(s, axis=-1, keepdims=True)
            if first:
                m_new = jnp.broadcast_to(m_cur, (ATT_ROWS, LANES))
            else:
                m_prev = m_scr[rows, :]
                m_new = jnp.maximum(m_prev, m_cur)
                alpha_scr[rows, :] = jnp.exp(m_prev - m_new)
            m_scr[rows, :] = m_new
            p = jnp.exp(s - jnp.concatenate([m_new] * (t // LANES), axis=1))
            p_scr[rows, :] = p.astype(p_scr.dtype)
        start = pl.multiple_of(j * t, t)
        pv = _dot(p_scr[...], vx_scr[pl.ds(start, t), :])
        if first:
            acc_scr[...] = pv
            return
        alpha = alpha_scr[...]
        for half in range(2):
            cols = pl.ds(half * LANES, LANES)
            acc_scr[:, cols] = alpha * acc_scr[:, cols] + pv[:, half * LANES:(half + 1) * LANES]

    scores(0, s0_scr)

    @pl.when(i == 0)
    def _():
        softmax_pv(0, s0_scr, True, first=True)

    @pl.when(i == 1)
    def _():
        scores(1, s1_scr)
        softmax_pv(0, s0_scr, False, first=True)
        softmax_pv(1, s1_scr, True)

    @pl.when(i >= 2)
    def _():
        scores(1, s1_scr)
        softmax_pv(0, s0_scr, False, first=True)
        scores(2, s0_scr)
        softmax_pv(1, s1_scr, False)

        def pair(jj, carry):
            j = 2 * jj
            scores(j + 1, s1_scr)
            softmax_pv(j, s0_scr, False)
            scores(j + 2, s0_scr)
            softmax_pv(j + 1, s1_scr, False)
            return carry

        lax.fori_loop(1, i // 2, pair, 0)

        @pl.when(i % 2 == 1)
        def _():
            scores(i, s1_scr)
            softmax_pv(i - 1, s0_scr, False)
            softmax_pv(i, s1_scr, True)

        @pl.when(i % 2 == 0)
        def _():
            softmax_pv(i, s0_scr, True)

    lam4 = lam_ref[...]
    lam = (jnp.exp(jnp.sum(lam4[0:1] * lam4[1:2], axis=-1, keepdims=True))
           - jnp.exp(jnp.sum(lam4[2:3] * lam4[3:4], axis=-1, keepdims=True)) + LAMBDA_INIT)
    o_all = acc_scr[:, :DA_VDIM] / acc_scr[:, DA_VDIM:]
    o = o_all[:t] - lam * o_all[t:]
    o = o * lax.rsqrt(jnp.mean(o * o, axis=-1, keepdims=True) + EPS) * gsub_ref[...] * (1.0 - LAMBDA_INIT)
    o_ref[...] = o.astype(o_ref.dtype)


def _diff_attn(proj, lam4, gsub, batch, seq):
    n = proj.shape[0]
    t = min(ATT_TILE, seq)
    nq = seq // t
    qb, kb, vb = COL_QA // LANES, COL_KA // LANES, COL_VA // LANES
    return pl.pallas_call(
        _diff_attn_kernel,
        grid=(batch, DA_HEADS, nq),
        in_specs=[
            pl.BlockSpec((t, LANES), lambda b, h, i: (b * nq + i, qb + h)),
            pl.BlockSpec((seq, LANES), lambda b, h, i: (b, kb + h)),
            pl.BlockSpec((seq, LANES), lambda b, h, i: (b, vb + h)),
            pl.BlockSpec((4, LANES), lambda b, h, i: (0, 0)),
            pl.BlockSpec((1, LANES), lambda b, h, i: (0, 0)),
        ],
        out_specs=pl.BlockSpec((t, LANES), lambda b, h, i: (b * nq + i, h)),
        out_shape=jax.ShapeDtypeStruct((n, DA_WIDTH), jnp.bfloat16),
        scratch_shapes=[pltpu.VMEM((2 * t, LANES), jnp.bfloat16),
                        pltpu.VMEM((seq, DA_VDIM + LANES), jnp.bfloat16),
                        pltpu.VMEM((2 * t, t), jnp.float32),
                        pltpu.VMEM((2 * t, t), jnp.float32),
                        pltpu.VMEM((2 * t, t), jnp.bfloat16),
                        pltpu.VMEM((2 * t, LANES), jnp.float32),
                        pltpu.VMEM((2 * t, LANES), jnp.float32),
                        pltpu.VMEM((2 * t, DA_VDIM + LANES), jnp.float32)],
        compiler_params=pltpu.CompilerParams(dimension_semantics=("arbitrary",) * 3,
                                             vmem_limit_bytes=VMEM_LIMIT),
        name="diff_attn",
    )(proj, proj, proj, lam4, gsub)


def _retention_kernel(q_ref, k_ref, v_ref, g_ref, gng_ref, gnb_ref, o_ref, r_scr, *, chunk):
    hf = jnp.full((1, 1), pl.program_id(1), jnp.int32).astype(jnp.float32)
    log_g = jnp.log1p(-jnp.exp2(-5.0 - hf))
    ri = lax.broadcasted_iota(jnp.int32, (chunk, chunk), 0)
    ci = lax.broadcasted_iota(jnp.int32, (chunk, chunk), 1)
    rel = (ri - ci).astype(jnp.float32)
    dmask = jnp.where(rel >= 0, jnp.exp(jnp.maximum(rel, 0.0) * log_g), 0.0)
    idx = lax.broadcasted_iota(jnp.int32, (chunk, 1), 0).astype(jnp.float32)
    zeta = jnp.exp((chunk - 1 - idx) * log_g)
    xi = jnp.exp((idx + 1.0) * log_g)
    g_chunk = jnp.exp(chunk * log_g)
    r_scr[...] = jnp.zeros(r_scr.shape, jnp.float32)
    gng = gng_ref[...]
    gnb = gnb_ref[...]

    def body(n, carry):
        start = pl.multiple_of(n * chunk, chunk)
        q = q_ref[pl.ds(start, chunk), :]
        k = k_ref[pl.ds(start, chunk), :]
        v = v_ref[pl.ds(start, chunk), :]
        s = _dot_nt(q, k) * dmask
        r_old = r_scr[...]
        o = _dot(s.astype(jnp.bfloat16), v) + xi * _dot(q, r_old.astype(jnp.bfloat16))
        kz = (k.astype(jnp.float32) * zeta).astype(jnp.bfloat16)
        r_scr[...] = g_chunk * r_old + _dot_tn(kz, v)
        mu = jnp.mean(o, axis=-1, keepdims=True)
        d = o - mu
        var = jnp.mean(d * d, axis=-1, keepdims=True)
        y = d * lax.rsqrt(var + EPS) * gng + gnb
        y = y * g_ref[pl.ds(start, chunk), :].astype(jnp.float32)
        o_ref[pl.ds(start, chunk), :] = y.astype(o_ref.dtype)
        return carry

    lax.fori_loop(0, q_ref.shape[0] // chunk, body, 0, unroll=RET_UNROLL)


def _retention(proj, gn_g, gn_b, batch, seq):
    n = proj.shape[0]
    chunk = min(RET_CHUNK, seq)
    col = lambda c0: (lambda b, h: (b, c0 // LANES + h))
    return pl.pallas_call(
        functools.partial(_retention_kernel, chunk=chunk),
        grid=(batch, RET_HEADS),
        in_specs=[
            pl.BlockSpec((seq, LANES), col(COL_QR)),
            pl.BlockSpec((seq, LANES), col(COL_KR)),
            pl.BlockSpec((seq, LANES), col(COL_VR)),
            pl.BlockSpec((seq, LANES), col(COL_GB)),
            pl.BlockSpec((1, LANES), lambda b, h: (0, h)),
            pl.BlockSpec((1, LANES), lambda b, h: (0, h)),
        ],
        out_specs=pl.BlockSpec((seq, LANES), lambda b, h: (b, h)),
        out_shape=jax.ShapeDtypeStruct((n, RET_WIDTH), jnp.bfloat16),
        scratch_shapes=[pltpu.VMEM((RET_KDIM, RET_VDIM), jnp.float32)],
        compiler_params=pltpu.CompilerParams(dimension_semantics=("arbitrary",) * 2,
                                             vmem_limit_bytes=VMEM_LIMIT),
        name="retention",
    )(proj, proj, proj, proj, gn_g.reshape(1, RET_WIDTH), gn_b.reshape(1, RET_WIDTH))


def _merge_kernel(x_ref, oa_ref, ob_ref, sa0_ref, sa1_ref, sb0_ref, sb1_ref, wa_ref, wb_ref, wo_ref,
                  g2_ref, wr_hi_ref, wr_lo_ref, br_ref, tri_ref, x1_ref, h2_ref, route_ref, cols_ref, counts_ref,
                  base_scr, logits_scr):
    i = pl.program_id(0)

    @pl.when(i == 0)
    def _():
        base_scr[...] = jnp.zeros(base_scr.shape, jnp.float32)
        logits_scr[...] = jnp.zeros(logits_scr.shape, jnp.float32)

    ya = _dot(oa_ref[...], wa_ref[...])
    yb = _dot(ob_ref[...], wb_ref[...])

    logits = logits_scr[...]
    lane = lax.broadcasted_iota(jnp.int32, logits.shape, 1)
    neg = -jnp.inf
    gl = jnp.where(lane < N_GROUPS, logits, neg)
    gmax = jnp.max(gl, axis=-1, keepdims=True)
    g_idx = jnp.min(jnp.where(gl == gmax, lane, LANES), axis=-1, keepdims=True)
    p_g = 1.0 / jnp.sum(jnp.exp(gl - gmax), axis=-1, keepdims=True)
    e_lo = N_GROUPS + EXPERTS_PER_GROUP * g_idx
    el = jnp.where((lane >= e_lo) & (lane < e_lo + EXPERTS_PER_GROUP), logits, neg)
    v1 = jnp.max(el, axis=-1, keepdims=True)
    i1 = jnp.min(jnp.where(el == v1, lane, LANES), axis=-1, keepdims=True)
    el2 = jnp.where(lane == i1, neg, el)
    v2 = jnp.max(el2, axis=-1, keepdims=True)
    i2 = jnp.min(jnp.where(el2 == v2, lane, LANES), axis=-1, keepdims=True)
    t = jnp.exp(v2 - v1)
    w1 = p_g / (1.0 + t)
    w2 = p_g * t / (1.0 + t)
    e1 = i1 - N_GROUPS
    e2 = i2 - N_GROUPS

    oh1 = lane == e1
    oh2 = lane == e2
    real = jnp.where(i > 0, 1.0, 0.0)
    picked = jnp.where(oh1 | oh2, real, 0.0)
    before = _dot(tri_ref[...], picked.astype(jnp.bfloat16)) + base_scr[0:1, :]
    rank1 = jnp.sum(jnp.where(oh1, before, 0.0), axis=-1, keepdims=True)
    rank2 = jnp.sum(jnp.where(oh2, before, 0.0), axis=-1, keepdims=True)
    base_scr[...] = base_scr[...] + jnp.sum(picked, axis=0, keepdims=True)
    counts_ref[...] = base_scr[...]

    cols = [e1.astype(jnp.float32), e2.astype(jnp.float32), w1, w2, rank1, rank2]
    route = jnp.zeros(logits.shape, jnp.float32)
    for c, val in enumerate(cols):
        route = jnp.where(lane == c, val, route)
    route_ref[...] = route
    cols_ref[...] = route.T[:8]

    sa = jnp.concatenate([sa0_ref[...], sa1_ref[...]], axis=1).astype(jnp.float32)
    sb = jnp.concatenate([sb0_ref[...], sb1_ref[...]], axis=1).astype(jnp.float32)
    merged = sa * ya + sb * yb
    x1 = x_ref[...] + _dot(merged.astype(jnp.bfloat16), wo_ref[...])
    x1_ref[...] = x1
    h2 = x1 * lax.rsqrt(jnp.mean(x1 * x1, axis=-1, keepdims=True) + EPS) * g2_ref[...]
    _store_packed(h2_ref, h2)

    hi = h2.astype(jnp.bfloat16)
    lo = (h2 - hi.astype(jnp.float32)).astype(jnp.bfloat16)
    logits_scr[...] = (_dot(hi, wr_hi_ref[...]) + _dot(lo, wr_hi_ref[...]) + _dot(hi, wr_lo_ref[...])
                       + br_ref[...])


def _merge(x2, oa, ob, proj, wa, wb, wo, g2, w_gr, b_gr, w_er, b_er):
    n = x2.shape[0]
    tm = min(PROJ_ROWS, n)
    half = D_MODEL // 2
    pad = LANES - N_GROUPS - N_EXPERTS
    wr = jnp.concatenate([w_gr, w_er, jnp.zeros((D_MODEL, pad), jnp.float32)], axis=1)
    wr_hi = wr.astype(jnp.bfloat16)
    wr_lo = (wr - wr_hi.astype(jnp.float32)).astype(jnp.bfloat16)
    br = jnp.concatenate([b_gr, b_er, jnp.zeros((pad,), jnp.float32)])[None, :]
    tri = (jnp.arange(tm)[:, None] > jnp.arange(tm)[None, :]).astype(jnp.bfloat16)
    full = lambda shape: pl.BlockSpec(shape, lambda i: (0,) * len(shape))
    nt = n // tm
    cur = lambda i: jnp.minimum(i, nt - 1)
    gate = lambda c0: pl.BlockSpec((tm, half), lambda i: (cur(i), c0 // half))
    return pl.pallas_call(
        _merge_kernel,
        grid=(nt + 1,),
        in_specs=[
            pl.BlockSpec((tm, D_MODEL), lambda i: (cur(i), 0)),
            pl.BlockSpec((tm, DA_WIDTH), lambda i: (cur(i), 0)),
            pl.BlockSpec((tm, RET_WIDTH), lambda i: (cur(i), 0)),
            gate(COL_GATE_A), gate(COL_GATE_A + half), gate(COL_GATE_B), gate(COL_GATE_B + half),
            full((DA_WIDTH, D_MODEL)), full((RET_WIDTH, D_MODEL)), full((D_MODEL, D_MODEL)),
            full((1, D_MODEL)), full((D_MODEL, LANES)), full((D_MODEL, LANES)), full((1, LANES)),
            full((tm, tm)),
        ],
        out_specs=[
            pl.BlockSpec((tm, D_MODEL), lambda i: (cur(i), 0)),
            pl.BlockSpec((ROW_PIECES, tm, PIECE), lambda i: (0, cur(i), 0)),
            pl.BlockSpec((tm, LANES), lambda i: (jnp.maximum(i - 1, 0), 0)),
            pl.BlockSpec((8, tm), lambda i: (0, jnp.maximum(i - 1, 0))),
            pl.BlockSpec((8, LANES), lambda i: (0, 0)),
        ],
        out_shape=[
            jax.ShapeDtypeStruct((n, D_MODEL), jnp.float32),
            jax.ShapeDtypeStruct((ROW_PIECES, n, PIECE), jnp.uint32),
            jax.ShapeDtypeStruct((n, LANES), jnp.float32),
            jax.ShapeDtypeStruct((8, n), jnp.float32),
            jax.ShapeDtypeStruct((8, LANES), jnp.float32),
        ],
        scratch_shapes=[pltpu.VMEM((8, LANES), jnp.float32), pltpu.VMEM((tm, LANES), jnp.float32)],
        compiler_params=pltpu.CompilerParams(dimension_semantics=("arbitrary",),
                                             vmem_limit_bytes=VMEM_LIMIT),
        name="merge",
    )(x2, oa, ob, proj, proj, proj, proj, wa.astype(jnp.bfloat16), wb.astype(jnp.bfloat16),
      wo.astype(jnp.bfloat16), g2.reshape(1, D_MODEL), wr_hi, wr_lo, br, tri)


def _sc_mesh():
    return plsc.VectorSubcoreMesh(core_axis_name="c", subcore_axis_name="s")


def _sc_scatter_rows(src, idx, out_rows, src_block):
    steps = idx.shape[1] // SC_WINDOW
    per_core = steps // SC_CORES

    @pl.kernel(out_type=jax.ShapeDtypeStruct((out_rows, PIECE), src.dtype), mesh=_sc_mesh())
    def scatter(src_hbm, idx_hbm, out_hbm):
        def body(src_vmem, idx_vmem):
            pltpu.sync_copy(src_vmem, out_hbm.at[idx_vmem.at[0]])

        pltpu.emit_pipeline(
            body,
            grid=(SC_CORES, per_core),
            in_specs=[pl.BlockSpec((SC_WINDOW, PIECE), lambda c, i: (src_block(c * per_core + i), 0)),
                      pl.BlockSpec((1, SC_WINDOW), lambda c, i: (0, c * per_core + i))],
            out_specs=[],
            core_axis_name=("c", "s"),
            dimension_semantics=(pltpu.PARALLEL, pltpu.PARALLEL),
        )(src_hbm, idx_hbm)

    return scatter(src, idx)


def _sc_gather_rows(table, idx):
    num = idx.shape[1]
    per_core = num // SC_WINDOW // SC_CORES

    @pl.kernel(out_type=jax.ShapeDtypeStruct((num, PIECE), table.dtype), mesh=_sc_mesh())
    def gather(table_hbm, idx_hbm, out_hbm):
        def body(idx_vmem, out_vmem):
            pltpu.sync_copy(table_hbm.at[idx_vmem.at[0]], out_vmem)

        pltpu.emit_pipeline(
            body,
            grid=(SC_CORES, per_core),
            in_specs=[pl.BlockSpec((1, SC_WINDOW), lambda c, i: (0, c * per_core + i))],
            out_specs=[pl.BlockSpec((SC_WINDOW, PIECE), lambda c, i: (c * per_core + i, 0))],
            core_axis_name=("c", "s"),
            dimension_semantics=(pltpu.PARALLEL, pltpu.PARALLEL),
        )(idx_hbm, out_hbm)

    return gather(table, idx)


def _store_packed(ref, val):
    as_bits = lambda v: lax.bitcast_convert_type(v.astype(jnp.bfloat16).astype(jnp.float32), jnp.uint32)
    words = (as_bits(val[:, :PACKED]) >> 16) | (as_bits(val[:, PACKED:]) & jnp.uint32(0xFFFF0000))
    for j in range(ROW_PIECES):
        ref[j, :val.shape[0], :] = words[:, j * PIECE:(j + 1) * PIECE]


def _load_packed(ref, rows=None):
    rows = ref.shape[1] if rows is None else rows
    words = jnp.concatenate([ref[j, :rows, :] for j in range(ROW_PIECES)], axis=1)
    low = lax.bitcast_convert_type(words << 16, jnp.float32)
    high = lax.bitcast_convert_type(words & jnp.uint32(0xFFFF0000), jnp.float32)
    return jnp.concatenate([low, high], axis=1)


def _expert_kernel(blk_e_ref, n_used_ref, nxt_ref, run_ref, rows_ref, x_ref, wg_hbm, wu_hbm, wd_hbm, o_ref,
                   wg_stage, wu_stage, wd_stage, wg_scr, wu_scr, wd_scr, sem):
    i = pl.program_id(0)
    used = i < n_used_ref[0]

    def weight_copies(e, s):
        return (pltpu.make_async_copy(wg_hbm.at[e], wg_stage.at[s], sem.at[s, 0]),
                pltpu.make_async_copy(wu_hbm.at[e], wu_stage.at[s], sem.at[s, 1]),
                pltpu.make_async_copy(wd_hbm.at[e], wd_stage.at[s], sem.at[s, 2]))

    @pl.when(i == 0)
    def _():
        for c in weight_copies(blk_e_ref[0], 0):
            c.start()

    @pl.when(used & ((i == 0) | (blk_e_ref[i] != blk_e_ref[jnp.maximum(i - 1, 0)])))
    def _():
        s = run_ref[i] % 2
        for c in weight_copies(blk_e_ref[i], s):
            c.wait()
        wg_scr[...] = wg_stage[s].astype(jnp.bfloat16)
        wu_scr[...] = wu_stage[s].astype(jnp.bfloat16)
        wd_scr[...] = wd_stage[s].astype(jnp.bfloat16)

        @pl.when(nxt_ref[i] >= 0)
        def _():
            for c in weight_copies(nxt_ref[i], 1 - s):
                c.start()

    def expert_mlp(rows):
        x = _load_packed(x_ref, rows).astype(jnp.bfloat16)
        a = _dot(x, wg_scr[...])
        u = _dot(x, wu_scr[...])
        hmid = (a * _sigmoid(a) * u).astype(jnp.bfloat16)
        _store_packed(o_ref, _dot(hmid, wd_scr[...]))

    half = MOE_BLOCK // 2
    short = rows_ref[i] <= half

    @pl.when(used & jnp.logical_not(short))
    def _():
        expert_mlp(MOE_BLOCK)

    @pl.when(used & short)
    def _():
        expert_mlp(half)
        o_ref[:, half:, :] = jnp.zeros((ROW_PIECES, MOE_BLOCK - half, PIECE), o_ref.dtype)

    @pl.when(jnp.logical_not(used))
    def _():
        o_ref[...] = jnp.zeros(o_ref.shape, o_ref.dtype)


def _experts(xs, blk_expert, blk_rows, n_used, w_gate, w_up, w_down):
    p = xs.shape[1]
    nblk = p // MOE_BLOCK
    idx = jnp.arange(nblk, dtype=jnp.int32)
    starts = (idx < n_used[0]) & ((idx == 0) | (blk_expert != jnp.roll(blk_expert, 1)))
    run = jnp.cumsum(starts.astype(jnp.int32)) - 1
    next_start = lax.cummin(jnp.where(starts, idx, nblk)[::-1])[::-1]
    after = jnp.concatenate([next_start[1:], jnp.full((1,), nblk, jnp.int32)])
    nxt = jnp.where(after < nblk, blk_expert[jnp.minimum(after, nblk - 1)], -1).astype(jnp.int32)
    live = lambda i, be, nu, *_: jnp.minimum(i, nu[0] - 1)
    any_spec = pl.BlockSpec(memory_space=pl.ANY)
    return pl.pallas_call(
        _expert_kernel,
        grid_spec=pltpu.PrefetchScalarGridSpec(
            num_scalar_prefetch=5,
            grid=(nblk,),
            in_specs=[
                pl.BlockSpec((ROW_PIECES, MOE_BLOCK, PIECE), lambda i, *pre: (0, live(i, *pre), 0)),
                any_spec, any_spec, any_spec,
            ],
            out_specs=pl.BlockSpec((ROW_PIECES, MOE_BLOCK, PIECE), lambda i, *pre: (0, i, 0)),
            scratch_shapes=[pltpu.VMEM((2, D_MODEL, EXPERT_FF), jnp.float32),
                            pltpu.VMEM((2, D_MODEL, EXPERT_FF), jnp.float32),
                            pltpu.VMEM((2, EXPERT_FF, D_MODEL), jnp.float32),
                            pltpu.VMEM((D_MODEL, EXPERT_FF), jnp.bfloat16),
                            pltpu.VMEM((D_MODEL, EXPERT_FF), jnp.bfloat16),
                            pltpu.VMEM((EXPERT_FF, D_MODEL), jnp.bfloat16),
                            pltpu.SemaphoreType.DMA((2, 3))],
        ),
        out_shape=jax.ShapeDtypeStruct((ROW_PIECES, p, PIECE), jnp.uint32),
        compiler_params=pltpu.CompilerParams(dimension_semantics=("arbitrary",),
                                             vmem_limit_bytes=VMEM_LIMIT),
        name="experts",
    )(blk_expert, n_used, nxt, run.astype(jnp.int32), blk_rows, xs, w_gate, w_up, w_down)


def _combine_kernel(x1_ref, route_ref, y0_ref, y1_ref, *rest):
    o_ref = rest[-1]
    route = route_ref[...]
    o_ref[...] = x1_ref[...] + route[:, 2:3] * _load_packed(y0_ref) + route[:, 3:4] * _load_packed(y1_ref)


def _combine(x1, yg, route, row0, out_prev):
    n = yg.shape[1] // TOP_K
    tm = min(PROJ_ROWS, n)
    blk0 = row0 // tm
    prev = () if out_prev is None else (out_prev,)
    return pl.pallas_call(
        _combine_kernel,
        grid=(n // tm,),
        in_specs=[
            pl.BlockSpec((tm, D_MODEL), lambda i: (blk0 + i, 0)),
            pl.BlockSpec((tm, LANES), lambda i: (blk0 + i, 0)),
            pl.BlockSpec((ROW_PIECES, tm, PIECE), lambda i: (0, i, 0)),
            pl.BlockSpec((ROW_PIECES, tm, PIECE), lambda i: (0, i + n // tm, 0)),
        ] + [pl.BlockSpec(memory_space=pl.ANY)] * len(prev),
        out_specs=pl.BlockSpec((tm, D_MODEL), lambda i: (blk0 + i, 0)),
        out_shape=jax.ShapeDtypeStruct(x1.shape, jnp.float32),
        input_output_aliases={4: 0} if prev else {},
        compiler_params=pltpu.CompilerParams(dimension_semantics=("arbitrary",),
                                             vmem_limit_bytes=VMEM_LIMIT),
        name="combine",
    )(x1, route, yg, yg, *prev)


def _dispatch_plan(route_cols, counts, n):
    counts = counts[0, :N_EXPERTS].astype(jnp.int32)
    padded = ((counts + MOE_BLOCK - 1) // MOE_BLOCK) * MOE_BLOCK
    seg_end = jnp.cumsum(padded).astype(jnp.int32)
    seg_start = seg_end - padded
    cols = route_cols.astype(jnp.int32)
    e, rank = cols[0:TOP_K], cols[4:4 + TOP_K]
    picked = e[None] == jnp.arange(N_EXPERTS, dtype=jnp.int32)[:, None, None]
    dest = jnp.sum(jnp.where(picked, seg_start[:, None, None], 0), axis=0) + rank
    p = n * TOP_K + N_EXPERTS * MOE_BLOCK
    slot = dest[None] + (jnp.arange(ROW_PIECES, dtype=jnp.int32) * p)[:, None, None]
    blk_start = jnp.arange(p // MOE_BLOCK, dtype=jnp.int32) * MOE_BLOCK
    blk_expert = jnp.sum((seg_end[None, :] <= blk_start[:, None]).astype(jnp.int32), axis=1)
    blk_expert = jnp.minimum(blk_expert, N_EXPERTS - 1)
    of_block = blk_expert[:, None] == jnp.arange(N_EXPERTS, dtype=jnp.int32)[None, :]
    real_end = jnp.sum(jnp.where(of_block, (seg_start + counts)[None, :], 0), axis=1)
    blk_rows = jnp.clip(real_end - blk_start, 0, MOE_BLOCK)
    n_used = (seg_end[-1] // MOE_BLOCK).reshape(1)
    return slot, blk_expert, blk_rows, n_used, p


def _layer(x, positions, norm1_g, w_in, q_norm_g, k_norm_g, lam4, diff_subln_g, ret_gn_g, ret_gn_b,
           w_branch_a, w_branch_b, w_out, norm2_g, w_gr, b_gr, w_er, b_er, w_gate, w_up, w_down):
    batch, seq, _ = x.shape
    n = batch * seq
    x2 = x.reshape(n, D_MODEL)
    proj = _in_proj(x2, positions.reshape(1, n), norm1_g, w_in, q_norm_g, k_norm_g)
    oa = _diff_attn(proj, lam4, diff_subln_g.reshape(1, DA_VDIM), batch, seq)
    ob = _retention(proj, ret_gn_g, ret_gn_b, batch, seq)
    x1, h2, route, route_cols, counts = _merge(x2, oa, ob, proj, w_branch_a, w_branch_b, w_out, norm2_g,
                                               w_gr, b_gr, w_er, b_er)
    slot, blk_expert, blk_rows, n_used, p = _dispatch_plan(route_cols, counts, n)
    win_n = n // SC_WINDOW
    src_block = lambda s: (s // (TOP_K * win_n)) * win_n + s % win_n
    xs = _sc_scatter_rows(h2.reshape(ROW_PIECES * n, PIECE), slot.reshape(1, -1), ROW_PIECES * p, src_block)
    ys = _experts(xs.reshape(ROW_PIECES, p, PIECE), blk_expert, blk_rows, n_used, w_gate, w_up, w_down)
    parts = COMBINE_PARTS if n % (COMBINE_PARTS * PROJ_ROWS) == 0 else 1
    m = n // parts
    out = None
    for t in range(parts):
        yg = _sc_gather_rows(ys.reshape(ROW_PIECES * p, PIECE), slot[:, :, t * m:(t + 1) * m].reshape(1, -1))
        out = _combine(x1, yg.reshape(ROW_PIECES, TOP_K * m, PIECE), route, t * m, out)
    return out.reshape(batch, seq, D_MODEL)


def kernel(x, positions, norm1_g, w_in, q_norm_g, k_norm_g, lambda_q1, lambda_k1, lambda_q2, lambda_k2, diff_subln_g, ret_gn_g, ret_gn_b, w_branch_a, w_branch_b, w_out, norm2_g, w_group_router, b_group_router, w_expert_router, b_expert_router, w_gate, w_up, w_down):
    assert x.shape[-1] == D_MODEL and norm1_g.shape[0] == 1, "single-layer, D_MODEL-wide input expected"
    lam4 = jnp.pad(jnp.stack([lambda_q1[0], lambda_k1[0], lambda_q2[0], lambda_k2[0]]), ((0, 0), (0, LANES - DA_HALF)))
    return _layer(x, positions, norm1_g[0], w_in[0], q_norm_g[0], k_norm_g[0], lam4, diff_subln_g[0],
                  ret_gn_g[0], ret_gn_b[0], w_branch_a[0], w_branch_b[0], w_out[0], norm2_g[0],
                  w_group_router[0], b_group_router[0], w_expert_router[0], b_expert_router[0],
                  w_gate[0], w_up[0], w_down[0])
```

```python
import functools
import math

import jax
import jax.numpy as jnp
from jax import lax
from jax.experimental import pallas as pl
from jax.experimental.pallas import tpu as pltpu
from jax.experimental.pallas import tpu_sc as plsc

D_MODEL = 1024
DA_HEADS = 4
DA_HALF = 64
DA_VDIM = 2 * DA_HALF
DA_WIDTH = DA_HEADS * DA_VDIM
ROPE_THETA = 500000.0
ROPE_DIM = DA_HALF // 4
RET_HEADS = 4
RET_KDIM = 128
RET_VDIM = 128
RET_WIDTH = RET_HEADS * RET_VDIM
RET_THETA = 10000.0
N_GROUPS = 4
EXPERTS_PER_GROUP = 8
N_EXPERTS = N_GROUPS * EXPERTS_PER_GROUP
TOP_K = 2
EXPERT_FF = 512
EPS = 1e-6
LAMBDA_INIT = 0.8 - 0.6 * math.exp(-0.3 * 0)

LANES = 128
IN_COLS = 3 * DA_WIDTH + 4 * RET_WIDTH + 2 * D_MODEL
COL_QA, COL_KA, COL_VA = 0, DA_WIDTH, 2 * DA_WIDTH
COL_QR = 3 * DA_WIDTH
COL_KR = COL_QR + RET_WIDTH
COL_VR = COL_KR + RET_WIDTH
COL_GB = COL_VR + RET_WIDTH
COL_GATE_A = COL_GB + RET_WIDTH
COL_GATE_B = COL_GATE_A + D_MODEL

PROJ_ROWS = 1024
IN_PROJ_ROWS = 1024
PROJ_CHUNK = 256
ATT_TILE = 1024
ATT_ROWS = 32
RET_CHUNK = 256
RET_UNROLL = 16
MOE_BLOCK = 512
PACKED = D_MODEL // 2
ROW_PIECES = 2
PIECE = PACKED // ROW_PIECES
SC_CORES = 2
SC_WINDOW = 128
COMBINE_PARTS = 2
VMEM_LIMIT = 56 * 1024 * 1024


def _dot(a, b):
    return jnp.dot(a, b, preferred_element_type=jnp.float32)


def _dot_nt(a, b):
    return lax.dot_general(a, b, (((1,), (1,)), ((), ())), preferred_element_type=jnp.float32)


def _dot_tn(a, b):
    return lax.dot_general(a, b, (((0,), (0,)), ((), ())), preferred_element_type=jnp.float32)


def _sigmoid(x):
    return 0.5 * jnp.tanh(0.5 * x) + 0.5


def _split3(x):
    a = x.astype(jnp.bfloat16)
    r = x - a.astype(jnp.float32)
    b = r.astype(jnp.bfloat16)
    c = (r - b.astype(jnp.float32)).astype(jnp.bfloat16)
    return a, b, c


def _in_proj_kernel(x_ref, pos_ref, g1_ref, w_ref, gsum_ref, gq_ref, gk_ref, fa_ref, fr_ref, sel_ref,
                    o_ref, h_scr):
    x = x_ref[...]
    rows = x.shape[0]
    h_scr[...] = (x * g1_ref[...]).astype(jnp.bfloat16)
    rms_scale = jnp.broadcast_to(lax.rsqrt(jnp.mean(x * x, axis=-1, keepdims=True) + EPS), (rows, PROJ_CHUNK))
    pos = pos_ref[...].astype(jnp.float32)

    lane = lax.broadcasted_iota(jnp.int32, (rows, LANES), 1)
    half_a = ROPE_DIM // 2
    tables = {}

    def da_tables():
        if "da" not in tables:
            ang_a = fa_ref[...] * pos
            pad = jnp.zeros((LANES - 2 * half_a, rows), jnp.float32)
            t_a = jnp.concatenate([jnp.cos(ang_a), jnp.sin(ang_a), pad], axis=0).T
            tab = sum(_dot(part, sel_ref[...]) for part in _split3(t_a))
            c_a = tab[:, :LANES] + jnp.where(lane % DA_HALF < ROPE_DIM, 0.0, 1.0)
            s_lo = tab[:, LANES:2 * LANES]
            s_hi = tab[:, 2 * LANES:]
            tables["da"] = tuple(jnp.concatenate([v, v], axis=1) for v in (c_a, s_lo, s_hi))
        return tables["da"]

    def ret_tables():
        if "ret" not in tables:
            ang_r = fr_ref[...] * pos
            t_r = jnp.concatenate([jnp.cos(ang_r), jnp.sin(ang_r)], axis=0).T
            sw_r = pltpu.roll(t_r, RET_KDIM // 2, axis=1)
            first = lane < RET_KDIM // 2
            c_r = jnp.where(first, t_r, sw_r)
            s_r = jnp.where(first, -sw_r, t_r)
            tables["ret"] = tuple(jnp.concatenate([v, v], axis=1) for v in (c_r, s_r))
        return tables["ret"]

    def qk_norm_rope(y, g, scale):
        c_a2, s_lo2, s_hi2 = da_tables()
        ss = y * y
        hi = ss.astype(jnp.bfloat16)
        lo = (ss - hi.astype(jnp.float32)).astype(jnp.bfloat16)
        gs = _dot(hi, gsum_ref[...]) + _dot(lo, gsum_ref[...])
        n = y * lax.rsqrt(gs * (1.0 / DA_HALF) + EPS) * g
        up = pltpu.roll(n, PROJ_CHUNK - half_a, axis=1)
        dn = pltpu.roll(n, half_a, axis=1)
        r = n * c_a2 + up * s_lo2 + dn * s_hi2
        return r * scale if scale != 1.0 else r

    def ret_rope(y, scale):
        c_r2, s_r2 = ret_tables()
        halves = [pltpu.roll(y[:, i * LANES:(i + 1) * LANES], RET_KDIM // 2, axis=1)
                  for i in range(PROJ_CHUNK // LANES)]
        sw = jnp.concatenate(halves, axis=1)
        r = y * c_r2 + sw * s_r2
        return r * scale if scale != 1.0 else r

    n_chunks = IN_COLS // PROJ_CHUNK
    is_long = lambda c: c * PROJ_CHUNK < COL_VA or COL_QR <= c * PROJ_CHUNK < COL_VR
    long_chunks = [c for c in range(n_chunks) if is_long(c)]
    short_chunks = [c for c in range(n_chunks) if not is_long(c)][::-1]
    order = [short_chunks.pop(0) for _ in range(3)]
    while long_chunks or short_chunks:
        if long_chunks:
            order.append(long_chunks.pop(0))
        if short_chunks:
            order.append(short_chunks.pop(0))
    for c in order:
        c0 = c * PROJ_CHUNK
        y = _dot(h_scr[...], w_ref[:, c0:c0 + PROJ_CHUNK]) * rms_scale
        if c0 < COL_KA:
            y = qk_norm_rope(y, gq_ref[...], DA_HALF ** -0.5)
        elif c0 < COL_VA:
            y = qk_norm_rope(y, gk_ref[...], 1.0)
        elif c0 < COL_QR:
            pass
        elif c0 < COL_KR:
            y = ret_rope(y, 1.0)
        elif c0 < COL_VR:
            y = ret_rope(y, RET_KDIM ** -0.5)
        elif c0 < COL_GB:
            pass
        elif c0 < COL_GATE_A:
            y = y * _sigmoid(y)
        else:
            y = _sigmoid(y)
        o_ref[:, c0:c0 + PROJ_CHUNK] = y.astype(o_ref.dtype)


def _in_proj(x2, pos2, g1, w_in, gq, gk):
    n = x2.shape[0]
    tm = min(IN_PROJ_ROWS, n)
    grp = jnp.arange(PROJ_CHUNK) // DA_HALF
    gsum = (grp[:, None] == grp[None, :]).astype(jnp.bfloat16)
    half_a = ROPE_DIM // 2
    fa = jnp.power(jnp.float32(ROPE_THETA), -2.0 * jnp.arange(half_a, dtype=jnp.float32) / ROPE_DIM)[:, None]
    half_r = RET_KDIM // 2
    fr = jnp.power(jnp.float32(RET_THETA), -2.0 * jnp.arange(half_r, dtype=jnp.float32) / RET_KDIM)[:, None]
    j = jnp.arange(LANES)[:, None]
    l64 = (jnp.arange(LANES) % DA_HALF)[None, :]
    sel_c = (j < half_a) & (l64 < ROPE_DIM) & (l64 % half_a == j)
    sel_lo = (j >= half_a) & (j < ROPE_DIM) & (l64 < half_a) & (l64 == j - half_a)
    sel_hi = (j >= half_a) & (j < ROPE_DIM) & (l64 >= half_a) & (l64 < ROPE_DIM) & (l64 == j)
    sel = jnp.concatenate([sel_c.astype(jnp.float32), -sel_lo.astype(jnp.float32),
                           sel_hi.astype(jnp.float32)], axis=1).astype(jnp.bfloat16)
    reps = PROJ_CHUNK // DA_HALF
    full = lambda shape: pl.BlockSpec(shape, lambda i: (0,) * len(shape))
    return pl.pallas_call(
        _in_proj_kernel,
        grid=(n // tm,),
        in_specs=[
            pl.BlockSpec((tm, D_MODEL), lambda i: (i, 0)),
            pl.BlockSpec((1, tm), lambda i: (0, i)),
            full((1, D_MODEL)),
            pl.BlockSpec((D_MODEL, IN_COLS), lambda i: (0, 0), pipeline_mode=pl.Buffered(1)),
            full((PROJ_CHUNK, PROJ_CHUNK)),
            full((1, PROJ_CHUNK)),
            full((1, PROJ_CHUNK)),
            full((half_a, 1)),
            full((half_r, 1)),
            full((LANES, 3 * LANES)),
        ],
        out_specs=pl.BlockSpec((tm, IN_COLS), lambda i: (i, 0)),
        out_shape=jax.ShapeDtypeStruct((n, IN_COLS), jnp.bfloat16),
        scratch_shapes=[pltpu.VMEM((tm, D_MODEL), jnp.bfloat16)],
        compiler_params=pltpu.CompilerParams(dimension_semantics=("arbitrary",),
                                             vmem_limit_bytes=VMEM_LIMIT),
        name="in_proj",
    )(x2, pos2, g1.reshape(1, D_MODEL), w_in.astype(jnp.bfloat16), gsum,
      jnp.tile(gq, reps)[None, :], jnp.tile(gk, reps)[None, :], fa, fr, sel)


def _diff_attn_kernel(q_ref, k_ref, v_ref, lam_ref, gsub_ref, o_ref,
                      qs_scr, vx_scr, s0_scr, s1_scr, p_scr, m_scr, alpha_scr, acc_scr):
    i = pl.program_id(2)
    t = q_ref.shape[0]

    @pl.when(i == 0)
    def _():
        vx_scr[:, :DA_VDIM] = v_ref[...]
        vx_scr[:, DA_VDIM:] = jnp.ones((vx_scr.shape[0], LANES), vx_scr.dtype)

    q = q_ref[...]
    lane = lax.broadcasted_iota(jnp.int32, q.shape, 1)
    zero = jnp.zeros_like(q)
    qs_scr[:t] = jnp.where(lane < DA_HALF, q, zero)
    qs_scr[t:] = jnp.where(lane >= DA_HALF, q, zero)

    def scores(j, s_ref):
        start = pl.multiple_of(j * t, t)
        s_ref[...] = _dot_nt(qs_scr[...], k_ref[pl.ds(start, t), :])

    def softmax_pv(j, s_ref, masked, first=False):
        for c in range(2 * t // ATT_ROWS):
            rows = pl.ds(c * ATT_ROWS, ATT_ROWS)
            s = s_ref[rows, :]
            if masked:
                r = lax.broadcasted_iota(jnp.int32, s.shape, 0) + (c * ATT_ROWS) % t
                col = lax.broadcasted_iota(jnp.int32, s.shape, 1)
                s = jnp.where(col <= r, s, -jnp.inf)
            m_cur = jnp.max(s, axis=-1, keepdims=True)
            if first:
                m_new = jnp.broadcast_to(m_cur, (ATT_ROWS, LANES))
            else:
                m_prev = m_scr[rows, :]
                m_new = jnp.maximum(m_prev, m_cur)
                alpha_scr[rows, :] = jnp.exp(m_prev - m_new)
            m_scr[rows, :] = m_new
            p = jnp.exp(s - jnp.concatenate([m_new] * (t // LANES), axis=1))
            p_scr[rows, :] = p.astype(p_scr.dtype)
        start = pl.multiple_of(j * t, t)
        pv = _dot(p_scr[...], vx_scr[pl.ds(start, t), :])
        if first:
            acc_scr[...] = pv
            return
        alpha = alpha_scr[...]
        for half in range(2):
            cols = pl.ds(half * LANES, LANES)
            acc_scr[:, cols] = alpha * acc_scr[:, cols] + pv[:, half * LANES:(half + 1) * LANES]

    scores(0, s0_scr)

    @pl.when(i == 0)
    def _():
        softmax_pv(0, s0_scr, True, first=True)

    @pl.when(i == 1)
    def _():
        scores(1, s1_scr)
        softmax_pv(0, s0_scr, False, first=True)
        softmax_pv(1, s1_scr, True)

    @pl.when(i >= 2)
    def _():
        scores(1, s1_scr)
        softmax_pv(0, s0_scr, False, first=True)
        scores(2, s0_scr)
        softmax_pv(1, s1_scr, False)

        def pair(jj, carry):
            j = 2 * jj
            scores(j + 1, s1_scr)
            softmax_pv(j, s0_scr, False)
            scores(j + 2, s0_scr)
            softmax_pv(j + 1, s1_scr, False)
            return carry

        lax.fori_loop(1, i // 2, pair, 0)

        @pl.when(i % 2 == 1)
        def _():
            scores(i, s1_scr)
            softmax_pv(i - 1, s0_scr, False)
            softmax_pv(i, s1_scr, True)

        @pl.when(i % 2 == 0)
        def _():
            softmax_pv(i, s0_scr, True)

    lam4 = lam_ref[...]
    lam = (jnp.exp(jnp.sum(lam4[0:1] * lam4[1:2], axis=-1, keepdims=True))
           - jnp.exp(jnp.sum(lam4[2:3] * lam4[3:4], axis=-1, keepdims=True)) + LAMBDA_INIT)
    o_all = acc_scr[:, :DA_VDIM] / acc_scr[:, DA_VDIM:]
    o = o_all[:t] - lam * o_all[t:]
    o = o * lax.rsqrt(jnp.mean(o * o, axis=-1, keepdims=True) + EPS) * gsub_ref[...] * (1.0 - LAMBDA_INIT)
    o_ref[...] = o.astype(o_ref.dtype)


def _diff_attn(proj, lam4, gsub, batch, seq):
    n = proj.shape[0]
    t = min(ATT_TILE, seq)
    nq = seq // t
    qb, kb, vb = COL_QA // LANES, COL_KA // LANES, COL_VA // LANES
    return pl.pallas_call(
        _diff_attn_kernel,
        grid=(batch, DA_HEADS, nq),
        in_specs=[
            pl.BlockSpec((t, LANES), lambda b, h, i: (b * nq + i, qb + h)),
            pl.BlockSpec((seq, LANES), lambda b, h, i: (b, kb + h)),
            pl.BlockSpec((seq, LANES), lambda b, h, i: (b, vb + h)),
            pl.BlockSpec((4, LANES), lambda b, h, i: (0, 0)),
            pl.BlockSpec((1, LANES), lambda b, h, i: (0, 0)),
        ],
        out_specs=pl.BlockSpec((t, LANES), lambda b, h, i: (b * nq + i, h)),
        out_shape=jax.ShapeDtypeStruct((n, DA_WIDTH), jnp.bfloat16),
        scratch_shapes=[pltpu.VMEM((2 * t, LANES), jnp.bfloat16),
                        pltpu.VMEM((seq, DA_VDIM + LANES), jnp.bfloat16),
                        pltpu.VMEM((2 * t, t), jnp.float32),
                        pltpu.VMEM((2 * t, t), jnp.float32),
                        pltpu.VMEM((2 * t, t), jnp.bfloat16),
                        pltpu.VMEM((2 * t, LANES), jnp.float32),
                        pltpu.VMEM((2 * t, LANES), jnp.float32),
                        pltpu.VMEM((2 * t, DA_VDIM + LANES), jnp.float32)],
        compiler_params=pltpu.CompilerParams(dimension_semantics=("arbitrary",) * 3,
                                             vmem_limit_bytes=VMEM_LIMIT),
        name="diff_attn",
    )(proj, proj, proj, lam4, gsub)


def _retention_kernel(q_ref, k_ref, v_ref, g_ref, gng_ref, gnb_ref, o_ref, r_scr, *, chunk):
    hf = jnp.full((1, 1), pl.program_id(1), jnp.int32).astype(jnp.float32)
    log_g = jnp.log1p(-jnp.exp2(-5.0 - hf))
    ri = lax.broadcasted_iota(jnp.int32, (chunk, chunk), 0)
    ci = lax.broadcasted_iota(jnp.int32, (chunk, chunk), 1)
    rel = (ri - ci).astype(jnp.float32)
    dmask = jnp.where(rel >= 0, jnp.exp(jnp.maximum(rel, 0.0) * log_g), 0.0)
    idx = lax.broadcasted_iota(jnp.int32, (chunk, 1), 0).astype(jnp.float32)
    zeta = jnp.exp((chunk - 1 - idx) * log_g)
    xi = jnp.exp((idx + 1.0) * log_g)
    g_chunk = jnp.exp(chunk * log_g)
    r_scr[...] = jnp.zeros(r_scr.shape, jnp.float32)
    gng = gng_ref[...]
    gnb = gnb_ref[...]

    def body(n, carry):
        start = pl.multiple_of(n * chunk, chunk)
        q = q_ref[pl.ds(start, chunk), :]
        k = k_ref[pl.ds(start, chunk), :]
        v = v_ref[pl.ds(start, chunk), :]
        s = _dot_nt(q, k) * dmask
        r_old = r_scr[...]
        o = _dot(s.astype(jnp.bfloat16), v) + xi * _dot(q, r_old.astype(jnp.bfloat16))
        kz = (k.astype(jnp.float32) * zeta).astype(jnp.bfloat16)
        r_scr[...] = g_chunk * r_old + _dot_tn(kz, v)
        mu = jnp.mean(o, axis=-1, keepdims=True)
        d = o - mu
        var = jnp.mean(d * d, axis=-1, keepdims=True)
        y = d * lax.rsqrt(var + EPS) * gng + gnb
        y = y * g_ref[pl.ds(start, chunk), :].astype(jnp.float32)
        o_ref[pl.ds(start, chunk), :] = y.astype(o_ref.dtype)
        return carry

    lax.fori_loop(0, q_ref.shape[0] // chunk, body, 0, unroll=RET_UNROLL)


def _retention(proj, gn_g, gn_b, batch, seq):
    n = proj.shape[0]
    chunk = min(RET_CHUNK, seq)
    col = lambda c0: (lambda b, h: (b, c0 // LANES + h))
    return pl.pallas_call(
        functools.partial(_retention_kernel, chunk=chunk),
        grid=(batch, RET_HEADS),
        in_specs=[
            pl.BlockSpec((seq, LANES), col(COL_QR)),
            pl.BlockSpec((seq, LANES), col(COL_KR)),
            pl.BlockSpec((seq, LANES), col(COL_VR)),
            pl.BlockSpec((seq, LANES), col(COL_GB)),
            pl.BlockSpec((1, LANES), lambda b, h: (0, h)),
            pl.BlockSpec((1, LANES), lambda b, h: (0, h)),
        ],
        out_specs=pl.BlockSpec((seq, LANES), lambda b, h: (b, h)),
        out_shape=jax.ShapeDtypeStruct((n, RET_WIDTH), jnp.bfloat16),
        scratch_shapes=[pltpu.VMEM((RET_KDIM, RET_VDIM), jnp.float32)],
        compiler_params=pltpu.CompilerParams(dimension_semantics=("arbitrary",) * 2,
                                             vmem_limit_bytes=VMEM_LIMIT),
        name="retention",
    )(proj, proj, proj, proj, gn_g.reshape(1, RET_WIDTH), gn_b.reshape(1, RET_WIDTH))


def _merge_kernel(x_ref, oa_ref, ob_ref, sa0_ref, sa1_ref, sb0_ref, sb1_ref, wa_ref, wb_ref, wo_ref,
                  g2_ref, wr_hi_ref, wr_lo_ref, br_ref, tri_ref, x1_ref, h2_ref, route_ref, cols_ref, counts_ref,
                  base_scr, logits_scr):
    i = pl.program_id(0)

    @pl.when(i == 0)
    def _():
        base_scr[...] = jnp.zeros(base_scr.shape, jnp.float32)
        logits_scr[...] = jnp.zeros(logits_scr.shape, jnp.float32)

    ya = _dot(oa_ref[...], wa_ref[...])
    yb = _dot(ob_ref[...], wb_ref[...])

    logits = logits_scr[...]
    lane = lax.broadcasted_iota(jnp.int32, logits.shape, 1)
    neg = -jnp.inf
    gl = jnp.where(lane < N_GROUPS, logits, neg)
    gmax = jnp.max(gl, axis=-1, keepdims=True)
    g_idx = jnp.min(jnp.where(gl == gmax, lane, LANES), axis=-1, keepdims=True)
    p_g = 1.0 / jnp.sum(jnp.exp(gl - gmax), axis=-1, keepdims=True)
    e_lo = N_GROUPS + EXPERTS_PER_GROUP * g_idx
    el = jnp.where((lane >= e_lo) & (lane < e_lo + EXPERTS_PER_GROUP), logits, neg)
    v1 = jnp.max(el, axis=-1, keepdims=True)
    i1 = jnp.min(jnp.where(el == v1, lane, LANES), axis=-1, keepdims=True)
    el2 = jnp.where(lane == i1, neg, el)
    v2 = jnp.max(el2, axis=-1, keepdims=True)
    i2 = jnp.min(jnp.where(el2 == v2, lane, LANES), axis=-1, keepdims=True)
    t = jnp.exp(v2 - v1)
    w1 = p_g / (1.0 + t)
    w2 = p_g * t / (1.0 + t)
    e1 = i1 - N_GROUPS
    e2 = i2 - N_GROUPS

    oh1 = lane == e1
    oh2 = lane == e2
    real = jnp.where(i > 0, 1.0, 0.0)
    picked = jnp.where(oh1 | oh2, real, 0.0)
    before = _dot(tri_ref[...], picked.astype(jnp.bfloat16)) + base_scr[0:1, :]
    rank1 = jnp.sum(jnp.where(oh1, before, 0.0), axis=-1, keepdims=True)
    rank2 = jnp.sum(jnp.where(oh2, before, 0.0), axis=-1, keepdims=True)
    base_scr[...] = base_scr[...] + jnp.sum(picked, axis=0, keepdims=True)
    counts_ref[...] = base_scr[...]

    cols = [e1.astype(jnp.float32), e2.astype(jnp.float32), w1, w2, rank1, rank2]
    route = jnp.zeros(logits.shape, jnp.float32)
    for c, val in enumerate(cols):
        route = jnp.where(lane == c, val, route)
    route_ref[...] = route
    cols_ref[...] = route.T[:8]

    sa = jnp.concatenate([sa0_ref[...], sa1_ref[...]], axis=1).astype(jnp.float32)
    sb = jnp.concatenate([sb0_ref[...], sb1_ref[...]], axis=1).astype(jnp.float32)
    merged = sa * ya + sb * yb
    x1 = x_ref[...] + _dot(merged.astype(jnp.bfloat16), wo_ref[...])
    x1_ref[...] = x1
    h2 = x1 * lax.rsqrt(jnp.mean(x1 * x1, axis=-1, keepdims=True) + EPS) * g2_ref[...]
    _store_packed(h2_ref, h2)

    hi = h2.astype(jnp.bfloat16)
    lo = (h2 - hi.astype(jnp.float32)).astype(jnp.bfloat16)
    logits_scr[...] = (_dot(hi, wr_hi_ref[...]) + _dot(lo, wr_hi_ref[...]) + _dot(hi, wr_lo_ref[...])
                       + br_ref[...])


def _merge(x2, oa, ob, proj, wa, wb, wo, g2, w_gr, b_gr, w_er, b_er):
    n = x2.shape[0]
    tm = min(PROJ_ROWS, n)
    half = D_MODEL // 2
    pad = LANES - N_GROUPS - N_EXPERTS
    wr = jnp.concatenate([w_gr, w_er, jnp.zeros((D_MODEL, pad), jnp.float32)], axis=1)
    wr_hi = wr.astype(jnp.bfloat16)
    wr_lo = (wr - wr_hi.astype(jnp.float32)).astype(jnp.bfloat16)
    br = jnp.concatenate([b_gr, b_er, jnp.zeros((pad,), jnp.float32)])[None, :]
    tri = (jnp.arange(tm)[:, None] > jnp.arange(tm)[None, :]).astype(jnp.bfloat16)
    full = lambda shape: pl.BlockSpec(shape, lambda i: (0,) * len(shape))
    nt = n // tm
    cur = lambda i: jnp.minimum(i, nt - 1)
    gate = lambda c0: pl.BlockSpec((tm, half), lambda i: (cur(i), c0 // half))
    return pl.pallas_call(
        _merge_kernel,
        grid=(nt + 1,),
        in_specs=[
            pl.BlockSpec((tm, D_MODEL), lambda i: (cur(i), 0)),
            pl.BlockSpec((tm, DA_WIDTH), lambda i: (cur(i), 0)),
            pl.BlockSpec((tm, RET_WIDTH), lambda i: (cur(i), 0)),
            gate(COL_GATE_A), gate(COL_GATE_A + half), gate(COL_GATE_B), gate(COL_GATE_B + half),
            full((DA_WIDTH, D_MODEL)), full((RET_WIDTH, D_MODEL)), full((D_MODEL, D_MODEL)),
            full((1, D_MODEL)), full((D_MODEL, LANES)), full((D_MODEL, LANES)), full((1, LANES)),
            full((tm, tm)),
        ],
        out_specs=[
            pl.BlockSpec((tm, D_MODEL), lambda i: (cur(i), 0)),
            pl.BlockSpec((ROW_PIECES, tm, PIECE), lambda i: (0, cur(i), 0)),
            pl.BlockSpec((tm, LANES), lambda i: (jnp.maximum(i - 1, 0), 0)),
            pl.BlockSpec((8, tm), lambda i: (0, jnp.maximum(i - 1, 0))),
            pl.BlockSpec((8, LANES), lambda i: (0, 0)),
        ],
        out_shape=[
            jax.ShapeDtypeStruct((n, D_MODEL), jnp.float32),
            jax.ShapeDtypeStruct((ROW_PIECES, n, PIECE), jnp.uint32),
            jax.ShapeDtypeStruct((n, LANES), jnp.float32),
            jax.ShapeDtypeStruct((8, n), jnp.float32),
            jax.ShapeDtypeStruct((8, LANES), jnp.float32),
        ],
        scratch_shapes=[pltpu.VMEM((8, LANES), jnp.float32), pltpu.VMEM((tm, LANES), jnp.float32)],
        compiler_params=pltpu.CompilerParams(dimension_semantics=("arbitrary",),
                                             vmem_limit_bytes=VMEM_LIMIT),
        name="merge",
    )(x2, oa, ob, proj, proj, proj, proj, wa.astype(jnp.bfloat16), wb.astype(jnp.bfloat16),
      wo.astype(jnp.bfloat16), g2.reshape(1, D_MODEL), wr_hi, wr_lo, br, tri)


def _sc_mesh():
    return plsc.VectorSubcoreMesh(core_axis_name="c", subcore_axis_name="s")


def _sc_scatter_rows(src, idx, out_rows, src_block):
    steps = idx.shape[1] // SC_WINDOW
    per_core = steps // SC_CORES

    @pl.kernel(out_type=jax.ShapeDtypeStruct((out_rows, PIECE), src.dtype), mesh=_sc_mesh())
    def scatter(src_hbm, idx_hbm, out_hbm):
        def body(src_vmem, idx_vmem):
            pltpu.sync_copy(src_vmem, out_hbm.at[idx_vmem.at[0]])

        pltpu.emit_pipeline(
            body,
            grid=(SC_CORES, per_core),
            in_specs=[pl.BlockSpec((SC_WINDOW, PIECE), lambda c, i: (src_block(c * per_core + i), 0)),
                      pl.BlockSpec((1, SC_WINDOW), lambda c, i: (0, c * per_core + i))],
            out_specs=[],
            core_axis_name=("c", "s"),
            dimension_semantics=(pltpu.PARALLEL, pltpu.PARALLEL),
        )(src_hbm, idx_hbm)

    return scatter(src, idx)


def _sc_gather_rows(table, idx):
    num = idx.shape[1]
    per_core = num // SC_WINDOW // SC_CORES

    @pl.kernel(out_type=jax.ShapeDtypeStruct((num, PIECE), table.dtype), mesh=_sc_mesh())
    def gather(table_hbm, idx_hbm, out_hbm):
        def body(idx_vmem, out_vmem):
            pltpu.sync_copy(table_hbm.at[idx_vmem.at[0]], out_vmem)

        pltpu.emit_pipeline(
            body,
            grid=(SC_CORES, per_core),
            in_specs=[pl.BlockSpec((1, SC_WINDOW), lambda c, i: (0, c * per_core + i))],
            out_specs=[pl.BlockSpec((SC_WINDOW, PIECE), lambda c, i: (c * per_core + i, 0))],
            core_axis_name=("c", "s"),
            dimension_semantics=(pltpu.PARALLEL, pltpu.PARALLEL),
        )(idx_hbm, out_hbm)

    return gather(table, idx)


def _store_packed(ref, val):
    as_bits = lambda v: lax.bitcast_convert_type(v.astype(jnp.bfloat16).astype(jnp.float32), jnp.uint32)
    words = (as_bits(val[:, :PACKED]) >> 16) | (as_bits(val[:, PACKED:]) & jnp.uint32(0xFFFF0000))
    for j in range(ROW_PIECES):
        ref[j, :val.shape[0], :] = words[:, j * PIECE:(j + 1) * PIECE]


def _load_packed(ref, rows=None):
    rows = ref.shape[1] if rows is None else rows
    words = jnp.concatenate([ref[j, :rows, :] for j in range(ROW_PIECES)], axis=1)
    low = lax.bitcast_convert_type(words << 16, jnp.float32)
    high = lax.bitcast_convert_type(words & jnp.uint32(0xFFFF0000), jnp.float32)
    return jnp.concatenate([low, high], axis=1)


def _expert_kernel(blk_e_ref, n_used_ref, nxt_ref, run_ref, rows_ref, x_ref, wg_hbm, wu_hbm, wd_hbm, o_ref,
                   wg_stage, wu_stage, wd_stage, wg_scr, wu_scr, wd_scr, sem):
    i = pl.program_id(0)
    used = i < n_used_ref[0]

    def weight_copies(e, s):
        return (pltpu.make_async_copy(wg_hbm.at[e], wg_stage.at[s], sem.at[s, 0]),
                pltpu.make_async_copy(wu_hbm.at[e], wu_stage.at[s], sem.at[s, 1]),
                pltpu.make_async_copy(wd_hbm.at[e], wd_stage.at[s], sem.at[s, 2]))

    @pl.when(i == 0)
    def _():
        for c in weight_copies(blk_e_ref[0], 0):
            c.start()

    @pl.when(used & ((i == 0) | (blk_e_ref[i] != blk_e_ref[jnp.maximum(i - 1, 0)])))
    def _():
        s = run_ref[i] % 2
        for c in weight_copies(blk_e_ref[i], s):
            c.wait()
        wg_scr[...] = wg_stage[s].astype(jnp.bfloat16)
        wu_scr[...] = wu_stage[s].astype(jnp.bfloat16)
        wd_scr[...] = wd_stage[s].astype(jnp.bfloat16)

        @pl.when(nxt_ref[i] >= 0)
        def _():
            for c in weight_copies(nxt_ref[i], 1 - s):
                c.start()

    def expert_mlp(rows):
        x = _load_packed(x_ref, rows).astype(jnp.bfloat16)
        a = _dot(x, wg_scr[...])
        u = _dot(x, wu_scr[...])
        hmid = (a * _sigmoid(a) * u).astype(jnp.bfloat16)
        _store_packed(o_ref, _dot(hmid, wd_scr[...]))

    half = MOE_BLOCK // 2
    short = rows_ref[i] <= half

    @pl.when(used & jnp.logical_not(short))
    def _():
        expert_mlp(MOE_BLOCK)

    @pl.when(used & short)
    def _():
        expert_mlp(half)
        o_ref[:, half:, :] = jnp.zeros((ROW_PIECES, MOE_BLOCK - half, PIECE), o_ref.dtype)

    @pl.when(jnp.logical_not(used))
    def _():
        o_ref[...] = jnp.zeros(o_ref.shape, o_ref.dtype)


def _experts(xs, blk_expert, blk_rows, n_used, w_gate, w_up, w_down):
    p = xs.shape[1]
    nblk = p // MOE_BLOCK
    idx = jnp.arange(nblk, dtype=jnp.int32)
    starts = (idx < n_used[0]) & ((idx == 0) | (blk_expert != jnp.roll(blk_expert, 1)))
    run = jnp.cumsum(starts.astype(jnp.int32)) - 1
    next_start = lax.cummin(jnp.where(starts, idx, nblk)[::-1])[::-1]
    after = jnp.concatenate([next_start[1:], jnp.full((1,), nblk, jnp.int32)])
    nxt = jnp.where(after < nblk, blk_expert[jnp.minimum(after, nblk - 1)], -1).astype(jnp.int32)
    live = lambda i, be, nu, *_: jnp.minimum(i, nu[0] - 1)
    any_spec = pl.BlockSpec(memory_space=pl.ANY)
    return pl.pallas_call(
        _expert_kernel,
        grid_spec=pltpu.PrefetchScalarGridSpec(
            num_scalar_prefetch=5,
            grid=(nblk,),
            in_specs=[
                pl.BlockSpec((ROW_PIECES, MOE_BLOCK, PIECE), lambda i, *pre: (0, live(i, *pre), 0)),
                any_spec, any_spec, any_spec,
            ],
            out_specs=pl.BlockSpec((ROW_PIECES, MOE_BLOCK, PIECE), lambda i, *pre: (0, i, 0)),
            scratch_shapes=[pltpu.VMEM((2, D_MODEL, EXPERT_FF), jnp.float32),
                            pltpu.VMEM((2, D_MODEL, EXPERT_FF), jnp.float32),
                            pltpu.VMEM((2, EXPERT_FF, D_MODEL), jnp.float32),
                            pltpu.VMEM((D_MODEL, EXPERT_FF), jnp.bfloat16),
                            pltpu.VMEM((D_MODEL, EXPERT_FF), jnp.bfloat16),
                            pltpu.VMEM((EXPERT_FF, D_MODEL), jnp.bfloat16),
                            pltpu.SemaphoreType.DMA((2, 3))],
        ),
        out_shape=jax.ShapeDtypeStruct((ROW_PIECES, p, PIECE), jnp.uint32),
        compiler_params=pltpu.CompilerParams(dimension_semantics=("arbitrary",),
                                             vmem_limit_bytes=VMEM_LIMIT),
        name="experts",
    )(blk_expert, n_used, nxt, run.astype(jnp.int32), blk_rows, xs, w_gate, w_up, w_down)


def _combine_kernel(x1_ref, route_ref, y0_ref, y1_ref, *rest):
    o_ref = rest[-1]
    route = route_ref[...]
    o_ref[...] = x1_ref[...] + route[:, 2:3] * _load_packed(y0_ref) + route[:, 3:4] * _load_packed(y1_ref)


def _combine(x1, yg, route, row0, out_prev):
    n = yg.shape[1] // TOP_K
    tm = min(PROJ_ROWS, n)
    blk0 = row0 // tm
    prev = () if out_prev is None else (out_prev,)
    return pl.pallas_call(
        _combine_kernel,
        grid=(n // tm,),
        in_specs=[
            pl.BlockSpec((tm, D_MODEL), lambda i: (blk0 + i, 0)),
            pl.BlockSpec((tm, LANES), lambda i: (blk0 + i, 0)),
            pl.BlockSpec((ROW_PIECES, tm, PIECE), lambda i: (0, i, 0)),
            pl.BlockSpec((ROW_PIECES, tm, PIECE), lambda i: (0, i + n // tm, 0)),
        ] + [pl.BlockSpec(memory_space=pl.ANY)] * len(prev),
        out_specs=pl.BlockSpec((tm, D_MODEL), lambda i: (blk0 + i, 0)),
        out_shape=jax.ShapeDtypeStruct(x1.shape, jnp.float32),
        input_output_aliases={4: 0} if prev else {},
        compiler_params=pltpu.CompilerParams(dimension_semantics=("arbitrary",),
                                             vmem_limit_bytes=VMEM_LIMIT),
        name="combine",
    )(x1, route, yg, yg, *prev)


def _dispatch_plan(route_cols, counts, n):
    counts = counts[0, :N_EXPERTS].astype(jnp.int32)
    padded = ((counts + MOE_BLOCK - 1) // MOE_BLOCK) * MOE_BLOCK
    seg_end = jnp.cumsum(padded).astype(jnp.int32)
    seg_start = seg_end - padded
    cols = route_cols.astype(jnp.int32)
    e, rank = cols[0:TOP_K], cols[4:4 + TOP_K]
    picked = e[None] == jnp.arange(N_EXPERTS, dtype=jnp.int32)[:, None, None]
    dest = jnp.sum(jnp.where(picked, seg_start[:, None, None], 0), axis=0) + rank
    p = n * TOP_K + N_EXPERTS * MOE_BLOCK
    slot = dest[None] + (jnp.arange(ROW_PIECES, dtype=jnp.int32) * p)[:, None, None]
    blk_start = jnp.arange(p // MOE_BLOCK, dtype=jnp.int32) * MOE_BLOCK
    blk_expert = jnp.sum((seg_end[None, :] <= blk_start[:, None]).astype(jnp.int32), axis=1)
    blk_expert = jnp.minimum(blk_expert, N_EXPERTS - 1)
    of_block = blk_expert[:, None] == jnp.arange(N_EXPERTS, dtype=jnp.int32)[None, :]
    real_end = jnp.sum(jnp.where(of_block, (seg_start + counts)[None, :], 0), axis=1)
    blk_rows = jnp.clip(real_end - blk_start, 0, MOE_BLOCK)
    n_used = (seg_end[-1] // MOE_BLOCK).reshape(1)
    return slot, blk_expert, blk_rows, n_used, p


def _layer(x, positions, norm1_g, w_in, q_norm_g, k_norm_g, lam4, diff_subln_g, ret_gn_g, ret_gn_b,
           w_branch_a, w_branch_b, w_out, norm2_g, w_gr, b_gr, w_er, b_er, w_gate, w_up, w_down):
    batch, seq, _ = x.shape
    n = batch * seq
    x2 = x.reshape(n, D_MODEL)
    proj = _in_proj(x2, positions.reshape(1, n), norm1_g, w_in, q_norm_g, k_norm_g)
    oa = _diff_attn(proj, lam4, diff_subln_g.reshape(1, DA_VDIM), batch, seq)
    ob = _retention(proj, ret_gn_g, ret_gn_b, batch, seq)
    x1, h2, route, route_cols, counts = _merge(x2, oa, ob, proj, w_branch_a, w_branch_b, w_out, norm2_g,
                                               w_gr, b_gr, w_er, b_er)
    slot, blk_expert, blk_rows, n_used, p = _dispatch_plan(route_cols, counts, n)
    win_n = n // SC_WINDOW
    src_block = lambda s: (s // (TOP_K * win_n)) * win_n + s % win_n
    xs = _sc_scatter_rows(h2.reshape(ROW_PIECES * n, PIECE), slot.reshape(1, -1), ROW_PIECES * p, src_block)
    ys = _experts(xs.reshape(ROW_PIECES, p, PIECE), blk_expert, blk_rows, n_used, w_gate, w_up, w_down)
    parts = COMBINE_PARTS if n % (COMBINE_PARTS * PROJ_ROWS) == 0 else 1
    m = n // parts
    out = None
    for t in range(parts):
        yg = _sc_gather_rows(ys.reshape(ROW_PIECES * p, PIECE), slot[:, :, t * m:(t + 1) * m].reshape(1, -1))
        out = _combine(x1, yg.reshape(ROW_PIECES, TOP_K * m, PIECE), route, t * m, out)
    return out.reshape(batch, seq, D_MODEL)


def kernel(x, positions, norm1_g, w_in, q_norm_g, k_norm_g, lambda_q1, lambda_k1, lambda_q2, lambda_k2, diff_subln_g, ret_gn_g, ret_gn_b, w_branch_a, w_branch_b, w_out, norm2_g, w_group_router, b_group_router, w_expert_router, b_expert_router, w_gate, w_up, w_down):
    assert x.shape[-1] == D_MODEL and norm1_g.shape[0] == 1, "single-layer, D_MODEL-wide input expected"
    lam4 = jnp.pad(jnp.stack([lambda_q1[0], lambda_k1[0], lambda_q2[0], lambda_k2[0]]), ((0, 0), (0, LANES - DA_HALF)))
    return _layer(x, positions, norm1_g[0], w_in[0], q_norm_g[0], k_norm_g[0], lam4, diff_subln_g[0],
                  ret_gn_g[0], ret_gn_b[0], w_branch_a[0], w_branch_b[0], w_out[0], norm2_g[0],
                  w_group_router[0], b_group_router[0], w_expert_router[0], b_expert_router[0],
                  w_gate[0], w_up[0], w_down[0])
```

```python
import functools
import math

import jax
import jax.numpy as jnp
from jax import lax
from jax.experimental import pallas as pl
from jax.experimental.pallas import tpu as pltpu
from jax.experimental.pallas import tpu_sc as plsc

D_MODEL = 1024
DA_HEADS = 4
DA_HALF = 64
DA_VDIM = 2 * DA_HALF
DA_WIDTH = DA_HEADS * DA_VDIM
ROPE_THETA = 500000.0
ROPE_DIM = DA_HALF // 4
RET_HEADS = 4
RET_KDIM = 128
RET_VDIM = 128
RET_WIDTH = RET_HEADS * RET_VDIM
RET_THETA = 10000.0
N_GROUPS = 4
EXPERTS_PER_GROUP = 8
N_EXPERTS = N_GROUPS * EXPERTS_PER_GROUP
TOP_K = 2
EXPERT_FF = 512
EPS = 1e-6
LAMBDA_INIT = 0.8 - 0.6 * math.exp(-0.3 * 0)

LANES = 128
IN_COLS = 3 * DA_WIDTH + 4 * RET_WIDTH + 2 * D_MODEL
COL_QA, COL_KA, COL_VA = 0, DA_WIDTH, 2 * DA_WIDTH
COL_QR = 3 * DA_WIDTH
COL_KR = COL_QR + RET_WIDTH
COL_VR = COL_KR + RET_WIDTH
COL_GB = COL_VR + RET_WIDTH
COL_GATE_A = COL_GB + RET_WIDTH
COL_GATE_B = COL_GATE_A + D_MODEL

PROJ_ROWS = 512
IN_PROJ_ROWS = 1024
PROJ_CHUNK = 256
ATT_TILE = 1024
ATT_ROWS = 32
RET_CHUNK = 256
RET_UNROLL = 16
MOE_BLOCK = 512
PACKED = D_MODEL // 2
ROW_PIECES = 2
PIECE = PACKED // ROW_PIECES
SC_CORES = 2
SC_WINDOW = 128
COMBINE_FIRST_16THS = 5
VMEM_LIMIT = 56 * 1024 * 1024


def _dot(a, b):
    return jnp.dot(a, b, preferred_element_type=jnp.float32)


def _dot_nt(a, b):
    return lax.dot_general(a, b, (((1,), (1,)), ((), ())), preferred_element_type=jnp.float32)


def _dot_tn(a, b):
    return lax.dot_general(a, b, (((0,), (0,)), ((), ())), preferred_element_type=jnp.float32)


def _sigmoid(x):
    return 0.5 * jnp.tanh(0.5 * x) + 0.5


def _split3(x):
    a = x.astype(jnp.bfloat16)
    r = x - a.astype(jnp.float32)
    b = r.astype(jnp.bfloat16)
    c = (r - b.astype(jnp.float32)).astype(jnp.bfloat16)
    return a, b, c


def _in_proj_kernel(x_ref, pos_ref, g1_ref, w_ref, gsum_ref, gq_ref, gk_ref, fa_ref, fr_ref, sel_ref,
                    o_ref, h_scr):
    x = x_ref[...]
    rows = x.shape[0]
    h_scr[...] = (x * g1_ref[...]).astype(jnp.bfloat16)
    rms_scale = jnp.broadcast_to(lax.rsqrt(jnp.mean(x * x, axis=-1, keepdims=True) + EPS), (rows, PROJ_CHUNK))
    pos = pos_ref[...].astype(jnp.float32)

    lane = lax.broadcasted_iota(jnp.int32, (rows, LANES), 1)
    half_a = ROPE_DIM // 2
    tables = {}

    def da_tables():
        if "da" not in tables:
            ang_a = fa_ref[...] * pos
            pad = jnp.zeros((LANES - 2 * half_a, rows), jnp.float32)
            t_a = jnp.concatenate([jnp.cos(ang_a), jnp.sin(ang_a), pad], axis=0).T
            tab = sum(_dot(part, sel_ref[...]) for part in _split3(t_a))
            c_a = tab[:, :LANES] + jnp.where(lane % DA_HALF < ROPE_DIM, 0.0, 1.0)
            s_lo = tab[:, LANES:2 * LANES]
            s_hi = tab[:, 2 * LANES:]
            tables["da"] = tuple(jnp.concatenate([v, v], axis=1) for v in (c_a, s_lo, s_hi))
        return tables["da"]

    def ret_tables():
        if "ret" not in tables:
            ang_r = fr_ref[...] * pos
            t_r = jnp.concatenate([jnp.cos(ang_r), jnp.sin(ang_r)], axis=0).T
            sw_r = pltpu.roll(t_r, RET_KDIM // 2, axis=1)
            first = lane < RET_KDIM // 2
            c_r = jnp.where(first, t_r, sw_r)
            s_r = jnp.where(first, -sw_r, t_r)
            tables["ret"] = tuple(jnp.concatenate([v, v], axis=1) for v in (c_r, s_r))
        return tables["ret"]

    def qk_norm_rope(y, g, scale):
        c_a2, s_lo2, s_hi2 = da_tables()
        ss = y * y
        hi = ss.astype(jnp.bfloat16)
        lo = (ss - hi.astype(jnp.float32)).astype(jnp.bfloat16)
        gs = _dot(hi, gsum_ref[...]) + _dot(lo, gsum_ref[...])
        n = y * lax.rsqrt(gs * (1.0 / DA_HALF) + EPS) * g
        up = pltpu.roll(n, PROJ_CHUNK - half_a, axis=1)
        dn = pltpu.roll(n, half_a, axis=1)
        r = n * c_a2 + up * s_lo2 + dn * s_hi2
        return r * scale if scale != 1.0 else r

    def ret_rope(y, scale):
        c_r2, s_r2 = ret_tables()
        halves = [pltpu.roll(y[:, i * LANES:(i + 1) * LANES], RET_KDIM // 2, axis=1)
                  for i in range(PROJ_CHUNK // LANES)]
        sw = jnp.concatenate(halves, axis=1)
        r = y * c_r2 + sw * s_r2
        return r * scale if scale != 1.0 else r

    n_chunks = IN_COLS // PROJ_CHUNK
    is_long = lambda c: c * PROJ_CHUNK < COL_VA or COL_QR <= c * PROJ_CHUNK < COL_VR
    long_chunks = [c for c in range(n_chunks) if is_long(c)]
    short_chunks = [c for c in range(n_chunks) if not is_long(c)][::-1]
    order = [short_chunks.pop(0) for _ in range(3)]
    while long_chunks or short_chunks:
        if long_chunks:
            order.append(long_chunks.pop(0))
        if short_chunks:
            order.append(short_chunks.pop(0))
    for c in order:
        c0 = c * PROJ_CHUNK
        y = _dot(h_scr[...], w_ref[:, c0:c0 + PROJ_CHUNK]) * rms_scale
        if c0 < COL_KA:
            y = qk_norm_rope(y, gq_ref[...], DA_HALF ** -0.5)
        elif c0 < COL_VA:
            y = qk_norm_rope(y, gk_ref[...], 1.0)
        elif c0 < COL_QR:
            pass
        elif c0 < COL_KR:
            y = ret_rope(y, 1.0)
        elif c0 < COL_VR:
            y = ret_rope(y, RET_KDIM ** -0.5)
        elif c0 < COL_GB:
            pass
        elif c0 < COL_GATE_A:
            y = y * _sigmoid(y)
        else:
            y = _sigmoid(y)
        o_ref[:, c0:c0 + PROJ_CHUNK] = y.astype(o_ref.dtype)


def _in_proj(x2, pos2, g1, w_in, gq, gk):
    n = x2.shape[0]
    tm = min(IN_PROJ_ROWS, n)
    grp = jnp.arange(PROJ_CHUNK) // DA_HALF
    gsum = (grp[:, None] == grp[None, :]).astype(jnp.bfloat16)
    half_a = ROPE_DIM // 2
    fa = jnp.power(jnp.float32(ROPE_THETA), -2.0 * jnp.arange(half_a, dtype=jnp.float32) / ROPE_DIM)[:, None]
    half_r = RET_KDIM // 2
    fr = jnp.power(jnp.float32(RET_THETA), -2.0 * jnp.arange(half_r, dtype=jnp.float32) / RET_KDIM)[:, None]
    j = jnp.arange(LANES)[:, None]
    l64 = (jnp.arange(LANES) % DA_HALF)[None, :]
    sel_c = (j < half_a) & (l64 < ROPE_DIM) & (l64 % half_a == j)
    sel_lo = (j >= half_a) & (j < ROPE_DIM) & (l64 < half_a) & (l64 == j - half_a)
    sel_hi = (j >= half_a) & (j < ROPE_DIM) & (l64 >= half_a) & (l64 < ROPE_DIM) & (l64 == j)
    sel = jnp.concatenate([sel_c.astype(jnp.float32), -sel_lo.astype(jnp.float32),
                           sel_hi.astype(jnp.float32)], axis=1).astype(jnp.bfloat16)
    reps = PROJ_CHUNK // DA_HALF
    full = lambda shape: pl.BlockSpec(shape, lambda i: (0,) * len(shape))
    return pl.pallas_call(
        _in_proj_kernel,
        grid=(n // tm,),
        in_specs=[
            pl.BlockSpec((tm, D_MODEL), lambda i: (i, 0)),
            pl.BlockSpec((1, tm), lambda i: (0, i)),
            full((1, D_MODEL)),
            pl.BlockSpec((D_MODEL, IN_COLS), lambda i: (0, 0), pipeline_mode=pl.Buffered(1)),
            full((PROJ_CHUNK, PROJ_CHUNK)),
            full((1, PROJ_CHUNK)),
            full((1, PROJ_CHUNK)),
            full((half_a, 1)),
            full((half_r, 1)),
            full((LANES, 3 * LANES)),
        ],
        out_specs=pl.BlockSpec((tm, IN_COLS), lambda i: (i, 0)),
        out_shape=jax.ShapeDtypeStruct((n, IN_COLS), jnp.bfloat16),
        scratch_shapes=[pltpu.VMEM((tm, D_MODEL), jnp.bfloat16)],
        compiler_params=pltpu.CompilerParams(dimension_semantics=("arbitrary",),
                                             vmem_limit_bytes=VMEM_LIMIT),
        name="in_proj",
    )(x2, pos2, g1.reshape(1, D_MODEL), w_in.astype(jnp.bfloat16), gsum,
      jnp.tile(gq, reps)[None, :], jnp.tile(gk, reps)[None, :], fa, fr, sel)


def _diff_attn_kernel(q_ref, k_ref, v_ref, lam_ref, gsub_ref, o_ref,
                      qs_scr, vx_scr, s0_scr, s1_scr, p_scr, m_scr, alpha_scr, acc_scr):
    i = pl.program_id(2)
    t = q_ref.shape[0]

    @pl.when(i == 0)
    def _():
        vx_scr[:, :DA_VDIM] = v_ref[...]
        vx_scr[:, DA_VDIM:] = jnp.ones((vx_scr.shape[0], LANES), vx_scr.dtype)

    q = q_ref[...]
    lane = lax.broadcasted_iota(jnp.int32, q.shape, 1)
    zero = jnp.zeros_like(q)
    qs_scr[:t] = jnp.where(lane < DA_HALF, q, zero)
    qs_scr[t:] = jnp.where(lane >= DA_HALF, q, zero)

    def scores(j, s_ref):
        start = pl.multiple_of(j * t, t)
        s_ref[...] = _dot_nt(qs_scr[...], k_ref[pl.ds(start, t), :])

    def softmax_pv(j, s_ref, masked, first=False):
        for c in range(2 * t // ATT_ROWS):
            rows = pl.ds(c * ATT_ROWS, ATT_ROWS)
            s = s_ref[rows, :]
            if masked:
                r = lax.broadcasted_iota(jnp.int32, s.shape, 0) + (c * ATT_ROWS) % t
                col = lax.broadcasted_iota(jnp.int32, s.shape, 1)
                s = jnp.where(col <= r, s, -jnp.inf)
            m_cur = jnp.max(s, axis=-1, keepdims=True)
            if first:
                m_new = jnp.broadcast_to(m_cur, (ATT_ROWS, LANES))
            else:
                m_prev = m_scr[rows, :]
                m_new = jnp.maximum(m_prev, m_cur)
                alpha_scr[rows, :] = jnp.exp(m_prev - m_new)
            m_scr[rows, :] = m_new
            p = jnp.exp(s - jnp.concatenate([m_new] * (t // LANES), axis=1))
            p_scr[rows, :] = p.astype(p_scr.dtype)
        start = pl.multiple_of(j * t, t)
        pv = _dot(p_scr[...], vx_scr[pl.ds(start, t), :])
        if first:
            acc_scr[...] = pv
            return
        alpha = alpha_scr[...]
        for half in range(2):
            cols = pl.ds(half * LANES, LANES)
            acc_scr[:, cols] = alpha * acc_scr[:, cols] + pv[:, half * LANES:(half + 1) * LANES]

    scores(0, s0_scr)

    @pl.when(i == 0)
    def _():
        softmax_pv(0, s0_scr, True, first=True)

    @pl.when(i == 1)
    def _():
        scores(1, s1_scr)
        softmax_pv(0, s0_scr, False, first=True)
        softmax_pv(1, s1_scr, True)

    @pl.when(i >= 2)
    def _():
        scores(1, s1_scr)
        softmax_pv(0, s0_scr, False, first=True)
        scores(2, s0_scr)
        softmax_pv(1, s1_scr, False)

        def pair(jj, carry):
            j = 2 * jj
            scores(j + 1, s1_scr)
            softmax_pv(j, s0_scr, False)
            scores(j + 2, s0_scr)
            softmax_pv(j + 1, s1_scr, False)
            return carry

        lax.fori_loop(1, i // 2, pair, 0)

        @pl.when(i % 2 == 1)
        def _():
            scores(i, s1_scr)
            softmax_pv(i - 1, s0_scr, False)
            softmax_pv(i, s1_scr, True)

        @pl.when(i % 2 == 0)
        def _():
            softmax_pv(i, s0_scr, True)

    lam4 = lam_ref[...]
    lam = (jnp.exp(jnp.sum(lam4[0:1] * lam4[1:2], axis=-1, keepdims=True))
           - jnp.exp(jnp.sum(lam4[2:3] * lam4[3:4], axis=-1, keepdims=True)) + LAMBDA_INIT)
    o_all = acc_scr[:, :DA_VDIM] / acc_scr[:, DA_VDIM:]
    o = o_all[:t] - lam * o_all[t:]
    o = o * lax.rsqrt(jnp.mean(o * o, axis=-1, keepdims=True) + EPS) * gsub_ref[...] * (1.0 - LAMBDA_INIT)
    o_ref[...] = o.astype(o_ref.dtype)


def _diff_attn(proj, lam4, gsub, batch, seq):
    n = proj.shape[0]
    t = min(ATT_TILE, seq)
    nq = seq // t
    qb, kb, vb = COL_QA // LANES, COL_KA // LANES, COL_VA // LANES
    return pl.pallas_call(
        _diff_attn_kernel,
        grid=(batch, DA_HEADS, nq),
        in_specs=[
            pl.BlockSpec((t, LANES), lambda b, h, i: (b * nq + i, qb + h)),
            pl.BlockSpec((seq, LANES), lambda b, h, i: (b, kb + h)),
            pl.BlockSpec((seq, LANES), lambda b, h, i: (b, vb + h)),
            pl.BlockSpec((4, LANES), lambda b, h, i: (0, 0)),
            pl.BlockSpec((1, LANES), lambda b, h, i: (0, 0)),
        ],
        out_specs=pl.BlockSpec((t, LANES), lambda b, h, i: (b * nq + i, h)),
        out_shape=jax.ShapeDtypeStruct((n, DA_WIDTH), jnp.bfloat16),
        scratch_shapes=[pltpu.VMEM((2 * t, LANES), jnp.bfloat16),
                        pltpu.VMEM((seq, DA_VDIM + LANES), jnp.bfloat16),
                        pltpu.VMEM((2 * t, t), jnp.float32),
                        pltpu.VMEM((2 * t, t), jnp.float32),
                        pltpu.VMEM((2 * t, t), jnp.bfloat16),
                        pltpu.VMEM((2 * t, LANES), jnp.float32),
                        pltpu.VMEM((2 * t, LANES), jnp.float32),
                        pltpu.VMEM((2 * t, DA_VDIM + LANES), jnp.float32)],
        compiler_params=pltpu.CompilerParams(dimension_semantics=("arbitrary",) * 3,
                                             vmem_limit_bytes=VMEM_LIMIT),
        name="diff_attn",
    )(proj, proj, proj, lam4, gsub)


def _retention_kernel(q_ref, k_ref, v_ref, g_ref, gng_ref, gnb_ref, o_ref, r_scr, *, chunk):
    hf = jnp.full((1, 1), pl.program_id(1), jnp.int32).astype(jnp.float32)
    log_g = jnp.log1p(-jnp.exp2(-5.0 - hf))
    ri = lax.broadcasted_iota(jnp.int32, (chunk, chunk), 0)
    ci = lax.broadcasted_iota(jnp.int32, (chunk, chunk), 1)
    rel = (ri - ci).astype(jnp.float32)
    dmask = jnp.where(rel >= 0, jnp.exp(jnp.maximum(rel, 0.0) * log_g), 0.0)
    idx = lax.broadcasted_iota(jnp.int32, (chunk, 1), 0).astype(jnp.float32)
    zeta = jnp.exp((chunk - 1 - idx) * log_g)
    xi = jnp.exp((idx + 1.0) * log_g)
    g_chunk = jnp.exp(chunk * log_g)
    r_scr[...] = jnp.zeros(r_scr.shape, jnp.float32)
    gng = gng_ref[...]
    gnb = gnb_ref[...]

    def body(n, carry):
        start = pl.multiple_of(n * chunk, chunk)
        q = q_ref[pl.ds(start, chunk), :]
        k = k_ref[pl.ds(start, chunk), :]
        v = v_ref[pl.ds(start, chunk), :]
        s = _dot_nt(q, k) * dmask
        r_old = r_scr[...]
        o = _dot(s.astype(jnp.bfloat16), v) + xi * _dot(q, r_old.astype(jnp.bfloat16))
        kz = (k.astype(jnp.float32) * zeta).astype(jnp.bfloat16)
        r_scr[...] = g_chunk * r_old + _dot_tn(kz, v)
        mu = jnp.mean(o, axis=-1, keepdims=True)
        d = o - mu
        var = jnp.mean(d * d, axis=-1, keepdims=True)
        y = d * lax.rsqrt(var + EPS) * gng + gnb
        y = y * g_ref[pl.ds(start, chunk), :].astype(jnp.float32)
        o_ref[pl.ds(start, chunk), :] = y.astype(o_ref.dtype)
        return carry

    lax.fori_loop(0, q_ref.shape[0] // chunk, body, 0, unroll=RET_UNROLL)


def _retention(proj, gn_g, gn_b, batch, seq):
    n = proj.shape[0]
    chunk = min(RET_CHUNK, seq)
    col = lambda c0: (lambda b, h: (b, c0 // LANES + h))
    return pl.pallas_call(
        functools.partial(_retention_kernel, chunk=chunk),
        grid=(batch, RET_HEADS),
        in_specs=[
            pl.BlockSpec((seq, LANES), col(COL_QR)),
            pl.BlockSpec((seq, LANES), col(COL_KR)),
            pl.BlockSpec((seq, LANES), col(COL_VR)),
            pl.BlockSpec((seq, LANES), col(COL_GB)),
            pl.BlockSpec((1, LANES), lambda b, h: (0, h)),
            pl.BlockSpec((1, LANES), lambda b, h: (0, h)),
        ],
        out_specs=pl.BlockSpec((seq, LANES), lambda b, h: (b, h)),
        out_shape=jax.ShapeDtypeStruct((n, RET_WIDTH), jnp.bfloat16),
        scratch_shapes=[pltpu.VMEM((RET_KDIM, RET_VDIM), jnp.float32)],
        compiler_params=pltpu.CompilerParams(dimension_semantics=("arbitrary",) * 2,
                                             vmem_limit_bytes=VMEM_LIMIT),
        name="retention",
    )(proj, proj, proj, proj, gn_g.reshape(1, RET_WIDTH), gn_b.reshape(1, RET_WIDTH))


def _merge_kernel(x_ref, oa_ref, ob_ref, sa0_ref, sa1_ref, sb0_ref, sb1_ref, wa_ref, wb_ref, wo_ref,
                  g2_ref, wr_hi_ref, wr_lo_ref, br_ref, tri_ref, x1_ref, h2_ref, route_ref, cols_ref, counts_ref,
                  base_scr, logits_scr):
    i = pl.program_id(0)

    @pl.when(i == 0)
    def _():
        base_scr[...] = jnp.zeros(base_scr.shape, jnp.float32)
        logits_scr[...] = jnp.zeros(logits_scr.shape, jnp.float32)

    ya = _dot(oa_ref[...], wa_ref[...])
    yb = _dot(ob_ref[...], wb_ref[...])

    logits = logits_scr[...]
    lane = lax.broadcasted_iota(jnp.int32, logits.shape, 1)
    neg = -jnp.inf
    gl = jnp.where(lane < N_GROUPS, logits, neg)
    gmax = jnp.max(gl, axis=-1, keepdims=True)
    g_idx = jnp.min(jnp.where(gl == gmax, lane, LANES), axis=-1, keepdims=True)
    p_g = 1.0 / jnp.sum(jnp.exp(gl - gmax), axis=-1, keepdims=True)
    e_lo = N_GROUPS + EXPERTS_PER_GROUP * g_idx
    el = jnp.where((lane >= e_lo) & (lane < e_lo + EXPERTS_PER_GROUP), logits, neg)
    v1 = jnp.max(el, axis=-1, keepdims=True)
    i1 = jnp.min(jnp.where(el == v1, lane, LANES), axis=-1, keepdims=True)
    el2 = jnp.where(lane == i1, neg, el)
    v2 = jnp.max(el2, axis=-1, keepdims=True)
    i2 = jnp.min(jnp.where(el2 == v2, lane, LANES), axis=-1, keepdims=True)
    t = jnp.exp(v2 - v1)
    w1 = p_g / (1.0 + t)
    w2 = p_g * t / (1.0 + t)
    e1 = i1 - N_GROUPS
    e2 = i2 - N_GROUPS

    oh1 = lane == e1
    oh2 = lane == e2
    real = jnp.where(i > 0, 1.0, 0.0)
    picked = jnp.where(oh1 | oh2, real, 0.0)
    before = _dot(tri_ref[...], picked.astype(jnp.bfloat16)) + base_scr[0:1, :]
    rank1 = jnp.sum(jnp.where(oh1, before, 0.0), axis=-1, keepdims=True)
    rank2 = jnp.sum(jnp.where(oh2, before, 0.0), axis=-1, keepdims=True)
    base_scr[...] = base_scr[...] + jnp.sum(picked, axis=0, keepdims=True)
    counts_ref[...] = base_scr[...]

    cols = [e1.astype(jnp.float32), e2.astype(jnp.float32), w1, w2, rank1, rank2]
    route = jnp.zeros(logits.shape, jnp.float32)
    for c, val in enumerate(cols):
        route = jnp.where(lane == c, val, route)
    route_ref[...] = route
    cols_ref[...] = route.T[:8]

    sa = jnp.concatenate([sa0_ref[...], sa1_ref[...]], axis=1).astype(jnp.float32)
    sb = jnp.concatenate([sb0_ref[...], sb1_ref[...]], axis=1).astype(jnp.float32)
    merged = sa * ya + sb * yb
    x1 = x_ref[...] + _dot(merged.astype(jnp.bfloat16), wo_ref[...])
    x1_ref[...] = x1
    h2 = x1 * lax.rsqrt(jnp.mean(x1 * x1, axis=-1, keepdims=True) + EPS) * g2_ref[...]
    _store_packed(h2_ref, h2)

    hi = h2.astype(jnp.bfloat16)
    lo = (h2 - hi.astype(jnp.float32)).astype(jnp.bfloat16)
    logits_scr[...] = (_dot(hi, wr_hi_ref[...]) + _dot(lo, wr_hi_ref[...]) + _dot(hi, wr_lo_ref[...])
                       + br_ref[...])


def _merge(x2, oa, ob, proj, wa, wb, wo, g2, w_gr, b_gr, w_er, b_er):
    n = x2.shape[0]
    tm = min(PROJ_ROWS, n)
    half = D_MODEL // 2
    pad = LANES - N_GROUPS - N_EXPERTS
    wr = jnp.concatenate([w_gr, w_er, jnp.zeros((D_MODEL, pad), jnp.float32)], axis=1)
    wr_hi = wr.astype(jnp.bfloat16)
    wr_lo = (wr - wr_hi.astype(jnp.float32)).astype(jnp.bfloat16)
    br = jnp.concatenate([b_gr, b_er, jnp.zeros((pad,), jnp.float32)])[None, :]
    tri = (jnp.arange(tm)[:, None] > jnp.arange(tm)[None, :]).astype(jnp.bfloat16)
    full = lambda shape: pl.BlockSpec(shape, lambda i: (0,) * len(shape))
    nt = n // tm
    cur = lambda i: jnp.minimum(i, nt - 1)
    gate = lambda c0: pl.BlockSpec((tm, half), lambda i: (cur(i), c0 // half))
    return pl.pallas_call(
        _merge_kernel,
        grid=(nt + 1,),
        in_specs=[
            pl.BlockSpec((tm, D_MODEL), lambda i: (cur(i), 0)),
            pl.BlockSpec((tm, DA_WIDTH), lambda i: (cur(i), 0)),
            pl.BlockSpec((tm, RET_WIDTH), lambda i: (cur(i), 0)),
            gate(COL_GATE_A), gate(COL_GATE_A + half), gate(COL_GATE_B), gate(COL_GATE_B + half),
            full((DA_WIDTH, D_MODEL)), full((RET_WIDTH, D_MODEL)), full((D_MODEL, D_MODEL)),
            full((1, D_MODEL)), full((D_MODEL, LANES)), full((D_MODEL, LANES)), full((1, LANES)),
            full((tm, tm)),
        ],
        out_specs=[
            pl.BlockSpec((tm, D_MODEL), lambda i: (cur(i), 0)),
            pl.BlockSpec((ROW_PIECES, tm, PIECE), lambda i: (0, cur(i), 0)),
            pl.BlockSpec((tm, LANES), lambda i: (jnp.maximum(i - 1, 0), 0)),
            pl.BlockSpec((8, tm), lambda i: (0, jnp.maximum(i - 1, 0))),
            pl.BlockSpec((8, LANES), lambda i: (0, 0)),
        ],
        out_shape=[
            jax.ShapeDtypeStruct((n, D_MODEL), jnp.float32),
            jax.ShapeDtypeStruct((ROW_PIECES, n, PIECE), jnp.uint32),
            jax.ShapeDtypeStruct((n, LANES), jnp.float32),
            jax.ShapeDtypeStruct((8, n), jnp.float32),
            jax.ShapeDtypeStruct((8, LANES), jnp.float32),
        ],
        scratch_shapes=[pltpu.VMEM((8, LANES), jnp.float32), pltpu.VMEM((tm, LANES), jnp.float32)],
        compiler_params=pltpu.CompilerParams(dimension_semantics=("arbitrary",),
                                             vmem_limit_bytes=VMEM_LIMIT),
        name="merge",
    )(x2, oa, ob, proj, proj, proj, proj, wa.astype(jnp.bfloat16), wb.astype(jnp.bfloat16),
      wo.astype(jnp.bfloat16), g2.reshape(1, D_MODEL), wr_hi, wr_lo, br, tri)


def _sc_mesh():
    return plsc.VectorSubcoreMesh(core_axis_name="c", subcore_axis_name="s")


def _sc_scatter_rows(src, idx, out_rows, src_block):
    steps = idx.shape[1] // SC_WINDOW
    per_core = steps // SC_CORES

    @pl.kernel(out_type=jax.ShapeDtypeStruct((out_rows, PIECE), src.dtype), mesh=_sc_mesh())
    def scatter(src_hbm, idx_hbm, out_hbm):
        def body(src_vmem, idx_vmem):
            pltpu.sync_copy(src_vmem, out_hbm.at[idx_vmem.at[0]])

        pltpu.emit_pipeline(
            body,
            grid=(SC_CORES, per_core),
            in_specs=[pl.BlockSpec((SC_WINDOW, PIECE), lambda c, i: (src_block(c * per_core + i), 0)),
                      pl.BlockSpec((1, SC_WINDOW), lambda c, i: (0, c * per_core + i))],
            out_specs=[],
            core_axis_name=("c", "s"),
            dimension_semantics=(pltpu.PARALLEL, pltpu.PARALLEL),
        )(src_hbm, idx_hbm)

    return scatter(src, idx)


def _sc_gather_rows(table, idx):
    num = idx.shape[1]
    per_core = num // SC_WINDOW // SC_CORES

    @pl.kernel(out_type=jax.ShapeDtypeStruct((num, PIECE), table.dtype), mesh=_sc_mesh())
    def gather(table_hbm, idx_hbm, out_hbm):
        def body(idx_vmem, out_vmem):
            pltpu.sync_copy(table_hbm.at[idx_vmem.at[0]], out_vmem)

        pltpu.emit_pipeline(
            body,
            grid=(SC_CORES, per_core),
            in_specs=[pl.BlockSpec((1, SC_WINDOW), lambda c, i: (0, c * per_core + i))],
            out_specs=[pl.BlockSpec((SC_WINDOW, PIECE), lambda c, i: (c * per_core + i, 0))],
            core_axis_name=("c", "s"),
            dimension_semantics=(pltpu.PARALLEL, pltpu.PARALLEL),
        )(idx_hbm, out_hbm)

    return gather(table, idx)


def _store_packed(ref, val):
    as_bits = lambda v: lax.bitcast_convert_type(v.astype(jnp.bfloat16).astype(jnp.float32), jnp.uint32)
    words = (as_bits(val[:, :PACKED]) >> 16) | (as_bits(val[:, PACKED:]) & jnp.uint32(0xFFFF0000))
    for j in range(ROW_PIECES):
        ref[j, :val.shape[0], :] = words[:, j * PIECE:(j + 1) * PIECE]


def _load_packed(ref, rows=None):
    rows = ref.shape[1] if rows is None else rows
    words = jnp.concatenate([ref[j, :rows, :] for j in range(ROW_PIECES)], axis=1)
    low = lax.bitcast_convert_type(words << 16, jnp.float32)
    high = lax.bitcast_convert_type(words & jnp.uint32(0xFFFF0000), jnp.float32)
    return jnp.concatenate([low, high], axis=1)


def _expert_kernel(blk_e_ref, n_used_ref, nxt_ref, run_ref, rows_ref, x_ref, wg_hbm, wu_hbm, wd_hbm, o_ref,
                   wg_stage, wu_stage, wd_stage, wg_scr, wu_scr, wd_scr, sem):
    i = pl.program_id(0)
    used = i < n_used_ref[0]

    def weight_copies(e, s):
        return (pltpu.make_async_copy(wg_hbm.at[e], wg_stage.at[s], sem.at[s, 0]),
                pltpu.make_async_copy(wu_hbm.at[e], wu_stage.at[s], sem.at[s, 1]),
                pltpu.make_async_copy(wd_hbm.at[e], wd_stage.at[s], sem.at[s, 2]))

    @pl.when(i == 0)
    def _():
        for c in weight_copies(blk_e_ref[0], 0):
            c.start()

    @pl.when(used & ((i == 0) | (blk_e_ref[i] != blk_e_ref[jnp.maximum(i - 1, 0)])))
    def _():
        s = run_ref[i] % 2
        for c in weight_copies(blk_e_ref[i], s):
            c.wait()
        wg_scr[...] = wg_stage[s].astype(jnp.bfloat16)
        wu_scr[...] = wu_stage[s].astype(jnp.bfloat16)
        wd_scr[...] = wd_stage[s].astype(jnp.bfloat16)

        @pl.when(nxt_ref[i] >= 0)
        def _():
            for c in weight_copies(nxt_ref[i], 1 - s):
                c.start()

    def expert_mlp(rows):
        x = _load_packed(x_ref, rows).astype(jnp.bfloat16)
        a = _dot(x, wg_scr[...])
        u = _dot(x, wu_scr[...])
        hmid = (a * _sigmoid(a) * u).astype(jnp.bfloat16)
        _store_packed(o_ref, _dot(hmid, wd_scr[...]))

    half = MOE_BLOCK // 2
    short = rows_ref[i] <= half

    @pl.when(used & jnp.logical_not(short))
    def _():
        expert_mlp(MOE_BLOCK)

    @pl.when(used & short)
    def _():
        expert_mlp(half)
        o_ref[:, half:, :] = jnp.zeros((ROW_PIECES, MOE_BLOCK - half, PIECE), o_ref.dtype)

    @pl.when(jnp.logical_not(used))
    def _():
        o_ref[...] = jnp.zeros(o_ref.shape, o_ref.dtype)


def _experts(xs, blk_expert, blk_rows, n_used, w_gate, w_up, w_down):
    p = xs.shape[1]
    nblk = p // MOE_BLOCK
    idx = jnp.arange(nblk, dtype=jnp.int32)
    starts = (idx < n_used[0]) & ((idx == 0) | (blk_expert != jnp.roll(blk_expert, 1)))
    run = jnp.cumsum(starts.astype(jnp.int32)) - 1
    next_start = lax.cummin(jnp.where(starts, idx, nblk)[::-1])[::-1]
    after = jnp.concatenate([next_start[1:], jnp.full((1,), nblk, jnp.int32)])
    nxt = jnp.where(after < nblk, blk_expert[jnp.minimum(after, nblk - 1)], -1).astype(jnp.int32)
    live = lambda i, be, nu, *_: jnp.minimum(i, nu[0] - 1)
    any_spec = pl.BlockSpec(memory_space=pl.ANY)
    return pl.pallas_call(
        _expert_kernel,
        grid_spec=pltpu.PrefetchScalarGridSpec(
            num_scalar_prefetch=5,
            grid=(nblk,),
            in_specs=[
                pl.BlockSpec((ROW_PIECES, MOE_BLOCK, PIECE), lambda i, *pre: (0, live(i, *pre), 0)),
                any_spec, any_spec, any_spec,
            ],
            out_specs=pl.BlockSpec((ROW_PIECES, MOE_BLOCK, PIECE), lambda i, *pre: (0, i, 0)),
            scratch_shapes=[pltpu.VMEM((2, D_MODEL, EXPERT_FF), jnp.float32),
                            pltpu.VMEM((2, D_MODEL, EXPERT_FF), jnp.float32),
                            pltpu.VMEM((2, EXPERT_FF, D_MODEL), jnp.float32),
                            pltpu.VMEM((D_MODEL, EXPERT_FF), jnp.bfloat16),
                            pltpu.VMEM((D_MODEL, EXPERT_FF), jnp.bfloat16),
                            pltpu.VMEM((EXPERT_FF, D_MODEL), jnp.bfloat16),
                            pltpu.SemaphoreType.DMA((2, 3))],
        ),
        out_shape=jax.ShapeDtypeStruct((ROW_PIECES, p, PIECE), jnp.uint32),
        compiler_params=pltpu.CompilerParams(dimension_semantics=("arbitrary",),
                                             vmem_limit_bytes=VMEM_LIMIT),
        name="experts",
    )(blk_expert, n_used, nxt, run.astype(jnp.int32), blk_rows, xs, w_gate, w_up, w_down)


def _combine_kernel(x1_ref, route_ref, y0_ref, y1_ref, *rest):
    o_ref = rest[-1]
    route = route_ref[...]
    o_ref[...] = x1_ref[...] + route[:, 2:3] * _load_packed(y0_ref) + route[:, 3:4] * _load_packed(y1_ref)


def _combine(x1, yg, route, row0, out_prev):
    n = yg.shape[1] // TOP_K
    tm = min(PROJ_ROWS, n)
    blk0 = row0 // tm
    prev = () if out_prev is None else (out_prev,)
    return pl.pallas_call(
        _combine_kernel,
        grid=(n // tm,),
        in_specs=[
            pl.BlockSpec((tm, D_MODEL), lambda i: (blk0 + i, 0)),
            pl.BlockSpec((tm, LANES), lambda i: (blk0 + i, 0)),
            pl.BlockSpec((ROW_PIECES, tm, PIECE), lambda i: (0, i, 0)),
            pl.BlockSpec((ROW_PIECES, tm, PIECE), lambda i: (0, i + n // tm, 0)),
        ] + [pl.BlockSpec(memory_space=pl.ANY)] * len(prev),
        out_specs=pl.BlockSpec((tm, D_MODEL), lambda i: (blk0 + i, 0)),
        out_shape=jax.ShapeDtypeStruct(x1.shape, jnp.float32),
        input_output_aliases={4: 0} if prev else {},
        compiler_params=pltpu.CompilerParams(dimension_semantics=("arbitrary",),
                                             vmem_limit_bytes=VMEM_LIMIT),
        name="combine",
    )(x1, route, yg, yg, *prev)


def _dispatch_plan(route_cols, counts, n):
    counts = counts[0, :N_EXPERTS].astype(jnp.int32)
    padded = ((counts + MOE_BLOCK - 1) // MOE_BLOCK) * MOE_BLOCK
    seg_end = jnp.cumsum(padded).astype(jnp.int32)
    seg_start = seg_end - padded
    cols = route_cols.astype(jnp.int32)
    e, rank = cols[0:TOP_K], cols[4:4 + TOP_K]
    picked = e[None] == jnp.arange(N_EXPERTS, dtype=jnp.int32)[:, None, None]
    dest = jnp.sum(jnp.where(picked, seg_start[:, None, None], 0), axis=0) + rank
    p = n * TOP_K + N_EXPERTS * MOE_BLOCK
    slot = dest[None] + (jnp.arange(ROW_PIECES, dtype=jnp.int32) * p)[:, None, None]
    blk_start = jnp.arange(p // MOE_BLOCK, dtype=jnp.int32) * MOE_BLOCK
    blk_expert = jnp.sum((seg_end[None, :] <= blk_start[:, None]).astype(jnp.int32), axis=1)
    blk_expert = jnp.minimum(blk_expert, N_EXPERTS - 1)
    of_block = blk_expert[:, None] == jnp.arange(N_EXPERTS, dtype=jnp.int32)[None, :]
    real_end = jnp.sum(jnp.where(of_block, (seg_start + counts)[None, :], 0), axis=1)
    blk_rows = jnp.clip(real_end - blk_start, 0, MOE_BLOCK)
    n_used = (seg_end[-1] // MOE_BLOCK).reshape(1)
    return slot, blk_expert, blk_rows, n_used, p


def _layer(x, positions, norm1_g, w_in, q_norm_g, k_norm_g, lam4, diff_subln_g, ret_gn_g, ret_gn_b,
           w_branch_a, w_branch_b, w_out, norm2_g, w_gr, b_gr, w_er, b_er, w_gate, w_up, w_down):
    batch, seq, _ = x.shape
    n = batch * seq
    x2 = x.reshape(n, D_MODEL)
    proj = _in_proj(x2, positions.reshape(1, n), norm1_g, w_in, q_norm_g, k_norm_g)
    oa = _diff_attn(proj, lam4, diff_subln_g.reshape(1, DA_VDIM), batch, seq)
    ob = _retention(proj, ret_gn_g, ret_gn_b, batch, seq)
    x1, h2, route, route_cols, counts = _merge(x2, oa, ob, proj, w_branch_a, w_branch_b, w_out, norm2_g,
                                               w_gr, b_gr, w_er, b_er)
    slot, blk_expert, blk_rows, n_used, p = _dispatch_plan(route_cols, counts, n)
    win_n = n // SC_WINDOW
    src_block = lambda s: (s // (TOP_K * win_n)) * win_n + s % win_n
    xs = _sc_scatter_rows(h2.reshape(ROW_PIECES * n, PIECE), slot.reshape(1, -1), ROW_PIECES * p, src_block)
    ys = _experts(xs.reshape(ROW_PIECES, p, PIECE), blk_expert, blk_rows, n_used, w_gate, w_up, w_down)
    first = (n * COMBINE_FIRST_16THS // 16) // PROJ_ROWS * PROJ_ROWS
    bounds = [0, first, n] if 0 < first < n else [0, n]
    out = None
    for lo, hi in zip(bounds[:-1], bounds[1:]):
        yg = _sc_gather_rows(ys.reshape(ROW_PIECES * p, PIECE), slot[:, :, lo:hi].reshape(1, -1))
        out = _combine(x1, yg.reshape(ROW_PIECES, TOP_K * (hi - lo), PIECE), route, lo, out)
    return out.reshape(batch, seq, D_MODEL)


def kernel(x, positions, norm1_g, w_in, q_norm_g, k_norm_g, lambda_q1, lambda_k1, lambda_q2, lambda_k2, diff_subln_g, ret_gn_g, ret_gn_b, w_branch_a, w_branch_b, w_out, norm2_g, w_group_router, b_group_router, w_expert_router, b_expert_router, w_gate, w_up, w_down):
    assert x.shape[-1] == D_MODEL and norm1_g.shape[0] == 1, "single-layer, D_MODEL-wide input expected"
    lam4 = jnp.pad(jnp.stack([lambda_q1[0], lambda_k1[0], lambda_q2[0], lambda_k2[0]]), ((0, 0), (0, LANES - DA_HALF)))
    return _layer(x, positions, norm1_g[0], w_in[0], q_norm_g[0], k_norm_g[0], lam4, diff_subln_g[0],
                  ret_gn_g[0], ret_gn_b[0], w_branch_a[0], w_branch_b[0], w_out[0], norm2_g[0],
                  w_group_router[0], b_group_router[0], w_expert_router[0], b_expert_router[0],
                  w_gate[0], w_up[0], w_down[0])
```

```python
import functools
import math

import jax
import jax.numpy as jnp
from jax import lax
from jax.experimental import pallas as pl
from jax.experimental.pallas import tpu as pltpu
from jax.experimental.pallas import tpu_sc as plsc

D_MODEL = 1024
DA_HEADS = 4
DA_HALF = 64
DA_VDIM = 2 * DA_HALF
DA_WIDTH = DA_HEADS * DA_VDIM
ROPE_THETA = 500000.0
ROPE_DIM = DA_HALF // 4
RET_HEADS = 4
RET_KDIM = 128
RET_VDIM = 128
RET_WIDTH = RET_HEADS * RET_VDIM
RET_THETA = 10000.0
N_GROUPS = 4
EXPERTS_PER_GROUP = 8
N_EXPERTS = N_GROUPS * EXPERTS_PER_GROUP
TOP_K = 2
EXPERT_FF = 512
EPS = 1e-6
LAMBDA_INIT = 0.8 - 0.6 * math.exp(-0.3 * 0)

LANES = 128
IN_COLS = 3 * DA_WIDTH + 4 * RET_WIDTH + 2 * D_MODEL
COL_QA, COL_KA, COL_VA = 0, DA_WIDTH, 2 * DA_WIDTH
COL_QR = 3 * DA_WIDTH
COL_KR = COL_QR + RET_WIDTH
COL_VR = COL_KR + RET_WIDTH
COL_GB = COL_VR + RET_WIDTH
COL_GATE_A = COL_GB + RET_WIDTH
COL_GATE_B = COL_GATE_A + D_MODEL

PROJ_ROWS = 512
IN_PROJ_ROWS = 1024
PROJ_CHUNK = 256
ATT_TILE = 512
ATT_HEADS_PER_STEP = 2
ATT_ROWS = 32
RET_CHUNK = 256
RET_UNROLL = 16
MOE_BLOCK = 512
PACKED = D_MODEL // 2
ROW_PIECES = 2
PIECE = PACKED // ROW_PIECES
SC_CORES = 2
SC_WINDOW = 128
COMBINE_PARTS = 2
VMEM_LIMIT = 56 * 1024 * 1024


def _dot(a, b):
    return jnp.dot(a, b, preferred_element_type=jnp.float32)


def _dot_nt(a, b):
    return lax.dot_general(a, b, (((1,), (1,)), ((), ())), preferred_element_type=jnp.float32)


def _dot_tn(a, b):
    return lax.dot_general(a, b, (((0,), (0,)), ((), ())), preferred_element_type=jnp.float32)


def _sigmoid(x):
    return 0.5 * jnp.tanh(0.5 * x) + 0.5


def _split3(x):
    a = x.astype(jnp.bfloat16)
    r = x - a.astype(jnp.float32)
    b = r.astype(jnp.bfloat16)
    c = (r - b.astype(jnp.float32)).astype(jnp.bfloat16)
    return a, b, c


def _in_proj_kernel(x_ref, pos_ref, g1_ref, w_ref, gsum_ref, gq_ref, gk_ref, fa_ref, fr_ref, sel_ref,
                    o_ref, h_scr):
    x = x_ref[...]
    rows = x.shape[0]
    h_scr[...] = (x * g1_ref[...]).astype(jnp.bfloat16)
    rms_scale = jnp.broadcast_to(lax.rsqrt(jnp.mean(x * x, axis=-1, keepdims=True) + EPS), (rows, PROJ_CHUNK))
    pos = pos_ref[...].astype(jnp.float32)

    lane = lax.broadcasted_iota(jnp.int32, (rows, LANES), 1)
    half_a = ROPE_DIM // 2
    tables = {}

    def da_tables():
        if "da" not in tables:
            ang_a = fa_ref[...] * pos
            pad = jnp.zeros((LANES - 2 * half_a, rows), jnp.float32)
            t_a = jnp.concatenate([jnp.cos(ang_a), jnp.sin(ang_a), pad], axis=0).T
            tab = sum(_dot(part, sel_ref[...]) for part in _split3(t_a))
            c_a = tab[:, :LANES] + jnp.where(lane % DA_HALF < ROPE_DIM, 0.0, 1.0)
            s_lo = tab[:, LANES:2 * LANES]
            s_hi = tab[:, 2 * LANES:]
            tables["da"] = tuple(jnp.concatenate([v, v], axis=1) for v in (c_a, s_lo, s_hi))
        return tables["da"]

    def ret_tables():
        if "ret" not in tables:
            ang_r = fr_ref[...] * pos
            t_r = jnp.concatenate([jnp.cos(ang_r), jnp.sin(ang_r)], axis=0).T
            sw_r = pltpu.roll(t_r, RET_KDIM // 2, axis=1)
            first = lane < RET_KDIM // 2
            c_r = jnp.where(first, t_r, sw_r)
            s_r = jnp.where(first, -sw_r, t_r)
            tables["ret"] = tuple(jnp.concatenate([v, v], axis=1) for v in (c_r, s_r))
        return tables["ret"]

    def qk_norm_rope(y, g, scale):
        c_a2, s_lo2, s_hi2 = da_tables()
        ss = y * y
        hi = ss.astype(jnp.bfloat16)
        lo = (ss - hi.astype(jnp.float32)).astype(jnp.bfloat16)
        gs = _dot(hi, gsum_ref[...]) + _dot(lo, gsum_ref[...])
        n = y * lax.rsqrt(gs * (1.0 / DA_HALF) + EPS) * g
        up = pltpu.roll(n, PROJ_CHUNK - half_a, axis=1)
        dn = pltpu.roll(n, half_a, axis=1)
        r = n * c_a2 + up * s_lo2 + dn * s_hi2
        return r * scale if scale != 1.0 else r

    def ret_rope(y, scale):
        c_r2, s_r2 = ret_tables()
        halves = [pltpu.roll(y[:, i * LANES:(i + 1) * LANES], RET_KDIM // 2, axis=1)
                  for i in range(PROJ_CHUNK // LANES)]
        sw = jnp.concatenate(halves, axis=1)
        r = y * c_r2 + sw * s_r2
        return r * scale if scale != 1.0 else r

    n_chunks = IN_COLS // PROJ_CHUNK
    is_long = lambda c: c * PROJ_CHUNK < COL_VA or COL_QR <= c * PROJ_CHUNK < COL_VR
    long_chunks = [c for c in range(n_chunks) if is_long(c)]
    short_chunks = [c for c in range(n_chunks) if not is_long(c)][::-1]
    order = [short_chunks.pop(0) for _ in range(3)]
    while long_chunks or short_chunks:
        if long_chunks:
            order.append(long_chunks.pop(0))
        if short_chunks:
            order.append(short_chunks.pop(0))
    for c in order:
        c0 = c * PROJ_CHUNK
        y = _dot(h_scr[...], w_ref[:, c0:c0 + PROJ_CHUNK]) * rms_scale
        if c0 < COL_KA:
            y = qk_norm_rope(y, gq_ref[...], DA_HALF ** -0.5)
        elif c0 < COL_VA:
            y = qk_norm_rope(y, gk_ref[...], 1.0)
        elif c0 < COL_QR:
            pass
        elif c0 < COL_KR:
            y = ret_rope(y, 1.0)
        elif c0 < COL_VR:
            y = ret_rope(y, RET_KDIM ** -0.5)
        elif c0 < COL_GB:
            pass
        elif c0 < COL_GATE_A:
            y = y * _sigmoid(y)
        else:
            y = _sigmoid(y)
        o_ref[:, c0:c0 + PROJ_CHUNK] = y.astype(o_ref.dtype)


def _in_proj(x2, pos2, g1, w_in, gq, gk):
    n = x2.shape[0]
    tm = min(IN_PROJ_ROWS, n)
    grp = jnp.arange(PROJ_CHUNK) // DA_HALF
    gsum = (grp[:, None] == grp[None, :]).astype(jnp.bfloat16)
    half_a = ROPE_DIM // 2
    fa = jnp.power(jnp.float32(ROPE_THETA), -2.0 * jnp.arange(half_a, dtype=jnp.float32) / ROPE_DIM)[:, None]
    half_r = RET_KDIM // 2
    fr = jnp.power(jnp.float32(RET_THETA), -2.0 * jnp.arange(half_r, dtype=jnp.float32) / RET_KDIM)[:, None]
    j = jnp.arange(LANES)[:, None]
    l64 = (jnp.arange(LANES) % DA_HALF)[None, :]
    sel_c = (j < half_a) & (l64 < ROPE_DIM) & (l64 % half_a == j)
    sel_lo = (j >= half_a) & (j < ROPE_DIM) & (l64 < half_a) & (l64 == j - half_a)
    sel_hi = (j >= half_a) & (j < ROPE_DIM) & (l64 >= half_a) & (l64 < ROPE_DIM) & (l64 == j)
    sel = jnp.concatenate([sel_c.astype(jnp.float32), -sel_lo.astype(jnp.float32),
                           sel_hi.astype(jnp.float32)], axis=1).astype(jnp.bfloat16)
    reps = PROJ_CHUNK // DA_HALF
    full = lambda shape: pl.BlockSpec(shape, lambda i: (0,) * len(shape))
    return pl.pallas_call(
        _in_proj_kernel,
        grid=(n // tm,),
        in_specs=[
            pl.BlockSpec((tm, D_MODEL), lambda i: (i, 0)),
            pl.BlockSpec((1, tm), lambda i: (0, i)),
            full((1, D_MODEL)),
            pl.BlockSpec((D_MODEL, IN_COLS), lambda i: (0, 0), pipeline_mode=pl.Buffered(1)),
            full((PROJ_CHUNK, PROJ_CHUNK)),
            full((1, PROJ_CHUNK)),
            full((1, PROJ_CHUNK)),
            full((half_a, 1)),
            full((half_r, 1)),
            full((LANES, 3 * LANES)),
        ],
        out_specs=pl.BlockSpec((tm, IN_COLS), lambda i: (i, 0)),
        out_shape=jax.ShapeDtypeStruct((n, IN_COLS), jnp.bfloat16),
        scratch_shapes=[pltpu.VMEM((tm, D_MODEL), jnp.bfloat16)],
        compiler_params=pltpu.CompilerParams(dimension_semantics=("arbitrary",),
                                             vmem_limit_bytes=VMEM_LIMIT),
        name="in_proj",
    )(x2, pos2, g1.reshape(1, D_MODEL), w_in.astype(jnp.bfloat16), gsum,
      jnp.tile(gq, reps)[None, :], jnp.tile(gk, reps)[None, :], fa, fr, sel)


def _diff_attn_kernel(q_ref, k_ref, v_ref, lam_ref, gsub_ref, o_ref,
                      qs_scr, vx_scr, s0_scr, s1_scr, p_scr, m_scr, alpha_scr, acc_scr):
    i = pl.program_id(2)
    t = q_ref.shape[0]
    heads = range(ATT_HEADS_PER_STEP)
    head_cols = lambda h: slice(h * LANES, (h + 1) * LANES)

    @pl.when(i == 0)
    def _():
        for h in heads:
            vx_scr[h, :, :DA_VDIM] = v_ref[:, head_cols(h)]
            vx_scr[h, :, DA_VDIM:] = jnp.ones((vx_scr.shape[1], LANES), vx_scr.dtype)

    for h in heads:
        q = q_ref[:, head_cols(h)]
        lane = lax.broadcasted_iota(jnp.int32, q.shape, 1)
        zero = jnp.zeros_like(q)
        qs_scr[h, :t] = jnp.where(lane < DA_HALF, q, zero)
        qs_scr[h, t:] = jnp.where(lane >= DA_HALF, q, zero)

    def scores_head(h, j, s_ref):
        start = pl.multiple_of(j * t, t)
        s_ref[h] = _dot_nt(qs_scr[h], k_ref[pl.ds(start, t), head_cols(h)])

    def scores(j, s_ref):
        for h in heads:
            scores_head(h, j, s_ref)

    def softmax_pv_head(h, j, s_ref, masked, first):
        for c in range(2 * t // ATT_ROWS):
            rows = pl.ds(c * ATT_ROWS, ATT_ROWS)
            s = s_ref[h, rows, :]
            if masked:
                r = lax.broadcasted_iota(jnp.int32, s.shape, 0) + (c * ATT_ROWS) % t
                col = lax.broadcasted_iota(jnp.int32, s.shape, 1)
                s = jnp.where(col <= r, s, -jnp.inf)
            m_cur = jnp.max(s, axis=-1, keepdims=True)
            if first:
                m_new = jnp.broadcast_to(m_cur, (ATT_ROWS, LANES))
            else:
                m_prev = m_scr[h, rows, :]
                m_new = jnp.maximum(m_prev, m_cur)
                alpha_scr[h, rows, :] = jnp.exp(m_prev - m_new)
            m_scr[h, rows, :] = m_new
            p = jnp.exp(s - jnp.concatenate([m_new] * (t // LANES), axis=1))
            p_scr[h, rows, :] = p.astype(p_scr.dtype)
        start = pl.multiple_of(j * t, t)
        pv = _dot(p_scr[h], vx_scr[h, pl.ds(start, t), :])
        if first:
            acc_scr[h] = pv
            return
        alpha = alpha_scr[h]
        for half in range(2):
            cols = pl.ds(half * LANES, LANES)
            acc_scr[h, :, cols] = alpha * acc_scr[h, :, cols] + pv[:, half * LANES:(half + 1) * LANES]

    def softmax_pv(j, s_ref, masked, first=False):
        for h in heads:
            softmax_pv_head(h, j, s_ref, masked, first)

    scores(0, s0_scr)

    @pl.when(i == 0)
    def _():
        softmax_pv(0, s0_scr, True, first=True)

    @pl.when(i == 1)
    def _():
        scores(1, s1_scr)
        softmax_pv(0, s0_scr, False, first=True)
        softmax_pv(1, s1_scr, True)

    @pl.when(i >= 2)
    def _():
        scores(1, s1_scr)
        softmax_pv(0, s0_scr, False, first=True)
        scores(2, s0_scr)
        softmax_pv(1, s1_scr, False)

        def pair(jj, carry):
            j = 2 * jj
            for h in heads:
                scores_head(h, j + 1, s1_scr)
                softmax_pv_head(h, j, s0_scr, False, False)
            for h in heads:
                scores_head(h, j + 2, s0_scr)
                softmax_pv_head(h, j + 1, s1_scr, False, False)
            return carry

        lax.fori_loop(1, i // 2, pair, 0)

        @pl.when(i % 2 == 1)
        def _():
            scores(i, s1_scr)
            softmax_pv(i - 1, s0_scr, False)
            softmax_pv(i, s1_scr, True)

        @pl.when(i % 2 == 0)
        def _():
            softmax_pv(i, s0_scr, True)

    lam4 = lam_ref[...]
    lam = (jnp.exp(jnp.sum(lam4[0:1] * lam4[1:2], axis=-1, keepdims=True))
           - jnp.exp(jnp.sum(lam4[2:3] * lam4[3:4], axis=-1, keepdims=True)) + LAMBDA_INIT)
    for h in heads:
        o_all = acc_scr[h, :, :DA_VDIM] / acc_scr[h, :, DA_VDIM:]
        o = o_all[:t] - lam * o_all[t:]
        o = o * lax.rsqrt(jnp.mean(o * o, axis=-1, keepdims=True) + EPS) * gsub_ref[...] * (1.0 - LAMBDA_INIT)
        o_ref[:, head_cols(h)] = o.astype(o_ref.dtype)


def _diff_attn(proj, lam4, gsub, batch, seq):
    n = proj.shape[0]
    t = min(ATT_TILE, seq)
    nq = seq // t
    g = ATT_HEADS_PER_STEP
    w = g * LANES
    qb, kb, vb = COL_QA // w, COL_KA // w, COL_VA // w
    return pl.pallas_call(
        _diff_attn_kernel,
        grid=(batch, DA_HEADS // g, nq),
        in_specs=[
            pl.BlockSpec((t, w), lambda b, h, i: (b * nq + i, qb + h)),
            pl.BlockSpec((seq, w), lambda b, h, i: (b, kb + h)),
            pl.BlockSpec((seq, w), lambda b, h, i: (b, vb + h)),
            pl.BlockSpec((4, LANES), lambda b, h, i: (0, 0)),
            pl.BlockSpec((1, LANES), lambda b, h, i: (0, 0)),
        ],
        out_specs=pl.BlockSpec((t, w), lambda b, h, i: (b * nq + i, h)),
        out_shape=jax.ShapeDtypeStruct((n, DA_WIDTH), jnp.bfloat16),
        scratch_shapes=[pltpu.VMEM((g, 2 * t, LANES), jnp.bfloat16),
                        pltpu.VMEM((g, seq, DA_VDIM + LANES), jnp.bfloat16),
                        pltpu.VMEM((g, 2 * t, t), jnp.float32),
                        pltpu.VMEM((g, 2 * t, t), jnp.float32),
                        pltpu.VMEM((g, 2 * t, t), jnp.bfloat16),
                        pltpu.VMEM((g, 2 * t, LANES), jnp.float32),
                        pltpu.VMEM((g, 2 * t, LANES), jnp.float32),
                        pltpu.VMEM((g, 2 * t, DA_VDIM + LANES), jnp.float32)],
        compiler_params=pltpu.CompilerParams(dimension_semantics=("arbitrary",) * 3,
                                             vmem_limit_bytes=VMEM_LIMIT),
        name="diff_attn",
    )(proj, proj, proj, lam4, gsub)


def _retention_kernel(q_ref, k_ref, v_ref, g_ref, gng_ref, gnb_ref, o_ref, r_scr, *, chunk):
    hf = jnp.full((1, 1), pl.program_id(1), jnp.int32).astype(jnp.float32)
    log_g = jnp.log1p(-jnp.exp2(-5.0 - hf))
    ri = lax.broadcasted_iota(jnp.int32, (chunk, chunk), 0)
    ci = lax.broadcasted_iota(jnp.int32, (chunk, chunk), 1)
    rel = (ri - ci).astype(jnp.float32)
    dmask = jnp.where(rel >= 0, jnp.exp(jnp.maximum(rel, 0.0) * log_g), 0.0)
    idx = lax.broadcasted_iota(jnp.int32, (chunk, 1), 0).astype(jnp.float32)
    zeta = jnp.exp((chunk - 1 - idx) * log_g)
    xi = jnp.exp((idx + 1.0) * log_g)
    g_chunk = jnp.exp(chunk * log_g)
    r_scr[...] = jnp.zeros(r_scr.shape, jnp.float32)
    gng = gng_ref[...]
    gnb = gnb_ref[...]

    def body(n, carry):
        start = pl.multiple_of(n * chunk, chunk)
        q = q_ref[pl.ds(start, chunk), :]
        k = k_ref[pl.ds(start, chunk), :]
        v = v_ref[pl.ds(start, chunk), :]
        s = _dot_nt(q, k) * dmask
        r_old = r_scr[...]
        o = _dot(s.astype(jnp.bfloat16), v) + xi * _dot(q, r_old.astype(jnp.bfloat16))
        kz = (k.astype(jnp.float32) * zeta).astype(jnp.bfloat16)
        r_scr[...] = g_chunk * r_old + _dot_tn(kz, v)
        mu = jnp.mean(o, axis=-1, keepdims=True)
        d = o - mu
        var = jnp.mean(d * d, axis=-1, keepdims=True)
        y = d * lax.rsqrt(var + EPS) * gng + gnb
        y = y * g_ref[pl.ds(start, chunk), :].astype(jnp.float32)
        o_ref[pl.ds(start, chunk), :] = y.astype(o_ref.dtype)
        return carry

    lax.fori_loop(0, q_ref.shape[0] // chunk, body, 0, unroll=RET_UNROLL)


def _retention(proj, gn_g, gn_b, batch, seq):
    n = proj.shape[0]
    chunk = min(RET_CHUNK, seq)
    col = lambda c0: (lambda b, h: (b, c0 // LANES + h))
    return pl.pallas_call(
        functools.partial(_retention_kernel, chunk=chunk),
        grid=(batch, RET_HEADS),
        in_specs=[
            pl.BlockSpec((seq, LANES), col(COL_QR)),
            pl.BlockSpec((seq, LANES), col(COL_KR)),
            pl.BlockSpec((seq, LANES), col(COL_VR)),
            pl.BlockSpec((seq, LANES), col(COL_GB)),
            pl.BlockSpec((1, LANES), lambda b, h: (0, h)),
            pl.BlockSpec((1, LANES), lambda b, h: (0, h)),
        ],
        out_specs=pl.BlockSpec((seq, LANES), lambda b, h: (b, h)),
        out_shape=jax.ShapeDtypeStruct((n, RET_WIDTH), jnp.bfloat16),
        scratch_shapes=[pltpu.VMEM((RET_KDIM, RET_VDIM), jnp.float32)],
        compiler_params=pltpu.CompilerParams(dimension_semantics=("arbitrary",) * 2,
                                             vmem_limit_bytes=VMEM_LIMIT),
        name="retention",
    )(proj, proj, proj, proj, gn_g.reshape(1, RET_WIDTH), gn_b.reshape(1, RET_WIDTH))


def _merge_kernel(x_ref, oa_ref, ob_ref, sa0_ref, sa1_ref, sb0_ref, sb1_ref, wa_ref, wb_ref, wo_ref,
                  g2_ref, wr_hi_ref, wr_lo_ref, br_ref, tri_ref, x1_ref, h2_ref, route_ref, cols_ref, counts_ref,
                  base_scr, logits_scr):
    i = pl.program_id(0)

    @pl.when(i == 0)
    def _():
        base_scr[...] = jnp.zeros(base_scr.shape, jnp.float32)
        logits_scr[...] = jnp.zeros(logits_scr.shape, jnp.float32)

    ya = _dot(oa_ref[...], wa_ref[...])
    yb = _dot(ob_ref[...], wb_ref[...])

    logits = logits_scr[...]
    lane = lax.broadcasted_iota(jnp.int32, logits.shape, 1)
    neg = -jnp.inf
    gl = jnp.where(lane < N_GROUPS, logits, neg)
    gmax = jnp.max(gl, axis=-1, keepdims=True)
    g_idx = jnp.min(jnp.where(gl == gmax, lane, LANES), axis=-1, keepdims=True)
    p_g = 1.0 / jnp.sum(jnp.exp(gl - gmax), axis=-1, keepdims=True)
    e_lo = N_GROUPS + EXPERTS_PER_GROUP * g_idx
    el = jnp.where((lane >= e_lo) & (lane < e_lo + EXPERTS_PER_GROUP), logits, neg)
    v1 = jnp.max(el, axis=-1, keepdims=True)
    i1 = jnp.min(jnp.where(el == v1, lane, LANES), axis=-1, keepdims=True)
    el2 = jnp.where(lane == i1, neg, el)
    v2 = jnp.max(el2, axis=-1, keepdims=True)
    i2 = jnp.min(jnp.where(el2 == v2, lane, LANES), axis=-1, keepdims=True)
    t = jnp.exp(v2 - v1)
    w1 = p_g / (1.0 + t)
    w2 = p_g * t / (1.0 + t)
    e1 = i1 - N_GROUPS
    e2 = i2 - N_GROUPS

    oh1 = lane == e1
    oh2 = lane == e2
    real = jnp.where(i > 0, 1.0, 0.0)
    picked = jnp.where(oh1 | oh2, real, 0.0)
    before = _dot(tri_ref[...], picked.astype(jnp.bfloat16)) + base_scr[0:1, :]
    rank1 = jnp.sum(jnp.where(oh1, before, 0.0), axis=-1, keepdims=True)
    rank2 = jnp.sum(jnp.where(oh2, before, 0.0), axis=-1, keepdims=True)
    base_scr[...] = base_scr[...] + jnp.sum(picked, axis=0, keepdims=True)
    counts_ref[...] = base_scr[...]

    cols = [e1.astype(jnp.float32), e2.astype(jnp.float32), w1, w2, rank1, rank2]
    route = jnp.zeros(logits.shape, jnp.float32)
    for c, val in enumerate(cols):
        route = jnp.where(lane == c, val, route)
    route_ref[...] = route
    cols_ref[...] = route.T[:8]

    sa = jnp.concatenate([sa0_ref[...], sa1_ref[...]], axis=1).astype(jnp.float32)
    sb = jnp.concatenate([sb0_ref[...], sb1_ref[...]], axis=1).astype(jnp.float32)
    merged = sa * ya + sb * yb
    x1 = x_ref[...] + _dot(merged.astype(jnp.bfloat16), wo_ref[...])
    x1_ref[...] = x1
    h2 = x1 * lax.rsqrt(jnp.mean(x1 * x1, axis=-1, keepdims=True) + EPS) * g2_ref[...]
    _store_packed(h2_ref, h2)

    hi = h2.astype(jnp.bfloat16)
    lo = (h2 - hi.astype(jnp.float32)).astype(jnp.bfloat16)
    logits_scr[...] = (_dot(hi, wr_hi_ref[...]) + _dot(lo, wr_hi_ref[...]) + _dot(hi, wr_lo_ref[...])
                       + br_ref[...])


def _merge(x2, oa, ob, proj, wa, wb, wo, g2, w_gr, b_gr, w_er, b_er):
    n = x2.shape[0]
    tm = min(PROJ_ROWS, n)
    half = D_MODEL // 2
    pad = LANES - N_GROUPS - N_EXPERTS
    wr = jnp.concatenate([w_gr, w_er, jnp.zeros((D_MODEL, pad), jnp.float32)], axis=1)
    wr_hi = wr.astype(jnp.bfloat16)
    wr_lo = (wr - wr_hi.astype(jnp.float32)).astype(jnp.bfloat16)
    br = jnp.concatenate([b_gr, b_er, jnp.zeros((pad,), jnp.float32)])[None, :]
    tri = (jnp.arange(tm)[:, None] > jnp.arange(tm)[None, :]).astype(jnp.bfloat16)
    full = lambda shape: pl.BlockSpec(shape, lambda i: (0,) * len(shape))
    nt = n // tm
    cur = lambda i: jnp.minimum(i, nt - 1)
    gate = lambda c0: pl.BlockSpec((tm, half), lambda i: (cur(i), c0 // half))
    return pl.pallas_call(
        _merge_kernel,
        grid=(nt + 1,),
        in_specs=[
            pl.BlockSpec((tm, D_MODEL), lambda i: (cur(i), 0)),
            pl.BlockSpec((tm, DA_WIDTH), lambda i: (cur(i), 0)),
            pl.BlockSpec((tm, RET_WIDTH), lambda i: (cur(i), 0)),
            gate(COL_GATE_A), gate(COL_GATE_A + half), gate(COL_GATE_B), gate(COL_GATE_B + half),
            full((DA_WIDTH, D_MODEL)), full((RET_WIDTH, D_MODEL)), full((D_MODEL, D_MODEL)),
            full((1, D_MODEL)), full((D_MODEL, LANES)), full((D_MODEL, LANES)), full((1, LANES)),
            full((tm, tm)),
        ],
        out_specs=[
            pl.BlockSpec((tm, D_MODEL), lambda i: (cur(i), 0)),
            pl.BlockSpec((ROW_PIECES, tm, PIECE), lambda i: (0, cur(i), 0)),
            pl.BlockSpec((tm, LANES), lambda i: (jnp.maximum(i - 1, 0), 0)),
            pl.BlockSpec((8, tm), lambda i: (0, jnp.maximum(i - 1, 0))),
            pl.BlockSpec((8, LANES), lambda i: (0, 0)),
        ],
        out_shape=[
            jax.ShapeDtypeStruct((n, D_MODEL), jnp.float32),
            jax.ShapeDtypeStruct((ROW_PIECES, n, PIECE), jnp.uint32),
            jax.ShapeDtypeStruct((n, LANES), jnp.float32),
            jax.ShapeDtypeStruct((8, n), jnp.float32),
            jax.ShapeDtypeStruct((8, LANES), jnp.float32),
        ],
        scratch_shapes=[pltpu.VMEM((8, LANES), jnp.float32), pltpu.VMEM((tm, LANES), jnp.float32)],
        compiler_params=pltpu.CompilerParams(dimension_semantics=("arbitrary",),
                                             vmem_limit_bytes=VMEM_LIMIT),
        name="merge",
    )(x2, oa, ob, proj, proj, proj, proj, wa.astype(jnp.bfloat16), wb.astype(jnp.bfloat16),
      wo.astype(jnp.bfloat16), g2.reshape(1, D_MODEL), wr_hi, wr_lo, br, tri)


def _sc_mesh():
    return plsc.VectorSubcoreMesh(core_axis_name="c", subcore_axis_name="s")


def _sc_scatter_rows(src, idx, out_rows, src_block):
    steps = idx.shape[1] // SC_WINDOW
    per_core = steps // SC_CORES

    @pl.kernel(out_type=jax.ShapeDtypeStruct((out_rows, PIECE), src.dtype), mesh=_sc_mesh())
    def scatter(src_hbm, idx_hbm, out_hbm):
        def body(src_vmem, idx_vmem):
            pltpu.sync_copy(src_vmem, out_hbm.at[idx_vmem.at[0]])

        pltpu.emit_pipeline(
            body,
            grid=(SC_CORES, per_core),
            in_specs=[pl.BlockSpec((SC_WINDOW, PIECE), lambda c, i: (src_block(c * per_core + i), 0)),
                      pl.BlockSpec((1, SC_WINDOW), lambda c, i: (0, c * per_core + i))],
            out_specs=[],
            core_axis_name=("c", "s"),
            dimension_semantics=(pltpu.PARALLEL, pltpu.PARALLEL),
        )(src_hbm, idx_hbm)

    return scatter(src, idx)


def _sc_gather_rows(table, idx):
    num = idx.shape[1]
    per_core = num // SC_WINDOW // SC_CORES

    @pl.kernel(out_type=jax.ShapeDtypeStruct((num, PIECE), table.dtype), mesh=_sc_mesh())
    def gather(table_hbm, idx_hbm, out_hbm):
        def body(idx_vmem, out_vmem):
            pltpu.sync_copy(table_hbm.at[idx_vmem.at[0]], out_vmem)

        pltpu.emit_pipeline(
            body,
            grid=(SC_CORES, per_core),
            in_specs=[pl.BlockSpec((1, SC_WINDOW), lambda c, i: (0, c * per_core + i))],
            out_specs=[pl.BlockSpec((SC_WINDOW, PIECE), lambda c, i: (c * per_core + i, 0))],
            core_axis_name=("c", "s"),
            dimension_semantics=(pltpu.PARALLEL, pltpu.PARALLEL),
        )(idx_hbm, out_hbm)

    return gather(table, idx)


def _store_packed(ref, val):
    as_bits = lambda v: lax.bitcast_convert_type(v.astype(jnp.bfloat16).astype(jnp.float32), jnp.uint32)
    words = (as_bits(val[:, :PACKED]) >> 16) | (as_bits(val[:, PACKED:]) & jnp.uint32(0xFFFF0000))
    for j in range(ROW_PIECES):
        ref[j, :val.shape[0], :] = words[:, j * PIECE:(j + 1) * PIECE]


def _load_packed(ref, rows=None):
    rows = ref.shape[1] if rows is None else rows
    words = jnp.concatenate([ref[j, :rows, :] for j in range(ROW_PIECES)], axis=1)
    low = lax.bitcast_convert_type(words << 16, jnp.float32)
    high = lax.bitcast_convert_type(words & jnp.uint32(0xFFFF0000), jnp.float32)
    return jnp.concatenate([low, high], axis=1)


def _expert_kernel(blk_e_ref, n_used_ref, nxt_ref, run_ref, rows_ref, x_ref, wg_hbm, wu_hbm, wd_hbm, o_ref,
                   wg_stage, wu_stage, wd_stage, wg_scr, wu_scr, wd_scr, sem):
    i = pl.program_id(0)
    used = i < n_used_ref[0]

    def weight_copies(e, s):
        return (pltpu.make_async_copy(wg_hbm.at[e], wg_stage.at[s], sem.at[s, 0]),
                pltpu.make_async_copy(wu_hbm.at[e], wu_stage.at[s], sem.at[s, 1]),
                pltpu.make_async_copy(wd_hbm.at[e], wd_stage.at[s], sem.at[s, 2]))

    @pl.when(i == 0)
    def _():
        for c in weight_copies(blk_e_ref[0], 0):
            c.start()

    @pl.when(used & ((i == 0) | (blk_e_ref[i] != blk_e_ref[jnp.maximum(i - 1, 0)])))
    def _():
        s = run_ref[i] % 2
        for c in weight_copies(blk_e_ref[i], s):
            c.wait()
        wg_scr[...] = wg_stage[s].astype(jnp.bfloat16)
        wu_scr[...] = wu_stage[s].astype(jnp.bfloat16)
        wd_scr[...] = wd_stage[s].astype(jnp.bfloat16)

        @pl.when(nxt_ref[i] >= 0)
        def _():
            for c in weight_copies(nxt_ref[i], 1 - s):
                c.start()

    def expert_mlp(rows):
        x = _load_packed(x_ref, rows).astype(jnp.bfloat16)
        a = _dot(x, wg_scr[...])
        u = _dot(x, wu_scr[...])
        hmid = (a * _sigmoid(a) * u).astype(jnp.bfloat16)
        _store_packed(o_ref, _dot(hmid, wd_scr[...]))

    half = MOE_BLOCK // 2
    short = rows_ref[i] <= half

    @pl.when(used & jnp.logical_not(short))
    def _():
        expert_mlp(MOE_BLOCK)

    @pl.when(used & short)
    def _():
        expert_mlp(half)
        o_ref[:, half:, :] = jnp.zeros((ROW_PIECES, MOE_BLOCK - half, PIECE), o_ref.dtype)

    @pl.when(jnp.logical_not(used))
    def _():
        o_ref[...] = jnp.zeros(o_ref.shape, o_ref.dtype)


def _experts(xs, blk_expert, blk_rows, n_used, w_gate, w_up, w_down):
    p = xs.shape[1]
    nblk = p // MOE_BLOCK
    idx = jnp.arange(nblk, dtype=jnp.int32)
    starts = (idx < n_used[0]) & ((idx == 0) | (blk_expert != jnp.roll(blk_expert, 1)))
    run = jnp.cumsum(starts.astype(jnp.int32)) - 1
    next_start = lax.cummin(jnp.where(starts, idx, nblk)[::-1])[::-1]
    after = jnp.concatenate([next_start[1:], jnp.full((1,), nblk, jnp.int32)])
    nxt = jnp.where(after < nblk, blk_expert[jnp.minimum(after, nblk - 1)], -1).astype(jnp.int32)
    live = lambda i, be, nu, *_: jnp.minimum(i, nu[0] - 1)
    any_spec = pl.BlockSpec(memory_space=pl.ANY)
    return pl.pallas_call(
        _expert_kernel,
        grid_spec=pltpu.PrefetchScalarGridSpec(
            num_scalar_prefetch=5,
            grid=(nblk,),
            in_specs=[
                pl.BlockSpec((ROW_PIECES, MOE_BLOCK, PIECE), lambda i, *pre: (0, live(i, *pre), 0)),
                any_spec, any_spec, any_spec,
            ],
            out_specs=pl.BlockSpec((ROW_PIECES, MOE_BLOCK, PIECE), lambda i, *pre: (0, i, 0)),
            scratch_shapes=[pltpu.VMEM((2, D_MODEL, EXPERT_FF), jnp.float32),
                            pltpu.VMEM((2, D_MODEL, EXPERT_FF), jnp.float32),
                            pltpu.VMEM((2, EXPERT_FF, D_MODEL), jnp.float32),
                            pltpu.VMEM((D_MODEL, EXPERT_FF), jnp.bfloat16),
                            pltpu.VMEM((D_MODEL, EXPERT_FF), jnp.bfloat16),
                            pltpu.VMEM((EXPERT_FF, D_MODEL), jnp.bfloat16),
                            pltpu.SemaphoreType.DMA((2, 3))],
        ),
        out_shape=jax.ShapeDtypeStruct((ROW_PIECES, p, PIECE), jnp.uint32),
        compiler_params=pltpu.CompilerParams(dimension_semantics=("arbitrary",),
                                             vmem_limit_bytes=VMEM_LIMIT),
        name="experts",
    )(blk_expert, n_used, nxt, run.astype(jnp.int32), blk_rows, xs, w_gate, w_up, w_down)


def _combine_kernel(x1_ref, route_ref, y0_ref, y1_ref, *rest):
    o_ref = rest[-1]
    route = route_ref[...]
    o_ref[...] = x1_ref[...] + route[:, 2:3] * _load_packed(y0_ref) + route[:, 3:4] * _load_packed(y1_ref)


def _combine(x1, yg, route, row0, out_prev):
    n = yg.shape[1] // TOP_K
    tm = min(PROJ_ROWS, n)
    blk0 = row0 // tm
    prev = () if out_prev is None else (out_prev,)
    return pl.pallas_call(
        _combine_kernel,
        grid=(n // tm,),
        in_specs=[
            pl.BlockSpec((tm, D_MODEL), lambda i: (blk0 + i, 0)),
            pl.BlockSpec((tm, LANES), lambda i: (blk0 + i, 0)),
            pl.BlockSpec((ROW_PIECES, tm, PIECE), lambda i: (0, i, 0)),
            pl.BlockSpec((ROW_PIECES, tm, PIECE), lambda i: (0, i + n // tm, 0)),
        ] + [pl.BlockSpec(memory_space=pl.ANY)] * len(prev),
        out_specs=pl.BlockSpec((tm, D_MODEL), lambda i: (blk0 + i, 0)),
        out_shape=jax.ShapeDtypeStruct(x1.shape, jnp.float32),
        input_output_aliases={4: 0} if prev else {},
        compiler_params=pltpu.CompilerParams(dimension_semantics=("arbitrary",),
                                             vmem_limit_bytes=VMEM_LIMIT),
        name="combine",
    )(x1, route, yg, yg, *prev)


def _dispatch_plan(route_cols, counts, n):
    counts = counts[0, :N_EXPERTS].astype(jnp.int32)
    padded = ((counts + MOE_BLOCK - 1) // MOE_BLOCK) * MOE_BLOCK
    seg_end = jnp.cumsum(padded).astype(jnp.int32)
    seg_start = seg_end - padded
    cols = route_cols.astype(jnp.int32)
    e, rank = cols[0:TOP_K], cols[4:4 + TOP_K]
    picked = e[None] == jnp.arange(N_EXPERTS, dtype=jnp.int32)[:, None, None]
    dest = jnp.sum(jnp.where(picked, seg_start[:, None, None], 0), axis=0) + rank
    p = n * TOP_K + N_EXPERTS * MOE_BLOCK
    slot = dest[None] + (jnp.arange(ROW_PIECES, dtype=jnp.int32) * p)[:, None, None]
    blk_start = jnp.arange(p // MOE_BLOCK, dtype=jnp.int32) * MOE_BLOCK
    blk_expert = jnp.sum((seg_end[None, :] <= blk_start[:, None]).astype(jnp.int32), axis=1)
    blk_expert = jnp.minimum(blk_expert, N_EXPERTS - 1)
    of_block = blk_expert[:, None] == jnp.arange(N_EXPERTS, dtype=jnp.int32)[None, :]
    real_end = jnp.sum(jnp.where(of_block, (seg_start + counts)[None, :], 0), axis=1)
    blk_rows = jnp.clip(real_end - blk_start, 0, MOE_BLOCK)
    n_used = (seg_end[-1] // MOE_BLOCK).reshape(1)
    return slot, blk_expert, blk_rows, n_used, p


def _layer(x, positions, norm1_g, w_in, q_norm_g, k_norm_g, lam4, diff_subln_g, ret_gn_g, ret_gn_b,
           w_branch_a, w_branch_b, w_out, norm2_g, w_gr, b_gr, w_er, b_er, w_gate, w_up, w_down):
    batch, seq, _ = x.shape
    n = batch * seq
    x2 = x.reshape(n, D_MODEL)
    proj = _in_proj(x2, positions.reshape(1, n), norm1_g, w_in, q_norm_g, k_norm_g)
    oa = _diff_attn(proj, lam4, diff_subln_g.reshape(1, DA_VDIM), batch, seq)
    ob = _retention(proj, ret_gn_g, ret_gn_b, batch, seq)
    x1, h2, route, route_cols, counts = _merge(x2, oa, ob, proj, w_branch_a, w_branch_b, w_out, norm2_g,
                                               w_gr, b_gr, w_er, b_er)
    slot, blk_expert, blk_rows, n_used, p = _dispatch_plan(route_cols, counts, n)
    win_n = n // SC_WINDOW
    src_block = lambda s: (s // (TOP_K * win_n)) * win_n + s % win_n
    xs = _sc_scatter_rows(h2.reshape(ROW_PIECES * n, PIECE), slot.reshape(1, -1), ROW_PIECES * p, src_block)
    ys = _experts(xs.reshape(ROW_PIECES, p, PIECE), blk_expert, blk_rows, n_used, w_gate, w_up, w_down)
    parts = COMBINE_PARTS if n % (COMBINE_PARTS * PROJ_ROWS) == 0 else 1
    m = n // parts
    out = None
    for t in range(parts):
        yg = _sc_gather_rows(ys.reshape(ROW_PIECES * p, PIECE), slot[:, :, t * m:(t + 1) * m].reshape(1, -1))
        out = _combine(x1, yg.reshape(ROW_PIECES, TOP_K * m, PIECE), route, t * m, out)
    return out.reshape(batch, seq, D_MODEL)


def kernel(x, positions, norm1_g, w_in, q_norm_g, k_norm_g, lambda_q1, lambda_k1, lambda_q2, lambda_k2, diff_subln_g, ret_gn_g, ret_gn_b, w_branch_a, w_branch_b, w_out, norm2_g, w_group_router, b_group_router, w_expert_router, b_expert_router, w_gate, w_up, w_down):
    assert x.shape[-1] == D_MODEL and norm1_g.shape[0] == 1, "single-layer, D_MODEL-wide input expected"
    lam4 = jnp.pad(jnp.stack([lambda_q1[0], lambda_k1[0], lambda_q2[0], lambda_k2[0]]), ((0, 0), (0, LANES - DA_HALF)))
    return _layer(x, positions, norm1_g[0], w_in[0], q_norm_g[0], k_norm_g[0], lam4, diff_subln_g[0],
                  ret_gn_g[0], ret_gn_b[0], w_branch_a[0], w_branch_b[0], w_out[0], norm2_g[0],
                  w_group_router[0], b_group_router[0], w_expert_router[0], b_expert_router[0],
                  w_gate[0], w_up[0], w_down[0])
```

```python
import functools
import math

import jax
import jax.numpy as jnp
from jax import lax
from jax.experimental import pallas as pl
from jax.experimental.pallas import tpu as pltpu
from jax.experimental.pallas import tpu_sc as plsc

D_MODEL = 1024
DA_HEADS = 4
DA_HALF = 64
DA_VDIM = 2 * DA_HALF
DA_WIDTH = DA_HEADS * DA_VDIM
ROPE_THETA = 500000.0
ROPE_DIM = DA_HALF // 4
RET_HEADS = 4
RET_KDIM = 128
RET_VDIM = 128
RET_WIDTH = RET_HEADS * RET_VDIM
RET_THETA = 10000.0
N_GROUPS = 4
EXPERTS_PER_GROUP = 8
N_EXPERTS = N_GROUPS * EXPERTS_PER_GROUP
TOP_K = 2
EXPERT_FF = 512
EPS = 1e-6
LAMBDA_INIT = 0.8 - 0.6 * math.exp(-0.3 * 0)

LANES = 128
IN_COLS = 3 * DA_WIDTH + 4 * RET_WIDTH + 2 * D_MODEL
COL_QA, COL_KA, COL_VA = 0, DA_WIDTH, 2 * DA_WIDTH
COL_QR = 3 * DA_WIDTH
COL_KR = COL_QR + RET_WIDTH
COL_VR = COL_KR + RET_WIDTH
COL_GB = COL_VR + RET_WIDTH
COL_GATE_A = COL_GB + RET_WIDTH
COL_GATE_B = COL_GATE_A + D_MODEL

PROJ_ROWS = 512
IN_PROJ_ROWS = 1024
PROJ_CHUNK = 256
ATT_TILE = 1024
ATT_ROWS = 32
RET_CHUNK = 256
RET_UNROLL = 16
MOE_BLOCK = 512
PACKED = D_MODEL // 2
ROW_PIECES = 2
PIECE = PACKED // ROW_PIECES
SC_CORES = 2
SC_WINDOW = 128
COMBINE_PARTS = 2
VMEM_LIMIT = 56 * 1024 * 1024


def _dot(a, b):
    return jnp.dot(a, b, preferred_element_type=jnp.float32)


def _dot_nt(a, b):
    return lax.dot_general(a, b, (((1,), (1,)), ((), ())), preferred_element_type=jnp.float32)


def _dot_tn(a, b):
    return lax.dot_general(a, b, (((0,), (0,)), ((), ())), preferred_element_type=jnp.float32)


def _sigmoid(x):
    return 0.5 * jnp.tanh(0.5 * x) + 0.5


def _split3(x):
    a = x.astype(jnp.bfloat16)
    r = x - a.astype(jnp.float32)
    b = r.astype(jnp.bfloat16)
    c = (r - b.astype(jnp.float32)).astype(jnp.bfloat16)
    return a, b, c


def _in_proj_kernel(x_ref, pos_ref, g1_ref, w_ref, gsum_ref, gq_ref, gk_ref, fa_ref, fr_ref, sel_ref,
                    o_ref, h_scr):
    x = x_ref[...]
    rows = x.shape[0]
    h_scr[...] = (x * g1_ref[...]).astype(jnp.bfloat16)
    rms_scale = jnp.broadcast_to(lax.rsqrt(jnp.mean(x * x, axis=-1, keepdims=True) + EPS), (rows, PROJ_CHUNK))
    pos = pos_ref[...].astype(jnp.float32)

    lane = lax.broadcasted_iota(jnp.int32, (rows, LANES), 1)
    half_a = ROPE_DIM // 2
    tables = {}

    def da_tables():
        if "da" not in tables:
            ang_a = fa_ref[...] * pos
            pad = jnp.zeros((LANES - 2 * half_a, rows), jnp.float32)
            t_a = jnp.concatenate([jnp.cos(ang_a), jnp.sin(ang_a), pad], axis=0).T
            tab = sum(_dot(part, sel_ref[...]) for part in _split3(t_a))
            c_a = tab[:, :LANES] + jnp.where(lane % DA_HALF < ROPE_DIM, 0.0, 1.0)
            s_lo = tab[:, LANES:2 * LANES]
            s_hi = tab[:, 2 * LANES:]
            tables["da"] = tuple(jnp.concatenate([v, v], axis=1) for v in (c_a, s_lo, s_hi))
        return tables["da"]

    def ret_tables():
        if "ret" not in tables:
            ang_r = fr_ref[...] * pos
            t_r = jnp.concatenate([jnp.cos(ang_r), jnp.sin(ang_r)], axis=0).T
            sw_r = pltpu.roll(t_r, RET_KDIM // 2, axis=1)
            first = lane < RET_KDIM // 2
            c_r = jnp.where(first, t_r, sw_r)
            s_r = jnp.where(first, -sw_r, t_r)
            tables["ret"] = tuple(jnp.concatenate([v, v], axis=1) for v in (c_r, s_r))
        return tables["ret"]

    def qk_norm_rope(y, g, scale):
        c_a2, s_lo2, s_hi2 = da_tables()
        ss = y * y
        hi = ss.astype(jnp.bfloat16)
        lo = (ss - hi.astype(jnp.float32)).astype(jnp.bfloat16)
        gs = _dot(hi, gsum_ref[...]) + _dot(lo, gsum_ref[...])
        n = y * lax.rsqrt(gs * (1.0 / DA_HALF) + EPS) * g
        up = pltpu.roll(n, PROJ_CHUNK - half_a, axis=1)
        dn = pltpu.roll(n, half_a, axis=1)
        r = n * c_a2 + up * s_lo2 + dn * s_hi2
        return r * scale if scale != 1.0 else r

    def ret_rope(y, scale):
        c_r2, s_r2 = ret_tables()
        halves = [pltpu.roll(y[:, i * LANES:(i + 1) * LANES], RET_KDIM // 2, axis=1)
                  for i in range(PROJ_CHUNK // LANES)]
        sw = jnp.concatenate(halves, axis=1)
        r = y * c_r2 + sw * s_r2
        return r * scale if scale != 1.0 else r

    n_chunks = IN_COLS // PROJ_CHUNK
    is_long = lambda c: c * PROJ_CHUNK < COL_VA or COL_QR <= c * PROJ_CHUNK < COL_VR
    long_chunks = [c for c in range(n_chunks) if is_long(c)]
    short_chunks = [c for c in range(n_chunks) if not is_long(c)][::-1]
    order = [short_chunks.pop(0) for _ in range(3)]
    while long_chunks or short_chunks:
        if long_chunks:
            order.append(long_chunks.pop(0))
        if short_chunks:
            order.append(short_chunks.pop(0))
    for c in order:
        c0 = c * PROJ_CHUNK
        y = _dot(h_scr[...], w_ref[:, c0:c0 + PROJ_CHUNK]) * rms_scale
        if c0 < COL_KA:
            y = qk_norm_rope(y, gq_ref[...], DA_HALF ** -0.5)
        elif c0 < COL_VA:
            y = qk_norm_rope(y, gk_ref[...], 1.0)
        elif c0 < COL_QR:
            pass
        elif c0 < COL_KR:
            y = ret_rope(y, 1.0)
        elif c0 < COL_VR:
            y = ret_rope(y, RET_KDIM ** -0.5)
        elif c0 < COL_GB:
            pass
        elif c0 < COL_GATE_A:
            y = y * _sigmoid(y)
        else:
            y = _sigmoid(y)
        o_ref[:, c0:c0 + PROJ_CHUNK] = y.astype(o_ref.dtype)


def _in_proj(x2, pos2, g1, w_in, gq, gk):
    n = x2.shape[0]
    tm = min(IN_PROJ_ROWS, n)
    grp = jnp.arange(PROJ_CHUNK) // DA_HALF
    gsum = (grp[:, None] == grp[None, :]).astype(jnp.bfloat16)
    half_a = ROPE_DIM // 2
    fa = jnp.power(jnp.float32(ROPE_THETA), -2.0 * jnp.arange(half_a, dtype=jnp.float32) / ROPE_DIM)[:, None]
    half_r = RET_KDIM // 2
    fr = jnp.power(jnp.float32(RET_THETA), -2.0 * jnp.arange(half_r, dtype=jnp.float32) / RET_KDIM)[:, None]
    j = jnp.arange(LANES)[:, None]
    l64 = (jnp.arange(LANES) % DA_HALF)[None, :]
    sel_c = (j < half_a) & (l64 < ROPE_DIM) & (l64 % half_a == j)
    sel_lo = (j >= half_a) & (j < ROPE_DIM) & (l64 < half_a) & (l64 == j - half_a)
    sel_hi = (j >= half_a) & (j < ROPE_DIM) & (l64 >= half_a) & (l64 < ROPE_DIM) & (l64 == j)
    sel = jnp.concatenate([sel_c.astype(jnp.float32), -sel_lo.astype(jnp.float32),
                           sel_hi.astype(jnp.float32)], axis=1).astype(jnp.bfloat16)
    reps = PROJ_CHUNK // DA_HALF
    full = lambda shape: pl.BlockSpec(shape, lambda i: (0,) * len(shape))
    return pl.pallas_call(
        _in_proj_kernel,
        grid=(n // tm,),
        in_specs=[
            pl.BlockSpec((tm, D_MODEL), lambda i: (i, 0)),
            pl.BlockSpec((1, tm), lambda i: (0, i)),
            full((1, D_MODEL)),
            pl.BlockSpec((D_MODEL, IN_COLS), lambda i: (0, 0), pipeline_mode=pl.Buffered(1)),
            full((PROJ_CHUNK, PROJ_CHUNK)),
            full((1, PROJ_CHUNK)),
            full((1, PROJ_CHUNK)),
            full((half_a, 1)),
            full((half_r, 1)),
            full((LANES, 3 * LANES)),
        ],
        out_specs=pl.BlockSpec((tm, IN_COLS), lambda i: (i, 0)),
        out_shape=jax.ShapeDtypeStruct((n, IN_COLS), jnp.bfloat16),
        scratch_shapes=[pltpu.VMEM((tm, D_MODEL), jnp.bfloat16)],
        compiler_params=pltpu.CompilerParams(dimension_semantics=("arbitrary",),
                                             vmem_limit_bytes=VMEM_LIMIT),
        name="in_proj",
    )(x2, pos2, g1.reshape(1, D_MODEL), w_in.astype(jnp.bfloat16), gsum,
      jnp.tile(gq, reps)[None, :], jnp.tile(gk, reps)[None, :], fa, fr, sel)


def _diff_attn_kernel(q_ref, k_ref, v_ref, lam_ref, gsub_ref, o_ref,
                      qs_scr, vx_scr, s0_scr, s1_scr, p_scr, m_scr, alpha_scr, acc_scr):
    i = pl.program_id(2)
    t = q_ref.shape[0]

    @pl.when(i == 0)
    def _():
        vx_scr[:, :DA_VDIM] = v_ref[...]
        vx_scr[:, DA_VDIM:] = jnp.ones((vx_scr.shape[0], LANES), vx_scr.dtype)

    q = q_ref[...]
    lane = lax.broadcasted_iota(jnp.int32, q.shape, 1)
    zero = jnp.zeros_like(q)
    qs_scr[:t] = jnp.where(lane < DA_HALF, q, zero)
    qs_scr[t:] = jnp.where(lane >= DA_HALF, q, zero)

    def scores(j, s_ref):
        start = pl.multiple_of(j * t, t)
        s_ref[...] = _dot_nt(qs_scr[...], k_ref[pl.ds(start, t), :])

    def softmax_pv(j, s_ref, masked, first=False):
        for c in range(2 * t // ATT_ROWS):
            rows = pl.ds(c * ATT_ROWS, ATT_ROWS)
            s = s_ref[rows, :]
            if masked:
                r = lax.broadcasted_iota(jnp.int32, s.shape, 0) + (c * ATT_ROWS) % t
                col = lax.broadcasted_iota(jnp.int32, s.shape, 1)
                s = jnp.where(col <= r, s, -jnp.inf)
            m_cur = jnp.max(s, axis=-1, keepdims=True)
            if first:
                m_new = jnp.broadcast_to(m_cur, (ATT_ROWS, LANES))
            else:
                m_prev = m_scr[rows, :]
                m_new = jnp.maximum(m_prev, m_cur)
                alpha_scr[rows, :] = jnp.exp(m_prev - m_new)
            m_scr[rows, :] = m_new
            p = jnp.exp(s - jnp.concatenate([m_new] * (t // LANES), axis=1))
            p_scr[rows, :] = p.astype(p_scr.dtype)
        start = pl.multiple_of(j * t, t)
        pv = _dot(p_scr[...], vx_scr[pl.ds(start, t), :])
        if first:
            acc_scr[...] = pv
            return
        alpha = alpha_scr[...]
        for half in range(2):
            cols = pl.ds(half * LANES, LANES)
            acc_scr[:, cols] = alpha * acc_scr[:, cols] + pv[:, half * LANES:(half + 1) * LANES]

    scores(0, s0_scr)

    @pl.when(i == 0)
    def _():
        softmax_pv(0, s0_scr, True, first=True)

    @pl.when(i == 1)
    def _():
        scores(1, s1_scr)
        softmax_pv(0, s0_scr, False, first=True)
        softmax_pv(1, s1_scr, True)

    @pl.when(i >= 2)
    def _():
        scores(1, s1_scr)
        softmax_pv(0, s0_scr, False, first=True)
        scores(2, s0_scr)
        softmax_pv(1, s1_scr, False)

        def pair(jj, carry):
            j = 2 * jj
            scores(j + 1, s1_scr)
            softmax_pv(j, s0_scr, False)
            scores(j + 2, s0_scr)
            softmax_pv(j + 1, s1_scr, False)
            return carry

        lax.fori_loop(1, i // 2, pair, 0)

        @pl.when(i % 2 == 1)
        def _():
            scores(i, s1_scr)
            softmax_pv(i - 1, s0_scr, False)
            softmax_pv(i, s1_scr, True)

        @pl.when(i % 2 == 0)
        def _():
            softmax_pv(i, s0_scr, True)

    lam4 = lam_ref[...]
    lam = (jnp.exp(jnp.sum(lam4[0:1] * lam4[1:2], axis=-1, keepdims=True))
           - jnp.exp(jnp.sum(lam4[2:3] * lam4[3:4], axis=-1, keepdims=True)) + LAMBDA_INIT)
    o_all = acc_scr[:, :DA_VDIM] / acc_scr[:, DA_VDIM:]
    o = o_all[:t] - lam * o_all[t:]
    o = o * lax.rsqrt(jnp.mean(o * o, axis=-1, keepdims=True) + EPS) * gsub_ref[...] * (1.0 - LAMBDA_INIT)
    o_ref[...] = o.astype(o_ref.dtype)


def _diff_attn(proj, lam4, gsub, batch, seq):
    n = proj.shape[0]
    t = min(ATT_TILE, seq)
    nq = seq // t
    qb, kb, vb = COL_QA // LANES, COL_KA // LANES, COL_VA // LANES
    return pl.pallas_call(
        _diff_attn_kernel,
        grid=(batch, DA_HEADS, nq),
        in_specs=[
            pl.BlockSpec((t, LANES), lambda b, h, i: (b * nq + i, qb + h)),
            pl.BlockSpec((seq, LANES), lambda b, h, i: (b, kb + h)),
            pl.BlockSpec((seq, LANES), lambda b, h, i: (b, vb + h)),
            pl.BlockSpec((4, LANES), lambda b, h, i: (0, 0)),
            pl.BlockSpec((1, LANES), lambda b, h, i: (0, 0)),
        ],
        out_specs=pl.BlockSpec((t, LANES), lambda b, h, i: (b * nq + i, h)),
        out_shape=jax.ShapeDtypeStruct((n, DA_WIDTH), jnp.bfloat16),
        scratch_shapes=[pltpu.VMEM((2 * t, LANES), jnp.bfloat16),
                        pltpu.VMEM((seq, DA_VDIM + LANES), jnp.bfloat16),
                        pltpu.VMEM((2 * t, t), jnp.float32),
                        pltpu.VMEM((2 * t, t), jnp.float32),
                        pltpu.VMEM((2 * t, t), jnp.bfloat16),
                        pltpu.VMEM((2 * t, LANES), jnp.float32),
                        pltpu.VMEM((2 * t, LANES), jnp.float32),
                        pltpu.VMEM((2 * t, DA_VDIM + LANES), jnp.float32)],
        compiler_params=pltpu.CompilerParams(dimension_semantics=("arbitrary",) * 3,
                                             vmem_limit_bytes=VMEM_LIMIT),
        name="diff_attn",
    )(proj, proj, proj, lam4, gsub)


def _retention_kernel(q_ref, k_ref, v_ref, g_ref, gng_ref, gnb_ref, o_ref, r_scr, *, chunk):
    hf = jnp.full((1, 1), pl.program_id(1), jnp.int32).astype(jnp.float32)
    log_g = jnp.log1p(-jnp.exp2(-5.0 - hf))
    ri = lax.broadcasted_iota(jnp.int32, (chunk, chunk), 0)
    ci = lax.broadcasted_iota(jnp.int32, (chunk, chunk), 1)
    rel = (ri - ci).astype(jnp.float32)
    dmask = jnp.where(rel >= 0, jnp.exp(jnp.maximum(rel, 0.0) * log_g), 0.0)
    idx = lax.broadcasted_iota(jnp.int32, (chunk, 1), 0).astype(jnp.float32)
    zeta = jnp.exp((chunk - 1 - idx) * log_g)
    xi = jnp.exp((idx + 1.0) * log_g)
    g_chunk = jnp.exp(chunk * log_g)
    r_scr[...] = jnp.zeros(r_scr.shape, jnp.float32)
    gng = gng_ref[...]
    gnb = gnb_ref[...]

    def body(n, carry):
        start = pl.multiple_of(n * chunk, chunk)
        q = q_ref[pl.ds(start, chunk), :]
        k = k_ref[pl.ds(start, chunk), :]
        v = v_ref[pl.ds(start, chunk), :]
        s = _dot_nt(q, k) * dmask
        r_old = r_scr[...]
        o = _dot(s.astype(jnp.bfloat16), v) + xi * _dot(q, r_old.astype(jnp.bfloat16))
        kz = (k.astype(jnp.float32) * zeta).astype(jnp.bfloat16)
        r_scr[...] = g_chunk * r_old + _dot_tn(kz, v)
        mu = jnp.mean(o, axis=-1, keepdims=True)
        d = o - mu
        var = jnp.mean(d * d, axis=-1, keepdims=True)
        y = d * lax.rsqrt(var + EPS) * gng + gnb
        y = y * g_ref[pl.ds(start, chunk), :].astype(jnp.float32)
        o_ref[pl.ds(start, chunk), :] = y.astype(o_ref.dtype)
        return carry

    lax.fori_loop(0, q_ref.shape[0] // chunk, body, 0, unroll=RET_UNROLL)


def _retention(proj, gn_g, gn_b, batch, seq):
    n = proj.shape[0]
    chunk = min(RET_CHUNK, seq)
    col = lambda c0: (lambda b, h: (b, c0 // LANES + h))
    return pl.pallas_call(
        functools.partial(_retention_kernel, chunk=chunk),
        grid=(batch, RET_HEADS),
        in_specs=[
            pl.BlockSpec((seq, LANES), col(COL_QR)),
            pl.BlockSpec((seq, LANES), col(COL_KR)),
            pl.BlockSpec((seq, LANES), col(COL_VR)),
            pl.BlockSpec((seq, LANES), col(COL_GB)),
            pl.BlockSpec((1, LANES), lambda b, h: (0, h)),
            pl.BlockSpec((1, LANES), lambda b, h: (0, h)),
        ],
        out_specs=pl.BlockSpec((seq, LANES), lambda b, h: (b, h)),
        out_shape=jax.ShapeDtypeStruct((n, RET_WIDTH), jnp.bfloat16),
        scratch_shapes=[pltpu.VMEM((RET_KDIM, RET_VDIM), jnp.float32)],
        compiler_params=pltpu.CompilerParams(dimension_semantics=("arbitrary",) * 2,
                                             vmem_limit_bytes=VMEM_LIMIT),
        name="retention",
    )(proj, proj, proj, proj, gn_g.reshape(1, RET_WIDTH), gn_b.reshape(1, RET_WIDTH))


def _merge_kernel(x_ref, oa_ref, ob_ref, sa0_ref, sa1_ref, sb0_ref, sb1_ref, wa_ref, wb_ref, wo_ref,
                  g2_ref, wr_hi_ref, wr_lo_ref, br_ref, tri_ref, x1_ref, h2_ref, route_ref, cols_ref, counts_ref,
                  base_scr, logits_scr):
    i = pl.program_id(0)

    @pl.when(i == 0)
    def _():
        base_scr[...] = jnp.zeros(base_scr.shape, jnp.float32)
        logits_scr[...] = jnp.zeros(logits_scr.shape, jnp.float32)

    def route_previous_tile():
        logits = logits_scr[...]
        lane = lax.broadcasted_iota(jnp.int32, logits.shape, 1)
        neg = -jnp.inf
        gl = jnp.where(lane < N_GROUPS, logits, neg)
        gmax = jnp.max(gl, axis=-1, keepdims=True)
        g_idx = jnp.min(jnp.where(gl == gmax, lane, LANES), axis=-1, keepdims=True)
        p_g = 1.0 / jnp.sum(jnp.exp(gl - gmax), axis=-1, keepdims=True)
        e_lo = N_GROUPS + EXPERTS_PER_GROUP * g_idx
        el = jnp.where((lane >= e_lo) & (lane < e_lo + EXPERTS_PER_GROUP), logits, neg)
        v1 = jnp.max(el, axis=-1, keepdims=True)
        i1 = jnp.min(jnp.where(el == v1, lane, LANES), axis=-1, keepdims=True)
        el2 = jnp.where(lane == i1, neg, el)
        v2 = jnp.max(el2, axis=-1, keepdims=True)
        i2 = jnp.min(jnp.where(el2 == v2, lane, LANES), axis=-1, keepdims=True)
        t = jnp.exp(v2 - v1)
        w1 = p_g / (1.0 + t)
        w2 = p_g * t / (1.0 + t)
        e1 = i1 - N_GROUPS
        e2 = i2 - N_GROUPS

        oh1 = lane == e1
        oh2 = lane == e2
        real = jnp.where(i > 0, 1.0, 0.0)
        picked = jnp.where(oh1 | oh2, real, 0.0)
        before = _dot(tri_ref[...], picked.astype(jnp.bfloat16)) + base_scr[0:1, :]
        rank1 = jnp.sum(jnp.where(oh1, before, 0.0), axis=-1, keepdims=True)
        rank2 = jnp.sum(jnp.where(oh2, before, 0.0), axis=-1, keepdims=True)
        base_scr[...] = base_scr[...] + jnp.sum(picked, axis=0, keepdims=True)
        counts_ref[...] = base_scr[...]

        cols = [e1.astype(jnp.float32), e2.astype(jnp.float32), w1, w2, rank1, rank2]
        route = jnp.zeros(logits.shape, jnp.float32)
        for c, val in enumerate(cols):
            route = jnp.where(lane == c, val, route)
        route_ref[...] = route
        cols_ref[...] = route.T[:8]

    last = pl.num_programs(0) - 1

    @pl.when(i < last)
    def _():
        ya = _dot(oa_ref[...], wa_ref[...])
        yb = _dot(ob_ref[...], wb_ref[...])
        route_previous_tile()
        sa = jnp.concatenate([sa0_ref[...], sa1_ref[...]], axis=1).astype(jnp.float32)
        sb = jnp.concatenate([sb0_ref[...], sb1_ref[...]], axis=1).astype(jnp.float32)
        merged = sa * ya + sb * yb
        x1 = x_ref[...] + _dot(merged.astype(jnp.bfloat16), wo_ref[...])
        x1_ref[...] = x1
        h2 = x1 * lax.rsqrt(jnp.mean(x1 * x1, axis=-1, keepdims=True) + EPS) * g2_ref[...]
        _store_packed(h2_ref, h2)

        hi = h2.astype(jnp.bfloat16)
        lo = (h2 - hi.astype(jnp.float32)).astype(jnp.bfloat16)
        logits_scr[...] = (_dot(hi, wr_hi_ref[...]) + _dot(lo, wr_hi_ref[...]) + _dot(hi, wr_lo_ref[...])
                           + br_ref[...])

    @pl.when(i == last)
    def _():
        route_previous_tile()


def _merge(x2, oa, ob, proj, wa, wb, wo, g2, w_gr, b_gr, w_er, b_er):
    n = x2.shape[0]
    tm = min(PROJ_ROWS, n)
    half = D_MODEL // 2
    pad = LANES - N_GROUPS - N_EXPERTS
    wr = jnp.concatenate([w_gr, w_er, jnp.zeros((D_MODEL, pad), jnp.float32)], axis=1)
    wr_hi = wr.astype(jnp.bfloat16)
    wr_lo = (wr - wr_hi.astype(jnp.float32)).astype(jnp.bfloat16)
    br = jnp.concatenate([b_gr, b_er, jnp.zeros((pad,), jnp.float32)])[None, :]
    tri = (jnp.arange(tm)[:, None] > jnp.arange(tm)[None, :]).astype(jnp.bfloat16)
    full = lambda shape: pl.BlockSpec(shape, lambda i: (0,) * len(shape))
    nt = n // tm
    cur = lambda i: jnp.minimum(i, nt - 1)
    gate = lambda c0: pl.BlockSpec((tm, half), lambda i: (cur(i), c0 // half))
    return pl.pallas_call(
        _merge_kernel,
        grid=(nt + 1,),
        in_specs=[
            pl.BlockSpec((tm, D_MODEL), lambda i: (cur(i), 0)),
            pl.BlockSpec((tm, DA_WIDTH), lambda i: (cur(i), 0)),
            pl.BlockSpec((tm, RET_WIDTH), lambda i: (cur(i), 0)),
            gate(COL_GATE_A), gate(COL_GATE_A + half), gate(COL_GATE_B), gate(COL_GATE_B + half),
            full((DA_WIDTH, D_MODEL)), full((RET_WIDTH, D_MODEL)), full((D_MODEL, D_MODEL)),
            full((1, D_MODEL)), full((D_MODEL, LANES)), full((D_MODEL, LANES)), full((1, LANES)),
            full((tm, tm)),
        ],
        out_specs=[
            pl.BlockSpec((tm, D_MODEL), lambda i: (cur(i), 0)),
            pl.BlockSpec((ROW_PIECES, tm, PIECE), lambda i: (0, cur(i), 0)),
            pl.BlockSpec((tm, LANES), lambda i: (jnp.maximum(i - 1, 0), 0)),
            pl.BlockSpec((8, tm), lambda i: (0, jnp.maximum(i - 1, 0))),
            pl.BlockSpec((8, LANES), lambda i: (0, 0)),
        ],
        out_shape=[
            jax.ShapeDtypeStruct((n, D_MODEL), jnp.float32),
            jax.ShapeDtypeStruct((ROW_PIECES, n, PIECE), jnp.uint32),
            jax.ShapeDtypeStruct((n, LANES), jnp.float32),
            jax.ShapeDtypeStruct((8, n), jnp.float32),
            jax.ShapeDtypeStruct((8, LANES), jnp.float32),
        ],
        scratch_shapes=[pltpu.VMEM((8, LANES), jnp.float32), pltpu.VMEM((tm, LANES), jnp.float32)],
        compiler_params=pltpu.CompilerParams(dimension_semantics=("arbitrary",),
                                             vmem_limit_bytes=VMEM_LIMIT),
        name="merge",
    )(x2, oa, ob, proj, proj, proj, proj, wa.astype(jnp.bfloat16), wb.astype(jnp.bfloat16),
      wo.astype(jnp.bfloat16), g2.reshape(1, D_MODEL), wr_hi, wr_lo, br, tri)


def _sc_mesh():
    return plsc.VectorSubcoreMesh(core_axis_name="c", subcore_axis_name="s")


def _sc_scatter_rows(src, idx, out_rows, src_block):
    steps = idx.shape[1] // SC_WINDOW
    per_core = steps // SC_CORES

    @pl.kernel(out_type=jax.ShapeDtypeStruct((out_rows, PIECE), src.dtype), mesh=_sc_mesh())
    def scatter(src_hbm, idx_hbm, out_hbm):
        def body(src_vmem, idx_vmem):
            pltpu.sync_copy(src_vmem, out_hbm.at[idx_vmem.at[0]])

        pltpu.emit_pipeline(
            body,
            grid=(SC_CORES, per_core),
            in_specs=[pl.BlockSpec((SC_WINDOW, PIECE), lambda c, i: (src_block(c * per_core + i), 0)),
                      pl.BlockSpec((1, SC_WINDOW), lambda c, i: (0, c * per_core + i))],
            out_specs=[],
            core_axis_name=("c", "s"),
            dimension_semantics=(pltpu.PARALLEL, pltpu.PARALLEL),
        )(src_hbm, idx_hbm)

    return scatter(src, idx)


def _sc_gather_rows(table, idx):
    num = idx.shape[1]
    per_core = num // SC_WINDOW // SC_CORES

    @pl.kernel(out_type=jax.ShapeDtypeStruct((num, PIECE), table.dtype), mesh=_sc_mesh())
    def gather(table_hbm, idx_hbm, out_hbm):
        def body(idx_vmem, out_vmem):
            pltpu.sync_copy(table_hbm.at[idx_vmem.at[0]], out_vmem)

        pltpu.emit_pipeline(
            body,
            grid=(SC_CORES, per_core),
            in_specs=[pl.BlockSpec((1, SC_WINDOW), lambda c, i: (0, c * per_core + i))],
            out_specs=[pl.BlockSpec((SC_WINDOW, PIECE), lambda c, i: (c * per_core + i, 0))],
            core_axis_name=("c", "s"),
            dimension_semantics=(pltpu.PARALLEL, pltpu.PARALLEL),
        )(idx_hbm, out_hbm)

    return gather(table, idx)


def _store_packed(ref, val):
    as_bits = lambda v: lax.bitcast_convert_type(v.astype(jnp.bfloat16).astype(jnp.float32), jnp.uint32)
    words = (as_bits(val[:, :PACKED]) >> 16) | (as_bits(val[:, PACKED:]) & jnp.uint32(0xFFFF0000))
    for j in range(ROW_PIECES):
        ref[j, :val.shape[0], :] = words[:, j * PIECE:(j + 1) * PIECE]


def _load_packed(ref, rows=None):
    rows = ref.shape[1] if rows is None else rows
    words = jnp.concatenate([ref[j, :rows, :] for j in range(ROW_PIECES)], axis=1)
    low = lax.bitcast_convert_type(words << 16, jnp.float32)
    high = lax.bitcast_convert_type(words & jnp.uint32(0xFFFF0000), jnp.float32)
    return jnp.concatenate([low, high], axis=1)


def _expert_kernel(blk_e_ref, n_used_ref, nxt_ref, run_ref, rows_ref, x_ref, wg_hbm, wu_hbm, wd_hbm, o_ref,
                   wg_stage, wu_stage, wd_stage, wg_scr, wu_scr, wd_scr, sem):
    i = pl.program_id(0)
    used = i < n_used_ref[0]

    def weight_copies(e, s):
        return (pltpu.make_async_copy(wg_hbm.at[e], wg_stage.at[s], sem.at[s, 0]),
                pltpu.make_async_copy(wu_hbm.at[e], wu_stage.at[s], sem.at[s, 1]),
                pltpu.make_async_copy(wd_hbm.at[e], wd_stage.at[s], sem.at[s, 2]))

    @pl.when(i == 0)
    def _():
        for c in weight_copies(blk_e_ref[0], 0):
            c.start()

    @pl.when(used & ((i == 0) | (blk_e_ref[i] != blk_e_ref[jnp.maximum(i - 1, 0)])))
    def _():
        s = run_ref[i] % 2
        for c in weight_copies(blk_e_ref[i], s):
            c.wait()
        wg_scr[...] = wg_stage[s].astype(jnp.bfloat16)
        wu_scr[...] = wu_stage[s].astype(jnp.bfloat16)
        wd_scr[...] = wd_stage[s].astype(jnp.bfloat16)

        @pl.when(nxt_ref[i] >= 0)
        def _():
            for c in weight_copies(nxt_ref[i], 1 - s):
                c.start()

    def expert_mlp(rows):
        x = _load_packed(x_ref, rows).astype(jnp.bfloat16)
        a = _dot(x, wg_scr[...])
        u = _dot(x, wu_scr[...])
        hmid = (a * _sigmoid(a) * u).astype(jnp.bfloat16)
        _store_packed(o_ref, _dot(hmid, wd_scr[...]))

    half = MOE_BLOCK // 2
    short = rows_ref[i] <= half

    @pl.when(used & jnp.logical_not(short))
    def _():
        expert_mlp(MOE_BLOCK)

    @pl.when(used & short)
    def _():
        expert_mlp(half)
        o_ref[:, half:, :] = jnp.zeros((ROW_PIECES, MOE_BLOCK - half, PIECE), o_ref.dtype)

    @pl.when(jnp.logical_not(used))
    def _():
        o_ref[...] = jnp.zeros(o_ref.shape, o_ref.dtype)


def _experts(xs, blk_expert, blk_rows, n_used, w_gate, w_up, w_down):
    p = xs.shape[1]
    nblk = p // MOE_BLOCK
    idx = jnp.arange(nblk, dtype=jnp.int32)
    starts = (idx < n_used[0]) & ((idx == 0) | (blk_expert != jnp.roll(blk_expert, 1)))
    run = jnp.cumsum(starts.astype(jnp.int32)) - 1
    next_start = lax.cummin(jnp.where(starts, idx, nblk)[::-1])[::-1]
    after = jnp.concatenate([next_start[1:], jnp.full((1,), nblk, jnp.int32)])
    nxt = jnp.where(after < nblk, blk_expert[jnp.minimum(after, nblk - 1)], -1).astype(jnp.int32)
    live = lambda i, be, nu, *_: jnp.minimum(i, nu[0] - 1)
    any_spec = pl.BlockSpec(memory_space=pl.ANY)
    return pl.pallas_call(
        _expert_kernel,
        grid_spec=pltpu.PrefetchScalarGridSpec(
            num_scalar_prefetch=5,
            grid=(nblk,),
            in_specs=[
                pl.BlockSpec((ROW_PIECES, MOE_BLOCK, PIECE), lambda i, *pre: (0, live(i, *pre), 0)),
                any_spec, any_spec, any_spec,
            ],
            out_specs=pl.BlockSpec((ROW_PIECES, MOE_BLOCK, PIECE), lambda i, *pre: (0, i, 0)),
            scratch_shapes=[pltpu.VMEM((2, D_MODEL, EXPERT_FF), jnp.float32),
                            pltpu.VMEM((2, D_MODEL, EXPERT_FF), jnp.float32),
                            pltpu.VMEM((2, EXPERT_FF, D_MODEL), jnp.float32),
                            pltpu.VMEM((D_MODEL, EXPERT_FF), jnp.bfloat16),
                            pltpu.VMEM((D_MODEL, EXPERT_FF), jnp.bfloat16),
                            pltpu.VMEM((EXPERT_FF, D_MODEL), jnp.bfloat16),
                            pltpu.SemaphoreType.DMA((2, 3))],
        ),
        out_shape=jax.ShapeDtypeStruct((ROW_PIECES, p, PIECE), jnp.uint32),
        compiler_params=pltpu.CompilerParams(dimension_semantics=("arbitrary",),
                                             vmem_limit_bytes=VMEM_LIMIT),
        name="experts",
    )(blk_expert, n_used, nxt, run.astype(jnp.int32), blk_rows, xs, w_gate, w_up, w_down)


def _combine_kernel(x1_ref, route_ref, y0_ref, y1_ref, *rest):
    o_ref = rest[-1]
    route = route_ref[...]
    o_ref[...] = x1_ref[...] + route[:, 2:3] * _load_packed(y0_ref) + route[:, 3:4] * _load_packed(y1_ref)


def _combine(x1, yg, route, row0, out_prev):
    n = yg.shape[1] // TOP_K
    tm = min(PROJ_ROWS, n)
    blk0 = row0 // tm
    prev = () if out_prev is None else (out_prev,)
    return pl.pallas_call(
        _combine_kernel,
        grid=(n // tm,),
        in_specs=[
            pl.BlockSpec((tm, D_MODEL), lambda i: (blk0 + i, 0)),
            pl.BlockSpec((tm, LANES), lambda i: (blk0 + i, 0)),
            pl.BlockSpec((ROW_PIECES, tm, PIECE), lambda i: (0, i, 0)),
            pl.BlockSpec((ROW_PIECES, tm, PIECE), lambda i: (0, i + n // tm, 0)),
        ] + [pl.BlockSpec(memory_space=pl.ANY)] * len(prev),
        out_specs=pl.BlockSpec((tm, D_MODEL), lambda i: (blk0 + i, 0)),
        out_shape=jax.ShapeDtypeStruct(x1.shape, jnp.float32),
        input_output_aliases={4: 0} if prev else {},
        compiler_params=pltpu.CompilerParams(dimension_semantics=("arbitrary",),
                                             vmem_limit_bytes=VMEM_LIMIT),
        name="combine",
    )(x1, route, yg, yg, *prev)


def _dispatch_plan(route_cols, counts, n):
    counts = counts[0, :N_EXPERTS].astype(jnp.int32)
    padded = ((counts + MOE_BLOCK - 1) // MOE_BLOCK) * MOE_BLOCK
    seg_end = jnp.cumsum(padded).astype(jnp.int32)
    seg_start = seg_end - padded
    cols = route_cols.astype(jnp.int32)
    e, rank = cols[0:TOP_K], cols[4:4 + TOP_K]
    picked = e[None] == jnp.arange(N_EXPERTS, dtype=jnp.int32)[:, None, None]
    dest = jnp.sum(jnp.where(picked, seg_start[:, None, None], 0), axis=0) + rank
    p = n * TOP_K + N_EXPERTS * MOE_BLOCK
    slot = dest[None] + (jnp.arange(ROW_PIECES, dtype=jnp.int32) * p)[:, None, None]
    blk_start = jnp.arange(p // MOE_BLOCK, dtype=jnp.int32) * MOE_BLOCK
    blk_expert = jnp.sum((seg_end[None, :] <= blk_start[:, None]).astype(jnp.int32), axis=1)
    blk_expert = jnp.minimum(blk_expert, N_EXPERTS - 1)
    of_block = blk_expert[:, None] == jnp.arange(N_EXPERTS, dtype=jnp.int32)[None, :]
    real_end = jnp.sum(jnp.where(of_block, (seg_start + counts)[None, :], 0), axis=1)
    blk_rows = jnp.clip(real_end - blk_start, 0, MOE_BLOCK)
    n_used = (seg_end[-1] // MOE_BLOCK).reshape(1)
    return slot, blk_expert, blk_rows, n_used, p


def _layer(x, positions, norm1_g, w_in, q_norm_g, k_norm_g, lam4, diff_subln_g, ret_gn_g, ret_gn_b,
           w_branch_a, w_branch_b, w_out, norm2_g, w_gr, b_gr, w_er, b_er, w_gate, w_up, w_down):
    batch, seq, _ = x.shape
    n = batch * seq
    x2 = x.reshape(n, D_MODEL)
    proj = _in_proj(x2, positions.reshape(1, n), norm1_g, w_in, q_norm_g, k_norm_g)
    oa = _diff_attn(proj, lam4, diff_subln_g.reshape(1, DA_VDIM), batch, seq)
    ob = _retention(proj, ret_gn_g, ret_gn_b, batch, seq)
    x1, h2, route, route_cols, counts = _merge(x2, oa, ob, proj, w_branch_a, w_branch_b, w_out, norm2_g,
                                               w_gr, b_gr, w_er, b_er)
    slot, blk_expert, blk_rows, n_used, p = _dispatch_plan(route_cols, counts, n)
    win_n = n // SC_WINDOW
    src_block = lambda s: (s // (TOP_K * win_n)) * win_n + s % win_n
    xs = _sc_scatter_rows(h2.reshape(ROW_PIECES * n, PIECE), slot.reshape(1, -1), ROW_PIECES * p, src_block)
    ys = _experts(xs.reshape(ROW_PIECES, p, PIECE), blk_expert, blk_rows, n_used, w_gate, w_up, w_down)
    parts = COMBINE_PARTS if n % (COMBINE_PARTS * PROJ_ROWS) == 0 else 1
    m = n // parts
    out = None
    for t in range(parts):
        yg = _sc_gather_rows(ys.reshape(ROW_PIECES * p, PIECE), slot[:, :, t * m:(t + 1) * m].reshape(1, -1))
        out = _combine(x1, yg.reshape(ROW_PIECES, TOP_K * m, PIECE), route, t * m, out)
    return out.reshape(batch, seq, D_MODEL)


def kernel(x, positions, norm1_g, w_in, q_norm_g, k_norm_g, lambda_q1, lambda_k1, lambda_q2, lambda_k2, diff_subln_g, ret_gn_g, ret_gn_b, w_branch_a, w_branch_b, w_out, norm2_g, w_group_router, b_group_router, w_expert_router, b_expert_router, w_gate, w_up, w_down):
    assert x.shape[-1] == D_MODEL and norm1_g.shape[0] == 1, "single-layer, D_MODEL-wide input expected"
    lam4 = jnp.pad(jnp.stack([lambda_q1[0], lambda_k1[0], lambda_q2[0], lambda_k2[0]]), ((0, 0), (0, LANES - DA_HALF)))
    return _layer(x, positions, norm1_g[0], w_in[0], q_norm_g[0], k_norm_g[0], lam4, diff_subln_g[0],
                  ret_gn_g[0], ret_gn_b[0], w_branch_a[0], w_branch_b[0], w_out[0], norm2_g[0],
                  w_group_router[0], b_group_router[0], w_expert_router[0], b_expert_router[0],
                  w_gate[0], w_up[0], w_down[0])
```
